```python
import jax
import jax.numpy as jnp
from jax import lax

D_MODEL = 1024
BATCH = 8
SEQ = 4096
DEPTH = 2

D_RNN = D_MODEL
D_POOL = D_MODEL
D_MIX = D_RNN + D_POOL
N_RNN_HEADS = 8
RNN_HEAD_DIM = D_RNN // N_RNN_HEADS
CONV_WIDTH = 4
LRU_C = 8.0
POOL_WINDOWS = (2, 4, 8, 16)
N_POOL_GROUPS = len(POOL_WINDOWS)
POOL_GROUP_DIM = D_POOL // N_POOL_GROUPS
NORM_EPS = 1e-6

kernel_name = "hybrid_rglru_multiscale_pool_parallel_heads"


def rmsnorm(x, g):
    xf = x.astype(jnp.float32)
    y = xf * lax.rsqrt(jnp.mean(xf * xf, axis=-1, keepdims=True) + NORM_EPS)
    return (y * g.astype(jnp.float32)).astype(x.dtype)


def causal_depthwise_conv(x, w, b):
    y = lax.conv_general_dilated(
        x, w[:, None, :].astype(x.dtype), window_strides=(1,),
        padding=[(CONV_WIDTH - 1, 0)],
        dimension_numbers=("NWC", "WIO", "NWC"),
        feature_group_count=x.shape[-1])
    return y + b


def rg_lru(x, w_a, b_a, w_x, b_x, lam):
    B, S, _ = x.shape
    xh = x.reshape(B, S, N_RNN_HEADS, RNN_HEAD_DIM)
    r = jax.nn.sigmoid(jnp.einsum("bshi,hij->bshj", xh, w_a) + b_a).reshape(B, S, D_RNN)
    i = jax.nn.sigmoid(jnp.einsum("bshi,hij->bshj", xh, w_x) + b_x).reshape(B, S, D_RNN)
    log_a = -LRU_C * r.astype(jnp.float32) * jax.nn.softplus(-lam.astype(jnp.float32))
    a = jnp.exp(log_a)
    mult = jnp.sqrt(-jnp.expm1(2.0 * log_a))
    u = mult * (i * x).astype(jnp.float32)

    def step(h, inp):
        a_t, u_t = inp
        h = a_t * h + u_t
        return h, h

    h0 = jnp.zeros((B, D_RNN), jnp.float32)
    _, hs = lax.scan(step, h0, (jnp.swapaxes(a, 0, 1), jnp.swapaxes(u, 0, 1)))
    return jnp.swapaxes(hs, 0, 1).astype(x.dtype)


def multi_scale_pool(x, w, b, scale):
    B, S, _ = x.shape
    xg = x.reshape(B, S, N_POOL_GROUPS, POOL_GROUP_DIM).astype(jnp.float32)
    cs = jnp.cumsum(xg, axis=1)
    t = jnp.arange(S)
    means = []
    for g, win in enumerate(POOL_WINDOWS):
        csg = cs[:, :, g]
        lagged = jnp.pad(csg, ((0, 0), (win, 0), (0, 0)))[:, :S]
        count = jnp.minimum(t + 1, win).astype(jnp.float32)[None, :, None]
        means.append((csg - lagged) / count)
    pooled = (jnp.stack(means, axis=2) - xg).astype(x.dtype)
    y = jnp.einsum("bsgi,gij->bsgj", pooled, w) + b
    return y.reshape(B, S, D_POOL) * scale


def _fwd_setup_inputs(seed: int = 0) -> dict:
    key = jax.random.key(seed)
    ks = jax.random.split(key, 24)
    f32 = jnp.float32
    nrm = lambda k, shape, s: jax.random.normal(k, shape, f32) * s
    L = DEPTH
    x = jax.random.normal(ks[0], (BATCH, SEQ, D_MODEL), f32)
    c = jax.random.normal(ks[1], (BATCH, D_MODEL), f32)
    ada_w = nrm(ks[2], (L, D_MODEL, 3 * D_MODEL), 0.5 * D_MODEL ** -0.5)
    ada_b = nrm(ks[3], (L, 3 * D_MODEL), 0.01)
    pre_norm_g = 1.0 + nrm(ks[4], (L, D_MODEL), 0.05)
    w_in = nrm(ks[5], (L, D_MODEL, 2 * D_MIX), D_MODEL ** -0.5)
    conv_w = nrm(ks[6], (L, CONV_WIDTH, D_RNN), CONV_WIDTH ** -0.5)
    conv_b = nrm(ks[7], (L, D_RNN), 0.01)
    gate_a_w = nrm(ks[8], (L, N_RNN_HEADS, RNN_HEAD_DIM, RNN_HEAD_DIM), RNN_HEAD_DIM ** -0.5)
    gate_a_b = nrm(ks[9], (L, N_RNN_HEADS, RNN_HEAD_DIM), 0.01)
    gate_x_w = nrm(ks[10], (L, N_RNN_HEADS, RNN_HEAD_DIM, RNN_HEAD_DIM), RNN_HEAD_DIM ** -0.5)
    gate_x_b = nrm(ks[11], (L, N_RNN_HEADS, RNN_HEAD_DIM), 0.01)
    a_c = jax.random.uniform(ks[12], (L, D_RNN), f32, 0.9, 0.999)
    a0 = a_c ** (1.0 / LRU_C)
    lru_lambda = jnp.log(a0) - jnp.log1p(-a0)
    pool_w = nrm(ks[13], (L, N_POOL_GROUPS, POOL_GROUP_DIM, POOL_GROUP_DIM), POOL_GROUP_DIM ** -0.5)
    pool_b = nrm(ks[14], (L, N_POOL_GROUPS, POOL_GROUP_DIM), 0.01)
    pool_scale = jax.random.uniform(ks[15], (L, D_POOL), f32, 0.5, 1.5)
    w_out = nrm(ks[16], (L, D_MIX, D_MODEL), D_MIX ** -0.5)
    post_norm_g = 1.0 + nrm(ks[17], (L, D_MODEL), 0.05)
    return {"x": x, "c": c, "ada_w": ada_w, "ada_b": ada_b, "pre_norm_g": pre_norm_g,
            "w_in": w_in, "conv_w": conv_w, "conv_b": conv_b,
            "gate_a_w": gate_a_w, "gate_a_b": gate_a_b, "gate_x_w": gate_x_w, "gate_x_b": gate_x_b,
            "lru_lambda": lru_lambda, "pool_w": pool_w, "pool_b": pool_b, "pool_scale": pool_scale,
            "w_out": w_out, "post_norm_g": post_norm_g}


def _fwd_reference(x, c, ada_w, ada_b, pre_norm_g, w_in, conv_w, conv_b,
              gate_a_w, gate_a_b, gate_x_w, gate_x_b, lru_lambda,
              pool_w, pool_b, pool_scale, w_out, post_norm_g):
    c_act = jax.nn.silu(c)
    for l in range(DEPTH):
        mod = c_act @ ada_w[l] + ada_b[l]
        shift, scale, gate = jnp.split(mod, 3, axis=-1)
        h = rmsnorm(x, pre_norm_g[l]) * (1.0 + scale[:, None, :]) + shift[:, None, :]
        proj = h @ w_in[l]
        x_rnn, g_rnn, x_pool, g_pool = jnp.split(
            proj, [D_RNN, 2 * D_RNN, 2 * D_RNN + D_POOL], axis=-1)
        u = causal_depthwise_conv(x_rnn, conv_w[l], conv_b[l])
        y_rnn = rg_lru(u, gate_a_w[l], gate_a_b[l], gate_x_w[l], gate_x_b[l],
                       lru_lambda[l]) * jax.nn.silu(g_rnn)
        y_pool = multi_scale_pool(x_pool, pool_w[l], pool_b[l], pool_scale[l]) * jax.nn.silu(g_pool)
        y = jnp.concatenate([y_rnn, y_pool], axis=-1) @ w_out[l]
        x = x + gate[:, None, :] * rmsnorm(y, post_norm_g[l])
    return x


import jax as _jax
import jax.numpy as _jnp

TWIN_FORMAT = 'train_step'
FWD_PARAMS = ['x', 'c', 'ada_w', 'ada_b', 'pre_norm_g', 'w_in', 'conv_w', 'conv_b', 'gate_a_w', 'gate_a_b', 'gate_x_w', 'gate_x_b', 'lru_lambda', 'pool_w', 'pool_b', 'pool_scale', 'w_out', 'post_norm_g']
TWIN_WEIGHTS = ['ada_w', 'ada_b', 'pre_norm_g', 'w_in', 'conv_w', 'conv_b', 'gate_a_w', 'gate_a_b', 'gate_x_w', 'gate_x_b', 'lru_lambda', 'pool_w', 'pool_b', 'pool_scale', 'w_out', 'post_norm_g']
TWIN_DIFF_INPUT = 'x'
TWIN_INPUTS = ['x', 'c', 'ada_w', 'ada_b', 'pre_norm_g', 'w_in', 'conv_w', 'conv_b', 'gate_a_w', 'gate_a_b', 'gate_x_w', 'gate_x_b', 'lru_lambda', 'pool_w', 'pool_b', 'pool_scale', 'w_out', 'post_norm_g', 'loss_target', 'm_ada_w', 'm_ada_b', 'm_pre_norm_g', 'm_w_in', 'm_conv_w', 'm_conv_b', 'm_gate_a_w', 'm_gate_a_b', 'm_gate_x_w', 'm_gate_x_b', 'm_lru_lambda', 'm_pool_w', 'm_pool_b', 'm_pool_scale', 'm_w_out', 'm_post_norm_g', 'v_ada_w', 'v_ada_b', 'v_pre_norm_g', 'v_w_in', 'v_conv_w', 'v_conv_b', 'v_gate_a_w', 'v_gate_a_b', 'v_gate_x_w', 'v_gate_x_b', 'v_lru_lambda', 'v_pool_w', 'v_pool_b', 'v_pool_scale', 'v_w_out', 'v_post_norm_g']
TWIN_OUTPUTS = ['loss', 'grad_x', 'grad_ada_w', 'grad_ada_b', 'grad_pre_norm_g', 'grad_w_in', 'grad_conv_w', 'grad_conv_b', 'grad_gate_a_w', 'grad_gate_a_b', 'grad_gate_x_w', 'grad_gate_x_b', 'grad_lru_lambda', 'grad_pool_w', 'grad_pool_b', 'grad_pool_scale', 'grad_w_out', 'grad_post_norm_g', 'delta_ada_w', 'delta_ada_b', 'delta_pre_norm_g', 'delta_w_in', 'delta_conv_w', 'delta_conv_b', 'delta_gate_a_w', 'delta_gate_a_b', 'delta_gate_x_w', 'delta_gate_x_b', 'delta_lru_lambda', 'delta_pool_w', 'delta_pool_b', 'delta_pool_scale', 'delta_w_out', 'delta_post_norm_g', 'new_m_ada_w', 'new_m_ada_b', 'new_m_pre_norm_g', 'new_m_w_in', 'new_m_conv_w', 'new_m_conv_b', 'new_m_gate_a_w', 'new_m_gate_a_b', 'new_m_gate_x_w', 'new_m_gate_x_b', 'new_m_lru_lambda', 'new_m_pool_w', 'new_m_pool_b', 'new_m_pool_scale', 'new_m_w_out', 'new_m_post_norm_g', 'new_v_ada_w', 'new_v_ada_b', 'new_v_pre_norm_g', 'new_v_w_in', 'new_v_conv_w', 'new_v_conv_b', 'new_v_gate_a_w', 'new_v_gate_a_b', 'new_v_gate_x_w', 'new_v_gate_x_b', 'new_v_lru_lambda', 'new_v_pool_w', 'new_v_pool_b', 'new_v_pool_scale', 'new_v_w_out', 'new_v_post_norm_g']
TWIN_LEAF_KINDS = {'loss': 'loss', 'grad_x': 'grad_x', 'grad_ada_w': 'grad_w', 'grad_ada_b': 'grad_w', 'grad_pre_norm_g': 'grad_w', 'grad_w_in': 'grad_w', 'grad_conv_w': 'grad_w', 'grad_conv_b': 'grad_w', 'grad_gate_a_w': 'grad_w', 'grad_gate_a_b': 'grad_w', 'grad_gate_x_w': 'grad_w', 'grad_gate_x_b': 'grad_w', 'grad_lru_lambda': 'grad_w', 'grad_pool_w': 'grad_w', 'grad_pool_b': 'grad_w', 'grad_pool_scale': 'grad_w', 'grad_w_out': 'grad_w', 'grad_post_norm_g': 'grad_w', 'delta_ada_w': 'delta_w', 'delta_ada_b': 'delta_w', 'delta_pre_norm_g': 'delta_w', 'delta_w_in': 'delta_w', 'delta_conv_w': 'delta_w', 'delta_conv_b': 'delta_w', 'delta_gate_a_w': 'delta_w', 'delta_gate_a_b': 'delta_w', 'delta_gate_x_w': 'delta_w', 'delta_gate_x_b': 'delta_w', 'delta_lru_lambda': 'delta_w', 'delta_pool_w': 'delta_w', 'delta_pool_b': 'delta_w', 'delta_pool_scale': 'delta_w', 'delta_w_out': 'delta_w', 'delta_post_norm_g': 'delta_w', 'new_m_ada_w': 'new_m', 'new_m_ada_b': 'new_m', 'new_m_pre_norm_g': 'new_m', 'new_m_w_in': 'new_m', 'new_m_conv_w': 'new_m', 'new_m_conv_b': 'new_m', 'new_m_gate_a_w': 'new_m', 'new_m_gate_a_b': 'new_m', 'new_m_gate_x_w': 'new_m', 'new_m_gate_x_b': 'new_m', 'new_m_lru_lambda': 'new_m', 'new_m_pool_w': 'new_m', 'new_m_pool_b': 'new_m', 'new_m_pool_scale': 'new_m', 'new_m_w_out': 'new_m', 'new_m_post_norm_g': 'new_m', 'new_v_ada_w': 'new_v', 'new_v_ada_b': 'new_v', 'new_v_pre_norm_g': 'new_v', 'new_v_w_in': 'new_v', 'new_v_conv_w': 'new_v', 'new_v_conv_b': 'new_v', 'new_v_gate_a_w': 'new_v', 'new_v_gate_a_b': 'new_v', 'new_v_gate_x_w': 'new_v', 'new_v_gate_x_b': 'new_v', 'new_v_lru_lambda': 'new_v', 'new_v_pool_w': 'new_v', 'new_v_pool_b': 'new_v', 'new_v_pool_scale': 'new_v', 'new_v_w_out': 'new_v', 'new_v_post_norm_g': 'new_v'}


def _forward(args):
    return _fwd_reference(*[args[k] for k in FWD_PARAMS])


def _output_shape():
    out = _jax.eval_shape(lambda: _forward(_fwd_setup_inputs(0)))
    return out.shape, out.dtype

N_MICROBATCH = 1
ADAM_LR = 0.001
ADAM_B1 = 0.9
ADAM_B2 = 0.999
ADAM_EPS = 1e-08
ADAM_WD = 0.01
ADAM_STEP = 10
PER_EXAMPLE_BATCH_AXIS = {'x': 0, 'c': 0, 'loss_target': 0}
SHARED_INPUTS = []
_WEIGHT_DTYPES = {'ada_w': _jnp.float32, 'ada_b': _jnp.float32, 'pre_norm_g': _jnp.float32, 'w_in': _jnp.float32, 'conv_w': _jnp.float32, 'conv_b': _jnp.float32, 'gate_a_w': _jnp.float32, 'gate_a_b': _jnp.float32, 'gate_x_w': _jnp.float32, 'gate_x_b': _jnp.float32, 'lru_lambda': _jnp.float32, 'pool_w': _jnp.float32, 'pool_b': _jnp.float32, 'pool_scale': _jnp.float32, 'w_out': _jnp.float32, 'post_norm_g': _jnp.float32}
MOMENT_SCALE = {'ada_w': 1.905964e+00, 'ada_b': 3.486385e+00, 'pre_norm_g': 1.403629e-01, 'w_in': 1.808005e-01, 'conv_w': 3.610719e-01, 'conv_b': 1.251969e+00, 'gate_a_w': 2.920910e-02, 'gate_a_b': 6.643655e-02, 'gate_x_w': 6.803002e-02, 'gate_x_b': 1.662005e-01, 'lru_lambda': 1.581211e-01, 'pool_w': 5.272712e-02, 'pool_b': 1.178630e-01, 'pool_scale': 5.177825e-02, 'w_out': 3.434316e-01, 'post_norm_g': 3.826512e+00}


def _to_microbatches(a, axis):
    t = _jnp.moveaxis(a, axis, 0)
    t = t.reshape((N_MICROBATCH, t.shape[0] // N_MICROBATCH) + t.shape[1:])
    return _jnp.moveaxis(t, 1, axis + 1)


def setup_inputs(seed: int = 0) -> dict:
    inp = _fwd_setup_inputs(seed)
    key = _jax.random.fold_in(_jax.random.key(seed), 7919)
    shape, _ = _output_shape()
    out = dict(inp)
    out["loss_target"] = _jax.random.normal(_jax.random.fold_in(key, 0), shape, _jnp.float32)
    for i, name in enumerate(TWIN_WEIGHTS):
        w = inp[name].astype(_jnp.float32)
        if MOMENT_SCALE is None:
            s = _jnp.sqrt(_jnp.mean(_jnp.square(w)) + 1e-30)
        else:
            s = MOMENT_SCALE[name]
        km, kv = _jax.random.split(_jax.random.fold_in(key, i + 1))
        out[name] = w
        out["m_" + name] = s * _jax.random.normal(km, w.shape, _jnp.float32)
        out["v_" + name] = (s * s) * _jax.random.uniform(kv, w.shape, _jnp.float32, 0.5, 1.5)
    if N_MICROBATCH > 1:
        for name, axis in PER_EXAMPLE_BATCH_AXIS.items():
            out[name] = _to_microbatches(out[name], axis)
    return {'x': out['x'], 'c': out['c'], 'ada_w': out['ada_w'], 'ada_b': out['ada_b'], 'pre_norm_g': out['pre_norm_g'], 'w_in': out['w_in'], 'conv_w': out['conv_w'], 'conv_b': out['conv_b'], 'gate_a_w': out['gate_a_w'], 'gate_a_b': out['gate_a_b'], 'gate_x_w': out['gate_x_w'], 'gate_x_b': out['gate_x_b'], 'lru_lambda': out['lru_lambda'], 'pool_w': out['pool_w'], 'pool_b': out['pool_b'], 'pool_scale': out['pool_scale'], 'w_out': out['w_out'], 'post_norm_g': out['post_norm_g'], 'loss_target': out['loss_target'], 'm_ada_w': out['m_ada_w'], 'm_ada_b': out['m_ada_b'], 'm_pre_norm_g': out['m_pre_norm_g'], 'm_w_in': out['m_w_in'], 'm_conv_w': out['m_conv_w'], 'm_conv_b': out['m_conv_b'], 'm_gate_a_w': out['m_gate_a_w'], 'm_gate_a_b': out['m_gate_a_b'], 'm_gate_x_w': out['m_gate_x_w'], 'm_gate_x_b': out['m_gate_x_b'], 'm_lru_lambda': out['m_lru_lambda'], 'm_pool_w': out['m_pool_w'], 'm_pool_b': out['m_pool_b'], 'm_pool_scale': out['m_pool_scale'], 'm_w_out': out['m_w_out'], 'm_post_norm_g': out['m_post_norm_g'], 'v_ada_w': out['v_ada_w'], 'v_ada_b': out['v_ada_b'], 'v_pre_norm_g': out['v_pre_norm_g'], 'v_w_in': out['v_w_in'], 'v_conv_w': out['v_conv_w'], 'v_conv_b': out['v_conv_b'], 'v_gate_a_w': out['v_gate_a_w'], 'v_gate_a_b': out['v_gate_a_b'], 'v_gate_x_w': out['v_gate_x_w'], 'v_gate_x_b': out['v_gate_x_b'], 'v_lru_lambda': out['v_lru_lambda'], 'v_pool_w': out['v_pool_w'], 'v_pool_b': out['v_pool_b'], 'v_pool_scale': out['v_pool_scale'], 'v_w_out': out['v_w_out'], 'v_post_norm_g': out['v_post_norm_g']}


def _loss(weights, diff, rest, loss_target):
    with _jax.named_scope("forward"):
        args = {**rest, TWIN_DIFF_INPUT: diff, **{k: w.astype(_WEIGHT_DTYPES[k]) for k, w in weights.items()}}
        y = _forward(args)
    with _jax.named_scope("loss_head"):
        err = _jnp.square(y.astype(_jnp.float32) - loss_target)
        return 0.5 * _jnp.sum(_jnp.mean(err, axis=-1)) if err.ndim else 0.5 * err


def _adamw(w, g, m, v):
    m = ADAM_B1 * m + (1.0 - ADAM_B1) * g
    v = ADAM_B2 * v + (1.0 - ADAM_B2) * _jnp.square(g)
    m_hat = m / (1.0 - ADAM_B1 ** ADAM_STEP)
    v_hat = v / (1.0 - ADAM_B2 ** ADAM_STEP)
    delta = -ADAM_LR * (m_hat / (_jnp.sqrt(v_hat) + ADAM_EPS) + ADAM_WD * w)
    return delta, m, v


def reference(x, c, ada_w, ada_b, pre_norm_g, w_in, conv_w, conv_b, gate_a_w, gate_a_b, gate_x_w, gate_x_b, lru_lambda, pool_w, pool_b, pool_scale, w_out, post_norm_g, loss_target, m_ada_w, m_ada_b, m_pre_norm_g, m_w_in, m_conv_w, m_conv_b, m_gate_a_w, m_gate_a_b, m_gate_x_w, m_gate_x_b, m_lru_lambda, m_pool_w, m_pool_b, m_pool_scale, m_w_out, m_post_norm_g, v_ada_w, v_ada_b, v_pre_norm_g, v_w_in, v_conv_w, v_conv_b, v_gate_a_w, v_gate_a_b, v_gate_x_w, v_gate_x_b, v_lru_lambda, v_pool_w, v_pool_b, v_pool_scale, v_w_out, v_post_norm_g):
    given = dict(x=x, c=c, ada_w=ada_w, ada_b=ada_b, pre_norm_g=pre_norm_g, w_in=w_in, conv_w=conv_w, conv_b=conv_b, gate_a_w=gate_a_w, gate_a_b=gate_a_b, gate_x_w=gate_x_w, gate_x_b=gate_x_b, lru_lambda=lru_lambda, pool_w=pool_w, pool_b=pool_b, pool_scale=pool_scale, w_out=w_out, post_norm_g=post_norm_g, loss_target=loss_target, m_ada_w=m_ada_w, m_ada_b=m_ada_b, m_pre_norm_g=m_pre_norm_g, m_w_in=m_w_in, m_conv_w=m_conv_w, m_conv_b=m_conv_b, m_gate_a_w=m_gate_a_w, m_gate_a_b=m_gate_a_b, m_gate_x_w=m_gate_x_w, m_gate_x_b=m_gate_x_b, m_lru_lambda=m_lru_lambda, m_pool_w=m_pool_w, m_pool_b=m_pool_b, m_pool_scale=m_pool_scale, m_w_out=m_w_out, m_post_norm_g=m_post_norm_g, v_ada_w=v_ada_w, v_ada_b=v_ada_b, v_pre_norm_g=v_pre_norm_g, v_w_in=v_w_in, v_conv_w=v_conv_w, v_conv_b=v_conv_b, v_gate_a_w=v_gate_a_w, v_gate_a_b=v_gate_a_b, v_gate_x_w=v_gate_x_w, v_gate_x_b=v_gate_x_b, v_lru_lambda=v_lru_lambda, v_pool_w=v_pool_w, v_pool_b=v_pool_b, v_pool_scale=v_pool_scale, v_w_out=v_w_out, v_post_norm_g=v_post_norm_g)
    weights = {n: given[n] for n in TWIN_WEIGHTS}
    shared = {n: given[n] for n in SHARED_INPUTS}
    per_example = {n: given[n] for n in ['x', 'c']}
    grad_fn = _jax.value_and_grad(_loss, argnums=(0, 1))

    def one_microbatch(ex, loss_target):
        ex = dict(ex)
        diff = ex.pop(TWIN_DIFF_INPUT)
        return grad_fn(weights, diff, {**shared, **ex}, loss_target)

    if N_MICROBATCH == 1:
        loss, (grad_w, grad_x) = one_microbatch(per_example, given["loss_target"])
    else:
        def body(carry, xs):
            loss_sum, grad_sum = carry
            l_k, (gw_k, gx_k) = one_microbatch(xs[0], xs[1])
            with _jax.named_scope("update"):
                return (loss_sum + l_k, _jax.tree.map(_jnp.add, grad_sum, gw_k)), gx_k

        init = (_jnp.zeros((), _jnp.float32), _jax.tree.map(_jnp.zeros_like, weights))
        (loss, grad_w), grad_x = _jax.lax.scan(body, init, (per_example, given["loss_target"]))
    with _jax.named_scope("update"):
        delta_w, new_m, new_v = {}, {}, {}
        for n in TWIN_WEIGHTS:
            delta_w[n], new_m[n], new_v[n] = _adamw(weights[n], grad_w[n], given["m_" + n], given["v_" + n])
    return (loss, grad_x, *[grad_w[n] for n in TWIN_WEIGHTS], *[delta_w[n] for n in TWIN_WEIGHTS],
            *[new_m[n] for n in TWIN_WEIGHTS], *[new_v[n] for n in TWIN_WEIGHTS])
```

```python
import functools

import jax
import jax.numpy as jnp
from jax import lax
from jax.experimental import pallas as pl
from jax.experimental.pallas import tpu as pltpu

F32 = jnp.float32
BF16 = jnp.bfloat16

NORM_EPS = 1e-6
LRU_C = 8.0
CONV_WIDTH = 4
MAX_POOL_WINDOW = 16
HALO = 16
ADAM_LR = 0.001
ADAM_B1 = 0.9
ADAM_B2 = 0.999
ADAM_EPS = 1e-08
ADAM_WD = 0.01
ADAM_STEP = 10

V7X_VMEM_LIMIT_BYTES = 56 * 1024 * 1024
MATMUL_ROWS = 256
SCAN_ROWS = 512
ELEMENTWISE_ROWS = 512

MESH = pl.DeviceIdType.MESH
ANY = pl.BlockSpec(memory_space=pl.ANY)
VMEM = pl.BlockSpec(memory_space=pltpu.VMEM)

NT_DIMS = (((1,), (1,)), ((), ()))
TN_DIMS = (((0,), (0,)), ((), ()))


def _params(n_grid_axes):
    return pltpu.CompilerParams(dimension_semantics=("arbitrary",) * n_grid_axes,
                                vmem_limit_bytes=V7X_VMEM_LIMIT_BYTES)


def _tile(total, want):
    t = min(want, max(total // 2, HALO))
    assert total % t == 0 and t % HALO == 0, (total, t)
    return t


def _row_tile(rows):
    for t in range(min(rows, ELEMENTWISE_ROWS) // 8 * 8, 0, -8):
        if rows % t == 0:
            return t
    return rows


def _sigmoid(z):
    return 1.0 / (1.0 + jnp.exp(-z))


def _softplus(z):
    return jnp.maximum(z, 0.0) + jnp.log(1.0 + jnp.exp(-jnp.abs(z)))


def _neg_expm1(z):
    return -jnp.tanh(0.5 * z) * (jnp.exp(z) + 1.0)


def _colsum(v):
    return jnp.sum(v, axis=0, keepdims=True)


def _prenorm(xt, vec_ref):
    rs = lax.rsqrt(jnp.mean(xt * xt, axis=-1, keepdims=True) + NORM_EPS)
    xn = xt * rs
    h = xn * vec_ref[3:4, :] * (1.0 + vec_ref[1:2, :]) + vec_ref[0:1, :]
    return h, xn, rs


def _shift_down(v, d, fill):
    t = v.shape[0]
    if d % 8 == 0:
        return jnp.concatenate([jnp.full((d, v.shape[1]), fill, v.dtype), v[:t - d]], axis=0)
    row = lax.broadcasted_iota(jnp.int32, v.shape, 0)
    return jnp.where(row >= d, pltpu.roll(v, d, 0), fill)


def _shift_up(v, d, fill):
    t = v.shape[0]
    if d % 8 == 0:
        return jnp.concatenate([v[d:], jnp.full((d, v.shape[1]), fill, v.dtype)], axis=0)
    row = lax.broadcasted_iota(jnp.int32, v.shape, 0)
    return jnp.where(row < t - d, pltpu.roll(v, t - d, 0), fill)


def _scan_fwd(a, v, h_before):
    d = 1
    while d < a.shape[0]:
        v = v + a * _shift_down(v, d, 0.0)
        a = a * _shift_down(a, d, 1.0)
        d *= 2
    return a * h_before + v


def _scan_rev(b, v):
    d = 1
    while d < b.shape[0]:
        v = v + b * _shift_up(v, d, 0.0)
        b = b * _shift_up(b, d, 0.0)
        d *= 2
    return v


def _inproj_fwd(x, vec, w_all, layer):
    s, d = x.shape
    p = w_all.shape[3]
    ts = _tile(s, MATMUL_ROWS)

    def body(x_ref, vec_ref, w_ref, proj_ref):
        h, _, _ = _prenorm(x_ref[...], vec_ref)
        hb = h.astype(BF16)
        for k in range(4):
            proj_ref[k] = jnp.dot(hb, w_ref[k], preferred_element_type=F32)

    return pl.pallas_call(
        body, name=f"inproj_fwd_l{layer}", grid=(s // ts,),
        in_specs=[pl.BlockSpec((ts, d), lambda i: (i, 0)),
                  pl.BlockSpec((8, d), lambda i: (0, 0)),
                  pl.BlockSpec((4, None, d, p), lambda i: (0, layer, 0, 0))],
        out_specs=pl.BlockSpec((4, ts, p), lambda i: (0, i, 0)),
        out_shape=jax.ShapeDtypeStruct((4, s, p), F32),
        compiler_params=_params(1),
    )(x, vec, w_all)


def _rnn_gates(u, wa_ref, wx_ref, vec_ref):
    ub = u.astype(BF16)
    r = _sigmoid(jnp.dot(ub, wa_ref[0], preferred_element_type=F32) + vec_ref[1:2, :])
    ig = _sigmoid(jnp.dot(ub, wx_ref[0], preferred_element_type=F32) + vec_ref[2:3, :])
    sp = _softplus(-vec_ref[3:4, :])
    log_a = (-LRU_C) * r * sp
    return ub, r, ig, sp, log_a


def _conv(xbuf, cw_ref, vec_ref, ts):
    u = vec_ref[0:1, :] + cw_ref[CONV_WIDTH - 1:CONV_WIDTH, :] * xbuf[pl.ds(HALO, ts), :]
    for k in range(CONV_WIDTH - 1):
        u = u + cw_ref[k:k + 1, :] * xbuf[pl.ds(HALO - (CONV_WIDTH - 1) + k, ts), :]
    return u


def _rnn_fwd(proj, cw, vec, wa, wx, layer):
    _, s, d = proj.shape
    nh, hd, _ = wa.shape
    ts = _tile(s, SCAN_ROWS)

    def body(proj_ref, cw_ref, vec_ref, wa_ref, wx_ref, ycat_ref, hs_ref, xbuf, hlast):
        i = pl.program_id(1)

        @pl.when(i == 0)
        def _():
            xbuf[0:HALO, :] = jnp.zeros((HALO, hd), F32)
            hlast[...] = jnp.zeros_like(hlast)

        xbuf[pl.ds(HALO, ts), :] = proj_ref[0]
        u = _conv(xbuf, cw_ref, vec_ref, ts)
        _, _, ig, _, log_a = _rnn_gates(u, wa_ref, wx_ref, vec_ref)
        a = jnp.exp(log_a)
        mult = jnp.sqrt(_neg_expm1(2.0 * log_a))
        hs = _scan_fwd(a, mult * (ig * u), hlast[0:1, :])
        hs_ref[...] = hs
        hlast[0:1, :] = hs_ref[ts - 1:ts, :]
        g = proj_ref[1]
        ycat_ref[...] = (hs * (g * _sigmoid(g))).astype(BF16)
        xbuf[0:HALO, :] = xbuf[pl.ds(ts, HALO), :]

    return pl.pallas_call(
        body, name=f"rnn_fwd_l{layer}", grid=(nh, s // ts),
        in_specs=[pl.BlockSpec((2, ts, hd), lambda h, i: (0, i, h)),
                  pl.BlockSpec((CONV_WIDTH, hd), lambda h, i: (0, h)),
                  pl.BlockSpec((8, hd), lambda h, i: (0, h)),
                  pl.BlockSpec((1, hd, hd), lambda h, i: (h, 0, 0)),
                  pl.BlockSpec((1, hd, hd), lambda h, i: (h, 0, 0))],
        out_specs=[pl.BlockSpec((ts, hd), lambda h, i: (i, h)),
                   pl.BlockSpec((ts, hd), lambda h, i: (i, h))],
        out_shape=[jax.ShapeDtypeStruct((s, 2 * d), BF16), jax.ShapeDtypeStruct((s, d), F32)],
        scratch_shapes=[pltpu.VMEM((ts + HALO, hd), F32), pltpu.VMEM((8, hd), F32)],
        compiler_params=_params(2),
    )(proj, cw, vec, wa, wx)


def _window_terms(g, i, ts, lanes):
    win = lax.shift_left(jnp.int32(2), g)
    t = i * ts + lax.broadcasted_iota(jnp.int32, (ts, lanes), 0)
    return win, 1.0 / jnp.minimum(t + 1, win).astype(F32)


def _pooled(xbuf, xt, win, inv_cnt, ts):
    acc = xt
    for dlt in range(1, MAX_POOL_WINDOW):
        acc = acc + jnp.where(dlt < win, 1.0, 0.0) * xbuf[pl.ds(HALO - dlt, ts), :]
    return acc * inv_cnt - xt


def _pool_fwd(proj, ycat, pw, vec, layer):
    _, s, d = proj.shape
    ng, gd, _ = pw.shape
    ts = _tile(s, SCAN_ROWS)

    def body(proj_ref, ycat_in, pw_ref, vec_ref, ycat_ref, xbuf):
        del ycat_in
        g, i = pl.program_id(0), pl.program_id(1)

        @pl.when(i == 0)
        def _():
            xbuf[0:HALO, :] = jnp.zeros((HALO, gd), F32)

        xt = proj_ref[0]
        xbuf[pl.ds(HALO, ts), :] = xt
        win, inv_cnt = _window_terms(g, i, ts, gd)
        pooled = _pooled(xbuf, xt, win, inv_cnt, ts).astype(BF16)
        z = jnp.dot(pooled, pw_ref[0], preferred_element_type=F32) + vec_ref[0:1, :]
        gg = proj_ref[1]
        ycat_ref[...] = (z * vec_ref[1:2, :] * (gg * _sigmoid(gg))).astype(BF16)
        xbuf[0:HALO, :] = xbuf[pl.ds(ts, HALO), :]

    return pl.pallas_call(
        body, name=f"pool_fwd_l{layer}", grid=(ng, s // ts),
        in_specs=[pl.BlockSpec((2, ts, gd), lambda g, i: (1, i, g)),
                  ANY,
                  pl.BlockSpec((1, gd, gd), lambda g, i: (g, 0, 0)),
                  pl.BlockSpec((8, gd), lambda g, i: (0, g))],
        out_specs=pl.BlockSpec((ts, gd), lambda g, i: (i, ng + g)),
        out_shape=jax.ShapeDtypeStruct((s, 2 * d), BF16),
        input_output_aliases={1: 0},
        scratch_shapes=[pltpu.VMEM((ts + HALO, gd), F32)],
        compiler_params=_params(2),
    )(proj, ycat, pw, vec)


def _outproj_fwd(ycat, w_all, x, vec, target, layer):
    s, d = x.shape
    nk, kd = w_all.shape[0], w_all.shape[2]
    ts = _tile(s, MATMUL_ROWS)
    last = target is not None

    def body(*refs):
        if last:
            ycat_ref, w_ref, x_ref, vec_ref, tgt_ref, y_ref, xo_ref, sq_ref = refs
        else:
            ycat_ref, w_ref, x_ref, vec_ref, y_ref, xo_ref = refs
        y = jnp.dot(ycat_ref[:, 0:kd], w_ref[0], preferred_element_type=F32)
        for k in range(1, nk):
            y = y + jnp.dot(ycat_ref[:, k * kd:(k + 1) * kd], w_ref[k], preferred_element_type=F32)
        y_ref[...] = y
        rs = lax.rsqrt(jnp.mean(y * y, axis=-1, keepdims=True) + NORM_EPS)
        xo = x_ref[...] + vec_ref[2:3, :] * (y * rs * vec_ref[4:5, :])
        if last:
            err = xo - tgt_ref[...]
            xo_ref[...] = err * (1.0 / d)

            @pl.when(pl.program_id(0) == 0)
            def _():
                sq_ref[...] = jnp.zeros_like(sq_ref)

            sq_ref[...] += jnp.sum(err * err)
        else:
            xo_ref[...] = xo

    row = pl.BlockSpec((ts, d), lambda i: (i, 0))
    in_specs = [pl.BlockSpec((ts, nk * kd), lambda i: (i, 0)),
                pl.BlockSpec((nk, None, kd, d), lambda i: (0, layer, 0, 0)),
                row, pl.BlockSpec((8, d), lambda i: (0, 0))]
    out_specs = [row, row]
    out_shape = [jax.ShapeDtypeStruct((s, d), F32), jax.ShapeDtypeStruct((s, d), F32)]
    args = [ycat, w_all, x, vec]
    if last:
        in_specs.append(row)
        args.append(target)
        out_specs.append(pl.BlockSpec((8, 128), lambda i: (0, 0)))
        out_shape.append(jax.ShapeDtypeStruct((8, 128), F32))
    out = pl.pallas_call(
        body, name=f"outproj_fwd_l{layer}", grid=(s // ts,),
        in_specs=in_specs, out_specs=out_specs, out_shape=out_shape,
        compiler_params=_params(1),
    )(*args)
    return (out[0], out[1], out[2]) if last else (out[0], out[1], None)


def _outproj_bwd(dxo, y, ycat, w_all, vec, layer):
    s, d = dxo.shape
    nk, kd = w_all.shape[0], w_all.shape[2]
    ts = _tile(s, MATMUL_ROWS)
    nt = s // ts

    def body(dxo_ref, y_ref, ycat_ref, w_ref, vec_ref, dycat_ref, dw_ref, dvec_ref, acc):
        i = pl.program_id(0)

        @pl.when(i == 0)
        def _():
            acc[...] = jnp.zeros_like(acc)
            dvec_ref[...] = jnp.zeros_like(dvec_ref)

        yt = y_ref[...]
        rs = lax.rsqrt(jnp.mean(yt * yt, axis=-1, keepdims=True) + NORM_EPS)
        yhat = yt * rs
        gate, gpost = vec_ref[2:3, :], vec_ref[4:5, :]
        dxo_t = dxo_ref[...]
        dyn = dxo_t * gate
        dvec_ref[0:1, :] += _colsum(dxo_t * (yhat * gpost))
        dvec_ref[1:2, :] += _colsum(dyn * yhat)
        t = dyn * gpost
        dy = (rs * (t - yhat * jnp.mean(t * yhat, axis=-1, keepdims=True))).astype(BF16)
        for k in range(nk):
            cols = slice(k * kd, (k + 1) * kd)
            dycat_ref[:, cols] = lax.dot_general(dy, w_ref[k], NT_DIMS, preferred_element_type=F32)
            acc[k] += lax.dot_general(ycat_ref[:, cols], dy, TN_DIMS, preferred_element_type=F32)

        @pl.when(i == nt - 1)
        def _():
            dw_ref[...] = acc[...].astype(BF16)

    row = pl.BlockSpec((ts, d), lambda i: (i, 0))
    wide = pl.BlockSpec((ts, nk * kd), lambda i: (i, 0))
    return pl.pallas_call(
        body, name=f"outproj_bwd_l{layer}", grid=(nt,),
        in_specs=[row, row, wide,
                  pl.BlockSpec((nk, None, kd, d), lambda i: (0, layer, 0, 0)),
                  pl.BlockSpec((8, d), lambda i: (0, 0))],
        out_specs=[wide,
                   pl.BlockSpec((nk, kd, d), lambda i: (0, 0, 0)),
                   pl.BlockSpec((8, d), lambda i: (0, 0))],
        out_shape=[jax.ShapeDtypeStruct((s, nk * kd), F32),
                   jax.ShapeDtypeStruct((nk, kd, d), BF16),
                   jax.ShapeDtypeStruct((8, d), F32)],
        scratch_shapes=[pltpu.VMEM((nk, kd, d), F32)],
        compiler_params=_params(1),
    )(dxo, y, ycat, w_all, vec)


def _halo_index(ts, nt):
    return lambda j: jnp.maximum((nt - 1 - j) * (ts // HALO) - 1, 0)


def _rnn_bwd(proj, hs, dycat, cw, vec, wa, wx, layer):
    _, s, d = proj.shape
    nh, hd, _ = wa.shape
    ts = _tile(s, SCAN_ROWS)
    nt = s // ts
    halo = _halo_index(ts, nt)

    def body(proj_ref, xh_ref, hs_ref, hsh_ref, dy_ref, cw_ref, vec_ref, wa_ref, wx_ref,
             dproj_ref, dwa_ref, dwx_ref, dvec_ref, xbuf, hbuf, dubuf, carry):
        j = pl.program_id(1)
        first_tile = j == nt - 1

        @pl.when(j == 0)
        def _():
            dubuf[pl.ds(ts, HALO), :] = jnp.zeros((HALO, hd), F32)
            carry[...] = jnp.zeros_like(carry)
            dwa_ref[...] = jnp.zeros_like(dwa_ref)
            dwx_ref[...] = jnp.zeros_like(dwx_ref)
            dvec_ref[...] = jnp.zeros_like(dvec_ref)

        xbuf[0:HALO, :] = jnp.where(first_tile, 0.0, xh_ref[0])
        xbuf[pl.ds(HALO, ts), :] = proj_ref[0]
        hbuf[0:HALO, :] = jnp.where(first_tile, 0.0, hsh_ref[...])
        hs = hs_ref[...]
        hbuf[pl.ds(HALO, ts), :] = hs

        u = _conv(xbuf, cw_ref, vec_ref, ts)
        ub, r, ig, sp, log_a = _rnn_gates(u, wa_ref, wx_ref, vec_ref)
        a = jnp.exp(log_a)
        e2 = jnp.exp(2.0 * log_a)
        mult = jnp.sqrt(_neg_expm1(2.0 * log_a))

        g = proj_ref[1]
        sg = _sigmoid(g)
        dyc = dy_ref[...]
        dproj_ref[1] = (dyc * hs * (sg * (1.0 + g * (1.0 - sg)))).astype(BF16)

        row = lax.broadcasted_iota(jnp.int32, (ts, hd), 0)
        dhs = dyc * (g * sg) + jnp.where(row == ts - 1, carry[0:1, :], 0.0)
        dh = _scan_rev(_shift_up(a, 1, 0.0), dhs)
        carry[...] = (a * dh)[0:8, :]

        h_prev = hbuf[pl.ds(HALO - 1, ts), :]
        dlog_a = dh * h_prev * a - dh * (ig * u) * (e2 / mult)
        di = dh * mult * u
        dzr = dlog_a * ((-LRU_C) * sp) * (r * (1.0 - r))
        dzi = di * (ig * (1.0 - ig))
        dvec_ref[3:4, :] += _colsum(dlog_a * r) * (LRU_C * _sigmoid(-vec_ref[3:4, :]))
        dvec_ref[1:2, :] += _colsum(dzr)
        dvec_ref[2:3, :] += _colsum(dzi)
        dzr_b, dzi_b = dzr.astype(BF16), dzi.astype(BF16)
        dwa_ref[0] += lax.dot_general(ub, dzr_b, TN_DIMS, preferred_element_type=F32)
        dwx_ref[0] += lax.dot_general(ub, dzi_b, TN_DIMS, preferred_element_type=F32)
        du = (dh * mult * ig
              + lax.dot_general(dzr_b, wa_ref[0], NT_DIMS, preferred_element_type=F32)
              + lax.dot_general(dzi_b, wx_ref[0], NT_DIMS, preferred_element_type=F32))
        dvec_ref[0:1, :] += _colsum(du)
        for k in range(CONV_WIDTH):
            dvec_ref[4 + k:5 + k, :] += _colsum(du * xbuf[pl.ds(HALO - (CONV_WIDTH - 1) + k, ts), :])

        dubuf[0:ts, :] = du
        dx = cw_ref[CONV_WIDTH - 1:CONV_WIDTH, :] * du
        for k in range(CONV_WIDTH - 1):
            dx = dx + cw_ref[k:k + 1, :] * dubuf[pl.ds(CONV_WIDTH - 1 - k, ts), :]
        dproj_ref[0] = dx.astype(BF16)
        dubuf[pl.ds(ts, HALO), :] = dubuf[0:HALO, :]

    rev = lambda h, j: (nt - 1 - j, h)
    return pl.pallas_call(
        body, name=f"rnn_bwd_l{layer}", grid=(nh, nt),
        in_specs=[pl.BlockSpec((2, ts, hd), lambda h, j: (0, nt - 1 - j, h)),
                  pl.BlockSpec((1, HALO, hd), lambda h, j: (0, halo(j), h)),
                  pl.BlockSpec((ts, hd), rev),
                  pl.BlockSpec((HALO, hd), lambda h, j: (halo(j), h)),
                  pl.BlockSpec((ts, hd), rev),
                  pl.BlockSpec((CONV_WIDTH, hd), lambda h, j: (0, h)),
                  pl.BlockSpec((8, hd), lambda h, j: (0, h)),
                  pl.BlockSpec((1, hd, hd), lambda h, j: (h, 0, 0)),
                  pl.BlockSpec((1, hd, hd), lambda h, j: (h, 0, 0))],
        out_specs=[pl.BlockSpec((2, ts, hd), lambda h, j: (0, nt - 1 - j, h)),
                   pl.BlockSpec((1, hd, hd), lambda h, j: (h, 0, 0)),
                   pl.BlockSpec((1, hd, hd), lambda h, j: (h, 0, 0)),
                   pl.BlockSpec((16, hd), lambda h, j: (0, h))],
        out_shape=[jax.ShapeDtypeStruct((4, s, d), BF16),
                   jax.ShapeDtypeStruct((nh, hd, hd), F32),
                   jax.ShapeDtypeStruct((nh, hd, hd), F32),
                   jax.ShapeDtypeStruct((16, d), F32)],
        scratch_shapes=[pltpu.VMEM((ts + HALO, hd), F32), pltpu.VMEM((ts + HALO, hd), F32),
                        pltpu.VMEM((ts + HALO, hd), F32), pltpu.VMEM((8, hd), F32)],
        compiler_params=_params(2),
    )(proj, proj, hs, hs, dycat, cw, vec, wa, wx)


def _pool_bwd(proj, dycat, dproj, pw, vec, layer):
    _, s, d = proj.shape
    ng, gd, _ = pw.shape
    ts = _tile(s, SCAN_ROWS)
    nt = s // ts
    halo = _halo_index(ts, nt)

    def body(proj_ref, xh_ref, dy_ref, dproj_in, pw_ref, vec_ref, dproj_ref, dpw_ref, dvec_ref, xbuf, qbuf, acc):
        del dproj_in
        g, j = pl.program_id(0), pl.program_id(1)
        i = nt - 1 - j

        @pl.when(j == 0)
        def _():
            qbuf[pl.ds(ts, HALO), :] = jnp.zeros((HALO, gd), F32)
            acc[...] = jnp.zeros_like(acc)
            dvec_ref[...] = jnp.zeros_like(dvec_ref)

        xt = proj_ref[0]
        xbuf[0:HALO, :] = jnp.where(i == 0, 0.0, xh_ref[0])
        xbuf[pl.ds(HALO, ts), :] = xt
        win, inv_cnt = _window_terms(g, i, ts, gd)
        pooled = _pooled(xbuf, xt, win, inv_cnt, ts).astype(BF16)
        z = jnp.dot(pooled, pw_ref[0], preferred_element_type=F32) + vec_ref[0:1, :]
        scale = vec_ref[1:2, :]
        gg = proj_ref[1]
        sg = _sigmoid(gg)
        dyc = dy_ref[...]
        dyp = dyc * (gg * sg)
        dproj_ref[1] = (dyc * (z * scale) * (sg * (1.0 + gg * (1.0 - sg)))).astype(BF16)
        dvec_ref[1:2, :] += _colsum(dyp * z)
        dz = dyp * scale
        dvec_ref[0:1, :] += _colsum(dz)
        dz_b = dz.astype(BF16)
        acc[...] += lax.dot_general(pooled, dz_b, TN_DIMS, preferred_element_type=F32)
        dpooled = lax.dot_general(dz_b, pw_ref[0], NT_DIMS, preferred_element_type=F32)

        qbuf[0:ts, :] = dpooled * inv_cnt
        dx = -dpooled
        for dlt in range(MAX_POOL_WINDOW):
            dx = dx + jnp.where(dlt < win, 1.0, 0.0) * qbuf[pl.ds(dlt, ts), :]
        dproj_ref[0] = dx.astype(BF16)
        qbuf[pl.ds(ts, HALO), :] = qbuf[0:HALO, :]

        @pl.when(j == nt - 1)
        def _():
            dpw_ref[0] = acc[...].astype(BF16)

    return pl.pallas_call(
        body, name=f"pool_bwd_l{layer}", grid=(ng, nt),
        in_specs=[pl.BlockSpec((2, ts, gd), lambda g, j: (1, nt - 1 - j, g)),
                  pl.BlockSpec((1, HALO, gd), lambda g, j: (2, halo(j), g)),
                  pl.BlockSpec((ts, gd), lambda g, j: (nt - 1 - j, ng + g)),
                  ANY,
                  pl.BlockSpec((1, gd, gd), lambda g, j: (g, 0, 0)),
                  pl.BlockSpec((8, gd), lambda g, j: (0, g))],
        out_specs=[pl.BlockSpec((2, ts, gd), lambda g, j: (1, nt - 1 - j, g)),
                   pl.BlockSpec((1, gd, gd), lambda g, j: (g, 0, 0)),
                   pl.BlockSpec((8, gd), lambda g, j: (0, g))],
        out_shape=[jax.ShapeDtypeStruct((4, s, d), BF16),
                   jax.ShapeDtypeStruct((ng, gd, gd), BF16),
                   jax.ShapeDtypeStruct((8, d), F32)],
        input_output_aliases={3: 0},
        scratch_shapes=[pltpu.VMEM((ts + HALO, gd), F32), pltpu.VMEM((ts + HALO, gd), F32),
                        pltpu.VMEM((gd, gd), F32)],
        compiler_params=_params(2),
    )(proj, proj, dycat, dproj, pw, vec)


def _inproj_bwd_x(dproj, w_all, x, dxo, vec, layer):
    s, d = x.shape
    p = w_all.shape[3]
    ts = _tile(s, MATMUL_ROWS)

    def body(dp_ref, w_ref, x_ref, dxo_ref, vec_ref, dx_ref, dvec_ref):
        @pl.when(pl.program_id(0) == 0)
        def _():
            dvec_ref[...] = jnp.zeros_like(dvec_ref)

        dh = lax.dot_general(dp_ref[0], w_ref[0], NT_DIMS, preferred_element_type=F32)
        for k in range(1, 4):
            dh = dh + lax.dot_general(dp_ref[k], w_ref[k], NT_DIMS, preferred_element_type=F32)
        _, xn, rs = _prenorm(x_ref[...], vec_ref)
        gpre, scale1 = vec_ref[3:4, :], 1.0 + vec_ref[1:2, :]
        dvec_ref[0:1, :] += _colsum(dh)
        dvec_ref[1:2, :] += _colsum(dh * (xn * gpre))
        dvec_ref[2:3, :] += _colsum(dh * (xn * scale1))
        t = dh * (gpre * scale1)
        dx_ref[...] = dxo_ref[...] + rs * (t - xn * jnp.mean(t * xn, axis=-1, keepdims=True))

    row = pl.BlockSpec((ts, d), lambda i: (i, 0))
    return pl.pallas_call(
        body, name=f"inproj_bwd_x_l{layer}", grid=(s // ts,),
        in_specs=[pl.BlockSpec((4, ts, p), lambda i: (0, i, 0)),
                  pl.BlockSpec((4, None, d, p), lambda i: (0, layer, 0, 0)),
                  row, row, pl.BlockSpec((8, d), lambda i: (0, 0))],
        out_specs=[row, pl.BlockSpec((8, d), lambda i: (0, 0))],
        out_shape=[jax.ShapeDtypeStruct((s, d), F32), jax.ShapeDtypeStruct((8, d), F32)],
        compiler_params=_params(1),
    )(dproj, w_all, x, dxo, vec)


def _inproj_bwd_w(dproj, x, vec, layer):
    s, d = x.shape
    p = dproj.shape[2]
    ts = _tile(s, MATMUL_ROWS)
    nt = s // ts

    def body(dp_ref, x_ref, vec_ref, dw_ref, acc):
        i = pl.program_id(1)

        @pl.when(i == 0)
        def _():
            acc[...] = jnp.zeros_like(acc)

        h, _, _ = _prenorm(x_ref[...], vec_ref)
        acc[...] += lax.dot_general(h.astype(BF16), dp_ref[0], TN_DIMS, preferred_element_type=F32)

        @pl.when(i == nt - 1)
        def _():
            dw_ref[0] = acc[...].astype(BF16)

    return pl.pallas_call(
        body, name=f"inproj_bwd_w_l{layer}", grid=(4, nt),
        in_specs=[pl.BlockSpec((1, ts, p), lambda k, i: (k, i, 0)),
                  pl.BlockSpec((ts, d), lambda k, i: (i, 0)),
                  pl.BlockSpec((8, d), lambda k, i: (0, 0))],
        out_specs=pl.BlockSpec((1, d, p), lambda k, i: (k, 0, 0)),
        out_shape=jax.ShapeDtypeStruct((4, d, p), BF16),
        scratch_shapes=[pltpu.VMEM((d, p), F32)],
        compiler_params=_params(2),
    )(dproj, x, vec)


def _sum_slots(stacked, name):
    n, rows, cols = stacked.shape
    tr = _row_tile(rows)

    def body(in_ref, out_ref):
        total = in_ref[0].astype(F32)
        for b in range(1, n):
            total = total + in_ref[b].astype(F32)
        out_ref[...] = total

    return pl.pallas_call(
        body, name=name, grid=(rows // tr,),
        in_specs=[pl.BlockSpec((n, tr, cols), lambda i: (0, i, 0))],
        out_specs=pl.BlockSpec((tr, cols), lambda i: (i, 0)),
        out_shape=jax.ShapeDtypeStruct((rows, cols), F32),
        compiler_params=_params(1),
    )(stacked)


def _adamw(w, m, v, grads, name):
    shape = w.shape
    cols = shape[-1]
    rows = w.size // cols
    tr = _row_tile(rows)
    n = len(grads)

    def body(*refs):
        w_ref, m_ref, v_ref = refs[:3]
        g_refs = refs[3:3 + n]
        g_out, d_out, m_out, v_out = refs[3 + n:]
        g = g_refs[0][...]
        for r in g_refs[1:]:
            g = g + r[...]
        m_new = ADAM_B1 * m_ref[...] + (1.0 - ADAM_B1) * g
        v_new = ADAM_B2 * v_ref[...] + (1.0 - ADAM_B2) * (g * g)
        m_hat = m_new / (1.0 - ADAM_B1 ** ADAM_STEP)
        v_hat = v_new / (1.0 - ADAM_B2 ** ADAM_STEP)
        g_out[...] = g
        d_out[...] = (-ADAM_LR) * (m_hat / (jnp.sqrt(v_hat) + ADAM_EPS) + ADAM_WD * w_ref[...])
        m_out[...] = m_new
        v_out[...] = v_new

    blk = pl.BlockSpec((tr, cols), lambda i: (i, 0))
    outs = pl.pallas_call(
        body, name=name, grid=(rows // tr,),
        in_specs=[blk] * (3 + n), out_specs=[blk] * 4,
        out_shape=[jax.ShapeDtypeStruct((rows, cols), F32)] * 4,
        compiler_params=_params(1),
    )(*[a.reshape(rows, cols) for a in (w, m, v, *grads)])
    return tuple(o.reshape(shape) for o in outs)


def _ada_w_grad(c_t, dmod):
    d, nb = c_t.shape
    nl, _, n = dmod.shape

    def body(c_ref, dm_ref, out_ref):
        for layer in range(nl):
            total = c_ref[:, 0:1] * dm_ref[layer, 0:1, :]
            for b in range(1, nb):
                total = total + c_ref[:, b:b + 1] * dm_ref[layer, b:b + 1, :]
            out_ref[layer] = total

    return pl.pallas_call(
        body, name="ada_w_grad", in_specs=[VMEM, VMEM], out_specs=VMEM,
        out_shape=jax.ShapeDtypeStruct((nl, d, n), F32),
        compiler_params=pltpu.CompilerParams(vmem_limit_bytes=V7X_VMEM_LIMIT_BYTES),
    )(c_t, dmod)


def _place():
    x, y, c = lax.axis_index("x"), lax.axis_index("y"), lax.axis_index("c")
    return x, y, c


OTHER_CHIPS = ((1, 0), (0, 1), (1, 1))
OTHER_DEVICES = tuple((fx, fy, fc) for fx in (0, 1) for fy in (0, 1) for fc in (0, 1))[1:]


def _mod_exchange(c_row, ada_w):
    nl, d, n = ada_w.shape

    def body(c_ref, w_ref, cbuf, modbuf, cblk, mres, send_a, recv_a, send_c, recv_c):
        x, y, c = _place()
        me = 4 * x + 2 * y + c
        chip = 2 * x + y
        cv = c_ref[...]
        cblk[...] = jnp.zeros_like(cblk)
        cblk[0:1, :] = cv * _sigmoid(cv)

        def rows_of(dev):
            return cbuf.at[pl.ds(pl.multiple_of(8 * dev, 8), 8), :]

        cbuf[pl.ds(pl.multiple_of(8 * me, 8), 8), :] = cblk[...]
        sends = []
        for j, (fx, fy, fc) in enumerate(OTHER_DEVICES):
            cp = pltpu.make_async_remote_copy(
                src_ref=cblk, dst_ref=rows_of(me), send_sem=send_a.at[j], recv_sem=recv_a.at[j],
                device_id=(x ^ fx, y ^ fy, c ^ fc), device_id_type=MESH)
            cp.start()
            sends.append(cp)
        for j, (fx, fy, fc) in enumerate(OTHER_DEVICES):
            peer = 4 * (x ^ fx) + 2 * (y ^ fy) + (c ^ fc)
            pltpu.make_async_remote_copy(
                src_ref=cblk, dst_ref=rows_of(peer), send_sem=send_a.at[j], recv_sem=recv_a.at[j],
                device_id=(x ^ fx, y ^ fy, c ^ fc), device_id_type=MESH).wait_recv()
        for cp in sends:
            cp.wait_send()

        call = cbuf[...]
        for layer in range(nl):
            mres[:, layer * n:(layer + 1) * n] = jnp.dot(
                call, w_ref[layer], preferred_element_type=F32, precision=lax.Precision.HIGHEST)

        def block_of(dev):
            return mres.at[pl.ds(pl.multiple_of(8 * dev, 8), 8), :]

        modbuf[chip] = mres[pl.ds(pl.multiple_of(8 * me, 8), 8), :]
        sends = []
        for j, (fx, fy) in enumerate(OTHER_CHIPS):
            peer = 4 * (x ^ fx) + 2 * (y ^ fy) + c
            cp = pltpu.make_async_remote_copy(
                src_ref=block_of(peer), dst_ref=modbuf.at[chip], send_sem=send_c.at[j], recv_sem=recv_c.at[j],
                device_id=(x ^ fx, y ^ fy, c), device_id_type=MESH)
            cp.start()
            sends.append(cp)
        for j, (fx, fy) in enumerate(OTHER_CHIPS):
            pltpu.make_async_remote_copy(
                src_ref=block_of(me), dst_ref=modbuf.at[2 * (x ^ fx) + (y ^ fy)],
                send_sem=send_c.at[j], recv_sem=recv_c.at[j],
                device_id=(x ^ fx, y ^ fy, c), device_id_type=MESH).wait_recv()
        for cp in sends:
            cp.wait_send()

    return pl.pallas_call(
        body, name="mod_exchange", in_specs=[VMEM, VMEM], out_specs=[VMEM, VMEM],
        out_shape=[jax.ShapeDtypeStruct((64, d), F32), jax.ShapeDtypeStruct((4, 8, nl * n), F32)],
        scratch_shapes=[pltpu.VMEM((8, d), F32), pltpu.VMEM((64, nl * n), F32),
                        pltpu.SemaphoreType.DMA((7,)), pltpu.SemaphoreType.DMA((7,)),
                        pltpu.SemaphoreType.DMA((3,)), pltpu.SemaphoreType.DMA((3,))],
        compiler_params=pltpu.CompilerParams(vmem_limit_bytes=V7X_VMEM_LIMIT_BYTES, has_side_effects=True),
    )(c_row, ada_w)


def _chip_gather(shards, name):
    n = len(shards)

    def body(*refs):
        srcs, outs = refs[:n], refs[n:2 * n]
        local_sem, send_sems, recv_sems = refs[2 * n:]
        x, y, c = _place()
        chip = 2 * x + y
        started = []
        for i in range(n):
            cp = pltpu.make_async_copy(srcs[i], outs[i].at[chip], local_sem.at[i])
            cp.start()
            started.append(cp)
        sends = []
        for i in range(n):
            for j, (fx, fy) in enumerate(OTHER_CHIPS):
                cp = pltpu.make_async_remote_copy(
                    src_ref=srcs[i], dst_ref=outs[i].at[chip],
                    send_sem=send_sems.at[3 * i + j], recv_sem=recv_sems.at[3 * i + j],
                    device_id=(x ^ fx, y ^ fy, c), device_id_type=MESH)
                cp.start()
                sends.append(cp)
        for i in range(n):
            for j, (fx, fy) in enumerate(OTHER_CHIPS):
                pltpu.make_async_remote_copy(
                    src_ref=srcs[i], dst_ref=outs[i].at[2 * (x ^ fx) + (y ^ fy)],
                    send_sem=send_sems.at[3 * i + j], recv_sem=recv_sems.at[3 * i + j],
                    device_id=(x ^ fx, y ^ fy, c), device_id_type=MESH).wait_recv()
        for cp in sends:
            cp.wait_send()
        for cp in started:
            cp.wait()

    return pl.pallas_call(
        body, name=name, in_specs=[ANY] * n, out_specs=[ANY] * n,
        out_shape=[jax.ShapeDtypeStruct((4, *a.shape), a.dtype) for a in shards],
        scratch_shapes=[pltpu.SemaphoreType.DMA((n,)), pltpu.SemaphoreType.DMA((3 * n,)),
                        pltpu.SemaphoreType.DMA((3 * n,))],
        compiler_params=pltpu.CompilerParams(has_side_effects=True),
    )(*shards)


def _grad_exchange(dwin, dwout, dpw, slab):
    nl = len(dwin)
    ng, gd, _ = dpw[0].shape
    q = gd // 4

    def body(*refs):
        dwin_r, dwout_r, dpw_r = refs[:nl], refs[nl:2 * nl], refs[2 * nl:3 * nl]
        slab_r, rwin, rwout, rpw, slabs, local_sem, send_sems, recv_sems = refs[3 * nl:]
        x, y, c = _place()
        me = 4 * x + 2 * y + c
        chip = 2 * x + y

        def pieces(owner):
            out = []
            for layer in range(nl):
                out.append((dwin_r[layer].at[owner], rwin.at[chip, layer]))
                out.append((dwout_r[layer].at[owner], rwout.at[chip, layer]))
                out.append((dpw_r[layer].at[:, pl.ds(pl.multiple_of(owner * q, q), q), :], rpw.at[chip, layer]))
            return out

        def landing(src_chip):
            out = []
            for layer in range(nl):
                out += [rwin.at[src_chip, layer], rwout.at[src_chip, layer], rpw.at[src_chip, layer]]
            return out

        npc = 3 * nl
        local = [pltpu.make_async_copy(s, d_, local_sem.at[i]) for i, (s, d_) in enumerate(pieces(chip))]
        local.append(pltpu.make_async_copy(slab_r, slabs.at[me], local_sem.at[npc]))
        for cp in local:
            cp.start()
        sends = []
        for j, (fx, fy) in enumerate(OTHER_CHIPS):
            owner = 2 * (x ^ fx) + (y ^ fy)
            for i, (s, d_) in enumerate(pieces(owner)):
                cp = pltpu.make_async_remote_copy(
                    src_ref=s, dst_ref=d_, send_sem=send_sems.at[npc * j + i], recv_sem=recv_sems.at[npc * j + i],
                    device_id=(x ^ fx, y ^ fy, c), device_id_type=MESH)
                cp.start()
                sends.append(cp)
        base = 3 * npc
        for j, (fx, fy, fc) in enumerate(OTHER_DEVICES):
            cp = pltpu.make_async_remote_copy(
                src_ref=slab_r, dst_ref=slabs.at[me], send_sem=send_sems.at[base + j], recv_sem=recv_sems.at[base + j],
                device_id=(x ^ fx, y ^ fy, c ^ fc), device_id_type=MESH)
            cp.start()
            sends.append(cp)
        for j, (fx, fy) in enumerate(OTHER_CHIPS):
            src_chip = 2 * (x ^ fx) + (y ^ fy)
            for i, d_ in enumerate(landing(src_chip)):
                pltpu.make_async_remote_copy(
                    src_ref=d_, dst_ref=d_, send_sem=send_sems.at[npc * j + i], recv_sem=recv_sems.at[npc * j + i],
                    device_id=(x ^ fx, y ^ fy, c), device_id_type=MESH).wait_recv()
        for j, (fx, fy, fc) in enumerate(OTHER_DEVICES):
            peer = 4 * (x ^ fx) + 2 * (y ^ fy) + (c ^ fc)
            pltpu.make_async_remote_copy(
                src_ref=slab_r, dst_ref=slabs.at[peer], send_sem=send_sems.at[base + j], recv_sem=recv_sems.at[base + j],
                device_id=(x ^ fx, y ^ fy, c ^ fc), device_id_type=MESH).wait_recv()
        for cp in sends:
            cp.wait_send()
        for cp in local:
            cp.wait()

    n_sem = 3 * 3 * nl + 7
    return pl.pallas_call(
        body, name="grad_exchange", in_specs=[ANY] * (3 * nl + 1), out_specs=[ANY] * 4,
        out_shape=[jax.ShapeDtypeStruct((4, nl, *dwin[0].shape[1:]), dwin[0].dtype),
                   jax.ShapeDtypeStruct((4, nl, *dwout[0].shape[1:]), dwout[0].dtype),
                   jax.ShapeDtypeStruct((4, nl, ng, q, gd), dpw[0].dtype),
                   jax.ShapeDtypeStruct((8, *slab.shape), slab.dtype)],
        scratch_shapes=[pltpu.SemaphoreType.DMA((3 * nl + 1,)), pltpu.SemaphoreType.DMA((n_sem,)),
                        pltpu.SemaphoreType.DMA((n_sem,))],
        compiler_params=pltpu.CompilerParams(has_side_effects=True),
    )(*dwin, *dwout, *dpw, slab)


def _sibling_swap(parts):
    n = len(parts)

    def body(*refs):
        srcs, outs = refs[:n], refs[n:2 * n]
        send_sems, recv_sems = refs[2 * n:]
        x, y, c = _place()
        cps = [pltpu.make_async_remote_copy(
            src_ref=srcs[i], dst_ref=outs[i], send_sem=send_sems.at[i], recv_sem=recv_sems.at[i],
            device_id=(x, y, 1 - c), device_id_type=MESH) for i in range(n)]
        for cp in cps:
            cp.start()
        for cp in cps:
            cp.wait()

    return pl.pallas_call(
        body, name="sibling_swap", in_specs=[ANY] * n, out_specs=[ANY] * n,
        out_shape=[jax.ShapeDtypeStruct(a.shape, a.dtype) for a in parts],
        scratch_shapes=[pltpu.SemaphoreType.DMA((n,)), pltpu.SemaphoreType.DMA((n,))],
        compiler_params=pltpu.CompilerParams(has_side_effects=True),
    )(*parts)


def _rows8(*rows):
    d = rows[0].shape[-1]
    out = jnp.zeros((8, d), F32)
    for i, r in enumerate(rows):
        out = out.at[i].set(r.reshape(d))
    return out


def kernel(x, c, ada_w, ada_b, pre_norm_g, w_in, conv_w, conv_b, gate_a_w, gate_a_b, gate_x_w, gate_x_b, lru_lambda, pool_w, pool_b, pool_scale, w_out, post_norm_g, loss_target, m_ada_w, m_ada_b, m_pre_norm_g, m_w_in, m_conv_w, m_conv_b, m_gate_a_w, m_gate_a_b, m_gate_x_w, m_gate_x_b, m_lru_lambda, m_pool_w, m_pool_b, m_pool_scale, m_w_out, m_post_norm_g, v_ada_w, v_ada_b, v_pre_norm_g, v_w_in, v_conv_w, v_conv_b, v_gate_a_w, v_gate_a_b, v_gate_x_w, v_gate_x_b, v_lru_lambda, v_pool_w, v_pool_b, v_pool_scale, v_w_out, v_post_norm_g):
    nl, d, n_ada = ada_w.shape
    s = x.shape[1]
    nh, hd = gate_a_w.shape[1], gate_a_w.shape[2]
    ng, gq, gd = pool_w.shape[1], pool_w.shape[2], pool_w.shape[3]
    chip = 2 * lax.axis_index("x") + lax.axis_index("y")
    x0 = x.reshape(s, d)
    target = loss_target.reshape(s, d)

    cbuf, modbuf = _mod_exchange(c.reshape(1, d), ada_w)
    mod = modbuf[:, 0, :].reshape(4, nl, n_ada).transpose(1, 0, 2).reshape(nl, 4 * n_ada) + ada_b
    win_g, wout_g, poolw_g, convw_g, poolb_g = _chip_gather(
        [w_in.astype(BF16), w_out.astype(BF16), pool_w.astype(BF16), conv_w, pool_b], "weight_gather")
    poolw_full = poolw_g.transpose(1, 2, 0, 3, 4).reshape(nl, ng, gd, gd)
    convw_full = convw_g.transpose(1, 2, 0, 3).reshape(nl, CONV_WIDTH, d)
    poolb_full = poolb_g.transpose(1, 2, 0, 3).reshape(nl, d)
    wa_b, wx_b = gate_a_w.astype(BF16), gate_x_w.astype(BF16)

    vecs, rvecs, pvecs = [], [], []
    for l in range(nl):
        vecs.append(_rows8(mod[l, 0:d], mod[l, d:2 * d], mod[l, 2 * d:3 * d], pre_norm_g[l], post_norm_g[l]))
        rvecs.append(_rows8(conv_b[l], gate_a_b[l], gate_x_b[l], lru_lambda[l]))
        pvecs.append(_rows8(poolb_full[l], pool_scale[l]))

    xs, projs, hss, ycats, ys = [x0], [], [], [], []
    sq = None
    for l in range(nl):
        proj = _inproj_fwd(xs[l], vecs[l], win_g, l)
        ycat, hs = _rnn_fwd(proj, convw_full[l], rvecs[l], wa_b[l], wx_b[l], l)
        ycat = _pool_fwd(proj, ycat, poolw_full[l], pvecs[l], l)
        y, xo, sq = _outproj_fwd(ycat, wout_g, xs[l], vecs[l], target if l == nl - 1 else None, l)
        projs.append(proj), hss.append(hs), ycats.append(ycat), ys.append(y), xs.append(xo)
    loss = lax.psum(sq[0, 0] * (0.5 / d), ("x", "y", "c"))

    dx = xs[nl]
    dwin, dwout, dpw, slab_parts = [None] * nl, [None] * nl, [None] * nl, [None] * nl
    for l in reversed(range(nl)):
        dycat, dwout[l], dvec_o = _outproj_bwd(dx, ys[l], ycats[l], wout_g, vecs[l], l)
        dproj, dwa, dwx, dvec_r = _rnn_bwd(projs[l], hss[l], dycat, convw_full[l], rvecs[l], wa_b[l], wx_b[l], l)
        dproj, dpw[l], dvec_p = _pool_bwd(projs[l], dycat, dproj, poolw_full[l], pvecs[l], l)
        dx, dvec_i = _inproj_bwd_x(dproj, win_g, xs[l], dx, vecs[l], l)
        dwin[l] = _inproj_bwd_w(dproj, xs[l], vecs[l], l)
        slab_parts[l] = [dvec_i, dvec_o, dvec_r, dvec_p,
                         dwa.reshape(nh * hd * hd // d, d), dwx.reshape(nh * hd * hd // d, d)]
    grad_x = dx.reshape(x.shape)

    slab = jnp.concatenate([p for l in range(nl) for p in slab_parts[l]], axis=0)
    rwin, rwout, rpw, slabs = _grad_exchange(dwin, dwout, dpw, slab)
    p_win = _sum_slots(rwin.reshape(4, nl * d, -1), "sum_w_in")
    p_wout = _sum_slots(rwout.reshape(4, -1, d), "sum_w_out")
    p_pw = _sum_slots(rpw.reshape(4, nl * ng * gq, gd), "sum_pool_w")
    q_win, q_wout, q_pw = _sibling_swap([p_win, p_wout, p_pw])
    total = _sum_slots(slabs, "sum_slab")

    rows_per_layer = slab.shape[0] // nl
    gate_rows = nh * hd * hd // d

    def small(l, r):
        return total[l * rows_per_layer + r]

    def stack(r):
        return jnp.stack([small(l, r) for l in range(nl)])

    g_dmod = jnp.stack([jnp.concatenate([small(l, 0), small(l, 1), small(l, 8)]) for l in range(nl)])
    g_pre, g_post = stack(2), stack(9)
    g_convb, g_ba, g_bx, g_lam = stack(16), stack(17), stack(18), stack(19)
    g_convw = jnp.stack([jnp.stack([small(l, 20 + k) for k in range(CONV_WIDTH)]) for l in range(nl)])
    g_poolb, g_pscale = stack(32), stack(33)

    def gate_block(l, which):
        r0 = l * rows_per_layer + 40 + which * gate_rows
        return total[r0:r0 + gate_rows].reshape(nh, hd, hd)

    g_wa = jnp.stack([gate_block(l, 0) for l in range(nl)])
    g_wx = jnp.stack([gate_block(l, 1) for l in range(nl)])

    per_dev = jnp.stack([jnp.concatenate([slabs[:, l * rows_per_layer + r] for r in (0, 1, 8)], axis=-1)
                         for l in range(nl)])
    dmod_mine = lax.dynamic_slice_in_dim(per_dev, chip * n_ada, n_ada, axis=2)
    c_all_t = cbuf.reshape(8, 8, d)[:, 0, :].T
    g_ada_w = _ada_w_grad(c_all_t, dmod_mine)

    def my_cols(full, width):
        return lax.dynamic_slice_in_dim(full, chip * width, width, axis=full.ndim - 1)

    g_convw_mine = my_cols(g_convw, d // 4)
    g_poolb_mine = my_cols(g_poolb.reshape(nl, ng, gd), gq)

    results = {
        "ada_w": _adamw(ada_w, m_ada_w, v_ada_w, [g_ada_w], "adamw_ada_w"),
        "ada_b": _adamw(ada_b, m_ada_b, v_ada_b, [g_dmod], "adamw_ada_b"),
        "pre_norm_g": _adamw(pre_norm_g, m_pre_norm_g, v_pre_norm_g, [g_pre], "adamw_pre_norm_g"),
        "w_in": _adamw(w_in, m_w_in, v_w_in, [p_win.reshape(w_in.shape), q_win.reshape(w_in.shape)], "adamw_w_in"),
        "conv_w": _adamw(conv_w, m_conv_w, v_conv_w, [g_convw_mine], "adamw_conv_w"),
        "conv_b": _adamw(conv_b, m_conv_b, v_conv_b, [g_convb], "adamw_conv_b"),
        "gate_a_w": _adamw(gate_a_w, m_gate_a_w, v_gate_a_w, [g_wa], "adamw_gate_a_w"),
        "gate_a_b": _adamw(gate_a_b, m_gate_a_b, v_gate_a_b, [g_ba.reshape(gate_a_b.shape)], "adamw_gate_a_b"),
        "gate_x_w": _adamw(gate_x_w, m_gate_x_w, v_gate_x_w, [g_wx], "adamw_gate_x_w"),
        "gate_x_b": _adamw(gate_x_b, m_gate_x_b, v_gate_x_b, [g_bx.reshape(gate_x_b.shape)], "adamw_gate_x_b"),
        "lru_lambda": _adamw(lru_lambda, m_lru_lambda, v_lru_lambda, [g_lam], "adamw_lru_lambda"),
        "pool_w": _adamw(pool_w, m_pool_w, v_pool_w,
                         [p_pw.reshape(pool_w.shape), q_pw.reshape(pool_w.shape)], "adamw_pool_w"),
        "pool_b": _adamw(pool_b, m_pool_b, v_pool_b, [g_poolb_mine], "adamw_pool_b"),
        "pool_scale": _adamw(pool_scale, m_pool_scale, v_pool_scale, [g_pscale], "adamw_pool_scale"),
        "w_out": _adamw(w_out, m_w_out, v_w_out,
                        [p_wout.reshape(w_out.shape), q_wout.reshape(w_out.shape)], "adamw_w_out"),
        "post_norm_g": _adamw(post_norm_g, m_post_norm_g, v_post_norm_g, [g_post], "adamw_post_norm_g"),
    }
    names = list(results)
    return (loss, grad_x,
            *[results[n][0] for n in names], *[results[n][1] for n in names],
            *[results[n][2] for n in names], *[results[n][3] for n in names])
```

```python
import functools

import jax
import jax.numpy as jnp
from jax import lax
from jax.experimental import pallas as pl
from jax.experimental.pallas import tpu as pltpu

F32 = jnp.float32
BF16 = jnp.bfloat16

NORM_EPS = 1e-6
LRU_C = 8.0
CONV_WIDTH = 4
MAX_POOL_WINDOW = 16
HALO = 16
ADAM_LR = 0.001
ADAM_B1 = 0.9
ADAM_B2 = 0.999
ADAM_EPS = 1e-08
ADAM_WD = 0.01
ADAM_STEP = 10

V7X_VMEM_LIMIT_BYTES = 56 * 1024 * 1024
MATMUL_ROWS = 256
SCAN_ROWS = 512
ELEMENTWISE_ROWS = 512

MESH = pl.DeviceIdType.MESH
ANY = pl.BlockSpec(memory_space=pl.ANY)
VMEM = pl.BlockSpec(memory_space=pltpu.VMEM)
HBM = pl.BlockSpec(memory_space=pltpu.HBM)
SEM = pl.BlockSpec(memory_space=pltpu.SEMAPHORE)
DATAFLOW_EFFECT = pltpu.SideEffectType.DATAFLOW_SIDE_EFFECTING

NT_DIMS = (((1,), (1,)), ((), ()))
TN_DIMS = (((0,), (0,)), ((), ()))


def _params(n_grid_axes):
    return pltpu.CompilerParams(dimension_semantics=("arbitrary",) * n_grid_axes,
                                vmem_limit_bytes=V7X_VMEM_LIMIT_BYTES)


def _tile(total, want):
    t = min(want, max(total // 2, HALO))
    assert total % t == 0 and t % HALO == 0, (total, t)
    return t


def _row_tile(rows):
    for t in range(min(rows, ELEMENTWISE_ROWS) // 8 * 8, 0, -8):
        if rows % t == 0:
            return t
    return rows


def _sigmoid(z):
    return 1.0 / (1.0 + jnp.exp(-z))


def _softplus(z):
    return jnp.maximum(z, 0.0) + jnp.log(1.0 + jnp.exp(-jnp.abs(z)))


def _neg_expm1(z):
    return -jnp.tanh(0.5 * z) * (jnp.exp(z) + 1.0)


def _colsum(v):
    return jnp.sum(v, axis=0, keepdims=True)


def _prenorm(xt, vec_ref):
    rs = lax.rsqrt(jnp.mean(xt * xt, axis=-1, keepdims=True) + NORM_EPS)
    xn = xt * rs
    h = xn * vec_ref[3:4, :] * (1.0 + vec_ref[1:2, :]) + vec_ref[0:1, :]
    return h, xn, rs


def _shift_down(v, d, fill):
    t = v.shape[0]
    if d % 8 == 0:
        return jnp.concatenate([jnp.full((d, v.shape[1]), fill, v.dtype), v[:t - d]], axis=0)
    row = lax.broadcasted_iota(jnp.int32, v.shape, 0)
    return jnp.where(row >= d, pltpu.roll(v, d, 0), fill)


def _shift_up(v, d, fill):
    t = v.shape[0]
    if d % 8 == 0:
        return jnp.concatenate([v[d:], jnp.full((d, v.shape[1]), fill, v.dtype)], axis=0)
    row = lax.broadcasted_iota(jnp.int32, v.shape, 0)
    return jnp.where(row < t - d, pltpu.roll(v, t - d, 0), fill)


def _scan_fwd(a, v, h_before):
    d = 1
    while d < a.shape[0]:
        v = v + a * _shift_down(v, d, 0.0)
        a = a * _shift_down(a, d, 1.0)
        d *= 2
    return a * h_before + v


def _scan_rev(b, v):
    d = 1
    while d < b.shape[0]:
        v = v + b * _shift_up(v, d, 0.0)
        b = b * _shift_up(b, d, 0.0)
        d *= 2
    return v


def _inproj_fwd(x, vec, w_all, layer):
    s, d = x.shape
    p = w_all.shape[2]
    ts = _tile(s, MATMUL_ROWS)

    def body(x_ref, vec_ref, w_ref, proj_ref):
        h, _, _ = _prenorm(x_ref[...], vec_ref)
        hb = h.astype(BF16)
        for k in range(4):
            proj_ref[k] = jnp.dot(hb, w_ref[k], preferred_element_type=F32)

    return pl.pallas_call(
        body, name=f"inproj_fwd_l{layer}", grid=(s // ts,),
        in_specs=[pl.BlockSpec((ts, d), lambda i: (i, 0)),
                  pl.BlockSpec((8, d), lambda i: (0, 0)),
                  pl.BlockSpec((4, d, p), lambda i: (0, 0, 0))],
        out_specs=pl.BlockSpec((4, ts, p), lambda i: (0, i, 0)),
        out_shape=jax.ShapeDtypeStruct((4, s, p), F32),
        compiler_params=_params(1),
    )(x, vec, w_all)


def _rnn_gates(u, wa_ref, wx_ref, vec_ref):
    ub = u.astype(BF16)
    r = _sigmoid(jnp.dot(ub, wa_ref[0], preferred_element_type=F32) + vec_ref[1:2, :])
    ig = _sigmoid(jnp.dot(ub, wx_ref[0], preferred_element_type=F32) + vec_ref[2:3, :])
    sp = _softplus(-vec_ref[3:4, :])
    log_a = (-LRU_C) * r * sp
    return ub, r, ig, sp, log_a


def _conv(xbuf, cw_ref, vec_ref, ts):
    u = vec_ref[0:1, :] + cw_ref[CONV_WIDTH - 1:CONV_WIDTH, :] * xbuf[pl.ds(HALO, ts), :]
    for k in range(CONV_WIDTH - 1):
        u = u + cw_ref[k:k + 1, :] * xbuf[pl.ds(HALO - (CONV_WIDTH - 1) + k, ts), :]
    return u


def _rnn_fwd(proj, cw, vec, wa, wx, layer):
    _, s, d = proj.shape
    nh, hd, _ = wa.shape
    ts = _tile(s, SCAN_ROWS)

    def body(proj_ref, cw_ref, vec_ref, wa_ref, wx_ref, ycat_ref, hs_ref, xbuf, hlast):
        i = pl.program_id(1)

        @pl.when(i == 0)
        def _():
            xbuf[0:HALO, :] = jnp.zeros((HALO, hd), F32)
            hlast[...] = jnp.zeros_like(hlast)

        xbuf[pl.ds(HALO, ts), :] = proj_ref[0]
        u = _conv(xbuf, cw_ref, vec_ref, ts)
        _, _, ig, _, log_a = _rnn_gates(u, wa_ref, wx_ref, vec_ref)
        a = jnp.exp(log_a)
        mult = jnp.sqrt(_neg_expm1(2.0 * log_a))
        hs = _scan_fwd(a, mult * (ig * u), hlast[0:1, :])
        hs_ref[...] = hs
        hlast[0:1, :] = hs_ref[ts - 1:ts, :]
        g = proj_ref[1]
        ycat_ref[...] = (hs * (g * _sigmoid(g))).astype(BF16)
        xbuf[0:HALO, :] = xbuf[pl.ds(ts, HALO), :]

    return pl.pallas_call(
        body, name=f"rnn_fwd_l{layer}", grid=(nh, s // ts),
        in_specs=[pl.BlockSpec((2, ts, hd), lambda h, i: (0, i, h)),
                  pl.BlockSpec((CONV_WIDTH, hd), lambda h, i: (0, h)),
                  pl.BlockSpec((8, hd), lambda h, i: (0, h)),
                  pl.BlockSpec((1, hd, hd), lambda h, i: (h, 0, 0)),
                  pl.BlockSpec((1, hd, hd), lambda h, i: (h, 0, 0))],
        out_specs=[pl.BlockSpec((ts, hd), lambda h, i: (i, h)),
                   pl.BlockSpec((ts, hd), lambda h, i: (i, h))],
        out_shape=[jax.ShapeDtypeStruct((s, 2 * d), BF16), jax.ShapeDtypeStruct((s, d), F32)],
        scratch_shapes=[pltpu.VMEM((ts + HALO, hd), F32), pltpu.VMEM((8, hd), F32)],
        compiler_params=_params(2),
    )(proj, cw, vec, wa, wx)


def _inv_count(i, ts, lanes, win):
    t = i * ts + lax.broadcasted_iota(jnp.int32, (ts, lanes), 0)
    return 1.0 / jnp.minimum(t + 1, win).astype(F32)


def _pooled(xbuf, xt, lanes, win, inv_cnt, ts):
    acc = xt
    for dlt in range(1, win):
        acc = acc + xbuf[pl.ds(HALO - dlt, ts), lanes]
    return acc * inv_cnt - xt


def _pool_fwd(proj, ycat, pw, vec, layer):
    _, s, d = proj.shape
    ng, gd, _ = pw.shape
    ts = _tile(s, MATMUL_ROWS)

    def body(proj_ref, ycat_in, pw_ref, vec_ref, ycat_ref, xbuf):
        del ycat_in
        i = pl.program_id(0)

        @pl.when(i == 0)
        def _():
            xbuf[0:HALO, :] = jnp.zeros((HALO, d), F32)

        xbuf[pl.ds(HALO, ts), :] = proj_ref[0]
        for g in range(ng):
            lanes = slice(g * gd, (g + 1) * gd)
            win = 2 << g
            xt = proj_ref[0, :, lanes]
            pooled = _pooled(xbuf, xt, lanes, win, _inv_count(i, ts, gd, win), ts).astype(BF16)
            z = jnp.dot(pooled, pw_ref[g], preferred_element_type=F32) + vec_ref[0:1, lanes]
            gg = proj_ref[1, :, lanes]
            ycat_ref[:, lanes] = (z * vec_ref[1:2, lanes] * (gg * _sigmoid(gg))).astype(BF16)
        xbuf[0:HALO, :] = xbuf[pl.ds(ts, HALO), :]

    return pl.pallas_call(
        body, name=f"pool_fwd_l{layer}", grid=(s // ts,),
        in_specs=[pl.BlockSpec((2, ts, d), lambda i: (1, i, 0)),
                  ANY,
                  pl.BlockSpec((ng, gd, gd), lambda i: (0, 0, 0)),
                  pl.BlockSpec((8, d), lambda i: (0, 0))],
        out_specs=pl.BlockSpec((ts, d), lambda i: (i, 1)),
        out_shape=jax.ShapeDtypeStruct((s, 2 * d), BF16),
        input_output_aliases={1: 0},
        scratch_shapes=[pltpu.VMEM((ts + HALO, d), F32)],
        compiler_params=_params(1),
    )(proj, ycat, pw, vec)


def _outproj_fwd(ycat, w_all, x, vec, target, layer):
    s, d = x.shape
    nk, kd = w_all.shape[0], w_all.shape[1]
    ts = _tile(s, MATMUL_ROWS)
    last = target is not None

    def body(*refs):
        if last:
            ycat_ref, w_ref, x_ref, vec_ref, tgt_ref, y_ref, xo_ref, sq_ref = refs
        else:
            ycat_ref, w_ref, x_ref, vec_ref, y_ref, xo_ref = refs
        y = jnp.dot(ycat_ref[:, 0:kd], w_ref[0], preferred_element_type=F32)
        for k in range(1, nk):
            y = y + jnp.dot(ycat_ref[:, k * kd:(k + 1) * kd], w_ref[k], preferred_element_type=F32)
        y_ref[...] = y
        rs = lax.rsqrt(jnp.mean(y * y, axis=-1, keepdims=True) + NORM_EPS)
        xo = x_ref[...] + vec_ref[2:3, :] * (y * rs * vec_ref[4:5, :])
        if last:
            err = xo - tgt_ref[...]
            xo_ref[...] = err * (1.0 / d)

            @pl.when(pl.program_id(0) == 0)
            def _():
                sq_ref[...] = jnp.zeros_like(sq_ref)

            sq_ref[...] += jnp.sum(err * err)
        else:
            xo_ref[...] = xo

    row = pl.BlockSpec((ts, d), lambda i: (i, 0))
    in_specs = [pl.BlockSpec((ts, nk * kd), lambda i: (i, 0)),
                pl.BlockSpec((nk, kd, d), lambda i: (0, 0, 0)),
                row, pl.BlockSpec((8, d), lambda i: (0, 0))]
    out_specs = [row, row]
    out_shape = [jax.ShapeDtypeStruct((s, d), F32), jax.ShapeDtypeStruct((s, d), F32)]
    args = [ycat, w_all, x, vec]
    if last:
        in_specs.append(row)
        args.append(target)
        out_specs.append(pl.BlockSpec((8, 128), lambda i: (0, 0)))
        out_shape.append(jax.ShapeDtypeStruct((8, 128), F32))
    out = pl.pallas_call(
        body, name=f"outproj_fwd_l{layer}", grid=(s // ts,),
        in_specs=in_specs, out_specs=out_specs, out_shape=out_shape,
        compiler_params=_params(1),
    )(*args)
    return (out[0], out[1], out[2]) if last else (out[0], out[1], None)


def _outproj_bwd(dxo, y, ycat, w_all, vec, layer):
    s, d = dxo.shape
    nk, kd = w_all.shape[0], w_all.shape[1]
    ts = _tile(s, MATMUL_ROWS)
    nt = s // ts

    def body(dxo_ref, y_ref, ycat_ref, w_ref, vec_ref, dycat_ref, dw_ref, dvec_ref, acc):
        i = pl.program_id(0)

        @pl.when(i == 0)
        def _():
            acc[...] = jnp.zeros_like(acc)
            dvec_ref[...] = jnp.zeros_like(dvec_ref)

        yt = y_ref[...]
        rs = lax.rsqrt(jnp.mean(yt * yt, axis=-1, keepdims=True) + NORM_EPS)
        yhat = yt * rs
        gate, gpost = vec_ref[2:3, :], vec_ref[4:5, :]
        dxo_t = dxo_ref[...]
        dyn = dxo_t * gate
        dvec_ref[0:1, :] += _colsum(dxo_t * (yhat * gpost))
        dvec_ref[1:2, :] += _colsum(dyn * yhat)
        t = dyn * gpost
        dy = (rs * (t - yhat * jnp.mean(t * yhat, axis=-1, keepdims=True))).astype(BF16)
        for k in range(nk):
            cols = slice(k * kd, (k + 1) * kd)
            dycat_ref[:, cols] = lax.dot_general(dy, w_ref[k], NT_DIMS, preferred_element_type=F32)
            acc[k] += lax.dot_general(ycat_ref[:, cols], dy, TN_DIMS, preferred_element_type=F32)

        @pl.when(i == nt - 1)
        def _():
            dw_ref[...] = acc[...].astype(BF16)

    row = pl.BlockSpec((ts, d), lambda i: (i, 0))
    wide = pl.BlockSpec((ts, nk * kd), lambda i: (i, 0))
    return pl.pallas_call(
        body, name=f"outproj_bwd_l{layer}", grid=(nt,),
        in_specs=[row, row, wide,
                  pl.BlockSpec((nk, kd, d), lambda i: (0, 0, 0)),
                  pl.BlockSpec((8, d), lambda i: (0, 0))],
        out_specs=[wide,
                   pl.BlockSpec((nk, kd, d), lambda i: (0, 0, 0)),
                   pl.BlockSpec((8, d), lambda i: (0, 0))],
        out_shape=[jax.ShapeDtypeStruct((s, nk * kd), F32),
                   jax.ShapeDtypeStruct((nk, kd, d), BF16),
                   jax.ShapeDtypeStruct((8, d), F32)],
        scratch_shapes=[pltpu.VMEM((nk, kd, d), F32)],
        compiler_params=_params(1),
    )(dxo, y, ycat, w_all, vec)


def _halo_index(ts, nt):
    return lambda j: jnp.maximum((nt - 1 - j) * (ts // HALO) - 1, 0)


def _rnn_bwd(proj, hs, dycat, cw, vec, wa, wx, layer):
    _, s, d = proj.shape
    nh, hd, _ = wa.shape
    ts = _tile(s, SCAN_ROWS)
    nt = s // ts
    halo = _halo_index(ts, nt)

    def body(proj_ref, xh_ref, hs_ref, hsh_ref, dy_ref, cw_ref, vec_ref, wa_ref, wx_ref,
             dproj_ref, dwa_ref, dwx_ref, dvec_ref, xbuf, hbuf, dubuf, carry, dwa_acc, dwx_acc):
        j = pl.program_id(1)
        first_tile = j == nt - 1

        @pl.when(j == 0)
        def _():
            dubuf[pl.ds(ts, HALO), :] = jnp.zeros((HALO, hd), F32)
            carry[...] = jnp.zeros_like(carry)
            dwa_acc[...] = jnp.zeros_like(dwa_acc)
            dwx_acc[...] = jnp.zeros_like(dwx_acc)
            dvec_ref[...] = jnp.zeros_like(dvec_ref)

        xbuf[0:HALO, :] = jnp.where(first_tile, 0.0, xh_ref[0])
        xbuf[pl.ds(HALO, ts), :] = proj_ref[0]
        hbuf[0:HALO, :] = jnp.where(first_tile, 0.0, hsh_ref[...])
        hs = hs_ref[...]
        hbuf[pl.ds(HALO, ts), :] = hs

        u = _conv(xbuf, cw_ref, vec_ref, ts)
        ub, r, ig, sp, log_a = _rnn_gates(u, wa_ref, wx_ref, vec_ref)
        a = jnp.exp(log_a)
        e2 = jnp.exp(2.0 * log_a)
        mult = jnp.sqrt(_neg_expm1(2.0 * log_a))

        g = proj_ref[1]
        sg = _sigmoid(g)
        dyc = dy_ref[...]
        dproj_ref[1] = (dyc * hs * (sg * (1.0 + g * (1.0 - sg)))).astype(BF16)

        row = lax.broadcasted_iota(jnp.int32, (ts, hd), 0)
        dhs = dyc * (g * sg) + jnp.where(row == ts - 1, carry[0:1, :], 0.0)
        dh = _scan_rev(_shift_up(a, 1, 0.0), dhs)
        carry[...] = (a * dh)[0:8, :]

        h_prev = hbuf[pl.ds(HALO - 1, ts), :]
        dlog_a = dh * h_prev * a - dh * (ig * u) * (e2 / mult)
        di = dh * mult * u
        dzr = dlog_a * ((-LRU_C) * sp) * (r * (1.0 - r))
        dzi = di * (ig * (1.0 - ig))
        dvec_ref[3:4, :] += _colsum(dlog_a * r) * (LRU_C * _sigmoid(-vec_ref[3:4, :]))
        dvec_ref[1:2, :] += _colsum(dzr)
        dvec_ref[2:3, :] += _colsum(dzi)
        dzr_b, dzi_b = dzr.astype(BF16), dzi.astype(BF16)
        dwa_acc[...] += lax.dot_general(ub, dzr_b, TN_DIMS, preferred_element_type=F32)
        dwx_acc[...] += lax.dot_general(ub, dzi_b, TN_DIMS, preferred_element_type=F32)
        du = (dh * mult * ig
              + lax.dot_general(dzr_b, wa_ref[0], NT_DIMS, preferred_element_type=F32)
              + lax.dot_general(dzi_b, wx_ref[0], NT_DIMS, preferred_element_type=F32))
        dvec_ref[0:1, :] += _colsum(du)
        for k in range(CONV_WIDTH):
            dvec_ref[4 + k:5 + k, :] += _colsum(du * xbuf[pl.ds(HALO - (CONV_WIDTH - 1) + k, ts), :])

        dubuf[0:ts, :] = du
        dx = cw_ref[CONV_WIDTH - 1:CONV_WIDTH, :] * du
        for k in range(CONV_WIDTH - 1):
            dx = dx + cw_ref[k:k + 1, :] * dubuf[pl.ds(CONV_WIDTH - 1 - k, ts), :]
        dproj_ref[0] = dx.astype(BF16)
        dubuf[pl.ds(ts, HALO), :] = dubuf[0:HALO, :]

        @pl.when(first_tile)
        def _():
            dwa_ref[0] = dwa_acc[...].astype(BF16)
            dwx_ref[0] = dwx_acc[...].astype(BF16)

    rev = lambda h, j: (nt - 1 - j, h)
    return pl.pallas_call(
        body, name=f"rnn_bwd_l{layer}", grid=(nh, nt),
        in_specs=[pl.BlockSpec((2, ts, hd), lambda h, j: (0, nt - 1 - j, h)),
                  pl.BlockSpec((1, HALO, hd), lambda h, j: (0, halo(j), h)),
                  pl.BlockSpec((ts, hd), rev),
                  pl.BlockSpec((HALO, hd), lambda h, j: (halo(j), h)),
                  pl.BlockSpec((ts, hd), rev),
                  pl.BlockSpec((CONV_WIDTH, hd), lambda h, j: (0, h)),
                  pl.BlockSpec((8, hd), lambda h, j: (0, h)),
                  pl.BlockSpec((1, hd, hd), lambda h, j: (h, 0, 0)),
                  pl.BlockSpec((1, hd, hd), lambda h, j: (h, 0, 0))],
        out_specs=[pl.BlockSpec((2, ts, hd), lambda h, j: (0, nt - 1 - j, h)),
                   pl.BlockSpec((1, hd, hd), lambda h, j: (h, 0, 0)),
                   pl.BlockSpec((1, hd, hd), lambda h, j: (h, 0, 0)),
                   pl.BlockSpec((16, hd), lambda h, j: (0, h))],
        out_shape=[jax.ShapeDtypeStruct((4, s, d), BF16),
                   jax.ShapeDtypeStruct((nh, hd, hd), BF16),
                   jax.ShapeDtypeStruct((nh, hd, hd), BF16),
                   jax.ShapeDtypeStruct((16, d), F32)],
        scratch_shapes=[pltpu.VMEM((ts + HALO, hd), F32), pltpu.VMEM((ts + HALO, hd), F32),
                        pltpu.VMEM((ts + HALO, hd), F32), pltpu.VMEM((8, hd), F32),
                        pltpu.VMEM((hd, hd), F32), pltpu.VMEM((hd, hd), F32)],
        compiler_params=_params(2),
    )(proj, proj, hs, hs, dycat, cw, vec, wa, wx)


def _pool_bwd(proj, dycat, dproj, pw, vec, layer):
    _, s, d = proj.shape
    ng, gd, _ = pw.shape
    ts = _tile(s, MATMUL_ROWS)
    nt = s // ts
    halo = _halo_index(ts, nt)

    def body(proj_ref, xh_ref, dy_ref, dproj_in, pw_ref, vec_ref, dproj_ref, dpw_ref, dvec_ref, xbuf, qbuf, acc):
        del dproj_in
        j = pl.program_id(0)
        i = nt - 1 - j

        @pl.when(j == 0)
        def _():
            qbuf[pl.ds(ts, HALO), :] = jnp.zeros((HALO, d), F32)
            acc[...] = jnp.zeros_like(acc)
            dvec_ref[...] = jnp.zeros_like(dvec_ref)

        xbuf[0:HALO, :] = jnp.where(i == 0, 0.0, xh_ref[0])
        xbuf[pl.ds(HALO, ts), :] = proj_ref[0]
        for g in range(ng):
            lanes = slice(g * gd, (g + 1) * gd)
            win = 2 << g
            xt = proj_ref[0, :, lanes]
            inv_cnt = _inv_count(i, ts, gd, win)
            pooled = _pooled(xbuf, xt, lanes, win, inv_cnt, ts).astype(BF16)
            z = jnp.dot(pooled, pw_ref[g], preferred_element_type=F32) + vec_ref[0:1, lanes]
            scale = vec_ref[1:2, lanes]
            gg = proj_ref[1, :, lanes]
            sg = _sigmoid(gg)
            dyc = dy_ref[:, lanes]
            dyp = dyc * (gg * sg)
            dproj_ref[1, :, lanes] = (dyc * (z * scale) * (sg * (1.0 + gg * (1.0 - sg)))).astype(BF16)
            dvec_ref[1:2, lanes] += _colsum(dyp * z)
            dz = dyp * scale
            dvec_ref[0:1, lanes] += _colsum(dz)
            dz_b = dz.astype(BF16)
            acc[g] += lax.dot_general(pooled, dz_b, TN_DIMS, preferred_element_type=F32)
            dpooled = lax.dot_general(dz_b, pw_ref[g], NT_DIMS, preferred_element_type=F32)

            q = dpooled * inv_cnt
            qbuf[0:ts, lanes] = q
            dx = q - dpooled
            for dlt in range(1, win):
                dx = dx + qbuf[pl.ds(dlt, ts), lanes]
            dproj_ref[0, :, lanes] = dx.astype(BF16)
        qbuf[pl.ds(ts, HALO), :] = qbuf[0:HALO, :]

        @pl.when(j == nt - 1)
        def _():
            dpw_ref[...] = acc[...].astype(BF16)

    return pl.pallas_call(
        body, name=f"pool_bwd_l{layer}", grid=(nt,),
        in_specs=[pl.BlockSpec((2, ts, d), lambda j: (1, nt - 1 - j, 0)),
                  pl.BlockSpec((1, HALO, d), lambda j: (2, halo(j), 0)),
                  pl.BlockSpec((ts, d), lambda j: (nt - 1 - j, 1)),
                  ANY,
                  pl.BlockSpec((ng, gd, gd), lambda j: (0, 0, 0)),
                  pl.BlockSpec((8, d), lambda j: (0, 0))],
        out_specs=[pl.BlockSpec((2, ts, d), lambda j: (1, nt - 1 - j, 0)),
                   pl.BlockSpec((ng, gd, gd), lambda j: (0, 0, 0)),
                   pl.BlockSpec((8, d), lambda j: (0, 0))],
        out_shape=[jax.ShapeDtypeStruct((4, s, d), BF16),
                   jax.ShapeDtypeStruct((ng, gd, gd), BF16),
                   jax.ShapeDtypeStruct((8, d), F32)],
        input_output_aliases={3: 0},
        scratch_shapes=[pltpu.VMEM((ts + HALO, d), F32), pltpu.VMEM((ts + HALO, d), F32),
                        pltpu.VMEM((ng, gd, gd), F32)],
        compiler_params=_params(1),
    )(proj, proj, dycat, dproj, pw, vec)


def _inproj_bwd_x(dproj, w_all, x, dxo, vec, layer):
    s, d = x.shape
    p = w_all.shape[2]
    ts = _tile(s, MATMUL_ROWS)

    def body(dp_ref, w_ref, x_ref, dxo_ref, vec_ref, dx_ref, dvec_ref):
        @pl.when(pl.program_id(0) == 0)
        def _():
            dvec_ref[...] = jnp.zeros_like(dvec_ref)

        dh = lax.dot_general(dp_ref[0], w_ref[0], NT_DIMS, preferred_element_type=F32)
        for k in range(1, 4):
            dh = dh + lax.dot_general(dp_ref[k], w_ref[k], NT_DIMS, preferred_element_type=F32)
        _, xn, rs = _prenorm(x_ref[...], vec_ref)
        gpre, scale1 = vec_ref[3:4, :], 1.0 + vec_ref[1:2, :]
        dvec_ref[0:1, :] += _colsum(dh)
        dvec_ref[1:2, :] += _colsum(dh * (xn * gpre))
        dvec_ref[2:3, :] += _colsum(dh * (xn * scale1))
        t = dh * (gpre * scale1)
        dx_ref[...] = dxo_ref[...] + rs * (t - xn * jnp.mean(t * xn, axis=-1, keepdims=True))

    row = pl.BlockSpec((ts, d), lambda i: (i, 0))
    return pl.pallas_call(
        body, name=f"inproj_bwd_x_l{layer}", grid=(s // ts,),
        in_specs=[pl.BlockSpec((4, ts, p), lambda i: (0, i, 0)),
                  pl.BlockSpec((4, d, p), lambda i: (0, 0, 0)),
                  row, row, pl.BlockSpec((8, d), lambda i: (0, 0))],
        out_specs=[row, pl.BlockSpec((8, d), lambda i: (0, 0))],
        out_shape=[jax.ShapeDtypeStruct((s, d), F32), jax.ShapeDtypeStruct((8, d), F32)],
        compiler_params=_params(1),
    )(dproj, w_all, x, dxo, vec)


def _inproj_bwd_w(dproj, x, vec, layer):
    s, d = x.shape
    p = dproj.shape[2]
    ts = _tile(s, MATMUL_ROWS)
    nt = s // ts

    def body(dp_ref, x_ref, vec_ref, dw_ref, acc):
        i = pl.program_id(1)

        @pl.when(i == 0)
        def _():
            acc[...] = jnp.zeros_like(acc)

        h, _, _ = _prenorm(x_ref[...], vec_ref)
        acc[...] += lax.dot_general(h.astype(BF16), dp_ref[0], TN_DIMS, preferred_element_type=F32)

        @pl.when(i == nt - 1)
        def _():
            dw_ref[0] = acc[...].astype(BF16)

    return pl.pallas_call(
        body, name=f"inproj_bwd_w_l{layer}", grid=(4, nt),
        in_specs=[pl.BlockSpec((1, ts, p), lambda k, i: (k, i, 0)),
                  pl.BlockSpec((ts, d), lambda k, i: (i, 0)),
                  pl.BlockSpec((8, d), lambda k, i: (0, 0))],
        out_specs=pl.BlockSpec((1, d, p), lambda k, i: (k, 0, 0)),
        out_shape=jax.ShapeDtypeStruct((4, d, p), BF16),
        scratch_shapes=[pltpu.VMEM((d, p), F32)],
        compiler_params=_params(2),
    )(dproj, x, vec)


def _sum_slots(stacked, name):
    n, rows, cols = stacked.shape
    tr = _row_tile(rows)

    def body(in_ref, out_ref):
        total = in_ref[0].astype(F32)
        for b in range(1, n):
            total = total + in_ref[b].astype(F32)
        out_ref[...] = total

    return pl.pallas_call(
        body, name=name, grid=(rows // tr,),
        in_specs=[pl.BlockSpec((n, tr, cols), lambda i: (0, i, 0))],
        out_specs=pl.BlockSpec((tr, cols), lambda i: (i, 0)),
        out_shape=jax.ShapeDtypeStruct((rows, cols), F32),
        compiler_params=_params(1),
    )(stacked)


def _adamw(w, m, v, grads, name):
    shape = w.shape
    cols = shape[-1]
    rows = w.size // cols
    tr = _row_tile(rows)
    n = len(grads)

    def body(*refs):
        w_ref, m_ref, v_ref = refs[:3]
        g_refs = refs[3:3 + n]
        g_out, d_out, m_out, v_out = refs[3 + n:]
        g = g_refs[0][...]
        for r in g_refs[1:]:
            g = g + r[...]
        m_new = ADAM_B1 * m_ref[...] + (1.0 - ADAM_B1) * g
        v_new = ADAM_B2 * v_ref[...] + (1.0 - ADAM_B2) * (g * g)
        m_hat = m_new / (1.0 - ADAM_B1 ** ADAM_STEP)
        v_hat = v_new / (1.0 - ADAM_B2 ** ADAM_STEP)
        g_out[...] = g
        d_out[...] = (-ADAM_LR) * (m_hat / (jnp.sqrt(v_hat) + ADAM_EPS) + ADAM_WD * w_ref[...])
        m_out[...] = m_new
        v_out[...] = v_new

    blk = pl.BlockSpec((tr, cols), lambda i: (i, 0))
    outs = pl.pallas_call(
        body, name=name, grid=(rows // tr,),
        in_specs=[blk] * (3 + n), out_specs=[blk] * 4,
        out_shape=[jax.ShapeDtypeStruct((rows, cols), F32)] * 4,
        compiler_params=_params(1),
    )(*[a.reshape(rows, cols) for a in (w, m, v, *grads)])
    return tuple(o.reshape(shape) for o in outs)


def _adam_update(w, m, v, g):
    m_new = ADAM_B1 * m + (1.0 - ADAM_B1) * g
    v_new = ADAM_B2 * v + (1.0 - ADAM_B2) * (g * g)
    m_hat = m_new / (1.0 - ADAM_B1 ** ADAM_STEP)
    v_hat = v_new / (1.0 - ADAM_B2 ** ADAM_STEP)
    return (-ADAM_LR) * (m_hat / (jnp.sqrt(v_hat) + ADAM_EPS) + ADAM_WD * w), m_new, v_new


def _adamw_layer(w, m, v, grads, layer, prev, name):
    nl = w.shape[0]
    cols = w.shape[-1]
    rows = w.size // (nl * cols)
    tr = _row_tile(rows)
    off = layer * (rows // tr)
    n = len(grads)
    n_prev = 0 if prev is None else 4

    def body(*refs):
        w_ref, m_ref, v_ref = refs[:3]
        g_refs = refs[3:3 + n]
        g_out, d_out, m_out, v_out = refs[3 + n + n_prev:]
        g = g_refs[0][...]
        for r in g_refs[1:]:
            g = g + r[...]
        g_out[...] = g
        d_out[...], m_out[...], v_out[...] = _adam_update(w_ref[...], m_ref[...], v_ref[...], g)

    mine = pl.BlockSpec((tr, cols), lambda i: (off + i, 0))
    args = [a.reshape(nl * rows, cols) for a in (w, m, v)] + [g.reshape(rows, cols) for g in grads]
    outs = pl.pallas_call(
        body, name=name, grid=(rows // tr,),
        in_specs=[mine] * 3 + [pl.BlockSpec((tr, cols), lambda i: (i, 0))] * n + [ANY] * n_prev,
        out_specs=[mine] * 4,
        out_shape=[jax.ShapeDtypeStruct((nl * rows, cols), F32)] * 4,
        input_output_aliases={3 + n + k: k for k in range(n_prev)},
        compiler_params=_params(1),
    )(*args, *(prev or ()))
    return tuple(outs)


def _into_slot(a, dtype, chip_arr, name):
    rows, cols = a.shape
    tr = _row_tile(rows)

    def body(chip_ref, a_ref, out_ref):
        del chip_ref
        out_ref[...] = a_ref[...].astype(dtype)

    return pl.pallas_call(
        body, name=name,
        grid_spec=pltpu.PrefetchScalarGridSpec(
            num_scalar_prefetch=1, grid=(rows // tr,),
            in_specs=[pl.BlockSpec((tr, cols), lambda i, chip: (i, 0))],
            out_specs=pl.BlockSpec((None, tr, cols), lambda i, chip: (chip[0], i, 0))),
        out_shape=jax.ShapeDtypeStruct((4, rows, cols), dtype),
        compiler_params=_params(1),
    )(chip_arr, a)


def _sum_owner(own, land, chip_arr, own_block, own_index, name):
    blk = land.shape[1:]
    tr = _row_tile(blk[-2])
    steps = blk[-2] // tr
    tile = (*blk[:-2], tr, blk[-1])
    lead = (0,) * (len(blk) - 2)

    def body(chip_ref, own_ref, l1, l2, l3, out_ref):
        del chip_ref
        out_ref[...] = (own_ref[...].astype(F32) + l1[...].astype(F32)) + (l2[...].astype(F32) + l3[...].astype(F32))

    def landed(k):
        return pl.BlockSpec((None, *tile), lambda i, chip: (chip[0] ^ k, *lead, i, 0))

    return pl.pallas_call(
        body, name=name,
        grid_spec=pltpu.PrefetchScalarGridSpec(
            num_scalar_prefetch=1, grid=(steps,),
            in_specs=[pl.BlockSpec(own_block(tr), own_index), landed(1), landed(2), landed(3)],
            out_specs=pl.BlockSpec(tile, lambda i, chip: (*lead, i, 0))),
        out_shape=jax.ShapeDtypeStruct(blk, F32),
        compiler_params=_params(1),
    )(chip_arr, own, land, land, land)


def _ada_w_grad(c_t, dmod):
    d, nb = c_t.shape
    nl, _, n = dmod.shape

    def body(c_ref, dm_ref, out_ref):
        for layer in range(nl):
            total = c_ref[:, 0:1] * dm_ref[layer, 0:1, :]
            for b in range(1, nb):
                total = total + c_ref[:, b:b + 1] * dm_ref[layer, b:b + 1, :]
            out_ref[layer] = total

    return pl.pallas_call(
        body, name="ada_w_grad", in_specs=[VMEM, VMEM], out_specs=VMEM,
        out_shape=jax.ShapeDtypeStruct((nl, d, n), F32),
        compiler_params=pltpu.CompilerParams(vmem_limit_bytes=V7X_VMEM_LIMIT_BYTES),
    )(c_t, dmod)


def _place():
    x, y, c = lax.axis_index("x"), lax.axis_index("y"), lax.axis_index("c")
    return x, y, c


OTHER_CHIPS = ((1, 0), (0, 1), (1, 1))
OTHER_DEVICES = tuple((fx, fy, fc) for fx in (0, 1) for fy in (0, 1) for fc in (0, 1))[1:]


def _mod_exchange(c_row, ada_w):
    nl, d, n = ada_w.shape

    def body(c_ref, w_ref, cbuf, modbuf, cblk, mres, send_a, recv_a, send_c, recv_c):
        x, y, c = _place()
        me = 4 * x + 2 * y + c
        chip = 2 * x + y
        cv = c_ref[...]
        cblk[...] = jnp.zeros_like(cblk)
        cblk[0:1, :] = cv * _sigmoid(cv)

        def rows_of(dev):
            return cbuf.at[pl.ds(pl.multiple_of(8 * dev, 8), 8), :]

        cbuf[pl.ds(pl.multiple_of(8 * me, 8), 8), :] = cblk[...]
        sends = []
        for j, (fx, fy, fc) in enumerate(OTHER_DEVICES):
            cp = pltpu.make_async_remote_copy(
                src_ref=cblk, dst_ref=rows_of(me), send_sem=send_a.at[j], recv_sem=recv_a.at[j],
                device_id=(x ^ fx, y ^ fy, c ^ fc), device_id_type=MESH)
            cp.start()
            sends.append(cp)
        for j, (fx, fy, fc) in enumerate(OTHER_DEVICES):
            peer = 4 * (x ^ fx) + 2 * (y ^ fy) + (c ^ fc)
            pltpu.make_async_remote_copy(
                src_ref=cblk, dst_ref=rows_of(peer), send_sem=send_a.at[j], recv_sem=recv_a.at[j],
                device_id=(x ^ fx, y ^ fy, c ^ fc), device_id_type=MESH).wait_recv()
        for cp in sends:
            cp.wait_send()

        call = cbuf[...]
        for layer in range(nl):
            mres[:, layer * n:(layer + 1) * n] = jnp.dot(
                call, w_ref[layer], preferred_element_type=F32, precision=lax.Precision.HIGHEST)

        def block_of(dev):
            return mres.at[pl.ds(pl.multiple_of(8 * dev, 8), 8), :]

        modbuf[chip] = mres[pl.ds(pl.multiple_of(8 * me, 8), 8), :]
        sends = []
        for j, (fx, fy) in enumerate(OTHER_CHIPS):
            peer = 4 * (x ^ fx) + 2 * (y ^ fy) + c
            cp = pltpu.make_async_remote_copy(
                src_ref=block_of(peer), dst_ref=modbuf.at[chip], send_sem=send_c.at[j], recv_sem=recv_c.at[j],
                device_id=(x ^ fx, y ^ fy, c), device_id_type=MESH)
            cp.start()
            sends.append(cp)
        for j, (fx, fy) in enumerate(OTHER_CHIPS):
            pltpu.make_async_remote_copy(
                src_ref=block_of(me), dst_ref=modbuf.at[2 * (x ^ fx) + (y ^ fy)],
                send_sem=send_c.at[j], recv_sem=recv_c.at[j],
                device_id=(x ^ fx, y ^ fy, c), device_id_type=MESH).wait_recv()
        for cp in sends:
            cp.wait_send()

    return pl.pallas_call(
        body, name="mod_exchange", in_specs=[VMEM, VMEM], out_specs=[VMEM, VMEM],
        out_shape=[jax.ShapeDtypeStruct((64, d), F32), jax.ShapeDtypeStruct((4, 8, nl * n), F32)],
        scratch_shapes=[pltpu.VMEM((8, d), F32), pltpu.VMEM((64, nl * n), F32),
                        pltpu.SemaphoreType.DMA((7,)), pltpu.SemaphoreType.DMA((7,)),
                        pltpu.SemaphoreType.DMA((3,)), pltpu.SemaphoreType.DMA((3,))],
        compiler_params=pltpu.CompilerParams(vmem_limit_bytes=V7X_VMEM_LIMIT_BYTES, has_side_effects=True),
    )(c_row, ada_w)


def _in_hbm(a):
    return pltpu.with_memory_space_constraint(a, pltpu.HBM)


def _gather_start(lands, groups):
    n, ngr = len(lands), len(groups)

    def body(*refs):
        land = refs[:n]
        sems = refs[n:n + 2 * ngr]
        token = refs[-1]
        x, y, c = _place()
        chip = 2 * x + y
        for gi, idxs in enumerate(groups):
            for t, i in enumerate(idxs):
                for j, (fx, fy) in enumerate(OTHER_CHIPS):
                    pltpu.make_async_remote_copy(
                        src_ref=land[i].at[chip], dst_ref=land[i].at[chip],
                        send_sem=sems[2 * gi].at[3 * t + j], recv_sem=sems[2 * gi + 1].at[3 * t + j],
                        device_id=(x ^ fx, y ^ fy, c), device_id_type=MESH).start()
        token[...] = jnp.zeros_like(token)

    sem_shapes = []
    for idxs in groups:
        sem_shapes += [pltpu.SemaphoreType.DMA((3 * len(idxs),))] * 2
    out = pl.pallas_call(
        body, name="weight_gather_start",
        in_specs=[HBM] * n, out_specs=[SEM] * (2 * ngr) + [HBM] * n + [VMEM],
        out_shape=sem_shapes + [pltpu.HBM(a.shape, a.dtype) for a in lands] + [jax.ShapeDtypeStruct((8, 128), F32)],
        input_output_aliases={i: 2 * ngr + i for i in range(n)},
        compiler_params=pltpu.CompilerParams(has_side_effects=DATAFLOW_EFFECT),
    )(*[_in_hbm(a) for a in lands])
    sems = [(out[2 * gi], out[2 * gi + 1]) for gi in range(ngr)]
    return sems, list(out[2 * ngr:2 * ngr + n]), out[-1]


def _gather_wait(lands, sems, after, name):
    n = len(lands)

    def body(*refs):
        land = refs[:n]
        send_sems, recv_sems = refs[n], refs[n + 1]
        x, y, c = _place()
        chip = 2 * x + y
        for t in range(n):
            for j, (fx, fy) in enumerate(OTHER_CHIPS):
                cp = pltpu.make_async_remote_copy(
                    src_ref=land[t].at[chip], dst_ref=land[t].at[2 * (x ^ fx) + (y ^ fy)],
                    send_sem=send_sems.at[3 * t + j], recv_sem=recv_sems.at[3 * t + j],
                    device_id=(x ^ fx, y ^ fy, c), device_id_type=MESH)
                cp.wait_send()
                cp.wait_recv()

    out = pl.pallas_call(
        body, name=name,
        in_specs=[HBM] * n + [SEM, SEM, ANY], out_specs=[HBM] * n,
        out_shape=[pltpu.HBM(a.shape, a.dtype) for a in lands],
        input_output_aliases={i: i for i in range(n)},
        compiler_params=pltpu.CompilerParams(has_side_effects=DATAFLOW_EFFECT),
    )(*lands, sems[0], sems[1], after)
    return list(out)


N_SCATTER = 3
N_BCAST = 2


def _grad_copies(refs, q):
    dwin, dwout, dpw, rwin, rwout, rpw, slabs, gates = refs
    x, y, c = _place()
    me = 4 * x + 2 * y + c
    chip = 2 * x + y
    out = []
    for j, (fx, fy) in enumerate(OTHER_CHIPS):
        peer = (x ^ fx, y ^ fy, c)
        owner = 2 * (x ^ fx) + (y ^ fy)
        out.append((dwin.at[owner], rwin.at[chip], rwin.at[owner], peer, 0 * 3 + j))
        out.append((dwout.at[owner], rwout.at[chip], rwout.at[owner], peer, 1 * 3 + j))
        out.append((dpw.at[:, pl.ds(pl.multiple_of(owner * q, q), q), :], rpw.at[chip], rpw.at[owner], peer, 2 * 3 + j))
    for t, buf in enumerate((slabs, gates)):
        for j, (fx, fy, fc) in enumerate(OTHER_DEVICES):
            them = 4 * (x ^ fx) + 2 * (y ^ fy) + (c ^ fc)
            out.append((buf.at[me], buf.at[me], buf.at[them], (x ^ fx, y ^ fy, c ^ fc), 3 * N_SCATTER + 7 * t + j))
    return out


N_GRAD_SEMS = 3 * N_SCATTER + 7 * N_BCAST


def _grad_start(arrays, q, layer):
    n = len(arrays)

    def body(*refs):
        send_sems, recv_sems = refs[n], refs[n + 1]
        for src, dst, _, peer, k in _grad_copies(refs[:n], q):
            pltpu.make_async_remote_copy(src_ref=src, dst_ref=dst, send_sem=send_sems.at[k], recv_sem=recv_sems.at[k],
                                         device_id=peer, device_id_type=MESH).start()
        refs[-1][...] = jnp.zeros_like(refs[-1])

    out = pl.pallas_call(
        body, name=f"grad_exchange_start_l{layer}",
        in_specs=[HBM] * n, out_specs=[SEM, SEM] + [HBM] * n + [VMEM],
        out_shape=[pltpu.SemaphoreType.DMA((N_GRAD_SEMS,))] * 2 + [pltpu.HBM(a.shape, a.dtype) for a in arrays]
        + [jax.ShapeDtypeStruct((8, 128), F32)],
        input_output_aliases={i: 2 + i for i in range(n)},
        compiler_params=pltpu.CompilerParams(has_side_effects=DATAFLOW_EFFECT),
    )(*[_in_hbm(a) for a in arrays])
    return (out[0], out[1]), list(out[2:2 + n]), out[-1]


def _grad_wait(arrays, sems, q, after, layer):
    n = len(arrays)

    def body(*refs):
        send_sems, recv_sems = refs[n], refs[n + 1]
        for src, _, landed, peer, k in _grad_copies(refs[:n], q):
            cp = pltpu.make_async_remote_copy(src_ref=src, dst_ref=landed, send_sem=send_sems.at[k], recv_sem=recv_sems.at[k],
                                              device_id=peer, device_id_type=MESH)
            cp.wait_send()
            cp.wait_recv()

    out = pl.pallas_call(
        body, name=f"grad_exchange_wait_l{layer}",
        in_specs=[HBM] * n + [SEM, SEM, ANY], out_specs=[HBM] * n,
        out_shape=[pltpu.HBM(a.shape, a.dtype) for a in arrays],
        input_output_aliases={i: i for i in range(n)},
        compiler_params=pltpu.CompilerParams(has_side_effects=DATAFLOW_EFFECT),
    )(*arrays, sems[0], sems[1], after)
    return list(out)


def _sibling_swap(parts, layer):
    n = len(parts)

    def body(*refs):
        srcs, outs = refs[:n], refs[n:2 * n]
        send_sems, recv_sems = refs[2 * n:]
        x, y, c = _place()
        cps = [pltpu.make_async_remote_copy(
            src_ref=srcs[i], dst_ref=outs[i], send_sem=send_sems.at[i], recv_sem=recv_sems.at[i],
            device_id=(x, y, 1 - c), device_id_type=MESH) for i in range(n)]
        for cp in cps:
            cp.start()
        for cp in cps:
            cp.wait()

    return pl.pallas_call(
        body, name=f"sibling_swap_l{layer}", in_specs=[ANY] * n, out_specs=[ANY] * n,
        out_shape=[jax.ShapeDtypeStruct(a.shape, a.dtype) for a in parts],
        scratch_shapes=[pltpu.SemaphoreType.DMA((n,)), pltpu.SemaphoreType.DMA((n,))],
        compiler_params=pltpu.CompilerParams(has_side_effects=True),
    )(*parts)


def _rows8(*rows):
    d = rows[0].shape[-1]
    out = jnp.zeros((8, d), F32)
    for i, r in enumerate(rows):
        out = out.at[i].set(r.reshape(d))
    return out


def kernel(x, c, ada_w, ada_b, pre_norm_g, w_in, conv_w, conv_b, gate_a_w, gate_a_b, gate_x_w, gate_x_b, lru_lambda, pool_w, pool_b, pool_scale, w_out, post_norm_g, loss_target, m_ada_w, m_ada_b, m_pre_norm_g, m_w_in, m_conv_w, m_conv_b, m_gate_a_w, m_gate_a_b, m_gate_x_w, m_gate_x_b, m_lru_lambda, m_pool_w, m_pool_b, m_pool_scale, m_w_out, m_post_norm_g, v_ada_w, v_ada_b, v_pre_norm_g, v_w_in, v_conv_w, v_conv_b, v_gate_a_w, v_gate_a_b, v_gate_x_w, v_gate_x_b, v_lru_lambda, v_pool_w, v_pool_b, v_pool_scale, v_w_out, v_post_norm_g):
    nl, d, n_ada = ada_w.shape
    s = x.shape[1]
    nh, hd = gate_a_w.shape[1], gate_a_w.shape[2]
    ng, gq, gd = pool_w.shape[1], pool_w.shape[2], pool_w.shape[3]
    me = 4 * lax.axis_index("x") + 2 * lax.axis_index("y") + lax.axis_index("c")
    chip = 2 * lax.axis_index("x") + lax.axis_index("y")
    chip_arr = jnp.reshape(chip, (1,)).astype(jnp.int32)
    x0 = x.reshape(s, d)
    target = loss_target.reshape(s, d)
    p_in = w_in.shape[2]
    r_out = w_out.shape[1]

    win = [_into_slot(w_in[l], BF16, chip_arr, f"slot_w_in_l{l}") for l in range(nl)]
    wout = [_into_slot(w_out[l], BF16, chip_arr, f"slot_w_out_l{l}") for l in range(nl)]
    pw = [_into_slot(pool_w[l].reshape(ng * gq, gd), BF16, chip_arr, f"slot_pool_w_l{l}") for l in range(nl)]
    convw = _into_slot(conv_w.reshape(nl * CONV_WIDTH, d // 4), F32, chip_arr, "slot_conv_w")
    poolb = _into_slot(pool_b.reshape(nl * ng, gq), F32, chip_arr, "slot_pool_b")
    lands = [win[0], convw, poolb, *pw, wout[0]]
    groups = [[0], list(range(1, len(lands)))]
    for l in range(1, nl):
        groups.append([len(lands), len(lands) + 1])
        lands += [win[l], wout[l]]
    sems, lands, token = _gather_start(lands, groups)

    cbuf, modbuf = _mod_exchange(c.reshape(1, d) + token[0:1, 0:1], ada_w)
    mod = modbuf[:, 0, :].reshape(4, nl, n_ada).transpose(1, 0, 2).reshape(nl, 4 * n_ada) + ada_b
    wa_b, wx_b = gate_a_w.astype(BF16), gate_x_w.astype(BF16)
    vecs, rvecs = [], []
    for l in range(nl):
        vecs.append(_rows8(mod[l, 0:d], mod[l, d:2 * d], mod[l, 2 * d:3 * d], pre_norm_g[l], post_norm_g[l]))
        rvecs.append(_rows8(conv_b[l], gate_a_b[l], gate_x_b[l], lru_lambda[l]))

    xs, projs, hss, ycats, ys = [x0], [], [], [], []
    sq = None
    convw_full = poolw_full = pvecs = None
    for l in range(nl):
        if l == 0:
            (win[0],) = _gather_wait([lands[0]], sems[0], modbuf, "weight_gather_wait_a")
        proj = _inproj_fwd(xs[l], vecs[l], win[l], l)
        if l == 0:
            got = _gather_wait(lands[1:len(groups[1]) + 1], sems[1], proj, "weight_gather_wait_b")
            convw_g, poolb_g, pw, wout[0] = got[0], got[1], got[2:2 + nl], got[2 + nl]
            convw_full = convw_g.reshape(4, nl, CONV_WIDTH, d // 4).transpose(1, 2, 0, 3).reshape(nl, CONV_WIDTH, d)
            poolb_full = poolb_g.reshape(4, nl, ng, gq).transpose(1, 2, 0, 3).reshape(nl, d)
            poolw_full = [a.reshape(4, ng, gq, gd).transpose(1, 0, 2, 3).reshape(ng, gd, gd) for a in pw]
            pvecs = [_rows8(poolb_full[k], pool_scale[k]) for k in range(nl)]
        ycat, hs = _rnn_fwd(proj, convw_full[l], rvecs[l], wa_b[l], wx_b[l], l)
        if l + 1 < nl:
            base = len(groups[1]) + 1 + 2 * l
            win[l + 1], wout[l + 1] = _gather_wait(lands[base:base + 2], sems[2 + l], hs, f"weight_gather_wait_c{l + 1}")
        ycat = _pool_fwd(proj, ycat, poolw_full[l], pvecs[l], l)
        y, xo, sq = _outproj_fwd(ycat, wout[l], xs[l], vecs[l], target if l == nl - 1 else None, l)
        projs.append(proj), hss.append(hs), ycats.append(ycat), ys.append(y), xs.append(xo)
    loss = lax.psum(sq[0, 0] * (0.5 / d), ("x", "y", "c"))

    gate_rows = nh * hd * hd // d

    def finish(l, flight, after, prev):
        sems_l, arrays = flight
        dwin_l, dwout_l, dpw_l, rwin, rwout, rpw, slabs, gates = _grad_wait(arrays, sems_l, gq, after, l)
        p_win = _sum_owner(dwin_l, rwin, chip_arr, lambda tr: (None, tr, p_in),
                           lambda i, chip: (chip[0], i, 0), f"sum_w_in_l{l}")
        p_wout = _sum_owner(dwout_l, rwout, chip_arr, lambda tr: (None, tr, d),
                            lambda i, chip: (chip[0], i, 0), f"sum_w_out_l{l}")
        p_pw = _sum_owner(dpw_l, rpw, chip_arr, lambda tr: (ng, tr, gd),
                          lambda i, chip: (0, chip[0], 0), f"sum_pool_w_l{l}")
        q_win, q_wout, q_pw = _sibling_swap([p_win, p_wout, p_pw], l)
        prev = prev or {}
        big = {
            "w_in": _adamw_layer(w_in, m_w_in, v_w_in, [p_win, q_win], l, prev.get("w_in"), f"adamw_w_in_l{l}"),
            "w_out": _adamw_layer(w_out, m_w_out, v_w_out, [p_wout, q_wout], l, prev.get("w_out"), f"adamw_w_out_l{l}"),
            "pool_w": _adamw_layer(pool_w, m_pool_w, v_pool_w, [p_pw, q_pw], l, prev.get("pool_w"), f"adamw_pool_w_l{l}"),
        }
        return big, slabs, _sum_slots(slabs, f"sum_slab_l{l}"), _sum_slots(gates, f"sum_gates_l{l}")

    dx = xs[nl]
    flight = token = big = None
    slabs_all, totals, gate_totals = [None] * nl, [None] * nl, [None] * nl
    for l in reversed(range(nl)):
        vec_l = vecs[l] if token is None else vecs[l] + token[0:1, 0:1]
        dycat, dwout_l, dvec_o = _outproj_bwd(dx, ys[l], ycats[l], wout[l], vec_l, l)
        dproj, dwa, dwx, dvec_r = _rnn_bwd(projs[l], hss[l], dycat, convw_full[l], rvecs[l], wa_b[l], wx_b[l], l)
        dproj, dpw_l, dvec_p = _pool_bwd(projs[l], dycat, dproj, poolw_full[l], pvecs[l], l)
        dx, dvec_i = _inproj_bwd_x(dproj, win[l], xs[l], dx, vec_l, l)
        dwin_l = _inproj_bwd_w(dproj, xs[l], vec_l, l)
        slab = jnp.concatenate([dvec_i, dvec_o, dvec_r, dvec_p], axis=0)
        gates = jnp.concatenate([dwa.reshape(gate_rows, d), dwx.reshape(gate_rows, d)], axis=0)
        slabs = lax.dynamic_update_slice(lax.empty((8, *slab.shape), F32), slab[None], (me, 0, 0))
        gatess = lax.dynamic_update_slice(lax.empty((8, *gates.shape), BF16), gates[None], (me, 0, 0))
        arrays = [dwin_l, dwout_l, dpw_l, lax.empty(dwin_l.shape, BF16), lax.empty(dwout_l.shape, BF16),
                  lax.empty((4, ng, gq, gd), BF16), slabs, gatess]
        sems_l, arrays, token_l = _grad_start(arrays, gq, l)
        if flight is not None:
            big, slabs_all[l + 1], totals[l + 1], gate_totals[l + 1] = finish(l + 1, flight, token_l, big)
        flight, token = (sems_l, arrays), token_l
    big, slabs_all[0], totals[0], gate_totals[0] = finish(0, flight, big["w_in"][3] if big else dx, big)
    grad_x = dx.reshape(x.shape)

    def stack(r):
        return jnp.stack([totals[l][r] for l in range(nl)])

    g_dmod = jnp.stack([jnp.concatenate([totals[l][0], totals[l][1], totals[l][8]]) for l in range(nl)])
    g_pre, g_post = stack(2), stack(9)
    g_convb, g_ba, g_bx, g_lam = stack(16), stack(17), stack(18), stack(19)
    g_convw = jnp.stack([totals[l][20:20 + CONV_WIDTH] for l in range(nl)])
    g_poolb, g_pscale = stack(32), stack(33)
    g_wa = jnp.stack([gate_totals[l][0:gate_rows].reshape(nh, hd, hd) for l in range(nl)])
    g_wx = jnp.stack([gate_totals[l][gate_rows:2 * gate_rows].reshape(nh, hd, hd) for l in range(nl)])

    per_dev = jnp.stack([jnp.concatenate([slabs_all[l][:, r] for r in (0, 1, 8)], axis=-1) for l in range(nl)])
    dmod_mine = lax.dynamic_slice_in_dim(per_dev, chip * n_ada, n_ada, axis=2)
    c_all_t = cbuf.reshape(8, 8, d)[:, 0, :].T
    g_ada_w = _ada_w_grad(c_all_t, dmod_mine)

    def my_cols(full, width):
        return lax.dynamic_slice_in_dim(full, chip * width, width, axis=full.ndim - 1)

    g_convw_mine = my_cols(g_convw, d // 4)
    g_poolb_mine = my_cols(g_poolb.reshape(nl, ng, gd), gq)

    results = {
        "ada_w": _adamw(ada_w, m_ada_w, v_ada_w, [g_ada_w], "adamw_ada_w"),
        "ada_b": _adamw(ada_b, m_ada_b, v_ada_b, [g_dmod], "adamw_ada_b"),
        "pre_norm_g": _adamw(pre_norm_g, m_pre_norm_g, v_pre_norm_g, [g_pre], "adamw_pre_norm_g"),
        "w_in": tuple(o.reshape(w_in.shape) for o in big["w_in"]),
        "conv_w": _adamw(conv_w, m_conv_w, v_conv_w, [g_convw_mine], "adamw_conv_w"),
        "conv_b": _adamw(conv_b, m_conv_b, v_conv_b, [g_convb], "adamw_conv_b"),
        "gate_a_w": _adamw(gate_a_w, m_gate_a_w, v_gate_a_w, [g_wa], "adamw_gate_a_w"),
        "gate_a_b": _adamw(gate_a_b, m_gate_a_b, v_gate_a_b, [g_ba.reshape(gate_a_b.shape)], "adamw_gate_a_b"),
        "gate_x_w": _adamw(gate_x_w, m_gate_x_w, v_gate_x_w, [g_wx], "adamw_gate_x_w"),
        "gate_x_b": _adamw(gate_x_b, m_gate_x_b, v_gate_x_b, [g_bx.reshape(gate_x_b.shape)], "adamw_gate_x_b"),
        "lru_lambda": _adamw(lru_lambda, m_lru_lambda, v_lru_lambda, [g_lam], "adamw_lru_lambda"),
        "pool_w": tuple(o.reshape(pool_w.shape) for o in big["pool_w"]),
        "pool_b": _adamw(pool_b, m_pool_b, v_pool_b, [g_poolb_mine], "adamw_pool_b"),
        "pool_scale": _adamw(pool_scale, m_pool_scale, v_pool_scale, [g_pscale], "adamw_pool_scale"),
        "w_out": tuple(o.reshape(w_out.shape) for o in big["w_out"]),
        "post_norm_g": _adamw(post_norm_g, m_post_norm_g, v_post_norm_g, [g_post], "adamw_post_norm_g"),
    }
    names = list(results)
    return (loss, grad_x,
            *[results[n][0] for n in names], *[results[n][1] for n in names],
            *[results[n][2] for n in names], *[results[n][3] for n in names])
```

```python
import functools

import jax
import jax.numpy as jnp
from jax import lax
from jax.experimental import pallas as pl
from jax.experimental.pallas import tpu as pltpu

F32 = jnp.float32
BF16 = jnp.bfloat16

NORM_EPS = 1e-6
LRU_C = 8.0
CONV_WIDTH = 4
MAX_POOL_WINDOW = 16
HALO = 16
ADAM_LR = 0.001
ADAM_B1 = 0.9
ADAM_B2 = 0.999
ADAM_EPS = 1e-08
ADAM_WD = 0.01
ADAM_STEP = 10

V7X_VMEM_LIMIT_BYTES = 56 * 1024 * 1024
MATMUL_ROWS = 512
SCAN_ROWS = 512
ELEMENTWISE_ROWS = 512

MESH = pl.DeviceIdType.MESH
ANY = pl.BlockSpec(memory_space=pl.ANY)
VMEM = pl.BlockSpec(memory_space=pltpu.VMEM)
HBM = pl.BlockSpec(memory_space=pltpu.HBM)
SEM = pl.BlockSpec(memory_space=pltpu.SEMAPHORE)
DATAFLOW_EFFECT = pltpu.SideEffectType.DATAFLOW_SIDE_EFFECTING

NT_DIMS = (((1,), (1,)), ((), ()))
TN_DIMS = (((0,), (0,)), ((), ()))


def _params(n_grid_axes):
    return pltpu.CompilerParams(dimension_semantics=("arbitrary",) * n_grid_axes,
                                vmem_limit_bytes=V7X_VMEM_LIMIT_BYTES)


def _tile(total, want):
    t = min(want, max(total // 2, HALO))
    assert total % t == 0 and t % HALO == 0, (total, t)
    return t


def _row_tile(rows):
    for t in range(min(rows, ELEMENTWISE_ROWS) // 8 * 8, 0, -8):
        if rows % t == 0:
            return t
    return rows


def _sigmoid(z):
    return 1.0 / (1.0 + jnp.exp(-z))


def _softplus(z):
    return jnp.maximum(z, 0.0) + jnp.log(1.0 + jnp.exp(-jnp.abs(z)))


def _neg_expm1(z):
    return -jnp.tanh(0.5 * z) * (jnp.exp(z) + 1.0)


def _colsum(v):
    return jnp.sum(v, axis=0, keepdims=True)


def _prenorm(xt, vec_ref):
    rs = lax.rsqrt(jnp.mean(xt * xt, axis=-1, keepdims=True) + NORM_EPS)
    xn = xt * rs
    h = xn * vec_ref[3:4, :] * (1.0 + vec_ref[1:2, :]) + vec_ref[0:1, :]
    return h, xn, rs


def _shift_down(v, d, fill):
    t = v.shape[0]
    if d % 8 == 0:
        return jnp.concatenate([jnp.full((d, v.shape[1]), fill, v.dtype), v[:t - d]], axis=0)
    row = lax.broadcasted_iota(jnp.int32, v.shape, 0)
    return jnp.where(row >= d, pltpu.roll(v, d, 0), fill)


def _shift_up(v, d, fill):
    t = v.shape[0]
    if d % 8 == 0:
        return jnp.concatenate([v[d:], jnp.full((d, v.shape[1]), fill, v.dtype)], axis=0)
    row = lax.broadcasted_iota(jnp.int32, v.shape, 0)
    return jnp.where(row < t - d, pltpu.roll(v, t - d, 0), fill)


def _scan_fwd(a, v, h_before):
    d = 1
    while d < a.shape[0]:
        v = v + a * _shift_down(v, d, 0.0)
        a = a * _shift_down(a, d, 1.0)
        d *= 2
    return a * h_before + v


def _scan_rev(b, v):
    d = 1
    while d < b.shape[0]:
        v = v + b * _shift_up(v, d, 0.0)
        b = b * _shift_up(b, d, 0.0)
        d *= 2
    return v


def _inproj_fwd(x, vec, w_all, layer):
    s, d = x.shape
    p = w_all.shape[2]
    ts = _tile(s, MATMUL_ROWS)

    def body(x_ref, vec_ref, w_ref, proj_ref):
        h, _, _ = _prenorm(x_ref[...], vec_ref)
        hb = h.astype(BF16)
        for k in range(4):
            proj_ref[k] = jnp.dot(hb, w_ref[k], preferred_element_type=F32)

    return pl.pallas_call(
        body, name=f"inproj_fwd_l{layer}", grid=(s // ts,),
        in_specs=[pl.BlockSpec((ts, d), lambda i: (i, 0)),
                  pl.BlockSpec((8, d), lambda i: (0, 0)),
                  pl.BlockSpec((4, d, p), lambda i: (0, 0, 0))],
        out_specs=pl.BlockSpec((4, ts, p), lambda i: (0, i, 0)),
        out_shape=jax.ShapeDtypeStruct((4, s, p), F32),
        compiler_params=_params(1),
    )(x, vec, w_all)


def _rnn_gates(u, wa_ref, wx_ref, vec_ref):
    ub = u.astype(BF16)
    r = _sigmoid(jnp.dot(ub, wa_ref[0], preferred_element_type=F32) + vec_ref[1:2, :])
    ig = _sigmoid(jnp.dot(ub, wx_ref[0], preferred_element_type=F32) + vec_ref[2:3, :])
    sp = _softplus(-vec_ref[3:4, :])
    log_a = (-LRU_C) * r * sp
    return ub, r, ig, sp, log_a


def _conv(xbuf, cw_ref, vec_ref, ts):
    u = vec_ref[0:1, :] + cw_ref[CONV_WIDTH - 1:CONV_WIDTH, :] * xbuf[pl.ds(HALO, ts), :]
    for k in range(CONV_WIDTH - 1):
        u = u + cw_ref[k:k + 1, :] * xbuf[pl.ds(HALO - (CONV_WIDTH - 1) + k, ts), :]
    return u


def _rnn_fwd(proj, cw, vec, wa, wx, layer):
    _, s, d = proj.shape
    nh, hd, _ = wa.shape
    ts = _tile(s, SCAN_ROWS)

    def body(proj_ref, cw_ref, vec_ref, wa_ref, wx_ref, ycat_ref, hs_ref, xbuf, hlast):
        i = pl.program_id(1)

        @pl.when(i == 0)
        def _():
            xbuf[0:HALO, :] = jnp.zeros((HALO, hd), F32)
            hlast[...] = jnp.zeros_like(hlast)

        xbuf[pl.ds(HALO, ts), :] = proj_ref[0]
        u = _conv(xbuf, cw_ref, vec_ref, ts)
        _, _, ig, _, log_a = _rnn_gates(u, wa_ref, wx_ref, vec_ref)
        a = jnp.exp(log_a)
        mult = jnp.sqrt(_neg_expm1(2.0 * log_a))
        hs = _scan_fwd(a, mult * (ig * u), hlast[0:1, :])
        hs_ref[...] = hs
        hlast[0:1, :] = hs_ref[ts - 1:ts, :]
        g = proj_ref[1]
        ycat_ref[...] = (hs * (g * _sigmoid(g))).astype(BF16)
        xbuf[0:HALO, :] = xbuf[pl.ds(ts, HALO), :]

    return pl.pallas_call(
        body, name=f"rnn_fwd_l{layer}", grid=(nh, s // ts),
        in_specs=[pl.BlockSpec((2, ts, hd), lambda h, i: (0, i, h)),
                  pl.BlockSpec((CONV_WIDTH, hd), lambda h, i: (0, h)),
                  pl.BlockSpec((8, hd), lambda h, i: (0, h)),
                  pl.BlockSpec((1, hd, hd), lambda h, i: (h, 0, 0)),
                  pl.BlockSpec((1, hd, hd), lambda h, i: (h, 0, 0))],
        out_specs=[pl.BlockSpec((ts, hd), lambda h, i: (i, h)),
                   pl.BlockSpec((ts, hd), lambda h, i: (i, h))],
        out_shape=[jax.ShapeDtypeStruct((s, 2 * d), BF16), jax.ShapeDtypeStruct((s, d), F32)],
        scratch_shapes=[pltpu.VMEM((ts + HALO, hd), F32), pltpu.VMEM((8, hd), F32)],
        compiler_params=_params(2),
    )(proj, cw, vec, wa, wx)


def _inv_count(i, ts, lanes, win):
    t = i * ts + lax.broadcasted_iota(jnp.int32, (ts, lanes), 0)
    return 1.0 / jnp.minimum(t + 1, win).astype(F32)


def _pooled(xbuf, xt, lanes, win, inv_cnt, ts):
    acc = xt
    for dlt in range(1, win):
        acc = acc + xbuf[pl.ds(HALO - dlt, ts), lanes]
    return acc * inv_cnt - xt


def _pool_fwd(proj, ycat, pw, vec, layer):
    _, s, d = proj.shape
    ng, gd, _ = pw.shape
    ts = _tile(s, MATMUL_ROWS)

    def body(proj_ref, ycat_in, pw_ref, vec_ref, ycat_ref, xbuf):
        del ycat_in
        i = pl.program_id(0)

        @pl.when(i == 0)
        def _():
            xbuf[0:HALO, :] = jnp.zeros((HALO, d), F32)

        xbuf[pl.ds(HALO, ts), :] = proj_ref[0]
        for g in range(ng):
            lanes = slice(g * gd, (g + 1) * gd)
            win = 2 << g
            xt = proj_ref[0, :, lanes]
            pooled = _pooled(xbuf, xt, lanes, win, _inv_count(i, ts, gd, win), ts).astype(BF16)
            z = jnp.dot(pooled, pw_ref[g], preferred_element_type=F32) + vec_ref[0:1, lanes]
            gg = proj_ref[1, :, lanes]
            ycat_ref[:, lanes] = (z * vec_ref[1:2, lanes] * (gg * _sigmoid(gg))).astype(BF16)
        xbuf[0:HALO, :] = xbuf[pl.ds(ts, HALO), :]

    return pl.pallas_call(
        body, name=f"pool_fwd_l{layer}", grid=(s // ts,),
        in_specs=[pl.BlockSpec((2, ts, d), lambda i: (1, i, 0)),
                  ANY,
                  pl.BlockSpec((ng, gd, gd), lambda i: (0, 0, 0)),
                  pl.BlockSpec((8, d), lambda i: (0, 0))],
        out_specs=pl.BlockSpec((ts, d), lambda i: (i, 1)),
        out_shape=jax.ShapeDtypeStruct((s, 2 * d), BF16),
        input_output_aliases={1: 0},
        scratch_shapes=[pltpu.VMEM((ts + HALO, d), F32)],
        compiler_params=_params(1),
    )(proj, ycat, pw, vec)


def _outproj_fwd(ycat, w_all, x, vec, target, layer):
    s, d = x.shape
    nk, kd = w_all.shape[0], w_all.shape[1]
    ts = _tile(s, MATMUL_ROWS)
    last = target is not None

    def body(*refs):
        if last:
            ycat_ref, w_ref, x_ref, vec_ref, tgt_ref, y_ref, xo_ref, sq_ref = refs
        else:
            ycat_ref, w_ref, x_ref, vec_ref, y_ref, xo_ref = refs
        y = jnp.dot(ycat_ref[:, 0:kd], w_ref[0], preferred_element_type=F32)
        for k in range(1, nk):
            y = y + jnp.dot(ycat_ref[:, k * kd:(k + 1) * kd], w_ref[k], preferred_element_type=F32)
        y_ref[...] = y
        rs = lax.rsqrt(jnp.mean(y * y, axis=-1, keepdims=True) + NORM_EPS)
        xo = x_ref[...] + vec_ref[2:3, :] * (y * rs * vec_ref[4:5, :])
        if last:
            err = xo - tgt_ref[...]
            xo_ref[...] = err * (1.0 / d)

            @pl.when(pl.program_id(0) == 0)
            def _():
                sq_ref[...] = jnp.zeros_like(sq_ref)

            sq_ref[...] += jnp.sum(err * err)
        else:
            xo_ref[...] = xo

    row = pl.BlockSpec((ts, d), lambda i: (i, 0))
    in_specs = [pl.BlockSpec((ts, nk * kd), lambda i: (i, 0)),
                pl.BlockSpec((nk, kd, d), lambda i: (0, 0, 0)),
                row, pl.BlockSpec((8, d), lambda i: (0, 0))]
    out_specs = [row, row]
    out_shape = [jax.ShapeDtypeStruct((s, d), F32), jax.ShapeDtypeStruct((s, d), F32)]
    args = [ycat, w_all, x, vec]
    if last:
        in_specs.append(row)
        args.append(target)
        out_specs.append(pl.BlockSpec((8, 128), lambda i: (0, 0)))
        out_shape.append(jax.ShapeDtypeStruct((8, 128), F32))
    out = pl.pallas_call(
        body, name=f"outproj_fwd_l{layer}", grid=(s // ts,),
        in_specs=in_specs, out_specs=out_specs, out_shape=out_shape,
        compiler_params=_params(1),
    )(*args)
    return (out[0], out[1], out[2]) if last else (out[0], out[1], None)


def _outproj_bwd(dxo, y, ycat, w_all, vec, layer):
    s, d = dxo.shape
    nk, kd = w_all.shape[0], w_all.shape[1]
    ts = _tile(s, MATMUL_ROWS)
    nt = s // ts

    def body(dxo_ref, y_ref, ycat_ref, w_ref, vec_ref, dycat_ref, dw_ref, dvec_ref, acc):
        i = pl.program_id(0)

        @pl.when(i == 0)
        def _():
            acc[...] = jnp.zeros_like(acc)
            dvec_ref[...] = jnp.zeros_like(dvec_ref)

        yt = y_ref[...]
        rs = lax.rsqrt(jnp.mean(yt * yt, axis=-1, keepdims=True) + NORM_EPS)
        yhat = yt * rs
        gate, gpost = vec_ref[2:3, :], vec_ref[4:5, :]
        dxo_t = dxo_ref[...]
        dyn = dxo_t * gate
        dvec_ref[0:1, :] += _colsum(dxo_t * (yhat * gpost))
        dvec_ref[1:2, :] += _colsum(dyn * yhat)
        t = dyn * gpost
        dy = (rs * (t - yhat * jnp.mean(t * yhat, axis=-1, keepdims=True))).astype(BF16)
        for k in range(nk):
            cols = slice(k * kd, (k + 1) * kd)
            dycat_ref[:, cols] = lax.dot_general(dy, w_ref[k], NT_DIMS, preferred_element_type=F32)
            acc[k] += lax.dot_general(ycat_ref[:, cols], dy, TN_DIMS, preferred_element_type=F32)

        @pl.when(i == nt - 1)
        def _():
            dw_ref[...] = acc[...].astype(BF16)

    row = pl.BlockSpec((ts, d), lambda i: (i, 0))
    wide = pl.BlockSpec((ts, nk * kd), lambda i: (i, 0))
    return pl.pallas_call(
        body, name=f"outproj_bwd_l{layer}", grid=(nt,),
        in_specs=[row, row, wide,
                  pl.BlockSpec((nk, kd, d), lambda i: (0, 0, 0)),
                  pl.BlockSpec((8, d), lambda i: (0, 0))],
        out_specs=[wide,
                   pl.BlockSpec((nk, kd, d), lambda i: (0, 0, 0)),
                   pl.BlockSpec((8, d), lambda i: (0, 0))],
        out_shape=[jax.ShapeDtypeStruct((s, nk * kd), F32),
                   jax.ShapeDtypeStruct((nk, kd, d), BF16),
                   jax.ShapeDtypeStruct((8, d), F32)],
        scratch_shapes=[pltpu.VMEM((nk, kd, d), F32)],
        compiler_params=_params(1),
    )(dxo, y, ycat, w_all, vec)


def _halo_index(ts, nt):
    return lambda j: jnp.maximum((nt - 1 - j) * (ts // HALO) - 1, 0)


def _rnn_bwd(proj, hs, dycat, cw, vec, wa, wx, layer):
    _, s, d = proj.shape
    nh, hd, _ = wa.shape
    ts = _tile(s, SCAN_ROWS)
    nt = s // ts
    halo = _halo_index(ts, nt)

    def body(proj_ref, xh_ref, hs_ref, hsh_ref, dy_ref, cw_ref, vec_ref, wa_ref, wx_ref,
             dproj_ref, dwa_ref, dwx_ref, dvec_ref, xbuf, hbuf, dubuf, carry, dwa_acc, dwx_acc):
        j = pl.program_id(1)
        first_tile = j == nt - 1

        @pl.when(j == 0)
        def _():
            dubuf[pl.ds(ts, HALO), :] = jnp.zeros((HALO, hd), F32)
            carry[...] = jnp.zeros_like(carry)
            dwa_acc[...] = jnp.zeros_like(dwa_acc)
            dwx_acc[...] = jnp.zeros_like(dwx_acc)
            dvec_ref[...] = jnp.zeros_like(dvec_ref)

        xbuf[0:HALO, :] = jnp.where(first_tile, 0.0, xh_ref[0])
        xbuf[pl.ds(HALO, ts), :] = proj_ref[0]
        hbuf[0:HALO, :] = jnp.where(first_tile, 0.0, hsh_ref[...])
        hs = hs_ref[...]
        hbuf[pl.ds(HALO, ts), :] = hs

        u = _conv(xbuf, cw_ref, vec_ref, ts)
        ub, r, ig, sp, log_a = _rnn_gates(u, wa_ref, wx_ref, vec_ref)
        a = jnp.exp(log_a)
        e2 = jnp.exp(2.0 * log_a)
        mult = jnp.sqrt(_neg_expm1(2.0 * log_a))

        g = proj_ref[1]
        sg = _sigmoid(g)
        dyc = dy_ref[...]
        dproj_ref[1] = (dyc * hs * (sg * (1.0 + g * (1.0 - sg)))).astype(BF16)

        row = lax.broadcasted_iota(jnp.int32, (ts, hd), 0)
        dhs = dyc * (g * sg) + jnp.where(row == ts - 1, carry[0:1, :], 0.0)
        dh = _scan_rev(_shift_up(a, 1, 0.0), dhs)
        carry[...] = (a * dh)[0:8, :]

        h_prev = hbuf[pl.ds(HALO - 1, ts), :]
        dlog_a = dh * h_prev * a - dh * (ig * u) * (e2 / mult)
        di = dh * mult * u
        dzr = dlog_a * ((-LRU_C) * sp) * (r * (1.0 - r))
        dzi = di * (ig * (1.0 - ig))
        dvec_ref[3:4, :] += _colsum(dlog_a * r) * (LRU_C * _sigmoid(-vec_ref[3:4, :]))
        dvec_ref[1:2, :] += _colsum(dzr)
        dvec_ref[2:3, :] += _colsum(dzi)
        dzr_b, dzi_b = dzr.astype(BF16), dzi.astype(BF16)
        dwa_acc[...] += lax.dot_general(ub, dzr_b, TN_DIMS, preferred_element_type=F32)
        dwx_acc[...] += lax.dot_general(ub, dzi_b, TN_DIMS, preferred_element_type=F32)
        du = (dh * mult * ig
              + lax.dot_general(dzr_b, wa_ref[0], NT_DIMS, preferred_element_type=F32)
              + lax.dot_general(dzi_b, wx_ref[0], NT_DIMS, preferred_element_type=F32))
        dvec_ref[0:1, :] += _colsum(du)
        for k in range(CONV_WIDTH):
            dvec_ref[4 + k:5 + k, :] += _colsum(du * xbuf[pl.ds(HALO - (CONV_WIDTH - 1) + k, ts), :])

        dubuf[0:ts, :] = du
        dx = cw_ref[CONV_WIDTH - 1:CONV_WIDTH, :] * du
        for k in range(CONV_WIDTH - 1):
            dx = dx + cw_ref[k:k + 1, :] * dubuf[pl.ds(CONV_WIDTH - 1 - k, ts), :]
        dproj_ref[0] = dx.astype(BF16)
        dubuf[pl.ds(ts, HALO), :] = dubuf[0:HALO, :]

        @pl.when(first_tile)
        def _():
            dwa_ref[0] = dwa_acc[...].astype(BF16)
            dwx_ref[0] = dwx_acc[...].astype(BF16)

    rev = lambda h, j: (nt - 1 - j, h)
    return pl.pallas_call(
        body, name=f"rnn_bwd_l{layer}", grid=(nh, nt),
        in_specs=[pl.BlockSpec((2, ts, hd), lambda h, j: (0, nt - 1 - j, h)),
                  pl.BlockSpec((1, HALO, hd), lambda h, j: (0, halo(j), h)),
                  pl.BlockSpec((ts, hd), rev),
                  pl.BlockSpec((HALO, hd), lambda h, j: (halo(j), h)),
                  pl.BlockSpec((ts, hd), rev),
                  pl.BlockSpec((CONV_WIDTH, hd), lambda h, j: (0, h)),
                  pl.BlockSpec((8, hd), lambda h, j: (0, h)),
                  pl.BlockSpec((1, hd, hd), lambda h, j: (h, 0, 0)),
                  pl.BlockSpec((1, hd, hd), lambda h, j: (h, 0, 0))],
        out_specs=[pl.BlockSpec((2, ts, hd), lambda h, j: (0, nt - 1 - j, h)),
                   pl.BlockSpec((1, hd, hd), lambda h, j: (h, 0, 0)),
                   pl.BlockSpec((1, hd, hd), lambda h, j: (h, 0, 0)),
                   pl.BlockSpec((16, hd), lambda h, j: (0, h))],
        out_shape=[jax.ShapeDtypeStruct((4, s, d), BF16),
                   jax.ShapeDtypeStruct((nh, hd, hd), BF16),
                   jax.ShapeDtypeStruct((nh, hd, hd), BF16),
                   jax.ShapeDtypeStruct((16, d), F32)],
        scratch_shapes=[pltpu.VMEM((ts + HALO, hd), F32), pltpu.VMEM((ts + HALO, hd), F32),
                        pltpu.VMEM((ts + HALO, hd), F32), pltpu.VMEM((8, hd), F32),
                        pltpu.VMEM((hd, hd), F32), pltpu.VMEM((hd, hd), F32)],
        compiler_params=_params(2),
    )(proj, proj, hs, hs, dycat, cw, vec, wa, wx)


def _pool_bwd(proj, dycat, dproj, pw, vec, layer):
    _, s, d = proj.shape
    ng, gd, _ = pw.shape
    ts = _tile(s, MATMUL_ROWS)
    nt = s // ts
    halo = _halo_index(ts, nt)

    def body(proj_ref, xh_ref, dy_ref, dproj_in, pw_ref, vec_ref, dproj_ref, dpw_ref, dvec_ref, xbuf, qbuf, acc):
        del dproj_in
        j = pl.program_id(0)
        i = nt - 1 - j

        @pl.when(j == 0)
        def _():
            qbuf[pl.ds(ts, HALO), :] = jnp.zeros((HALO, d), F32)
            acc[...] = jnp.zeros_like(acc)
            dvec_ref[...] = jnp.zeros_like(dvec_ref)

        xbuf[0:HALO, :] = jnp.where(i == 0, 0.0, xh_ref[0])
        xbuf[pl.ds(HALO, ts), :] = proj_ref[0]
        for g in range(ng):
            lanes = slice(g * gd, (g + 1) * gd)
            win = 2 << g
            xt = proj_ref[0, :, lanes]
            inv_cnt = _inv_count(i, ts, gd, win)
            pooled = _pooled(xbuf, xt, lanes, win, inv_cnt, ts).astype(BF16)
            z = jnp.dot(pooled, pw_ref[g], preferred_element_type=F32) + vec_ref[0:1, lanes]
            scale = vec_ref[1:2, lanes]
            gg = proj_ref[1, :, lanes]
            sg = _sigmoid(gg)
            dyc = dy_ref[:, lanes]
            dyp = dyc * (gg * sg)
            dproj_ref[1, :, lanes] = (dyc * (z * scale) * (sg * (1.0 + gg * (1.0 - sg)))).astype(BF16)
            dvec_ref[1:2, lanes] += _colsum(dyp * z)
            dz = dyp * scale
            dvec_ref[0:1, lanes] += _colsum(dz)
            dz_b = dz.astype(BF16)
            acc[g] += lax.dot_general(pooled, dz_b, TN_DIMS, preferred_element_type=F32)
            dpooled = lax.dot_general(dz_b, pw_ref[g], NT_DIMS, preferred_element_type=F32)

            q = dpooled * inv_cnt
            qbuf[0:ts, lanes] = q
            dx = q - dpooled
            for dlt in range(1, win):
                dx = dx + qbuf[pl.ds(dlt, ts), lanes]
            dproj_ref[0, :, lanes] = dx.astype(BF16)
        qbuf[pl.ds(ts, HALO), :] = qbuf[0:HALO, :]

        @pl.when(j == nt - 1)
        def _():
            dpw_ref[...] = acc[...].astype(BF16)

    return pl.pallas_call(
        body, name=f"pool_bwd_l{layer}", grid=(nt,),
        in_specs=[pl.BlockSpec((2, ts, d), lambda j: (1, nt - 1 - j, 0)),
                  pl.BlockSpec((1, HALO, d), lambda j: (2, halo(j), 0)),
                  pl.BlockSpec((ts, d), lambda j: (nt - 1 - j, 1)),
                  ANY,
                  pl.BlockSpec((ng, gd, gd), lambda j: (0, 0, 0)),
                  pl.BlockSpec((8, d), lambda j: (0, 0))],
        out_specs=[pl.BlockSpec((2, ts, d), lambda j: (1, nt - 1 - j, 0)),
                   pl.BlockSpec((ng, gd, gd), lambda j: (0, 0, 0)),
                   pl.BlockSpec((8, d), lambda j: (0, 0))],
        out_shape=[jax.ShapeDtypeStruct((4, s, d), BF16),
                   jax.ShapeDtypeStruct((ng, gd, gd), BF16),
                   jax.ShapeDtypeStruct((8, d), F32)],
        input_output_aliases={3: 0},
        scratch_shapes=[pltpu.VMEM((ts + HALO, d), F32), pltpu.VMEM((ts + HALO, d), F32),
                        pltpu.VMEM((ng, gd, gd), F32)],
        compiler_params=_params(1),
    )(proj, proj, dycat, dproj, pw, vec)


def _inproj_bwd_x(dproj, w_all, x, dxo, vec, layer):
    s, d = x.shape
    p = w_all.shape[2]
    ts = _tile(s, MATMUL_ROWS)

    def body(dp_ref, w_ref, x_ref, dxo_ref, vec_ref, dx_ref, dvec_ref):
        @pl.when(pl.program_id(0) == 0)
        def _():
            dvec_ref[...] = jnp.zeros_like(dvec_ref)

        dh = lax.dot_general(dp_ref[0], w_ref[0], NT_DIMS, preferred_element_type=F32)
        for k in range(1, 4):
            dh = dh + lax.dot_general(dp_ref[k], w_ref[k], NT_DIMS, preferred_element_type=F32)
        _, xn, rs = _prenorm(x_ref[...], vec_ref)
        gpre, scale1 = vec_ref[3:4, :], 1.0 + vec_ref[1:2, :]
        dvec_ref[0:1, :] += _colsum(dh)
        dvec_ref[1:2, :] += _colsum(dh * (xn * gpre))
        dvec_ref[2:3, :] += _colsum(dh * (xn * scale1))
        t = dh * (gpre * scale1)
        dx_ref[...] = dxo_ref[...] + rs * (t - xn * jnp.mean(t * xn, axis=-1, keepdims=True))

    row = pl.BlockSpec((ts, d), lambda i: (i, 0))
    return pl.pallas_call(
        body, name=f"inproj_bwd_x_l{layer}", grid=(s // ts,),
        in_specs=[pl.BlockSpec((4, ts, p), lambda i: (0, i, 0)),
                  pl.BlockSpec((4, d, p), lambda i: (0, 0, 0)),
                  row, row, pl.BlockSpec((8, d), lambda i: (0, 0))],
        out_specs=[row, pl.BlockSpec((8, d), lambda i: (0, 0))],
        out_shape=[jax.ShapeDtypeStruct((s, d), F32), jax.ShapeDtypeStruct((8, d), F32)],
        compiler_params=_params(1),
    )(dproj, w_all, x, dxo, vec)


def _inproj_bwd_w(dproj, x, vec, layer):
    s, d = x.shape
    p = dproj.shape[2]
    ts = _tile(s, MATMUL_ROWS)
    nt = s // ts

    def body(dp_ref, x_ref, vec_ref, dw_ref, acc):
        i = pl.program_id(0)

        @pl.when(i == 0)
        def _():
            acc[...] = jnp.zeros_like(acc)

        h, _, _ = _prenorm(x_ref[...], vec_ref)
        hb = h.astype(BF16)
        for k in range(4):
            acc[k] += lax.dot_general(hb, dp_ref[k], TN_DIMS, preferred_element_type=F32)

        @pl.when(i == nt - 1)
        def _():
            dw_ref[...] = acc[...].astype(BF16)

    return pl.pallas_call(
        body, name=f"inproj_bwd_w_l{layer}", grid=(nt,),
        in_specs=[pl.BlockSpec((4, ts, p), lambda i: (0, i, 0)),
                  pl.BlockSpec((ts, d), lambda i: (i, 0)),
                  pl.BlockSpec((8, d), lambda i: (0, 0))],
        out_specs=pl.BlockSpec((4, d, p), lambda i: (0, 0, 0)),
        out_shape=jax.ShapeDtypeStruct((4, d, p), BF16),
        scratch_shapes=[pltpu.VMEM((4, d, p), F32)],
        compiler_params=_params(1),
    )(dproj, x, vec)


def _sum_slots(stacked, name):
    n, rows, cols = stacked.shape
    tr = _row_tile(rows)

    def body(in_ref, out_ref):
        total = in_ref[0].astype(F32)
        for b in range(1, n):
            total = total + in_ref[b].astype(F32)
        out_ref[...] = total

    return pl.pallas_call(
        body, name=name, grid=(rows // tr,),
        in_specs=[pl.BlockSpec((n, tr, cols), lambda i: (0, i, 0))],
        out_specs=pl.BlockSpec((tr, cols), lambda i: (i, 0)),
        out_shape=jax.ShapeDtypeStruct((rows, cols), F32),
        compiler_params=_params(1),
    )(stacked)


def _adamw(w, m, v, grads, name):
    shape = w.shape
    cols = shape[-1]
    rows = w.size // cols
    tr = _row_tile(rows)
    n = len(grads)

    def body(*refs):
        w_ref, m_ref, v_ref = refs[:3]
        g_refs = refs[3:3 + n]
        g_out, d_out, m_out, v_out = refs[3 + n:]
        g = g_refs[0][...]
        for r in g_refs[1:]:
            g = g + r[...]
        m_new = ADAM_B1 * m_ref[...] + (1.0 - ADAM_B1) * g
        v_new = ADAM_B2 * v_ref[...] + (1.0 - ADAM_B2) * (g * g)
        m_hat = m_new / (1.0 - ADAM_B1 ** ADAM_STEP)
        v_hat = v_new / (1.0 - ADAM_B2 ** ADAM_STEP)
        g_out[...] = g
        d_out[...] = (-ADAM_LR) * (m_hat / (jnp.sqrt(v_hat) + ADAM_EPS) + ADAM_WD * w_ref[...])
        m_out[...] = m_new
        v_out[...] = v_new

    blk = pl.BlockSpec((tr, cols), lambda i: (i, 0))
    outs = pl.pallas_call(
        body, name=name, grid=(rows // tr,),
        in_specs=[blk] * (3 + n), out_specs=[blk] * 4,
        out_shape=[jax.ShapeDtypeStruct((rows, cols), F32)] * 4,
        compiler_params=_params(1),
    )(*[a.reshape(rows, cols) for a in (w, m, v, *grads)])
    return tuple(o.reshape(shape) for o in outs)


def _adam_update(w, m, v, g):
    m_new = ADAM_B1 * m + (1.0 - ADAM_B1) * g
    v_new = ADAM_B2 * v + (1.0 - ADAM_B2) * (g * g)
    m_hat = m_new / (1.0 - ADAM_B1 ** ADAM_STEP)
    v_hat = v_new / (1.0 - ADAM_B2 ** ADAM_STEP)
    return (-ADAM_LR) * (m_hat / (jnp.sqrt(v_hat) + ADAM_EPS) + ADAM_WD * w), m_new, v_new


def _adamw_layer(w, m, v, grads, layer, prev, name):
    nl = w.shape[0]
    cols = w.shape[-1]
    rows = w.size // (nl * cols)
    tr = _row_tile(rows)
    off = layer * (rows // tr)
    n = len(grads)
    n_prev = 0 if prev is None else 4

    def body(*refs):
        w_ref, m_ref, v_ref = refs[:3]
        g_refs = refs[3:3 + n]
        g_out, d_out, m_out, v_out = refs[3 + n + n_prev:]
        g = g_refs[0][...]
        for r in g_refs[1:]:
            g = g + r[...]
        g_out[...] = g
        d_out[...], m_out[...], v_out[...] = _adam_update(w_ref[...], m_ref[...], v_ref[...], g)

    mine = pl.BlockSpec((tr, cols), lambda i: (off + i, 0))
    args = [a.reshape(nl * rows, cols) for a in (w, m, v)] + [g.reshape(rows, cols) for g in grads]
    outs = pl.pallas_call(
        body, name=name, grid=(rows // tr,),
        in_specs=[mine] * 3 + [pl.BlockSpec((tr, cols), lambda i: (i, 0))] * n + [ANY] * n_prev,
        out_specs=[mine] * 4,
        out_shape=[jax.ShapeDtypeStruct((nl * rows, cols), F32)] * 4,
        input_output_aliases={3 + n + k: k for k in range(n_prev)},
        compiler_params=_params(1),
    )(*args, *(prev or ()))
    return tuple(outs)


def _into_slot(a, dtype, chip_arr, name):
    rows, cols = a.shape
    tr = _row_tile(rows)

    def body(chip_ref, a_ref, out_ref):
        del chip_ref
        out_ref[...] = a_ref[...].astype(dtype)

    return pl.pallas_call(
        body, name=name,
        grid_spec=pltpu.PrefetchScalarGridSpec(
            num_scalar_prefetch=1, grid=(rows // tr,),
            in_specs=[pl.BlockSpec((tr, cols), lambda i, chip: (i, 0))],
            out_specs=pl.BlockSpec((None, tr, cols), lambda i, chip: (chip[0], i, 0))),
        out_shape=jax.ShapeDtypeStruct((4, rows, cols), dtype),
        compiler_params=_params(1),
    )(chip_arr, a)


def _sum_owner(own, land, chip_arr, own_block, own_index, name):
    blk = land.shape[1:]
    tr = _row_tile(blk[-2])
    steps = blk[-2] // tr
    tile = (*blk[:-2], tr, blk[-1])
    lead = (0,) * (len(blk) - 2)

    def body(chip_ref, own_ref, l1, l2, l3, out_ref):
        del chip_ref
        out_ref[...] = (own_ref[...].astype(F32) + l1[...].astype(F32)) + (l2[...].astype(F32) + l3[...].astype(F32))

    def landed(k):
        return pl.BlockSpec((None, *tile), lambda i, chip: (chip[0] ^ k, *lead, i, 0))

    return pl.pallas_call(
        body, name=name,
        grid_spec=pltpu.PrefetchScalarGridSpec(
            num_scalar_prefetch=1, grid=(steps,),
            in_specs=[pl.BlockSpec(own_block(tr), own_index), landed(1), landed(2), landed(3)],
            out_specs=pl.BlockSpec(tile, lambda i, chip: (*lead, i, 0))),
        out_shape=jax.ShapeDtypeStruct(blk, F32),
        compiler_params=_params(1),
    )(chip_arr, own, land, land, land)


def _ada_w_grad(c_t, dmod):
    d, nb = c_t.shape
    nl, _, n = dmod.shape

    def body(c_ref, dm_ref, out_ref):
        for layer in range(nl):
            total = c_ref[:, 0:1] * dm_ref[layer, 0:1, :]
            for b in range(1, nb):
                total = total + c_ref[:, b:b + 1] * dm_ref[layer, b:b + 1, :]
            out_ref[layer] = total

    return pl.pallas_call(
        body, name="ada_w_grad", in_specs=[VMEM, VMEM], out_specs=VMEM,
        out_shape=jax.ShapeDtypeStruct((nl, d, n), F32),
        compiler_params=pltpu.CompilerParams(vmem_limit_bytes=V7X_VMEM_LIMIT_BYTES),
    )(c_t, dmod)


def _place():
    x, y, c = lax.axis_index("x"), lax.axis_index("y"), lax.axis_index("c")
    return x, y, c


OTHER_CHIPS = ((1, 0), (0, 1), (1, 1))
OTHER_DEVICES = tuple((fx, fy, fc) for fx in (0, 1) for fy in (0, 1) for fc in (0, 1))[1:]


def _mod_exchange(c_row, ada_w):
    nl, d, n = ada_w.shape

    def body(c_ref, w_ref, cbuf, modbuf, token, cblk, mres, send_a, recv_a, send_c, recv_c):
        token[...] = jnp.zeros_like(token)
        x, y, c = _place()
        me = 4 * x + 2 * y + c
        chip = 2 * x + y
        cv = c_ref[...]
        cblk[...] = jnp.zeros_like(cblk)
        cblk[0:1, :] = cv * _sigmoid(cv)

        def rows_of(dev):
            return cbuf.at[pl.ds(pl.multiple_of(8 * dev, 8), 8), :]

        cbuf[pl.ds(pl.multiple_of(8 * me, 8), 8), :] = cblk[...]
        sends = []
        for j, (fx, fy, fc) in enumerate(OTHER_DEVICES):
            cp = pltpu.make_async_remote_copy(
                src_ref=cblk, dst_ref=rows_of(me), send_sem=send_a.at[j], recv_sem=recv_a.at[j],
                device_id=(x ^ fx, y ^ fy, c ^ fc), device_id_type=MESH)
            cp.start()
            sends.append(cp)
        for j, (fx, fy, fc) in enumerate(OTHER_DEVICES):
            peer = 4 * (x ^ fx) + 2 * (y ^ fy) + (c ^ fc)
            pltpu.make_async_remote_copy(
                src_ref=cblk, dst_ref=rows_of(peer), send_sem=send_a.at[j], recv_sem=recv_a.at[j],
                device_id=(x ^ fx, y ^ fy, c ^ fc), device_id_type=MESH).wait_recv()
        for cp in sends:
            cp.wait_send()

        call = cbuf[...]
        for layer in range(nl):
            mres[:, layer * n:(layer + 1) * n] = jnp.dot(
                call, w_ref[layer], preferred_element_type=F32, precision=lax.Precision.HIGHEST)

        def block_of(dev):
            return mres.at[pl.ds(pl.multiple_of(8 * dev, 8), 8), :]

        modbuf[chip] = mres[pl.ds(pl.multiple_of(8 * me, 8), 8), :]
        sends = []
        for j, (fx, fy) in enumerate(OTHER_CHIPS):
            peer = 4 * (x ^ fx) + 2 * (y ^ fy) + c
            cp = pltpu.make_async_remote_copy(
                src_ref=block_of(peer), dst_ref=modbuf.at[chip], send_sem=send_c.at[j], recv_sem=recv_c.at[j],
                device_id=(x ^ fx, y ^ fy, c), device_id_type=MESH)
            cp.start()
            sends.append(cp)
        for j, (fx, fy) in enumerate(OTHER_CHIPS):
            pltpu.make_async_remote_copy(
                src_ref=block_of(me), dst_ref=modbuf.at[2 * (x ^ fx) + (y ^ fy)],
                send_sem=send_c.at[j], recv_sem=recv_c.at[j],
                device_id=(x ^ fx, y ^ fy, c), device_id_type=MESH).wait_recv()
        for cp in sends:
            cp.wait_send()

    return pl.pallas_call(
        body, name="mod_exchange", in_specs=[VMEM, VMEM], out_specs=[VMEM, VMEM, VMEM],
        out_shape=[jax.ShapeDtypeStruct((64, d), F32), jax.ShapeDtypeStruct((4, 8, nl * n), F32),
                   jax.ShapeDtypeStruct((8, 128), F32)],
        scratch_shapes=[pltpu.VMEM((8, d), F32), pltpu.VMEM((64, nl * n), F32),
                        pltpu.SemaphoreType.DMA((7,)), pltpu.SemaphoreType.DMA((7,)),
                        pltpu.SemaphoreType.DMA((3,)), pltpu.SemaphoreType.DMA((3,))],
        compiler_params=pltpu.CompilerParams(vmem_limit_bytes=V7X_VMEM_LIMIT_BYTES, has_side_effects=True),
    )(c_row, ada_w)


def _in_hbm(a):
    return pltpu.with_memory_space_constraint(a, pltpu.HBM)


def _gather_start(lands, groups):
    n, ngr = len(lands), len(groups)

    def body(*refs):
        land = refs[:n]
        sems = refs[n:n + 2 * ngr]
        token = refs[-1]
        x, y, c = _place()
        chip = 2 * x + y
        for gi, idxs in enumerate(groups):
            for t, i in enumerate(idxs):
                for j, (fx, fy) in enumerate(OTHER_CHIPS):
                    pltpu.make_async_remote_copy(
                        src_ref=land[i].at[chip], dst_ref=land[i].at[chip],
                        send_sem=sems[2 * gi].at[3 * t + j], recv_sem=sems[2 * gi + 1].at[3 * t + j],
                        device_id=(x ^ fx, y ^ fy, c), device_id_type=MESH).start()
        token[...] = jnp.zeros_like(token)

    sem_shapes = []
    for idxs in groups:
        sem_shapes += [pltpu.SemaphoreType.DMA((3 * len(idxs),))] * 2
    out = pl.pallas_call(
        body, name="weight_gather_start",
        in_specs=[HBM] * n, out_specs=[SEM] * (2 * ngr) + [HBM] * n + [VMEM],
        out_shape=sem_shapes + [pltpu.HBM(a.shape, a.dtype) for a in lands] + [jax.ShapeDtypeStruct((8, 128), F32)],
        input_output_aliases={i: 2 * ngr + i for i in range(n)},
        compiler_params=pltpu.CompilerParams(has_side_effects=DATAFLOW_EFFECT),
    )(*[_in_hbm(a) for a in lands])
    sems = [(out[2 * gi], out[2 * gi + 1]) for gi in range(ngr)]
    return sems, list(out[2 * ngr:2 * ngr + n]), out[-1]


def _gather_wait(lands, sems, after, name):
    n = len(lands)

    def body(*refs):
        land = refs[:n]
        send_sems, recv_sems = refs[n], refs[n + 1]
        x, y, c = _place()
        chip = 2 * x + y
        for t in range(n):
            for j, (fx, fy) in enumerate(OTHER_CHIPS):
                cp = pltpu.make_async_remote_copy(
                    src_ref=land[t].at[chip], dst_ref=land[t].at[2 * (x ^ fx) + (y ^ fy)],
                    send_sem=send_sems.at[3 * t + j], recv_sem=recv_sems.at[3 * t + j],
                    device_id=(x ^ fx, y ^ fy, c), device_id_type=MESH)
                cp.wait_send()
                cp.wait_recv()

    out = pl.pallas_call(
        body, name=name,
        in_specs=[HBM] * n + [SEM, SEM, ANY], out_specs=[HBM] * n,
        out_shape=[pltpu.HBM(a.shape, a.dtype) for a in lands],
        input_output_aliases={i: i for i in range(n)},
        compiler_params=pltpu.CompilerParams(has_side_effects=DATAFLOW_EFFECT),
    )(*lands, sems[0], sems[1], after)
    return list(out)


def _to_owner_copies(pairs, q):
    x, y, c = _place()
    chip = 2 * x + y
    out = []
    for t, (part, land) in enumerate(pairs):
        for j, (fx, fy) in enumerate(OTHER_CHIPS):
            owner = 2 * (x ^ fx) + (y ^ fy)
            if part.shape[0] == 4 and part.shape[1:] == land.shape[1:]:
                src = part.at[owner]
            else:
                src = part.at[:, pl.ds(pl.multiple_of(owner * q, q), q), :]
            out.append((src, land.at[chip], land.at[owner], (x ^ fx, y ^ fy, c), 3 * t + j))
    return out


def _to_all_copies(bufs, first_sem):
    x, y, c = _place()
    me = 4 * x + 2 * y + c
    out = []
    for t, buf in enumerate(bufs):
        for j, (fx, fy, fc) in enumerate(OTHER_DEVICES):
            them = 4 * (x ^ fx) + 2 * (y ^ fy) + (c ^ fc)
            out.append((buf.at[me], buf.at[me], buf.at[them], (x ^ fx, y ^ fy, c ^ fc), first_sem + 7 * t + j))
    return out


def _exchange_copies(refs, n_owner, q):
    pairs = list(zip(refs[:n_owner], refs[n_owner:2 * n_owner]))
    return _to_owner_copies(pairs, q) + _to_all_copies(refs[2 * n_owner:], 3 * n_owner)


def _exchange_start(arrays, n_owner, q, name):
    n = len(arrays)
    n_sems = 3 * n_owner + 7 * (n - 2 * n_owner)

    def body(*refs):
        send_sems, recv_sems = refs[n], refs[n + 1]
        for src, dst, _, peer, k in _exchange_copies(refs[:n], n_owner, q):
            pltpu.make_async_remote_copy(src_ref=src, dst_ref=dst, send_sem=send_sems.at[k], recv_sem=recv_sems.at[k],
                                         device_id=peer, device_id_type=MESH).start()
        refs[-1][...] = jnp.zeros_like(refs[-1])

    out = pl.pallas_call(
        body, name=name,
        in_specs=[HBM] * n, out_specs=[SEM, SEM] + [HBM] * n + [VMEM],
        out_shape=[pltpu.SemaphoreType.DMA((n_sems,))] * 2 + [pltpu.HBM(a.shape, a.dtype) for a in arrays]
        + [jax.ShapeDtypeStruct((8, 128), F32)],
        input_output_aliases={i: 2 + i for i in range(n)},
        compiler_params=pltpu.CompilerParams(has_side_effects=DATAFLOW_EFFECT),
    )(*[_in_hbm(a) for a in arrays])
    return (out[0], out[1]), list(out[2:2 + n]), out[-1]


def _exchange_wait(arrays, sems, n_owner, q, after, name):
    n = len(arrays)

    def body(*refs):
        send_sems, recv_sems = refs[n], refs[n + 1]
        for src, _, landed, peer, k in _exchange_copies(refs[:n], n_owner, q):
            cp = pltpu.make_async_remote_copy(src_ref=src, dst_ref=landed, send_sem=send_sems.at[k], recv_sem=recv_sems.at[k],
                                              device_id=peer, device_id_type=MESH)
            cp.wait_send()
            cp.wait_recv()

    out = pl.pallas_call(
        body, name=name,
        in_specs=[HBM] * n + [SEM, SEM, ANY], out_specs=[HBM] * n,
        out_shape=[pltpu.HBM(a.shape, a.dtype) for a in arrays],
        input_output_aliases={i: i for i in range(n)},
        compiler_params=pltpu.CompilerParams(has_side_effects=DATAFLOW_EFFECT),
    )(*arrays, sems[0], sems[1], after)
    return list(out)


def _sibling_swap(parts, layer):
    n = len(parts)

    def body(*refs):
        srcs, outs = refs[:n], refs[n:2 * n]
        send_sems, recv_sems = refs[2 * n:]
        x, y, c = _place()
        cps = [pltpu.make_async_remote_copy(
            src_ref=srcs[i], dst_ref=outs[i], send_sem=send_sems.at[i], recv_sem=recv_sems.at[i],
            device_id=(x, y, 1 - c), device_id_type=MESH) for i in range(n)]
        for cp in cps:
            cp.start()
        for cp in cps:
            cp.wait()

    return pl.pallas_call(
        body, name=f"sibling_swap_l{layer}", in_specs=[ANY] * n, out_specs=[ANY] * n,
        out_shape=[jax.ShapeDtypeStruct(a.shape, a.dtype) for a in parts],
        scratch_shapes=[pltpu.SemaphoreType.DMA((n,)), pltpu.SemaphoreType.DMA((n,))],
        compiler_params=pltpu.CompilerParams(has_side_effects=True),
    )(*parts)


def _rows8(*rows):
    d = rows[0].shape[-1]
    out = jnp.zeros((8, d), F32)
    for i, r in enumerate(rows):
        out = out.at[i].set(r.reshape(d))
    return out


def kernel(x, c, ada_w, ada_b, pre_norm_g, w_in, conv_w, conv_b, gate_a_w, gate_a_b, gate_x_w, gate_x_b, lru_lambda, pool_w, pool_b, pool_scale, w_out, post_norm_g, loss_target, m_ada_w, m_ada_b, m_pre_norm_g, m_w_in, m_conv_w, m_conv_b, m_gate_a_w, m_gate_a_b, m_gate_x_w, m_gate_x_b, m_lru_lambda, m_pool_w, m_pool_b, m_pool_scale, m_w_out, m_post_norm_g, v_ada_w, v_ada_b, v_pre_norm_g, v_w_in, v_conv_w, v_conv_b, v_gate_a_w, v_gate_a_b, v_gate_x_w, v_gate_x_b, v_lru_lambda, v_pool_w, v_pool_b, v_pool_scale, v_w_out, v_post_norm_g):
    nl, d, n_ada = ada_w.shape
    s = x.shape[1]
    nh, hd = gate_a_w.shape[1], gate_a_w.shape[2]
    ng, gq, gd = pool_w.shape[1], pool_w.shape[2], pool_w.shape[3]
    me = 4 * lax.axis_index("x") + 2 * lax.axis_index("y") + lax.axis_index("c")
    chip = 2 * lax.axis_index("x") + lax.axis_index("y")
    chip_arr = jnp.reshape(chip, (1,)).astype(jnp.int32)
    x0 = x.reshape(s, d)
    target = loss_target.reshape(s, d)
    p_in = w_in.shape[2]
    r_out = w_out.shape[1]

    cbuf, modbuf, mod_token = _mod_exchange(c.reshape(1, d), ada_w)
    mod = modbuf[:, 0, :].reshape(4, nl, n_ada).transpose(1, 0, 2).reshape(nl, 4 * n_ada) + ada_b

    win = [_into_slot(w_in[l], BF16, chip_arr, f"slot_w_in_l{l}") for l in range(nl)]
    wout = [_into_slot(w_out[l], BF16, chip_arr, f"slot_w_out_l{l}") for l in range(nl)]
    pw = [_into_slot(pool_w[l].reshape(ng * gq, gd), BF16, chip_arr, f"slot_pool_w_l{l}") for l in range(nl)]
    convw = _into_slot(conv_w.reshape(nl * CONV_WIDTH, d // 4) + mod_token[0:1, 0:1], F32, chip_arr, "slot_conv_w")
    poolb = _into_slot(pool_b.reshape(nl * ng, gq), F32, chip_arr, "slot_pool_b")
    lands = [win[0], convw, poolb, *pw, wout[0]]
    groups = [[0], list(range(1, len(lands)))]
    for l in range(1, nl):
        groups.append([len(lands), len(lands) + 1])
        lands += [win[l], wout[l]]
    sems, lands, token = _gather_start(lands, groups)
    wa_b, wx_b = gate_a_w.astype(BF16), gate_x_w.astype(BF16)
    vecs, rvecs = [], []
    for l in range(nl):
        vecs.append(_rows8(mod[l, 0:d], mod[l, d:2 * d], mod[l, 2 * d:3 * d], pre_norm_g[l], post_norm_g[l]))
        rvecs.append(_rows8(conv_b[l], gate_a_b[l], gate_x_b[l], lru_lambda[l]))

    xs, projs, hss, ycats, ys = [x0], [], [], [], []
    sq = None
    convw_full = poolw_full = pvecs = None
    for l in range(nl):
        if l == 0:
            (win[0],) = _gather_wait([lands[0]], sems[0], modbuf, "weight_gather_wait_a")
        proj = _inproj_fwd(xs[l], vecs[l], win[l], l)
        if l == 0:
            got = _gather_wait(lands[1:len(groups[1]) + 1], sems[1], proj, "weight_gather_wait_b")
            convw_g, poolb_g, pw, wout[0] = got[0], got[1], got[2:2 + nl], got[2 + nl]
            convw_full = convw_g.reshape(4, nl, CONV_WIDTH, d // 4).transpose(1, 2, 0, 3).reshape(nl, CONV_WIDTH, d)
            poolb_full = poolb_g.reshape(4, nl, ng, gq).transpose(1, 2, 0, 3).reshape(nl, d)
            poolw_full = [a.reshape(4, ng, gq, gd).transpose(1, 0, 2, 3).reshape(ng, gd, gd) for a in pw]
            pvecs = [_rows8(poolb_full[k], pool_scale[k]) for k in range(nl)]
        ycat, hs = _rnn_fwd(proj, convw_full[l], rvecs[l], wa_b[l], wx_b[l], l)
        if l + 1 < nl:
            base = len(groups[1]) + 1 + 2 * l
            win[l + 1], wout[l + 1] = _gather_wait(lands[base:base + 2], sems[2 + l], hs, f"weight_gather_wait_c{l + 1}")
        ycat = _pool_fwd(proj, ycat, poolw_full[l], pvecs[l], l)
        y, xo, sq = _outproj_fwd(ycat, wout[l], xs[l], vecs[l], target if l == nl - 1 else None, l)
        projs.append(proj), hss.append(hs), ycats.append(ycat), ys.append(y), xs.append(xo)

    gate_rows = nh * hd * hd // d

    def finish(l, flights, after, prev):
        (sems_a, arr_a), (sems_b, arr_b), (sems_c, arr_c) = flights
        dwout_l, dpw_l, rwout, rpw, gates = _exchange_wait(arr_a, sems_a, 2, gq, after, f"grad_wait_a_l{l}")
        dwin_l, rwin = _exchange_wait(arr_b, sems_b, 1, gq, gates, f"grad_wait_b_l{l}")
        (slabs,) = _exchange_wait(arr_c, sems_c, 0, gq, rwin, f"grad_wait_c_l{l}")
        p_win = _sum_owner(dwin_l, rwin, chip_arr, lambda tr: (None, tr, p_in),
                           lambda i, chip: (chip[0], i, 0), f"sum_w_in_l{l}")
        p_wout = _sum_owner(dwout_l, rwout, chip_arr, lambda tr: (None, tr, d),
                            lambda i, chip: (chip[0], i, 0), f"sum_w_out_l{l}")
        p_pw = _sum_owner(dpw_l, rpw, chip_arr, lambda tr: (ng, tr, gd),
                          lambda i, chip: (0, chip[0], 0), f"sum_pool_w_l{l}")
        q_win, q_wout, q_pw = _sibling_swap([p_win, p_wout, p_pw], l)
        prev = prev or {}
        big = {
            "w_in": _adamw_layer(w_in, m_w_in, v_w_in, [p_win, q_win], l, prev.get("w_in"), f"adamw_w_in_l{l}"),
            "w_out": _adamw_layer(w_out, m_w_out, v_w_out, [p_wout, q_wout], l, prev.get("w_out"), f"adamw_w_out_l{l}"),
            "pool_w": _adamw_layer(pool_w, m_pool_w, v_pool_w, [p_pw, q_pw], l, prev.get("pool_w"), f"adamw_pool_w_l{l}"),
        }
        return big, slabs, _sum_slots(slabs, f"sum_slab_l{l}"), _sum_slots(gates, f"sum_gates_l{l}")

    dx = xs[nl]
    flights = token = big = None
    slabs_all, totals, gate_totals = [None] * nl, [None] * nl, [None] * nl
    for l in reversed(range(nl)):
        vec_l = vecs[l] if token is None else vecs[l] + token[0:1, 0:1]
        dycat, dwout_l, dvec_o = _outproj_bwd(dx, ys[l], ycats[l], wout[l], vec_l, l)
        dproj, dwa, dwx, dvec_r = _rnn_bwd(projs[l], hss[l], dycat, convw_full[l], rvecs[l], wa_b[l], wx_b[l], l)
        dproj, dpw_l, dvec_p = _pool_bwd(projs[l], dycat, dproj, poolw_full[l], pvecs[l], l)
        gates = jnp.concatenate([dwa.reshape(gate_rows, d), dwx.reshape(gate_rows, d)], axis=0)
        gatess = lax.dynamic_update_slice(lax.empty((8, *gates.shape), BF16), gates[None], (me, 0, 0))
        sems_a, arr_a, tok_a = _exchange_start(
            [dwout_l, dpw_l, lax.empty(dwout_l.shape, BF16), lax.empty((4, ng, gq, gd), BF16), gatess],
            2, gq, f"grad_start_a_l{l}")
        dwin_l = _inproj_bwd_w(dproj, xs[l], vec_l + tok_a[0:1, 0:1], l)
        sems_b, arr_b, tok_b = _exchange_start([dwin_l, lax.empty(dwin_l.shape, BF16)], 1, gq, f"grad_start_b_l{l}")
        dx, dvec_i = _inproj_bwd_x(dproj, win[l], xs[l], dx, vec_l + tok_b[0:1, 0:1], l)
        parts = [dvec_i, dvec_o, dvec_r, dvec_p]
        if l == nl - 1:
            parts.append(jnp.tile(sq, (1, d // sq.shape[1])))
        slab = jnp.concatenate(parts, axis=0)
        slabs = lax.dynamic_update_slice(lax.empty((8, *slab.shape), F32), slab[None], (me, 0, 0))
        sems_c, arr_c, token = _exchange_start([slabs], 0, gq, f"grad_start_c_l{l}")
        if flights is not None:
            big, slabs_all[l + 1], totals[l + 1], gate_totals[l + 1] = finish(l + 1, flights, token, big)
        flights = ((sems_a, arr_a), (sems_b, arr_b), (sems_c, arr_c))
    big, slabs_all[0], totals[0], gate_totals[0] = finish(0, flights, big["w_in"][3] if big else dx, big)
    grad_x = dx.reshape(x.shape)
    loss = totals[nl - 1][40, 0] * (0.5 / d)

    def stack(r):
        return jnp.stack([totals[l][r] for l in range(nl)])

    g_dmod = jnp.stack([jnp.concatenate([totals[l][0], totals[l][1], totals[l][8]]) for l in range(nl)])
    g_pre, g_post = stack(2), stack(9)
    g_convb, g_ba, g_bx, g_lam = stack(16), stack(17), stack(18), stack(19)
    g_convw = jnp.stack([totals[l][20:20 + CONV_WIDTH] for l in range(nl)])
    g_poolb, g_pscale = stack(32), stack(33)
    g_wa = jnp.stack([gate_totals[l][0:gate_rows].reshape(nh, hd, hd) for l in range(nl)])
    g_wx = jnp.stack([gate_totals[l][gate_rows:2 * gate_rows].reshape(nh, hd, hd) for l in range(nl)])

    per_dev = jnp.stack([jnp.concatenate([slabs_all[l][:, r] for r in (0, 1, 8)], axis=-1) for l in range(nl)])
    dmod_mine = lax.dynamic_slice_in_dim(per_dev, chip * n_ada, n_ada, axis=2)
    c_all_t = cbuf.reshape(8, 8, d)[:, 0, :].T
    g_ada_w = _ada_w_grad(c_all_t, dmod_mine)

    def my_cols(full, width):
        return lax.dynamic_slice_in_dim(full, chip * width, width, axis=full.ndim - 1)

    g_convw_mine = my_cols(g_convw, d // 4)
    g_poolb_mine = my_cols(g_poolb.reshape(nl, ng, gd), gq)

    results = {
        "ada_w": _adamw(ada_w, m_ada_w, v_ada_w, [g_ada_w], "adamw_ada_w"),
        "ada_b": _adamw(ada_b, m_ada_b, v_ada_b, [g_dmod], "adamw_ada_b"),
        "pre_norm_g": _adamw(pre_norm_g, m_pre_norm_g, v_pre_norm_g, [g_pre], "adamw_pre_norm_g"),
        "w_in": tuple(o.reshape(w_in.shape) for o in big["w_in"]),
        "conv_w": _adamw(conv_w, m_conv_w, v_conv_w, [g_convw_mine], "adamw_conv_w"),
        "conv_b": _adamw(conv_b, m_conv_b, v_conv_b, [g_convb], "adamw_conv_b"),
        "gate_a_w": _adamw(gate_a_w, m_gate_a_w, v_gate_a_w, [g_wa], "adamw_gate_a_w"),
        "gate_a_b": _adamw(gate_a_b, m_gate_a_b, v_gate_a_b, [g_ba.reshape(gate_a_b.shape)], "adamw_gate_a_b"),
        "gate_x_w": _adamw(gate_x_w, m_gate_x_w, v_gate_x_w, [g_wx], "adamw_gate_x_w"),
        "gate_x_b": _adamw(gate_x_b, m_gate_x_b, v_gate_x_b, [g_bx.reshape(gate_x_b.shape)], "adamw_gate_x_b"),
        "lru_lambda": _adamw(lru_lambda, m_lru_lambda, v_lru_lambda, [g_lam], "adamw_lru_lambda"),
        "pool_w": tuple(o.reshape(pool_w.shape) for o in big["pool_w"]),
        "pool_b": _adamw(pool_b, m_pool_b, v_pool_b, [g_poolb_mine], "adamw_pool_b"),
        "pool_scale": _adamw(pool_scale, m_pool_scale, v_pool_scale, [g_pscale], "adamw_pool_scale"),
        "w_out": tuple(o.reshape(w_out.shape) for o in big["w_out"]),
        "post_norm_g": _adamw(post_norm_g, m_post_norm_g, v_post_norm_g, [g_post], "adamw_post_norm_g"),
    }
    names = list(results)
    return (loss, grad_x,
            *[results[n][0] for n in names], *[results[n][1] for n in names],
            *[results[n][2] for n in names], *[results[n][3] for n in names])
```

```python
import functools

import jax
import jax.numpy as jnp
from jax import lax
from jax.experimental import pallas as pl
from jax.experimental.pallas import tpu as pltpu

F32 = jnp.float32
BF16 = jnp.bfloat16

NORM_EPS = 1e-6
LRU_C = 8.0
CONV_WIDTH = 4
MAX_POOL_WINDOW = 16
HALO = 16
ADAM_LR = 0.001
ADAM_B1 = 0.9
ADAM_B2 = 0.999
ADAM_EPS = 1e-08
ADAM_WD = 0.01
ADAM_STEP = 10

V7X_VMEM_LIMIT_BYTES = 56 * 1024 * 1024
MATMUL_ROWS = 512
SCAN_ROWS = 512
ELEMENTWISE_ROWS = 512

MESH = pl.DeviceIdType.MESH
ANY = pl.BlockSpec(memory_space=pl.ANY)
VMEM = pl.BlockSpec(memory_space=pltpu.VMEM)
HBM = pl.BlockSpec(memory_space=pltpu.HBM)
SEM = pl.BlockSpec(memory_space=pltpu.SEMAPHORE)
DATAFLOW_EFFECT = pltpu.SideEffectType.DATAFLOW_SIDE_EFFECTING

NT_DIMS = (((1,), (1,)), ((), ()))
TN_DIMS = (((0,), (0,)), ((), ()))


def _params(n_grid_axes):
    return pltpu.CompilerParams(dimension_semantics=("arbitrary",) * n_grid_axes,
                                vmem_limit_bytes=V7X_VMEM_LIMIT_BYTES)


def _tile(total, want):
    t = min(want, max(total // 2, HALO))
    assert total % t == 0 and t % HALO == 0, (total, t)
    return t


def _row_tile(rows):
    for t in range(min(rows, ELEMENTWISE_ROWS) // 8 * 8, 0, -8):
        if rows % t == 0:
            return t
    return rows


def _sigmoid(z):
    return 1.0 / (1.0 + jnp.exp(-z))


def _softplus(z):
    return jnp.maximum(z, 0.0) + jnp.log(1.0 + jnp.exp(-jnp.abs(z)))


def _neg_expm1(z):
    return -jnp.tanh(0.5 * z) * (jnp.exp(z) + 1.0)


def _colsum(v):
    return jnp.sum(v, axis=0, keepdims=True)


def _prenorm(xt, vec_ref):
    rs = lax.rsqrt(jnp.mean(xt * xt, axis=-1, keepdims=True) + NORM_EPS)
    xn = xt * rs
    h = xn * vec_ref[3:4, :] * (1.0 + vec_ref[1:2, :]) + vec_ref[0:1, :]
    return h, xn, rs


def _shift_down(v, d, fill):
    t = v.shape[0]
    if d % 8 == 0:
        return jnp.concatenate([jnp.full((d, v.shape[1]), fill, v.dtype), v[:t - d]], axis=0)
    row = lax.broadcasted_iota(jnp.int32, v.shape, 0)
    return jnp.where(row >= d, pltpu.roll(v, d, 0), fill)


def _shift_up(v, d, fill):
    t = v.shape[0]
    if d % 8 == 0:
        return jnp.concatenate([v[d:], jnp.full((d, v.shape[1]), fill, v.dtype)], axis=0)
    row = lax.broadcasted_iota(jnp.int32, v.shape, 0)
    return jnp.where(row < t - d, pltpu.roll(v, t - d, 0), fill)


def _scan_fwd(a, v, h_before):
    d = 1
    while d < a.shape[0]:
        v = v + a * _shift_down(v, d, 0.0)
        a = a * _shift_down(a, d, 1.0)
        d *= 2
    return a * h_before + v


def _scan_rev(b, v):
    d = 1
    while d < b.shape[0]:
        v = v + b * _shift_up(v, d, 0.0)
        b = b * _shift_up(b, d, 0.0)
        d *= 2
    return v


def _inproj_fwd(x, vec, w_all, layer):
    s, d = x.shape
    p = w_all.shape[2]
    ts = _tile(s, MATMUL_ROWS)

    def body(x_ref, vec_ref, w_ref, proj_ref):
        h, _, _ = _prenorm(x_ref[...], vec_ref)
        hb = h.astype(BF16)
        for k in range(4):
            proj_ref[k] = jnp.dot(hb, w_ref[k], preferred_element_type=F32)

    return pl.pallas_call(
        body, name=f"inproj_fwd_l{layer}", grid=(s // ts,),
        in_specs=[pl.BlockSpec((ts, d), lambda i: (i, 0)),
                  pl.BlockSpec((8, d), lambda i: (0, 0)),
                  pl.BlockSpec((4, d, p), lambda i: (0, 0, 0))],
        out_specs=pl.BlockSpec((4, ts, p), lambda i: (0, i, 0)),
        out_shape=jax.ShapeDtypeStruct((4, s, p), F32),
        compiler_params=_params(1),
    )(x, vec, w_all)


HEADS_PER_STEP = 2
BWD_HEADS_PER_STEP = 1


def _rnn_gates(u, wa, wx, vec_ref, lanes):
    ub = u.astype(BF16)
    r = _sigmoid(jnp.dot(ub, wa, preferred_element_type=F32) + vec_ref[1:2, lanes])
    ig = _sigmoid(jnp.dot(ub, wx, preferred_element_type=F32) + vec_ref[2:3, lanes])
    sp = _softplus(-vec_ref[3:4, lanes])
    log_a = (-LRU_C) * r * sp
    return ub, r, ig, sp, log_a


def _conv(xbuf, cw_ref, vec_ref, lanes, ts):
    u = vec_ref[0:1, lanes] + cw_ref[CONV_WIDTH - 1:CONV_WIDTH, lanes] * xbuf[pl.ds(HALO, ts), lanes]
    for k in range(CONV_WIDTH - 1):
        u = u + cw_ref[k:k + 1, lanes] * xbuf[pl.ds(HALO - (CONV_WIDTH - 1) + k, ts), lanes]
    return u


def _rnn_fwd(proj, cw, vec, wa, wx, layer):
    _, s, d = proj.shape
    nh, hd, _ = wa.shape
    ts = _tile(s, SCAN_ROWS)
    hps = HEADS_PER_STEP
    wl = hps * hd

    def body(proj_ref, cw_ref, vec_ref, wa_ref, wx_ref, ycat_ref, hs_ref, xbuf, hlast):
        i = pl.program_id(1)

        @pl.when(i == 0)
        def _():
            xbuf[0:HALO, :] = jnp.zeros((HALO, wl), F32)
            hlast[...] = jnp.zeros_like(hlast)

        xbuf[pl.ds(HALO, ts), :] = proj_ref[0]
        for hh in range(hps):
            lanes = slice(hh * hd, (hh + 1) * hd)
            u = _conv(xbuf, cw_ref, vec_ref, lanes, ts)
            _, _, ig, _, log_a = _rnn_gates(u, wa_ref[hh], wx_ref[hh], vec_ref, lanes)
            a = jnp.exp(log_a)
            mult = jnp.sqrt(_neg_expm1(2.0 * log_a))
            hs = _scan_fwd(a, mult * (ig * u), hlast[0:1, lanes])
            hs_ref[:, lanes] = hs
            hlast[0:1, lanes] = hs_ref[ts - 1:ts, lanes]
            g = proj_ref[1, :, lanes]
            ycat_ref[:, lanes] = (hs * (g * _sigmoid(g))).astype(BF16)
        xbuf[0:HALO, :] = xbuf[pl.ds(ts, HALO), :]

    return pl.pallas_call(
        body, name=f"rnn_fwd_l{layer}", grid=(nh // hps, s // ts),
        in_specs=[pl.BlockSpec((2, ts, wl), lambda h, i: (0, i, h)),
                  pl.BlockSpec((CONV_WIDTH, wl), lambda h, i: (0, h)),
                  pl.BlockSpec((8, wl), lambda h, i: (0, h)),
                  pl.BlockSpec((hps, hd, hd), lambda h, i: (h, 0, 0)),
                  pl.BlockSpec((hps, hd, hd), lambda h, i: (h, 0, 0))],
        out_specs=[pl.BlockSpec((ts, wl), lambda h, i: (i, h)),
                   pl.BlockSpec((ts, wl), lambda h, i: (i, h))],
        out_shape=[jax.ShapeDtypeStruct((s, 2 * d), BF16), jax.ShapeDtypeStruct((s, d), F32)],
        scratch_shapes=[pltpu.VMEM((ts + HALO, wl), F32), pltpu.VMEM((8, wl), F32)],
        compiler_params=_params(2),
    )(proj, cw, vec, wa, wx)


def _inv_count(i, ts, lanes, win):
    t = i * ts + lax.broadcasted_iota(jnp.int32, (ts, lanes), 0)
    return 1.0 / jnp.minimum(t + 1, win).astype(F32)


def _pooled(xbuf, xt, lanes, win, inv_cnt, ts):
    acc = xt
    for dlt in range(1, win):
        acc = acc + xbuf[pl.ds(HALO - dlt, ts), lanes]
    return acc * inv_cnt - xt


def _pool_fwd(proj, ycat, pw, vec, layer):
    _, s, d = proj.shape
    ng, gd, _ = pw.shape
    ts = _tile(s, MATMUL_ROWS)

    def body(proj_ref, ycat_in, pw_ref, vec_ref, ycat_ref, xbuf):
        del ycat_in
        i = pl.program_id(0)

        @pl.when(i == 0)
        def _():
            xbuf[0:HALO, :] = jnp.zeros((HALO, d), F32)

        xbuf[pl.ds(HALO, ts), :] = proj_ref[0]
        for g in range(ng):
            lanes = slice(g * gd, (g + 1) * gd)
            win = 2 << g
            xt = proj_ref[0, :, lanes]
            pooled = _pooled(xbuf, xt, lanes, win, _inv_count(i, ts, gd, win), ts).astype(BF16)
            z = jnp.dot(pooled, pw_ref[g], preferred_element_type=F32) + vec_ref[0:1, lanes]
            gg = proj_ref[1, :, lanes]
            ycat_ref[:, lanes] = (z * vec_ref[1:2, lanes] * (gg * _sigmoid(gg))).astype(BF16)
        xbuf[0:HALO, :] = xbuf[pl.ds(ts, HALO), :]

    return pl.pallas_call(
        body, name=f"pool_fwd_l{layer}", grid=(s // ts,),
        in_specs=[pl.BlockSpec((2, ts, d), lambda i: (1, i, 0)),
                  ANY,
                  pl.BlockSpec((ng, gd, gd), lambda i: (0, 0, 0)),
                  pl.BlockSpec((8, d), lambda i: (0, 0))],
        out_specs=pl.BlockSpec((ts, d), lambda i: (i, 1)),
        out_shape=jax.ShapeDtypeStruct((s, 2 * d), BF16),
        input_output_aliases={1: 0},
        scratch_shapes=[pltpu.VMEM((ts + HALO, d), F32)],
        compiler_params=_params(1),
    )(proj, ycat, pw, vec)


def _outproj_fwd(ycat, w_all, x, vec, target, layer):
    s, d = x.shape
    nk, kd = w_all.shape[0], w_all.shape[1]
    ts = _tile(s, MATMUL_ROWS)
    last = target is not None

    def body(*refs):
        if last:
            ycat_ref, w_ref, x_ref, vec_ref, tgt_ref, y_ref, xo_ref, sq_ref = refs
        else:
            ycat_ref, w_ref, x_ref, vec_ref, y_ref, xo_ref = refs
        y = jnp.dot(ycat_ref[:, 0:kd], w_ref[0], preferred_element_type=F32)
        for k in range(1, nk):
            y = y + jnp.dot(ycat_ref[:, k * kd:(k + 1) * kd], w_ref[k], preferred_element_type=F32)
        y_ref[...] = y
        rs = lax.rsqrt(jnp.mean(y * y, axis=-1, keepdims=True) + NORM_EPS)
        xo = x_ref[...] + vec_ref[2:3, :] * (y * rs * vec_ref[4:5, :])
        if last:
            err = xo - tgt_ref[...]
            xo_ref[...] = err * (1.0 / d)

            @pl.when(pl.program_id(0) == 0)
            def _():
                sq_ref[...] = jnp.zeros_like(sq_ref)

            sq_ref[...] += jnp.sum(err * err)
        else:
            xo_ref[...] = xo

    row = pl.BlockSpec((ts, d), lambda i: (i, 0))
    in_specs = [pl.BlockSpec((ts, nk * kd), lambda i: (i, 0)),
                pl.BlockSpec((nk, kd, d), lambda i: (0, 0, 0)),
                row, pl.BlockSpec((8, d), lambda i: (0, 0))]
    out_specs = [row, row]
    out_shape = [jax.ShapeDtypeStruct((s, d), F32), jax.ShapeDtypeStruct((s, d), F32)]
    args = [ycat, w_all, x, vec]
    if last:
        in_specs.append(row)
        args.append(target)
        out_specs.append(pl.BlockSpec((8, 128), lambda i: (0, 0)))
        out_shape.append(jax.ShapeDtypeStruct((8, 128), F32))
    out = pl.pallas_call(
        body, name=f"outproj_fwd_l{layer}", grid=(s // ts,),
        in_specs=in_specs, out_specs=out_specs, out_shape=out_shape,
        compiler_params=_params(1),
    )(*args)
    return (out[0], out[1], out[2]) if last else (out[0], out[1], None)


def _outproj_bwd(dxo, y, ycat, w_all, vec, layer):
    s, d = dxo.shape
    nk, kd = w_all.shape[0], w_all.shape[1]
    ts = _tile(s, MATMUL_ROWS)
    nt = s // ts

    def body(dxo_ref, y_ref, ycat_ref, w_ref, vec_ref, dycat_ref, dw_ref, dvec_ref, acc):
        i = pl.program_id(0)

        @pl.when(i == 0)
        def _():
            acc[...] = jnp.zeros_like(acc)
            dvec_ref[...] = jnp.zeros_like(dvec_ref)

        yt = y_ref[...]
        rs = lax.rsqrt(jnp.mean(yt * yt, axis=-1, keepdims=True) + NORM_EPS)
        yhat = yt * rs
        gate, gpost = vec_ref[2:3, :], vec_ref[4:5, :]
        dxo_t = dxo_ref[...]
        dyn = dxo_t * gate
        dvec_ref[0:1, :] += _colsum(dxo_t * (yhat * gpost))
        dvec_ref[1:2, :] += _colsum(dyn * yhat)
        t = dyn * gpost
        dy = (rs * (t - yhat * jnp.mean(t * yhat, axis=-1, keepdims=True))).astype(BF16)
        for k in range(nk):
            cols = slice(k * kd, (k + 1) * kd)
            dycat_ref[:, cols] = lax.dot_general(dy, w_ref[k], NT_DIMS, preferred_element_type=F32)
            acc[k] += lax.dot_general(ycat_ref[:, cols], dy, TN_DIMS, preferred_element_type=F32)

        @pl.when(i == nt - 1)
        def _():
            dw_ref[...] = acc[...].astype(BF16)

    row = pl.BlockSpec((ts, d), lambda i: (i, 0))
    wide = pl.BlockSpec((ts, nk * kd), lambda i: (i, 0))
    return pl.pallas_call(
        body, name=f"outproj_bwd_l{layer}", grid=(nt,),
        in_specs=[row, row, wide,
                  pl.BlockSpec((nk, kd, d), lambda i: (0, 0, 0)),
                  pl.BlockSpec((8, d), lambda i: (0, 0))],
        out_specs=[wide,
                   pl.BlockSpec((nk, kd, d), lambda i: (0, 0, 0)),
                   pl.BlockSpec((8, d), lambda i: (0, 0))],
        out_shape=[jax.ShapeDtypeStruct((s, nk * kd), F32),
                   jax.ShapeDtypeStruct((nk, kd, d), BF16),
                   jax.ShapeDtypeStruct((8, d), F32)],
        scratch_shapes=[pltpu.VMEM((nk, kd, d), F32)],
        compiler_params=_params(1),
    )(dxo, y, ycat, w_all, vec)


def _halo_index(ts, nt):
    return lambda j: jnp.maximum((nt - 1 - j) * (ts // HALO) - 1, 0)


def _rnn_bwd(proj, hs, dycat, cw, vec, wa, wx, layer):
    _, s, d = proj.shape
    nh, hd, _ = wa.shape
    ts = _tile(s, SCAN_ROWS)
    nt = s // ts
    halo = _halo_index(ts, nt)
    hps = BWD_HEADS_PER_STEP
    wl = hps * hd

    def body(proj_ref, xh_ref, hs_ref, hsh_ref, dy_ref, cw_ref, vec_ref, wa_ref, wx_ref,
             dproj_ref, dgates_ref, dvec_ref, xbuf, hbuf, dubuf, carry, dw_acc):
        j = pl.program_id(1)
        first_tile = j == nt - 1

        @pl.when(j == 0)
        def _():
            dubuf[pl.ds(ts, HALO), :] = jnp.zeros((HALO, wl), F32)
            carry[...] = jnp.zeros_like(carry)
            dw_acc[...] = jnp.zeros_like(dw_acc)
            dvec_ref[...] = jnp.zeros_like(dvec_ref)

        xbuf[0:HALO, :] = jnp.where(first_tile, 0.0, xh_ref[0])
        xbuf[pl.ds(HALO, ts), :] = proj_ref[0]
        hbuf[0:HALO, :] = jnp.where(first_tile, 0.0, hsh_ref[...])
        hbuf[pl.ds(HALO, ts), :] = hs_ref[...]

        for hh in range(hps):
            lanes = slice(hh * hd, (hh + 1) * hd)
            wa, wx = wa_ref[hh], wx_ref[hh]
            hs = hs_ref[:, lanes]
            u = _conv(xbuf, cw_ref, vec_ref, lanes, ts)
            ub, r, ig, sp, log_a = _rnn_gates(u, wa, wx, vec_ref, lanes)
            a = jnp.exp(log_a)
            e2 = jnp.exp(2.0 * log_a)
            one_minus_a2 = _neg_expm1(2.0 * log_a)
            inv_mult = lax.rsqrt(one_minus_a2)
            mult = one_minus_a2 * inv_mult

            g = proj_ref[1, :, lanes]
            sg = _sigmoid(g)
            dyc = dy_ref[:, lanes]
            dproj_ref[1, :, lanes] = (dyc * hs * (sg * (1.0 + g * (1.0 - sg)))).astype(BF16)

            row = lax.broadcasted_iota(jnp.int32, (ts, hd), 0)
            dhs = dyc * (g * sg) + jnp.where(row == ts - 1, carry[0:1, lanes], 0.0)
            dh = _scan_rev(_shift_up(a, 1, 0.0), dhs)
            carry[:, lanes] = (a * dh)[0:8, :]

            h_prev = hbuf[pl.ds(HALO - 1, ts), lanes]
            dlog_a = dh * h_prev * a - dh * (ig * u) * (e2 * inv_mult)
            di = dh * mult * u
            dzr = dlog_a * ((-LRU_C) * sp) * (r * (1.0 - r))
            dzi = di * (ig * (1.0 - ig))
            dvec_ref[3:4, lanes] += _colsum(dlog_a * r) * (LRU_C * _sigmoid(-vec_ref[3:4, lanes]))
            dvec_ref[1:2, lanes] += _colsum(dzr)
            dvec_ref[2:3, lanes] += _colsum(dzi)
            dzr_b, dzi_b = dzr.astype(BF16), dzi.astype(BF16)
            dw_acc[0, hh] += lax.dot_general(ub, dzr_b, TN_DIMS, preferred_element_type=F32)
            dw_acc[1, hh] += lax.dot_general(ub, dzi_b, TN_DIMS, preferred_element_type=F32)
            du = (dh * mult * ig
                  + lax.dot_general(dzr_b, wa, NT_DIMS, preferred_element_type=F32)
                  + lax.dot_general(dzi_b, wx, NT_DIMS, preferred_element_type=F32))
            dvec_ref[0:1, lanes] += _colsum(du)
            for k in range(CONV_WIDTH):
                dvec_ref[4 + k:5 + k, lanes] += _colsum(du * xbuf[pl.ds(HALO - (CONV_WIDTH - 1) + k, ts), lanes])

            dubuf[0:ts, lanes] = du
            dx = cw_ref[CONV_WIDTH - 1:CONV_WIDTH, lanes] * du
            for k in range(CONV_WIDTH - 1):
                dx = dx + cw_ref[k:k + 1, lanes] * dubuf[pl.ds(CONV_WIDTH - 1 - k, ts), lanes]
            dproj_ref[0, :, lanes] = dx.astype(BF16)
        dubuf[pl.ds(ts, HALO), :] = dubuf[0:HALO, :]

        @pl.when(first_tile)
        def _():
            dgates_ref[...] = dw_acc[...].astype(BF16)

    rev = lambda h, j: (nt - 1 - j, h)
    return pl.pallas_call(
        body, name=f"rnn_bwd_l{layer}", grid=(nh // hps, nt),
        in_specs=[pl.BlockSpec((2, ts, wl), lambda h, j: (0, nt - 1 - j, h)),
                  pl.BlockSpec((1, HALO, wl), lambda h, j: (0, halo(j), h)),
                  pl.BlockSpec((ts, wl), rev),
                  pl.BlockSpec((HALO, wl), lambda h, j: (halo(j), h)),
                  pl.BlockSpec((ts, wl), rev),
                  pl.BlockSpec((CONV_WIDTH, wl), lambda h, j: (0, h)),
                  pl.BlockSpec((8, wl), lambda h, j: (0, h)),
                  pl.BlockSpec((hps, hd, hd), lambda h, j: (h, 0, 0)),
                  pl.BlockSpec((hps, hd, hd), lambda h, j: (h, 0, 0))],
        out_specs=[pl.BlockSpec((2, ts, wl), lambda h, j: (0, nt - 1 - j, h)),
                   pl.BlockSpec((2, hps, hd, hd), lambda h, j: (0, h, 0, 0)),
                   pl.BlockSpec((16, wl), lambda h, j: (0, h))],
        out_shape=[jax.ShapeDtypeStruct((4, s, d), BF16),
                   jax.ShapeDtypeStruct((2, nh, hd, hd), BF16),
                   jax.ShapeDtypeStruct((16, d), F32)],
        scratch_shapes=[pltpu.VMEM((ts + HALO, wl), F32), pltpu.VMEM((ts + HALO, wl), F32),
                        pltpu.VMEM((ts + HALO, wl), F32), pltpu.VMEM((8, wl), F32),
                        pltpu.VMEM((2, hps, hd, hd), F32)],
        compiler_params=_params(2),
    )(proj, proj, hs, hs, dycat, cw, vec, wa, wx)


def _pool_bwd(proj, dycat, dproj, pw, vec, layer):
    _, s, d = proj.shape
    ng, gd, _ = pw.shape
    ts = _tile(s, MATMUL_ROWS)
    nt = s // ts
    halo = _halo_index(ts, nt)

    def body(proj_ref, xh_ref, dy_ref, dproj_in, pw_ref, vec_ref, dproj_ref, dpw_ref, dvec_ref, xbuf, qbuf, acc):
        del dproj_in
        j = pl.program_id(0)
        i = nt - 1 - j

        @pl.when(j == 0)
        def _():
            qbuf[pl.ds(ts, HALO), :] = jnp.zeros((HALO, d), F32)
            acc[...] = jnp.zeros_like(acc)
            dvec_ref[...] = jnp.zeros_like(dvec_ref)

        xbuf[0:HALO, :] = jnp.where(i == 0, 0.0, xh_ref[0])
        xbuf[pl.ds(HALO, ts), :] = proj_ref[0]
        for g in range(ng):
            lanes = slice(g * gd, (g + 1) * gd)
            win = 2 << g
            xt = proj_ref[0, :, lanes]
            inv_cnt = _inv_count(i, ts, gd, win)
            pooled = _pooled(xbuf, xt, lanes, win, inv_cnt, ts).astype(BF16)
            z = jnp.dot(pooled, pw_ref[g], preferred_element_type=F32) + vec_ref[0:1, lanes]
            scale = vec_ref[1:2, lanes]
            gg = proj_ref[1, :, lanes]
            sg = _sigmoid(gg)
            dyc = dy_ref[:, lanes]
            dyp = dyc * (gg * sg)
            dproj_ref[1, :, lanes] = (dyc * (z * scale) * (sg * (1.0 + gg * (1.0 - sg)))).astype(BF16)
            dvec_ref[1:2, lanes] += _colsum(dyp * z)
            dz = dyp * scale
            dvec_ref[0:1, lanes] += _colsum(dz)
            dz_b = dz.astype(BF16)
            acc[g] += lax.dot_general(pooled, dz_b, TN_DIMS, preferred_element_type=F32)
            dpooled = lax.dot_general(dz_b, pw_ref[g], NT_DIMS, preferred_element_type=F32)

            q = dpooled * inv_cnt
            qbuf[0:ts, lanes] = q
            dx = q - dpooled
            for dlt in range(1, win):
                dx = dx + qbuf[pl.ds(dlt, ts), lanes]
            dproj_ref[0, :, lanes] = dx.astype(BF16)
        qbuf[pl.ds(ts, HALO), :] = qbuf[0:HALO, :]

        @pl.when(j == nt - 1)
        def _():
            dpw_ref[...] = acc[...].astype(BF16)

    return pl.pallas_call(
        body, name=f"pool_bwd_l{layer}", grid=(nt,),
        in_specs=[pl.BlockSpec((2, ts, d), lambda j: (1, nt - 1 - j, 0)),
                  pl.BlockSpec((1, HALO, d), lambda j: (2, halo(j), 0)),
                  pl.BlockSpec((ts, d), lambda j: (nt - 1 - j, 1)),
                  ANY,
                  pl.BlockSpec((ng, gd, gd), lambda j: (0, 0, 0)),
                  pl.BlockSpec((8, d), lambda j: (0, 0))],
        out_specs=[pl.BlockSpec((2, ts, d), lambda j: (1, nt - 1 - j, 0)),
                   pl.BlockSpec((ng, gd, gd), lambda j: (0, 0, 0)),
                   pl.BlockSpec((8, d), lambda j: (0, 0))],
        out_shape=[jax.ShapeDtypeStruct((4, s, d), BF16),
                   jax.ShapeDtypeStruct((ng, gd, gd), BF16),
                   jax.ShapeDtypeStruct((8, d), F32)],
        input_output_aliases={3: 0},
        scratch_shapes=[pltpu.VMEM((ts + HALO, d), F32), pltpu.VMEM((ts + HALO, d), F32),
                        pltpu.VMEM((ng, gd, gd), F32)],
        compiler_params=_params(1),
    )(proj, proj, dycat, dproj, pw, vec)


def _inproj_bwd_x(dproj, w_all, x, dxo, vec, layer):
    s, d = x.shape
    p = w_all.shape[2]
    ts = _tile(s, MATMUL_ROWS)

    def body(dp_ref, w_ref, x_ref, dxo_ref, vec_ref, dx_ref, dvec_ref):
        @pl.when(pl.program_id(0) == 0)
        def _():
            dvec_ref[...] = jnp.zeros_like(dvec_ref)

        dh = lax.dot_general(dp_ref[0], w_ref[0], NT_DIMS, preferred_element_type=F32)
        for k in range(1, 4):
            dh = dh + lax.dot_general(dp_ref[k], w_ref[k], NT_DIMS, preferred_element_type=F32)
        _, xn, rs = _prenorm(x_ref[...], vec_ref)
        gpre, scale1 = vec_ref[3:4, :], 1.0 + vec_ref[1:2, :]
        dvec_ref[0:1, :] += _colsum(dh)
        dvec_ref[1:2, :] += _colsum(dh * (xn * gpre))
        dvec_ref[2:3, :] += _colsum(dh * (xn * scale1))
        t = dh * (gpre * scale1)
        dx_ref[...] = dxo_ref[...] + rs * (t - xn * jnp.mean(t * xn, axis=-1, keepdims=True))

    row = pl.BlockSpec((ts, d), lambda i: (i, 0))
    return pl.pallas_call(
        body, name=f"inproj_bwd_x_l{layer}", grid=(s // ts,),
        in_specs=[pl.BlockSpec((4, ts, p), lambda i: (0, i, 0)),
                  pl.BlockSpec((4, d, p), lambda i: (0, 0, 0)),
                  row, row, pl.BlockSpec((8, d), lambda i: (0, 0))],
        out_specs=[row, pl.BlockSpec((8, d), lambda i: (0, 0))],
        out_shape=[jax.ShapeDtypeStruct((s, d), F32), jax.ShapeDtypeStruct((8, d), F32)],
        compiler_params=_params(1),
    )(dproj, w_all, x, dxo, vec)


def _inproj_bwd_w(dproj, x, vec, layer):
    s, d = x.shape
    p = dproj.shape[2]
    ts = _tile(s, MATMUL_ROWS)
    nt = s // ts

    def body(dp_ref, x_ref, vec_ref, dw_ref, acc):
        i = pl.program_id(0)

        @pl.when(i == 0)
        def _():
            acc[...] = jnp.zeros_like(acc)

        h, _, _ = _prenorm(x_ref[...], vec_ref)
        hb = h.astype(BF16)
        for k in range(4):
            acc[k] += lax.dot_general(hb, dp_ref[k], TN_DIMS, preferred_element_type=F32)

        @pl.when(i == nt - 1)
        def _():
            dw_ref[...] = acc[...].astype(BF16)

    return pl.pallas_call(
        body, name=f"inproj_bwd_w_l{layer}", grid=(nt,),
        in_specs=[pl.BlockSpec((4, ts, p), lambda i: (0, i, 0)),
                  pl.BlockSpec((ts, d), lambda i: (i, 0)),
                  pl.BlockSpec((8, d), lambda i: (0, 0))],
        out_specs=pl.BlockSpec((4, d, p), lambda i: (0, 0, 0)),
        out_shape=jax.ShapeDtypeStruct((4, d, p), BF16),
        scratch_shapes=[pltpu.VMEM((4, d, p), F32)],
        compiler_params=_params(1),
    )(dproj, x, vec)


def _sum_slots(stacked, name):
    n, rows, cols = stacked.shape
    tr = _row_tile(rows)

    def body(in_ref, out_ref):
        total = in_ref[0].astype(F32)
        for b in range(1, n):
            total = total + in_ref[b].astype(F32)
        out_ref[...] = total

    return pl.pallas_call(
        body, name=name, grid=(rows // tr,),
        in_specs=[pl.BlockSpec((n, tr, cols), lambda i: (0, i, 0))],
        out_specs=pl.BlockSpec((tr, cols), lambda i: (i, 0)),
        out_shape=jax.ShapeDtypeStruct((rows, cols), F32),
        compiler_params=_params(1),
    )(stacked)


def _adamw(w, m, v, grads, name):
    shape = w.shape
    cols = shape[-1]
    rows = w.size // cols
    tr = _row_tile(rows)
    n = len(grads)

    def body(*refs):
        w_ref, m_ref, v_ref = refs[:3]
        g_refs = refs[3:3 + n]
        g_out, d_out, m_out, v_out = refs[3 + n:]
        g = g_refs[0][...]
        for r in g_refs[1:]:
            g = g + r[...]
        m_new = ADAM_B1 * m_ref[...] + (1.0 - ADAM_B1) * g
        v_new = ADAM_B2 * v_ref[...] + (1.0 - ADAM_B2) * (g * g)
        m_hat = m_new / (1.0 - ADAM_B1 ** ADAM_STEP)
        v_hat = v_new / (1.0 - ADAM_B2 ** ADAM_STEP)
        g_out[...] = g
        d_out[...] = (-ADAM_LR) * (m_hat / (jnp.sqrt(v_hat) + ADAM_EPS) + ADAM_WD * w_ref[...])
        m_out[...] = m_new
        v_out[...] = v_new

    blk = pl.BlockSpec((tr, cols), lambda i: (i, 0))
    outs = pl.pallas_call(
        body, name=name, grid=(rows // tr,),
        in_specs=[blk] * (3 + n), out_specs=[blk] * 4,
        out_shape=[jax.ShapeDtypeStruct((rows, cols), F32)] * 4,
        compiler_params=_params(1),
    )(*[a.reshape(rows, cols) for a in (w, m, v, *grads)])
    return tuple(o.reshape(shape) for o in outs)


def _adam_update(w, m, v, g):
    m_new = ADAM_B1 * m + (1.0 - ADAM_B1) * g
    v_new = ADAM_B2 * v + (1.0 - ADAM_B2) * (g * g)
    m_hat = m_new / (1.0 - ADAM_B1 ** ADAM_STEP)
    v_hat = v_new / (1.0 - ADAM_B2 ** ADAM_STEP)
    return (-ADAM_LR) * (m_hat / (jnp.sqrt(v_hat) + ADAM_EPS) + ADAM_WD * w), m_new, v_new


def _adamw_layer(w, m, v, grads, layer, prev, name, grad_row_offset=0):
    nl = w.shape[0]
    cols = w.shape[-1]
    rows = w.size // (nl * cols)
    tr = _row_tile(rows)
    off = layer * (rows // tr)
    g_off = grad_row_offset // tr
    n = len(grads)
    n_prev = 0 if prev is None else 4

    def body(*refs):
        w_ref, m_ref, v_ref = refs[:3]
        g_refs = refs[3:3 + n]
        g_out, d_out, m_out, v_out = refs[3 + n + n_prev:]
        g = g_refs[0][...]
        for r in g_refs[1:]:
            g = g + r[...]
        g_out[...] = g
        d_out[...], m_out[...], v_out[...] = _adam_update(w_ref[...], m_ref[...], v_ref[...], g)

    mine = pl.BlockSpec((tr, cols), lambda i: (off + i, 0))
    args = [a.reshape(nl * rows, cols) for a in (w, m, v)] + [g.reshape(-1, cols) for g in grads]
    outs = pl.pallas_call(
        body, name=name, grid=(rows // tr,),
        in_specs=[mine] * 3 + [pl.BlockSpec((tr, cols), lambda i: (g_off + i, 0))] * n + [ANY] * n_prev,
        out_specs=[mine] * 4,
        out_shape=[jax.ShapeDtypeStruct((nl * rows, cols), F32)] * 4,
        input_output_aliases={3 + n + k: k for k in range(n_prev)},
        compiler_params=_params(1),
    )(*args, *(prev or ()))
    return tuple(outs)


def _into_slot(a, dtype, chip_arr, name):
    rows, cols = a.shape
    tr = _row_tile(rows)

    def body(chip_ref, a_ref, out_ref):
        del chip_ref
        out_ref[...] = a_ref[...].astype(dtype)

    return pl.pallas_call(
        body, name=name,
        grid_spec=pltpu.PrefetchScalarGridSpec(
            num_scalar_prefetch=1, grid=(rows // tr,),
            in_specs=[pl.BlockSpec((tr, cols), lambda i, chip: (i, 0))],
            out_specs=pl.BlockSpec((None, tr, cols), lambda i, chip: (chip[0], i, 0))),
        out_shape=jax.ShapeDtypeStruct((4, rows, cols), dtype),
        compiler_params=_params(1),
    )(chip_arr, a)


def _sum_owner(own, land, chip_arr, own_block, own_index, name):
    blk = land.shape[1:]
    tr = _row_tile(blk[-2])
    steps = blk[-2] // tr
    tile = (*blk[:-2], tr, blk[-1])
    lead = (0,) * (len(blk) - 2)

    def body(chip_ref, own_ref, l1, l2, l3, out_ref):
        del chip_ref
        out_ref[...] = (own_ref[...].astype(F32) + l1[...].astype(F32)) + (l2[...].astype(F32) + l3[...].astype(F32))

    def landed(k):
        return pl.BlockSpec((None, *tile), lambda i, chip: (chip[0] ^ k, *lead, i, 0))

    return pl.pallas_call(
        body, name=name,
        grid_spec=pltpu.PrefetchScalarGridSpec(
            num_scalar_prefetch=1, grid=(steps,),
            in_specs=[pl.BlockSpec(own_block(tr), own_index), landed(1), landed(2), landed(3)],
            out_specs=pl.BlockSpec(tile, lambda i, chip: (*lead, i, 0))),
        out_shape=jax.ShapeDtypeStruct(blk, F32),
        compiler_params=_params(1),
    )(chip_arr, own, land, land, land)


_WHOLE_VMEM = pltpu.CompilerParams(vmem_limit_bytes=V7X_VMEM_LIMIT_BYTES)


def _pack_vectors(modbuf, ada_b, pre_norm_g, post_norm_g, conv_b, gate_a_b, gate_x_b, lru_lambda):
    nl, d = pre_norm_g.shape
    n = modbuf.shape[2] // nl
    nh, hd = gate_a_b.shape[1], gate_a_b.shape[2]

    def body(mb_ref, ab_ref, pre_ref, post_ref, cb_ref, gab_ref, gxb_ref, lam_ref, *outs):
        for layer in range(nl):
            vec_ref, rvec_ref = outs[layer], outs[nl + layer]
            vec_ref[...] = jnp.zeros_like(vec_ref)
            rvec_ref[...] = jnp.zeros_like(rvec_ref)
            for k in range(4):
                piece = mb_ref[k, 0:1, layer * n:(layer + 1) * n] + ab_ref[layer:layer + 1, k * n:(k + 1) * n]
                lo = k * n
                while lo < (k + 1) * n:
                    row = lo // d
                    hi = min((row + 1) * d, (k + 1) * n)
                    vec_ref[row:row + 1, lo - row * d:hi - row * d] = piece[:, lo - k * n:hi - k * n]
                    lo = hi
            vec_ref[3:4, :] = pre_ref[layer:layer + 1, :]
            vec_ref[4:5, :] = post_ref[layer:layer + 1, :]
            rvec_ref[0:1, :] = cb_ref[layer:layer + 1, :]
            for h in range(nh):
                rvec_ref[1:2, h * hd:(h + 1) * hd] = gab_ref[layer, h:h + 1, :]
                rvec_ref[2:3, h * hd:(h + 1) * hd] = gxb_ref[layer, h:h + 1, :]
            rvec_ref[3:4, :] = lam_ref[layer:layer + 1, :]

    out = pl.pallas_call(
        body, name="pack_vectors", in_specs=[VMEM] * 8, out_specs=[VMEM] * (2 * nl),
        out_shape=[jax.ShapeDtypeStruct((8, d), F32)] * (2 * nl), compiler_params=_WHOLE_VMEM,
    )(modbuf, ada_b, pre_norm_g, post_norm_g, conv_b, gate_a_b, gate_x_b, lru_lambda)
    return list(out[:nl]), list(out[nl:])


def _pack_gathered(convw_g, poolb_g, pws, pool_scale, ng):
    nl, d = pool_scale.shape
    taps = convw_g.shape[1] // nl
    dq = convw_g.shape[2]
    gq, gd = poolb_g.shape[2], pws[0].shape[2]

    def body(cg_ref, pb_ref, *rest):
        pw_refs, ps_ref = rest[:nl], rest[nl]
        outs = rest[nl + 1:]
        for layer in range(nl):
            cw_ref, pvec_ref, pwf_ref = outs[layer], outs[nl + layer], outs[2 * nl + layer]
            pvec_ref[...] = jnp.zeros_like(pvec_ref)
            pvec_ref[1:2, :] = ps_ref[layer:layer + 1, :]
            for k in range(4):
                cw_ref[:, k * dq:(k + 1) * dq] = cg_ref[k, layer * taps:(layer + 1) * taps, :]
                for g in range(ng):
                    lo = g * gd + k * gq
                    pvec_ref[0:1, lo:lo + gq] = pb_ref[k, layer * ng + g:layer * ng + g + 1, :]
                    pwf_ref[g, k * gq:(k + 1) * gq, :] = pw_refs[layer][k, g * gq:(g + 1) * gq, :]

    out = pl.pallas_call(
        body, name="pack_gathered", in_specs=[VMEM] * (3 + nl), out_specs=[VMEM] * (3 * nl),
        out_shape=[jax.ShapeDtypeStruct((taps, d), F32)] * nl + [jax.ShapeDtypeStruct((8, d), F32)] * nl
        + [jax.ShapeDtypeStruct((ng, gd, gd), BF16)] * nl,
        compiler_params=_WHOLE_VMEM,
    )(convw_g, poolb_g, *pws, pool_scale)
    return list(out[:nl]), list(out[nl:2 * nl]), list(out[2 * nl:])


ROW_SHIFT, ROW_SCALE, ROW_PRE, ROW_GATE, ROW_POST = 0, 1, 2, 8, 9
ROW_CONV_B, ROW_GATE_A_B, ROW_GATE_X_B, ROW_LAMBDA, ROW_CONV_W = 16, 17, 18, 19, 20
ROW_POOL_B, ROW_POOL_SCALE, ROW_SQ = 32, 33, 40


def _adamw_small(totals, chip_arr, params):
    nl = len(totals)
    d = totals[0].shape[1]
    n_par = len(params)
    flat = [a for p in params for a in p]
    nh, hd = params[6][0].shape[1], params[6][0].shape[2]
    taps, dq = params[8][0].shape[1], params[8][0].shape[2]
    ng, gq = params[9][0].shape[1], params[9][0].shape[2]
    gd = d // ng

    def body(chip_ref, *refs):
        tot = refs[:nl]
        ins = refs[nl:nl + 3 * n_par]
        outs = refs[nl + 3 * n_par:]
        chip = chip_ref[0]

        def update(p, idx, g):
            delta, m_new, v_new = _adam_update(ins[3 * p][idx], ins[3 * p + 1][idx], ins[3 * p + 2][idx], g)
            outs[4 * p][idx] = g
            outs[4 * p + 1][idx] = delta
            outs[4 * p + 2][idx] = m_new
            outs[4 * p + 3][idx] = v_new

        def mine(candidates):
            g = candidates[0]
            for k in range(1, 4):
                g = jnp.where(chip == k, candidates[k], g)
            return g

        for layer in range(nl):
            t = tot[layer]
            row = (slice(layer, layer + 1), slice(None))
            for j, r in enumerate((ROW_SHIFT, ROW_SCALE, ROW_GATE)):
                update(0, (slice(layer, layer + 1), slice(j * d, (j + 1) * d)), t[r:r + 1, :])
            for p, r in ((1, ROW_PRE), (2, ROW_POST), (3, ROW_CONV_B), (4, ROW_LAMBDA), (5, ROW_POOL_SCALE)):
                update(p, row, t[r:r + 1, :])
            for h in range(nh):
                idx = (layer, slice(h, h + 1), slice(None))
                update(6, idx, t[ROW_GATE_A_B:ROW_GATE_A_B + 1, h * hd:(h + 1) * hd])
                update(7, idx, t[ROW_GATE_X_B:ROW_GATE_X_B + 1, h * hd:(h + 1) * hd])
            for k in range(taps):
                r = ROW_CONV_W + k
                update(8, (layer, slice(k, k + 1), slice(None)), mine([t[r:r + 1, c * dq:(c + 1) * dq] for c in range(4)]))
            for g in range(ng):
                cands = [t[ROW_POOL_B:ROW_POOL_B + 1, g * gd + c * gq:g * gd + (c + 1) * gq] for c in range(4)]
                update(9, (layer, slice(g, g + 1), slice(None)), mine(cands))

    out = pl.pallas_call(
        body, name="adamw_small",
        in_specs=[pl.BlockSpec(memory_space=pltpu.SMEM)] + [VMEM] * (nl + 3 * n_par),
        out_specs=[VMEM] * (4 * n_par),
        out_shape=[jax.ShapeDtypeStruct(p[0].shape, F32) for p in params for _ in range(4)],
        compiler_params=_WHOLE_VMEM,
    )(chip_arr, *totals, *flat)
    return [tuple(out[4 * p:4 * p + 4]) for p in range(n_par)]


def _ada_w_grad(c_t, dmod):
    d, nb = c_t.shape
    nl, _, n = dmod.shape

    def body(c_ref, dm_ref, out_ref):
        for layer in range(nl):
            total = c_ref[:, 0:1] * dm_ref[layer, 0:1, :]
            for b in range(1, nb):
                total = total + c_ref[:, b:b + 1] * dm_ref[layer, b:b + 1, :]
            out_ref[layer] = total

    return pl.pallas_call(
        body, name="ada_w_grad", in_specs=[VMEM, VMEM], out_specs=VMEM,
        out_shape=jax.ShapeDtypeStruct((nl, d, n), F32),
        compiler_params=pltpu.CompilerParams(vmem_limit_bytes=V7X_VMEM_LIMIT_BYTES),
    )(c_t, dmod)


def _place():
    x, y, c = lax.axis_index("x"), lax.axis_index("y"), lax.axis_index("c")
    return x, y, c


OTHER_CHIPS = ((1, 0), (0, 1), (1, 1))
OTHER_DEVICES = tuple((fx, fy, fc) for fx in (0, 1) for fy in (0, 1) for fc in (0, 1))[1:]


def _mod_exchange(c_row, ada_w):
    nl, d, n = ada_w.shape

    def body(c_ref, w_ref, cbuf, modbuf, token, cblk, mres, send_a, recv_a, send_c, recv_c):
        token[...] = jnp.zeros_like(token)
        x, y, c = _place()
        me = 4 * x + 2 * y + c
        chip = 2 * x + y
        cv = c_ref[...]
        cblk[...] = jnp.zeros_like(cblk)
        cblk[0:1, :] = cv * _sigmoid(cv)

        def rows_of(dev):
            return cbuf.at[pl.ds(pl.multiple_of(8 * dev, 8), 8), :]

        cbuf[pl.ds(pl.multiple_of(8 * me, 8), 8), :] = cblk[...]
        sends = []
        for j, (fx, fy, fc) in enumerate(OTHER_DEVICES):
            cp = pltpu.make_async_remote_copy(
                src_ref=cblk, dst_ref=rows_of(me), send_sem=send_a.at[j], recv_sem=recv_a.at[j],
                device_id=(x ^ fx, y ^ fy, c ^ fc), device_id_type=MESH)
            cp.start()
            sends.append(cp)
        for j, (fx, fy, fc) in enumerate(OTHER_DEVICES):
            peer = 4 * (x ^ fx) + 2 * (y ^ fy) + (c ^ fc)
            pltpu.make_async_remote_copy(
                src_ref=cblk, dst_ref=rows_of(peer), send_sem=send_a.at[j], recv_sem=recv_a.at[j],
                device_id=(x ^ fx, y ^ fy, c ^ fc), device_id_type=MESH).wait_recv()
        for cp in sends:
            cp.wait_send()

        call = cbuf[...]
        for layer in range(nl):
            mres[:, layer * n:(layer + 1) * n] = jnp.dot(
                call, w_ref[layer], preferred_element_type=F32, precision=lax.Precision.HIGHEST)

        def block_of(dev):
            return mres.at[pl.ds(pl.multiple_of(8 * dev, 8), 8), :]

        modbuf[chip] = mres[pl.ds(pl.multiple_of(8 * me, 8), 8), :]
        sends = []
        for j, (fx, fy) in enumerate(OTHER_CHIPS):
            peer = 4 * (x ^ fx) + 2 * (y ^ fy) + c
            cp = pltpu.make_async_remote_copy(
                src_ref=block_of(peer), dst_ref=modbuf.at[chip], send_sem=send_c.at[j], recv_sem=recv_c.at[j],
                device_id=(x ^ fx, y ^ fy, c), device_id_type=MESH)
            cp.start()
            sends.append(cp)
        for j, (fx, fy) in enumerate(OTHER_CHIPS):
            pltpu.make_async_remote_copy(
                src_ref=block_of(me), dst_ref=modbuf.at[2 * (x ^ fx) + (y ^ fy)],
                send_sem=send_c.at[j], recv_sem=recv_c.at[j],
                device_id=(x ^ fx, y ^ fy, c), device_id_type=MESH).wait_recv()
        for cp in sends:
            cp.wait_send()

    return pl.pallas_call(
        body, name="mod_exchange", in_specs=[VMEM, VMEM], out_specs=[VMEM, VMEM, VMEM],
        out_shape=[jax.ShapeDtypeStruct((64, d), F32), jax.ShapeDtypeStruct((4, 8, nl * n), F32),
                   jax.ShapeDtypeStruct((8, 128), F32)],
        scratch_shapes=[pltpu.VMEM((8, d), F32), pltpu.VMEM((64, nl * n), F32),
                        pltpu.SemaphoreType.DMA((7,)), pltpu.SemaphoreType.DMA((7,)),
                        pltpu.SemaphoreType.DMA((3,)), pltpu.SemaphoreType.DMA((3,))],
        compiler_params=pltpu.CompilerParams(vmem_limit_bytes=V7X_VMEM_LIMIT_BYTES, has_side_effects=True),
    )(c_row, ada_w)


def _in_hbm(a):
    return pltpu.with_memory_space_constraint(a, pltpu.HBM)


def _gather_start(lands, groups):
    n, ngr = len(lands), len(groups)

    def body(*refs):
        land = refs[:n]
        sems = refs[n:n + 2 * ngr]
        token = refs[-1]
        x, y, c = _place()
        chip = 2 * x + y
        for gi, idxs in enumerate(groups):
            for t, i in enumerate(idxs):
                for j, (fx, fy) in enumerate(OTHER_CHIPS):
                    pltpu.make_async_remote_copy(
                        src_ref=land[i].at[chip], dst_ref=land[i].at[chip],
                        send_sem=sems[2 * gi].at[3 * t + j], recv_sem=sems[2 * gi + 1].at[3 * t + j],
                        device_id=(x ^ fx, y ^ fy, c), device_id_type=MESH).start()
        token[...] = jnp.zeros_like(token)

    sem_shapes = []
    for idxs in groups:
        sem_shapes += [pltpu.SemaphoreType.DMA((3 * len(idxs),))] * 2
    out = pl.pallas_call(
        body, name="weight_gather_start",
        in_specs=[HBM] * n, out_specs=[SEM] * (2 * ngr) + [HBM] * n + [VMEM],
        out_shape=sem_shapes + [pltpu.HBM(a.shape, a.dtype) for a in lands] + [jax.ShapeDtypeStruct((8, 128), F32)],
        input_output_aliases={i: 2 * ngr + i for i in range(n)},
        compiler_params=pltpu.CompilerParams(has_side_effects=DATAFLOW_EFFECT),
    )(*[_in_hbm(a) for a in lands])
    sems = [(out[2 * gi], out[2 * gi + 1]) for gi in range(ngr)]
    return sems, list(out[2 * ngr:2 * ngr + n]), out[-1]


def _gather_wait(lands, sems, after, name):
    n = len(lands)

    def body(*refs):
        land = refs[:n]
        send_sems, recv_sems = refs[n], refs[n + 1]
        x, y, c = _place()
        chip = 2 * x + y
        for t in range(n):
            for j, (fx, fy) in enumerate(OTHER_CHIPS):
                cp = pltpu.make_async_remote_copy(
                    src_ref=land[t].at[chip], dst_ref=land[t].at[2 * (x ^ fx) + (y ^ fy)],
                    send_sem=send_sems.at[3 * t + j], recv_sem=recv_sems.at[3 * t + j],
                    device_id=(x ^ fx, y ^ fy, c), device_id_type=MESH)
                cp.wait_send()
                cp.wait_recv()

    out = pl.pallas_call(
        body, name=name,
        in_specs=[HBM] * n + [SEM, SEM, ANY], out_specs=[HBM] * n,
        out_shape=[pltpu.HBM(a.shape, a.dtype) for a in lands],
        input_output_aliases={i: i for i in range(n)},
        compiler_params=pltpu.CompilerParams(has_side_effects=DATAFLOW_EFFECT),
    )(*lands, sems[0], sems[1], after)
    return list(out)


def _to_owner_copies(pairs, q):
    x, y, c = _place()
    chip = 2 * x + y
    out = []
    for t, (part, land) in enumerate(pairs):
        for j, (fx, fy) in enumerate(OTHER_CHIPS):
            owner = 2 * (x ^ fx) + (y ^ fy)
            if part.shape[0] == 4 and part.shape[1:] == land.shape[1:]:
                src = part.at[owner]
            else:
                src = part.at[:, pl.ds(pl.multiple_of(owner * q, q), q), :]
            out.append((src, land.at[chip], land.at[owner], (x ^ fx, y ^ fy, c), 3 * t + j))
    return out


def _to_all_copies(bufs, first_sem):
    x, y, c = _place()
    me = 4 * x + 2 * y + c
    out = []
    for t, buf in enumerate(bufs):
        for j, (fx, fy, fc) in enumerate(OTHER_DEVICES):
            them = 4 * (x ^ fx) + 2 * (y ^ fy) + (c ^ fc)
            out.append((buf.at[me], buf.at[me], buf.at[them], (x ^ fx, y ^ fy, c ^ fc), first_sem + 7 * t + j))
    return out


def _exchange_copies(refs, n_owner, q):
    pairs = list(zip(refs[:n_owner], refs[n_owner:2 * n_owner]))
    return _to_owner_copies(pairs, q) + _to_all_copies(refs[2 * n_owner:], 3 * n_owner)


def _exchange_start(arrays, n_owner, q, name):
    n = len(arrays)
    n_sems = 3 * n_owner + 7 * (n - 2 * n_owner)

    def body(*refs):
        send_sems, recv_sems = refs[n], refs[n + 1]
        for src, dst, _, peer, k in _exchange_copies(refs[:n], n_owner, q):
            pltpu.make_async_remote_copy(src_ref=src, dst_ref=dst, send_sem=send_sems.at[k], recv_sem=recv_sems.at[k],
                                         device_id=peer, device_id_type=MESH).start()
        refs[-1][...] = jnp.zeros_like(refs[-1])

    out = pl.pallas_call(
        body, name=name,
        in_specs=[HBM] * n, out_specs=[SEM, SEM] + [HBM] * n + [VMEM],
        out_shape=[pltpu.SemaphoreType.DMA((n_sems,))] * 2 + [pltpu.HBM(a.shape, a.dtype) for a in arrays]
        + [jax.ShapeDtypeStruct((8, 128), F32)],
        input_output_aliases={i: 2 + i for i in range(n)},
        compiler_params=pltpu.CompilerParams(has_side_effects=DATAFLOW_EFFECT),
    )(*[_in_hbm(a) for a in arrays])
    return (out[0], out[1]), list(out[2:2 + n]), out[-1]


def _exchange_wait(arrays, sems, n_owner, q, after, name):
    n = len(arrays)

    def body(*refs):
        send_sems, recv_sems = refs[n], refs[n + 1]
        for src, _, landed, peer, k in _exchange_copies(refs[:n], n_owner, q):
            cp = pltpu.make_async_remote_copy(src_ref=src, dst_ref=landed, send_sem=send_sems.at[k], recv_sem=recv_sems.at[k],
                                              device_id=peer, device_id_type=MESH)
            cp.wait_send()
            cp.wait_recv()

    out = pl.pallas_call(
        body, name=name,
        in_specs=[HBM] * n + [SEM, SEM, ANY], out_specs=[HBM] * n,
        out_shape=[pltpu.HBM(a.shape, a.dtype) for a in arrays],
        input_output_aliases={i: i for i in range(n)},
        compiler_params=pltpu.CompilerParams(has_side_effects=DATAFLOW_EFFECT),
    )(*arrays, sems[0], sems[1], after)
    return list(out)


def _sibling_swap(parts, layer):
    n = len(parts)

    def body(*refs):
        srcs, outs = refs[:n], refs[n:2 * n]
        send_sems, recv_sems = refs[2 * n:]
        x, y, c = _place()
        cps = [pltpu.make_async_remote_copy(
            src_ref=srcs[i], dst_ref=outs[i], send_sem=send_sems.at[i], recv_sem=recv_sems.at[i],
            device_id=(x, y, 1 - c), device_id_type=MESH) for i in range(n)]
        for cp in cps:
            cp.start()
        for cp in cps:
            cp.wait()

    return pl.pallas_call(
        body, name=f"sibling_swap_l{layer}", in_specs=[ANY] * n, out_specs=[ANY] * n,
        out_shape=[jax.ShapeDtypeStruct(a.shape, a.dtype) for a in parts],
        scratch_shapes=[pltpu.SemaphoreType.DMA((n,)), pltpu.SemaphoreType.DMA((n,))],
        compiler_params=pltpu.CompilerParams(has_side_effects=True),
    )(*parts)


def kernel(x, c, ada_w, ada_b, pre_norm_g, w_in, conv_w, conv_b, gate_a_w, gate_a_b, gate_x_w, gate_x_b, lru_lambda, pool_w, pool_b, pool_scale, w_out, post_norm_g, loss_target, m_ada_w, m_ada_b, m_pre_norm_g, m_w_in, m_conv_w, m_conv_b, m_gate_a_w, m_gate_a_b, m_gate_x_w, m_gate_x_b, m_lru_lambda, m_pool_w, m_pool_b, m_pool_scale, m_w_out, m_post_norm_g, v_ada_w, v_ada_b, v_pre_norm_g, v_w_in, v_conv_w, v_conv_b, v_gate_a_w, v_gate_a_b, v_gate_x_w, v_gate_x_b, v_lru_lambda, v_pool_w, v_pool_b, v_pool_scale, v_w_out, v_post_norm_g):
    nl, d, n_ada = ada_w.shape
    s = x.shape[1]
    nh, hd = gate_a_w.shape[1], gate_a_w.shape[2]
    ng, gq, gd = pool_w.shape[1], pool_w.shape[2], pool_w.shape[3]
    me = 4 * lax.axis_index("x") + 2 * lax.axis_index("y") + lax.axis_index("c")
    chip = 2 * lax.axis_index("x") + lax.axis_index("y")
    chip_arr = jnp.reshape(chip, (1,)).astype(jnp.int32)
    x0 = x.reshape(s, d)
    target = loss_target.reshape(s, d)
    p_in = w_in.shape[2]
    r_out = w_out.shape[1]

    cbuf, modbuf, mod_token = _mod_exchange(c.reshape(1, d), ada_w)
    vecs, rvecs = _pack_vectors(modbuf, ada_b, pre_norm_g, post_norm_g, conv_b, gate_a_b, gate_x_b, lru_lambda)

    win = [_into_slot(w_in[l], BF16, chip_arr, f"slot_w_in_l{l}") for l in range(nl)]
    wout = [_into_slot(w_out[l], BF16, chip_arr, f"slot_w_out_l{l}") for l in range(nl)]
    pw = [_into_slot(pool_w[l].reshape(ng * gq, gd), BF16, chip_arr, f"slot_pool_w_l{l}") for l in range(nl)]
    convw = _into_slot(conv_w.reshape(nl * CONV_WIDTH, d // 4) + mod_token[0:1, 0:1], F32, chip_arr, "slot_conv_w")
    poolb = _into_slot(pool_b.reshape(nl * ng, gq), F32, chip_arr, "slot_pool_b")
    lands = [win[0], convw, poolb, *pw, wout[0]]
    groups = [[0], list(range(1, len(lands)))]
    for l in range(1, nl):
        groups.append([len(lands), len(lands) + 1])
        lands += [win[l], wout[l]]
    sems, lands, token = _gather_start(lands, groups)
    wa_b, wx_b = gate_a_w.astype(BF16), gate_x_w.astype(BF16)

    xs, projs, hss, ycats, ys = [x0], [], [], [], []
    sq = None
    convw_full = poolw_full = pvecs = None
    for l in range(nl):
        if l == 0:
            (win[0],) = _gather_wait([lands[0]], sems[0], modbuf, "weight_gather_wait_a")
        proj = _inproj_fwd(xs[l], vecs[l], win[l], l)
        if l == 0:
            got = _gather_wait(lands[1:len(groups[1]) + 1], sems[1], proj, "weight_gather_wait_b")
            wout[0] = got[2 + nl]
            convw_full, pvecs, poolw_full = _pack_gathered(got[0], got[1], got[2:2 + nl], pool_scale, ng)
        ycat, hs = _rnn_fwd(proj, convw_full[l], rvecs[l], wa_b[l], wx_b[l], l)
        if l + 1 < nl:
            base = len(groups[1]) + 1 + 2 * l
            win[l + 1], wout[l + 1] = _gather_wait(lands[base:base + 2], sems[2 + l], hs, f"weight_gather_wait_c{l + 1}")
        ycat = _pool_fwd(proj, ycat, poolw_full[l], pvecs[l], l)
        y, xo, sq = _outproj_fwd(ycat, wout[l], xs[l], vecs[l], target if l == nl - 1 else None, l)
        projs.append(proj), hss.append(hs), ycats.append(ycat), ys.append(y), xs.append(xo)

    def finish(l, flights, after, prev):
        (sems_a, arr_a), (sems_b, arr_b), (sems_c, arr_c) = flights
        dwout_l, dpw_l, rwout, rpw, gates = _exchange_wait(arr_a, sems_a, 2, gq, after, f"grad_wait_a_l{l}")
        dwin_l, rwin = _exchange_wait(arr_b, sems_b, 1, gq, gates, f"grad_wait_b_l{l}")
        (slabs,) = _exchange_wait(arr_c, sems_c, 0, gq, rwin, f"grad_wait_c_l{l}")
        p_win = _sum_owner(dwin_l, rwin, chip_arr, lambda tr: (None, tr, p_in),
                           lambda i, chip: (chip[0], i, 0), f"sum_w_in_l{l}")
        p_wout = _sum_owner(dwout_l, rwout, chip_arr, lambda tr: (None, tr, d),
                            lambda i, chip: (chip[0], i, 0), f"sum_w_out_l{l}")
        p_pw = _sum_owner(dpw_l, rpw, chip_arr, lambda tr: (ng, tr, gd),
                          lambda i, chip: (0, chip[0], 0), f"sum_pool_w_l{l}")
        q_win, q_wout, q_pw = _sibling_swap([p_win, p_wout, p_pw], l)
        g_gates = _sum_slots(gates.reshape(8, 2 * nh * hd, hd), f"sum_gates_l{l}")
        prev = prev or {}
        big = {
            "w_in": _adamw_layer(w_in, m_w_in, v_w_in, [p_win, q_win], l, prev.get("w_in"), f"adamw_w_in_l{l}"),
            "w_out": _adamw_layer(w_out, m_w_out, v_w_out, [p_wout, q_wout], l, prev.get("w_out"), f"adamw_w_out_l{l}"),
            "pool_w": _adamw_layer(pool_w, m_pool_w, v_pool_w, [p_pw, q_pw], l, prev.get("pool_w"), f"adamw_pool_w_l{l}"),
            "gate_a_w": _adamw_layer(gate_a_w, m_gate_a_w, v_gate_a_w, [g_gates], l, prev.get("gate_a_w"),
                                     f"adamw_gate_a_w_l{l}"),
            "gate_x_w": _adamw_layer(gate_x_w, m_gate_x_w, v_gate_x_w, [g_gates], l, prev.get("gate_x_w"),
                                     f"adamw_gate_x_w_l{l}", grad_row_offset=nh * hd),
        }
        return big, slabs, _sum_slots(slabs, f"sum_slab_l{l}")

    dx = xs[nl]
    flights = token = big = None
    slabs_all, totals = [None] * nl, [None] * nl
    for l in reversed(range(nl)):
        vec_l = vecs[l] if token is None else vecs[l] + token[0:1, 0:1]
        dycat, dwout_l, dvec_o = _outproj_bwd(dx, ys[l], ycats[l], wout[l], vec_l, l)
        dproj, dgates, dvec_r = _rnn_bwd(projs[l], hss[l], dycat, convw_full[l], rvecs[l], wa_b[l], wx_b[l], l)
        dproj, dpw_l, dvec_p = _pool_bwd(projs[l], dycat, dproj, poolw_full[l], pvecs[l], l)
        gatess = lax.dynamic_update_slice(lax.empty((8, *dgates.shape), BF16), dgates[None], (me, 0, 0, 0, 0))
        sems_a, arr_a, tok_a = _exchange_start(
            [dwout_l, dpw_l, lax.empty(dwout_l.shape, BF16), lax.empty((4, ng, gq, gd), BF16), gatess],
            2, gq, f"grad_start_a_l{l}")
        dwin_l = _inproj_bwd_w(dproj, xs[l], vec_l + tok_a[0:1, 0:1], l)
        sems_b, arr_b, tok_b = _exchange_start([dwin_l, lax.empty(dwin_l.shape, BF16)], 1, gq, f"grad_start_b_l{l}")
        dx, dvec_i = _inproj_bwd_x(dproj, win[l], xs[l], dx, vec_l + tok_b[0:1, 0:1], l)
        parts = [dvec_i, dvec_o, dvec_r, dvec_p]
        if l == nl - 1:
            parts.append(jnp.tile(sq, (1, d // sq.shape[1])))
        slab = jnp.concatenate(parts, axis=0)
        slabs = lax.dynamic_update_slice(lax.empty((8, *slab.shape), F32), slab[None], (me, 0, 0))
        sems_c, arr_c, token = _exchange_start([slabs], 0, gq, f"grad_start_c_l{l}")
        if flights is not None:
            big, slabs_all[l + 1], totals[l + 1] = finish(l + 1, flights, token, big)
        flights = ((sems_a, arr_a), (sems_b, arr_b), (sems_c, arr_c))
    big, slabs_all[0], totals[0] = finish(0, flights, big["w_in"][3] if big else dx, big)
    grad_x = dx.reshape(x.shape)
    loss = totals[nl - 1][ROW_SQ, 0] * (0.5 / d)

    small = _adamw_small(totals, chip_arr, [
        (ada_b, m_ada_b, v_ada_b), (pre_norm_g, m_pre_norm_g, v_pre_norm_g), (post_norm_g, m_post_norm_g, v_post_norm_g),
        (conv_b, m_conv_b, v_conv_b), (lru_lambda, m_lru_lambda, v_lru_lambda), (pool_scale, m_pool_scale, v_pool_scale),
        (gate_a_b, m_gate_a_b, v_gate_a_b), (gate_x_b, m_gate_x_b, v_gate_x_b),
        (conv_w, m_conv_w, v_conv_w), (pool_b, m_pool_b, v_pool_b)])

    per_dev = jnp.stack([jnp.concatenate([slabs_all[l][:, r] for r in (ROW_SHIFT, ROW_SCALE, ROW_GATE)], axis=-1)
                         for l in range(nl)])
    dmod_mine = lax.dynamic_slice_in_dim(per_dev, chip * n_ada, n_ada, axis=2)
    c_all_t = cbuf.reshape(8, 8, d)[:, 0, :].T
    g_ada_w = _ada_w_grad(c_all_t, dmod_mine)

    results = {
        "ada_w": _adamw(ada_w, m_ada_w, v_ada_w, [g_ada_w], "adamw_ada_w"),
        "ada_b": small[0],
        "pre_norm_g": small[1],
        "w_in": tuple(o.reshape(w_in.shape) for o in big["w_in"]),
        "conv_w": small[8],
        "conv_b": small[3],
        "gate_a_w": tuple(o.reshape(gate_a_w.shape) for o in big["gate_a_w"]),
        "gate_a_b": small[6],
        "gate_x_w": tuple(o.reshape(gate_x_w.shape) for o in big["gate_x_w"]),
        "gate_x_b": small[7],
        "lru_lambda": small[4],
        "pool_w": tuple(o.reshape(pool_w.shape) for o in big["pool_w"]),
        "pool_b": small[9],
        "pool_scale": small[5],
        "w_out": tuple(o.reshape(w_out.shape) for o in big["w_out"]),
        "post_norm_g": small[2],
    }
    names = list(results)
    return (loss, grad_x,
            *[results[n][0] for n in names], *[results[n][1] for n in names],
            *[results[n][2] for n in names], *[results[n][3] for n in names])
```

```python
import functools

import jax
import jax.numpy as jnp
from jax import lax
from jax.experimental import pallas as pl
from jax.experimental.pallas import tpu as pltpu

F32 = jnp.float32
BF16 = jnp.bfloat16

NORM_EPS = 1e-6
LRU_C = 8.0
CONV_WIDTH = 4
MAX_POOL_WINDOW = 16
HALO = 16
ADAM_LR = 0.001
ADAM_B1 = 0.9
ADAM_B2 = 0.999
ADAM_EPS = 1e-08
ADAM_WD = 0.01
ADAM_STEP = 10

V7X_VMEM_LIMIT_BYTES = 56 * 1024 * 1024
MATMUL_ROWS = 512
SCAN_ROWS = 512
ELEMENTWISE_ROWS = 512

MESH = pl.DeviceIdType.MESH
ANY = pl.BlockSpec(memory_space=pl.ANY)
VMEM = pl.BlockSpec(memory_space=pltpu.VMEM)
HBM = pl.BlockSpec(memory_space=pltpu.HBM)
SEM = pl.BlockSpec(memory_space=pltpu.SEMAPHORE)
DATAFLOW_EFFECT = pltpu.SideEffectType.DATAFLOW_SIDE_EFFECTING

NT_DIMS = (((1,), (1,)), ((), ()))
TN_DIMS = (((0,), (0,)), ((), ()))


def _params(n_grid_axes):
    return pltpu.CompilerParams(dimension_semantics=("arbitrary",) * n_grid_axes,
                                vmem_limit_bytes=V7X_VMEM_LIMIT_BYTES)


def _tile(total, want):
    t = min(want, max(total // 2, HALO))
    assert total % t == 0 and t % HALO == 0, (total, t)
    return t


def _row_tile(rows):
    for t in range(min(rows, ELEMENTWISE_ROWS) // 8 * 8, 0, -8):
        if rows % t == 0:
            return t
    return rows


def _sigmoid(z):
    return 1.0 / (1.0 + jnp.exp(-z))


def _softplus(z):
    return jnp.maximum(z, 0.0) + jnp.log(1.0 + jnp.exp(-jnp.abs(z)))


def _neg_expm1(z):
    return -jnp.tanh(0.5 * z) * (jnp.exp(z) + 1.0)


def _colsum(v):
    return jnp.sum(v, axis=0, keepdims=True)


def _prenorm(xt, vec_ref):
    rs = lax.rsqrt(jnp.mean(xt * xt, axis=-1, keepdims=True) + NORM_EPS)
    xn = xt * rs
    h = xn * vec_ref[3:4, :] * (1.0 + vec_ref[1:2, :]) + vec_ref[0:1, :]
    return h, xn, rs


def _shift_down(v, d, fill):
    t = v.shape[0]
    if d % 8 == 0:
        return jnp.concatenate([jnp.full((d, v.shape[1]), fill, v.dtype), v[:t - d]], axis=0)
    row = lax.broadcasted_iota(jnp.int32, v.shape, 0)
    return jnp.where(row >= d, pltpu.roll(v, d, 0), fill)


def _shift_up(v, d, fill):
    t = v.shape[0]
    if d % 8 == 0:
        return jnp.concatenate([v[d:], jnp.full((d, v.shape[1]), fill, v.dtype)], axis=0)
    row = lax.broadcasted_iota(jnp.int32, v.shape, 0)
    return jnp.where(row < t - d, pltpu.roll(v, t - d, 0), fill)


def _scan_fwd(a, v, h_before):
    d = 1
    while d < a.shape[0]:
        v = v + a * _shift_down(v, d, 0.0)
        a = a * _shift_down(a, d, 1.0)
        d *= 2
    return a * h_before + v


def _scan_rev(b, v):
    d = 1
    while d < b.shape[0]:
        v = v + b * _shift_up(v, d, 0.0)
        b = b * _shift_up(b, d, 0.0)
        d *= 2
    return v


def _inproj_fwd(x, vec, w_all, layer):
    s, d = x.shape
    p = w_all.shape[2]
    ts = _tile(s, MATMUL_ROWS)

    def body(x_ref, vec_ref, w_ref, proj_ref):
        h, _, _ = _prenorm(x_ref[...], vec_ref)
        hb = h.astype(BF16)
        for k in range(4):
            proj_ref[k] = jnp.dot(hb, w_ref[k], preferred_element_type=F32)

    return pl.pallas_call(
        body, name=f"inproj_fwd_l{layer}", grid=(s // ts,),
        in_specs=[pl.BlockSpec((ts, d), lambda i: (i, 0)),
                  pl.BlockSpec((8, d), lambda i: (0, 0)),
                  pl.BlockSpec((4, d, p), lambda i: (0, 0, 0))],
        out_specs=pl.BlockSpec((4, ts, p), lambda i: (0, i, 0)),
        out_shape=jax.ShapeDtypeStruct((4, s, p), F32),
        compiler_params=_params(1),
    )(x, vec, w_all)


HEADS_PER_STEP = 2
BWD_HEADS_PER_STEP = 1


def _rnn_gates(u, wa, wx, vec_ref, lanes):
    ub = u.astype(BF16)
    r = _sigmoid(jnp.dot(ub, wa, preferred_element_type=F32) + vec_ref[1:2, lanes])
    ig = _sigmoid(jnp.dot(ub, wx, preferred_element_type=F32) + vec_ref[2:3, lanes])
    sp = _softplus(-vec_ref[3:4, lanes])
    log_a = (-LRU_C) * r * sp
    return ub, r, ig, sp, log_a


def _conv(xbuf, cw_ref, vec_ref, lanes, ts):
    u = vec_ref[0:1, lanes] + cw_ref[CONV_WIDTH - 1:CONV_WIDTH, lanes] * xbuf[pl.ds(HALO, ts), lanes]
    for k in range(CONV_WIDTH - 1):
        u = u + cw_ref[k:k + 1, lanes] * xbuf[pl.ds(HALO - (CONV_WIDTH - 1) + k, ts), lanes]
    return u


def _rnn_fwd(proj, cw, vec, wa, wx, layer):
    _, s, d = proj.shape
    nh, hd, _ = wa.shape
    ts = _tile(s, SCAN_ROWS)
    hps = HEADS_PER_STEP
    wl = hps * hd

    def body(proj_ref, cw_ref, vec_ref, wa_ref, wx_ref, ycat_ref, hs_ref, xbuf, hlast):
        i = pl.program_id(1)

        @pl.when(i == 0)
        def _():
            xbuf[0:HALO, :] = jnp.zeros((HALO, wl), F32)
            hlast[...] = jnp.zeros_like(hlast)

        xbuf[pl.ds(HALO, ts), :] = proj_ref[0]
        for hh in range(hps):
            lanes = slice(hh * hd, (hh + 1) * hd)
            u = _conv(xbuf, cw_ref, vec_ref, lanes, ts)
            _, _, ig, _, log_a = _rnn_gates(u, wa_ref[hh], wx_ref[hh], vec_ref, lanes)
            a = jnp.exp(log_a)
            mult = jnp.sqrt(_neg_expm1(2.0 * log_a))
            hs = _scan_fwd(a, mult * (ig * u), hlast[0:1, lanes])
            hs_ref[:, lanes] = hs
            hlast[0:1, lanes] = hs_ref[ts - 1:ts, lanes]
            g = proj_ref[1, :, lanes]
            ycat_ref[:, lanes] = (hs * (g * _sigmoid(g))).astype(BF16)
        xbuf[0:HALO, :] = xbuf[pl.ds(ts, HALO), :]

    return pl.pallas_call(
        body, name=f"rnn_fwd_l{layer}", grid=(nh // hps, s // ts),
        in_specs=[pl.BlockSpec((2, ts, wl), lambda h, i: (0, i, h)),
                  pl.BlockSpec((CONV_WIDTH, wl), lambda h, i: (0, h)),
                  pl.BlockSpec((8, wl), lambda h, i: (0, h)),
                  pl.BlockSpec((hps, hd, hd), lambda h, i: (h, 0, 0)),
                  pl.BlockSpec((hps, hd, hd), lambda h, i: (h, 0, 0))],
        out_specs=[pl.BlockSpec((ts, wl), lambda h, i: (i, h)),
                   pl.BlockSpec((ts, wl), lambda h, i: (i, h))],
        out_shape=[jax.ShapeDtypeStruct((s, 2 * d), BF16), jax.ShapeDtypeStruct((s, d), F32)],
        scratch_shapes=[pltpu.VMEM((ts + HALO, wl), F32), pltpu.VMEM((8, wl), F32)],
        compiler_params=_params(2),
    )(proj, cw, vec, wa, wx)


def _inv_count(i, ts, lanes, win):
    t = i * ts + lax.broadcasted_iota(jnp.int32, (ts, lanes), 0)
    return 1.0 / jnp.minimum(t + 1, win).astype(F32)


def _pooled(xbuf, xt, lanes, win, inv_cnt, ts):
    acc = xt
    for dlt in range(1, win):
        acc = acc + xbuf[pl.ds(HALO - dlt, ts), lanes]
    return acc * inv_cnt - xt


def _pool_fwd(proj, ycat, pw, vec, layer):
    _, s, d = proj.shape
    ng, gd, _ = pw.shape
    ts = _tile(s, MATMUL_ROWS)

    def body(proj_ref, ycat_in, pw_ref, vec_ref, ycat_ref, xbuf):
        del ycat_in
        i = pl.program_id(0)

        @pl.when(i == 0)
        def _():
            xbuf[0:HALO, :] = jnp.zeros((HALO, d), F32)

        xbuf[pl.ds(HALO, ts), :] = proj_ref[0]
        for g in range(ng):
            lanes = slice(g * gd, (g + 1) * gd)
            win = 2 << g
            xt = proj_ref[0, :, lanes]
            pooled = _pooled(xbuf, xt, lanes, win, _inv_count(i, ts, gd, win), ts).astype(BF16)
            z = jnp.dot(pooled, pw_ref[g], preferred_element_type=F32) + vec_ref[0:1, lanes]
            gg = proj_ref[1, :, lanes]
            ycat_ref[:, lanes] = (z * vec_ref[1:2, lanes] * (gg * _sigmoid(gg))).astype(BF16)
        xbuf[0:HALO, :] = xbuf[pl.ds(ts, HALO), :]

    return pl.pallas_call(
        body, name=f"pool_fwd_l{layer}", grid=(s // ts,),
        in_specs=[pl.BlockSpec((2, ts, d), lambda i: (1, i, 0)),
                  ANY,
                  pl.BlockSpec((ng, gd, gd), lambda i: (0, 0, 0)),
                  pl.BlockSpec((8, d), lambda i: (0, 0))],
        out_specs=pl.BlockSpec((ts, d), lambda i: (i, 1)),
        out_shape=jax.ShapeDtypeStruct((s, 2 * d), BF16),
        input_output_aliases={1: 0},
        scratch_shapes=[pltpu.VMEM((ts + HALO, d), F32)],
        compiler_params=_params(1),
    )(proj, ycat, pw, vec)


def _outproj_fwd(ycat, w_all, x, vec, target, layer):
    s, d = x.shape
    nk, kd = w_all.shape[0], w_all.shape[1]
    ts = _tile(s, MATMUL_ROWS)
    last = target is not None

    def body(*refs):
        if last:
            ycat_ref, w_ref, x_ref, vec_ref, tgt_ref, y_ref, xo_ref, sq_ref = refs
        else:
            ycat_ref, w_ref, x_ref, vec_ref, y_ref, xo_ref = refs
        y = jnp.dot(ycat_ref[:, 0:kd], w_ref[0], preferred_element_type=F32)
        for k in range(1, nk):
            y = y + jnp.dot(ycat_ref[:, k * kd:(k + 1) * kd], w_ref[k], preferred_element_type=F32)
        y_ref[...] = y
        rs = lax.rsqrt(jnp.mean(y * y, axis=-1, keepdims=True) + NORM_EPS)
        xo = x_ref[...] + vec_ref[2:3, :] * (y * rs * vec_ref[4:5, :])
        if last:
            err = xo - tgt_ref[...]
            xo_ref[...] = err * (1.0 / d)

            @pl.when(pl.program_id(0) == 0)
            def _():
                sq_ref[...] = jnp.zeros_like(sq_ref)

            sq_ref[...] += jnp.sum(err * err)
        else:
            xo_ref[...] = xo

    row = pl.BlockSpec((ts, d), lambda i: (i, 0))
    in_specs = [pl.BlockSpec((ts, nk * kd), lambda i: (i, 0)),
                pl.BlockSpec((nk, kd, d), lambda i: (0, 0, 0)),
                row, pl.BlockSpec((8, d), lambda i: (0, 0))]
    out_specs = [row, row]
    out_shape = [jax.ShapeDtypeStruct((s, d), F32), jax.ShapeDtypeStruct((s, d), F32)]
    args = [ycat, w_all, x, vec]
    if last:
        in_specs.append(row)
        args.append(target)
        out_specs.append(pl.BlockSpec((8, 128), lambda i: (0, 0)))
        out_shape.append(jax.ShapeDtypeStruct((8, 128), F32))
    out = pl.pallas_call(
        body, name=f"outproj_fwd_l{layer}", grid=(s // ts,),
        in_specs=in_specs, out_specs=out_specs, out_shape=out_shape,
        compiler_params=_params(1),
    )(*args)
    return (out[0], out[1], out[2]) if last else (out[0], out[1], None)


def _outproj_bwd(dxo, y, ycat, w_all, vec, layer):
    s, d = dxo.shape
    nk, kd = w_all.shape[0], w_all.shape[1]
    ts = _tile(s, MATMUL_ROWS)
    nt = s // ts

    def body(dxo_ref, y_ref, ycat_ref, w_ref, vec_ref, dycat_ref, dw_ref, dvec_ref, acc):
        i = pl.program_id(0)

        @pl.when(i == 0)
        def _():
            acc[...] = jnp.zeros_like(acc)
            dvec_ref[...] = jnp.zeros_like(dvec_ref)

        yt = y_ref[...]
        rs = lax.rsqrt(jnp.mean(yt * yt, axis=-1, keepdims=True) + NORM_EPS)
        yhat = yt * rs
        gate, gpost = vec_ref[2:3, :], vec_ref[4:5, :]
        dxo_t = dxo_ref[...]
        dyn = dxo_t * gate
        dvec_ref[0:1, :] += _colsum(dxo_t * (yhat * gpost))
        dvec_ref[1:2, :] += _colsum(dyn * yhat)
        t = dyn * gpost
        dy = (rs * (t - yhat * jnp.mean(t * yhat, axis=-1, keepdims=True))).astype(BF16)
        for k in range(nk):
            cols = slice(k * kd, (k + 1) * kd)
            dycat_ref[:, cols] = lax.dot_general(dy, w_ref[k], NT_DIMS, preferred_element_type=F32)
            acc[k] += lax.dot_general(ycat_ref[:, cols], dy, TN_DIMS, preferred_element_type=F32)

        @pl.when(i == nt - 1)
        def _():
            dw_ref[...] = acc[...].astype(BF16)

    row = pl.BlockSpec((ts, d), lambda i: (i, 0))
    wide = pl.BlockSpec((ts, nk * kd), lambda i: (i, 0))
    return pl.pallas_call(
        body, name=f"outproj_bwd_l{layer}", grid=(nt,),
        in_specs=[row, row, wide,
                  pl.BlockSpec((nk, kd, d), lambda i: (0, 0, 0)),
                  pl.BlockSpec((8, d), lambda i: (0, 0))],
        out_specs=[wide,
                   pl.BlockSpec((nk, kd, d), lambda i: (0, 0, 0)),
                   pl.BlockSpec((8, d), lambda i: (0, 0))],
        out_shape=[jax.ShapeDtypeStruct((s, nk * kd), F32),
                   jax.ShapeDtypeStruct((nk, kd, d), BF16),
                   jax.ShapeDtypeStruct((8, d), F32)],
        scratch_shapes=[pltpu.VMEM((nk, kd, d), F32)],
        compiler_params=_params(1),
    )(dxo, y, ycat, w_all, vec)


def _halo_index(ts, nt):
    return lambda j: jnp.maximum((nt - 1 - j) * (ts // HALO) - 1, 0)


def _rnn_bwd(proj, hs, dycat, cw, vec, wa, wx, layer):
    _, s, d = proj.shape
    nh, hd, _ = wa.shape
    ts = _tile(s, SCAN_ROWS)
    nt = s // ts
    halo = _halo_index(ts, nt)
    hps = BWD_HEADS_PER_STEP
    wl = hps * hd

    def body(proj_ref, xh_ref, hs_ref, hsh_ref, dy_ref, cw_ref, vec_ref, wa_ref, wx_ref,
             dproj_ref, dgates_ref, dvec_ref, xbuf, hbuf, dubuf, carry, dw_acc):
        j = pl.program_id(1)
        first_tile = j == nt - 1

        @pl.when(j == 0)
        def _():
            dubuf[pl.ds(ts, HALO), :] = jnp.zeros((HALO, wl), F32)
            carry[...] = jnp.zeros_like(carry)
            dw_acc[...] = jnp.zeros_like(dw_acc)
            dvec_ref[...] = jnp.zeros_like(dvec_ref)

        xbuf[0:HALO, :] = jnp.where(first_tile, 0.0, xh_ref[0])
        xbuf[pl.ds(HALO, ts), :] = proj_ref[0]
        hbuf[0:HALO, :] = jnp.where(first_tile, 0.0, hsh_ref[...])
        hbuf[pl.ds(HALO, ts), :] = hs_ref[...]

        for hh in range(hps):
            lanes = slice(hh * hd, (hh + 1) * hd)
            wa, wx = wa_ref[hh], wx_ref[hh]
            hs = hs_ref[:, lanes]
            u = _conv(xbuf, cw_ref, vec_ref, lanes, ts)
            ub, r, ig, sp, log_a = _rnn_gates(u, wa, wx, vec_ref, lanes)
            a = jnp.exp(log_a)
            e2 = jnp.exp(2.0 * log_a)
            one_minus_a2 = _neg_expm1(2.0 * log_a)
            inv_mult = lax.rsqrt(one_minus_a2)
            mult = one_minus_a2 * inv_mult

            g = proj_ref[1, :, lanes]
            sg = _sigmoid(g)
            dyc = dy_ref[:, lanes]
            dproj_ref[1, :, lanes] = (dyc * hs * (sg * (1.0 + g * (1.0 - sg)))).astype(BF16)

            row = lax.broadcasted_iota(jnp.int32, (ts, hd), 0)
            dhs = dyc * (g * sg) + jnp.where(row == ts - 1, carry[0:1, lanes], 0.0)
            dh = _scan_rev(_shift_up(a, 1, 0.0), dhs)
            carry[:, lanes] = (a * dh)[0:8, :]

            h_prev = hbuf[pl.ds(HALO - 1, ts), lanes]
            dlog_a = dh * h_prev * a - dh * (ig * u) * (e2 * inv_mult)
            di = dh * mult * u
            dzr = dlog_a * ((-LRU_C) * sp) * (r * (1.0 - r))
            dzi = di * (ig * (1.0 - ig))
            dvec_ref[3:4, lanes] += _colsum(dlog_a * r) * (LRU_C * _sigmoid(-vec_ref[3:4, lanes]))
            dvec_ref[1:2, lanes] += _colsum(dzr)
            dvec_ref[2:3, lanes] += _colsum(dzi)
            dzr_b, dzi_b = dzr.astype(BF16), dzi.astype(BF16)
            dw_acc[0, hh] += lax.dot_general(ub, dzr_b, TN_DIMS, preferred_element_type=F32)
            dw_acc[1, hh] += lax.dot_general(ub, dzi_b, TN_DIMS, preferred_element_type=F32)
            du = (dh * mult * ig
                  + lax.dot_general(dzr_b, wa, NT_DIMS, preferred_element_type=F32)
                  + lax.dot_general(dzi_b, wx, NT_DIMS, preferred_element_type=F32))
            dvec_ref[0:1, lanes] += _colsum(du)
            for k in range(CONV_WIDTH):
                dvec_ref[4 + k:5 + k, lanes] += _colsum(du * xbuf[pl.ds(HALO - (CONV_WIDTH - 1) + k, ts), lanes])

            dubuf[0:ts, lanes] = du
            dx = cw_ref[CONV_WIDTH - 1:CONV_WIDTH, lanes] * du
            for k in range(CONV_WIDTH - 1):
                dx = dx + cw_ref[k:k + 1, lanes] * dubuf[pl.ds(CONV_WIDTH - 1 - k, ts), lanes]
            dproj_ref[0, :, lanes] = dx.astype(BF16)
        dubuf[pl.ds(ts, HALO), :] = dubuf[0:HALO, :]

        @pl.when(first_tile)
        def _():
            dgates_ref[...] = dw_acc[...].astype(BF16)

    rev = lambda h, j: (nt - 1 - j, h)
    return pl.pallas_call(
        body, name=f"rnn_bwd_l{layer}", grid=(nh // hps, nt),
        in_specs=[pl.BlockSpec((2, ts, wl), lambda h, j: (0, nt - 1 - j, h)),
                  pl.BlockSpec((1, HALO, wl), lambda h, j: (0, halo(j), h)),
                  pl.BlockSpec((ts, wl), rev),
                  pl.BlockSpec((HALO, wl), lambda h, j: (halo(j), h)),
                  pl.BlockSpec((ts, wl), rev),
                  pl.BlockSpec((CONV_WIDTH, wl), lambda h, j: (0, h)),
                  pl.BlockSpec((8, wl), lambda h, j: (0, h)),
                  pl.BlockSpec((hps, hd, hd), lambda h, j: (h, 0, 0)),
                  pl.BlockSpec((hps, hd, hd), lambda h, j: (h, 0, 0))],
        out_specs=[pl.BlockSpec((2, ts, wl), lambda h, j: (0, nt - 1 - j, h)),
                   pl.BlockSpec((2, hps, hd, hd), lambda h, j: (0, h, 0, 0)),
                   pl.BlockSpec((16, wl), lambda h, j: (0, h))],
        out_shape=[jax.ShapeDtypeStruct((4, s, d), BF16),
                   jax.ShapeDtypeStruct((2, nh, hd, hd), BF16),
                   jax.ShapeDtypeStruct((16, d), F32)],
        scratch_shapes=[pltpu.VMEM((ts + HALO, wl), F32), pltpu.VMEM((ts + HALO, wl), F32),
                        pltpu.VMEM((ts + HALO, wl), F32), pltpu.VMEM((8, wl), F32),
                        pltpu.VMEM((2, hps, hd, hd), F32)],
        compiler_params=_params(2),
    )(proj, proj, hs, hs, dycat, cw, vec, wa, wx)


def _pool_bwd(proj, dycat, dproj, pw, vec, layer):
    _, s, d = proj.shape
    ng, gd, _ = pw.shape
    ts = _tile(s, MATMUL_ROWS)
    nt = s // ts
    halo = _halo_index(ts, nt)

    def body(proj_ref, xh_ref, dy_ref, dproj_in, pw_ref, vec_ref, dproj_ref, dpw_ref, dvec_ref, xbuf, qbuf, acc):
        del dproj_in
        j = pl.program_id(0)
        i = nt - 1 - j

        @pl.when(j == 0)
        def _():
            qbuf[pl.ds(ts, HALO), :] = jnp.zeros((HALO, d), F32)
            acc[...] = jnp.zeros_like(acc)
            dvec_ref[...] = jnp.zeros_like(dvec_ref)

        xbuf[0:HALO, :] = jnp.where(i == 0, 0.0, xh_ref[0])
        xbuf[pl.ds(HALO, ts), :] = proj_ref[0]
        for g in range(ng):
            lanes = slice(g * gd, (g + 1) * gd)
            win = 2 << g
            xt = proj_ref[0, :, lanes]
            inv_cnt = _inv_count(i, ts, gd, win)
            pooled = _pooled(xbuf, xt, lanes, win, inv_cnt, ts).astype(BF16)
            z = jnp.dot(pooled, pw_ref[g], preferred_element_type=F32) + vec_ref[0:1, lanes]
            scale = vec_ref[1:2, lanes]
            gg = proj_ref[1, :, lanes]
            sg = _sigmoid(gg)
            dyc = dy_ref[:, lanes]
            dyp = dyc * (gg * sg)
            dproj_ref[1, :, lanes] = (dyc * (z * scale) * (sg * (1.0 + gg * (1.0 - sg)))).astype(BF16)
            dvec_ref[1:2, lanes] += _colsum(dyp * z)
            dz = dyp * scale
            dvec_ref[0:1, lanes] += _colsum(dz)
            dz_b = dz.astype(BF16)
            acc[g] += lax.dot_general(pooled, dz_b, TN_DIMS, preferred_element_type=F32)
            dpooled = lax.dot_general(dz_b, pw_ref[g], NT_DIMS, preferred_element_type=F32)

            q = dpooled * inv_cnt
            qbuf[0:ts, lanes] = q
            dx = q - dpooled
            for dlt in range(1, win):
                dx = dx + qbuf[pl.ds(dlt, ts), lanes]
            dproj_ref[0, :, lanes] = dx.astype(BF16)
        qbuf[pl.ds(ts, HALO), :] = qbuf[0:HALO, :]

        @pl.when(j == nt - 1)
        def _():
            dpw_ref[...] = acc[...].astype(BF16)

    return pl.pallas_call(
        body, name=f"pool_bwd_l{layer}", grid=(nt,),
        in_specs=[pl.BlockSpec((2, ts, d), lambda j: (1, nt - 1 - j, 0)),
                  pl.BlockSpec((1, HALO, d), lambda j: (2, halo(j), 0)),
                  pl.BlockSpec((ts, d), lambda j: (nt - 1 - j, 1)),
                  ANY,
                  pl.BlockSpec((ng, gd, gd), lambda j: (0, 0, 0)),
                  pl.BlockSpec((8, d), lambda j: (0, 0))],
        out_specs=[pl.BlockSpec((2, ts, d), lambda j: (1, nt - 1 - j, 0)),
                   pl.BlockSpec((ng, gd, gd), lambda j: (0, 0, 0)),
                   pl.BlockSpec((8, d), lambda j: (0, 0))],
        out_shape=[jax.ShapeDtypeStruct((4, s, d), BF16),
                   jax.ShapeDtypeStruct((ng, gd, gd), BF16),
                   jax.ShapeDtypeStruct((8, d), F32)],
        input_output_aliases={3: 0},
        scratch_shapes=[pltpu.VMEM((ts + HALO, d), F32), pltpu.VMEM((ts + HALO, d), F32),
                        pltpu.VMEM((ng, gd, gd), F32)],
        compiler_params=_params(1),
    )(proj, proj, dycat, dproj, pw, vec)


def _inproj_bwd_x(dproj, w_all, x, dxo, vec, layer):
    s, d = x.shape
    p = w_all.shape[2]
    ts = _tile(s, MATMUL_ROWS)

    def body(dp_ref, w_ref, x_ref, dxo_ref, vec_ref, dx_ref, dvec_ref):
        @pl.when(pl.program_id(0) == 0)
        def _():
            dvec_ref[...] = jnp.zeros_like(dvec_ref)

        dh = lax.dot_general(dp_ref[0], w_ref[0], NT_DIMS, preferred_element_type=F32)
        for k in range(1, 4):
            dh = dh + lax.dot_general(dp_ref[k], w_ref[k], NT_DIMS, preferred_element_type=F32)
        _, xn, rs = _prenorm(x_ref[...], vec_ref)
        gpre, scale1 = vec_ref[3:4, :], 1.0 + vec_ref[1:2, :]
        dvec_ref[0:1, :] += _colsum(dh)
        dvec_ref[1:2, :] += _colsum(dh * (xn * gpre))
        dvec_ref[2:3, :] += _colsum(dh * (xn * scale1))
        t = dh * (gpre * scale1)
        dx_ref[...] = dxo_ref[...] + rs * (t - xn * jnp.mean(t * xn, axis=-1, keepdims=True))

    row = pl.BlockSpec((ts, d), lambda i: (i, 0))
    return pl.pallas_call(
        body, name=f"inproj_bwd_x_l{layer}", grid=(s // ts,),
        in_specs=[pl.BlockSpec((4, ts, p), lambda i: (0, i, 0)),
                  pl.BlockSpec((4, d, p), lambda i: (0, 0, 0)),
                  row, row, pl.BlockSpec((8, d), lambda i: (0, 0))],
        out_specs=[row, pl.BlockSpec((8, d), lambda i: (0, 0))],
        out_shape=[jax.ShapeDtypeStruct((s, d), F32), jax.ShapeDtypeStruct((8, d), F32)],
        compiler_params=_params(1),
    )(dproj, w_all, x, dxo, vec)


def _inproj_bwd_w(dproj, x, vec, layer):
    s, d = x.shape
    p = dproj.shape[2]
    ts = _tile(s, MATMUL_ROWS)
    nt = s // ts

    def body(dp_ref, x_ref, vec_ref, dw_ref, acc):
        i = pl.program_id(0)

        @pl.when(i == 0)
        def _():
            acc[...] = jnp.zeros_like(acc)

        h, _, _ = _prenorm(x_ref[...], vec_ref)
        hb = h.astype(BF16)
        for k in range(4):
            acc[k] += lax.dot_general(hb, dp_ref[k], TN_DIMS, preferred_element_type=F32)

        @pl.when(i == nt - 1)
        def _():
            dw_ref[...] = acc[...].astype(BF16)

    return pl.pallas_call(
        body, name=f"inproj_bwd_w_l{layer}", grid=(nt,),
        in_specs=[pl.BlockSpec((4, ts, p), lambda i: (0, i, 0)),
                  pl.BlockSpec((ts, d), lambda i: (i, 0)),
                  pl.BlockSpec((8, d), lambda i: (0, 0))],
        out_specs=pl.BlockSpec((4, d, p), lambda i: (0, 0, 0)),
        out_shape=jax.ShapeDtypeStruct((4, d, p), BF16),
        scratch_shapes=[pltpu.VMEM((4, d, p), F32)],
        compiler_params=_params(1),
    )(dproj, x, vec)


def _sum_slots(stacked, name):
    n, rows, cols = stacked.shape
    tr = _row_tile(rows)

    def body(in_ref, out_ref):
        total = in_ref[0].astype(F32)
        for b in range(1, n):
            total = total + in_ref[b].astype(F32)
        out_ref[...] = total

    return pl.pallas_call(
        body, name=name, grid=(rows // tr,),
        in_specs=[pl.BlockSpec((n, tr, cols), lambda i: (0, i, 0))],
        out_specs=pl.BlockSpec((tr, cols), lambda i: (i, 0)),
        out_shape=jax.ShapeDtypeStruct((rows, cols), F32),
        compiler_params=_params(1),
    )(stacked)


def _adamw(w, m, v, grads, name):
    shape = w.shape
    cols = shape[-1]
    rows = w.size // cols
    tr = _row_tile(rows)
    n = len(grads)

    def body(*refs):
        w_ref, m_ref, v_ref = refs[:3]
        g_refs = refs[3:3 + n]
        g_out, d_out, m_out, v_out = refs[3 + n:]
        g = g_refs[0][...]
        for r in g_refs[1:]:
            g = g + r[...]
        m_new = ADAM_B1 * m_ref[...] + (1.0 - ADAM_B1) * g
        v_new = ADAM_B2 * v_ref[...] + (1.0 - ADAM_B2) * (g * g)
        m_hat = m_new / (1.0 - ADAM_B1 ** ADAM_STEP)
        v_hat = v_new / (1.0 - ADAM_B2 ** ADAM_STEP)
        g_out[...] = g
        d_out[...] = (-ADAM_LR) * (m_hat / (jnp.sqrt(v_hat) + ADAM_EPS) + ADAM_WD * w_ref[...])
        m_out[...] = m_new
        v_out[...] = v_new

    blk = pl.BlockSpec((tr, cols), lambda i: (i, 0))
    outs = pl.pallas_call(
        body, name=name, grid=(rows // tr,),
        in_specs=[blk] * (3 + n), out_specs=[blk] * 4,
        out_shape=[jax.ShapeDtypeStruct((rows, cols), F32)] * 4,
        compiler_params=_params(1),
    )(*[a.reshape(rows, cols) for a in (w, m, v, *grads)])
    return tuple(o.reshape(shape) for o in outs)


def _adam_update(w, m, v, g):
    m_new = ADAM_B1 * m + (1.0 - ADAM_B1) * g
    v_new = ADAM_B2 * v + (1.0 - ADAM_B2) * (g * g)
    m_hat = m_new / (1.0 - ADAM_B1 ** ADAM_STEP)
    v_hat = v_new / (1.0 - ADAM_B2 ** ADAM_STEP)
    return (-ADAM_LR) * (m_hat / (jnp.sqrt(v_hat) + ADAM_EPS) + ADAM_WD * w), m_new, v_new


def _adamw_layer(w, m, v, grads, layer, prev, name, grad_row_offset=0):
    nl = w.shape[0]
    cols = w.shape[-1]
    rows = w.size // (nl * cols)
    tr = _row_tile(rows)
    off = layer * (rows // tr)
    g_off = grad_row_offset // tr
    n = len(grads)
    n_prev = 0 if prev is None else 4

    def body(*refs):
        w_ref, m_ref, v_ref = refs[:3]
        g_refs = refs[3:3 + n]
        g_out, d_out, m_out, v_out = refs[3 + n + n_prev:]
        g = g_refs[0][...]
        for r in g_refs[1:]:
            g = g + r[...]
        g_out[...] = g
        d_out[...], m_out[...], v_out[...] = _adam_update(w_ref[...], m_ref[...], v_ref[...], g)

    mine = pl.BlockSpec((tr, cols), lambda i: (off + i, 0))
    args = [a.reshape(nl * rows, cols) for a in (w, m, v)] + [g.reshape(-1, cols) for g in grads]
    outs = pl.pallas_call(
        body, name=name, grid=(rows // tr,),
        in_specs=[mine] * 3 + [pl.BlockSpec((tr, cols), lambda i: (g_off + i, 0))] * n + [ANY] * n_prev,
        out_specs=[mine] * 4,
        out_shape=[jax.ShapeDtypeStruct((nl * rows, cols), F32)] * 4,
        input_output_aliases={3 + n + k: k for k in range(n_prev)},
        compiler_params=_params(1),
    )(*args, *(prev or ()))
    return tuple(outs)


def _into_slot(a, dtype, chip_arr, name):
    rows, cols = a.shape
    tr = _row_tile(rows)

    def body(chip_ref, a_ref, out_ref):
        del chip_ref
        out_ref[...] = a_ref[...].astype(dtype)

    return pl.pallas_call(
        body, name=name,
        grid_spec=pltpu.PrefetchScalarGridSpec(
            num_scalar_prefetch=1, grid=(rows // tr,),
            in_specs=[pl.BlockSpec((tr, cols), lambda i, chip: (i, 0))],
            out_specs=pl.BlockSpec((None, tr, cols), lambda i, chip: (chip[0], i, 0))),
        out_shape=jax.ShapeDtypeStruct((4, rows, cols), dtype),
        compiler_params=_params(1),
    )(chip_arr, a)


def _sum_owner(own, land, chip_arr, own_block, own_index, name):
    blk = land.shape[1:]
    tr = _row_tile(blk[-2])
    steps = blk[-2] // tr
    tile = (*blk[:-2], tr, blk[-1])
    lead = (0,) * (len(blk) - 2)

    def body(chip_ref, own_ref, l1, l2, l3, out_ref):
        del chip_ref
        out_ref[...] = (own_ref[...].astype(F32) + l1[...].astype(F32)) + (l2[...].astype(F32) + l3[...].astype(F32))

    def landed(k):
        return pl.BlockSpec((None, *tile), lambda i, chip: (chip[0] ^ k, *lead, i, 0))

    return pl.pallas_call(
        body, name=name,
        grid_spec=pltpu.PrefetchScalarGridSpec(
            num_scalar_prefetch=1, grid=(steps,),
            in_specs=[pl.BlockSpec(own_block(tr), own_index), landed(1), landed(2), landed(3)],
            out_specs=pl.BlockSpec(tile, lambda i, chip: (*lead, i, 0))),
        out_shape=jax.ShapeDtypeStruct(blk, F32),
        compiler_params=_params(1),
    )(chip_arr, own, land, land, land)


_WHOLE_VMEM = pltpu.CompilerParams(vmem_limit_bytes=V7X_VMEM_LIMIT_BYTES)


def _pack_vectors(modbuf, ada_b, pre_norm_g, post_norm_g, conv_b, gate_a_b, gate_x_b, lru_lambda):
    nl, d = pre_norm_g.shape
    n = modbuf.shape[2] // nl
    nh, hd = gate_a_b.shape[1], gate_a_b.shape[2]

    def body(mb_ref, ab_ref, pre_ref, post_ref, cb_ref, gab_ref, gxb_ref, lam_ref, *outs):
        for layer in range(nl):
            vec_ref, rvec_ref = outs[layer], outs[nl + layer]
            vec_ref[...] = jnp.zeros_like(vec_ref)
            rvec_ref[...] = jnp.zeros_like(rvec_ref)
            for k in range(4):
                piece = mb_ref[k, 0:1, layer * n:(layer + 1) * n] + ab_ref[layer:layer + 1, k * n:(k + 1) * n]
                lo = k * n
                while lo < (k + 1) * n:
                    row = lo // d
                    hi = min((row + 1) * d, (k + 1) * n)
                    vec_ref[row:row + 1, lo - row * d:hi - row * d] = piece[:, lo - k * n:hi - k * n]
                    lo = hi
            vec_ref[3:4, :] = pre_ref[layer:layer + 1, :]
            vec_ref[4:5, :] = post_ref[layer:layer + 1, :]
            rvec_ref[0:1, :] = cb_ref[layer:layer + 1, :]
            for h in range(nh):
                rvec_ref[1:2, h * hd:(h + 1) * hd] = gab_ref[layer, h:h + 1, :]
                rvec_ref[2:3, h * hd:(h + 1) * hd] = gxb_ref[layer, h:h + 1, :]
            rvec_ref[3:4, :] = lam_ref[layer:layer + 1, :]

    out = pl.pallas_call(
        body, name="pack_vectors", in_specs=[VMEM] * 8, out_specs=[VMEM] * (2 * nl),
        out_shape=[jax.ShapeDtypeStruct((8, d), F32)] * (2 * nl), compiler_params=_WHOLE_VMEM,
    )(modbuf, ada_b, pre_norm_g, post_norm_g, conv_b, gate_a_b, gate_x_b, lru_lambda)
    return list(out[:nl]), list(out[nl:])


def _pack_gathered(convw_g, poolb_g, pws, pool_scale, ng):
    nl, d = pool_scale.shape
    taps = convw_g.shape[1] // nl
    dq = convw_g.shape[2]
    gq, gd = poolb_g.shape[2], pws[0].shape[2]

    def body(cg_ref, pb_ref, *rest):
        pw_refs, ps_ref = rest[:nl], rest[nl]
        outs = rest[nl + 1:]
        for layer in range(nl):
            cw_ref, pvec_ref, pwf_ref = outs[layer], outs[nl + layer], outs[2 * nl + layer]
            pvec_ref[...] = jnp.zeros_like(pvec_ref)
            pvec_ref[1:2, :] = ps_ref[layer:layer + 1, :]
            for k in range(4):
                cw_ref[:, k * dq:(k + 1) * dq] = cg_ref[k, layer * taps:(layer + 1) * taps, :]
                for g in range(ng):
                    lo = g * gd + k * gq
                    pvec_ref[0:1, lo:lo + gq] = pb_ref[k, layer * ng + g:layer * ng + g + 1, :]
                    pwf_ref[g, k * gq:(k + 1) * gq, :] = pw_refs[layer][k, g * gq:(g + 1) * gq, :]

    out = pl.pallas_call(
        body, name="pack_gathered", in_specs=[VMEM] * (3 + nl), out_specs=[VMEM] * (3 * nl),
        out_shape=[jax.ShapeDtypeStruct((taps, d), F32)] * nl + [jax.ShapeDtypeStruct((8, d), F32)] * nl
        + [jax.ShapeDtypeStruct((ng, gd, gd), BF16)] * nl,
        compiler_params=_WHOLE_VMEM,
    )(convw_g, poolb_g, *pws, pool_scale)
    return list(out[:nl]), list(out[nl:2 * nl]), list(out[2 * nl:])


ROW_SHIFT, ROW_SCALE, ROW_PRE, ROW_GATE, ROW_POST = 0, 1, 2, 8, 9
ROW_CONV_B, ROW_GATE_A_B, ROW_GATE_X_B, ROW_LAMBDA, ROW_CONV_W = 16, 17, 18, 19, 20
ROW_POOL_B, ROW_POOL_SCALE, ROW_SQ = 32, 33, 40


def _adamw_small(totals, chip_arr, params):
    nl = len(totals)
    d = totals[0].shape[1]
    n_par = len(params)
    flat = [a for p in params for a in p]
    nh, hd = params[6][0].shape[1], params[6][0].shape[2]
    taps, dq = params[8][0].shape[1], params[8][0].shape[2]
    ng, gq = params[9][0].shape[1], params[9][0].shape[2]
    gd = d // ng

    def body(chip_ref, *refs):
        tot = refs[:nl]
        ins = refs[nl:nl + 3 * n_par]
        outs = refs[nl + 3 * n_par:]
        chip = chip_ref[0]

        def update(p, idx, g):
            delta, m_new, v_new = _adam_update(ins[3 * p][idx], ins[3 * p + 1][idx], ins[3 * p + 2][idx], g)
            outs[4 * p][idx] = g
            outs[4 * p + 1][idx] = delta
            outs[4 * p + 2][idx] = m_new
            outs[4 * p + 3][idx] = v_new

        def mine(candidates):
            g = candidates[0]
            for k in range(1, 4):
                g = jnp.where(chip == k, candidates[k], g)
            return g

        for layer in range(nl):
            t = tot[layer]
            row = (slice(layer, layer + 1), slice(None))
            for j, r in enumerate((ROW_SHIFT, ROW_SCALE, ROW_GATE)):
                update(0, (slice(layer, layer + 1), slice(j * d, (j + 1) * d)), t[r:r + 1, :])
            for p, r in ((1, ROW_PRE), (2, ROW_POST), (3, ROW_CONV_B), (4, ROW_LAMBDA), (5, ROW_POOL_SCALE)):
                update(p, row, t[r:r + 1, :])
            for h in range(nh):
                idx = (layer, slice(h, h + 1), slice(None))
                update(6, idx, t[ROW_GATE_A_B:ROW_GATE_A_B + 1, h * hd:(h + 1) * hd])
                update(7, idx, t[ROW_GATE_X_B:ROW_GATE_X_B + 1, h * hd:(h + 1) * hd])
            for k in range(taps):
                r = ROW_CONV_W + k
                update(8, (layer, slice(k, k + 1), slice(None)), mine([t[r:r + 1, c * dq:(c + 1) * dq] for c in range(4)]))
            for g in range(ng):
                cands = [t[ROW_POOL_B:ROW_POOL_B + 1, g * gd + c * gq:g * gd + (c + 1) * gq] for c in range(4)]
                update(9, (layer, slice(g, g + 1), slice(None)), mine(cands))

    out = pl.pallas_call(
        body, name="adamw_small",
        in_specs=[pl.BlockSpec(memory_space=pltpu.SMEM)] + [VMEM] * (nl + 3 * n_par),
        out_specs=[VMEM] * (4 * n_par),
        out_shape=[jax.ShapeDtypeStruct(p[0].shape, F32) for p in params for _ in range(4)],
        compiler_params=_WHOLE_VMEM,
    )(chip_arr, *totals, *flat)
    return [tuple(out[4 * p:4 * p + 4]) for p in range(n_par)]


def _adamw_ada_w_layer(c_t, dmod, w, m, v, layer, prev, name):
    nl, d, n = w.shape
    nb = c_t.shape[1]
    tr = _row_tile(d)
    off = layer * (d // tr)
    n_prev = 0 if prev is None else 4

    def body(c_ref, dm_ref, w_ref, m_ref, v_ref, *rest):
        g_out, d_out, m_out, v_out = rest[n_prev:]
        g = c_ref[:, 0:1] * dm_ref[0:1, :]
        for b in range(1, nb):
            g = g + c_ref[:, b:b + 1] * dm_ref[b:b + 1, :]
        g_out[...] = g
        d_out[...], m_out[...], v_out[...] = _adam_update(w_ref[...], m_ref[...], v_ref[...], g)

    mine = pl.BlockSpec((tr, n), lambda i: (off + i, 0))
    outs = pl.pallas_call(
        body, name=name, grid=(d // tr,),
        in_specs=[pl.BlockSpec((tr, nb), lambda i: (i, 0)), pl.BlockSpec((nb, n), lambda i: (0, 0))] + [mine] * 3
        + [ANY] * n_prev,
        out_specs=[mine] * 4,
        out_shape=[jax.ShapeDtypeStruct((nl * d, n), F32)] * 4,
        input_output_aliases={5 + k: k for k in range(n_prev)},
        compiler_params=_params(1),
    )(c_t, dmod, *[a.reshape(nl * d, n) for a in (w, m, v)], *(prev or ()))
    return tuple(outs)


def _place():
    x, y, c = lax.axis_index("x"), lax.axis_index("y"), lax.axis_index("c")
    return x, y, c


OTHER_CHIPS = ((1, 0), (0, 1), (1, 1))
OTHER_DEVICES = tuple((fx, fy, fc) for fx in (0, 1) for fy in (0, 1) for fc in (0, 1))[1:]


def _mod_exchange(c_row, ada_w):
    nl, d, n = ada_w.shape

    def body(c_ref, w_ref, cbuf, modbuf, token, cblk, mres, send_a, recv_a, send_c, recv_c):
        token[...] = jnp.zeros_like(token)
        x, y, c = _place()
        me = 4 * x + 2 * y + c
        chip = 2 * x + y
        cv = c_ref[...]
        cblk[...] = jnp.zeros_like(cblk)
        cblk[0:1, :] = cv * _sigmoid(cv)

        def rows_of(dev):
            return cbuf.at[pl.ds(pl.multiple_of(8 * dev, 8), 8), :]

        cbuf[pl.ds(pl.multiple_of(8 * me, 8), 8), :] = cblk[...]
        sends = []
        for j, (fx, fy, fc) in enumerate(OTHER_DEVICES):
            cp = pltpu.make_async_remote_copy(
                src_ref=cblk, dst_ref=rows_of(me), send_sem=send_a.at[j], recv_sem=recv_a.at[j],
                device_id=(x ^ fx, y ^ fy, c ^ fc), device_id_type=MESH)
            cp.start()
            sends.append(cp)
        for j, (fx, fy, fc) in enumerate(OTHER_DEVICES):
            peer = 4 * (x ^ fx) + 2 * (y ^ fy) + (c ^ fc)
            pltpu.make_async_remote_copy(
                src_ref=cblk, dst_ref=rows_of(peer), send_sem=send_a.at[j], recv_sem=recv_a.at[j],
                device_id=(x ^ fx, y ^ fy, c ^ fc), device_id_type=MESH).wait_recv()
        for cp in sends:
            cp.wait_send()

        call = cbuf[...]
        for layer in range(nl):
            mres[:, layer * n:(layer + 1) * n] = jnp.dot(
                call, w_ref[layer], preferred_element_type=F32, precision=lax.Precision.HIGHEST)

        def block_of(dev):
            return mres.at[pl.ds(pl.multiple_of(8 * dev, 8), 8), :]

        modbuf[chip] = mres[pl.ds(pl.multiple_of(8 * me, 8), 8), :]
        sends = []
        for j, (fx, fy) in enumerate(OTHER_CHIPS):
            peer = 4 * (x ^ fx) + 2 * (y ^ fy) + c
            cp = pltpu.make_async_remote_copy(
                src_ref=block_of(peer), dst_ref=modbuf.at[chip], send_sem=send_c.at[j], recv_sem=recv_c.at[j],
                device_id=(x ^ fx, y ^ fy, c), device_id_type=MESH)
            cp.start()
            sends.append(cp)
        for j, (fx, fy) in enumerate(OTHER_CHIPS):
            pltpu.make_async_remote_copy(
                src_ref=block_of(me), dst_ref=modbuf.at[2 * (x ^ fx) + (y ^ fy)],
                send_sem=send_c.at[j], recv_sem=recv_c.at[j],
                device_id=(x ^ fx, y ^ fy, c), device_id_type=MESH).wait_recv()
        for cp in sends:
            cp.wait_send()

    return pl.pallas_call(
        body, name="mod_exchange", in_specs=[VMEM, VMEM], out_specs=[VMEM, VMEM, VMEM],
        out_shape=[jax.ShapeDtypeStruct((64, d), F32), jax.ShapeDtypeStruct((4, 8, nl * n), F32),
                   jax.ShapeDtypeStruct((8, 128), F32)],
        scratch_shapes=[pltpu.VMEM((8, d), F32), pltpu.VMEM((64, nl * n), F32),
                        pltpu.SemaphoreType.DMA((7,)), pltpu.SemaphoreType.DMA((7,)),
                        pltpu.SemaphoreType.DMA((3,)), pltpu.SemaphoreType.DMA((3,))],
        compiler_params=pltpu.CompilerParams(vmem_limit_bytes=V7X_VMEM_LIMIT_BYTES, has_side_effects=True),
    )(c_row, ada_w)


def _in_hbm(a):
    return pltpu.with_memory_space_constraint(a, pltpu.HBM)


def _gather_start(lands, groups):
    n, ngr = len(lands), len(groups)

    def body(*refs):
        land = refs[:n]
        sems = refs[n:n + 2 * ngr]
        token = refs[-1]
        x, y, c = _place()
        chip = 2 * x + y
        for gi, idxs in enumerate(groups):
            for t, i in enumerate(idxs):
                for j, (fx, fy) in enumerate(OTHER_CHIPS):
                    pltpu.make_async_remote_copy(
                        src_ref=land[i].at[chip], dst_ref=land[i].at[chip],
                        send_sem=sems[2 * gi].at[3 * t + j], recv_sem=sems[2 * gi + 1].at[3 * t + j],
                        device_id=(x ^ fx, y ^ fy, c), device_id_type=MESH).start()
        token[...] = jnp.zeros_like(token)

    sem_shapes = []
    for idxs in groups:
        sem_shapes += [pltpu.SemaphoreType.DMA((3 * len(idxs),))] * 2
    out = pl.pallas_call(
        body, name="weight_gather_start",
        in_specs=[HBM] * n, out_specs=[SEM] * (2 * ngr) + [HBM] * n + [VMEM],
        out_shape=sem_shapes + [pltpu.HBM(a.shape, a.dtype) for a in lands] + [jax.ShapeDtypeStruct((8, 128), F32)],
        input_output_aliases={i: 2 * ngr + i for i in range(n)},
        compiler_params=pltpu.CompilerParams(has_side_effects=DATAFLOW_EFFECT),
    )(*[_in_hbm(a) for a in lands])
    sems = [(out[2 * gi], out[2 * gi + 1]) for gi in range(ngr)]
    return sems, list(out[2 * ngr:2 * ngr + n]), out[-1]


def _gather_wait(lands, sems, after, name):
    n = len(lands)

    def body(*refs):
        land = refs[:n]
        send_sems, recv_sems = refs[n], refs[n + 1]
        x, y, c = _place()
        chip = 2 * x + y
        for t in range(n):
            for j, (fx, fy) in enumerate(OTHER_CHIPS):
                cp = pltpu.make_async_remote_copy(
                    src_ref=land[t].at[chip], dst_ref=land[t].at[2 * (x ^ fx) + (y ^ fy)],
                    send_sem=send_sems.at[3 * t + j], recv_sem=recv_sems.at[3 * t + j],
                    device_id=(x ^ fx, y ^ fy, c), device_id_type=MESH)
                cp.wait_send()
                cp.wait_recv()

    out = pl.pallas_call(
        body, name=name,
        in_specs=[HBM] * n + [SEM, SEM, ANY], out_specs=[HBM] * n,
        out_shape=[pltpu.HBM(a.shape, a.dtype) for a in lands],
        input_output_aliases={i: i for i in range(n)},
        compiler_params=pltpu.CompilerParams(has_side_effects=DATAFLOW_EFFECT),
    )(*lands, sems[0], sems[1], after)
    return list(out)


def _to_owner_copies(pairs, q):
    x, y, c = _place()
    chip = 2 * x + y
    out = []
    for t, (part, land) in enumerate(pairs):
        for j, (fx, fy) in enumerate(OTHER_CHIPS):
            owner = 2 * (x ^ fx) + (y ^ fy)
            if part.shape[0] == 4 and part.shape[1:] == land.shape[1:]:
                src = part.at[owner]
            else:
                src = part.at[:, pl.ds(pl.multiple_of(owner * q, q), q), :]
            out.append((src, land.at[chip], land.at[owner], (x ^ fx, y ^ fy, c), 3 * t + j))
    return out


def _to_all_copies(bufs, first_sem):
    x, y, c = _place()
    me = 4 * x + 2 * y + c
    out = []
    for t, buf in enumerate(bufs):
        for j, (fx, fy, fc) in enumerate(OTHER_DEVICES):
            them = 4 * (x ^ fx) + 2 * (y ^ fy) + (c ^ fc)
            out.append((buf.at[me], buf.at[me], buf.at[them], (x ^ fx, y ^ fy, c ^ fc), first_sem + 7 * t + j))
    return out


def _to_chips_copies(bufs, first_sem):
    x, y, c = _place()
    chip = 2 * x + y
    out = []
    for t, buf in enumerate(bufs):
        for j, (fx, fy) in enumerate(OTHER_CHIPS):
            them = 2 * (x ^ fx) + (y ^ fy)
            out.append((buf.at[chip], buf.at[chip], buf.at[them], (x ^ fx, y ^ fy, c), first_sem + 3 * t + j))
    return out


def _exchange_copies(refs, kinds, q):
    n_owner, n_chips = kinds
    pairs = list(zip(refs[:n_owner], refs[n_owner:2 * n_owner]))
    first_all = 3 * (n_owner + n_chips)
    return (_to_owner_copies(pairs, q) + _to_chips_copies(refs[2 * n_owner:2 * n_owner + n_chips], 3 * n_owner)
            + _to_all_copies(refs[2 * n_owner + n_chips:], first_all))


def _exchange_start(arrays, kinds, q, name):
    n = len(arrays)
    n_sems = 3 * (kinds[0] + kinds[1]) + 7 * (n - 2 * kinds[0] - kinds[1])

    def body(*refs):
        send_sems, recv_sems = refs[n], refs[n + 1]
        for src, dst, _, peer, k in _exchange_copies(refs[:n], kinds, q):
            pltpu.make_async_remote_copy(src_ref=src, dst_ref=dst, send_sem=send_sems.at[k], recv_sem=recv_sems.at[k],
                                         device_id=peer, device_id_type=MESH).start()
        refs[-1][...] = jnp.zeros_like(refs[-1])

    out = pl.pallas_call(
        body, name=name,
        in_specs=[HBM] * n, out_specs=[SEM, SEM] + [HBM] * n + [VMEM],
        out_shape=[pltpu.SemaphoreType.DMA((n_sems,))] * 2 + [pltpu.HBM(a.shape, a.dtype) for a in arrays]
        + [jax.ShapeDtypeStruct((8, 128), F32)],
        input_output_aliases={i: 2 + i for i in range(n)},
        compiler_params=pltpu.CompilerParams(has_side_effects=DATAFLOW_EFFECT),
    )(*[_in_hbm(a) for a in arrays])
    return (out[0], out[1]), list(out[2:2 + n]), out[-1]


def _exchange_wait(arrays, sems, kinds, q, after, name):
    n = len(arrays)

    def body(*refs):
        send_sems, recv_sems = refs[n], refs[n + 1]
        for src, _, landed, peer, k in _exchange_copies(refs[:n], kinds, q):
            cp = pltpu.make_async_remote_copy(src_ref=src, dst_ref=landed, send_sem=send_sems.at[k], recv_sem=recv_sems.at[k],
                                              device_id=peer, device_id_type=MESH)
            cp.wait_send()
            cp.wait_recv()

    out = pl.pallas_call(
        body, name=name,
        in_specs=[HBM] * n + [SEM, SEM, ANY], out_specs=[HBM] * n,
        out_shape=[pltpu.HBM(a.shape, a.dtype) for a in arrays],
        input_output_aliases={i: i for i in range(n)},
        compiler_params=pltpu.CompilerParams(has_side_effects=DATAFLOW_EFFECT),
    )(*arrays, sems[0], sems[1], after)
    return list(out)


def _sibling_swap(parts, layer):
    n = len(parts)

    def body(*refs):
        srcs, outs = refs[:n], refs[n:2 * n]
        send_sems, recv_sems = refs[2 * n:]
        x, y, c = _place()
        cps = [pltpu.make_async_remote_copy(
            src_ref=srcs[i], dst_ref=outs[i], send_sem=send_sems.at[i], recv_sem=recv_sems.at[i],
            device_id=(x, y, 1 - c), device_id_type=MESH) for i in range(n)]
        for cp in cps:
            cp.start()
        for cp in cps:
            cp.wait()

    return pl.pallas_call(
        body, name=f"sibling_swap_l{layer}", in_specs=[ANY] * n, out_specs=[ANY] * n,
        out_shape=[jax.ShapeDtypeStruct(a.shape, a.dtype) for a in parts],
        scratch_shapes=[pltpu.SemaphoreType.DMA((n,)), pltpu.SemaphoreType.DMA((n,))],
        compiler_params=pltpu.CompilerParams(has_side_effects=True),
    )(*parts)


def kernel(x, c, ada_w, ada_b, pre_norm_g, w_in, conv_w, conv_b, gate_a_w, gate_a_b, gate_x_w, gate_x_b, lru_lambda, pool_w, pool_b, pool_scale, w_out, post_norm_g, loss_target, m_ada_w, m_ada_b, m_pre_norm_g, m_w_in, m_conv_w, m_conv_b, m_gate_a_w, m_gate_a_b, m_gate_x_w, m_gate_x_b, m_lru_lambda, m_pool_w, m_pool_b, m_pool_scale, m_w_out, m_post_norm_g, v_ada_w, v_ada_b, v_pre_norm_g, v_w_in, v_conv_w, v_conv_b, v_gate_a_w, v_gate_a_b, v_gate_x_w, v_gate_x_b, v_lru_lambda, v_pool_w, v_pool_b, v_pool_scale, v_w_out, v_post_norm_g):
    nl, d, n_ada = ada_w.shape
    s = x.shape[1]
    nh, hd = gate_a_w.shape[1], gate_a_w.shape[2]
    ng, gq, gd = pool_w.shape[1], pool_w.shape[2], pool_w.shape[3]
    me = 4 * lax.axis_index("x") + 2 * lax.axis_index("y") + lax.axis_index("c")
    chip = 2 * lax.axis_index("x") + lax.axis_index("y")
    chip_arr = jnp.reshape(chip, (1,)).astype(jnp.int32)
    x0 = x.reshape(s, d)
    target = loss_target.reshape(s, d)
    p_in = w_in.shape[2]
    r_out = w_out.shape[1]

    cbuf, modbuf, mod_token = _mod_exchange(c.reshape(1, d), ada_w)
    vecs, rvecs = _pack_vectors(modbuf, ada_b, pre_norm_g, post_norm_g, conv_b, gate_a_b, gate_x_b, lru_lambda)

    win = [_into_slot(w_in[l], BF16, chip_arr, f"slot_w_in_l{l}") for l in range(nl)]
    wout = [_into_slot(w_out[l], BF16, chip_arr, f"slot_w_out_l{l}") for l in range(nl)]
    pw = [_into_slot(pool_w[l].reshape(ng * gq, gd), BF16, chip_arr, f"slot_pool_w_l{l}") for l in range(nl)]
    convw = _into_slot(conv_w.reshape(nl * CONV_WIDTH, d // 4) + mod_token[0:1, 0:1], F32, chip_arr, "slot_conv_w")
    poolb = _into_slot(pool_b.reshape(nl * ng, gq), F32, chip_arr, "slot_pool_b")
    lands = [win[0], convw, poolb, *pw, wout[0]]
    groups = [[0], list(range(1, len(lands)))]
    for l in range(1, nl):
        groups.append([len(lands), len(lands) + 1])
        lands += [win[l], wout[l]]
    sems, lands, token = _gather_start(lands, groups)
    wa_b, wx_b = gate_a_w.astype(BF16), gate_x_w.astype(BF16)

    xs, projs, hss, ycats, ys = [x0], [], [], [], []
    sq = None
    convw_full = poolw_full = pvecs = None
    for l in range(nl):
        if l == 0:
            (win[0],) = _gather_wait([lands[0]], sems[0], modbuf, "weight_gather_wait_a")
        proj = _inproj_fwd(xs[l], vecs[l], win[l], l)
        if l == 0:
            got = _gather_wait(lands[1:len(groups[1]) + 1], sems[1], proj, "weight_gather_wait_b")
            wout[0] = got[2 + nl]
            convw_full, pvecs, poolw_full = _pack_gathered(got[0], got[1], got[2:2 + nl], pool_scale, ng)
        ycat, hs = _rnn_fwd(proj, convw_full[l], rvecs[l], wa_b[l], wx_b[l], l)
        if l + 1 < nl:
            base = len(groups[1]) + 1 + 2 * l
            win[l + 1], wout[l + 1] = _gather_wait(lands[base:base + 2], sems[2 + l], hs, f"weight_gather_wait_c{l + 1}")
        ycat = _pool_fwd(proj, ycat, poolw_full[l], pvecs[l], l)
        y, xo, sq = _outproj_fwd(ycat, wout[l], xs[l], vecs[l], target if l == nl - 1 else None, l)
        projs.append(proj), hss.append(hs), ycats.append(ycat), ys.append(y), xs.append(xo)

    c_all_t = cbuf.reshape(8, 8, d)[:, 0, :].T

    def finish(l, flights, after, prev):
        (sems_a, arr_a), (sems_g, arr_g), (sems_b, arr_b), (sems_c, arr_c) = flights
        dwout_l, rwout = _exchange_wait(arr_a, sems_a, (1, 0), gq, after, f"grad_wait_a_l{l}")
        dpw_l, rpw, gates = _exchange_wait(arr_g, sems_g, (1, 1), gq, rwout, f"grad_wait_g_l{l}")
        dwin_l, rwin = _exchange_wait(arr_b, sems_b, (1, 0), gq, gates, f"grad_wait_b_l{l}")
        (slabs,) = _exchange_wait(arr_c, sems_c, (0, 0), gq, rwin, f"grad_wait_c_l{l}")
        p_win = _sum_owner(dwin_l, rwin, chip_arr, lambda tr: (None, tr, p_in),
                           lambda i, chip: (chip[0], i, 0), f"sum_w_in_l{l}")
        p_wout = _sum_owner(dwout_l, rwout, chip_arr, lambda tr: (None, tr, d),
                            lambda i, chip: (chip[0], i, 0), f"sum_w_out_l{l}")
        p_pw = _sum_owner(dpw_l, rpw, chip_arr, lambda tr: (ng, tr, gd),
                          lambda i, chip: (0, chip[0], 0), f"sum_pool_w_l{l}")
        p_gates = _sum_slots(gates.reshape(4, 2 * nh * hd, hd), f"sum_gates_l{l}")
        q_win, q_wout, q_pw, q_gates = _sibling_swap([p_win, p_wout, p_pw, p_gates], l)
        per_dev = jnp.concatenate([slabs[:, r] for r in (ROW_SHIFT, ROW_SCALE, ROW_GATE)], axis=-1)
        dmod_mine = lax.dynamic_slice_in_dim(per_dev, chip * n_ada, n_ada, axis=1)
        prev = prev or {}
        big = {
            "w_in": _adamw_layer(w_in, m_w_in, v_w_in, [p_win, q_win], l, prev.get("w_in"), f"adamw_w_in_l{l}"),
            "w_out": _adamw_layer(w_out, m_w_out, v_w_out, [p_wout, q_wout], l, prev.get("w_out"), f"adamw_w_out_l{l}"),
            "pool_w": _adamw_layer(pool_w, m_pool_w, v_pool_w, [p_pw, q_pw], l, prev.get("pool_w"), f"adamw_pool_w_l{l}"),
            "gate_a_w": _adamw_layer(gate_a_w, m_gate_a_w, v_gate_a_w, [p_gates, q_gates], l, prev.get("gate_a_w"),
                                     f"adamw_gate_a_w_l{l}"),
            "gate_x_w": _adamw_layer(gate_x_w, m_gate_x_w, v_gate_x_w, [p_gates, q_gates], l, prev.get("gate_x_w"),
                                     f"adamw_gate_x_w_l{l}", grad_row_offset=nh * hd),
            "ada_w": _adamw_ada_w_layer(c_all_t, dmod_mine, ada_w, m_ada_w, v_ada_w, l, prev.get("ada_w"),
                                        f"adamw_ada_w_l{l}"),
        }
        return big, _sum_slots(slabs, f"sum_slab_l{l}")

    dx = xs[nl]
    flights = token = big = None
    totals = [None] * nl
    for l in reversed(range(nl)):
        vec_l = vecs[l] if token is None else vecs[l] + token[0:1, 0:1]
        dycat, dwout_l, dvec_o = _outproj_bwd(dx, ys[l], ycats[l], wout[l], vec_l, l)
        sems_a, arr_a, tok_a = _exchange_start([dwout_l, lax.empty(dwout_l.shape, BF16)], (1, 0), gq, f"grad_start_a_l{l}")
        dproj, dgates, dvec_r = _rnn_bwd(projs[l], hss[l], dycat, convw_full[l], rvecs[l] + tok_a[0:1, 0:1],
                                         wa_b[l], wx_b[l], l)
        dproj, dpw_l, dvec_p = _pool_bwd(projs[l], dycat, dproj, poolw_full[l], pvecs[l], l)
        gates4 = lax.dynamic_update_slice(lax.empty((4, *dgates.shape), BF16), dgates[None], (chip, 0, 0, 0, 0))
        sems_g, arr_g, tok_g = _exchange_start([dpw_l, lax.empty((4, ng, gq, gd), BF16), gates4], (1, 1), gq,
                                               f"grad_start_g_l{l}")
        dwin_l = _inproj_bwd_w(dproj, xs[l], vec_l + tok_g[0:1, 0:1], l)
        sems_b, arr_b, tok_b = _exchange_start([dwin_l, lax.empty(dwin_l.shape, BF16)], (1, 0), gq, f"grad_start_b_l{l}")
        dx, dvec_i = _inproj_bwd_x(dproj, win[l], xs[l], dx, vec_l + tok_b[0:1, 0:1], l)
        parts = [dvec_i, dvec_o, dvec_r, dvec_p]
        if l == nl - 1:
            parts.append(jnp.tile(sq, (1, d // sq.shape[1])))
        slab = jnp.concatenate(parts, axis=0)
        slabs = lax.dynamic_update_slice(lax.empty((8, *slab.shape), F32), slab[None], (me, 0, 0))
        sems_c, arr_c, token = _exchange_start([slabs], (0, 0), gq, f"grad_start_c_l{l}")
        if flights is not None:
            big, totals[l + 1] = finish(l + 1, flights, token, big)
        flights = ((sems_a, arr_a), (sems_g, arr_g), (sems_b, arr_b), (sems_c, arr_c))
    big, totals[0] = finish(0, flights, big["w_in"][3] if big else dx, big)
    grad_x = dx.reshape(x.shape)
    loss = totals[nl - 1][ROW_SQ, 0] * (0.5 / d)

    small = _adamw_small(totals, chip_arr, [
        (ada_b, m_ada_b, v_ada_b), (pre_norm_g, m_pre_norm_g, v_pre_norm_g), (post_norm_g, m_post_norm_g, v_post_norm_g),
        (conv_b, m_conv_b, v_conv_b), (lru_lambda, m_lru_lambda, v_lru_lambda), (pool_scale, m_pool_scale, v_pool_scale),
        (gate_a_b, m_gate_a_b, v_gate_a_b), (gate_x_b, m_gate_x_b, v_gate_x_b),
        (conv_w, m_conv_w, v_conv_w), (pool_b, m_pool_b, v_pool_b)])

    results = {
        "ada_w": tuple(o.reshape(ada_w.shape) for o in big["ada_w"]),
        "ada_b": small[0],
        "pre_norm_g": small[1],
        "w_in": tuple(o.reshape(w_in.shape) for o in big["w_in"]),
        "conv_w": small[8],
        "conv_b": small[3],
        "gate_a_w": tuple(o.reshape(gate_a_w.shape) for o in big["gate_a_w"]),
        "gate_a_b": small[6],
        "gate_x_w": tuple(o.reshape(gate_x_w.shape) for o in big["gate_x_w"]),
        "gate_x_b": small[7],
        "lru_lambda": small[4],
        "pool_w": tuple(o.reshape(pool_w.shape) for o in big["pool_w"]),
        "pool_b": small[9],
        "pool_scale": small[5],
        "w_out": tuple(o.reshape(w_out.shape) for o in big["w_out"]),
        "post_norm_g": small[2],
    }
    names = list(results)
    return (loss, grad_x,
            *[results[n][0] for n in names], *[results[n][1] for n in names],
            *[results[n][2] for n in names], *[results[n][3] for n in names])
```

```python
import functools

import jax
import jax.numpy as jnp
from jax import lax
from jax.experimental import pallas as pl
from jax.experimental.pallas import tpu as pltpu

F32 = jnp.float32
BF16 = jnp.bfloat16

NORM_EPS = 1e-6
LRU_C = 8.0
CONV_WIDTH = 4
MAX_POOL_WINDOW = 16
HALO = 16
ADAM_LR = 0.001
ADAM_B1 = 0.9
ADAM_B2 = 0.999
ADAM_EPS = 1e-08
ADAM_WD = 0.01
ADAM_STEP = 10

V7X_VMEM_LIMIT_BYTES = 56 * 1024 * 1024
MATMUL_ROWS = 512
SCAN_ROWS = 512
ELEMENTWISE_ROWS = 512

MESH = pl.DeviceIdType.MESH
ANY = pl.BlockSpec(memory_space=pl.ANY)
VMEM = pl.BlockSpec(memory_space=pltpu.VMEM)
HBM = pl.BlockSpec(memory_space=pltpu.HBM)
SEM = pl.BlockSpec(memory_space=pltpu.SEMAPHORE)
DATAFLOW_EFFECT = pltpu.SideEffectType.DATAFLOW_SIDE_EFFECTING

NT_DIMS = (((1,), (1,)), ((), ()))
TN_DIMS = (((0,), (0,)), ((), ()))


def _params(n_grid_axes):
    return pltpu.CompilerParams(dimension_semantics=("arbitrary",) * n_grid_axes,
                                vmem_limit_bytes=V7X_VMEM_LIMIT_BYTES)


def _tile(total, want):
    t = min(want, max(total // 2, HALO))
    assert total % t == 0 and t % HALO == 0, (total, t)
    return t


def _row_tile(rows):
    for t in range(min(rows, ELEMENTWISE_ROWS) // 8 * 8, 0, -8):
        if rows % t == 0:
            return t
    return rows


def _sigmoid(z):
    return 1.0 / (1.0 + jnp.exp(-z))


def _softplus(z):
    return jnp.maximum(z, 0.0) + jnp.log(1.0 + jnp.exp(-jnp.abs(z)))


def _neg_expm1(z):
    return -jnp.tanh(0.5 * z) * (jnp.exp(z) + 1.0)


def _colsum(v):
    return jnp.sum(v, axis=0, keepdims=True)


def _prenorm(xt, vec_ref):
    rs = lax.rsqrt(jnp.mean(xt * xt, axis=-1, keepdims=True) + NORM_EPS)
    xn = xt * rs
    h = xn * vec_ref[3:4, :] * (1.0 + vec_ref[1:2, :]) + vec_ref[0:1, :]
    return h, xn, rs


def _shift_down(v, d, fill):
    t = v.shape[0]
    if d % 8 == 0:
        return jnp.concatenate([jnp.full((d, v.shape[1]), fill, v.dtype), v[:t - d]], axis=0)
    row = lax.broadcasted_iota(jnp.int32, v.shape, 0)
    return jnp.where(row >= d, pltpu.roll(v, d, 0), fill)


def _shift_up(v, d, fill):
    t = v.shape[0]
    if d % 8 == 0:
        return jnp.concatenate([v[d:], jnp.full((d, v.shape[1]), fill, v.dtype)], axis=0)
    row = lax.broadcasted_iota(jnp.int32, v.shape, 0)
    return jnp.where(row < t - d, pltpu.roll(v, t - d, 0), fill)


def _scan_fwd(a, v, h_before):
    d = 1
    while d < a.shape[0]:
        v = v + a * _shift_down(v, d, 0.0)
        a = a * _shift_down(a, d, 1.0)
        d *= 2
    return a * h_before + v


def _scan_rev(b, v):
    d = 1
    while d < b.shape[0]:
        v = v + b * _shift_up(v, d, 0.0)
        b = b * _shift_up(b, d, 0.0)
        d *= 2
    return v


def _inproj_fwd(x, vec, w_all, layer):
    s, d = x.shape
    p = w_all.shape[2]
    ts = _tile(s, MATMUL_ROWS)

    def body(x_ref, vec_ref, w_ref, proj_ref):
        h, _, _ = _prenorm(x_ref[...], vec_ref)
        hb = h.astype(BF16)
        for k in range(4):
            proj_ref[k] = jnp.dot(hb, w_ref[k], preferred_element_type=F32)

    return pl.pallas_call(
        body, name=f"inproj_fwd_l{layer}", grid=(s // ts,),
        in_specs=[pl.BlockSpec((ts, d), lambda i: (i, 0)),
                  pl.BlockSpec((8, d), lambda i: (0, 0)),
                  pl.BlockSpec((4, d, p), lambda i: (0, 0, 0))],
        out_specs=pl.BlockSpec((4, ts, p), lambda i: (0, i, 0)),
        out_shape=jax.ShapeDtypeStruct((4, s, p), F32),
        compiler_params=_params(1),
    )(x, vec, w_all)


HEADS_PER_STEP = 2
BWD_HEADS_PER_STEP = 1


def _rnn_gates(u, wa, wx, vec_ref, lanes):
    ub = u.astype(BF16)
    r = _sigmoid(jnp.dot(ub, wa, preferred_element_type=F32) + vec_ref[1:2, lanes])
    ig = _sigmoid(jnp.dot(ub, wx, preferred_element_type=F32) + vec_ref[2:3, lanes])
    sp = _softplus(-vec_ref[3:4, lanes])
    log_a = (-LRU_C) * r * sp
    return ub, r, ig, sp, log_a


def _conv(xbuf, cw_ref, vec_ref, lanes, ts):
    u = vec_ref[0:1, lanes] + cw_ref[CONV_WIDTH - 1:CONV_WIDTH, lanes] * xbuf[pl.ds(HALO, ts), lanes]
    for k in range(CONV_WIDTH - 1):
        u = u + cw_ref[k:k + 1, lanes] * xbuf[pl.ds(HALO - (CONV_WIDTH - 1) + k, ts), lanes]
    return u


def _rnn_fwd(proj, cw, vec, wa, wx, layer):
    _, s, d = proj.shape
    nh, hd, _ = wa.shape
    ts = _tile(s, SCAN_ROWS)
    hps = HEADS_PER_STEP
    wl = hps * hd

    def body(proj_ref, cw_ref, vec_ref, wa_ref, wx_ref, ycat_ref, hs_ref, xbuf, hlast):
        i = pl.program_id(1)

        @pl.when(i == 0)
        def _():
            xbuf[0:HALO, :] = jnp.zeros((HALO, wl), F32)
            hlast[...] = jnp.zeros_like(hlast)

        xbuf[pl.ds(HALO, ts), :] = proj_ref[0]
        for hh in range(hps):
            lanes = slice(hh * hd, (hh + 1) * hd)
            u = _conv(xbuf, cw_ref, vec_ref, lanes, ts)
            _, _, ig, _, log_a = _rnn_gates(u, wa_ref[hh], wx_ref[hh], vec_ref, lanes)
            a = jnp.exp(log_a)
            mult = jnp.sqrt(_neg_expm1(2.0 * log_a))
            hs = _scan_fwd(a, mult * (ig * u), hlast[0:1, lanes])
            hs_ref[:, lanes] = hs
            hlast[0:1, lanes] = hs_ref[ts - 1:ts, lanes]
            g = proj_ref[1, :, lanes]
            ycat_ref[:, lanes] = (hs * (g * _sigmoid(g))).astype(BF16)
        xbuf[0:HALO, :] = xbuf[pl.ds(ts, HALO), :]

    return pl.pallas_call(
        body, name=f"rnn_fwd_l{layer}", grid=(nh // hps, s // ts),
        in_specs=[pl.BlockSpec((2, ts, wl), lambda h, i: (0, i, h)),
                  pl.BlockSpec((CONV_WIDTH, wl), lambda h, i: (0, h)),
                  pl.BlockSpec((8, wl), lambda h, i: (0, h)),
                  pl.BlockSpec((hps, hd, hd), lambda h, i: (h, 0, 0)),
                  pl.BlockSpec((hps, hd, hd), lambda h, i: (h, 0, 0))],
        out_specs=[pl.BlockSpec((ts, wl), lambda h, i: (i, h)),
                   pl.BlockSpec((ts, wl), lambda h, i: (i, h))],
        out_shape=[jax.ShapeDtypeStruct((s, 2 * d), BF16), jax.ShapeDtypeStruct((s, d), F32)],
        scratch_shapes=[pltpu.VMEM((ts + HALO, wl), F32), pltpu.VMEM((8, wl), F32)],
        compiler_params=_params(2),
    )(proj, cw, vec, wa, wx)


def _inv_count(i, ts, lanes, win):
    t = i * ts + lax.broadcasted_iota(jnp.int32, (ts, lanes), 0)
    return 1.0 / jnp.minimum(t + 1, win).astype(F32)


def _window_sum(ext, win, forward):
    rows = ext.shape[0]
    s, d = ext, 1
    while d < win:
        s = s + pltpu.roll(s, d if forward else rows - d, 0)
        d *= 2
    return s


def _pooled(xbuf, xt, lanes, win, inv_cnt, ts):
    acc = _window_sum(xbuf[:, lanes], win, True)[HALO:, :]
    return acc * inv_cnt - xt


def _pool_fwd(proj, ycat, pw, vec, layer):
    _, s, d = proj.shape
    ng, gd, _ = pw.shape
    ts = _tile(s, MATMUL_ROWS)

    def body(proj_ref, ycat_in, pw_ref, vec_ref, ycat_ref, xbuf):
        del ycat_in
        i = pl.program_id(0)

        @pl.when(i == 0)
        def _():
            xbuf[0:HALO, :] = jnp.zeros((HALO, d), F32)

        xbuf[pl.ds(HALO, ts), :] = proj_ref[0]
        for g in range(ng):
            lanes = slice(g * gd, (g + 1) * gd)
            win = 2 << g
            xt = proj_ref[0, :, lanes]
            pooled = _pooled(xbuf, xt, lanes, win, _inv_count(i, ts, gd, win), ts).astype(BF16)
            z = jnp.dot(pooled, pw_ref[g], preferred_element_type=F32) + vec_ref[0:1, lanes]
            gg = proj_ref[1, :, lanes]
            ycat_ref[:, lanes] = (z * vec_ref[1:2, lanes] * (gg * _sigmoid(gg))).astype(BF16)
        xbuf[0:HALO, :] = xbuf[pl.ds(ts, HALO), :]

    return pl.pallas_call(
        body, name=f"pool_fwd_l{layer}", grid=(s // ts,),
        in_specs=[pl.BlockSpec((2, ts, d), lambda i: (1, i, 0)),
                  ANY,
                  pl.BlockSpec((ng, gd, gd), lambda i: (0, 0, 0)),
                  pl.BlockSpec((8, d), lambda i: (0, 0))],
        out_specs=pl.BlockSpec((ts, d), lambda i: (i, 1)),
        out_shape=jax.ShapeDtypeStruct((s, 2 * d), BF16),
        input_output_aliases={1: 0},
        scratch_shapes=[pltpu.VMEM((ts + HALO, d), F32)],
        compiler_params=_params(1),
    )(proj, ycat, pw, vec)


def _outproj_fwd(ycat, w_all, x, vec, target, layer):
    s, d = x.shape
    nk, kd = w_all.shape[0], w_all.shape[1]
    ts = _tile(s, MATMUL_ROWS)
    last = target is not None

    def body(*refs):
        if last:
            ycat_ref, w_ref, x_ref, vec_ref, tgt_ref, y_ref, xo_ref, sq_ref = refs
        else:
            ycat_ref, w_ref, x_ref, vec_ref, y_ref, xo_ref = refs
        y = jnp.dot(ycat_ref[:, 0:kd], w_ref[0], preferred_element_type=F32)
        for k in range(1, nk):
            y = y + jnp.dot(ycat_ref[:, k * kd:(k + 1) * kd], w_ref[k], preferred_element_type=F32)
        y_ref[...] = y
        rs = lax.rsqrt(jnp.mean(y * y, axis=-1, keepdims=True) + NORM_EPS)
        xo = x_ref[...] + vec_ref[2:3, :] * (y * rs * vec_ref[4:5, :])
        if last:
            err = xo - tgt_ref[...]
            xo_ref[...] = err * (1.0 / d)

            @pl.when(pl.program_id(0) == 0)
            def _():
                sq_ref[...] = jnp.zeros_like(sq_ref)

            sq_ref[...] += jnp.sum(err * err)
        else:
            xo_ref[...] = xo

    row = pl.BlockSpec((ts, d), lambda i: (i, 0))
    in_specs = [pl.BlockSpec((ts, nk * kd), lambda i: (i, 0)),
                pl.BlockSpec((nk, kd, d), lambda i: (0, 0, 0)),
                row, pl.BlockSpec((8, d), lambda i: (0, 0))]
    out_specs = [row, row]
    out_shape = [jax.ShapeDtypeStruct((s, d), F32), jax.ShapeDtypeStruct((s, d), F32)]
    args = [ycat, w_all, x, vec]
    if last:
        in_specs.append(row)
        args.append(target)
        out_specs.append(pl.BlockSpec((8, 128), lambda i: (0, 0)))
        out_shape.append(jax.ShapeDtypeStruct((8, 128), F32))
    out = pl.pallas_call(
        body, name=f"outproj_fwd_l{layer}", grid=(s // ts,),
        in_specs=in_specs, out_specs=out_specs, out_shape=out_shape,
        compiler_params=_params(1),
    )(*args)
    return (out[0], out[1], out[2]) if last else (out[0], out[1], None)


def _outproj_bwd(dxo, y, ycat, w_all, vec, layer):
    s, d = dxo.shape
    nk, kd = w_all.shape[0], w_all.shape[1]
    ts = _tile(s, MATMUL_ROWS)
    nt = s // ts

    def body(dxo_ref, y_ref, ycat_ref, w_ref, vec_ref, dycat_ref, dw_ref, dvec_ref, acc):
        i = pl.program_id(0)

        @pl.when(i == 0)
        def _():
            acc[...] = jnp.zeros_like(acc)
            dvec_ref[...] = jnp.zeros_like(dvec_ref)

        yt = y_ref[...]
        rs = lax.rsqrt(jnp.mean(yt * yt, axis=-1, keepdims=True) + NORM_EPS)
        yhat = yt * rs
        gate, gpost = vec_ref[2:3, :], vec_ref[4:5, :]
        dxo_t = dxo_ref[...]
        dyn = dxo_t * gate
        dvec_ref[0:1, :] += _colsum(dxo_t * (yhat * gpost))
        dvec_ref[1:2, :] += _colsum(dyn * yhat)
        t = dyn * gpost
        dy = (rs * (t - yhat * jnp.mean(t * yhat, axis=-1, keepdims=True))).astype(BF16)
        for k in range(nk):
            cols = slice(k * kd, (k + 1) * kd)
            dycat_ref[:, cols] = lax.dot_general(dy, w_ref[k], NT_DIMS, preferred_element_type=F32)
            acc[k] += lax.dot_general(ycat_ref[:, cols], dy, TN_DIMS, preferred_element_type=F32)

        @pl.when(i == nt - 1)
        def _():
            dw_ref[...] = acc[...].astype(BF16)

    row = pl.BlockSpec((ts, d), lambda i: (i, 0))
    wide = pl.BlockSpec((ts, nk * kd), lambda i: (i, 0))
    return pl.pallas_call(
        body, name=f"outproj_bwd_l{layer}", grid=(nt,),
        in_specs=[row, row, wide,
                  pl.BlockSpec((nk, kd, d), lambda i: (0, 0, 0)),
                  pl.BlockSpec((8, d), lambda i: (0, 0))],
        out_specs=[wide,
                   pl.BlockSpec((nk, kd, d), lambda i: (0, 0, 0)),
                   pl.BlockSpec((8, d), lambda i: (0, 0))],
        out_shape=[jax.ShapeDtypeStruct((s, nk * kd), F32),
                   jax.ShapeDtypeStruct((nk, kd, d), BF16),
                   jax.ShapeDtypeStruct((8, d), F32)],
        scratch_shapes=[pltpu.VMEM((nk, kd, d), F32)],
        compiler_params=_params(1),
    )(dxo, y, ycat, w_all, vec)


def _halo_index(ts, nt):
    return lambda j: jnp.maximum((nt - 1 - j) * (ts // HALO) - 1, 0)


def _rnn_bwd(proj, hs, dycat, cw, vec, wa, wx, layer):
    _, s, d = proj.shape
    nh, hd, _ = wa.shape
    ts = _tile(s, SCAN_ROWS)
    nt = s // ts
    halo = _halo_index(ts, nt)
    hps = BWD_HEADS_PER_STEP
    wl = hps * hd

    def body(proj_ref, xh_ref, hs_ref, hsh_ref, dy_ref, cw_ref, vec_ref, wa_ref, wx_ref,
             dproj_ref, dgates_ref, dvec_ref, xbuf, hbuf, dubuf, carry, dw_acc):
        j = pl.program_id(1)
        first_tile = j == nt - 1

        @pl.when(j == 0)
        def _():
            dubuf[pl.ds(ts, HALO), :] = jnp.zeros((HALO, wl), F32)
            carry[...] = jnp.zeros_like(carry)
            dw_acc[...] = jnp.zeros_like(dw_acc)
            dvec_ref[...] = jnp.zeros_like(dvec_ref)

        xbuf[0:HALO, :] = jnp.where(first_tile, 0.0, xh_ref[0])
        xbuf[pl.ds(HALO, ts), :] = proj_ref[0]
        hbuf[0:HALO, :] = jnp.where(first_tile, 0.0, hsh_ref[...])
        hbuf[pl.ds(HALO, ts), :] = hs_ref[...]

        for hh in range(hps):
            lanes = slice(hh * hd, (hh + 1) * hd)
            wa, wx = wa_ref[hh], wx_ref[hh]
            hs = hs_ref[:, lanes]
            u = _conv(xbuf, cw_ref, vec_ref, lanes, ts)
            ub, r, ig, sp, log_a = _rnn_gates(u, wa, wx, vec_ref, lanes)
            a = jnp.exp(log_a)
            e2 = jnp.exp(2.0 * log_a)
            one_minus_a2 = _neg_expm1(2.0 * log_a)
            inv_mult = lax.rsqrt(one_minus_a2)
            mult = one_minus_a2 * inv_mult

            g = proj_ref[1, :, lanes]
            sg = _sigmoid(g)
            dyc = dy_ref[:, lanes]
            dproj_ref[1, :, lanes] = (dyc * hs * (sg * (1.0 + g * (1.0 - sg)))).astype(BF16)

            row = lax.broadcasted_iota(jnp.int32, (ts, hd), 0)
            dhs = dyc * (g * sg) + jnp.where(row == ts - 1, carry[0:1, lanes], 0.0)
            dh = _scan_rev(_shift_up(a, 1, 0.0), dhs)
            carry[:, lanes] = (a * dh)[0:8, :]

            h_prev = hbuf[pl.ds(HALO - 1, ts), lanes]
            dlog_a = dh * h_prev * a - dh * (ig * u) * (e2 * inv_mult)
            di = dh * mult * u
            dzr = dlog_a * ((-LRU_C) * sp) * (r * (1.0 - r))
            dzi = di * (ig * (1.0 - ig))
            dvec_ref[3:4, lanes] += _colsum(dlog_a * r) * (LRU_C * _sigmoid(-vec_ref[3:4, lanes]))
            dvec_ref[1:2, lanes] += _colsum(dzr)
            dvec_ref[2:3, lanes] += _colsum(dzi)
            dzr_b, dzi_b = dzr.astype(BF16), dzi.astype(BF16)
            dw_acc[0, hh] += lax.dot_general(ub, dzr_b, TN_DIMS, preferred_element_type=F32)
            dw_acc[1, hh] += lax.dot_general(ub, dzi_b, TN_DIMS, preferred_element_type=F32)
            du = (dh * mult * ig
                  + lax.dot_general(dzr_b, wa, NT_DIMS, preferred_element_type=F32)
                  + lax.dot_general(dzi_b, wx, NT_DIMS, preferred_element_type=F32))
            dvec_ref[0:1, lanes] += _colsum(du)
            for k in range(CONV_WIDTH):
                dvec_ref[4 + k:5 + k, lanes] += _colsum(du * xbuf[pl.ds(HALO - (CONV_WIDTH - 1) + k, ts), lanes])

            dubuf[0:ts, lanes] = du
            dx = cw_ref[CONV_WIDTH - 1:CONV_WIDTH, lanes] * du
            for k in range(CONV_WIDTH - 1):
                dx = dx + cw_ref[k:k + 1, lanes] * dubuf[pl.ds(CONV_WIDTH - 1 - k, ts), lanes]
            dproj_ref[0, :, lanes] = dx.astype(BF16)
        dubuf[pl.ds(ts, HALO), :] = dubuf[0:HALO, :]

        @pl.when(first_tile)
        def _():
            dgates_ref[...] = dw_acc[...].astype(BF16)

    rev = lambda h, j: (nt - 1 - j, h)
    return pl.pallas_call(
        body, name=f"rnn_bwd_l{layer}", grid=(nh // hps, nt),
        in_specs=[pl.BlockSpec((2, ts, wl), lambda h, j: (0, nt - 1 - j, h)),
                  pl.BlockSpec((1, HALO, wl), lambda h, j: (0, halo(j), h)),
                  pl.BlockSpec((ts, wl), rev),
                  pl.BlockSpec((HALO, wl), lambda h, j: (halo(j), h)),
                  pl.BlockSpec((ts, wl), rev),
                  pl.BlockSpec((CONV_WIDTH, wl), lambda h, j: (0, h)),
                  pl.BlockSpec((8, wl), lambda h, j: (0, h)),
                  pl.BlockSpec((hps, hd, hd), lambda h, j: (h, 0, 0)),
                  pl.BlockSpec((hps, hd, hd), lambda h, j: (h, 0, 0))],
        out_specs=[pl.BlockSpec((2, ts, wl), lambda h, j: (0, nt - 1 - j, h)),
                   pl.BlockSpec((2, hps, hd, hd), lambda h, j: (0, h, 0, 0)),
                   pl.BlockSpec((16, wl), lambda h, j: (0, h))],
        out_shape=[jax.ShapeDtypeStruct((4, s, d), BF16),
                   jax.ShapeDtypeStruct((2, nh, hd, hd), BF16),
                   jax.ShapeDtypeStruct((16, d), F32)],
        scratch_shapes=[pltpu.VMEM((ts + HALO, wl), F32), pltpu.VMEM((ts + HALO, wl), F32),
                        pltpu.VMEM((ts + HALO, wl), F32), pltpu.VMEM((8, wl), F32),
                        pltpu.VMEM((2, hps, hd, hd), F32)],
        compiler_params=_params(2),
    )(proj, proj, hs, hs, dycat, cw, vec, wa, wx)


def _pool_bwd(proj, dycat, dproj, pw, vec, layer):
    _, s, d = proj.shape
    ng, gd, _ = pw.shape
    ts = _tile(s, MATMUL_ROWS)
    nt = s // ts
    halo = _halo_index(ts, nt)

    def body(proj_ref, xh_ref, dy_ref, dproj_in, pw_ref, vec_ref, dproj_ref, dpw_ref, dvec_ref, xbuf, qbuf, acc):
        del dproj_in
        j = pl.program_id(0)
        i = nt - 1 - j

        @pl.when(j == 0)
        def _():
            qbuf[pl.ds(ts, HALO), :] = jnp.zeros((HALO, d), F32)
            acc[...] = jnp.zeros_like(acc)
            dvec_ref[...] = jnp.zeros_like(dvec_ref)

        xbuf[0:HALO, :] = jnp.where(i == 0, 0.0, xh_ref[0])
        xbuf[pl.ds(HALO, ts), :] = proj_ref[0]
        for g in range(ng):
            lanes = slice(g * gd, (g + 1) * gd)
            win = 2 << g
            xt = proj_ref[0, :, lanes]
            inv_cnt = _inv_count(i, ts, gd, win)
            pooled = _pooled(xbuf, xt, lanes, win, inv_cnt, ts).astype(BF16)
            z = jnp.dot(pooled, pw_ref[g], preferred_element_type=F32) + vec_ref[0:1, lanes]
            scale = vec_ref[1:2, lanes]
            gg = proj_ref[1, :, lanes]
            sg = _sigmoid(gg)
            dyc = dy_ref[:, lanes]
            dyp = dyc * (gg * sg)
            dproj_ref[1, :, lanes] = (dyc * (z * scale) * (sg * (1.0 + gg * (1.0 - sg)))).astype(BF16)
            dvec_ref[1:2, lanes] += _colsum(dyp * z)
            dz = dyp * scale
            dvec_ref[0:1, lanes] += _colsum(dz)
            dz_b = dz.astype(BF16)
            acc[g] += lax.dot_general(pooled, dz_b, TN_DIMS, preferred_element_type=F32)
            dpooled = lax.dot_general(dz_b, pw_ref[g], NT_DIMS, preferred_element_type=F32)

            qbuf[0:ts, lanes] = dpooled * inv_cnt
            dx = _window_sum(qbuf[:, lanes], win, False)[0:ts, :] - dpooled
            dproj_ref[0, :, lanes] = dx.astype(BF16)
        qbuf[pl.ds(ts, HALO), :] = qbuf[0:HALO, :]

        @pl.when(j == nt - 1)
        def _():
            dpw_ref[...] = acc[...].astype(BF16)

    return pl.pallas_call(
        body, name=f"pool_bwd_l{layer}", grid=(nt,),
        in_specs=[pl.BlockSpec((2, ts, d), lambda j: (1, nt - 1 - j, 0)),
                  pl.BlockSpec((1, HALO, d), lambda j: (2, halo(j), 0)),
                  pl.BlockSpec((ts, d), lambda j: (nt - 1 - j, 1)),
                  ANY,
                  pl.BlockSpec((ng, gd, gd), lambda j: (0, 0, 0)),
                  pl.BlockSpec((8, d), lambda j: (0, 0))],
        out_specs=[pl.BlockSpec((2, ts, d), lambda j: (1, nt - 1 - j, 0)),
                   pl.BlockSpec((ng, gd, gd), lambda j: (0, 0, 0)),
                   pl.BlockSpec((8, d), lambda j: (0, 0))],
        out_shape=[jax.ShapeDtypeStruct((4, s, d), BF16),
                   jax.ShapeDtypeStruct((ng, gd, gd), BF16),
                   jax.ShapeDtypeStruct((8, d), F32)],
        input_output_aliases={3: 0},
        scratch_shapes=[pltpu.VMEM((ts + HALO, d), F32), pltpu.VMEM((ts + HALO, d), F32),
                        pltpu.VMEM((ng, gd, gd), F32)],
        compiler_params=_params(1),
    )(proj, proj, dycat, dproj, pw, vec)


def _inproj_bwd_x(dproj, w_all, x, dxo, vec, layer):
    s, d = x.shape
    p = w_all.shape[2]
    ts = _tile(s, MATMUL_ROWS)

    def body(dp_ref, w_ref, x_ref, dxo_ref, vec_ref, dx_ref, dvec_ref):
        @pl.when(pl.program_id(0) == 0)
        def _():
            dvec_ref[...] = jnp.zeros_like(dvec_ref)

        dh = lax.dot_general(dp_ref[0], w_ref[0], NT_DIMS, preferred_element_type=F32)
        for k in range(1, 4):
            dh = dh + lax.dot_general(dp_ref[k], w_ref[k], NT_DIMS, preferred_element_type=F32)
        _, xn, rs = _prenorm(x_ref[...], vec_ref)
        gpre, scale1 = vec_ref[3:4, :], 1.0 + vec_ref[1:2, :]
        dvec_ref[0:1, :] += _colsum(dh)
        dvec_ref[1:2, :] += _colsum(dh * (xn * gpre))
        dvec_ref[2:3, :] += _colsum(dh * (xn * scale1))
        t = dh * (gpre * scale1)
        dx_ref[...] = dxo_ref[...] + rs * (t - xn * jnp.mean(t * xn, axis=-1, keepdims=True))

    row = pl.BlockSpec((ts, d), lambda i: (i, 0))
    return pl.pallas_call(
        body, name=f"inproj_bwd_x_l{layer}", grid=(s // ts,),
        in_specs=[pl.BlockSpec((4, ts, p), lambda i: (0, i, 0)),
                  pl.BlockSpec((4, d, p), lambda i: (0, 0, 0)),
                  row, row, pl.BlockSpec((8, d), lambda i: (0, 0))],
        out_specs=[row, pl.BlockSpec((8, d), lambda i: (0, 0))],
        out_shape=[jax.ShapeDtypeStruct((s, d), F32), jax.ShapeDtypeStruct((8, d), F32)],
        compiler_params=_params(1),
    )(dproj, w_all, x, dxo, vec)


def _inproj_bwd_w(dproj, x, vec, layer):
    s, d = x.shape
    p = dproj.shape[2]
    ts = _tile(s, MATMUL_ROWS)
    nt = s // ts

    def body(dp_ref, x_ref, vec_ref, dw_ref, acc):
        i = pl.program_id(0)

        @pl.when(i == 0)
        def _():
            acc[...] = jnp.zeros_like(acc)

        h, _, _ = _prenorm(x_ref[...], vec_ref)
        hb = h.astype(BF16)
        for k in range(4):
            acc[k] += lax.dot_general(hb, dp_ref[k], TN_DIMS, preferred_element_type=F32)

        @pl.when(i == nt - 1)
        def _():
            dw_ref[...] = acc[...].astype(BF16)

    return pl.pallas_call(
        body, name=f"inproj_bwd_w_l{layer}", grid=(nt,),
        in_specs=[pl.BlockSpec((4, ts, p), lambda i: (0, i, 0)),
                  pl.BlockSpec((ts, d), lambda i: (i, 0)),
                  pl.BlockSpec((8, d), lambda i: (0, 0))],
        out_specs=pl.BlockSpec((4, d, p), lambda i: (0, 0, 0)),
        out_shape=jax.ShapeDtypeStruct((4, d, p), BF16),
        scratch_shapes=[pltpu.VMEM((4, d, p), F32)],
        compiler_params=_params(1),
    )(dproj, x, vec)


def _sum_slots(stacked, name):
    n, rows, cols = stacked.shape
    tr = _row_tile(rows)

    def body(in_ref, out_ref):
        total = in_ref[0].astype(F32)
        for b in range(1, n):
            total = total + in_ref[b].astype(F32)
        out_ref[...] = total

    return pl.pallas_call(
        body, name=name, grid=(rows // tr,),
        in_specs=[pl.BlockSpec((n, tr, cols), lambda i: (0, i, 0))],
        out_specs=pl.BlockSpec((tr, cols), lambda i: (i, 0)),
        out_shape=jax.ShapeDtypeStruct((rows, cols), F32),
        compiler_params=_params(1),
    )(stacked)


def _adamw(w, m, v, grads, name):
    shape = w.shape
    cols = shape[-1]
    rows = w.size // cols
    tr = _row_tile(rows)
    n = len(grads)

    def body(*refs):
        w_ref, m_ref, v_ref = refs[:3]
        g_refs = refs[3:3 + n]
        g_out, d_out, m_out, v_out = refs[3 + n:]
        g = g_refs[0][...]
        for r in g_refs[1:]:
            g = g + r[...]
        m_new = ADAM_B1 * m_ref[...] + (1.0 - ADAM_B1) * g
        v_new = ADAM_B2 * v_ref[...] + (1.0 - ADAM_B2) * (g * g)
        m_hat = m_new / (1.0 - ADAM_B1 ** ADAM_STEP)
        v_hat = v_new / (1.0 - ADAM_B2 ** ADAM_STEP)
        g_out[...] = g
        d_out[...] = (-ADAM_LR) * (m_hat / (jnp.sqrt(v_hat) + ADAM_EPS) + ADAM_WD * w_ref[...])
        m_out[...] = m_new
        v_out[...] = v_new

    blk = pl.BlockSpec((tr, cols), lambda i: (i, 0))
    outs = pl.pallas_call(
        body, name=name, grid=(rows // tr,),
        in_specs=[blk] * (3 + n), out_specs=[blk] * 4,
        out_shape=[jax.ShapeDtypeStruct((rows, cols), F32)] * 4,
        compiler_params=_params(1),
    )(*[a.reshape(rows, cols) for a in (w, m, v, *grads)])
    return tuple(o.reshape(shape) for o in outs)


def _adam_update(w, m, v, g):
    m_new = ADAM_B1 * m + (1.0 - ADAM_B1) * g
    v_new = ADAM_B2 * v + (1.0 - ADAM_B2) * (g * g)
    m_hat = m_new / (1.0 - ADAM_B1 ** ADAM_STEP)
    v_hat = v_new / (1.0 - ADAM_B2 ** ADAM_STEP)
    return (-ADAM_LR) * (m_hat / (jnp.sqrt(v_hat) + ADAM_EPS) + ADAM_WD * w), m_new, v_new


def _adamw_layer(w, m, v, grads, layer, prev, name, grad_row_offset=0):
    nl = w.shape[0]
    cols = w.shape[-1]
    rows = w.size // (nl * cols)
    tr = _row_tile(rows)
    off = layer * (rows // tr)
    g_off = grad_row_offset // tr
    n = len(grads)
    n_prev = 0 if prev is None else 4

    def body(*refs):
        w_ref, m_ref, v_ref = refs[:3]
        g_refs = refs[3:3 + n]
        g_out, d_out, m_out, v_out = refs[3 + n + n_prev:]
        g = g_refs[0][...]
        for r in g_refs[1:]:
            g = g + r[...]
        g_out[...] = g
        d_out[...], m_out[...], v_out[...] = _adam_update(w_ref[...], m_ref[...], v_ref[...], g)

    mine = pl.BlockSpec((tr, cols), lambda i: (off + i, 0))
    args = [a.reshape(nl * rows, cols) for a in (w, m, v)] + [g.reshape(-1, cols) for g in grads]
    outs = pl.pallas_call(
        body, name=name, grid=(rows // tr,),
        in_specs=[mine] * 3 + [pl.BlockSpec((tr, cols), lambda i: (g_off + i, 0))] * n + [ANY] * n_prev,
        out_specs=[mine] * 4,
        out_shape=[jax.ShapeDtypeStruct((nl * rows, cols), F32)] * 4,
        input_output_aliases={3 + n + k: k for k in range(n_prev)},
        compiler_params=_params(1),
    )(*args, *(prev or ()))
    return tuple(outs)


def _into_slot(a, dtype, chip_arr, name):
    rows, cols = a.shape
    tr = _row_tile(rows)

    def body(chip_ref, a_ref, out_ref):
        del chip_ref
        out_ref[...] = a_ref[...].astype(dtype)

    return pl.pallas_call(
        body, name=name,
        grid_spec=pltpu.PrefetchScalarGridSpec(
            num_scalar_prefetch=1, grid=(rows // tr,),
            in_specs=[pl.BlockSpec((tr, cols), lambda i, chip: (i, 0))],
            out_specs=pl.BlockSpec((None, tr, cols), lambda i, chip: (chip[0], i, 0))),
        out_shape=jax.ShapeDtypeStruct((4, rows, cols), dtype),
        compiler_params=_params(1),
    )(chip_arr, a)


def _sum_owner(own, land, chip_arr, own_block, own_index, name):
    blk = land.shape[1:]
    tr = _row_tile(blk[-2])
    steps = blk[-2] // tr
    tile = (*blk[:-2], tr, blk[-1])
    lead = (0,) * (len(blk) - 2)

    def body(chip_ref, own_ref, l1, l2, l3, out_ref):
        del chip_ref
        out_ref[...] = (own_ref[...].astype(F32) + l1[...].astype(F32)) + (l2[...].astype(F32) + l3[...].astype(F32))

    def landed(k):
        return pl.BlockSpec((None, *tile), lambda i, chip: (chip[0] ^ k, *lead, i, 0))

    return pl.pallas_call(
        body, name=name,
        grid_spec=pltpu.PrefetchScalarGridSpec(
            num_scalar_prefetch=1, grid=(steps,),
            in_specs=[pl.BlockSpec(own_block(tr), own_index), landed(1), landed(2), landed(3)],
            out_specs=pl.BlockSpec(tile, lambda i, chip: (*lead, i, 0))),
        out_shape=jax.ShapeDtypeStruct(blk, F32),
        compiler_params=_params(1),
    )(chip_arr, own, land, land, land)


_WHOLE_VMEM = pltpu.CompilerParams(vmem_limit_bytes=V7X_VMEM_LIMIT_BYTES)


def _pack_vectors(modbuf, ada_b, pre_norm_g, post_norm_g, conv_b, gate_a_b, gate_x_b, lru_lambda):
    nl, d = pre_norm_g.shape
    n = modbuf.shape[2] // nl
    nh, hd = gate_a_b.shape[1], gate_a_b.shape[2]

    def body(mb_ref, ab_ref, pre_ref, post_ref, cb_ref, gab_ref, gxb_ref, lam_ref, *outs):
        for layer in range(nl):
            vec_ref, rvec_ref = outs[layer], outs[nl + layer]
            vec_ref[...] = jnp.zeros_like(vec_ref)
            rvec_ref[...] = jnp.zeros_like(rvec_ref)
            for k in range(4):
                piece = mb_ref[k, 0:1, layer * n:(layer + 1) * n] + ab_ref[layer:layer + 1, k * n:(k + 1) * n]
                lo = k * n
                while lo < (k + 1) * n:
                    row = lo // d
                    hi = min((row + 1) * d, (k + 1) * n)
                    vec_ref[row:row + 1, lo - row * d:hi - row * d] = piece[:, lo - k * n:hi - k * n]
                    lo = hi
            vec_ref[3:4, :] = pre_ref[layer:layer + 1, :]
            vec_ref[4:5, :] = post_ref[layer:layer + 1, :]
            rvec_ref[0:1, :] = cb_ref[layer:layer + 1, :]
            for h in range(nh):
                rvec_ref[1:2, h * hd:(h + 1) * hd] = gab_ref[layer, h:h + 1, :]
                rvec_ref[2:3, h * hd:(h + 1) * hd] = gxb_ref[layer, h:h + 1, :]
            rvec_ref[3:4, :] = lam_ref[layer:layer + 1, :]

    out = pl.pallas_call(
        body, name="pack_vectors", in_specs=[VMEM] * 8, out_specs=[VMEM] * (2 * nl),
        out_shape=[jax.ShapeDtypeStruct((8, d), F32)] * (2 * nl), compiler_params=_WHOLE_VMEM,
    )(modbuf, ada_b, pre_norm_g, post_norm_g, conv_b, gate_a_b, gate_x_b, lru_lambda)
    return list(out[:nl]), list(out[nl:])


def _pack_gathered(convw_g, poolb_g, pws, pool_scale, ng):
    nl, d = pool_scale.shape
    taps = convw_g.shape[1] // nl
    dq = convw_g.shape[2]
    gq, gd = poolb_g.shape[2], pws[0].shape[2]

    def body(cg_ref, pb_ref, *rest):
        pw_refs, ps_ref = rest[:nl], rest[nl]
        outs = rest[nl + 1:]
        for layer in range(nl):
            cw_ref, pvec_ref, pwf_ref = outs[layer], outs[nl + layer], outs[2 * nl + layer]
            pvec_ref[...] = jnp.zeros_like(pvec_ref)
            pvec_ref[1:2, :] = ps_ref[layer:layer + 1, :]
            for k in range(4):
                cw_ref[:, k * dq:(k + 1) * dq] = cg_ref[k, layer * taps:(layer + 1) * taps, :]
                for g in range(ng):
                    lo = g * gd + k * gq
                    pvec_ref[0:1, lo:lo + gq] = pb_ref[k, layer * ng + g:layer * ng + g + 1, :]
                    pwf_ref[g, k * gq:(k + 1) * gq, :] = pw_refs[layer][k, g * gq:(g + 1) * gq, :]

    out = pl.pallas_call(
        body, name="pack_gathered", in_specs=[VMEM] * (3 + nl), out_specs=[VMEM] * (3 * nl),
        out_shape=[jax.ShapeDtypeStruct((taps, d), F32)] * nl + [jax.ShapeDtypeStruct((8, d), F32)] * nl
        + [jax.ShapeDtypeStruct((ng, gd, gd), BF16)] * nl,
        compiler_params=_WHOLE_VMEM,
    )(convw_g, poolb_g, *pws, pool_scale)
    return list(out[:nl]), list(out[nl:2 * nl]), list(out[2 * nl:])


ROW_SHIFT, ROW_SCALE, ROW_PRE, ROW_GATE, ROW_POST = 0, 1, 2, 8, 9
ROW_CONV_B, ROW_GATE_A_B, ROW_GATE_X_B, ROW_LAMBDA, ROW_CONV_W = 16, 17, 18, 19, 20
ROW_POOL_B, ROW_POOL_SCALE, ROW_SQ = 32, 33, 40


def _adamw_small(totals, chip_arr, params):
    nl = len(totals)
    d = totals[0].shape[1]
    n_par = len(params)
    flat = [a for p in params for a in p]
    nh, hd = params[6][0].shape[1], params[6][0].shape[2]
    taps, dq = params[8][0].shape[1], params[8][0].shape[2]
    ng, gq = params[9][0].shape[1], params[9][0].shape[2]
    gd = d // ng

    def body(chip_ref, *refs):
        tot = refs[:nl]
        ins = refs[nl:nl + 3 * n_par]
        outs = refs[nl + 3 * n_par:]
        chip = chip_ref[0]

        def update(p, idx, g):
            delta, m_new, v_new = _adam_update(ins[3 * p][idx], ins[3 * p + 1][idx], ins[3 * p + 2][idx], g)
            outs[4 * p][idx] = g
            outs[4 * p + 1][idx] = delta
            outs[4 * p + 2][idx] = m_new
            outs[4 * p + 3][idx] = v_new

        def mine(candidates):
            g = candidates[0]
            for k in range(1, 4):
                g = jnp.where(chip == k, candidates[k], g)
            return g

        for layer in range(nl):
            t = tot[layer]
            row = (slice(layer, layer + 1), slice(None))
            for j, r in enumerate((ROW_SHIFT, ROW_SCALE, ROW_GATE)):
                update(0, (slice(layer, layer + 1), slice(j * d, (j + 1) * d)), t[r:r + 1, :])
            for p, r in ((1, ROW_PRE), (2, ROW_POST), (3, ROW_CONV_B), (4, ROW_LAMBDA), (5, ROW_POOL_SCALE)):
                update(p, row, t[r:r + 1, :])
            for h in range(nh):
                idx = (layer, slice(h, h + 1), slice(None))
                update(6, idx, t[ROW_GATE_A_B:ROW_GATE_A_B + 1, h * hd:(h + 1) * hd])
                update(7, idx, t[ROW_GATE_X_B:ROW_GATE_X_B + 1, h * hd:(h + 1) * hd])
            for k in range(taps):
                r = ROW_CONV_W + k
                update(8, (layer, slice(k, k + 1), slice(None)), mine([t[r:r + 1, c * dq:(c + 1) * dq] for c in range(4)]))
            for g in range(ng):
                cands = [t[ROW_POOL_B:ROW_POOL_B + 1, g * gd + c * gq:g * gd + (c + 1) * gq] for c in range(4)]
                update(9, (layer, slice(g, g + 1), slice(None)), mine(cands))

    out = pl.pallas_call(
        body, name="adamw_small",
        in_specs=[pl.BlockSpec(memory_space=pltpu.SMEM)] + [VMEM] * (nl + 3 * n_par),
        out_specs=[VMEM] * (4 * n_par),
        out_shape=[jax.ShapeDtypeStruct(p[0].shape, F32) for p in params for _ in range(4)],
        compiler_params=_WHOLE_VMEM,
    )(chip_arr, *totals, *flat)
    return [tuple(out[4 * p:4 * p + 4]) for p in range(n_par)]


def _adamw_ada_w_layer(c_t, dmod, w, m, v, layer, prev, name):
    nl, d, n = w.shape
    nb = c_t.shape[1]
    tr = _row_tile(d)
    off = layer * (d // tr)
    n_prev = 0 if prev is None else 4

    def body(c_ref, dm_ref, w_ref, m_ref, v_ref, *rest):
        g_out, d_out, m_out, v_out = rest[n_prev:]
        g = c_ref[:, 0:1] * dm_ref[0:1, :]
        for b in range(1, nb):
            g = g + c_ref[:, b:b + 1] * dm_ref[b:b + 1, :]
        g_out[...] = g
        d_out[...], m_out[...], v_out[...] = _adam_update(w_ref[...], m_ref[...], v_ref[...], g)

    mine = pl.BlockSpec((tr, n), lambda i: (off + i, 0))
    outs = pl.pallas_call(
        body, name=name, grid=(d // tr,),
        in_specs=[pl.BlockSpec((tr, nb), lambda i: (i, 0)), pl.BlockSpec((nb, n), lambda i: (0, 0))] + [mine] * 3
        + [ANY] * n_prev,
        out_specs=[mine] * 4,
        out_shape=[jax.ShapeDtypeStruct((nl * d, n), F32)] * 4,
        input_output_aliases={5 + k: k for k in range(n_prev)},
        compiler_params=_params(1),
    )(c_t, dmod, *[a.reshape(nl * d, n) for a in (w, m, v)], *(prev or ()))
    return tuple(outs)


def _place():
    x, y, c = lax.axis_index("x"), lax.axis_index("y"), lax.axis_index("c")
    return x, y, c


OTHER_CHIPS = ((1, 0), (0, 1), (1, 1))
OTHER_DEVICES = tuple((fx, fy, fc) for fx in (0, 1) for fy in (0, 1) for fc in (0, 1))[1:]


def _mod_exchange(c_row, ada_w):
    nl, d, n = ada_w.shape

    def body(c_ref, w_ref, cbuf, modbuf, token, cblk, mres, send_a, recv_a, send_c, recv_c):
        token[...] = jnp.zeros_like(token)
        x, y, c = _place()
        me = 4 * x + 2 * y + c
        chip = 2 * x + y
        cv = c_ref[...]
        cblk[...] = jnp.zeros_like(cblk)
        cblk[0:1, :] = cv * _sigmoid(cv)

        def rows_of(dev):
            return cbuf.at[pl.ds(pl.multiple_of(8 * dev, 8), 8), :]

        cbuf[pl.ds(pl.multiple_of(8 * me, 8), 8), :] = cblk[...]
        sends = []
        for j, (fx, fy, fc) in enumerate(OTHER_DEVICES):
            cp = pltpu.make_async_remote_copy(
                src_ref=cblk, dst_ref=rows_of(me), send_sem=send_a.at[j], recv_sem=recv_a.at[j],
                device_id=(x ^ fx, y ^ fy, c ^ fc), device_id_type=MESH)
            cp.start()
            sends.append(cp)
        for j, (fx, fy, fc) in enumerate(OTHER_DEVICES):
            peer = 4 * (x ^ fx) + 2 * (y ^ fy) + (c ^ fc)
            pltpu.make_async_remote_copy(
                src_ref=cblk, dst_ref=rows_of(peer), send_sem=send_a.at[j], recv_sem=recv_a.at[j],
                device_id=(x ^ fx, y ^ fy, c ^ fc), device_id_type=MESH).wait_recv()
        for cp in sends:
            cp.wait_send()

        call = cbuf[...]
        for layer in range(nl):
            mres[:, layer * n:(layer + 1) * n] = jnp.dot(
                call, w_ref[layer], preferred_element_type=F32, precision=lax.Precision.HIGHEST)

        def block_of(dev):
            return mres.at[pl.ds(pl.multiple_of(8 * dev, 8), 8), :]

        modbuf[chip] = mres[pl.ds(pl.multiple_of(8 * me, 8), 8), :]
        sends = []
        for j, (fx, fy) in enumerate(OTHER_CHIPS):
            peer = 4 * (x ^ fx) + 2 * (y ^ fy) + c
            cp = pltpu.make_async_remote_copy(
                src_ref=block_of(peer), dst_ref=modbuf.at[chip], send_sem=send_c.at[j], recv_sem=recv_c.at[j],
                device_id=(x ^ fx, y ^ fy, c), device_id_type=MESH)
            cp.start()
            sends.append(cp)
        for j, (fx, fy) in enumerate(OTHER_CHIPS):
            pltpu.make_async_remote_copy(
                src_ref=block_of(me), dst_ref=modbuf.at[2 * (x ^ fx) + (y ^ fy)],
                send_sem=send_c.at[j], recv_sem=recv_c.at[j],
                device_id=(x ^ fx, y ^ fy, c), device_id_type=MESH).wait_recv()
        for cp in sends:
            cp.wait_send()

    return pl.pallas_call(
        body, name="mod_exchange", in_specs=[VMEM, VMEM], out_specs=[VMEM, VMEM, VMEM],
        out_shape=[jax.ShapeDtypeStruct((64, d), F32), jax.ShapeDtypeStruct((4, 8, nl * n), F32),
                   jax.ShapeDtypeStruct((8, 128), F32)],
        scratch_shapes=[pltpu.VMEM((8, d), F32), pltpu.VMEM((64, nl * n), F32),
                        pltpu.SemaphoreType.DMA((7,)), pltpu.SemaphoreType.DMA((7,)),
                        pltpu.SemaphoreType.DMA((3,)), pltpu.SemaphoreType.DMA((3,))],
        compiler_params=pltpu.CompilerParams(vmem_limit_bytes=V7X_VMEM_LIMIT_BYTES, has_side_effects=True),
    )(c_row, ada_w)


def _in_hbm(a):
    return pltpu.with_memory_space_constraint(a, pltpu.HBM)


def _gather_copies(lands, split, over_ici):
    x, y, c = _place()
    chip = 2 * x + y
    out = []
    for t, land in enumerate(lands):
        half = land.shape[1] // 2
        mine = pl.ds(pl.multiple_of(c * half, half), half)
        theirs = pl.ds(pl.multiple_of((1 - c) * half, half), half)
        for j, (fx, fy) in enumerate(OTHER_CHIPS):
            them = 2 * (x ^ fx) + (y ^ fy)
            if over_ici and split[t]:
                out.append((land.at[chip, mine], land.at[chip, mine], land.at[them, mine], (x ^ fx, y ^ fy, c), 3 * t + j))
            elif over_ici:
                out.append((land.at[chip], land.at[chip], land.at[them], (x ^ fx, y ^ fy, c), 3 * t + j))
            elif split[t]:
                out.append((land.at[them, mine], land.at[them, mine], land.at[them, theirs], (x, y, 1 - c), 3 * t + j))
    return out


def _gather_start(lands, groups, split):
    n, ngr = len(lands), len(groups)

    def body(*refs):
        sems = refs[n:n + 2 * ngr]
        for gi, idxs in enumerate(groups):
            for src, dst, _, peer, k in _gather_copies([refs[i] for i in idxs], [split[i] for i in idxs], True):
                pltpu.make_async_remote_copy(src_ref=src, dst_ref=dst, send_sem=sems[2 * gi].at[k],
                                             recv_sem=sems[2 * gi + 1].at[k], device_id=peer, device_id_type=MESH).start()
        refs[-1][...] = jnp.zeros_like(refs[-1])

    sem_shapes = []
    for idxs in groups:
        sem_shapes += [pltpu.SemaphoreType.DMA((3 * len(idxs),))] * 2
    out = pl.pallas_call(
        body, name="weight_gather_start",
        in_specs=[HBM] * n, out_specs=[SEM] * (2 * ngr) + [HBM] * n + [VMEM],
        out_shape=sem_shapes + [pltpu.HBM(a.shape, a.dtype) for a in lands] + [jax.ShapeDtypeStruct((8, 128), F32)],
        input_output_aliases={i: 2 * ngr + i for i in range(n)},
        compiler_params=pltpu.CompilerParams(has_side_effects=DATAFLOW_EFFECT),
    )(*[_in_hbm(a) for a in lands])
    sems = [(out[2 * gi], out[2 * gi + 1]) for gi in range(ngr)]
    return sems, list(out[2 * ngr:2 * ngr + n]), out[-1]


def _gather_forward(lands, split, sems, after, name):
    n = len(lands)

    def body(*refs):
        ici_send, ici_recv = refs[n], refs[n + 1]
        fwd_send, fwd_recv = refs[n + 3], refs[n + 4]
        forwards = {k: (src, dst, peer) for src, dst, _, peer, k in _gather_copies(refs[:n], split, False)}
        for src, _, landed, peer, k in _gather_copies(refs[:n], split, True):
            cp = pltpu.make_async_remote_copy(src_ref=src, dst_ref=landed, send_sem=ici_send.at[k], recv_sem=ici_recv.at[k],
                                              device_id=peer, device_id_type=MESH)
            cp.wait_recv()
            if k in forwards:
                fsrc, fdst, fpeer = forwards[k]
                pltpu.make_async_remote_copy(src_ref=fsrc, dst_ref=fdst, send_sem=fwd_send.at[k], recv_sem=fwd_recv.at[k],
                                             device_id=fpeer, device_id_type=MESH).start()
            cp.wait_send()

    out = pl.pallas_call(
        body, name=name,
        in_specs=[HBM] * n + [SEM, SEM, ANY], out_specs=[SEM, SEM] + [HBM] * n,
        out_shape=[pltpu.SemaphoreType.DMA((3 * n,))] * 2 + [pltpu.HBM(a.shape, a.dtype) for a in lands],
        input_output_aliases={i: 2 + i for i in range(n)},
        compiler_params=pltpu.CompilerParams(has_side_effects=DATAFLOW_EFFECT),
    )(*lands, sems[0], sems[1], after)
    return (out[0], out[1]), list(out[2:])


def _gather_wait(lands, split, sems, name):
    n = len(lands)

    def body(*refs):
        send_sems, recv_sems = refs[n], refs[n + 1]
        for src, _, landed, peer, k in _gather_copies(refs[:n], split, False):
            cp = pltpu.make_async_remote_copy(src_ref=src, dst_ref=landed, send_sem=send_sems.at[k], recv_sem=recv_sems.at[k],
                                              device_id=peer, device_id_type=MESH)
            cp.wait_send()
            cp.wait_recv()

    out = pl.pallas_call(
        body, name=name,
        in_specs=[HBM] * n + [SEM, SEM], out_specs=[HBM] * n,
        out_shape=[pltpu.HBM(a.shape, a.dtype) for a in lands],
        input_output_aliases={i: i for i in range(n)},
        compiler_params=pltpu.CompilerParams(has_side_effects=DATAFLOW_EFFECT),
    )(*lands, sems[0], sems[1])
    return list(out)


def _to_owner_copies(pairs, q):
    x, y, c = _place()
    chip = 2 * x + y
    out = []
    for t, (part, land) in enumerate(pairs):
        for j, (fx, fy) in enumerate(OTHER_CHIPS):
            owner = 2 * (x ^ fx) + (y ^ fy)
            if part.shape[0] == 4 and part.shape[1:] == land.shape[1:]:
                src = part.at[owner]
            else:
                src = part.at[:, pl.ds(pl.multiple_of(owner * q, q), q), :]
            out.append((src, land.at[chip], land.at[owner], (x ^ fx, y ^ fy, c), 3 * t + j))
    return out


def _to_all_copies(bufs, first_sem):
    x, y, c = _place()
    me = 4 * x + 2 * y + c
    out = []
    for t, buf in enumerate(bufs):
        for j, (fx, fy, fc) in enumerate(OTHER_DEVICES):
            them = 4 * (x ^ fx) + 2 * (y ^ fy) + (c ^ fc)
            out.append((buf.at[me], buf.at[me], buf.at[them], (x ^ fx, y ^ fy, c ^ fc), first_sem + 7 * t + j))
    return out


def _to_chips_copies(bufs, first_sem):
    x, y, c = _place()
    chip = 2 * x + y
    out = []
    for t, buf in enumerate(bufs):
        for j, (fx, fy) in enumerate(OTHER_CHIPS):
            them = 2 * (x ^ fx) + (y ^ fy)
            out.append((buf.at[chip], buf.at[chip], buf.at[them], (x ^ fx, y ^ fy, c), first_sem + 3 * t + j))
    return out


def _exchange_copies(refs, kinds, q):
    n_owner, n_chips = kinds
    pairs = list(zip(refs[:n_owner], refs[n_owner:2 * n_owner]))
    first_all = 3 * (n_owner + n_chips)
    return (_to_owner_copies(pairs, q) + _to_chips_copies(refs[2 * n_owner:2 * n_owner + n_chips], 3 * n_owner)
            + _to_all_copies(refs[2 * n_owner + n_chips:], first_all))


def _exchange_start(arrays, kinds, q, name):
    n = len(arrays)
    n_sems = 3 * (kinds[0] + kinds[1]) + 7 * (n - 2 * kinds[0] - kinds[1])

    def body(*refs):
        send_sems, recv_sems = refs[n], refs[n + 1]
        for src, dst, _, peer, k in _exchange_copies(refs[:n], kinds, q):
            pltpu.make_async_remote_copy(src_ref=src, dst_ref=dst, send_sem=send_sems.at[k], recv_sem=recv_sems.at[k],
                                         device_id=peer, device_id_type=MESH).start()
        refs[-1][...] = jnp.zeros_like(refs[-1])

    out = pl.pallas_call(
        body, name=name,
        in_specs=[HBM] * n, out_specs=[SEM, SEM] + [HBM] * n + [VMEM],
        out_shape=[pltpu.SemaphoreType.DMA((n_sems,))] * 2 + [pltpu.HBM(a.shape, a.dtype) for a in arrays]
        + [jax.ShapeDtypeStruct((8, 128), F32)],
        input_output_aliases={i: 2 + i for i in range(n)},
        compiler_params=pltpu.CompilerParams(has_side_effects=DATAFLOW_EFFECT),
    )(*[_in_hbm(a) for a in arrays])
    return (out[0], out[1]), list(out[2:2 + n]), out[-1]


def _exchange_wait(arrays, sems, kinds, q, after, name):
    n = len(arrays)

    def body(*refs):
        send_sems, recv_sems = refs[n], refs[n + 1]
        for src, _, landed, peer, k in _exchange_copies(refs[:n], kinds, q):
            cp = pltpu.make_async_remote_copy(src_ref=src, dst_ref=landed, send_sem=send_sems.at[k], recv_sem=recv_sems.at[k],
                                              device_id=peer, device_id_type=MESH)
            cp.wait_send()
            cp.wait_recv()

    out = pl.pallas_call(
        body, name=name,
        in_specs=[HBM] * n + [SEM, SEM, ANY], out_specs=[HBM] * n,
        out_shape=[pltpu.HBM(a.shape, a.dtype) for a in arrays],
        input_output_aliases={i: i for i in range(n)},
        compiler_params=pltpu.CompilerParams(has_side_effects=DATAFLOW_EFFECT),
    )(*arrays, sems[0], sems[1], after)
    return list(out)


def _sibling_swap(parts, layer):
    n = len(parts)

    def body(*refs):
        srcs, outs = refs[:n], refs[n:2 * n]
        send_sems, recv_sems = refs[2 * n:]
        x, y, c = _place()
        cps = [pltpu.make_async_remote_copy(
            src_ref=srcs[i], dst_ref=outs[i], send_sem=send_sems.at[i], recv_sem=recv_sems.at[i],
            device_id=(x, y, 1 - c), device_id_type=MESH) for i in range(n)]
        for cp in cps:
            cp.start()
        for cp in cps:
            cp.wait()

    return pl.pallas_call(
        body, name=f"sibling_swap_l{layer}", in_specs=[ANY] * n, out_specs=[ANY] * n,
        out_shape=[jax.ShapeDtypeStruct(a.shape, a.dtype) for a in parts],
        scratch_shapes=[pltpu.SemaphoreType.DMA((n,)), pltpu.SemaphoreType.DMA((n,))],
        compiler_params=pltpu.CompilerParams(has_side_effects=True),
    )(*parts)


def kernel(x, c, ada_w, ada_b, pre_norm_g, w_in, conv_w, conv_b, gate_a_w, gate_a_b, gate_x_w, gate_x_b, lru_lambda, pool_w, pool_b, pool_scale, w_out, post_norm_g, loss_target, m_ada_w, m_ada_b, m_pre_norm_g, m_w_in, m_conv_w, m_conv_b, m_gate_a_w, m_gate_a_b, m_gate_x_w, m_gate_x_b, m_lru_lambda, m_pool_w, m_pool_b, m_pool_scale, m_w_out, m_post_norm_g, v_ada_w, v_ada_b, v_pre_norm_g, v_w_in, v_conv_w, v_conv_b, v_gate_a_w, v_gate_a_b, v_gate_x_w, v_gate_x_b, v_lru_lambda, v_pool_w, v_pool_b, v_pool_scale, v_w_out, v_post_norm_g):
    nl, d, n_ada = ada_w.shape
    s = x.shape[1]
    nh, hd = gate_a_w.shape[1], gate_a_w.shape[2]
    ng, gq, gd = pool_w.shape[1], pool_w.shape[2], pool_w.shape[3]
    me = 4 * lax.axis_index("x") + 2 * lax.axis_index("y") + lax.axis_index("c")
    chip = 2 * lax.axis_index("x") + lax.axis_index("y")
    chip_arr = jnp.reshape(chip, (1,)).astype(jnp.int32)
    x0 = x.reshape(s, d)
    target = loss_target.reshape(s, d)
    p_in = w_in.shape[2]
    r_out = w_out.shape[1]

    cbuf, modbuf, mod_token = _mod_exchange(c.reshape(1, d), ada_w)
    vecs, rvecs = _pack_vectors(modbuf, ada_b, pre_norm_g, post_norm_g, conv_b, gate_a_b, gate_x_b, lru_lambda)

    win = [_into_slot(w_in[l], BF16, chip_arr, f"slot_w_in_l{l}") for l in range(nl)]
    wout = [_into_slot(w_out[l], BF16, chip_arr, f"slot_w_out_l{l}") for l in range(nl)]
    pw = [_into_slot(pool_w[l].reshape(ng * gq, gd), BF16, chip_arr, f"slot_pool_w_l{l}") for l in range(nl)]
    convw = _into_slot(conv_w.reshape(nl * CONV_WIDTH, d // 4) + mod_token[0:1, 0:1], F32, chip_arr, "slot_conv_w")
    poolb = _into_slot(pool_b.reshape(nl * ng, gq), F32, chip_arr, "slot_pool_b")
    lands = [win[0], convw, poolb, *pw, wout[0]]
    split = [True, False, False] + [True] * (nl + 1)
    groups = [[0], list(range(1, len(lands)))]
    for l in range(1, nl):
        groups.append([len(lands), len(lands) + 1])
        lands += [win[l], wout[l]]
        split += [True, True]
    sems, lands, token = _gather_start(lands, groups, split)
    wa_b, wx_b = gate_a_w.astype(BF16), gate_x_w.astype(BF16)

    def gathered(gi, after, tag):
        idxs = groups[gi]
        arrays, halves = [lands[i] for i in idxs], [split[i] for i in idxs]
        between, arrays = _gather_forward(arrays, halves, sems[gi], after, f"weight_gather_forward_{tag}")
        return _gather_wait(arrays, halves, between, f"weight_gather_wait_{tag}")

    xs, projs, hss, ycats, ys = [x0], [], [], [], []
    sq = None
    convw_full = poolw_full = pvecs = None
    for l in range(nl):
        if l == 0:
            (win[0],) = gathered(0, modbuf, "a")
        proj = _inproj_fwd(xs[l], vecs[l], win[l], l)
        if l == 0:
            got = gathered(1, proj, "b")
            wout[0] = got[2 + nl]
            convw_full, pvecs, poolw_full = _pack_gathered(got[0], got[1], got[2:2 + nl], pool_scale, ng)
        ycat, hs = _rnn_fwd(proj, convw_full[l], rvecs[l], wa_b[l], wx_b[l], l)
        if l + 1 < nl:
            win[l + 1], wout[l + 1] = gathered(2 + l, hs, f"c{l + 1}")
        ycat = _pool_fwd(proj, ycat, poolw_full[l], pvecs[l], l)
        y, xo, sq = _outproj_fwd(ycat, wout[l], xs[l], vecs[l], target if l == nl - 1 else None, l)
        projs.append(proj), hss.append(hs), ycats.append(ycat), ys.append(y), xs.append(xo)

    c_all_t = cbuf.reshape(8, 8, d)[:, 0, :].T

    def finish(l, flights, after, prev):
        (sems_a, arr_a), (sems_g, arr_g), (sems_b, arr_b), (sems_c, arr_c) = flights
        dwout_l, rwout = _exchange_wait(arr_a, sems_a, (1, 0), gq, after, f"grad_wait_a_l{l}")
        dpw_l, rpw, gates = _exchange_wait(arr_g, sems_g, (1, 1), gq, rwout, f"grad_wait_g_l{l}")
        dwin_l, rwin = _exchange_wait(arr_b, sems_b, (1, 0), gq, gates, f"grad_wait_b_l{l}")
        (slabs,) = _exchange_wait(arr_c, sems_c, (0, 0), gq, rwin, f"grad_wait_c_l{l}")
        p_win = _sum_owner(dwin_l, rwin, chip_arr, lambda tr: (None, tr, p_in),
                           lambda i, chip: (chip[0], i, 0), f"sum_w_in_l{l}")
        p_wout = _sum_owner(dwout_l, rwout, chip_arr, lambda tr: (None, tr, d),
                            lambda i, chip: (chip[0], i, 0), f"sum_w_out_l{l}")
        p_pw = _sum_owner(dpw_l, rpw, chip_arr, lambda tr: (ng, tr, gd),
                          lambda i, chip: (0, chip[0], 0), f"sum_pool_w_l{l}")
        p_gates = _sum_slots(gates.reshape(4, 2 * nh * hd, hd), f"sum_gates_l{l}")
        q_win, q_wout, q_pw, q_gates = _sibling_swap([p_win, p_wout, p_pw, p_gates], l)
        per_dev = jnp.concatenate([slabs[:, r] for r in (ROW_SHIFT, ROW_SCALE, ROW_GATE)], axis=-1)
        dmod_mine = lax.dynamic_slice_in_dim(per_dev, chip * n_ada, n_ada, axis=1)
        prev = prev or {}
        big = {
            "w_in": _adamw_layer(w_in, m_w_in, v_w_in, [p_win, q_win], l, prev.get("w_in"), f"adamw_w_in_l{l}"),
            "w_out": _adamw_layer(w_out, m_w_out, v_w_out, [p_wout, q_wout], l, prev.get("w_out"), f"adamw_w_out_l{l}"),
            "pool_w": _adamw_layer(pool_w, m_pool_w, v_pool_w, [p_pw, q_pw], l, prev.get("pool_w"), f"adamw_pool_w_l{l}"),
            "gate_a_w": _adamw_layer(gate_a_w, m_gate_a_w, v_gate_a_w, [p_gates, q_gates], l, prev.get("gate_a_w"),
                                     f"adamw_gate_a_w_l{l}"),
            "gate_x_w": _adamw_layer(gate_x_w, m_gate_x_w, v_gate_x_w, [p_gates, q_gates], l, prev.get("gate_x_w"),
                                     f"adamw_gate_x_w_l{l}", grad_row_offset=nh * hd),
            "ada_w": _adamw_ada_w_layer(c_all_t, dmod_mine, ada_w, m_ada_w, v_ada_w, l, prev.get("ada_w"),
                                        f"adamw_ada_w_l{l}"),
        }
        return big, _sum_slots(slabs, f"sum_slab_l{l}")

    dx = xs[nl]
    flights = token = big = None
    totals = [None] * nl
    for l in reversed(range(nl)):
        vec_l = vecs[l] if token is None else vecs[l] + token[0:1, 0:1]
        dycat, dwout_l, dvec_o = _outproj_bwd(dx, ys[l], ycats[l], wout[l], vec_l, l)
        sems_a, arr_a, tok_a = _exchange_start([dwout_l, lax.empty(dwout_l.shape, BF16)], (1, 0), gq, f"grad_start_a_l{l}")
        dproj, dgates, dvec_r = _rnn_bwd(projs[l], hss[l], dycat, convw_full[l], rvecs[l] + tok_a[0:1, 0:1],
                                         wa_b[l], wx_b[l], l)
        dproj, dpw_l, dvec_p = _pool_bwd(projs[l], dycat, dproj, poolw_full[l], pvecs[l], l)
        gates4 = lax.dynamic_update_slice(lax.empty((4, *dgates.shape), BF16), dgates[None], (chip, 0, 0, 0, 0))
        sems_g, arr_g, tok_g = _exchange_start([dpw_l, lax.empty((4, ng, gq, gd), BF16), gates4], (1, 1), gq,
                                               f"grad_start_g_l{l}")
        dwin_l = _inproj_bwd_w(dproj, xs[l], vec_l + tok_g[0:1, 0:1], l)
        sems_b, arr_b, tok_b = _exchange_start([dwin_l, lax.empty(dwin_l.shape, BF16)], (1, 0), gq, f"grad_start_b_l{l}")
        dx, dvec_i = _inproj_bwd_x(dproj, win[l], xs[l], dx, vec_l + tok_b[0:1, 0:1], l)
        parts = [dvec_i, dvec_o, dvec_r, dvec_p]
        if l == nl - 1:
            parts.append(jnp.tile(sq, (1, d // sq.shape[1])))
        slab = jnp.concatenate(parts, axis=0)
        slabs = lax.dynamic_update_slice(lax.empty((8, *slab.shape), F32), slab[None], (me, 0, 0))
        sems_c, arr_c, token = _exchange_start([slabs], (0, 0), gq, f"grad_start_c_l{l}")
        if flights is not None:
            big, totals[l + 1] = finish(l + 1, flights, token, big)
        flights = ((sems_a, arr_a), (sems_g, arr_g), (sems_b, arr_b), (sems_c, arr_c))
    big, totals[0] = finish(0, flights, big["w_in"][3] if big else dx, big)
    grad_x = dx.reshape(x.shape)
    loss = totals[nl - 1][ROW_SQ, 0] * (0.5 / d)

    small = _adamw_small(totals, chip_arr, [
        (ada_b, m_ada_b, v_ada_b), (pre_norm_g, m_pre_norm_g, v_pre_norm_g), (post_norm_g, m_post_norm_g, v_post_norm_g),
        (conv_b, m_conv_b, v_conv_b), (lru_lambda, m_lru_lambda, v_lru_lambda), (pool_scale, m_pool_scale, v_pool_scale),
        (gate_a_b, m_gate_a_b, v_gate_a_b), (gate_x_b, m_gate_x_b, v_gate_x_b),
        (conv_w, m_conv_w, v_conv_w), (pool_b, m_pool_b, v_pool_b)])

    results = {
        "ada_w": tuple(o.reshape(ada_w.shape) for o in big["ada_w"]),
        "ada_b": small[0],
        "pre_norm_g": small[1],
        "w_in": tuple(o.reshape(w_in.shape) for o in big["w_in"]),
        "conv_w": small[8],
        "conv_b": small[3],
        "gate_a_w": tuple(o.reshape(gate_a_w.shape) for o in big["gate_a_w"]),
        "gate_a_b": small[6],
        "gate_x_w": tuple(o.reshape(gate_x_w.shape) for o in big["gate_x_w"]),
        "gate_x_b": small[7],
        "lru_lambda": small[4],
        "pool_w": tuple(o.reshape(pool_w.shape) for o in big["pool_w"]),
        "pool_b": small[9],
        "pool_scale": small[5],
        "w_out": tuple(o.reshape(w_out.shape) for o in big["w_out"]),
        "post_norm_g": small[2],
    }
    names = list(results)
    return (loss, grad_x,
            *[results[n][0] for n in names], *[results[n][1] for n in names],
            *[results[n][2] for n in names], *[results[n][3] for n in names])
```

```python
import functools

import jax
import jax.numpy as jnp
from jax import lax
from jax.experimental import pallas as pl
from jax.experimental.pallas import tpu as pltpu

F32 = jnp.float32
BF16 = jnp.bfloat16

NORM_EPS = 1e-6
LRU_C = 8.0
CONV_WIDTH = 4
MAX_POOL_WINDOW = 16
HALO = 16
ADAM_LR = 0.001
ADAM_B1 = 0.9
ADAM_B2 = 0.999
ADAM_EPS = 1e-08
ADAM_WD = 0.01
ADAM_STEP = 10

V7X_VMEM_LIMIT_BYTES = 56 * 1024 * 1024
MATMUL_ROWS = 512
SCAN_ROWS = 1024
ELEMENTWISE_ROWS = 512

MESH = pl.DeviceIdType.MESH
ANY = pl.BlockSpec(memory_space=pl.ANY)
VMEM = pl.BlockSpec(memory_space=pltpu.VMEM)
HBM = pl.BlockSpec(memory_space=pltpu.HBM)
SEM = pl.BlockSpec(memory_space=pltpu.SEMAPHORE)
DATAFLOW_EFFECT = pltpu.SideEffectType.DATAFLOW_SIDE_EFFECTING

NT_DIMS = (((1,), (1,)), ((), ()))
TN_DIMS = (((0,), (0,)), ((), ()))


def _params(n_grid_axes):
    return pltpu.CompilerParams(dimension_semantics=("arbitrary",) * n_grid_axes,
                                vmem_limit_bytes=V7X_VMEM_LIMIT_BYTES)


def _tile(total, want):
    t = min(want, max(total // 2, HALO))
    assert total % t == 0 and t % HALO == 0, (total, t)
    return t


def _row_tile(rows):
    for t in range(min(rows, ELEMENTWISE_ROWS) // 8 * 8, 0, -8):
        if rows % t == 0:
            return t
    return rows


def _sigmoid(z):
    return 1.0 / (1.0 + jnp.exp(-z))


def _softplus(z):
    return jnp.maximum(z, 0.0) + jnp.log(1.0 + jnp.exp(-jnp.abs(z)))


def _neg_expm1(z):
    return -jnp.tanh(0.5 * z) * (jnp.exp(z) + 1.0)


def _colsum(v):
    return jnp.sum(v, axis=0, keepdims=True)


def _prenorm(xt, vec_ref):
    rs = lax.rsqrt(jnp.mean(xt * xt, axis=-1, keepdims=True) + NORM_EPS)
    xn = xt * rs
    h = xn * vec_ref[3:4, :] * (1.0 + vec_ref[1:2, :]) + vec_ref[0:1, :]
    return h, xn, rs


def _shift_down(v, d, fill):
    t = v.shape[0]
    if d % 8 == 0:
        return jnp.concatenate([jnp.full((d, v.shape[1]), fill, v.dtype), v[:t - d]], axis=0)
    row = lax.broadcasted_iota(jnp.int32, v.shape, 0)
    return jnp.where(row >= d, pltpu.roll(v, d, 0), fill)


def _shift_up(v, d, fill):
    t = v.shape[0]
    if d % 8 == 0:
        return jnp.concatenate([v[d:], jnp.full((d, v.shape[1]), fill, v.dtype)], axis=0)
    row = lax.broadcasted_iota(jnp.int32, v.shape, 0)
    return jnp.where(row < t - d, pltpu.roll(v, t - d, 0), fill)


def _scan_fwd(a, v, h_before):
    d = 1
    while d < a.shape[0]:
        v = v + a * _shift_down(v, d, 0.0)
        a = a * _shift_down(a, d, 1.0)
        d *= 2
    return a * h_before + v


def _scan_rev(b, v):
    d = 1
    while d < b.shape[0]:
        v = v + b * _shift_up(v, d, 0.0)
        b = b * _shift_up(b, d, 0.0)
        d *= 2
    return v


def _inproj_fwd(x, vec, w_all, layer):
    s, d = x.shape
    p = w_all.shape[2]
    ts = _tile(s, MATMUL_ROWS)

    def body(x_ref, vec_ref, w_ref, proj_ref):
        h, _, _ = _prenorm(x_ref[...], vec_ref)
        hb = h.astype(BF16)
        for k in range(4):
            proj_ref[k] = jnp.dot(hb, w_ref[k], preferred_element_type=F32)

    return pl.pallas_call(
        body, name=f"inproj_fwd_l{layer}", grid=(s // ts,),
        in_specs=[pl.BlockSpec((ts, d), lambda i: (i, 0)),
                  pl.BlockSpec((8, d), lambda i: (0, 0)),
                  pl.BlockSpec((4, d, p), lambda i: (0, 0, 0))],
        out_specs=pl.BlockSpec((4, ts, p), lambda i: (0, i, 0)),
        out_shape=jax.ShapeDtypeStruct((4, s, p), F32),
        compiler_params=_params(1),
    )(x, vec, w_all)


HEADS_PER_STEP = 2
BWD_HEADS_PER_STEP = 1


def _rnn_gates(u, wa, wx, vec_ref, lanes):
    ub = u.astype(BF16)
    r = _sigmoid(jnp.dot(ub, wa, preferred_element_type=F32) + vec_ref[1:2, lanes])
    ig = _sigmoid(jnp.dot(ub, wx, preferred_element_type=F32) + vec_ref[2:3, lanes])
    sp = _softplus(-vec_ref[3:4, lanes])
    log_a = (-LRU_C) * r * sp
    return ub, r, ig, sp, log_a


def _conv(xbuf, cw_ref, vec_ref, lanes, ts):
    u = vec_ref[0:1, lanes] + cw_ref[CONV_WIDTH - 1:CONV_WIDTH, lanes] * xbuf[pl.ds(HALO, ts), lanes]
    for k in range(CONV_WIDTH - 1):
        u = u + cw_ref[k:k + 1, lanes] * xbuf[pl.ds(HALO - (CONV_WIDTH - 1) + k, ts), lanes]
    return u


def _rnn_fwd(proj, cw, vec, wa, wx, layer):
    _, s, d = proj.shape
    nh, hd, _ = wa.shape
    ts = _tile(s, SCAN_ROWS)
    hps = HEADS_PER_STEP
    wl = hps * hd

    def body(proj_ref, cw_ref, vec_ref, wa_ref, wx_ref, ycat_ref, hs_ref, xbuf, hlast):
        i = pl.program_id(1)

        @pl.when(i == 0)
        def _():
            xbuf[0:HALO, :] = jnp.zeros((HALO, wl), F32)
            hlast[...] = jnp.zeros_like(hlast)

        xbuf[pl.ds(HALO, ts), :] = proj_ref[0]
        for hh in range(hps):
            lanes = slice(hh * hd, (hh + 1) * hd)
            u = _conv(xbuf, cw_ref, vec_ref, lanes, ts)
            _, _, ig, _, log_a = _rnn_gates(u, wa_ref[hh], wx_ref[hh], vec_ref, lanes)
            a = jnp.exp(log_a)
            mult = jnp.sqrt(_neg_expm1(2.0 * log_a))
            hs = _scan_fwd(a, mult * (ig * u), hlast[0:1, lanes])
            hs_ref[:, lanes] = hs
            hlast[0:1, lanes] = hs_ref[ts - 1:ts, lanes]
            g = proj_ref[1, :, lanes]
            ycat_ref[:, lanes] = (hs * (g * _sigmoid(g))).astype(BF16)
        xbuf[0:HALO, :] = xbuf[pl.ds(ts, HALO), :]

    return pl.pallas_call(
        body, name=f"rnn_fwd_l{layer}", grid=(nh // hps, s // ts),
        in_specs=[pl.BlockSpec((2, ts, wl), lambda h, i: (0, i, h)),
                  pl.BlockSpec((CONV_WIDTH, wl), lambda h, i: (0, h)),
                  pl.BlockSpec((8, wl), lambda h, i: (0, h)),
                  pl.BlockSpec((hps, hd, hd), lambda h, i: (h, 0, 0)),
                  pl.BlockSpec((hps, hd, hd), lambda h, i: (h, 0, 0))],
        out_specs=[pl.BlockSpec((ts, wl), lambda h, i: (i, h)),
                   pl.BlockSpec((ts, wl), lambda h, i: (i, h))],
        out_shape=[jax.ShapeDtypeStruct((s, 2 * d), BF16), jax.ShapeDtypeStruct((s, d), F32)],
        scratch_shapes=[pltpu.VMEM((ts + HALO, wl), F32), pltpu.VMEM((8, wl), F32)],
        compiler_params=_params(2),
    )(proj, cw, vec, wa, wx)


def _inv_count(i, ts, lanes, win):
    t = i * ts + lax.broadcasted_iota(jnp.int32, (ts, lanes), 0)
    return 1.0 / jnp.minimum(t + 1, win).astype(F32)


def _window_sum(ext, win, forward):
    rows = ext.shape[0]
    s, d = ext, 1
    while d < win:
        s = s + pltpu.roll(s, d if forward else rows - d, 0)
        d *= 2
    return s


def _pooled(xbuf, xt, lanes, win, inv_cnt, ts):
    acc = _window_sum(xbuf[:, lanes], win, True)[HALO:, :]
    return acc * inv_cnt - xt


def _pool_fwd(proj, ycat, pw, vec, layer):
    _, s, d = proj.shape
    ng, gd, _ = pw.shape
    ts = _tile(s, MATMUL_ROWS)

    def body(proj_ref, ycat_in, pw_ref, vec_ref, ycat_ref, xbuf):
        del ycat_in
        i = pl.program_id(0)

        @pl.when(i == 0)
        def _():
            xbuf[0:HALO, :] = jnp.zeros((HALO, d), F32)

        xbuf[pl.ds(HALO, ts), :] = proj_ref[0]
        for g in range(ng):
            lanes = slice(g * gd, (g + 1) * gd)
            win = 2 << g
            xt = proj_ref[0, :, lanes]
            pooled = _pooled(xbuf, xt, lanes, win, _inv_count(i, ts, gd, win), ts).astype(BF16)
            z = jnp.dot(pooled, pw_ref[g], preferred_element_type=F32) + vec_ref[0:1, lanes]
            gg = proj_ref[1, :, lanes]
            ycat_ref[:, lanes] = (z * vec_ref[1:2, lanes] * (gg * _sigmoid(gg))).astype(BF16)
        xbuf[0:HALO, :] = xbuf[pl.ds(ts, HALO), :]

    return pl.pallas_call(
        body, name=f"pool_fwd_l{layer}", grid=(s // ts,),
        in_specs=[pl.BlockSpec((2, ts, d), lambda i: (1, i, 0)),
                  ANY,
                  pl.BlockSpec((ng, gd, gd), lambda i: (0, 0, 0)),
                  pl.BlockSpec((8, d), lambda i: (0, 0))],
        out_specs=pl.BlockSpec((ts, d), lambda i: (i, 1)),
        out_shape=jax.ShapeDtypeStruct((s, 2 * d), BF16),
        input_output_aliases={1: 0},
        scratch_shapes=[pltpu.VMEM((ts + HALO, d), F32)],
        compiler_params=_params(1),
    )(proj, ycat, pw, vec)


def _outproj_fwd(ycat, w_all, x, vec, target, layer):
    s, d = x.shape
    nk, kd = w_all.shape[0], w_all.shape[1]
    ts = _tile(s, MATMUL_ROWS)
    last = target is not None

    def body(*refs):
        if last:
            ycat_ref, w_ref, x_ref, vec_ref, tgt_ref, y_ref, xo_ref, sq_ref = refs
        else:
            ycat_ref, w_ref, x_ref, vec_ref, y_ref, xo_ref = refs
        y = jnp.dot(ycat_ref[:, 0:kd], w_ref[0], preferred_element_type=F32)
        for k in range(1, nk):
            y = y + jnp.dot(ycat_ref[:, k * kd:(k + 1) * kd], w_ref[k], preferred_element_type=F32)
        y_ref[...] = y
        rs = lax.rsqrt(jnp.mean(y * y, axis=-1, keepdims=True) + NORM_EPS)
        xo = x_ref[...] + vec_ref[2:3, :] * (y * rs * vec_ref[4:5, :])
        if last:
            err = xo - tgt_ref[...]
            xo_ref[...] = err * (1.0 / d)

            @pl.when(pl.program_id(0) == 0)
            def _():
                sq_ref[...] = jnp.zeros_like(sq_ref)

            sq_ref[...] += jnp.sum(err * err)
        else:
            xo_ref[...] = xo

    row = pl.BlockSpec((ts, d), lambda i: (i, 0))
    in_specs = [pl.BlockSpec((ts, nk * kd), lambda i: (i, 0)),
                pl.BlockSpec((nk, kd, d), lambda i: (0, 0, 0)),
                row, pl.BlockSpec((8, d), lambda i: (0, 0))]
    out_specs = [row, row]
    out_shape = [jax.ShapeDtypeStruct((s, d), F32), jax.ShapeDtypeStruct((s, d), F32)]
    args = [ycat, w_all, x, vec]
    if last:
        in_specs.append(row)
        args.append(target)
        out_specs.append(pl.BlockSpec((8, 128), lambda i: (0, 0)))
        out_shape.append(jax.ShapeDtypeStruct((8, 128), F32))
    out = pl.pallas_call(
        body, name=f"outproj_fwd_l{layer}", grid=(s // ts,),
        in_specs=in_specs, out_specs=out_specs, out_shape=out_shape,
        compiler_params=_params(1),
    )(*args)
    return (out[0], out[1], out[2]) if last else (out[0], out[1], None)


def _outproj_bwd(dxo, y, ycat, w_all, vec, layer, after):
    s, d = dxo.shape
    nk, kd = w_all.shape[0], w_all.shape[1]
    ts = _tile(s, MATMUL_ROWS)
    nt = s // ts

    def body(dxo_ref, y_ref, ycat_ref, w_ref, vec_ref, after_ref, dycat_ref, dw_ref, dvec_ref, acc):
        del after_ref
        i = pl.program_id(0)

        @pl.when(i == 0)
        def _():
            acc[...] = jnp.zeros_like(acc)
            dvec_ref[...] = jnp.zeros_like(dvec_ref)

        yt = y_ref[...]
        rs = lax.rsqrt(jnp.mean(yt * yt, axis=-1, keepdims=True) + NORM_EPS)
        yhat = yt * rs
        gate, gpost = vec_ref[2:3, :], vec_ref[4:5, :]
        dxo_t = dxo_ref[...]
        dyn = dxo_t * gate
        dvec_ref[0:1, :] += _colsum(dxo_t * (yhat * gpost))
        dvec_ref[1:2, :] += _colsum(dyn * yhat)
        t = dyn * gpost
        dy = (rs * (t - yhat * jnp.mean(t * yhat, axis=-1, keepdims=True))).astype(BF16)
        for k in range(nk):
            cols = slice(k * kd, (k + 1) * kd)
            dycat_ref[:, cols] = lax.dot_general(dy, w_ref[k], NT_DIMS, preferred_element_type=F32)
            acc[k] += lax.dot_general(ycat_ref[:, cols], dy, TN_DIMS, preferred_element_type=F32)

        @pl.when(i == nt - 1)
        def _():
            dw_ref[...] = acc[...].astype(BF16)

    row = pl.BlockSpec((ts, d), lambda i: (i, 0))
    wide = pl.BlockSpec((ts, nk * kd), lambda i: (i, 0))
    return pl.pallas_call(
        body, name=f"outproj_bwd_l{layer}", grid=(nt,),
        in_specs=[row, row, wide,
                  pl.BlockSpec((nk, kd, d), lambda i: (0, 0, 0)),
                  pl.BlockSpec((8, d), lambda i: (0, 0)), ANY],
        out_specs=[wide,
                   pl.BlockSpec((nk, kd, d), lambda i: (0, 0, 0)),
                   pl.BlockSpec((8, d), lambda i: (0, 0))],
        out_shape=[jax.ShapeDtypeStruct((s, nk * kd), F32),
                   jax.ShapeDtypeStruct((nk, kd, d), BF16),
                   jax.ShapeDtypeStruct((8, d), F32)],
        scratch_shapes=[pltpu.VMEM((nk, kd, d), F32)],
        compiler_params=_params(1),
    )(dxo, y, ycat, w_all, vec, after)


def _halo_index(ts, nt):
    return lambda j: jnp.maximum((nt - 1 - j) * (ts // HALO) - 1, 0)


def _rnn_bwd(proj, hs, dycat, cw, vec, wa, wx, layer, after):
    _, s, d = proj.shape
    nh, hd, _ = wa.shape
    ts = _tile(s, SCAN_ROWS)
    nt = s // ts
    halo = _halo_index(ts, nt)
    hps = BWD_HEADS_PER_STEP
    wl = hps * hd

    def body(proj_ref, xh_ref, hs_ref, hsh_ref, dy_ref, cw_ref, vec_ref, wa_ref, wx_ref, after_ref,
             dproj_ref, dgates_ref, dvec_ref, xbuf, hbuf, dubuf, carry, dw_acc):
        del after_ref
        j = pl.program_id(1)
        first_tile = j == nt - 1

        @pl.when(j == 0)
        def _():
            dubuf[pl.ds(ts, HALO), :] = jnp.zeros((HALO, wl), F32)
            carry[...] = jnp.zeros_like(carry)
            dw_acc[...] = jnp.zeros_like(dw_acc)
            dvec_ref[...] = jnp.zeros_like(dvec_ref)

        xbuf[0:HALO, :] = jnp.where(first_tile, 0.0, xh_ref[0])
        xbuf[pl.ds(HALO, ts), :] = proj_ref[0]
        hbuf[0:HALO, :] = jnp.where(first_tile, 0.0, hsh_ref[...])
        hbuf[pl.ds(HALO, ts), :] = hs_ref[...]

        for hh in range(hps):
            lanes = slice(hh * hd, (hh + 1) * hd)
            wa, wx = wa_ref[hh], wx_ref[hh]
            hs = hs_ref[:, lanes]
            u = _conv(xbuf, cw_ref, vec_ref, lanes, ts)
            ub, r, ig, sp, log_a = _rnn_gates(u, wa, wx, vec_ref, lanes)
            a = jnp.exp(log_a)
            e2 = jnp.exp(2.0 * log_a)
            one_minus_a2 = _neg_expm1(2.0 * log_a)
            inv_mult = lax.rsqrt(one_minus_a2)
            mult = one_minus_a2 * inv_mult

            g = proj_ref[1, :, lanes]
            sg = _sigmoid(g)
            dyc = dy_ref[:, lanes]
            dproj_ref[1, :, lanes] = (dyc * hs * (sg * (1.0 + g * (1.0 - sg)))).astype(BF16)

            row = lax.broadcasted_iota(jnp.int32, (ts, hd), 0)
            dhs = dyc * (g * sg) + jnp.where(row == ts - 1, carry[0:1, lanes], 0.0)
            dh = _scan_rev(_shift_up(a, 1, 0.0), dhs)
            carry[:, lanes] = (a * dh)[0:8, :]

            h_prev = hbuf[pl.ds(HALO - 1, ts), lanes]
            dlog_a = dh * h_prev * a - dh * (ig * u) * (e2 * inv_mult)
            di = dh * mult * u
            dzr = dlog_a * ((-LRU_C) * sp) * (r * (1.0 - r))
            dzi = di * (ig * (1.0 - ig))
            dvec_ref[3:4, lanes] += _colsum(dlog_a * r) * (LRU_C * _sigmoid(-vec_ref[3:4, lanes]))
            dvec_ref[1:2, lanes] += _colsum(dzr)
            dvec_ref[2:3, lanes] += _colsum(dzi)
            dzr_b, dzi_b = dzr.astype(BF16), dzi.astype(BF16)
            dw_acc[0, hh] += lax.dot_general(ub, dzr_b, TN_DIMS, preferred_element_type=F32)
            dw_acc[1, hh] += lax.dot_general(ub, dzi_b, TN_DIMS, preferred_element_type=F32)
            du = (dh * mult * ig
                  + lax.dot_general(dzr_b, wa, NT_DIMS, preferred_element_type=F32)
                  + lax.dot_general(dzi_b, wx, NT_DIMS, preferred_element_type=F32))
            dvec_ref[0:1, lanes] += _colsum(du)
            for k in range(CONV_WIDTH):
                dvec_ref[4 + k:5 + k, lanes] += _colsum(du * xbuf[pl.ds(HALO - (CONV_WIDTH - 1) + k, ts), lanes])

            dubuf[0:ts, lanes] = du
            dx = cw_ref[CONV_WIDTH - 1:CONV_WIDTH, lanes] * du
            for k in range(CONV_WIDTH - 1):
                dx = dx + cw_ref[k:k + 1, lanes] * dubuf[pl.ds(CONV_WIDTH - 1 - k, ts), lanes]
            dproj_ref[0, :, lanes] = dx.astype(BF16)
        dubuf[pl.ds(ts, HALO), :] = dubuf[0:HALO, :]

        @pl.when(first_tile)
        def _():
            dgates_ref[...] = dw_acc[...].astype(BF16)

    rev = lambda h, j: (nt - 1 - j, h)
    return pl.pallas_call(
        body, name=f"rnn_bwd_l{layer}", grid=(nh // hps, nt),
        in_specs=[pl.BlockSpec((2, ts, wl), lambda h, j: (0, nt - 1 - j, h)),
                  pl.BlockSpec((1, HALO, wl), lambda h, j: (0, halo(j), h)),
                  pl.BlockSpec((ts, wl), rev),
                  pl.BlockSpec((HALO, wl), lambda h, j: (halo(j), h)),
                  pl.BlockSpec((ts, wl), rev),
                  pl.BlockSpec((CONV_WIDTH, wl), lambda h, j: (0, h)),
                  pl.BlockSpec((8, wl), lambda h, j: (0, h)),
                  pl.BlockSpec((hps, hd, hd), lambda h, j: (h, 0, 0)),
                  pl.BlockSpec((hps, hd, hd), lambda h, j: (h, 0, 0)), ANY],
        out_specs=[pl.BlockSpec((2, ts, wl), lambda h, j: (0, nt - 1 - j, h)),
                   pl.BlockSpec((2, hps, hd, hd), lambda h, j: (0, h, 0, 0)),
                   pl.BlockSpec((16, wl), lambda h, j: (0, h))],
        out_shape=[jax.ShapeDtypeStruct((4, s, d), BF16),
                   jax.ShapeDtypeStruct((2, nh, hd, hd), BF16),
                   jax.ShapeDtypeStruct((16, d), F32)],
        scratch_shapes=[pltpu.VMEM((ts + HALO, wl), F32), pltpu.VMEM((ts + HALO, wl), F32),
                        pltpu.VMEM((ts + HALO, wl), F32), pltpu.VMEM((8, wl), F32),
                        pltpu.VMEM((2, hps, hd, hd), F32)],
        compiler_params=_params(2),
    )(proj, proj, hs, hs, dycat, cw, vec, wa, wx, after)


def _pool_bwd(proj, dycat, dproj, pw, vec, layer):
    _, s, d = proj.shape
    ng, gd, _ = pw.shape
    ts = _tile(s, MATMUL_ROWS)
    nt = s // ts
    halo = _halo_index(ts, nt)

    def body(proj_ref, xh_ref, dy_ref, dproj_in, pw_ref, vec_ref, dproj_ref, dpw_ref, dvec_ref, xbuf, qbuf, acc):
        del dproj_in
        j = pl.program_id(0)
        i = nt - 1 - j

        @pl.when(j == 0)
        def _():
            qbuf[pl.ds(ts, HALO), :] = jnp.zeros((HALO, d), F32)
            acc[...] = jnp.zeros_like(acc)
            dvec_ref[...] = jnp.zeros_like(dvec_ref)

        xbuf[0:HALO, :] = jnp.where(i == 0, 0.0, xh_ref[0])
        xbuf[pl.ds(HALO, ts), :] = proj_ref[0]
        for g in range(ng):
            lanes = slice(g * gd, (g + 1) * gd)
            win = 2 << g
            xt = proj_ref[0, :, lanes]
            inv_cnt = _inv_count(i, ts, gd, win)
            pooled = _pooled(xbuf, xt, lanes, win, inv_cnt, ts).astype(BF16)
            z = jnp.dot(pooled, pw_ref[g], preferred_element_type=F32) + vec_ref[0:1, lanes]
            scale = vec_ref[1:2, lanes]
            gg = proj_ref[1, :, lanes]
            sg = _sigmoid(gg)
            dyc = dy_ref[:, lanes]
            dyp = dyc * (gg * sg)
            dproj_ref[1, :, lanes] = (dyc * (z * scale) * (sg * (1.0 + gg * (1.0 - sg)))).astype(BF16)
            dvec_ref[1:2, lanes] += _colsum(dyp * z)
            dz = dyp * scale
            dvec_ref[0:1, lanes] += _colsum(dz)
            dz_b = dz.astype(BF16)
            acc[g] += lax.dot_general(pooled, dz_b, TN_DIMS, preferred_element_type=F32)
            dpooled = lax.dot_general(dz_b, pw_ref[g], NT_DIMS, preferred_element_type=F32)

            qbuf[0:ts, lanes] = dpooled * inv_cnt
            dx = _window_sum(qbuf[:, lanes], win, False)[0:ts, :] - dpooled
            dproj_ref[0, :, lanes] = dx.astype(BF16)
        qbuf[pl.ds(ts, HALO), :] = qbuf[0:HALO, :]

        @pl.when(j == nt - 1)
        def _():
            dpw_ref[...] = acc[...].astype(BF16)

    return pl.pallas_call(
        body, name=f"pool_bwd_l{layer}", grid=(nt,),
        in_specs=[pl.BlockSpec((2, ts, d), lambda j: (1, nt - 1 - j, 0)),
                  pl.BlockSpec((1, HALO, d), lambda j: (2, halo(j), 0)),
                  pl.BlockSpec((ts, d), lambda j: (nt - 1 - j, 1)),
                  ANY,
                  pl.BlockSpec((ng, gd, gd), lambda j: (0, 0, 0)),
                  pl.BlockSpec((8, d), lambda j: (0, 0))],
        out_specs=[pl.BlockSpec((2, ts, d), lambda j: (1, nt - 1 - j, 0)),
                   pl.BlockSpec((ng, gd, gd), lambda j: (0, 0, 0)),
                   pl.BlockSpec((8, d), lambda j: (0, 0))],
        out_shape=[jax.ShapeDtypeStruct((4, s, d), BF16),
                   jax.ShapeDtypeStruct((ng, gd, gd), BF16),
                   jax.ShapeDtypeStruct((8, d), F32)],
        input_output_aliases={3: 0},
        scratch_shapes=[pltpu.VMEM((ts + HALO, d), F32), pltpu.VMEM((ts + HALO, d), F32),
                        pltpu.VMEM((ng, gd, gd), F32)],
        compiler_params=_params(1),
    )(proj, proj, dycat, dproj, pw, vec)


def _inproj_bwd_x(dproj, w_all, x, dxo, vec, layer, after):
    s, d = x.shape
    p = w_all.shape[2]
    ts = _tile(s, MATMUL_ROWS)

    def body(dp_ref, w_ref, x_ref, dxo_ref, vec_ref, after_ref, dx_ref, dvec_ref):
        del after_ref

        @pl.when(pl.program_id(0) == 0)
        def _():
            dvec_ref[...] = jnp.zeros_like(dvec_ref)

        dh = lax.dot_general(dp_ref[0], w_ref[0], NT_DIMS, preferred_element_type=F32)
        for k in range(1, 4):
            dh = dh + lax.dot_general(dp_ref[k], w_ref[k], NT_DIMS, preferred_element_type=F32)
        _, xn, rs = _prenorm(x_ref[...], vec_ref)
        gpre, scale1 = vec_ref[3:4, :], 1.0 + vec_ref[1:2, :]
        dvec_ref[0:1, :] += _colsum(dh)
        dvec_ref[1:2, :] += _colsum(dh * (xn * gpre))
        dvec_ref[2:3, :] += _colsum(dh * (xn * scale1))
        t = dh * (gpre * scale1)
        dx_ref[...] = dxo_ref[...] + rs * (t - xn * jnp.mean(t * xn, axis=-1, keepdims=True))

    row = pl.BlockSpec((ts, d), lambda i: (i, 0))
    return pl.pallas_call(
        body, name=f"inproj_bwd_x_l{layer}", grid=(s // ts,),
        in_specs=[pl.BlockSpec((4, ts, p), lambda i: (0, i, 0)),
                  pl.BlockSpec((4, d, p), lambda i: (0, 0, 0)),
                  row, row, pl.BlockSpec((8, d), lambda i: (0, 0)), ANY],
        out_specs=[row, pl.BlockSpec((8, d), lambda i: (0, 0))],
        out_shape=[jax.ShapeDtypeStruct((s, d), F32), jax.ShapeDtypeStruct((8, d), F32)],
        compiler_params=_params(1),
    )(dproj, w_all, x, dxo, vec, after)


def _inproj_bwd_w(dproj, x, vec, layer, after):
    s, d = x.shape
    p = dproj.shape[2]
    ts = _tile(s, MATMUL_ROWS)
    nt = s // ts

    def body(dp_ref, x_ref, vec_ref, after_ref, dw_ref, acc):
        del after_ref
        i = pl.program_id(0)

        @pl.when(i == 0)
        def _():
            acc[...] = jnp.zeros_like(acc)

        h, _, _ = _prenorm(x_ref[...], vec_ref)
        hb = h.astype(BF16)
        for k in range(4):
            acc[k] += lax.dot_general(hb, dp_ref[k], TN_DIMS, preferred_element_type=F32)

        @pl.when(i == nt - 1)
        def _():
            dw_ref[...] = acc[...].astype(BF16)

    return pl.pallas_call(
        body, name=f"inproj_bwd_w_l{layer}", grid=(nt,),
        in_specs=[pl.BlockSpec((4, ts, p), lambda i: (0, i, 0)),
                  pl.BlockSpec((ts, d), lambda i: (i, 0)),
                  pl.BlockSpec((8, d), lambda i: (0, 0)), ANY],
        out_specs=pl.BlockSpec((4, d, p), lambda i: (0, 0, 0)),
        out_shape=jax.ShapeDtypeStruct((4, d, p), BF16),
        scratch_shapes=[pltpu.VMEM((4, d, p), F32)],
        compiler_params=_params(1),
    )(dproj, x, vec, after)


def _sum_slots(stacked, name):
    n, rows, cols = stacked.shape
    tr = _row_tile(rows)

    def body(in_ref, out_ref):
        total = in_ref[0].astype(F32)
        for b in range(1, n):
            total = total + in_ref[b].astype(F32)
        out_ref[...] = total

    return pl.pallas_call(
        body, name=name, grid=(rows // tr,),
        in_specs=[pl.BlockSpec((n, tr, cols), lambda i: (0, i, 0))],
        out_specs=pl.BlockSpec((tr, cols), lambda i: (i, 0)),
        out_shape=jax.ShapeDtypeStruct((rows, cols), F32),
        compiler_params=_params(1),
    )(stacked)


def _adamw(w, m, v, grads, name):
    shape = w.shape
    cols = shape[-1]
    rows = w.size // cols
    tr = _row_tile(rows)
    n = len(grads)

    def body(*refs):
        w_ref, m_ref, v_ref = refs[:3]
        g_refs = refs[3:3 + n]
        g_out, d_out, m_out, v_out = refs[3 + n:]
        g = g_refs[0][...]
        for r in g_refs[1:]:
            g = g + r[...]
        m_new = ADAM_B1 * m_ref[...] + (1.0 - ADAM_B1) * g
        v_new = ADAM_B2 * v_ref[...] + (1.0 - ADAM_B2) * (g * g)
        m_hat = m_new / (1.0 - ADAM_B1 ** ADAM_STEP)
        v_hat = v_new / (1.0 - ADAM_B2 ** ADAM_STEP)
        g_out[...] = g
        d_out[...] = (-ADAM_LR) * (m_hat / (jnp.sqrt(v_hat) + ADAM_EPS) + ADAM_WD * w_ref[...])
        m_out[...] = m_new
        v_out[...] = v_new

    blk = pl.BlockSpec((tr, cols), lambda i: (i, 0))
    outs = pl.pallas_call(
        body, name=name, grid=(rows // tr,),
        in_specs=[blk] * (3 + n), out_specs=[blk] * 4,
        out_shape=[jax.ShapeDtypeStruct((rows, cols), F32)] * 4,
        compiler_params=_params(1),
    )(*[a.reshape(rows, cols) for a in (w, m, v, *grads)])
    return tuple(o.reshape(shape) for o in outs)


def _adam_update(w, m, v, g):
    m_new = ADAM_B1 * m + (1.0 - ADAM_B1) * g
    v_new = ADAM_B2 * v + (1.0 - ADAM_B2) * (g * g)
    m_hat = m_new / (1.0 - ADAM_B1 ** ADAM_STEP)
    v_hat = v_new / (1.0 - ADAM_B2 ** ADAM_STEP)
    return (-ADAM_LR) * (m_hat / (jnp.sqrt(v_hat) + ADAM_EPS) + ADAM_WD * w), m_new, v_new


def _adamw_layer(w, m, v, grads, layer, prev, name, grad_row_offset=0):
    nl = w.shape[0]
    cols = w.shape[-1]
    rows = w.size // (nl * cols)
    tr = _row_tile(rows)
    off = layer * (rows // tr)
    g_off = grad_row_offset // tr
    n = len(grads)
    n_prev = 0 if prev is None else 4

    def body(*refs):
        w_ref, m_ref, v_ref = refs[:3]
        g_refs = refs[3:3 + n]
        g_out, d_out, m_out, v_out = refs[3 + n + n_prev:]
        g = g_refs[0][...]
        for r in g_refs[1:]:
            g = g + r[...]
        g_out[...] = g
        d_out[...], m_out[...], v_out[...] = _adam_update(w_ref[...], m_ref[...], v_ref[...], g)

    mine = pl.BlockSpec((tr, cols), lambda i: (off + i, 0))
    args = [a.reshape(nl * rows, cols) for a in (w, m, v)] + [g.reshape(-1, cols) for g in grads]
    outs = pl.pallas_call(
        body, name=name, grid=(rows // tr,),
        in_specs=[mine] * 3 + [pl.BlockSpec((tr, cols), lambda i: (g_off + i, 0))] * n + [ANY] * n_prev,
        out_specs=[mine] * 4,
        out_shape=[jax.ShapeDtypeStruct((nl * rows, cols), F32)] * 4,
        input_output_aliases={3 + n + k: k for k in range(n_prev)},
        compiler_params=_params(1),
    )(*args, *(prev or ()))
    return tuple(outs)


def _into_slot(a, dtype, chip_arr, name, layer=None, after=None):
    rows, cols = a.shape[-2:]
    tr = _row_tile(rows)

    def body(chip_ref, a_ref, *rest):
        del chip_ref
        rest[-1][...] = a_ref[...].astype(dtype)

    if layer is None:
        in_spec = pl.BlockSpec((tr, cols), lambda i, chip: (i, 0))
    else:
        in_spec = pl.BlockSpec((None, tr, cols), lambda i, chip: (layer, i, 0))
    extra = [] if after is None else [after]
    return pl.pallas_call(
        body, name=name,
        grid_spec=pltpu.PrefetchScalarGridSpec(
            num_scalar_prefetch=1, grid=(rows // tr,),
            in_specs=[in_spec] + [ANY] * len(extra),
            out_specs=pl.BlockSpec((None, tr, cols), lambda i, chip: (chip[0], i, 0))),
        out_shape=jax.ShapeDtypeStruct((4, rows, cols), dtype),
        compiler_params=_params(1),
    )(chip_arr, a, *extra)


def _sum_owner(own, land, chip_arr, own_block, own_index, name):
    blk = land.shape[1:]
    tr = _row_tile(blk[-2])
    steps = blk[-2] // tr
    tile = (*blk[:-2], tr, blk[-1])
    lead = (0,) * (len(blk) - 2)

    def body(chip_ref, own_ref, l1, l2, l3, out_ref):
        del chip_ref
        out_ref[...] = (own_ref[...].astype(F32) + l1[...].astype(F32)) + (l2[...].astype(F32) + l3[...].astype(F32))

    def landed(k):
        return pl.BlockSpec((None, *tile), lambda i, chip: (chip[0] ^ k, *lead, i, 0))

    return pl.pallas_call(
        body, name=name,
        grid_spec=pltpu.PrefetchScalarGridSpec(
            num_scalar_prefetch=1, grid=(steps,),
            in_specs=[pl.BlockSpec(own_block(tr), own_index), landed(1), landed(2), landed(3)],
            out_specs=pl.BlockSpec(tile, lambda i, chip: (*lead, i, 0))),
        out_shape=jax.ShapeDtypeStruct(blk, F32),
        compiler_params=_params(1),
    )(chip_arr, own, land, land, land)


_WHOLE_VMEM = pltpu.CompilerParams(vmem_limit_bytes=V7X_VMEM_LIMIT_BYTES)


def _pack_vectors(modbuf, ada_b, pre_norm_g, post_norm_g, conv_b, gate_a_b, gate_x_b, lru_lambda):
    nl, d = pre_norm_g.shape
    n = modbuf.shape[2] // nl
    nh, hd = gate_a_b.shape[1], gate_a_b.shape[2]

    def body(mb_ref, ab_ref, pre_ref, post_ref, cb_ref, gab_ref, gxb_ref, lam_ref, *outs):
        for layer in range(nl):
            vec_ref, rvec_ref = outs[layer], outs[nl + layer]
            vec_ref[...] = jnp.zeros_like(vec_ref)
            rvec_ref[...] = jnp.zeros_like(rvec_ref)
            for k in range(4):
                piece = mb_ref[k, 0:1, layer * n:(layer + 1) * n] + ab_ref[layer:layer + 1, k * n:(k + 1) * n]
                lo = k * n
                while lo < (k + 1) * n:
                    row = lo // d
                    hi = min((row + 1) * d, (k + 1) * n)
                    vec_ref[row:row + 1, lo - row * d:hi - row * d] = piece[:, lo - k * n:hi - k * n]
                    lo = hi
            vec_ref[3:4, :] = pre_ref[layer:layer + 1, :]
            vec_ref[4:5, :] = post_ref[layer:layer + 1, :]
            rvec_ref[0:1, :] = cb_ref[layer:layer + 1, :]
            for h in range(nh):
                rvec_ref[1:2, h * hd:(h + 1) * hd] = gab_ref[layer, h:h + 1, :]
                rvec_ref[2:3, h * hd:(h + 1) * hd] = gxb_ref[layer, h:h + 1, :]
            rvec_ref[3:4, :] = lam_ref[layer:layer + 1, :]

    out = pl.pallas_call(
        body, name="pack_vectors", in_specs=[VMEM] * 8, out_specs=[VMEM] * (2 * nl),
        out_shape=[jax.ShapeDtypeStruct((8, d), F32)] * (2 * nl), compiler_params=_WHOLE_VMEM,
    )(modbuf, ada_b, pre_norm_g, post_norm_g, conv_b, gate_a_b, gate_x_b, lru_lambda)
    return list(out[:nl]), list(out[nl:])


def _pack_gathered(convw_g, poolb_g, pws, pool_scale, ng):
    nl, d = pool_scale.shape
    taps = convw_g.shape[1] // nl
    dq = convw_g.shape[2]
    gq, gd = poolb_g.shape[2], pws[0].shape[2]

    def body(cg_ref, pb_ref, *rest):
        pw_refs, ps_ref = rest[:nl], rest[nl]
        outs = rest[nl + 1:]
        for layer in range(nl):
            cw_ref, pvec_ref, pwf_ref = outs[layer], outs[nl + layer], outs[2 * nl + layer]
            pvec_ref[...] = jnp.zeros_like(pvec_ref)
            pvec_ref[1:2, :] = ps_ref[layer:layer + 1, :]
            for k in range(4):
                cw_ref[:, k * dq:(k + 1) * dq] = cg_ref[k, layer * taps:(layer + 1) * taps, :]
                for g in range(ng):
                    lo = g * gd + k * gq
                    pvec_ref[0:1, lo:lo + gq] = pb_ref[k, layer * ng + g:layer * ng + g + 1, :]
                    pwf_ref[g, k * gq:(k + 1) * gq, :] = pw_refs[layer][k, g * gq:(g + 1) * gq, :]

    out = pl.pallas_call(
        body, name="pack_gathered", in_specs=[VMEM] * (3 + nl), out_specs=[VMEM] * (3 * nl),
        out_shape=[jax.ShapeDtypeStruct((taps, d), F32)] * nl + [jax.ShapeDtypeStruct((8, d), F32)] * nl
        + [jax.ShapeDtypeStruct((ng, gd, gd), BF16)] * nl,
        compiler_params=_WHOLE_VMEM,
    )(convw_g, poolb_g, *pws, pool_scale)
    return list(out[:nl]), list(out[nl:2 * nl]), list(out[2 * nl:])


ROW_SHIFT, ROW_SCALE, ROW_PRE, ROW_GATE, ROW_POST = 0, 1, 2, 8, 9
ROW_CONV_B, ROW_GATE_A_B, ROW_GATE_X_B, ROW_LAMBDA, ROW_CONV_W = 16, 17, 18, 19, 20
ROW_POOL_B, ROW_POOL_SCALE, ROW_SQ = 32, 33, 40


def _adamw_small(totals, chip_arr, params):
    nl = len(totals)
    d = totals[0].shape[1]
    n_par = len(params)
    flat = [a for p in params for a in p]
    nh, hd = params[6][0].shape[1], params[6][0].shape[2]
    taps, dq = params[8][0].shape[1], params[8][0].shape[2]
    ng, gq = params[9][0].shape[1], params[9][0].shape[2]
    gd = d // ng

    def body(chip_ref, *refs):
        tot = refs[:nl]
        ins = refs[nl:nl + 3 * n_par]
        outs = refs[nl + 3 * n_par:]
        chip = chip_ref[0]

        def update(p, idx, g):
            delta, m_new, v_new = _adam_update(ins[3 * p][idx], ins[3 * p + 1][idx], ins[3 * p + 2][idx], g)
            outs[4 * p][idx] = g
            outs[4 * p + 1][idx] = delta
            outs[4 * p + 2][idx] = m_new
            outs[4 * p + 3][idx] = v_new

        def mine(candidates):
            g = candidates[0]
            for k in range(1, 4):
                g = jnp.where(chip == k, candidates[k], g)
            return g

        for layer in range(nl):
            t = tot[layer]
            row = (slice(layer, layer + 1), slice(None))
            for j, r in enumerate((ROW_SHIFT, ROW_SCALE, ROW_GATE)):
                update(0, (slice(layer, layer + 1), slice(j * d, (j + 1) * d)), t[r:r + 1, :])
            for p, r in ((1, ROW_PRE), (2, ROW_POST), (3, ROW_CONV_B), (4, ROW_LAMBDA), (5, ROW_POOL_SCALE)):
                update(p, row, t[r:r + 1, :])
            for h in range(nh):
                idx = (layer, slice(h, h + 1), slice(None))
                update(6, idx, t[ROW_GATE_A_B:ROW_GATE_A_B + 1, h * hd:(h + 1) * hd])
                update(7, idx, t[ROW_GATE_X_B:ROW_GATE_X_B + 1, h * hd:(h + 1) * hd])
            for k in range(taps):
                r = ROW_CONV_W + k
                update(8, (layer, slice(k, k + 1), slice(None)), mine([t[r:r + 1, c * dq:(c + 1) * dq] for c in range(4)]))
            for g in range(ng):
                cands = [t[ROW_POOL_B:ROW_POOL_B + 1, g * gd + c * gq:g * gd + (c + 1) * gq] for c in range(4)]
                update(9, (layer, slice(g, g + 1), slice(None)), mine(cands))

    out = pl.pallas_call(
        body, name="adamw_small",
        in_specs=[pl.BlockSpec(memory_space=pltpu.SMEM)] + [VMEM] * (nl + 3 * n_par),
        out_specs=[VMEM] * (4 * n_par),
        out_shape=[jax.ShapeDtypeStruct(p[0].shape, F32) for p in params for _ in range(4)],
        compiler_params=_WHOLE_VMEM,
    )(chip_arr, *totals, *flat)
    return [tuple(out[4 * p:4 * p + 4]) for p in range(n_par)]


def _adamw_ada_w_layer(c_t, dmod, w, m, v, layer, prev, name):
    nl, d, n = w.shape
    nb = c_t.shape[1]
    tr = _row_tile(d)
    off = layer * (d // tr)
    n_prev = 0 if prev is None else 4

    def body(c_ref, dm_ref, w_ref, m_ref, v_ref, *rest):
        g_out, d_out, m_out, v_out = rest[n_prev:]
        g = c_ref[:, 0:1] * dm_ref[0:1, :]
        for b in range(1, nb):
            g = g + c_ref[:, b:b + 1] * dm_ref[b:b + 1, :]
        g_out[...] = g
        d_out[...], m_out[...], v_out[...] = _adam_update(w_ref[...], m_ref[...], v_ref[...], g)

    mine = pl.BlockSpec((tr, n), lambda i: (off + i, 0))
    outs = pl.pallas_call(
        body, name=name, grid=(d // tr,),
        in_specs=[pl.BlockSpec((tr, nb), lambda i: (i, 0)), pl.BlockSpec((nb, n), lambda i: (0, 0))] + [mine] * 3
        + [ANY] * n_prev,
        out_specs=[mine] * 4,
        out_shape=[jax.ShapeDtypeStruct((nl * d, n), F32)] * 4,
        input_output_aliases={5 + k: k for k in range(n_prev)},
        compiler_params=_params(1),
    )(c_t, dmod, *[a.reshape(nl * d, n) for a in (w, m, v)], *(prev or ()))
    return tuple(outs)


def _place():
    x, y, c = lax.axis_index("x"), lax.axis_index("y"), lax.axis_index("c")
    return x, y, c


OTHER_CHIPS = ((1, 0), (0, 1), (1, 1))
OTHER_DEVICES = tuple((fx, fy, fc) for fx in (0, 1) for fy in (0, 1) for fc in (0, 1))[1:]


def _mod_exchange(c_row, ada_w):
    nl, d, n = ada_w.shape

    def body(c_ref, w_ref, cbuf, modbuf, token, cblk, mres, send_a, recv_a, send_c, recv_c):
        token[...] = jnp.zeros_like(token)
        x, y, c = _place()
        me = 4 * x + 2 * y + c
        chip = 2 * x + y
        cv = c_ref[...]
        cblk[...] = jnp.zeros_like(cblk)
        cblk[0:1, :] = cv * _sigmoid(cv)

        def rows_of(dev):
            return cbuf.at[pl.ds(pl.multiple_of(8 * dev, 8), 8), :]

        cbuf[pl.ds(pl.multiple_of(8 * me, 8), 8), :] = cblk[...]
        sends = []
        for j, (fx, fy, fc) in enumerate(OTHER_DEVICES):
            cp = pltpu.make_async_remote_copy(
                src_ref=cblk, dst_ref=rows_of(me), send_sem=send_a.at[j], recv_sem=recv_a.at[j],
                device_id=(x ^ fx, y ^ fy, c ^ fc), device_id_type=MESH)
            cp.start()
            sends.append(cp)
        for j, (fx, fy, fc) in enumerate(OTHER_DEVICES):
            peer = 4 * (x ^ fx) + 2 * (y ^ fy) + (c ^ fc)
            pltpu.make_async_remote_copy(
                src_ref=cblk, dst_ref=rows_of(peer), send_sem=send_a.at[j], recv_sem=recv_a.at[j],
                device_id=(x ^ fx, y ^ fy, c ^ fc), device_id_type=MESH).wait_recv()
        for cp in sends:
            cp.wait_send()

        call = cbuf[...]
        for layer in range(nl):
            mres[:, layer * n:(layer + 1) * n] = jnp.dot(
                call, w_ref[layer], preferred_element_type=F32, precision=lax.Precision.HIGHEST)

        def block_of(dev):
            return mres.at[pl.ds(pl.multiple_of(8 * dev, 8), 8), :]

        modbuf[chip] = mres[pl.ds(pl.multiple_of(8 * me, 8), 8), :]
        sends = []
        for j, (fx, fy) in enumerate(OTHER_CHIPS):
            peer = 4 * (x ^ fx) + 2 * (y ^ fy) + c
            cp = pltpu.make_async_remote_copy(
                src_ref=block_of(peer), dst_ref=modbuf.at[chip], send_sem=send_c.at[j], recv_sem=recv_c.at[j],
                device_id=(x ^ fx, y ^ fy, c), device_id_type=MESH)
            cp.start()
            sends.append(cp)
        for j, (fx, fy) in enumerate(OTHER_CHIPS):
            pltpu.make_async_remote_copy(
                src_ref=block_of(me), dst_ref=modbuf.at[2 * (x ^ fx) + (y ^ fy)],
                send_sem=send_c.at[j], recv_sem=recv_c.at[j],
                device_id=(x ^ fx, y ^ fy, c), device_id_type=MESH).wait_recv()
        for cp in sends:
            cp.wait_send()

    return pl.pallas_call(
        body, name="mod_exchange", in_specs=[VMEM, VMEM], out_specs=[VMEM, VMEM, VMEM],
        out_shape=[jax.ShapeDtypeStruct((64, d), F32), jax.ShapeDtypeStruct((4, 8, nl * n), F32),
                   jax.ShapeDtypeStruct((8, 128), F32)],
        scratch_shapes=[pltpu.VMEM((8, d), F32), pltpu.VMEM((64, nl * n), F32),
                        pltpu.SemaphoreType.DMA((7,)), pltpu.SemaphoreType.DMA((7,)),
                        pltpu.SemaphoreType.DMA((3,)), pltpu.SemaphoreType.DMA((3,))],
        compiler_params=pltpu.CompilerParams(vmem_limit_bytes=V7X_VMEM_LIMIT_BYTES, has_side_effects=True),
    )(c_row, ada_w)


def _in_hbm(a):
    return pltpu.with_memory_space_constraint(a, pltpu.HBM)


def _gather_copies(lands, split, over_ici):
    x, y, c = _place()
    chip = 2 * x + y
    out = []
    for t, land in enumerate(lands):
        half = land.shape[1] // 2
        mine = pl.ds(pl.multiple_of(c * half, half), half)
        theirs = pl.ds(pl.multiple_of((1 - c) * half, half), half)
        for j, (fx, fy) in enumerate(OTHER_CHIPS):
            them = 2 * (x ^ fx) + (y ^ fy)
            if over_ici and split[t]:
                out.append((land.at[chip, mine], land.at[chip, mine], land.at[them, mine], (x ^ fx, y ^ fy, c), 3 * t + j))
            elif over_ici:
                out.append((land.at[chip], land.at[chip], land.at[them], (x ^ fx, y ^ fy, c), 3 * t + j))
            elif split[t]:
                out.append((land.at[them, mine], land.at[them, mine], land.at[them, theirs], (x, y, 1 - c), 3 * t + j))
    return out


def _gather_start(lands, groups, split):
    n, ngr = len(lands), len(groups)

    def body(*refs):
        sems = refs[n:n + 2 * ngr]
        for gi, idxs in enumerate(groups):
            for src, dst, _, peer, k in _gather_copies([refs[i] for i in idxs], [split[i] for i in idxs], True):
                pltpu.make_async_remote_copy(src_ref=src, dst_ref=dst, send_sem=sems[2 * gi].at[k],
                                             recv_sem=sems[2 * gi + 1].at[k], device_id=peer, device_id_type=MESH).start()
        refs[-1][...] = jnp.zeros_like(refs[-1])

    sem_shapes = []
    for idxs in groups:
        sem_shapes += [pltpu.SemaphoreType.DMA((3 * len(idxs),))] * 2
    out = pl.pallas_call(
        body, name="weight_gather_start",
        in_specs=[HBM] * n, out_specs=[SEM] * (2 * ngr) + [HBM] * n + [VMEM],
        out_shape=sem_shapes + [pltpu.HBM(a.shape, a.dtype) for a in lands] + [jax.ShapeDtypeStruct((8, 128), F32)],
        input_output_aliases={i: 2 * ngr + i for i in range(n)},
        compiler_params=pltpu.CompilerParams(has_side_effects=DATAFLOW_EFFECT),
    )(*[_in_hbm(a) for a in lands])
    sems = [(out[2 * gi], out[2 * gi + 1]) for gi in range(ngr)]
    return sems, list(out[2 * ngr:2 * ngr + n]), out[-1]


def _gather_forward(lands, split, sems, after, name):
    n = len(lands)

    def body(*refs):
        ici_send, ici_recv = refs[n], refs[n + 1]
        fwd_send, fwd_recv = refs[n + 3], refs[n + 4]
        forwards = {k: (src, dst, peer) for src, dst, _, peer, k in _gather_copies(refs[:n], split, False)}
        for src, _, landed, peer, k in _gather_copies(refs[:n], split, True):
            cp = pltpu.make_async_remote_copy(src_ref=src, dst_ref=landed, send_sem=ici_send.at[k], recv_sem=ici_recv.at[k],
                                              device_id=peer, device_id_type=MESH)
            cp.wait_recv()
            if k in forwards:
                fsrc, fdst, fpeer = forwards[k]
                pltpu.make_async_remote_copy(src_ref=fsrc, dst_ref=fdst, send_sem=fwd_send.at[k], recv_sem=fwd_recv.at[k],
                                             device_id=fpeer, device_id_type=MESH).start()
            cp.wait_send()

    out = pl.pallas_call(
        body, name=name,
        in_specs=[HBM] * n + [SEM, SEM, ANY], out_specs=[SEM, SEM] + [HBM] * n,
        out_shape=[pltpu.SemaphoreType.DMA((3 * n,))] * 2 + [pltpu.HBM(a.shape, a.dtype) for a in lands],
        input_output_aliases={i: 2 + i for i in range(n)},
        compiler_params=pltpu.CompilerParams(has_side_effects=DATAFLOW_EFFECT),
    )(*lands, sems[0], sems[1], after)
    return (out[0], out[1]), list(out[2:])


def _gather_wait(lands, split, sems, name):
    n = len(lands)

    def body(*refs):
        send_sems, recv_sems = refs[n], refs[n + 1]
        for src, _, landed, peer, k in _gather_copies(refs[:n], split, False):
            cp = pltpu.make_async_remote_copy(src_ref=src, dst_ref=landed, send_sem=send_sems.at[k], recv_sem=recv_sems.at[k],
                                              device_id=peer, device_id_type=MESH)
            cp.wait_send()
            cp.wait_recv()

    out = pl.pallas_call(
        body, name=name,
        in_specs=[HBM] * n + [SEM, SEM], out_specs=[HBM] * n,
        out_shape=[pltpu.HBM(a.shape, a.dtype) for a in lands],
        input_output_aliases={i: i for i in range(n)},
        compiler_params=pltpu.CompilerParams(has_side_effects=DATAFLOW_EFFECT),
    )(*lands, sems[0], sems[1])
    return list(out)


def _to_owner_copies(pairs, q):
    x, y, c = _place()
    chip = 2 * x + y
    out = []
    for t, (part, land) in enumerate(pairs):
        for j, (fx, fy) in enumerate(OTHER_CHIPS):
            owner = 2 * (x ^ fx) + (y ^ fy)
            if part.shape[0] == 4 and part.shape[1:] == land.shape[1:]:
                src = part.at[owner]
            else:
                src = part.at[:, pl.ds(pl.multiple_of(owner * q, q), q), :]
            out.append((src, land.at[chip], land.at[owner], (x ^ fx, y ^ fy, c), 3 * t + j))
    return out


def _to_all_copies(bufs, first_sem):
    x, y, c = _place()
    me = 4 * x + 2 * y + c
    out = []
    for t, buf in enumerate(bufs):
        for j, (fx, fy, fc) in enumerate(OTHER_DEVICES):
            them = 4 * (x ^ fx) + 2 * (y ^ fy) + (c ^ fc)
            out.append((buf.at[me], buf.at[me], buf.at[them], (x ^ fx, y ^ fy, c ^ fc), first_sem + 7 * t + j))
    return out


def _to_chips_copies(bufs, first_sem):
    x, y, c = _place()
    chip = 2 * x + y
    out = []
    for t, buf in enumerate(bufs):
        for j, (fx, fy) in enumerate(OTHER_CHIPS):
            them = 2 * (x ^ fx) + (y ^ fy)
            out.append((buf.at[chip], buf.at[chip], buf.at[them], (x ^ fx, y ^ fy, c), first_sem + 3 * t + j))
    return out


def _exchange_copies(refs, kinds, q):
    n_owner, n_chips = kinds
    pairs = list(zip(refs[:n_owner], refs[n_owner:2 * n_owner]))
    first_all = 3 * (n_owner + n_chips)
    return (_to_owner_copies(pairs, q) + _to_chips_copies(refs[2 * n_owner:2 * n_owner + n_chips], 3 * n_owner)
            + _to_all_copies(refs[2 * n_owner + n_chips:], first_all))


def _exchange_start(arrays, kinds, q, name):
    n = len(arrays)
    n_sems = 3 * (kinds[0] + kinds[1]) + 7 * (n - 2 * kinds[0] - kinds[1])

    def body(*refs):
        send_sems, recv_sems = refs[n], refs[n + 1]
        for src, dst, _, peer, k in _exchange_copies(refs[:n], kinds, q):
            pltpu.make_async_remote_copy(src_ref=src, dst_ref=dst, send_sem=send_sems.at[k], recv_sem=recv_sems.at[k],
                                         device_id=peer, device_id_type=MESH).start()
        refs[-1][...] = jnp.zeros_like(refs[-1])

    out = pl.pallas_call(
        body, name=name,
        in_specs=[HBM] * n, out_specs=[SEM, SEM] + [HBM] * n + [VMEM],
        out_shape=[pltpu.SemaphoreType.DMA((n_sems,))] * 2 + [pltpu.HBM(a.shape, a.dtype) for a in arrays]
        + [jax.ShapeDtypeStruct((8, 128), F32)],
        input_output_aliases={i: 2 + i for i in range(n)},
        compiler_params=pltpu.CompilerParams(has_side_effects=DATAFLOW_EFFECT),
    )(*[_in_hbm(a) for a in arrays])
    return (out[0], out[1]), list(out[2:2 + n]), out[-1]


def _exchange_wait(arrays, sems, kinds, q, after, name):
    n = len(arrays)

    def body(*refs):
        send_sems, recv_sems = refs[n], refs[n + 1]
        for src, _, landed, peer, k in _exchange_copies(refs[:n], kinds, q):
            cp = pltpu.make_async_remote_copy(src_ref=src, dst_ref=landed, send_sem=send_sems.at[k], recv_sem=recv_sems.at[k],
                                              device_id=peer, device_id_type=MESH)
            cp.wait_send()
            cp.wait_recv()

    out = pl.pallas_call(
        body, name=name,
        in_specs=[HBM] * n + [SEM, SEM, ANY], out_specs=[HBM] * n,
        out_shape=[pltpu.HBM(a.shape, a.dtype) for a in arrays],
        input_output_aliases={i: i for i in range(n)},
        compiler_params=pltpu.CompilerParams(has_side_effects=DATAFLOW_EFFECT),
    )(*arrays, sems[0], sems[1], after)
    return list(out)


def _sibling_swap(parts, layer):
    n = len(parts)

    def body(*refs):
        srcs, outs = refs[:n], refs[n:2 * n]
        send_sems, recv_sems = refs[2 * n:]
        x, y, c = _place()
        cps = [pltpu.make_async_remote_copy(
            src_ref=srcs[i], dst_ref=outs[i], send_sem=send_sems.at[i], recv_sem=recv_sems.at[i],
            device_id=(x, y, 1 - c), device_id_type=MESH) for i in range(n)]
        for cp in cps:
            cp.start()
        for cp in cps:
            cp.wait()

    return pl.pallas_call(
        body, name=f"sibling_swap_l{layer}", in_specs=[ANY] * n, out_specs=[ANY] * n,
        out_shape=[jax.ShapeDtypeStruct(a.shape, a.dtype) for a in parts],
        scratch_shapes=[pltpu.SemaphoreType.DMA((n,)), pltpu.SemaphoreType.DMA((n,))],
        compiler_params=pltpu.CompilerParams(has_side_effects=True),
    )(*parts)


def kernel(x, c, ada_w, ada_b, pre_norm_g, w_in, conv_w, conv_b, gate_a_w, gate_a_b, gate_x_w, gate_x_b, lru_lambda, pool_w, pool_b, pool_scale, w_out, post_norm_g, loss_target, m_ada_w, m_ada_b, m_pre_norm_g, m_w_in, m_conv_w, m_conv_b, m_gate_a_w, m_gate_a_b, m_gate_x_w, m_gate_x_b, m_lru_lambda, m_pool_w, m_pool_b, m_pool_scale, m_w_out, m_post_norm_g, v_ada_w, v_ada_b, v_pre_norm_g, v_w_in, v_conv_w, v_conv_b, v_gate_a_w, v_gate_a_b, v_gate_x_w, v_gate_x_b, v_lru_lambda, v_pool_w, v_pool_b, v_pool_scale, v_w_out, v_post_norm_g):
    nl, d, n_ada = ada_w.shape
    s = x.shape[1]
    nh, hd = gate_a_w.shape[1], gate_a_w.shape[2]
    ng, gq, gd = pool_w.shape[1], pool_w.shape[2], pool_w.shape[3]
    me = 4 * lax.axis_index("x") + 2 * lax.axis_index("y") + lax.axis_index("c")
    chip = 2 * lax.axis_index("x") + lax.axis_index("y")
    chip_arr = jnp.reshape(chip, (1,)).astype(jnp.int32)
    x0 = x.reshape(s, d)
    target = loss_target.reshape(s, d)
    p_in = w_in.shape[2]
    r_out = w_out.shape[1]

    cbuf, modbuf, mod_token = _mod_exchange(c.reshape(1, d), ada_w)
    vecs, rvecs = _pack_vectors(modbuf, ada_b, pre_norm_g, post_norm_g, conv_b, gate_a_b, gate_x_b, lru_lambda)

    win = [_into_slot(w_in, BF16, chip_arr, f"slot_w_in_l{l}", l) for l in range(nl)]
    wout = [_into_slot(w_out, BF16, chip_arr, f"slot_w_out_l{l}", l) for l in range(nl)]
    pw = [_into_slot(pool_w.reshape(nl, ng * gq, gd), BF16, chip_arr, f"slot_pool_w_l{l}", l) for l in range(nl)]
    convw = _into_slot(conv_w.reshape(nl * CONV_WIDTH, d // 4), F32, chip_arr, "slot_conv_w", after=mod_token)
    poolb = _into_slot(pool_b.reshape(nl * ng, gq), F32, chip_arr, "slot_pool_b")
    lands = [win[0], convw, poolb, *pw, wout[0]]
    split = [True, False, False] + [True] * (nl + 1)
    groups = [[0], list(range(1, len(lands)))]
    for l in range(1, nl):
        groups.append([len(lands), len(lands) + 1])
        lands += [win[l], wout[l]]
        split += [True, True]
    sems, lands, token = _gather_start(lands, groups, split)
    wa_b, wx_b = gate_a_w.astype(BF16), gate_x_w.astype(BF16)

    def gathered(gi, after, tag):
        idxs = groups[gi]
        arrays, halves = [lands[i] for i in idxs], [split[i] for i in idxs]
        between, arrays = _gather_forward(arrays, halves, sems[gi], after, f"weight_gather_forward_{tag}")
        return _gather_wait(arrays, halves, between, f"weight_gather_wait_{tag}")

    xs, projs, hss, ycats, ys = [x0], [], [], [], []
    sq = None
    convw_full = poolw_full = pvecs = None
    for l in range(nl):
        if l == 0:
            (win[0],) = gathered(0, modbuf, "a")
        proj = _inproj_fwd(xs[l], vecs[l], win[l], l)
        if l == 0:
            got = gathered(1, proj, "b")
            wout[0] = got[2 + nl]
            convw_full, pvecs, poolw_full = _pack_gathered(got[0], got[1], got[2:2 + nl], pool_scale, ng)
        ycat, hs = _rnn_fwd(proj, convw_full[l], rvecs[l], wa_b[l], wx_b[l], l)
        if l + 1 < nl:
            win[l + 1], wout[l + 1] = gathered(2 + l, hs, f"c{l + 1}")
        ycat = _pool_fwd(proj, ycat, poolw_full[l], pvecs[l], l)
        y, xo, sq = _outproj_fwd(ycat, wout[l], xs[l], vecs[l], target if l == nl - 1 else None, l)
        projs.append(proj), hss.append(hs), ycats.append(ycat), ys.append(y), xs.append(xo)

    c_all_t = cbuf.reshape(8, 8, d)[:, 0, :].T

    def finish(l, flights, after, prev):
        (sems_a, arr_a), (sems_g, arr_g), (sems_b, arr_b), (sems_c, arr_c) = flights
        dwout_l, rwout = _exchange_wait(arr_a, sems_a, (1, 0), gq, after, f"grad_wait_a_l{l}")
        dpw_l, rpw, gates = _exchange_wait(arr_g, sems_g, (1, 1), gq, rwout, f"grad_wait_g_l{l}")
        dwin_l, rwin = _exchange_wait(arr_b, sems_b, (1, 0), gq, gates, f"grad_wait_b_l{l}")
        (slabs,) = _exchange_wait(arr_c, sems_c, (0, 0), gq, rwin, f"grad_wait_c_l{l}")
        p_win = _sum_owner(dwin_l, rwin, chip_arr, lambda tr: (None, tr, p_in),
                           lambda i, chip: (chip[0], i, 0), f"sum_w_in_l{l}")
        p_wout = _sum_owner(dwout_l, rwout, chip_arr, lambda tr: (None, tr, d),
                            lambda i, chip: (chip[0], i, 0), f"sum_w_out_l{l}")
        p_pw = _sum_owner(dpw_l, rpw, chip_arr, lambda tr: (ng, tr, gd),
                          lambda i, chip: (0, chip[0], 0), f"sum_pool_w_l{l}")
        p_gates = _sum_slots(gates.reshape(4, 2 * nh * hd, hd), f"sum_gates_l{l}")
        q_win, q_wout, q_pw, q_gates = _sibling_swap([p_win, p_wout, p_pw, p_gates], l)
        per_dev = jnp.concatenate([slabs[:, r] for r in (ROW_SHIFT, ROW_SCALE, ROW_GATE)], axis=-1)
        dmod_mine = lax.dynamic_slice_in_dim(per_dev, chip * n_ada, n_ada, axis=1)
        prev = prev or {}
        big = {
            "w_in": _adamw_layer(w_in, m_w_in, v_w_in, [p_win, q_win], l, prev.get("w_in"), f"adamw_w_in_l{l}"),
            "w_out": _adamw_layer(w_out, m_w_out, v_w_out, [p_wout, q_wout], l, prev.get("w_out"), f"adamw_w_out_l{l}"),
            "pool_w": _adamw_layer(pool_w, m_pool_w, v_pool_w, [p_pw, q_pw], l, prev.get("pool_w"), f"adamw_pool_w_l{l}"),
            "gate_a_w": _adamw_layer(gate_a_w, m_gate_a_w, v_gate_a_w, [p_gates, q_gates], l, prev.get("gate_a_w"),
                                     f"adamw_gate_a_w_l{l}"),
            "gate_x_w": _adamw_layer(gate_x_w, m_gate_x_w, v_gate_x_w, [p_gates, q_gates], l, prev.get("gate_x_w"),
                                     f"adamw_gate_x_w_l{l}", grad_row_offset=nh * hd),
            "ada_w": _adamw_ada_w_layer(c_all_t, dmod_mine, ada_w, m_ada_w, v_ada_w, l, prev.get("ada_w"),
                                        f"adamw_ada_w_l{l}"),
        }
        return big, _sum_slots(slabs, f"sum_slab_l{l}")

    dx = xs[nl]
    flights = token = big = None
    totals = [None] * nl
    for l in reversed(range(nl)):
        vec_l = vecs[l]
        dycat, dwout_l, dvec_o = _outproj_bwd(dx, ys[l], ycats[l], wout[l], vec_l, l, vec_l if token is None else token)
        sems_a, arr_a, tok_a = _exchange_start([dwout_l, lax.empty(dwout_l.shape, BF16)], (1, 0), gq, f"grad_start_a_l{l}")
        dproj, dgates, dvec_r = _rnn_bwd(projs[l], hss[l], dycat, convw_full[l], rvecs[l], wa_b[l], wx_b[l], l, tok_a)
        dproj, dpw_l, dvec_p = _pool_bwd(projs[l], dycat, dproj, poolw_full[l], pvecs[l], l)
        gates4 = lax.dynamic_update_slice(lax.empty((4, *dgates.shape), BF16), dgates[None], (chip, 0, 0, 0, 0))
        sems_g, arr_g, tok_g = _exchange_start([dpw_l, lax.empty((4, ng, gq, gd), BF16), gates4], (1, 1), gq,
                                               f"grad_start_g_l{l}")
        dwin_l = _inproj_bwd_w(dproj, xs[l], vec_l, l, tok_g)
        sems_b, arr_b, tok_b = _exchange_start([dwin_l, lax.empty(dwin_l.shape, BF16)], (1, 0), gq, f"grad_start_b_l{l}")
        dx, dvec_i = _inproj_bwd_x(dproj, win[l], xs[l], dx, vec_l, l, tok_b)
        parts = [dvec_i, dvec_o, dvec_r, dvec_p]
        if l == nl - 1:
            parts.append(jnp.tile(sq, (1, d // sq.shape[1])))
        slab = jnp.concatenate(parts, axis=0)
        slabs = lax.dynamic_update_slice(lax.empty((8, *slab.shape), F32), slab[None], (me, 0, 0))
        sems_c, arr_c, token = _exchange_start([slabs], (0, 0), gq, f"grad_start_c_l{l}")
        if flights is not None:
            big, totals[l + 1] = finish(l + 1, flights, token, big)
        flights = ((sems_a, arr_a), (sems_g, arr_g), (sems_b, arr_b), (sems_c, arr_c))
    big, totals[0] = finish(0, flights, big["w_in"][3] if big else dx, big)
    grad_x = dx.reshape(x.shape)
    loss = totals[nl - 1][ROW_SQ, 0] * (0.5 / d)

    small = _adamw_small(totals, chip_arr, [
        (ada_b, m_ada_b, v_ada_b), (pre_norm_g, m_pre_norm_g, v_pre_norm_g), (post_norm_g, m_post_norm_g, v_post_norm_g),
        (conv_b, m_conv_b, v_conv_b), (lru_lambda, m_lru_lambda, v_lru_lambda), (pool_scale, m_pool_scale, v_pool_scale),
        (gate_a_b, m_gate_a_b, v_gate_a_b), (gate_x_b, m_gate_x_b, v_gate_x_b),
        (conv_w, m_conv_w, v_conv_w), (pool_b, m_pool_b, v_pool_b)])

    results = {
        "ada_w": tuple(o.reshape(ada_w.shape) for o in big["ada_w"]),
        "ada_b": small[0],
        "pre_norm_g": small[1],
        "w_in": tuple(o.reshape(w_in.shape) for o in big["w_in"]),
        "conv_w": small[8],
        "conv_b": small[3],
        "gate_a_w": tuple(o.reshape(gate_a_w.shape) for o in big["gate_a_w"]),
        "gate_a_b": small[6],
        "gate_x_w": tuple(o.reshape(gate_x_w.shape) for o in big["gate_x_w"]),
        "gate_x_b": small[7],
        "lru_lambda": small[4],
        "pool_w": tuple(o.reshape(pool_w.shape) for o in big["pool_w"]),
        "pool_b": small[9],
        "pool_scale": small[5],
        "w_out": tuple(o.reshape(w_out.shape) for o in big["w_out"]),
        "post_norm_g": small[2],
    }
    names = list(results)
    return (loss, grad_x,
            *[results[n][0] for n in names], *[results[n][1] for n in names],
            *[results[n][2] for n in names], *[results[n][3] for n in names])
```

```python
import functools

import jax
import jax.numpy as jnp
from jax import lax
from jax.experimental import pallas as pl
from jax.experimental.pallas import tpu as pltpu

F32 = jnp.float32
BF16 = jnp.bfloat16

NORM_EPS = 1e-6
LRU_C = 8.0
CONV_WIDTH = 4
MAX_POOL_WINDOW = 16
HALO = 16
ADAM_LR = 0.001
ADAM_B1 = 0.9
ADAM_B2 = 0.999
ADAM_EPS = 1e-08
ADAM_WD = 0.01
ADAM_STEP = 10

V7X_VMEM_LIMIT_BYTES = 56 * 1024 * 1024
MATMUL_ROWS = 512
SCAN_ROWS = 512
BWD_SCAN_ROWS = 1024
ELEMENTWISE_ROWS = 512

MESH = pl.DeviceIdType.MESH
ANY = pl.BlockSpec(memory_space=pl.ANY)
VMEM = pl.BlockSpec(memory_space=pltpu.VMEM)
HBM = pl.BlockSpec(memory_space=pltpu.HBM)
SEM = pl.BlockSpec(memory_space=pltpu.SEMAPHORE)
DATAFLOW_EFFECT = pltpu.SideEffectType.DATAFLOW_SIDE_EFFECTING

NT_DIMS = (((1,), (1,)), ((), ()))
TN_DIMS = (((0,), (0,)), ((), ()))


def _params(n_grid_axes):
    return pltpu.CompilerParams(dimension_semantics=("arbitrary",) * n_grid_axes,
                                vmem_limit_bytes=V7X_VMEM_LIMIT_BYTES)


def _tile(total, want):
    t = min(want, max(total // 2, HALO))
    assert total % t == 0 and t % HALO == 0, (total, t)
    return t


def _row_tile(rows):
    for t in range(min(rows, ELEMENTWISE_ROWS) // 8 * 8, 0, -8):
        if rows % t == 0:
            return t
    return rows


def _sigmoid(z):
    return 1.0 / (1.0 + jnp.exp(-z))


def _softplus(z):
    return jnp.maximum(z, 0.0) + jnp.log(1.0 + jnp.exp(-jnp.abs(z)))


def _neg_expm1(z):
    return -jnp.tanh(0.5 * z) * (jnp.exp(z) + 1.0)


def _colsum(v):
    return jnp.sum(v, axis=0, keepdims=True)


def _prenorm(xt, vec_ref):
    rs = lax.rsqrt(jnp.mean(xt * xt, axis=-1, keepdims=True) + NORM_EPS)
    xn = xt * rs
    h = xn * vec_ref[3:4, :] * (1.0 + vec_ref[1:2, :]) + vec_ref[0:1, :]
    return h, xn, rs


def _shift_down(v, d, fill):
    t = v.shape[0]
    if d % 8 == 0:
        return jnp.concatenate([jnp.full((d, v.shape[1]), fill, v.dtype), v[:t - d]], axis=0)
    row = lax.broadcasted_iota(jnp.int32, v.shape, 0)
    return jnp.where(row >= d, pltpu.roll(v, d, 0), fill)


def _shift_up(v, d, fill):
    t = v.shape[0]
    if d % 8 == 0:
        return jnp.concatenate([v[d:], jnp.full((d, v.shape[1]), fill, v.dtype)], axis=0)
    row = lax.broadcasted_iota(jnp.int32, v.shape, 0)
    return jnp.where(row < t - d, pltpu.roll(v, t - d, 0), fill)


def _scan_fwd(a, v, h_before):
    d = 1
    while d < a.shape[0]:
        v = v + a * _shift_down(v, d, 0.0)
        a = a * _shift_down(a, d, 1.0)
        d *= 2
    return a * h_before + v


def _scan_rev(b, v):
    d = 1
    while d < b.shape[0]:
        v = v + b * _shift_up(v, d, 0.0)
        b = b * _shift_up(b, d, 0.0)
        d *= 2
    return v


def _inproj_fwd(x, vec, w_all, layer):
    s, d = x.shape
    p = w_all.shape[2]
    ts = _tile(s, MATMUL_ROWS)

    def body(x_ref, vec_ref, w_ref, proj_ref):
        h, _, _ = _prenorm(x_ref[...], vec_ref)
        hb = h.astype(BF16)
        for k in range(4):
            proj_ref[k] = jnp.dot(hb, w_ref[k], preferred_element_type=F32)

    return pl.pallas_call(
        body, name=f"inproj_fwd_l{layer}", grid=(s // ts,),
        in_specs=[pl.BlockSpec((ts, d), lambda i: (i, 0)),
                  pl.BlockSpec((8, d), lambda i: (0, 0)),
                  pl.BlockSpec((4, d, p), lambda i: (0, 0, 0))],
        out_specs=pl.BlockSpec((4, ts, p), lambda i: (0, i, 0)),
        out_shape=jax.ShapeDtypeStruct((4, s, p), F32),
        compiler_params=_params(1),
    )(x, vec, w_all)


HEADS_PER_STEP = 2
BWD_HEADS_PER_STEP = 1


def _rnn_gates(u, wa, wx, vec_ref, lanes):
    ub = u.astype(BF16)
    r = _sigmoid(jnp.dot(ub, wa, preferred_element_type=F32) + vec_ref[1:2, lanes])
    ig = _sigmoid(jnp.dot(ub, wx, preferred_element_type=F32) + vec_ref[2:3, lanes])
    sp = _softplus(-vec_ref[3:4, lanes])
    log_a = (-LRU_C) * r * sp
    return ub, r, ig, sp, log_a


def _conv(xbuf, cw_ref, vec_ref, lanes, ts):
    u = vec_ref[0:1, lanes] + cw_ref[CONV_WIDTH - 1:CONV_WIDTH, lanes] * xbuf[pl.ds(HALO, ts), lanes]
    for k in range(CONV_WIDTH - 1):
        u = u + cw_ref[k:k + 1, lanes] * xbuf[pl.ds(HALO - (CONV_WIDTH - 1) + k, ts), lanes]
    return u


def _rnn_fwd(proj, cw, vec, wa, wx, layer):
    _, s, d = proj.shape
    nh, hd, _ = wa.shape
    ts = _tile(s, SCAN_ROWS)
    hps = HEADS_PER_STEP
    wl = hps * hd

    def body(proj_ref, cw_ref, vec_ref, wa_ref, wx_ref, ycat_ref, hs_ref, xbuf, hlast):
        i = pl.program_id(1)

        @pl.when(i == 0)
        def _():
            xbuf[0:HALO, :] = jnp.zeros((HALO, wl), F32)
            hlast[...] = jnp.zeros_like(hlast)

        xbuf[pl.ds(HALO, ts), :] = proj_ref[0]
        for hh in range(hps):
            lanes = slice(hh * hd, (hh + 1) * hd)
            u = _conv(xbuf, cw_ref, vec_ref, lanes, ts)
            _, _, ig, _, log_a = _rnn_gates(u, wa_ref[hh], wx_ref[hh], vec_ref, lanes)
            a = jnp.exp(log_a)
            mult = jnp.sqrt(_neg_expm1(2.0 * log_a))
            hs = _scan_fwd(a, mult * (ig * u), hlast[0:1, lanes])
            hs_ref[:, lanes] = hs
            hlast[0:1, lanes] = hs_ref[ts - 1:ts, lanes]
            g = proj_ref[1, :, lanes]
            ycat_ref[:, lanes] = (hs * (g * _sigmoid(g))).astype(BF16)
        xbuf[0:HALO, :] = xbuf[pl.ds(ts, HALO), :]

    return pl.pallas_call(
        body, name=f"rnn_fwd_l{layer}", grid=(nh // hps, s // ts),
        in_specs=[pl.BlockSpec((2, ts, wl), lambda h, i: (0, i, h)),
                  pl.BlockSpec((CONV_WIDTH, wl), lambda h, i: (0, h)),
                  pl.BlockSpec((8, wl), lambda h, i: (0, h)),
                  pl.BlockSpec((hps, hd, hd), lambda h, i: (h, 0, 0)),
                  pl.BlockSpec((hps, hd, hd), lambda h, i: (h, 0, 0))],
        out_specs=[pl.BlockSpec((ts, wl), lambda h, i: (i, h)),
                   pl.BlockSpec((ts, wl), lambda h, i: (i, h))],
        out_shape=[jax.ShapeDtypeStruct((s, 2 * d), BF16), jax.ShapeDtypeStruct((s, d), F32)],
        scratch_shapes=[pltpu.VMEM((ts + HALO, wl), F32), pltpu.VMEM((8, wl), F32)],
        compiler_params=_params(2),
    )(proj, cw, vec, wa, wx)


def _inv_count(i, ts, lanes, win):
    t = i * ts + lax.broadcasted_iota(jnp.int32, (ts, lanes), 0)
    return 1.0 / jnp.minimum(t + 1, win).astype(F32)


def _window_sum(ext, win, forward):
    rows = ext.shape[0]
    s, d = ext, 1
    while d < win:
        s = s + pltpu.roll(s, d if forward else rows - d, 0)
        d *= 2
    return s


def _pooled(xbuf, xt, lanes, win, inv_cnt, ts):
    acc = _window_sum(xbuf[:, lanes], win, True)[HALO:, :]
    return acc * inv_cnt - xt


def _pool_fwd(proj, ycat, pw, vec, layer):
    _, s, d = proj.shape
    ng, gd, _ = pw.shape
    ts = _tile(s, MATMUL_ROWS)

    def body(proj_ref, ycat_in, pw_ref, vec_ref, ycat_ref, xbuf):
        del ycat_in
        i = pl.program_id(0)

        @pl.when(i == 0)
        def _():
            xbuf[0:HALO, :] = jnp.zeros((HALO, d), F32)

        xbuf[pl.ds(HALO, ts), :] = proj_ref[0]
        for g in range(ng):
            lanes = slice(g * gd, (g + 1) * gd)
            win = 2 << g
            xt = proj_ref[0, :, lanes]
            pooled = _pooled(xbuf, xt, lanes, win, _inv_count(i, ts, gd, win), ts).astype(BF16)
            z = jnp.dot(pooled, pw_ref[g], preferred_element_type=F32) + vec_ref[0:1, lanes]
            gg = proj_ref[1, :, lanes]
            ycat_ref[:, lanes] = (z * vec_ref[1:2, lanes] * (gg * _sigmoid(gg))).astype(BF16)
        xbuf[0:HALO, :] = xbuf[pl.ds(ts, HALO), :]

    return pl.pallas_call(
        body, name=f"pool_fwd_l{layer}", grid=(s // ts,),
        in_specs=[pl.BlockSpec((2, ts, d), lambda i: (1, i, 0)),
                  ANY,
                  pl.BlockSpec((ng, gd, gd), lambda i: (0, 0, 0)),
                  pl.BlockSpec((8, d), lambda i: (0, 0))],
        out_specs=pl.BlockSpec((ts, d), lambda i: (i, 1)),
        out_shape=jax.ShapeDtypeStruct((s, 2 * d), BF16),
        input_output_aliases={1: 0},
        scratch_shapes=[pltpu.VMEM((ts + HALO, d), F32)],
        compiler_params=_params(1),
    )(proj, ycat, pw, vec)


def _outproj_fwd(ycat, w_all, x, vec, target, layer):
    s, d = x.shape
    nk, kd = w_all.shape[0], w_all.shape[1]
    ts = _tile(s, MATMUL_ROWS)
    last = target is not None

    def body(*refs):
        if last:
            ycat_ref, w_ref, x_ref, vec_ref, tgt_ref, y_ref, xo_ref, sq_ref = refs
        else:
            ycat_ref, w_ref, x_ref, vec_ref, y_ref, xo_ref = refs
        y = jnp.dot(ycat_ref[:, 0:kd], w_ref[0], preferred_element_type=F32)
        for k in range(1, nk):
            y = y + jnp.dot(ycat_ref[:, k * kd:(k + 1) * kd], w_ref[k], preferred_element_type=F32)
        y_ref[...] = y
        rs = lax.rsqrt(jnp.mean(y * y, axis=-1, keepdims=True) + NORM_EPS)
        xo = x_ref[...] + vec_ref[2:3, :] * (y * rs * vec_ref[4:5, :])
        if last:
            err = xo - tgt_ref[...]
            xo_ref[...] = err * (1.0 / d)

            @pl.when(pl.program_id(0) == 0)
            def _():
                sq_ref[...] = jnp.zeros_like(sq_ref)

            sq_ref[...] += jnp.sum(err * err)
        else:
            xo_ref[...] = xo

    row = pl.BlockSpec((ts, d), lambda i: (i, 0))
    in_specs = [pl.BlockSpec((ts, nk * kd), lambda i: (i, 0)),
                pl.BlockSpec((nk, kd, d), lambda i: (0, 0, 0)),
                row, pl.BlockSpec((8, d), lambda i: (0, 0))]
    out_specs = [row, row]
    out_shape = [jax.ShapeDtypeStruct((s, d), F32), jax.ShapeDtypeStruct((s, d), F32)]
    args = [ycat, w_all, x, vec]
    if last:
        in_specs.append(row)
        args.append(target)
        out_specs.append(pl.BlockSpec((8, 128), lambda i: (0, 0)))
        out_shape.append(jax.ShapeDtypeStruct((8, 128), F32))
    out = pl.pallas_call(
        body, name=f"outproj_fwd_l{layer}", grid=(s // ts,),
        in_specs=in_specs, out_specs=out_specs, out_shape=out_shape,
        compiler_params=_params(1),
    )(*args)
    return (out[0], out[1], out[2]) if last else (out[0], out[1], None)


def _outproj_bwd(dxo, y, ycat, w_all, vec, layer, after):
    s, d = dxo.shape
    nk, kd = w_all.shape[0], w_all.shape[1]
    ts = _tile(s, MATMUL_ROWS)
    nt = s // ts

    def body(dxo_ref, y_ref, ycat_ref, w_ref, vec_ref, after_ref, dycat_ref, dw_ref, dvec_ref, acc):
        del after_ref
        i = pl.program_id(0)

        @pl.when(i == 0)
        def _():
            acc[...] = jnp.zeros_like(acc)
            dvec_ref[...] = jnp.zeros_like(dvec_ref)

        yt = y_ref[...]
        rs = lax.rsqrt(jnp.mean(yt * yt, axis=-1, keepdims=True) + NORM_EPS)
        yhat = yt * rs
        gate, gpost = vec_ref[2:3, :], vec_ref[4:5, :]
        dxo_t = dxo_ref[...]
        dyn = dxo_t * gate
        dvec_ref[0:1, :] += _colsum(dxo_t * (yhat * gpost))
        dvec_ref[1:2, :] += _colsum(dyn * yhat)
        t = dyn * gpost
        dy = (rs * (t - yhat * jnp.mean(t * yhat, axis=-1, keepdims=True))).astype(BF16)
        for k in range(nk):
            cols = slice(k * kd, (k + 1) * kd)
            dycat_ref[:, cols] = lax.dot_general(dy, w_ref[k], NT_DIMS, preferred_element_type=F32)
            acc[k] += lax.dot_general(ycat_ref[:, cols], dy, TN_DIMS, preferred_element_type=F32)

        @pl.when(i == nt - 1)
        def _():
            dw_ref[...] = acc[...].astype(BF16)

    row = pl.BlockSpec((ts, d), lambda i: (i, 0))
    wide = pl.BlockSpec((ts, nk * kd), lambda i: (i, 0))
    return pl.pallas_call(
        body, name=f"outproj_bwd_l{layer}", grid=(nt,),
        in_specs=[row, row, wide,
                  pl.BlockSpec((nk, kd, d), lambda i: (0, 0, 0)),
                  pl.BlockSpec((8, d), lambda i: (0, 0)), ANY],
        out_specs=[wide,
                   pl.BlockSpec((nk, kd, d), lambda i: (0, 0, 0)),
                   pl.BlockSpec((8, d), lambda i: (0, 0))],
        out_shape=[jax.ShapeDtypeStruct((s, nk * kd), F32),
                   jax.ShapeDtypeStruct((nk, kd, d), BF16),
                   jax.ShapeDtypeStruct((8, d), F32)],
        scratch_shapes=[pltpu.VMEM((nk, kd, d), F32)],
        compiler_params=_params(1),
    )(dxo, y, ycat, w_all, vec, after)


def _halo_index(ts, nt):
    return lambda j: jnp.maximum((nt - 1 - j) * (ts // HALO) - 1, 0)


def _rnn_bwd(proj, hs, dycat, cw, vec, wa, wx, layer, after):
    _, s, d = proj.shape
    nh, hd, _ = wa.shape
    ts = _tile(s, BWD_SCAN_ROWS)
    nt = s // ts
    halo = _halo_index(ts, nt)
    hps = BWD_HEADS_PER_STEP
    wl = hps * hd

    def body(proj_ref, xh_ref, hs_ref, hsh_ref, dy_ref, cw_ref, vec_ref, wa_ref, wx_ref, after_ref,
             dproj_ref, dgates_ref, dvec_ref, xbuf, hbuf, dubuf, carry, dw_acc):
        del after_ref
        j = pl.program_id(1)
        first_tile = j == nt - 1

        @pl.when(j == 0)
        def _():
            dubuf[pl.ds(ts, HALO), :] = jnp.zeros((HALO, wl), F32)
            carry[...] = jnp.zeros_like(carry)
            dw_acc[...] = jnp.zeros_like(dw_acc)
            dvec_ref[...] = jnp.zeros_like(dvec_ref)

        xbuf[0:HALO, :] = jnp.where(first_tile, 0.0, xh_ref[0])
        xbuf[pl.ds(HALO, ts), :] = proj_ref[0]
        hbuf[0:HALO, :] = jnp.where(first_tile, 0.0, hsh_ref[...])
        hbuf[pl.ds(HALO, ts), :] = hs_ref[...]

        for hh in range(hps):
            lanes = slice(hh * hd, (hh + 1) * hd)
            wa, wx = wa_ref[hh], wx_ref[hh]
            hs = hs_ref[:, lanes]
            u = _conv(xbuf, cw_ref, vec_ref, lanes, ts)
            ub, r, ig, sp, log_a = _rnn_gates(u, wa, wx, vec_ref, lanes)
            a = jnp.exp(log_a)
            e2 = jnp.exp(2.0 * log_a)
            one_minus_a2 = _neg_expm1(2.0 * log_a)
            inv_mult = lax.rsqrt(one_minus_a2)
            mult = one_minus_a2 * inv_mult

            g = proj_ref[1, :, lanes]
            sg = _sigmoid(g)
            dyc = dy_ref[:, lanes]
            dproj_ref[1, :, lanes] = (dyc * hs * (sg * (1.0 + g * (1.0 - sg)))).astype(BF16)

            row = lax.broadcasted_iota(jnp.int32, (ts, hd), 0)
            dhs = dyc * (g * sg) + jnp.where(row == ts - 1, carry[0:1, lanes], 0.0)
            dh = _scan_rev(_shift_up(a, 1, 0.0), dhs)
            carry[:, lanes] = (a * dh)[0:8, :]

            h_prev = hbuf[pl.ds(HALO - 1, ts), lanes]
            dlog_a = dh * h_prev * a - dh * (ig * u) * (e2 * inv_mult)
            di = dh * mult * u
            dzr = dlog_a * ((-LRU_C) * sp) * (r * (1.0 - r))
            dzi = di * (ig * (1.0 - ig))
            dvec_ref[3:4, lanes] += _colsum(dlog_a * r) * (LRU_C * _sigmoid(-vec_ref[3:4, lanes]))
            dvec_ref[1:2, lanes] += _colsum(dzr)
            dvec_ref[2:3, lanes] += _colsum(dzi)
            dzr_b, dzi_b = dzr.astype(BF16), dzi.astype(BF16)
            dw_acc[0, hh] += lax.dot_general(ub, dzr_b, TN_DIMS, preferred_element_type=F32)
            dw_acc[1, hh] += lax.dot_general(ub, dzi_b, TN_DIMS, preferred_element_type=F32)
            du = (dh * mult * ig
                  + lax.dot_general(dzr_b, wa, NT_DIMS, preferred_element_type=F32)
                  + lax.dot_general(dzi_b, wx, NT_DIMS, preferred_element_type=F32))
            dvec_ref[0:1, lanes] += _colsum(du)
            for k in range(CONV_WIDTH):
                dvec_ref[4 + k:5 + k, lanes] += _colsum(du * xbuf[pl.ds(HALO - (CONV_WIDTH - 1) + k, ts), lanes])

            dubuf[0:ts, lanes] = du
            dx = cw_ref[CONV_WIDTH - 1:CONV_WIDTH, lanes] * du
            for k in range(CONV_WIDTH - 1):
                dx = dx + cw_ref[k:k + 1, lanes] * dubuf[pl.ds(CONV_WIDTH - 1 - k, ts), lanes]
            dproj_ref[0, :, lanes] = dx.astype(BF16)
        dubuf[pl.ds(ts, HALO), :] = dubuf[0:HALO, :]

        @pl.when(first_tile)
        def _():
            dgates_ref[...] = dw_acc[...].astype(BF16)

    rev = lambda h, j: (nt - 1 - j, h)
    return pl.pallas_call(
        body, name=f"rnn_bwd_l{layer}", grid=(nh // hps, nt),
        in_specs=[pl.BlockSpec((2, ts, wl), lambda h, j: (0, nt - 1 - j, h)),
                  pl.BlockSpec((1, HALO, wl), lambda h, j: (0, halo(j), h)),
                  pl.BlockSpec((ts, wl), rev),
                  pl.BlockSpec((HALO, wl), lambda h, j: (halo(j), h)),
                  pl.BlockSpec((ts, wl), rev),
                  pl.BlockSpec((CONV_WIDTH, wl), lambda h, j: (0, h)),
                  pl.BlockSpec((8, wl), lambda h, j: (0, h)),
                  pl.BlockSpec((hps, hd, hd), lambda h, j: (h, 0, 0)),
                  pl.BlockSpec((hps, hd, hd), lambda h, j: (h, 0, 0)), ANY],
        out_specs=[pl.BlockSpec((2, ts, wl), lambda h, j: (0, nt - 1 - j, h)),
                   pl.BlockSpec((2, hps, hd, hd), lambda h, j: (0, h, 0, 0)),
                   pl.BlockSpec((16, wl), lambda h, j: (0, h))],
        out_shape=[jax.ShapeDtypeStruct((4, s, d), BF16),
                   jax.ShapeDtypeStruct((2, nh, hd, hd), BF16),
                   jax.ShapeDtypeStruct((16, d), F32)],
        scratch_shapes=[pltpu.VMEM((ts + HALO, wl), F32), pltpu.VMEM((ts + HALO, wl), F32),
                        pltpu.VMEM((ts + HALO, wl), F32), pltpu.VMEM((8, wl), F32),
                        pltpu.VMEM((2, hps, hd, hd), F32)],
        compiler_params=_params(2),
    )(proj, proj, hs, hs, dycat, cw, vec, wa, wx, after)


def _pool_bwd(proj, dycat, dproj, pw, vec, layer):
    _, s, d = proj.shape
    ng, gd, _ = pw.shape
    ts = _tile(s, MATMUL_ROWS)
    nt = s // ts
    halo = _halo_index(ts, nt)

    def body(proj_ref, xh_ref, dy_ref, dproj_in, pw_ref, vec_ref, dproj_ref, dpw_ref, dvec_ref, xbuf, qbuf, acc):
        del dproj_in
        j = pl.program_id(0)
        i = nt - 1 - j

        @pl.when(j == 0)
        def _():
            qbuf[pl.ds(ts, HALO), :] = jnp.zeros((HALO, d), F32)
            acc[...] = jnp.zeros_like(acc)
            dvec_ref[...] = jnp.zeros_like(dvec_ref)

        xbuf[0:HALO, :] = jnp.where(i == 0, 0.0, xh_ref[0])
        xbuf[pl.ds(HALO, ts), :] = proj_ref[0]
        for g in range(ng):
            lanes = slice(g * gd, (g + 1) * gd)
            win = 2 << g
            xt = proj_ref[0, :, lanes]
            inv_cnt = _inv_count(i, ts, gd, win)
            pooled = _pooled(xbuf, xt, lanes, win, inv_cnt, ts).astype(BF16)
            z = jnp.dot(pooled, pw_ref[g], preferred_element_type=F32) + vec_ref[0:1, lanes]
            scale = vec_ref[1:2, lanes]
            gg = proj_ref[1, :, lanes]
            sg = _sigmoid(gg)
            dyc = dy_ref[:, lanes]
            dyp = dyc * (gg * sg)
            dproj_ref[1, :, lanes] = (dyc * (z * scale) * (sg * (1.0 + gg * (1.0 - sg)))).astype(BF16)
            dvec_ref[1:2, lanes] += _colsum(dyp * z)
            dz = dyp * scale
            dvec_ref[0:1, lanes] += _colsum(dz)
            dz_b = dz.astype(BF16)
            acc[g] += lax.dot_general(pooled, dz_b, TN_DIMS, preferred_element_type=F32)
            dpooled = lax.dot_general(dz_b, pw_ref[g], NT_DIMS, preferred_element_type=F32)

            qbuf[0:ts, lanes] = dpooled * inv_cnt
            dx = _window_sum(qbuf[:, lanes], win, False)[0:ts, :] - dpooled
            dproj_ref[0, :, lanes] = dx.astype(BF16)
        qbuf[pl.ds(ts, HALO), :] = qbuf[0:HALO, :]

        @pl.when(j == nt - 1)
        def _():
            dpw_ref[...] = acc[...].astype(BF16)

    return pl.pallas_call(
        body, name=f"pool_bwd_l{layer}", grid=(nt,),
        in_specs=[pl.BlockSpec((2, ts, d), lambda j: (1, nt - 1 - j, 0)),
                  pl.BlockSpec((1, HALO, d), lambda j: (2, halo(j), 0)),
                  pl.BlockSpec((ts, d), lambda j: (nt - 1 - j, 1)),
                  ANY,
                  pl.BlockSpec((ng, gd, gd), lambda j: (0, 0, 0)),
                  pl.BlockSpec((8, d), lambda j: (0, 0))],
        out_specs=[pl.BlockSpec((2, ts, d), lambda j: (1, nt - 1 - j, 0)),
                   pl.BlockSpec((ng, gd, gd), lambda j: (0, 0, 0)),
                   pl.BlockSpec((8, d), lambda j: (0, 0))],
        out_shape=[jax.ShapeDtypeStruct((4, s, d), BF16),
                   jax.ShapeDtypeStruct((ng, gd, gd), BF16),
                   jax.ShapeDtypeStruct((8, d), F32)],
        input_output_aliases={3: 0},
        scratch_shapes=[pltpu.VMEM((ts + HALO, d), F32), pltpu.VMEM((ts + HALO, d), F32),
                        pltpu.VMEM((ng, gd, gd), F32)],
        compiler_params=_params(1),
    )(proj, proj, dycat, dproj, pw, vec)


def _inproj_bwd_x(dproj, w_all, x, dxo, vec, layer, after):
    s, d = x.shape
    p = w_all.shape[2]
    ts = _tile(s, MATMUL_ROWS)

    def body(dp_ref, w_ref, x_ref, dxo_ref, vec_ref, after_ref, dx_ref, dvec_ref):
        del after_ref

        @pl.when(pl.program_id(0) == 0)
        def _():
            dvec_ref[...] = jnp.zeros_like(dvec_ref)

        dh = lax.dot_general(dp_ref[0], w_ref[0], NT_DIMS, preferred_element_type=F32)
        for k in range(1, 4):
            dh = dh + lax.dot_general(dp_ref[k], w_ref[k], NT_DIMS, preferred_element_type=F32)
        _, xn, rs = _prenorm(x_ref[...], vec_ref)
        gpre, scale1 = vec_ref[3:4, :], 1.0 + vec_ref[1:2, :]
        dvec_ref[0:1, :] += _colsum(dh)
        dvec_ref[1:2, :] += _colsum(dh * (xn * gpre))
        dvec_ref[2:3, :] += _colsum(dh * (xn * scale1))
        t = dh * (gpre * scale1)
        dx_ref[...] = dxo_ref[...] + rs * (t - xn * jnp.mean(t * xn, axis=-1, keepdims=True))

    row = pl.BlockSpec((ts, d), lambda i: (i, 0))
    return pl.pallas_call(
        body, name=f"inproj_bwd_x_l{layer}", grid=(s // ts,),
        in_specs=[pl.BlockSpec((4, ts, p), lambda i: (0, i, 0)),
                  pl.BlockSpec((4, d, p), lambda i: (0, 0, 0)),
                  row, row, pl.BlockSpec((8, d), lambda i: (0, 0)), ANY],
        out_specs=[row, pl.BlockSpec((8, d), lambda i: (0, 0))],
        out_shape=[jax.ShapeDtypeStruct((s, d), F32), jax.ShapeDtypeStruct((8, d), F32)],
        compiler_params=_params(1),
    )(dproj, w_all, x, dxo, vec, after)


def _inproj_bwd_w(dproj, x, vec, layer, after):
    s, d = x.shape
    p = dproj.shape[2]
    ts = _tile(s, MATMUL_ROWS)
    nt = s // ts

    def body(dp_ref, x_ref, vec_ref, after_ref, dw_ref, acc):
        del after_ref
        i = pl.program_id(0)

        @pl.when(i == 0)
        def _():
            acc[...] = jnp.zeros_like(acc)

        h, _, _ = _prenorm(x_ref[...], vec_ref)
        hb = h.astype(BF16)
        for k in range(4):
            acc[k] += lax.dot_general(hb, dp_ref[k], TN_DIMS, preferred_element_type=F32)

        @pl.when(i == nt - 1)
        def _():
            dw_ref[...] = acc[...].astype(BF16)

    return pl.pallas_call(
        body, name=f"inproj_bwd_w_l{layer}", grid=(nt,),
        in_specs=[pl.BlockSpec((4, ts, p), lambda i: (0, i, 0)),
                  pl.BlockSpec((ts, d), lambda i: (i, 0)),
                  pl.BlockSpec((8, d), lambda i: (0, 0)), ANY],
        out_specs=pl.BlockSpec((4, d, p), lambda i: (0, 0, 0)),
        out_shape=jax.ShapeDtypeStruct((4, d, p), BF16),
        scratch_shapes=[pltpu.VMEM((4, d, p), F32)],
        compiler_params=_params(1),
    )(dproj, x, vec, after)


def _sum_slots(stacked, name, out_dtype=F32):
    n, rows, cols = stacked.shape
    tr = _row_tile(rows)

    def body(in_ref, out_ref):
        total = in_ref[0].astype(F32)
        for b in range(1, n):
            total = total + in_ref[b].astype(F32)
        out_ref[...] = total.astype(out_dtype)

    return pl.pallas_call(
        body, name=name, grid=(rows // tr,),
        in_specs=[pl.BlockSpec((n, tr, cols), lambda i: (0, i, 0))],
        out_specs=pl.BlockSpec((tr, cols), lambda i: (i, 0)),
        out_shape=jax.ShapeDtypeStruct((rows, cols), out_dtype),
        compiler_params=_params(1),
    )(stacked)


def _adam_update(w, m, v, g):
    m_new = ADAM_B1 * m + (1.0 - ADAM_B1) * g
    v_new = ADAM_B2 * v + (1.0 - ADAM_B2) * (g * g)
    m_hat = m_new / (1.0 - ADAM_B1 ** ADAM_STEP)
    v_hat = v_new / (1.0 - ADAM_B2 ** ADAM_STEP)
    return (-ADAM_LR) * (m_hat / (jnp.sqrt(v_hat) + ADAM_EPS) + ADAM_WD * w), m_new, v_new


def _adamw_layer(w, m, v, grads, layer, prev, name, grad_row_offset=0):
    nl = w.shape[0]
    cols = w.shape[-1]
    rows = w.size // (nl * cols)
    tr = _row_tile(rows)
    off = layer * (rows // tr)
    g_off = grad_row_offset // tr
    n = len(grads)
    n_prev = 0 if prev is None else 4

    def body(*refs):
        w_ref, m_ref, v_ref = refs[:3]
        g_refs = refs[3:3 + n]
        g_out, d_out, m_out, v_out = refs[3 + n + n_prev:]
        g = g_refs[0][...].astype(F32)
        for r in g_refs[1:]:
            g = g + r[...].astype(F32)
        g_out[...] = g
        d_out[...], m_out[...], v_out[...] = _adam_update(w_ref[...], m_ref[...], v_ref[...], g)

    mine = pl.BlockSpec((tr, cols), lambda i: (off + i, 0))
    args = [a.reshape(nl * rows, cols) for a in (w, m, v)] + [g.reshape(-1, cols) for g in grads]
    outs = pl.pallas_call(
        body, name=name, grid=(rows // tr,),
        in_specs=[mine] * 3 + [pl.BlockSpec((tr, cols), lambda i: (g_off + i, 0))] * n + [ANY] * n_prev,
        out_specs=[mine] * 4,
        out_shape=[jax.ShapeDtypeStruct((nl * rows, cols), F32)] * 4,
        input_output_aliases={3 + n + k: k for k in range(n_prev)},
        compiler_params=_params(1),
    )(*args, *(prev or ()))
    return tuple(outs)


def _into_slot(a, dtype, chip_arr, name, layer=None, after=None):
    rows, cols = a.shape[-2:]
    tr = _row_tile(rows)

    def body(chip_ref, a_ref, *rest):
        del chip_ref
        rest[-1][...] = a_ref[...].astype(dtype)

    if layer is None:
        in_spec = pl.BlockSpec((tr, cols), lambda i, chip: (i, 0))
    else:
        in_spec = pl.BlockSpec((None, tr, cols), lambda i, chip: (layer, i, 0))
    extra = [] if after is None else [after]
    return pl.pallas_call(
        body, name=name,
        grid_spec=pltpu.PrefetchScalarGridSpec(
            num_scalar_prefetch=1, grid=(rows // tr,),
            in_specs=[in_spec] + [ANY] * len(extra),
            out_specs=pl.BlockSpec((None, tr, cols), lambda i, chip: (chip[0], i, 0))),
        out_shape=jax.ShapeDtypeStruct((4, rows, cols), dtype),
        compiler_params=_params(1),
    )(chip_arr, a, *extra)


def _sum_owner(own, land, chip_arr, own_block, own_index, name):
    blk = land.shape[1:]
    tr = _row_tile(blk[-2])
    steps = blk[-2] // tr
    tile = (*blk[:-2], tr, blk[-1])
    lead = (0,) * (len(blk) - 2)

    def body(chip_ref, own_ref, l1, l2, l3, out_ref):
        del chip_ref
        total = (own_ref[...].astype(F32) + l1[...].astype(F32)) + (l2[...].astype(F32) + l3[...].astype(F32))
        out_ref[...] = total.astype(BF16)

    def landed(k):
        return pl.BlockSpec((None, *tile), lambda i, chip: (chip[0] ^ k, *lead, i, 0))

    return pl.pallas_call(
        body, name=name,
        grid_spec=pltpu.PrefetchScalarGridSpec(
            num_scalar_prefetch=1, grid=(steps,),
            in_specs=[pl.BlockSpec(own_block(tr), own_index), landed(1), landed(2), landed(3)],
            out_specs=pl.BlockSpec(tile, lambda i, chip: (*lead, i, 0))),
        out_shape=jax.ShapeDtypeStruct(blk, BF16),
        compiler_params=_params(1),
    )(chip_arr, own, land, land, land)


_WHOLE_VMEM = pltpu.CompilerParams(vmem_limit_bytes=V7X_VMEM_LIMIT_BYTES)


def _pack_vectors(modbuf, ada_b, pre_norm_g, post_norm_g, conv_b, gate_a_b, gate_x_b, lru_lambda):
    nl, d = pre_norm_g.shape
    n = modbuf.shape[2] // nl
    nh, hd = gate_a_b.shape[1], gate_a_b.shape[2]

    def body(mb_ref, ab_ref, pre_ref, post_ref, cb_ref, gab_ref, gxb_ref, lam_ref, *outs):
        for layer in range(nl):
            vec_ref, rvec_ref = outs[layer], outs[nl + layer]
            vec_ref[...] = jnp.zeros_like(vec_ref)
            rvec_ref[...] = jnp.zeros_like(rvec_ref)
            for k in range(4):
                piece = mb_ref[k, 0:1, layer * n:(layer + 1) * n] + ab_ref[layer:layer + 1, k * n:(k + 1) * n]
                lo = k * n
                while lo < (k + 1) * n:
                    row = lo // d
                    hi = min((row + 1) * d, (k + 1) * n)
                    vec_ref[row:row + 1, lo - row * d:hi - row * d] = piece[:, lo - k * n:hi - k * n]
                    lo = hi
            vec_ref[3:4, :] = pre_ref[layer:layer + 1, :]
            vec_ref[4:5, :] = post_ref[layer:layer + 1, :]
            rvec_ref[0:1, :] = cb_ref[layer:layer + 1, :]
            for h in range(nh):
                rvec_ref[1:2, h * hd:(h + 1) * hd] = gab_ref[layer, h:h + 1, :]
                rvec_ref[2:3, h * hd:(h + 1) * hd] = gxb_ref[layer, h:h + 1, :]
            rvec_ref[3:4, :] = lam_ref[layer:layer + 1, :]

    out = pl.pallas_call(
        body, name="pack_vectors", in_specs=[VMEM] * 8, out_specs=[VMEM] * (2 * nl),
        out_shape=[jax.ShapeDtypeStruct((8, d), F32)] * (2 * nl), compiler_params=_WHOLE_VMEM,
    )(modbuf, ada_b, pre_norm_g, post_norm_g, conv_b, gate_a_b, gate_x_b, lru_lambda)
    return list(out[:nl]), list(out[nl:])


def _pack_gathered(convw_g, poolb_g, pws, pool_scale, ng):
    nl, d = pool_scale.shape
    taps = convw_g.shape[1] // nl
    dq = convw_g.shape[2]
    gq, gd = poolb_g.shape[2], pws[0].shape[2]

    def body(cg_ref, pb_ref, *rest):
        pw_refs, ps_ref = rest[:nl], rest[nl]
        outs = rest[nl + 1:]
        for layer in range(nl):
            cw_ref, pvec_ref, pwf_ref = outs[layer], outs[nl + layer], outs[2 * nl + layer]
            pvec_ref[...] = jnp.zeros_like(pvec_ref)
            pvec_ref[1:2, :] = ps_ref[layer:layer + 1, :]
            for k in range(4):
                cw_ref[:, k * dq:(k + 1) * dq] = cg_ref[k, layer * taps:(layer + 1) * taps, :]
                for g in range(ng):
                    lo = g * gd + k * gq
                    pvec_ref[0:1, lo:lo + gq] = pb_ref[k, layer * ng + g:layer * ng + g + 1, :]
                    pwf_ref[g, k * gq:(k + 1) * gq, :] = pw_refs[layer][k, g * gq:(g + 1) * gq, :]

    out = pl.pallas_call(
        body, name="pack_gathered", in_specs=[VMEM] * (3 + nl), out_specs=[VMEM] * (3 * nl),
        out_shape=[jax.ShapeDtypeStruct((taps, d), F32)] * nl + [jax.ShapeDtypeStruct((8, d), F32)] * nl
        + [jax.ShapeDtypeStruct((ng, gd, gd), BF16)] * nl,
        compiler_params=_WHOLE_VMEM,
    )(convw_g, poolb_g, *pws, pool_scale)
    return list(out[:nl]), list(out[nl:2 * nl]), list(out[2 * nl:])


ROW_SHIFT, ROW_SCALE, ROW_PRE, ROW_GATE, ROW_POST = 0, 1, 2, 8, 9
ROW_CONV_B, ROW_GATE_A_B, ROW_GATE_X_B, ROW_LAMBDA, ROW_CONV_W = 16, 17, 18, 19, 20
ROW_POOL_B, ROW_POOL_SCALE, ROW_SQ = 32, 33, 40


def _adamw_small(totals, chip_arr, params):
    nl = len(totals)
    d = totals[0].shape[1]
    n_par = len(params)
    flat = [a for p in params for a in p]
    nh, hd = params[6][0].shape[1], params[6][0].shape[2]
    taps, dq = params[8][0].shape[1], params[8][0].shape[2]
    ng, gq = params[9][0].shape[1], params[9][0].shape[2]
    gd = d // ng

    def body(chip_ref, *refs):
        tot = refs[:nl]
        ins = refs[nl:nl + 3 * n_par]
        outs = refs[nl + 3 * n_par:]
        chip = chip_ref[0]

        def update(p, idx, g):
            delta, m_new, v_new = _adam_update(ins[3 * p][idx], ins[3 * p + 1][idx], ins[3 * p + 2][idx], g)
            outs[4 * p][idx] = g
            outs[4 * p + 1][idx] = delta
            outs[4 * p + 2][idx] = m_new
            outs[4 * p + 3][idx] = v_new

        def mine(candidates):
            g = candidates[0]
            for k in range(1, 4):
                g = jnp.where(chip == k, candidates[k], g)
            return g

        for layer in range(nl):
            t = tot[layer]
            row = (slice(layer, layer + 1), slice(None))
            for j, r in enumerate((ROW_SHIFT, ROW_SCALE, ROW_GATE)):
                update(0, (slice(layer, layer + 1), slice(j * d, (j + 1) * d)), t[r:r + 1, :])
            for p, r in ((1, ROW_PRE), (2, ROW_POST), (3, ROW_CONV_B), (4, ROW_LAMBDA), (5, ROW_POOL_SCALE)):
                update(p, row, t[r:r + 1, :])
            for h in range(nh):
                idx = (layer, slice(h, h + 1), slice(None))
                update(6, idx, t[ROW_GATE_A_B:ROW_GATE_A_B + 1, h * hd:(h + 1) * hd])
                update(7, idx, t[ROW_GATE_X_B:ROW_GATE_X_B + 1, h * hd:(h + 1) * hd])
            for k in range(taps):
                r = ROW_CONV_W + k
                update(8, (layer, slice(k, k + 1), slice(None)), mine([t[r:r + 1, c * dq:(c + 1) * dq] for c in range(4)]))
            for g in range(ng):
                cands = [t[ROW_POOL_B:ROW_POOL_B + 1, g * gd + c * gq:g * gd + (c + 1) * gq] for c in range(4)]
                update(9, (layer, slice(g, g + 1), slice(None)), mine(cands))

    out = pl.pallas_call(
        body, name="adamw_small",
        in_specs=[pl.BlockSpec(memory_space=pltpu.SMEM)] + [VMEM] * (nl + 3 * n_par),
        out_specs=[VMEM] * (4 * n_par),
        out_shape=[jax.ShapeDtypeStruct(p[0].shape, F32) for p in params for _ in range(4)],
        compiler_params=_WHOLE_VMEM,
    )(chip_arr, *totals, *flat)
    return [tuple(out[4 * p:4 * p + 4]) for p in range(n_par)]


def _adamw_ada_w_layer(c_t, slabs, chip_arr, w, m, v, layer, prev, name):
    nl, d, n = w.shape
    nb = c_t.shape[1]
    tr = _row_tile(d)
    off = layer * (d // tr)
    n_prev = 0 if prev is None else 4
    mod_rows = (ROW_SHIFT, ROW_SCALE, ROW_GATE)

    def body(chip_ref, c_ref, slab_ref, w_ref, m_ref, v_ref, *rest):
        g_out, d_out, m_out, v_out = rest[n_prev:n_prev + 4]
        dm = rest[-1]

        @pl.when(pl.program_id(0) == 0)
        def _():
            for k in range(4):
                @pl.when(chip_ref[0] == k)
                def _():
                    lo = k * n
                    while lo < (k + 1) * n:
                        hi = min((lo // d + 1) * d, (k + 1) * n)
                        row = mod_rows[lo // d]
                        for b in range(nb):
                            dm[b:b + 1, lo - k * n:hi - k * n] = slab_ref[b, row:row + 1, lo % d:lo % d + hi - lo]
                        lo = hi

        g = c_ref[:, 0:1] * dm[0:1, :]
        for b in range(1, nb):
            g = g + c_ref[:, b:b + 1] * dm[b:b + 1, :]
        g_out[...] = g
        d_out[...], m_out[...], v_out[...] = _adam_update(w_ref[...], m_ref[...], v_ref[...], g)

    mine = pl.BlockSpec((tr, n), lambda i, chip: (off + i, 0))
    outs = pl.pallas_call(
        body, name=name,
        grid_spec=pltpu.PrefetchScalarGridSpec(
            num_scalar_prefetch=1, grid=(d // tr,),
            in_specs=[pl.BlockSpec((tr, nb), lambda i, chip: (i, 0)),
                      pl.BlockSpec(slabs.shape, lambda i, chip: (0, 0, 0))] + [mine] * 3 + [ANY] * n_prev,
            out_specs=[mine] * 4,
            scratch_shapes=[pltpu.VMEM((nb, n), F32)]),
        out_shape=[jax.ShapeDtypeStruct((nl * d, n), F32)] * 4,
        input_output_aliases={6 + k: k for k in range(n_prev)},
        compiler_params=_params(1),
    )(chip_arr, c_t, slabs, *[a.reshape(nl * d, n) for a in (w, m, v)], *(prev or ()))
    return tuple(outs)


def _place():
    x, y, c = lax.axis_index("x"), lax.axis_index("y"), lax.axis_index("c")
    return x, y, c


OTHER_CHIPS = ((1, 0), (0, 1), (1, 1))
OTHER_DEVICES = tuple((fx, fy, fc) for fx in (0, 1) for fy in (0, 1) for fc in (0, 1))[1:]


def _mod_exchange(c_row, ada_w):
    nl, d, n = ada_w.shape

    def body(c_ref, w_ref, cbuf, modbuf, token, cblk, mres, send_a, recv_a, send_c, recv_c):
        token[...] = jnp.zeros_like(token)
        x, y, c = _place()
        me = 4 * x + 2 * y + c
        chip = 2 * x + y
        cv = c_ref[...]
        cblk[...] = jnp.zeros_like(cblk)
        cblk[0:1, :] = cv * _sigmoid(cv)

        def rows_of(dev):
            return cbuf.at[pl.ds(pl.multiple_of(8 * dev, 8), 8), :]

        cbuf[pl.ds(pl.multiple_of(8 * me, 8), 8), :] = cblk[...]
        sends = []
        for j, (fx, fy, fc) in enumerate(OTHER_DEVICES):
            cp = pltpu.make_async_remote_copy(
                src_ref=cblk, dst_ref=rows_of(me), send_sem=send_a.at[j], recv_sem=recv_a.at[j],
                device_id=(x ^ fx, y ^ fy, c ^ fc), device_id_type=MESH)
            cp.start()
            sends.append(cp)
        for j, (fx, fy, fc) in enumerate(OTHER_DEVICES):
            peer = 4 * (x ^ fx) + 2 * (y ^ fy) + (c ^ fc)
            pltpu.make_async_remote_copy(
                src_ref=cblk, dst_ref=rows_of(peer), send_sem=send_a.at[j], recv_sem=recv_a.at[j],
                device_id=(x ^ fx, y ^ fy, c ^ fc), device_id_type=MESH).wait_recv()
        for cp in sends:
            cp.wait_send()

        call = cbuf[...]
        for layer in range(nl):
            mres[:, layer * n:(layer + 1) * n] = jnp.dot(
                call, w_ref[layer], preferred_element_type=F32, precision=lax.Precision.HIGHEST)

        def block_of(dev):
            return mres.at[pl.ds(pl.multiple_of(8 * dev, 8), 8), :]

        modbuf[chip] = mres[pl.ds(pl.multiple_of(8 * me, 8), 8), :]
        sends = []
        for j, (fx, fy) in enumerate(OTHER_CHIPS):
            peer = 4 * (x ^ fx) + 2 * (y ^ fy) + c
            cp = pltpu.make_async_remote_copy(
                src_ref=block_of(peer), dst_ref=modbuf.at[chip], send_sem=send_c.at[j], recv_sem=recv_c.at[j],
                device_id=(x ^ fx, y ^ fy, c), device_id_type=MESH)
            cp.start()
            sends.append(cp)
        for j, (fx, fy) in enumerate(OTHER_CHIPS):
            pltpu.make_async_remote_copy(
                src_ref=block_of(me), dst_ref=modbuf.at[2 * (x ^ fx) + (y ^ fy)],
                send_sem=send_c.at[j], recv_sem=recv_c.at[j],
                device_id=(x ^ fx, y ^ fy, c), device_id_type=MESH).wait_recv()
        for cp in sends:
            cp.wait_send()

    return pl.pallas_call(
        body, name="mod_exchange", in_specs=[VMEM, VMEM], out_specs=[VMEM, VMEM, VMEM],
        out_shape=[jax.ShapeDtypeStruct((64, d), F32), jax.ShapeDtypeStruct((4, 8, nl * n), F32),
                   jax.ShapeDtypeStruct((8, 128), F32)],
        scratch_shapes=[pltpu.VMEM((8, d), F32), pltpu.VMEM((64, nl * n), F32),
                        pltpu.SemaphoreType.DMA((7,)), pltpu.SemaphoreType.DMA((7,)),
                        pltpu.SemaphoreType.DMA((3,)), pltpu.SemaphoreType.DMA((3,))],
        compiler_params=pltpu.CompilerParams(vmem_limit_bytes=V7X_VMEM_LIMIT_BYTES, has_side_effects=True),
    )(c_row, ada_w)


def _in_hbm(a):
    return pltpu.with_memory_space_constraint(a, pltpu.HBM)


def _gather_copies(lands, split, over_ici):
    x, y, c = _place()
    chip = 2 * x + y
    out = []
    for t, land in enumerate(lands):
        half = land.shape[1] // 2
        mine = pl.ds(pl.multiple_of(c * half, half), half)
        theirs = pl.ds(pl.multiple_of((1 - c) * half, half), half)
        for j, (fx, fy) in enumerate(OTHER_CHIPS):
            them = 2 * (x ^ fx) + (y ^ fy)
            if over_ici and split[t]:
                out.append((land.at[chip, mine], land.at[chip, mine], land.at[them, mine], (x ^ fx, y ^ fy, c), 3 * t + j))
            elif over_ici:
                out.append((land.at[chip], land.at[chip], land.at[them], (x ^ fx, y ^ fy, c), 3 * t + j))
            elif split[t]:
                out.append((land.at[them, mine], land.at[them, mine], land.at[them, theirs], (x, y, 1 - c), 3 * t + j))
    return out


def _gather_start(lands, groups, split):
    n, ngr = len(lands), len(groups)

    def body(*refs):
        sems = refs[n:n + 2 * ngr]
        for gi, idxs in enumerate(groups):
            for src, dst, _, peer, k in _gather_copies([refs[i] for i in idxs], [split[i] for i in idxs], True):
                pltpu.make_async_remote_copy(src_ref=src, dst_ref=dst, send_sem=sems[2 * gi].at[k],
                                             recv_sem=sems[2 * gi + 1].at[k], device_id=peer, device_id_type=MESH).start()
        refs[-1][...] = jnp.zeros_like(refs[-1])

    sem_shapes = []
    for idxs in groups:
        sem_shapes += [pltpu.SemaphoreType.DMA((3 * len(idxs),))] * 2
    out = pl.pallas_call(
        body, name="weight_gather_start",
        in_specs=[HBM] * n, out_specs=[SEM] * (2 * ngr) + [HBM] * n + [VMEM],
        out_shape=sem_shapes + [pltpu.HBM(a.shape, a.dtype) for a in lands] + [jax.ShapeDtypeStruct((8, 128), F32)],
        input_output_aliases={i: 2 * ngr + i for i in range(n)},
        compiler_params=pltpu.CompilerParams(has_side_effects=DATAFLOW_EFFECT),
    )(*[_in_hbm(a) for a in lands])
    sems = [(out[2 * gi], out[2 * gi + 1]) for gi in range(ngr)]
    return sems, list(out[2 * ngr:2 * ngr + n]), out[-1]


def _gather_forward(lands, split, sems, after, name):
    n = len(lands)

    def body(*refs):
        ici_send, ici_recv = refs[n], refs[n + 1]
        fwd_send, fwd_recv = refs[n + 3], refs[n + 4]
        forwards = {k: (src, dst, peer) for src, dst, _, peer, k in _gather_copies(refs[:n], split, False)}
        for src, _, landed, peer, k in _gather_copies(refs[:n], split, True):
            cp = pltpu.make_async_remote_copy(src_ref=src, dst_ref=landed, send_sem=ici_send.at[k], recv_sem=ici_recv.at[k],
                                              device_id=peer, device_id_type=MESH)
            cp.wait_recv()
            if k in forwards:
                fsrc, fdst, fpeer = forwards[k]
                pltpu.make_async_remote_copy(src_ref=fsrc, dst_ref=fdst, send_sem=fwd_send.at[k], recv_sem=fwd_recv.at[k],
                                             device_id=fpeer, device_id_type=MESH).start()
            cp.wait_send()

    out = pl.pallas_call(
        body, name=name,
        in_specs=[HBM] * n + [SEM, SEM, ANY], out_specs=[SEM, SEM] + [HBM] * n,
        out_shape=[pltpu.SemaphoreType.DMA((3 * n,))] * 2 + [pltpu.HBM(a.shape, a.dtype) for a in lands],
        input_output_aliases={i: 2 + i for i in range(n)},
        compiler_params=pltpu.CompilerParams(has_side_effects=DATAFLOW_EFFECT),
    )(*lands, sems[0], sems[1], after)
    return (out[0], out[1]), list(out[2:])


def _gather_wait(lands, split, sems, name):
    n = len(lands)

    def body(*refs):
        send_sems, recv_sems = refs[n], refs[n + 1]
        for src, _, landed, peer, k in _gather_copies(refs[:n], split, False):
            cp = pltpu.make_async_remote_copy(src_ref=src, dst_ref=landed, send_sem=send_sems.at[k], recv_sem=recv_sems.at[k],
                                              device_id=peer, device_id_type=MESH)
            cp.wait_send()
            cp.wait_recv()

    out = pl.pallas_call(
        body, name=name,
        in_specs=[HBM] * n + [SEM, SEM], out_specs=[HBM] * n,
        out_shape=[pltpu.HBM(a.shape, a.dtype) for a in lands],
        input_output_aliases={i: i for i in range(n)},
        compiler_params=pltpu.CompilerParams(has_side_effects=DATAFLOW_EFFECT),
    )(*lands, sems[0], sems[1])
    return list(out)


def _to_owner_copies(pairs, q):
    x, y, c = _place()
    chip = 2 * x + y
    out = []
    for t, (part, land) in enumerate(pairs):
        for j, (fx, fy) in enumerate(OTHER_CHIPS):
            owner = 2 * (x ^ fx) + (y ^ fy)
            if part.shape[0] == 4 and part.shape[1:] == land.shape[1:]:
                src = part.at[owner]
            else:
                src = part.at[:, pl.ds(pl.multiple_of(owner * q, q), q), :]
            out.append((src, land.at[chip], land.at[owner], (x ^ fx, y ^ fy, c), 3 * t + j))
    return out


def _to_all_copies(bufs, first_sem):
    x, y, c = _place()
    me = 4 * x + 2 * y + c
    out = []
    for t, buf in enumerate(bufs):
        for j, (fx, fy, fc) in enumerate(OTHER_DEVICES):
            them = 4 * (x ^ fx) + 2 * (y ^ fy) + (c ^ fc)
            out.append((buf.at[me], buf.at[me], buf.at[them], (x ^ fx, y ^ fy, c ^ fc), first_sem + 7 * t + j))
    return out


def _to_chips_copies(bufs, first_sem):
    x, y, c = _place()
    chip = 2 * x + y
    out = []
    for t, buf in enumerate(bufs):
        for j, (fx, fy) in enumerate(OTHER_CHIPS):
            them = 2 * (x ^ fx) + (y ^ fy)
            out.append((buf.at[chip], buf.at[chip], buf.at[them], (x ^ fx, y ^ fy, c), first_sem + 3 * t + j))
    return out


def _exchange_copies(refs, kinds, q):
    n_owner, n_chips = kinds
    pairs = list(zip(refs[:n_owner], refs[n_owner:2 * n_owner]))
    first_all = 3 * (n_owner + n_chips)
    return (_to_owner_copies(pairs, q) + _to_chips_copies(refs[2 * n_owner:2 * n_owner + n_chips], 3 * n_owner)
            + _to_all_copies(refs[2 * n_owner + n_chips:], first_all))


def _exchange_start(arrays, kinds, q, name):
    n = len(arrays)
    n_sems = 3 * (kinds[0] + kinds[1]) + 7 * (n - 2 * kinds[0] - kinds[1])

    def body(*refs):
        send_sems, recv_sems = refs[n], refs[n + 1]
        for src, dst, _, peer, k in _exchange_copies(refs[:n], kinds, q):
            pltpu.make_async_remote_copy(src_ref=src, dst_ref=dst, send_sem=send_sems.at[k], recv_sem=recv_sems.at[k],
                                         device_id=peer, device_id_type=MESH).start()
        refs[-1][...] = jnp.zeros_like(refs[-1])

    out = pl.pallas_call(
        body, name=name,
        in_specs=[HBM] * n, out_specs=[SEM, SEM] + [HBM] * n + [VMEM],
        out_shape=[pltpu.SemaphoreType.DMA((n_sems,))] * 2 + [pltpu.HBM(a.shape, a.dtype) for a in arrays]
        + [jax.ShapeDtypeStruct((8, 128), F32)],
        input_output_aliases={i: 2 + i for i in range(n)},
        compiler_params=pltpu.CompilerParams(has_side_effects=DATAFLOW_EFFECT),
    )(*[_in_hbm(a) for a in arrays])
    return (out[0], out[1]), list(out[2:2 + n]), out[-1]


def _exchange_wait(arrays, sems, kinds, q, after, name):
    n = len(arrays)

    def body(*refs):
        send_sems, recv_sems = refs[n], refs[n + 1]
        for src, _, landed, peer, k in _exchange_copies(refs[:n], kinds, q):
            cp = pltpu.make_async_remote_copy(src_ref=src, dst_ref=landed, send_sem=send_sems.at[k], recv_sem=recv_sems.at[k],
                                              device_id=peer, device_id_type=MESH)
            cp.wait_send()
            cp.wait_recv()

    out = pl.pallas_call(
        body, name=name,
        in_specs=[HBM] * n + [SEM, SEM, ANY], out_specs=[HBM] * n,
        out_shape=[pltpu.HBM(a.shape, a.dtype) for a in arrays],
        input_output_aliases={i: i for i in range(n)},
        compiler_params=pltpu.CompilerParams(has_side_effects=DATAFLOW_EFFECT),
    )(*arrays, sems[0], sems[1], after)
    return list(out)


def _sibling_swap(parts, layer):
    n = len(parts)

    def body(*refs):
        srcs, outs = refs[:n], refs[n:2 * n]
        send_sems, recv_sems = refs[2 * n:]
        x, y, c = _place()
        cps = [pltpu.make_async_remote_copy(
            src_ref=srcs[i], dst_ref=outs[i], send_sem=send_sems.at[i], recv_sem=recv_sems.at[i],
            device_id=(x, y, 1 - c), device_id_type=MESH) for i in range(n)]
        for cp in cps:
            cp.start()
        for cp in cps:
            cp.wait()

    return pl.pallas_call(
        body, name=f"sibling_swap_l{layer}", in_specs=[ANY] * n, out_specs=[ANY] * n,
        out_shape=[jax.ShapeDtypeStruct(a.shape, a.dtype) for a in parts],
        scratch_shapes=[pltpu.SemaphoreType.DMA((n,)), pltpu.SemaphoreType.DMA((n,))],
        compiler_params=pltpu.CompilerParams(has_side_effects=True),
    )(*parts)


def kernel(x, c, ada_w, ada_b, pre_norm_g, w_in, conv_w, conv_b, gate_a_w, gate_a_b, gate_x_w, gate_x_b, lru_lambda, pool_w, pool_b, pool_scale, w_out, post_norm_g, loss_target, m_ada_w, m_ada_b, m_pre_norm_g, m_w_in, m_conv_w, m_conv_b, m_gate_a_w, m_gate_a_b, m_gate_x_w, m_gate_x_b, m_lru_lambda, m_pool_w, m_pool_b, m_pool_scale, m_w_out, m_post_norm_g, v_ada_w, v_ada_b, v_pre_norm_g, v_w_in, v_conv_w, v_conv_b, v_gate_a_w, v_gate_a_b, v_gate_x_w, v_gate_x_b, v_lru_lambda, v_pool_w, v_pool_b, v_pool_scale, v_w_out, v_post_norm_g):
    nl, d, _ = ada_w.shape
    s = x.shape[1]
    nh, hd = gate_a_w.shape[1], gate_a_w.shape[2]
    ng, gq, gd = pool_w.shape[1], pool_w.shape[2], pool_w.shape[3]
    me = 4 * lax.axis_index("x") + 2 * lax.axis_index("y") + lax.axis_index("c")
    chip = 2 * lax.axis_index("x") + lax.axis_index("y")
    chip_arr = jnp.reshape(chip, (1,)).astype(jnp.int32)
    x0 = x.reshape(s, d)
    target = loss_target.reshape(s, d)
    p_in = w_in.shape[2]
    r_out = w_out.shape[1]

    cbuf, modbuf, mod_token = _mod_exchange(c.reshape(1, d), ada_w)
    vecs, rvecs = _pack_vectors(modbuf, ada_b, pre_norm_g, post_norm_g, conv_b, gate_a_b, gate_x_b, lru_lambda)

    win = [_into_slot(w_in, BF16, chip_arr, f"slot_w_in_l{l}", l) for l in range(nl)]
    wout = [_into_slot(w_out, BF16, chip_arr, f"slot_w_out_l{l}", l) for l in range(nl)]
    pw = [_into_slot(pool_w.reshape(nl, ng * gq, gd), BF16, chip_arr, f"slot_pool_w_l{l}", l) for l in range(nl)]
    convw = _into_slot(conv_w.reshape(nl * CONV_WIDTH, d // 4), F32, chip_arr, "slot_conv_w", after=mod_token)
    poolb = _into_slot(pool_b.reshape(nl * ng, gq), F32, chip_arr, "slot_pool_b")
    lands = [win[0], convw, poolb, *pw, wout[0]]
    split = [True, False, False] + [True] * (nl + 1)
    groups = [[0], list(range(1, len(lands)))]
    for l in range(1, nl):
        groups.append([len(lands), len(lands) + 1])
        lands += [win[l], wout[l]]
        split += [True, True]
    sems, lands, token = _gather_start(lands, groups, split)
    wa_b, wx_b = gate_a_w.astype(BF16), gate_x_w.astype(BF16)

    def gathered(gi, after, tag):
        idxs = groups[gi]
        arrays, halves = [lands[i] for i in idxs], [split[i] for i in idxs]
        between, arrays = _gather_forward(arrays, halves, sems[gi], after, f"weight_gather_forward_{tag}")
        return _gather_wait(arrays, halves, between, f"weight_gather_wait_{tag}")

    xs, projs, hss, ycats, ys = [x0], [], [], [], []
    sq = None
    convw_full = poolw_full = pvecs = None
    for l in range(nl):
        if l == 0:
            (win[0],) = gathered(0, modbuf, "a")
        proj = _inproj_fwd(xs[l], vecs[l], win[l], l)
        if l == 0:
            got = gathered(1, proj, "b")
            wout[0] = got[2 + nl]
            convw_full, pvecs, poolw_full = _pack_gathered(got[0], got[1], got[2:2 + nl], pool_scale, ng)
        ycat, hs = _rnn_fwd(proj, convw_full[l], rvecs[l], wa_b[l], wx_b[l], l)
        if l + 1 < nl:
            win[l + 1], wout[l + 1] = gathered(2 + l, hs, f"c{l + 1}")
        ycat = _pool_fwd(proj, ycat, poolw_full[l], pvecs[l], l)
        y, xo, sq = _outproj_fwd(ycat, wout[l], xs[l], vecs[l], target if l == nl - 1 else None, l)
        projs.append(proj), hss.append(hs), ycats.append(ycat), ys.append(y), xs.append(xo)

    c_all_t = cbuf.reshape(8, 8, d)[:, 0, :].T

    def finish(l, flights, after, prev):
        (sems_a, arr_a), (sems_g, arr_g), (sems_b, arr_b), (sems_c, arr_c) = flights
        dwout_l, rwout = _exchange_wait(arr_a, sems_a, (1, 0), gq, after, f"grad_wait_a_l{l}")
        dpw_l, rpw, gates = _exchange_wait(arr_g, sems_g, (1, 1), gq, rwout, f"grad_wait_g_l{l}")
        dwin_l, rwin = _exchange_wait(arr_b, sems_b, (1, 0), gq, gates, f"grad_wait_b_l{l}")
        (slabs,) = _exchange_wait(arr_c, sems_c, (0, 0), gq, rwin, f"grad_wait_c_l{l}")
        p_win = _sum_owner(dwin_l, rwin, chip_arr, lambda tr: (None, tr, p_in),
                           lambda i, chip: (chip[0], i, 0), f"sum_w_in_l{l}")
        p_wout = _sum_owner(dwout_l, rwout, chip_arr, lambda tr: (None, tr, d),
                            lambda i, chip: (chip[0], i, 0), f"sum_w_out_l{l}")
        p_pw = _sum_owner(dpw_l, rpw, chip_arr, lambda tr: (ng, tr, gd),
                          lambda i, chip: (0, chip[0], 0), f"sum_pool_w_l{l}")
        p_gates = _sum_slots(gates.reshape(4, 2 * nh * hd, hd), f"sum_gates_l{l}", BF16)
        q_win, q_wout, q_pw, q_gates = _sibling_swap([p_win, p_wout, p_pw, p_gates], l)
        prev = prev or {}
        big = {
            "w_in": _adamw_layer(w_in, m_w_in, v_w_in, [p_win, q_win], l, prev.get("w_in"), f"adamw_w_in_l{l}"),
            "w_out": _adamw_layer(w_out, m_w_out, v_w_out, [p_wout, q_wout], l, prev.get("w_out"), f"adamw_w_out_l{l}"),
            "pool_w": _adamw_layer(pool_w, m_pool_w, v_pool_w, [p_pw, q_pw], l, prev.get("pool_w"), f"adamw_pool_w_l{l}"),
            "gate_a_w": _adamw_layer(gate_a_w, m_gate_a_w, v_gate_a_w, [p_gates, q_gates], l, prev.get("gate_a_w"),
                                     f"adamw_gate_a_w_l{l}"),
            "gate_x_w": _adamw_layer(gate_x_w, m_gate_x_w, v_gate_x_w, [p_gates, q_gates], l, prev.get("gate_x_w"),
                                     f"adamw_gate_x_w_l{l}", grad_row_offset=nh * hd),
            "ada_w": _adamw_ada_w_layer(c_all_t, slabs, chip_arr, ada_w, m_ada_w, v_ada_w, l, prev.get("ada_w"),
                                        f"adamw_ada_w_l{l}"),
        }
        return big, _sum_slots(slabs, f"sum_slab_l{l}")

    dx = xs[nl]
    flights = token = big = None
    totals = [None] * nl
    for l in reversed(range(nl)):
        vec_l = vecs[l]
        dycat, dwout_l, dvec_o = _outproj_bwd(dx, ys[l], ycats[l], wout[l], vec_l, l, vec_l if token is None else token)
        sems_a, arr_a, tok_a = _exchange_start([dwout_l, lax.empty(dwout_l.shape, BF16)], (1, 0), gq, f"grad_start_a_l{l}")
        dproj, dgates, dvec_r = _rnn_bwd(projs[l], hss[l], dycat, convw_full[l], rvecs[l], wa_b[l], wx_b[l], l, tok_a)
        dproj, dpw_l, dvec_p = _pool_bwd(projs[l], dycat, dproj, poolw_full[l], pvecs[l], l)
        gates4 = lax.dynamic_update_slice(lax.empty((4, *dgates.shape), BF16), dgates[None], (chip, 0, 0, 0, 0))
        sems_g, arr_g, tok_g = _exchange_start([dpw_l, lax.empty((4, ng, gq, gd), BF16), gates4], (1, 1), gq,
                                               f"grad_start_g_l{l}")
        dwin_l = _inproj_bwd_w(dproj, xs[l], vec_l, l, tok_g)
        sems_b, arr_b, tok_b = _exchange_start([dwin_l, lax.empty(dwin_l.shape, BF16)], (1, 0), gq, f"grad_start_b_l{l}")
        dx, dvec_i = _inproj_bwd_x(dproj, win[l], xs[l], dx, vec_l, l, tok_b)
        parts = [dvec_i, dvec_o, dvec_r, dvec_p]
        if l == nl - 1:
            parts.append(jnp.tile(sq, (1, d // sq.shape[1])))
        slab = jnp.concatenate(parts, axis=0)
        slabs = lax.dynamic_update_slice(lax.empty((8, *slab.shape), F32), slab[None], (me, 0, 0))
        sems_c, arr_c, token = _exchange_start([slabs], (0, 0), gq, f"grad_start_c_l{l}")
        if flights is not None:
            big, totals[l + 1] = finish(l + 1, flights, token, big)
        flights = ((sems_a, arr_a), (sems_g, arr_g), (sems_b, arr_b), (sems_c, arr_c))
    big, totals[0] = finish(0, flights, big["w_in"][3] if big else dx, big)
    grad_x = dx.reshape(x.shape)
    loss = totals[nl - 1][ROW_SQ, 0] * (0.5 / d)

    small = _adamw_small(totals, chip_arr, [
        (ada_b, m_ada_b, v_ada_b), (pre_norm_g, m_pre_norm_g, v_pre_norm_g), (post_norm_g, m_post_norm_g, v_post_norm_g),
        (conv_b, m_conv_b, v_conv_b), (lru_lambda, m_lru_lambda, v_lru_lambda), (pool_scale, m_pool_scale, v_pool_scale),
        (gate_a_b, m_gate_a_b, v_gate_a_b), (gate_x_b, m_gate_x_b, v_gate_x_b),
        (conv_w, m_conv_w, v_conv_w), (pool_b, m_pool_b, v_pool_b)])

    results = {
        "ada_w": tuple(o.reshape(ada_w.shape) for o in big["ada_w"]),
        "ada_b": small[0],
        "pre_norm_g": small[1],
        "w_in": tuple(o.reshape(w_in.shape) for o in big["w_in"]),
        "conv_w": small[8],
        "conv_b": small[3],
        "gate_a_w": tuple(o.reshape(gate_a_w.shape) for o in big["gate_a_w"]),
        "gate_a_b": small[6],
        "gate_x_w": tuple(o.reshape(gate_x_w.shape) for o in big["gate_x_w"]),
        "gate_x_b": small[7],
        "lru_lambda": small[4],
        "pool_w": tuple(o.reshape(pool_w.shape) for o in big["pool_w"]),
        "pool_b": small[9],
        "pool_scale": small[5],
        "w_out": tuple(o.reshape(w_out.shape) for o in big["w_out"]),
        "post_norm_g": small[2],
    }
    names = list(results)
    return (loss, grad_x,
            *[results[n][0] for n in names], *[results[n][1] for n in names],
            *[results[n][2] for n in names], *[results[n][3] for n in names])
```

```python
import functools

import jax
import jax.numpy as jnp
from jax import lax
from jax.experimental import pallas as pl
from jax.experimental.pallas import tpu as pltpu

F32 = jnp.float32
BF16 = jnp.bfloat16

NORM_EPS = 1e-6
LRU_C = 8.0
CONV_WIDTH = 4
MAX_POOL_WINDOW = 16
HALO = 16
ADAM_LR = 0.001
ADAM_B1 = 0.9
ADAM_B2 = 0.999
ADAM_EPS = 1e-08
ADAM_WD = 0.01
ADAM_STEP = 10

V7X_VMEM_LIMIT_BYTES = 56 * 1024 * 1024
MATMUL_ROWS = 512
SCAN_ROWS = 512
BWD_SCAN_ROWS = 1024
ELEMENTWISE_ROWS = 512

MESH = pl.DeviceIdType.MESH
ANY = pl.BlockSpec(memory_space=pl.ANY)
VMEM = pl.BlockSpec(memory_space=pltpu.VMEM)
HBM = pl.BlockSpec(memory_space=pltpu.HBM)
SEM = pl.BlockSpec(memory_space=pltpu.SEMAPHORE)
DATAFLOW_EFFECT = pltpu.SideEffectType.DATAFLOW_SIDE_EFFECTING

NT_DIMS = (((1,), (1,)), ((), ()))
TN_DIMS = (((0,), (0,)), ((), ()))


def _params(n_grid_axes):
    return pltpu.CompilerParams(dimension_semantics=("arbitrary",) * n_grid_axes,
                                vmem_limit_bytes=V7X_VMEM_LIMIT_BYTES)


def _tile(total, want):
    t = min(want, max(total // 2, HALO))
    assert total % t == 0 and t % HALO == 0, (total, t)
    return t


def _row_tile(rows):
    for t in range(min(rows, ELEMENTWISE_ROWS) // 8 * 8, 0, -8):
        if rows % t == 0:
            return t
    return rows


def _sigmoid(z):
    return 1.0 / (1.0 + jnp.exp(-z))


def _softplus(z):
    return jnp.maximum(z, 0.0) + jnp.log(1.0 + jnp.exp(-jnp.abs(z)))


def _neg_expm1(z):
    return -jnp.tanh(0.5 * z) * (jnp.exp(z) + 1.0)


def _colsum(v):
    return jnp.sum(v, axis=0, keepdims=True)


def _prenorm(xt, vec_ref):
    rs = lax.rsqrt(jnp.mean(xt * xt, axis=-1, keepdims=True) + NORM_EPS)
    xn = xt * rs
    h = xn * vec_ref[3:4, :] * (1.0 + vec_ref[1:2, :]) + vec_ref[0:1, :]
    return h, xn, rs


def _shift_down(v, d, fill):
    t = v.shape[0]
    if d % 8 == 0:
        return jnp.concatenate([jnp.full((d, v.shape[1]), fill, v.dtype), v[:t - d]], axis=0)
    row = lax.broadcasted_iota(jnp.int32, v.shape, 0)
    return jnp.where(row >= d, pltpu.roll(v, d, 0), fill)


def _shift_up(v, d, fill):
    t = v.shape[0]
    if d % 8 == 0:
        return jnp.concatenate([v[d:], jnp.full((d, v.shape[1]), fill, v.dtype)], axis=0)
    row = lax.broadcasted_iota(jnp.int32, v.shape, 0)
    return jnp.where(row < t - d, pltpu.roll(v, t - d, 0), fill)


def _scan_fwd(a, v, h_before):
    d = 1
    while d < a.shape[0]:
        v = v + a * _shift_down(v, d, 0.0)
        a = a * _shift_down(a, d, 1.0)
        d *= 2
    return a * h_before + v


def _scan_rev(b, v):
    d = 1
    while d < b.shape[0]:
        v = v + b * _shift_up(v, d, 0.0)
        b = b * _shift_up(b, d, 0.0)
        d *= 2
    return v


def _inproj_fwd(x, vec, w_all, layer):
    s, d = x.shape
    p = w_all.shape[2]
    ts = _tile(s, MATMUL_ROWS)

    def body(x_ref, vec_ref, w_ref, proj_ref):
        h, _, _ = _prenorm(x_ref[...], vec_ref)
        hb = h.astype(BF16)
        for k in range(4):
            proj_ref[k] = jnp.dot(hb, w_ref[k], preferred_element_type=F32)

    return pl.pallas_call(
        body, name=f"inproj_fwd_l{layer}", grid=(s // ts,),
        in_specs=[pl.BlockSpec((ts, d), lambda i: (i, 0)),
                  pl.BlockSpec((8, d), lambda i: (0, 0)),
                  pl.BlockSpec((4, d, p), lambda i: (0, 0, 0))],
        out_specs=pl.BlockSpec((4, ts, p), lambda i: (0, i, 0)),
        out_shape=jax.ShapeDtypeStruct((4, s, p), F32),
        compiler_params=_params(1),
    )(x, vec, w_all)


HEADS_PER_STEP = 4
BWD_HEADS_PER_STEP = 1


def _rnn_gates(u, wa, wx, vec_ref, lanes):
    ub = u.astype(BF16)
    r = _sigmoid(jnp.dot(ub, wa, preferred_element_type=F32) + vec_ref[1:2, lanes])
    ig = _sigmoid(jnp.dot(ub, wx, preferred_element_type=F32) + vec_ref[2:3, lanes])
    sp = _softplus(-vec_ref[3:4, lanes])
    log_a = (-LRU_C) * r * sp
    return ub, r, ig, sp, log_a


def _conv(xbuf, cw_ref, vec_ref, lanes, ts):
    u = vec_ref[0:1, lanes] + cw_ref[CONV_WIDTH - 1:CONV_WIDTH, lanes] * xbuf[pl.ds(HALO, ts), lanes]
    for k in range(CONV_WIDTH - 1):
        u = u + cw_ref[k:k + 1, lanes] * xbuf[pl.ds(HALO - (CONV_WIDTH - 1) + k, ts), lanes]
    return u


def _rnn_fwd(proj, cw, vec, wa, wx, layer):
    _, s, d = proj.shape
    nh, hd, _ = wa.shape
    ts = _tile(s, SCAN_ROWS)
    hps = HEADS_PER_STEP
    wl = hps * hd

    def body(proj_ref, cw_ref, vec_ref, wa_ref, wx_ref, ycat_ref, hs_ref, xbuf, hlast):
        i = pl.program_id(1)

        @pl.when(i == 0)
        def _():
            xbuf[0:HALO, :] = jnp.zeros((HALO, wl), F32)
            hlast[...] = jnp.zeros_like(hlast)

        xbuf[pl.ds(HALO, ts), :] = proj_ref[0]
        for hh in range(hps):
            lanes = slice(hh * hd, (hh + 1) * hd)
            u = _conv(xbuf, cw_ref, vec_ref, lanes, ts)
            _, _, ig, _, log_a = _rnn_gates(u, wa_ref[hh], wx_ref[hh], vec_ref, lanes)
            a = jnp.exp(log_a)
            mult = jnp.sqrt(_neg_expm1(2.0 * log_a))
            hs = _scan_fwd(a, mult * (ig * u), hlast[0:1, lanes])
            hs_ref[:, lanes] = hs
            hlast[0:1, lanes] = hs_ref[ts - 1:ts, lanes]
            g = proj_ref[1, :, lanes]
            ycat_ref[:, lanes] = (hs * (g * _sigmoid(g))).astype(BF16)
        xbuf[0:HALO, :] = xbuf[pl.ds(ts, HALO), :]

    return pl.pallas_call(
        body, name=f"rnn_fwd_l{layer}", grid=(nh // hps, s // ts),
        in_specs=[pl.BlockSpec((2, ts, wl), lambda h, i: (0, i, h)),
                  pl.BlockSpec((CONV_WIDTH, wl), lambda h, i: (0, h)),
                  pl.BlockSpec((8, wl), lambda h, i: (0, h)),
                  pl.BlockSpec((hps, hd, hd), lambda h, i: (h, 0, 0)),
                  pl.BlockSpec((hps, hd, hd), lambda h, i: (h, 0, 0))],
        out_specs=[pl.BlockSpec((ts, wl), lambda h, i: (i, h)),
                   pl.BlockSpec((ts, wl), lambda h, i: (i, h))],
        out_shape=[jax.ShapeDtypeStruct((s, 2 * d), BF16), jax.ShapeDtypeStruct((s, d), F32)],
        scratch_shapes=[pltpu.VMEM((ts + HALO, wl), F32), pltpu.VMEM((8, wl), F32)],
        compiler_params=_params(2),
    )(proj, cw, vec, wa, wx)


def _inv_count(i, ts, lanes, win):
    t = i * ts + lax.broadcasted_iota(jnp.int32, (ts, lanes), 0)
    return 1.0 / jnp.minimum(t + 1, win).astype(F32)


def _window_sum(ext, win, forward):
    rows = ext.shape[0]
    s, d = ext, 1
    while d < win:
        s = s + pltpu.roll(s, d if forward else rows - d, 0)
        d *= 2
    return s


def _pooled(xbuf, xt, lanes, win, inv_cnt, ts):
    acc = _window_sum(xbuf[:, lanes], win, True)[HALO:, :]
    return acc * inv_cnt - xt


def _pool_fwd(proj, ycat, pw, vec, layer):
    _, s, d = proj.shape
    ng, gd, _ = pw.shape
    ts = _tile(s, MATMUL_ROWS)

    def body(proj_ref, ycat_in, pw_ref, vec_ref, ycat_ref, xbuf):
        del ycat_in
        i = pl.program_id(0)

        @pl.when(i == 0)
        def _():
            xbuf[0:HALO, :] = jnp.zeros((HALO, d), F32)

        xbuf[pl.ds(HALO, ts), :] = proj_ref[0]
        for g in range(ng):
            lanes = slice(g * gd, (g + 1) * gd)
            win = 2 << g
            xt = proj_ref[0, :, lanes]
            pooled = _pooled(xbuf, xt, lanes, win, _inv_count(i, ts, gd, win), ts).astype(BF16)
            z = jnp.dot(pooled, pw_ref[g], preferred_element_type=F32) + vec_ref[0:1, lanes]
            gg = proj_ref[1, :, lanes]
            ycat_ref[:, lanes] = (z * vec_ref[1:2, lanes] * (gg * _sigmoid(gg))).astype(BF16)
        xbuf[0:HALO, :] = xbuf[pl.ds(ts, HALO), :]

    return pl.pallas_call(
        body, name=f"pool_fwd_l{layer}", grid=(s // ts,),
        in_specs=[pl.BlockSpec((2, ts, d), lambda i: (1, i, 0)),
                  ANY,
                  pl.BlockSpec((ng, gd, gd), lambda i: (0, 0, 0)),
                  pl.BlockSpec((8, d), lambda i: (0, 0))],
        out_specs=pl.BlockSpec((ts, d), lambda i: (i, 1)),
        out_shape=jax.ShapeDtypeStruct((s, 2 * d), BF16),
        input_output_aliases={1: 0},
        scratch_shapes=[pltpu.VMEM((ts + HALO, d), F32)],
        compiler_params=_params(1),
    )(proj, ycat, pw, vec)


def _outproj_fwd(ycat, w_all, x, vec, target, layer):
    s, d = x.shape
    nk, kd = w_all.shape[0], w_all.shape[1]
    ts = _tile(s, MATMUL_ROWS)
    last = target is not None

    def body(*refs):
        if last:
            ycat_ref, w_ref, x_ref, vec_ref, tgt_ref, y_ref, xo_ref, sq_ref = refs
        else:
            ycat_ref, w_ref, x_ref, vec_ref, y_ref, xo_ref = refs
        y = jnp.dot(ycat_ref[:, 0:kd], w_ref[0], preferred_element_type=F32)
        for k in range(1, nk):
            y = y + jnp.dot(ycat_ref[:, k * kd:(k + 1) * kd], w_ref[k], preferred_element_type=F32)
        y_ref[...] = y
        rs = lax.rsqrt(jnp.mean(y * y, axis=-1, keepdims=True) + NORM_EPS)
        xo = x_ref[...] + vec_ref[2:3, :] * (y * rs * vec_ref[4:5, :])
        if last:
            err = xo - tgt_ref[...]
            xo_ref[...] = err * (1.0 / d)

            @pl.when(pl.program_id(0) == 0)
            def _():
                sq_ref[...] = jnp.zeros_like(sq_ref)

            sq_ref[...] += jnp.sum(err * err)
        else:
            xo_ref[...] = xo

    row = pl.BlockSpec((ts, d), lambda i: (i, 0))
    in_specs = [pl.BlockSpec((ts, nk * kd), lambda i: (i, 0)),
                pl.BlockSpec((nk, kd, d), lambda i: (0, 0, 0)),
                row, pl.BlockSpec((8, d), lambda i: (0, 0))]
    out_specs = [row, row]
    out_shape = [jax.ShapeDtypeStruct((s, d), F32), jax.ShapeDtypeStruct((s, d), F32)]
    args = [ycat, w_all, x, vec]
    if last:
        in_specs.append(row)
        args.append(target)
        out_specs.append(pl.BlockSpec((8, 128), lambda i: (0, 0)))
        out_shape.append(jax.ShapeDtypeStruct((8, 128), F32))
    out = pl.pallas_call(
        body, name=f"outproj_fwd_l{layer}", grid=(s // ts,),
        in_specs=in_specs, out_specs=out_specs, out_shape=out_shape,
        compiler_params=_params(1),
    )(*args)
    return (out[0], out[1], out[2]) if last else (out[0], out[1], None)


def _outproj_bwd(dxo, y, ycat, w_all, vec, layer, after):
    s, d = dxo.shape
    nk, kd = w_all.shape[0], w_all.shape[1]
    ts = _tile(s, MATMUL_ROWS)
    nt = s // ts

    def body(dxo_ref, y_ref, ycat_ref, w_ref, vec_ref, after_ref, dycat_ref, dw_ref, dvec_ref, acc):
        del after_ref
        i = pl.program_id(0)

        @pl.when(i == 0)
        def _():
            acc[...] = jnp.zeros_like(acc)
            dvec_ref[...] = jnp.zeros_like(dvec_ref)

        yt = y_ref[...]
        rs = lax.rsqrt(jnp.mean(yt * yt, axis=-1, keepdims=True) + NORM_EPS)
        yhat = yt * rs
        gate, gpost = vec_ref[2:3, :], vec_ref[4:5, :]
        dxo_t = dxo_ref[...]
        dyn = dxo_t * gate
        dvec_ref[0:1, :] += _colsum(dxo_t * (yhat * gpost))
        dvec_ref[1:2, :] += _colsum(dyn * yhat)
        t = dyn * gpost
        dy = (rs * (t - yhat * jnp.mean(t * yhat, axis=-1, keepdims=True))).astype(BF16)
        for k in range(nk):
            cols = slice(k * kd, (k + 1) * kd)
            dycat_ref[:, cols] = lax.dot_general(dy, w_ref[k], NT_DIMS, preferred_element_type=F32)
            acc[k] += lax.dot_general(ycat_ref[:, cols], dy, TN_DIMS, preferred_element_type=F32)

        @pl.when(i == nt - 1)
        def _():
            dw_ref[...] = acc[...].astype(BF16)

    row = pl.BlockSpec((ts, d), lambda i: (i, 0))
    wide = pl.BlockSpec((ts, nk * kd), lambda i: (i, 0))
    return pl.pallas_call(
        body, name=f"outproj_bwd_l{layer}", grid=(nt,),
        in_specs=[row, row, wide,
                  pl.BlockSpec((nk, kd, d), lambda i: (0, 0, 0)),
                  pl.BlockSpec((8, d), lambda i: (0, 0)), ANY],
        out_specs=[wide,
                   pl.BlockSpec((nk, kd, d), lambda i: (0, 0, 0)),
                   pl.BlockSpec((8, d), lambda i: (0, 0))],
        out_shape=[jax.ShapeDtypeStruct((s, nk * kd), F32),
                   jax.ShapeDtypeStruct((nk, kd, d), BF16),
                   jax.ShapeDtypeStruct((8, d), F32)],
        scratch_shapes=[pltpu.VMEM((nk, kd, d), F32)],
        compiler_params=_params(1),
    )(dxo, y, ycat, w_all, vec, after)


def _halo_index(ts, nt):
    return lambda j: jnp.maximum((nt - 1 - j) * (ts // HALO) - 1, 0)


def _rnn_bwd(proj, hs, dycat, cw, vec, wa, wx, layer, after):
    _, s, d = proj.shape
    nh, hd, _ = wa.shape
    ts = _tile(s, BWD_SCAN_ROWS)
    nt = s // ts
    halo = _halo_index(ts, nt)
    hps = BWD_HEADS_PER_STEP
    wl = hps * hd

    def body(proj_ref, xh_ref, hs_ref, hsh_ref, dy_ref, cw_ref, vec_ref, wa_ref, wx_ref, after_ref,
             dproj_ref, dgates_ref, dvec_ref, xbuf, hbuf, dubuf, carry, dw_acc):
        del after_ref
        j = pl.program_id(1)
        first_tile = j == nt - 1

        @pl.when(j == 0)
        def _():
            dubuf[pl.ds(ts, HALO), :] = jnp.zeros((HALO, wl), F32)
            carry[...] = jnp.zeros_like(carry)
            dw_acc[...] = jnp.zeros_like(dw_acc)
            dvec_ref[...] = jnp.zeros_like(dvec_ref)

        xbuf[0:HALO, :] = jnp.where(first_tile, 0.0, xh_ref[0])
        xbuf[pl.ds(HALO, ts), :] = proj_ref[0]
        hbuf[0:HALO, :] = jnp.where(first_tile, 0.0, hsh_ref[...])
        hbuf[pl.ds(HALO, ts), :] = hs_ref[...]

        for hh in range(hps):
            lanes = slice(hh * hd, (hh + 1) * hd)
            wa, wx = wa_ref[hh], wx_ref[hh]
            hs = hs_ref[:, lanes]
            u = _conv(xbuf, cw_ref, vec_ref, lanes, ts)
            ub, r, ig, sp, log_a = _rnn_gates(u, wa, wx, vec_ref, lanes)
            a = jnp.exp(log_a)
            e2 = jnp.exp(2.0 * log_a)
            one_minus_a2 = _neg_expm1(2.0 * log_a)
            inv_mult = lax.rsqrt(one_minus_a2)
            mult = one_minus_a2 * inv_mult

            g = proj_ref[1, :, lanes]
            sg = _sigmoid(g)
            dyc = dy_ref[:, lanes]
            dproj_ref[1, :, lanes] = (dyc * hs * (sg * (1.0 + g * (1.0 - sg)))).astype(BF16)

            row = lax.broadcasted_iota(jnp.int32, (ts, hd), 0)
            dhs = dyc * (g * sg) + jnp.where(row == ts - 1, carry[0:1, lanes], 0.0)
            dh = _scan_rev(_shift_up(a, 1, 0.0), dhs)
            carry[:, lanes] = (a * dh)[0:8, :]

            h_prev = hbuf[pl.ds(HALO - 1, ts), lanes]
            dlog_a = dh * h_prev * a - dh * (ig * u) * (e2 * inv_mult)
            di = dh * mult * u
            dzr = dlog_a * ((-LRU_C) * sp) * (r * (1.0 - r))
            dzi = di * (ig * (1.0 - ig))
            dvec_ref[3:4, lanes] += _colsum(dlog_a * r) * (LRU_C * _sigmoid(-vec_ref[3:4, lanes]))
            dvec_ref[1:2, lanes] += _colsum(dzr)
            dvec_ref[2:3, lanes] += _colsum(dzi)
            dzr_b, dzi_b = dzr.astype(BF16), dzi.astype(BF16)
            dw_acc[0, hh] += lax.dot_general(ub, dzr_b, TN_DIMS, preferred_element_type=F32)
            dw_acc[1, hh] += lax.dot_general(ub, dzi_b, TN_DIMS, preferred_element_type=F32)
            du = (dh * mult * ig
                  + lax.dot_general(dzr_b, wa, NT_DIMS, preferred_element_type=F32)
                  + lax.dot_general(dzi_b, wx, NT_DIMS, preferred_element_type=F32))
            dvec_ref[0:1, lanes] += _colsum(du)
            for k in range(CONV_WIDTH):
                dvec_ref[4 + k:5 + k, lanes] += _colsum(du * xbuf[pl.ds(HALO - (CONV_WIDTH - 1) + k, ts), lanes])

            dubuf[0:ts, lanes] = du
            dx = cw_ref[CONV_WIDTH - 1:CONV_WIDTH, lanes] * du
            for k in range(CONV_WIDTH - 1):
                dx = dx + cw_ref[k:k + 1, lanes] * dubuf[pl.ds(CONV_WIDTH - 1 - k, ts), lanes]
            dproj_ref[0, :, lanes] = dx.astype(BF16)
        dubuf[pl.ds(ts, HALO), :] = dubuf[0:HALO, :]

        @pl.when(first_tile)
        def _():
            dgates_ref[...] = dw_acc[...].astype(BF16)

    rev = lambda h, j: (nt - 1 - j, h)
    return pl.pallas_call(
        body, name=f"rnn_bwd_l{layer}", grid=(nh // hps, nt),
        in_specs=[pl.BlockSpec((2, ts, wl), lambda h, j: (0, nt - 1 - j, h)),
                  pl.BlockSpec((1, HALO, wl), lambda h, j: (0, halo(j), h)),
                  pl.BlockSpec((ts, wl), rev),
                  pl.BlockSpec((HALO, wl), lambda h, j: (halo(j), h)),
                  pl.BlockSpec((ts, wl), rev),
                  pl.BlockSpec((CONV_WIDTH, wl), lambda h, j: (0, h)),
                  pl.BlockSpec((8, wl), lambda h, j: (0, h)),
                  pl.BlockSpec((hps, hd, hd), lambda h, j: (h, 0, 0)),
                  pl.BlockSpec((hps, hd, hd), lambda h, j: (h, 0, 0)), ANY],
        out_specs=[pl.BlockSpec((2, ts, wl), lambda h, j: (0, nt - 1 - j, h)),
                   pl.BlockSpec((2, hps, hd, hd), lambda h, j: (0, h, 0, 0)),
                   pl.BlockSpec((16, wl), lambda h, j: (0, h))],
        out_shape=[jax.ShapeDtypeStruct((4, s, d), BF16),
                   jax.ShapeDtypeStruct((2, nh, hd, hd), BF16),
                   jax.ShapeDtypeStruct((16, d), F32)],
        scratch_shapes=[pltpu.VMEM((ts + HALO, wl), F32), pltpu.VMEM((ts + HALO, wl), F32),
                        pltpu.VMEM((ts + HALO, wl), F32), pltpu.VMEM((8, wl), F32),
                        pltpu.VMEM((2, hps, hd, hd), F32)],
        compiler_params=_params(2),
    )(proj, proj, hs, hs, dycat, cw, vec, wa, wx, after)


def _pool_bwd(proj, dycat, dproj, pw, vec, layer):
    _, s, d = proj.shape
    ng, gd, _ = pw.shape
    ts = _tile(s, MATMUL_ROWS)
    nt = s // ts
    halo = _halo_index(ts, nt)

    def body(proj_ref, xh_ref, dy_ref, dproj_in, pw_ref, vec_ref, dproj_ref, dpw_ref, dvec_ref, xbuf, qbuf, acc):
        del dproj_in
        j = pl.program_id(0)
        i = nt - 1 - j

        @pl.when(j == 0)
        def _():
            qbuf[pl.ds(ts, HALO), :] = jnp.zeros((HALO, d), F32)
            acc[...] = jnp.zeros_like(acc)
            dvec_ref[...] = jnp.zeros_like(dvec_ref)

        xbuf[0:HALO, :] = jnp.where(i == 0, 0.0, xh_ref[0])
        xbuf[pl.ds(HALO, ts), :] = proj_ref[0]
        for g in range(ng):
            lanes = slice(g * gd, (g + 1) * gd)
            win = 2 << g
            xt = proj_ref[0, :, lanes]
            inv_cnt = _inv_count(i, ts, gd, win)
            pooled = _pooled(xbuf, xt, lanes, win, inv_cnt, ts).astype(BF16)
            z = jnp.dot(pooled, pw_ref[g], preferred_element_type=F32) + vec_ref[0:1, lanes]
            scale = vec_ref[1:2, lanes]
            gg = proj_ref[1, :, lanes]
            sg = _sigmoid(gg)
            dyc = dy_ref[:, lanes]
            dyp = dyc * (gg * sg)
            dproj_ref[1, :, lanes] = (dyc * (z * scale) * (sg * (1.0 + gg * (1.0 - sg)))).astype(BF16)
            dvec_ref[1:2, lanes] += _colsum(dyp * z)
            dz = dyp * scale
            dvec_ref[0:1, lanes] += _colsum(dz)
            dz_b = dz.astype(BF16)
            acc[g] += lax.dot_general(pooled, dz_b, TN_DIMS, preferred_element_type=F32)
            dpooled = lax.dot_general(dz_b, pw_ref[g], NT_DIMS, preferred_element_type=F32)

            qbuf[0:ts, lanes] = dpooled * inv_cnt
            dx = _window_sum(qbuf[:, lanes], win, False)[0:ts, :] - dpooled
            dproj_ref[0, :, lanes] = dx.astype(BF16)
        qbuf[pl.ds(ts, HALO), :] = qbuf[0:HALO, :]

        @pl.when(j == nt - 1)
        def _():
            dpw_ref[...] = acc[...].astype(BF16)

    return pl.pallas_call(
        body, name=f"pool_bwd_l{layer}", grid=(nt,),
        in_specs=[pl.BlockSpec((2, ts, d), lambda j: (1, nt - 1 - j, 0)),
                  pl.BlockSpec((1, HALO, d), lambda j: (2, halo(j), 0)),
                  pl.BlockSpec((ts, d), lambda j: (nt - 1 - j, 1)),
                  ANY,
                  pl.BlockSpec((ng, gd, gd), lambda j: (0, 0, 0)),
                  pl.BlockSpec((8, d), lambda j: (0, 0))],
        out_specs=[pl.BlockSpec((2, ts, d), lambda j: (1, nt - 1 - j, 0)),
                   pl.BlockSpec((ng, gd, gd), lambda j: (0, 0, 0)),
                   pl.BlockSpec((8, d), lambda j: (0, 0))],
        out_shape=[jax.ShapeDtypeStruct((4, s, d), BF16),
                   jax.ShapeDtypeStruct((ng, gd, gd), BF16),
                   jax.ShapeDtypeStruct((8, d), F32)],
        input_output_aliases={3: 0},
        scratch_shapes=[pltpu.VMEM((ts + HALO, d), F32), pltpu.VMEM((ts + HALO, d), F32),
                        pltpu.VMEM((ng, gd, gd), F32)],
        compiler_params=_params(1),
    )(proj, proj, dycat, dproj, pw, vec)


def _inproj_bwd_x(dproj, w_all, x, dxo, vec, layer, after):
    s, d = x.shape
    p = w_all.shape[2]
    ts = _tile(s, MATMUL_ROWS)

    def body(dp_ref, w_ref, x_ref, dxo_ref, vec_ref, after_ref, dx_ref, dvec_ref):
        del after_ref

        @pl.when(pl.program_id(0) == 0)
        def _():
            dvec_ref[...] = jnp.zeros_like(dvec_ref)

        dh = lax.dot_general(dp_ref[0], w_ref[0], NT_DIMS, preferred_element_type=F32)
        for k in range(1, 4):
            dh = dh + lax.dot_general(dp_ref[k], w_ref[k], NT_DIMS, preferred_element_type=F32)
        _, xn, rs = _prenorm(x_ref[...], vec_ref)
        gpre, scale1 = vec_ref[3:4, :], 1.0 + vec_ref[1:2, :]
        dvec_ref[0:1, :] += _colsum(dh)
        dvec_ref[1:2, :] += _colsum(dh * (xn * gpre))
        dvec_ref[2:3, :] += _colsum(dh * (xn * scale1))
        t = dh * (gpre * scale1)
        dx_ref[...] = dxo_ref[...] + rs * (t - xn * jnp.mean(t * xn, axis=-1, keepdims=True))

    row = pl.BlockSpec((ts, d), lambda i: (i, 0))
    return pl.pallas_call(
        body, name=f"inproj_bwd_x_l{layer}", grid=(s // ts,),
        in_specs=[pl.BlockSpec((4, ts, p), lambda i: (0, i, 0)),
                  pl.BlockSpec((4, d, p), lambda i: (0, 0, 0)),
                  row, row, pl.BlockSpec((8, d), lambda i: (0, 0)), ANY],
        out_specs=[row, pl.BlockSpec((8, d), lambda i: (0, 0))],
        out_shape=[jax.ShapeDtypeStruct((s, d), F32), jax.ShapeDtypeStruct((8, d), F32)],
        compiler_params=_params(1),
    )(dproj, w_all, x, dxo, vec, after)


def _inproj_bwd_w(dproj, x, vec, layer, after):
    s, d = x.shape
    p = dproj.shape[2]
    ts = _tile(s, MATMUL_ROWS)
    nt = s // ts

    def body(dp_ref, x_ref, vec_ref, after_ref, dw_ref, acc):
        del after_ref
        i = pl.program_id(0)

        @pl.when(i == 0)
        def _():
            acc[...] = jnp.zeros_like(acc)

        h, _, _ = _prenorm(x_ref[...], vec_ref)
        hb = h.astype(BF16)
        for k in range(4):
            acc[k] += lax.dot_general(hb, dp_ref[k], TN_DIMS, preferred_element_type=F32)

        @pl.when(i == nt - 1)
        def _():
            dw_ref[...] = acc[...].astype(BF16)

    return pl.pallas_call(
        body, name=f"inproj_bwd_w_l{layer}", grid=(nt,),
        in_specs=[pl.BlockSpec((4, ts, p), lambda i: (0, i, 0)),
                  pl.BlockSpec((ts, d), lambda i: (i, 0)),
                  pl.BlockSpec((8, d), lambda i: (0, 0)), ANY],
        out_specs=pl.BlockSpec((4, d, p), lambda i: (0, 0, 0)),
        out_shape=jax.ShapeDtypeStruct((4, d, p), BF16),
        scratch_shapes=[pltpu.VMEM((4, d, p), F32)],
        compiler_params=_params(1),
    )(dproj, x, vec, after)


def _sum_slots(stacked, name, out_dtype=F32):
    n, rows, cols = stacked.shape
    tr = _row_tile(rows)

    def body(in_ref, out_ref):
        total = in_ref[0].astype(F32)
        for b in range(1, n):
            total = total + in_ref[b].astype(F32)
        out_ref[...] = total.astype(out_dtype)

    return pl.pallas_call(
        body, name=name, grid=(rows // tr,),
        in_specs=[pl.BlockSpec((n, tr, cols), lambda i: (0, i, 0))],
        out_specs=pl.BlockSpec((tr, cols), lambda i: (i, 0)),
        out_shape=jax.ShapeDtypeStruct((rows, cols), out_dtype),
        compiler_params=_params(1),
    )(stacked)


def _adam_update(w, m, v, g):
    m_new = ADAM_B1 * m + (1.0 - ADAM_B1) * g
    v_new = ADAM_B2 * v + (1.0 - ADAM_B2) * (g * g)
    m_hat = m_new / (1.0 - ADAM_B1 ** ADAM_STEP)
    v_hat = v_new / (1.0 - ADAM_B2 ** ADAM_STEP)
    return (-ADAM_LR) * (m_hat / (jnp.sqrt(v_hat) + ADAM_EPS) + ADAM_WD * w), m_new, v_new


def _adamw_layer(w, m, v, grads, layer, prev, name, grad_row_offset=0):
    nl = w.shape[0]
    cols = w.shape[-1]
    rows = w.size // (nl * cols)
    tr = _row_tile(rows)
    off = layer * (rows // tr)
    g_off = grad_row_offset // tr
    n = len(grads)
    n_prev = 0 if prev is None else 4

    def body(*refs):
        w_ref, m_ref, v_ref = refs[:3]
        g_refs = refs[3:3 + n]
        g_out, d_out, m_out, v_out = refs[3 + n + n_prev:]
        g = g_refs[0][...].astype(F32)
        for r in g_refs[1:]:
            g = g + r[...].astype(F32)
        g_out[...] = g
        d_out[...], m_out[...], v_out[...] = _adam_update(w_ref[...], m_ref[...], v_ref[...], g)

    mine = pl.BlockSpec((tr, cols), lambda i: (off + i, 0))
    args = [a.reshape(nl * rows, cols) for a in (w, m, v)] + [g.reshape(-1, cols) for g in grads]
    outs = pl.pallas_call(
        body, name=name, grid=(rows // tr,),
        in_specs=[mine] * 3 + [pl.BlockSpec((tr, cols), lambda i: (g_off + i, 0))] * n + [ANY] * n_prev,
        out_specs=[mine] * 4,
        out_shape=[jax.ShapeDtypeStruct((nl * rows, cols), F32)] * 4,
        input_output_aliases={3 + n + k: k for k in range(n_prev)},
        compiler_params=_params(1),
    )(*args, *(prev or ()))
    return tuple(outs)


def _into_slot(a, dtype, chip_arr, name, layer=None, after=None):
    rows, cols = a.shape[-2:]
    tr = _row_tile(rows)

    def body(chip_ref, a_ref, *rest):
        del chip_ref
        rest[-1][...] = a_ref[...].astype(dtype)

    if layer is None:
        in_spec = pl.BlockSpec((tr, cols), lambda i, chip: (i, 0))
    else:
        in_spec = pl.BlockSpec((None, tr, cols), lambda i, chip: (layer, i, 0))
    extra = [] if after is None else [after]
    return pl.pallas_call(
        body, name=name,
        grid_spec=pltpu.PrefetchScalarGridSpec(
            num_scalar_prefetch=1, grid=(rows // tr,),
            in_specs=[in_spec] + [ANY] * len(extra),
            out_specs=pl.BlockSpec((None, tr, cols), lambda i, chip: (chip[0], i, 0))),
        out_shape=jax.ShapeDtypeStruct((4, rows, cols), dtype),
        compiler_params=_params(1),
    )(chip_arr, a, *extra)


def _sum_owner(own, land, chip_arr, own_block, own_index, name):
    blk = land.shape[1:]
    tr = _row_tile(blk[-2])
    steps = blk[-2] // tr
    tile = (*blk[:-2], tr, blk[-1])
    lead = (0,) * (len(blk) - 2)

    def body(chip_ref, own_ref, l1, l2, l3, out_ref):
        del chip_ref
        total = (own_ref[...].astype(F32) + l1[...].astype(F32)) + (l2[...].astype(F32) + l3[...].astype(F32))
        out_ref[...] = total.astype(BF16)

    def landed(k):
        return pl.BlockSpec((None, *tile), lambda i, chip: (chip[0] ^ k, *lead, i, 0))

    return pl.pallas_call(
        body, name=name,
        grid_spec=pltpu.PrefetchScalarGridSpec(
            num_scalar_prefetch=1, grid=(steps,),
            in_specs=[pl.BlockSpec(own_block(tr), own_index), landed(1), landed(2), landed(3)],
            out_specs=pl.BlockSpec(tile, lambda i, chip: (*lead, i, 0))),
        out_shape=jax.ShapeDtypeStruct(blk, BF16),
        compiler_params=_params(1),
    )(chip_arr, own, land, land, land)


_WHOLE_VMEM = pltpu.CompilerParams(vmem_limit_bytes=V7X_VMEM_LIMIT_BYTES)


def _pack_vectors(modbuf, ada_b, pre_norm_g, post_norm_g, conv_b, gate_a_b, gate_x_b, lru_lambda):
    nl, d = pre_norm_g.shape
    n = modbuf.shape[2] // nl
    nh, hd = gate_a_b.shape[1], gate_a_b.shape[2]

    def body(mb_ref, ab_ref, pre_ref, post_ref, cb_ref, gab_ref, gxb_ref, lam_ref, *outs):
        for layer in range(nl):
            vec_ref, rvec_ref = outs[layer], outs[nl + layer]
            vec_ref[...] = jnp.zeros_like(vec_ref)
            rvec_ref[...] = jnp.zeros_like(rvec_ref)
            for k in range(4):
                piece = mb_ref[k, 0:1, layer * n:(layer + 1) * n] + ab_ref[layer:layer + 1, k * n:(k + 1) * n]
                lo = k * n
                while lo < (k + 1) * n:
                    row = lo // d
                    hi = min((row + 1) * d, (k + 1) * n)
                    vec_ref[row:row + 1, lo - row * d:hi - row * d] = piece[:, lo - k * n:hi - k * n]
                    lo = hi
            vec_ref[3:4, :] = pre_ref[layer:layer + 1, :]
            vec_ref[4:5, :] = post_ref[layer:layer + 1, :]
            rvec_ref[0:1, :] = cb_ref[layer:layer + 1, :]
            for h in range(nh):
                rvec_ref[1:2, h * hd:(h + 1) * hd] = gab_ref[layer, h:h + 1, :]
                rvec_ref[2:3, h * hd:(h + 1) * hd] = gxb_ref[layer, h:h + 1, :]
            rvec_ref[3:4, :] = lam_ref[layer:layer + 1, :]

    out = pl.pallas_call(
        body, name="pack_vectors", in_specs=[VMEM] * 8, out_specs=[VMEM] * (2 * nl),
        out_shape=[jax.ShapeDtypeStruct((8, d), F32)] * (2 * nl), compiler_params=_WHOLE_VMEM,
    )(modbuf, ada_b, pre_norm_g, post_norm_g, conv_b, gate_a_b, gate_x_b, lru_lambda)
    return list(out[:nl]), list(out[nl:])


def _pack_gathered(convw_g, poolb_g, pws, pool_scale, ng):
    nl, d = pool_scale.shape
    taps = convw_g.shape[1] // nl
    dq = convw_g.shape[2]
    gq, gd = poolb_g.shape[2], pws[0].shape[2]

    def body(cg_ref, pb_ref, *rest):
        pw_refs, ps_ref = rest[:nl], rest[nl]
        outs = rest[nl + 1:]
        for layer in range(nl):
            cw_ref, pvec_ref, pwf_ref = outs[layer], outs[nl + layer], outs[2 * nl + layer]
            pvec_ref[...] = jnp.zeros_like(pvec_ref)
            pvec_ref[1:2, :] = ps_ref[layer:layer + 1, :]
            for k in range(4):
                cw_ref[:, k * dq:(k + 1) * dq] = cg_ref[k, layer * taps:(layer + 1) * taps, :]
                for g in range(ng):
                    lo = g * gd + k * gq
                    pvec_ref[0:1, lo:lo + gq] = pb_ref[k, layer * ng + g:layer * ng + g + 1, :]
                    pwf_ref[g, k * gq:(k + 1) * gq, :] = pw_refs[layer][k, g * gq:(g + 1) * gq, :]

    out = pl.pallas_call(
        body, name="pack_gathered", in_specs=[VMEM] * (3 + nl), out_specs=[VMEM] * (3 * nl),
        out_shape=[jax.ShapeDtypeStruct((taps, d), F32)] * nl + [jax.ShapeDtypeStruct((8, d), F32)] * nl
        + [jax.ShapeDtypeStruct((ng, gd, gd), BF16)] * nl,
        compiler_params=_WHOLE_VMEM,
    )(convw_g, poolb_g, *pws, pool_scale)
    return list(out[:nl]), list(out[nl:2 * nl]), list(out[2 * nl:])


ROW_SHIFT, ROW_SCALE, ROW_PRE, ROW_GATE, ROW_POST = 0, 1, 2, 8, 9
ROW_CONV_B, ROW_GATE_A_B, ROW_GATE_X_B, ROW_LAMBDA, ROW_CONV_W = 16, 17, 18, 19, 20
ROW_POOL_B, ROW_POOL_SCALE, ROW_SQ = 32, 33, 40


def _adamw_small(totals, chip_arr, params):
    nl = len(totals)
    d = totals[0].shape[1]
    n_par = len(params)
    flat = [a for p in params for a in p]
    nh, hd = params[6][0].shape[1], params[6][0].shape[2]
    taps, dq = params[8][0].shape[1], params[8][0].shape[2]
    ng, gq = params[9][0].shape[1], params[9][0].shape[2]
    gd = d // ng

    def body(chip_ref, *refs):
        tot = refs[:nl]
        ins = refs[nl:nl + 3 * n_par]
        outs = refs[nl + 3 * n_par:]
        chip = chip_ref[0]

        def update(p, idx, g):
            delta, m_new, v_new = _adam_update(ins[3 * p][idx], ins[3 * p + 1][idx], ins[3 * p + 2][idx], g)
            outs[4 * p][idx] = g
            outs[4 * p + 1][idx] = delta
            outs[4 * p + 2][idx] = m_new
            outs[4 * p + 3][idx] = v_new

        def mine(candidates):
            g = candidates[0]
            for k in range(1, 4):
                g = jnp.where(chip == k, candidates[k], g)
            return g

        for layer in range(nl):
            t = tot[layer]
            row = (slice(layer, layer + 1), slice(None))
            for j, r in enumerate((ROW_SHIFT, ROW_SCALE, ROW_GATE)):
                update(0, (slice(layer, layer + 1), slice(j * d, (j + 1) * d)), t[r:r + 1, :])
            for p, r in ((1, ROW_PRE), (2, ROW_POST), (3, ROW_CONV_B), (4, ROW_LAMBDA), (5, ROW_POOL_SCALE)):
                update(p, row, t[r:r + 1, :])
            for h in range(nh):
                idx = (layer, slice(h, h + 1), slice(None))
                update(6, idx, t[ROW_GATE_A_B:ROW_GATE_A_B + 1, h * hd:(h + 1) * hd])
                update(7, idx, t[ROW_GATE_X_B:ROW_GATE_X_B + 1, h * hd:(h + 1) * hd])
            for k in range(taps):
                r = ROW_CONV_W + k
                update(8, (layer, slice(k, k + 1), slice(None)), mine([t[r:r + 1, c * dq:(c + 1) * dq] for c in range(4)]))
            for g in range(ng):
                cands = [t[ROW_POOL_B:ROW_POOL_B + 1, g * gd + c * gq:g * gd + (c + 1) * gq] for c in range(4)]
                update(9, (layer, slice(g, g + 1), slice(None)), mine(cands))

    out = pl.pallas_call(
        body, name="adamw_small",
        in_specs=[pl.BlockSpec(memory_space=pltpu.SMEM)] + [VMEM] * (nl + 3 * n_par),
        out_specs=[VMEM] * (4 * n_par),
        out_shape=[jax.ShapeDtypeStruct(p[0].shape, F32) for p in params for _ in range(4)],
        compiler_params=_WHOLE_VMEM,
    )(chip_arr, *totals, *flat)
    return [tuple(out[4 * p:4 * p + 4]) for p in range(n_par)]


def _adamw_ada_w_layer(c_t, slabs, chip_arr, w, m, v, layer, prev, name):
    nl, d, n = w.shape
    nb = c_t.shape[1]
    tr = _row_tile(d)
    off = layer * (d // tr)
    n_prev = 0 if prev is None else 4
    mod_rows = (ROW_SHIFT, ROW_SCALE, ROW_GATE)

    def body(chip_ref, c_ref, slab_ref, w_ref, m_ref, v_ref, *rest):
        g_out, d_out, m_out, v_out = rest[n_prev:n_prev + 4]
        dm = rest[-1]

        @pl.when(pl.program_id(0) == 0)
        def _():
            for k in range(4):
                @pl.when(chip_ref[0] == k)
                def _():
                    lo = k * n
                    while lo < (k + 1) * n:
                        hi = min((lo // d + 1) * d, (k + 1) * n)
                        row = mod_rows[lo // d]
                        for b in range(nb):
                            dm[b:b + 1, lo - k * n:hi - k * n] = slab_ref[b, row:row + 1, lo % d:lo % d + hi - lo]
                        lo = hi

        g = c_ref[:, 0:1] * dm[0:1, :]
        for b in range(1, nb):
            g = g + c_ref[:, b:b + 1] * dm[b:b + 1, :]
        g_out[...] = g
        d_out[...], m_out[...], v_out[...] = _adam_update(w_ref[...], m_ref[...], v_ref[...], g)

    mine = pl.BlockSpec((tr, n), lambda i, chip: (off + i, 0))
    outs = pl.pallas_call(
        body, name=name,
        grid_spec=pltpu.PrefetchScalarGridSpec(
            num_scalar_prefetch=1, grid=(d // tr,),
            in_specs=[pl.BlockSpec((tr, nb), lambda i, chip: (i, 0)),
                      pl.BlockSpec(slabs.shape, lambda i, chip: (0, 0, 0))] + [mine] * 3 + [ANY] * n_prev,
            out_specs=[mine] * 4,
            scratch_shapes=[pltpu.VMEM((nb, n), F32)]),
        out_shape=[jax.ShapeDtypeStruct((nl * d, n), F32)] * 4,
        input_output_aliases={6 + k: k for k in range(n_prev)},
        compiler_params=_params(1),
    )(chip_arr, c_t, slabs, *[a.reshape(nl * d, n) for a in (w, m, v)], *(prev or ()))
    return tuple(outs)


def _place():
    x, y, c = lax.axis_index("x"), lax.axis_index("y"), lax.axis_index("c")
    return x, y, c


OTHER_CHIPS = ((1, 0), (0, 1), (1, 1))
OTHER_DEVICES = tuple((fx, fy, fc) for fx in (0, 1) for fy in (0, 1) for fc in (0, 1))[1:]


def _mod_exchange(c_row, ada_w, after):
    nl, d, n = ada_w.shape

    def body(c_ref, w_ref, after_ref, cbuf, modbuf, token, cblk, mres, send_a, recv_a, send_c, recv_c):
        del after_ref
        token[...] = jnp.zeros_like(token)
        x, y, c = _place()
        me = 4 * x + 2 * y + c
        chip = 2 * x + y
        cv = c_ref[...]
        cblk[...] = jnp.zeros_like(cblk)
        cblk[0:1, :] = cv * _sigmoid(cv)

        def rows_of(dev):
            return cbuf.at[pl.ds(pl.multiple_of(8 * dev, 8), 8), :]

        cbuf[pl.ds(pl.multiple_of(8 * me, 8), 8), :] = cblk[...]
        sends = []
        for j, (fx, fy, fc) in enumerate(OTHER_DEVICES):
            cp = pltpu.make_async_remote_copy(
                src_ref=cblk, dst_ref=rows_of(me), send_sem=send_a.at[j], recv_sem=recv_a.at[j],
                device_id=(x ^ fx, y ^ fy, c ^ fc), device_id_type=MESH)
            cp.start()
            sends.append(cp)
        for j, (fx, fy, fc) in enumerate(OTHER_DEVICES):
            peer = 4 * (x ^ fx) + 2 * (y ^ fy) + (c ^ fc)
            pltpu.make_async_remote_copy(
                src_ref=cblk, dst_ref=rows_of(peer), send_sem=send_a.at[j], recv_sem=recv_a.at[j],
                device_id=(x ^ fx, y ^ fy, c ^ fc), device_id_type=MESH).wait_recv()
        for cp in sends:
            cp.wait_send()

        call = cbuf[...]
        for layer in range(nl):
            mres[:, layer * n:(layer + 1) * n] = jnp.dot(
                call, w_ref[layer], preferred_element_type=F32, precision=lax.Precision.HIGHEST)

        def block_of(dev):
            return mres.at[pl.ds(pl.multiple_of(8 * dev, 8), 8), :]

        modbuf[chip] = mres[pl.ds(pl.multiple_of(8 * me, 8), 8), :]
        sends = []
        for j, (fx, fy) in enumerate(OTHER_CHIPS):
            peer = 4 * (x ^ fx) + 2 * (y ^ fy) + c
            cp = pltpu.make_async_remote_copy(
                src_ref=block_of(peer), dst_ref=modbuf.at[chip], send_sem=send_c.at[j], recv_sem=recv_c.at[j],
                device_id=(x ^ fx, y ^ fy, c), device_id_type=MESH)
            cp.start()
            sends.append(cp)
        for j, (fx, fy) in enumerate(OTHER_CHIPS):
            pltpu.make_async_remote_copy(
                src_ref=block_of(me), dst_ref=modbuf.at[2 * (x ^ fx) + (y ^ fy)],
                send_sem=send_c.at[j], recv_sem=recv_c.at[j],
                device_id=(x ^ fx, y ^ fy, c), device_id_type=MESH).wait_recv()
        for cp in sends:
            cp.wait_send()

    return pl.pallas_call(
        body, name="mod_exchange", in_specs=[VMEM, VMEM, ANY], out_specs=[VMEM, VMEM, VMEM],
        out_shape=[jax.ShapeDtypeStruct((64, d), F32), jax.ShapeDtypeStruct((4, 8, nl * n), F32),
                   jax.ShapeDtypeStruct((8, 128), F32)],
        scratch_shapes=[pltpu.VMEM((8, d), F32), pltpu.VMEM((64, nl * n), F32),
                        pltpu.SemaphoreType.DMA((7,)), pltpu.SemaphoreType.DMA((7,)),
                        pltpu.SemaphoreType.DMA((3,)), pltpu.SemaphoreType.DMA((3,))],
        compiler_params=pltpu.CompilerParams(vmem_limit_bytes=V7X_VMEM_LIMIT_BYTES, has_side_effects=True),
    )(c_row, ada_w, after)


def _in_hbm(a):
    return pltpu.with_memory_space_constraint(a, pltpu.HBM)


def _gather_copies(lands, split, over_ici):
    x, y, c = _place()
    chip = 2 * x + y
    out = []
    for t, land in enumerate(lands):
        half = land.shape[1] // 2
        mine = pl.ds(pl.multiple_of(c * half, half), half)
        theirs = pl.ds(pl.multiple_of((1 - c) * half, half), half)
        for j, (fx, fy) in enumerate(OTHER_CHIPS):
            them = 2 * (x ^ fx) + (y ^ fy)
            if over_ici and split[t]:
                out.append((land.at[chip, mine], land.at[chip, mine], land.at[them, mine], (x ^ fx, y ^ fy, c), 3 * t + j))
            elif over_ici:
                out.append((land.at[chip], land.at[chip], land.at[them], (x ^ fx, y ^ fy, c), 3 * t + j))
            elif split[t]:
                out.append((land.at[them, mine], land.at[them, mine], land.at[them, theirs], (x, y, 1 - c), 3 * t + j))
    return out


def _gather_start(lands, groups, split, name):
    n, ngr = len(lands), len(groups)

    def body(*refs):
        sems = refs[n:n + 2 * ngr]
        for gi, idxs in enumerate(groups):
            for src, dst, _, peer, k in _gather_copies([refs[i] for i in idxs], [split[i] for i in idxs], True):
                pltpu.make_async_remote_copy(src_ref=src, dst_ref=dst, send_sem=sems[2 * gi].at[k],
                                             recv_sem=sems[2 * gi + 1].at[k], device_id=peer, device_id_type=MESH).start()
        refs[-1][...] = jnp.zeros_like(refs[-1])

    sem_shapes = []
    for idxs in groups:
        sem_shapes += [pltpu.SemaphoreType.DMA((3 * len(idxs),))] * 2
    out = pl.pallas_call(
        body, name=name,
        in_specs=[HBM] * n, out_specs=[SEM] * (2 * ngr) + [HBM] * n + [VMEM],
        out_shape=sem_shapes + [pltpu.HBM(a.shape, a.dtype) for a in lands] + [jax.ShapeDtypeStruct((8, 128), F32)],
        input_output_aliases={i: 2 * ngr + i for i in range(n)},
        compiler_params=pltpu.CompilerParams(has_side_effects=DATAFLOW_EFFECT),
    )(*[_in_hbm(a) for a in lands])
    sems = [(out[2 * gi], out[2 * gi + 1]) for gi in range(ngr)]
    return sems, list(out[2 * ngr:2 * ngr + n]), out[-1]


def _gather_forward(lands, split, sems, after, name):
    n = len(lands)

    def body(*refs):
        ici_send, ici_recv = refs[n], refs[n + 1]
        fwd_send, fwd_recv = refs[n + 3], refs[n + 4]
        forwards = {k: (src, dst, peer) for src, dst, _, peer, k in _gather_copies(refs[:n], split, False)}
        for src, _, landed, peer, k in _gather_copies(refs[:n], split, True):
            cp = pltpu.make_async_remote_copy(src_ref=src, dst_ref=landed, send_sem=ici_send.at[k], recv_sem=ici_recv.at[k],
                                              device_id=peer, device_id_type=MESH)
            cp.wait_recv()
            if k in forwards:
                fsrc, fdst, fpeer = forwards[k]
                pltpu.make_async_remote_copy(src_ref=fsrc, dst_ref=fdst, send_sem=fwd_send.at[k], recv_sem=fwd_recv.at[k],
                                             device_id=fpeer, device_id_type=MESH).start()
            cp.wait_send()

    out = pl.pallas_call(
        body, name=name,
        in_specs=[HBM] * n + [SEM, SEM, ANY], out_specs=[SEM, SEM] + [HBM] * n,
        out_shape=[pltpu.SemaphoreType.DMA((3 * n,))] * 2 + [pltpu.HBM(a.shape, a.dtype) for a in lands],
        input_output_aliases={i: 2 + i for i in range(n)},
        compiler_params=pltpu.CompilerParams(has_side_effects=DATAFLOW_EFFECT),
    )(*lands, sems[0], sems[1], after)
    return (out[0], out[1]), list(out[2:])


def _gather_wait(lands, split, sems, name):
    n = len(lands)

    def body(*refs):
        send_sems, recv_sems = refs[n], refs[n + 1]
        for src, _, landed, peer, k in _gather_copies(refs[:n], split, False):
            cp = pltpu.make_async_remote_copy(src_ref=src, dst_ref=landed, send_sem=send_sems.at[k], recv_sem=recv_sems.at[k],
                                              device_id=peer, device_id_type=MESH)
            cp.wait_send()
            cp.wait_recv()

    out = pl.pallas_call(
        body, name=name,
        in_specs=[HBM] * n + [SEM, SEM], out_specs=[HBM] * n,
        out_shape=[pltpu.HBM(a.shape, a.dtype) for a in lands],
        input_output_aliases={i: i for i in range(n)},
        compiler_params=pltpu.CompilerParams(has_side_effects=DATAFLOW_EFFECT),
    )(*lands, sems[0], sems[1])
    return list(out)


def _to_owner_copies(pairs, q):
    x, y, c = _place()
    chip = 2 * x + y
    out = []
    for t, (part, land) in enumerate(pairs):
        for j, (fx, fy) in enumerate(OTHER_CHIPS):
            owner = 2 * (x ^ fx) + (y ^ fy)
            if part.shape[0] == 4 and part.shape[1:] == land.shape[1:]:
                src = part.at[owner]
            else:
                src = part.at[:, pl.ds(pl.multiple_of(owner * q, q), q), :]
            out.append((src, land.at[chip], land.at[owner], (x ^ fx, y ^ fy, c), 3 * t + j))
    return out


def _to_all_copies(bufs, first_sem):
    x, y, c = _place()
    me = 4 * x + 2 * y + c
    out = []
    for t, buf in enumerate(bufs):
        for j, (fx, fy, fc) in enumerate(OTHER_DEVICES):
            them = 4 * (x ^ fx) + 2 * (y ^ fy) + (c ^ fc)
            out.append((buf.at[me], buf.at[me], buf.at[them], (x ^ fx, y ^ fy, c ^ fc), first_sem + 7 * t + j))
    return out


def _to_chips_copies(bufs, first_sem):
    x, y, c = _place()
    chip = 2 * x + y
    out = []
    for t, buf in enumerate(bufs):
        for j, (fx, fy) in enumerate(OTHER_CHIPS):
            them = 2 * (x ^ fx) + (y ^ fy)
            out.append((buf.at[chip], buf.at[chip], buf.at[them], (x ^ fx, y ^ fy, c), first_sem + 3 * t + j))
    return out


def _exchange_copies(refs, kinds, q):
    n_owner, n_chips = kinds
    pairs = list(zip(refs[:n_owner], refs[n_owner:2 * n_owner]))
    first_all = 3 * (n_owner + n_chips)
    return (_to_owner_copies(pairs, q) + _to_chips_copies(refs[2 * n_owner:2 * n_owner + n_chips], 3 * n_owner)
            + _to_all_copies(refs[2 * n_owner + n_chips:], first_all))


def _exchange_start(arrays, kinds, q, name):
    n = len(arrays)
    n_sems = 3 * (kinds[0] + kinds[1]) + 7 * (n - 2 * kinds[0] - kinds[1])

    def body(*refs):
        send_sems, recv_sems = refs[n], refs[n + 1]
        for src, dst, _, peer, k in _exchange_copies(refs[:n], kinds, q):
            pltpu.make_async_remote_copy(src_ref=src, dst_ref=dst, send_sem=send_sems.at[k], recv_sem=recv_sems.at[k],
                                         device_id=peer, device_id_type=MESH).start()
        refs[-1][...] = jnp.zeros_like(refs[-1])

    out = pl.pallas_call(
        body, name=name,
        in_specs=[HBM] * n, out_specs=[SEM, SEM] + [HBM] * n + [VMEM],
        out_shape=[pltpu.SemaphoreType.DMA((n_sems,))] * 2 + [pltpu.HBM(a.shape, a.dtype) for a in arrays]
        + [jax.ShapeDtypeStruct((8, 128), F32)],
        input_output_aliases={i: 2 + i for i in range(n)},
        compiler_params=pltpu.CompilerParams(has_side_effects=DATAFLOW_EFFECT),
    )(*[_in_hbm(a) for a in arrays])
    return (out[0], out[1]), list(out[2:2 + n]), out[-1]


def _exchange_wait(arrays, sems, kinds, q, after, name):
    n = len(arrays)

    def body(*refs):
        send_sems, recv_sems = refs[n], refs[n + 1]
        for src, _, landed, peer, k in _exchange_copies(refs[:n], kinds, q):
            cp = pltpu.make_async_remote_copy(src_ref=src, dst_ref=landed, send_sem=send_sems.at[k], recv_sem=recv_sems.at[k],
                                              device_id=peer, device_id_type=MESH)
            cp.wait_send()
            cp.wait_recv()

    out = pl.pallas_call(
        body, name=name,
        in_specs=[HBM] * n + [SEM, SEM, ANY], out_specs=[HBM] * n,
        out_shape=[pltpu.HBM(a.shape, a.dtype) for a in arrays],
        input_output_aliases={i: i for i in range(n)},
        compiler_params=pltpu.CompilerParams(has_side_effects=DATAFLOW_EFFECT),
    )(*arrays, sems[0], sems[1], after)
    return list(out)


def _sibling_swap(parts, layer):
    n = len(parts)

    def body(*refs):
        srcs, outs = refs[:n], refs[n:2 * n]
        send_sems, recv_sems = refs[2 * n:]
        x, y, c = _place()
        cps = [pltpu.make_async_remote_copy(
            src_ref=srcs[i], dst_ref=outs[i], send_sem=send_sems.at[i], recv_sem=recv_sems.at[i],
            device_id=(x, y, 1 - c), device_id_type=MESH) for i in range(n)]
        for cp in cps:
            cp.start()
        for cp in cps:
            cp.wait()

    return pl.pallas_call(
        body, name=f"sibling_swap_l{layer}", in_specs=[ANY] * n, out_specs=[ANY] * n,
        out_shape=[jax.ShapeDtypeStruct(a.shape, a.dtype) for a in parts],
        scratch_shapes=[pltpu.SemaphoreType.DMA((n,)), pltpu.SemaphoreType.DMA((n,))],
        compiler_params=pltpu.CompilerParams(has_side_effects=True),
    )(*parts)


def kernel(x, c, ada_w, ada_b, pre_norm_g, w_in, conv_w, conv_b, gate_a_w, gate_a_b, gate_x_w, gate_x_b, lru_lambda, pool_w, pool_b, pool_scale, w_out, post_norm_g, loss_target, m_ada_w, m_ada_b, m_pre_norm_g, m_w_in, m_conv_w, m_conv_b, m_gate_a_w, m_gate_a_b, m_gate_x_w, m_gate_x_b, m_lru_lambda, m_pool_w, m_pool_b, m_pool_scale, m_w_out, m_post_norm_g, v_ada_w, v_ada_b, v_pre_norm_g, v_w_in, v_conv_w, v_conv_b, v_gate_a_w, v_gate_a_b, v_gate_x_w, v_gate_x_b, v_lru_lambda, v_pool_w, v_pool_b, v_pool_scale, v_w_out, v_post_norm_g):
    nl, d, _ = ada_w.shape
    s = x.shape[1]
    nh, hd = gate_a_w.shape[1], gate_a_w.shape[2]
    ng, gq, gd = pool_w.shape[1], pool_w.shape[2], pool_w.shape[3]
    me = 4 * lax.axis_index("x") + 2 * lax.axis_index("y") + lax.axis_index("c")
    chip = 2 * lax.axis_index("x") + lax.axis_index("y")
    chip_arr = jnp.reshape(chip, (1,)).astype(jnp.int32)
    x0 = x.reshape(s, d)
    target = loss_target.reshape(s, d)
    p_in = w_in.shape[2]
    r_out = w_out.shape[1]

    win = [_into_slot(w_in, BF16, chip_arr, f"slot_w_in_l{l}", l) for l in range(nl)]
    wout = [_into_slot(w_out, BF16, chip_arr, f"slot_w_out_l{l}", l) for l in range(nl)]
    pw = [_into_slot(pool_w.reshape(nl, ng * gq, gd), BF16, chip_arr, f"slot_pool_w_l{l}", l) for l in range(nl)]
    first_sems, first_lands, first_token = _gather_start([win[0]], [[0]], [True], "weight_gather_start_first")
    cbuf, modbuf, mod_token = _mod_exchange(c.reshape(1, d), ada_w, first_token)
    vecs, rvecs = _pack_vectors(modbuf, ada_b, pre_norm_g, post_norm_g, conv_b, gate_a_b, gate_x_b, lru_lambda)
    convw = _into_slot(conv_w.reshape(nl * CONV_WIDTH, d // 4), F32, chip_arr, "slot_conv_w", after=mod_token)
    poolb = _into_slot(pool_b.reshape(nl * ng, gq), F32, chip_arr, "slot_pool_b")
    lands = [convw, poolb, *pw, wout[0]]
    split = [False, False] + [True] * (nl + 1)
    groups = [list(range(len(lands)))]
    for l in range(1, nl):
        groups.append([len(lands), len(lands) + 1])
        lands += [win[l], wout[l]]
        split += [True, True]
    sems, lands, _ = _gather_start(lands, groups, split, "weight_gather_start_rest")
    lands, split = first_lands + lands, [True] + split
    groups, sems = [[0]] + [[i + 1 for i in g] for g in groups], first_sems + sems
    wa_b, wx_b = gate_a_w.astype(BF16), gate_x_w.astype(BF16)

    def gathered(gi, after, tag):
        idxs = groups[gi]
        arrays, halves = [lands[i] for i in idxs], [split[i] for i in idxs]
        between, arrays = _gather_forward(arrays, halves, sems[gi], after, f"weight_gather_forward_{tag}")
        return _gather_wait(arrays, halves, between, f"weight_gather_wait_{tag}")

    xs, projs, hss, ycats, ys = [x0], [], [], [], []
    sq = None
    convw_full = poolw_full = pvecs = None
    for l in range(nl):
        if l == 0:
            (win[0],) = gathered(0, modbuf, "a")
        proj = _inproj_fwd(xs[l], vecs[l], win[l], l)
        if l == 0:
            got = gathered(1, proj, "b")
            wout[0] = got[2 + nl]
            convw_full, pvecs, poolw_full = _pack_gathered(got[0], got[1], got[2:2 + nl], pool_scale, ng)
        ycat, hs = _rnn_fwd(proj, convw_full[l], rvecs[l], wa_b[l], wx_b[l], l)
        if l + 1 < nl:
            win[l + 1], wout[l + 1] = gathered(2 + l, hs, f"c{l + 1}")
        ycat = _pool_fwd(proj, ycat, poolw_full[l], pvecs[l], l)
        y, xo, sq = _outproj_fwd(ycat, wout[l], xs[l], vecs[l], target if l == nl - 1 else None, l)
        projs.append(proj), hss.append(hs), ycats.append(ycat), ys.append(y), xs.append(xo)

    c_all_t = cbuf.reshape(8, 8, d)[:, 0, :].T

    def finish(l, flights, after, prev):
        (sems_a, arr_a), (sems_g, arr_g), (sems_b, arr_b), (sems_c, arr_c) = flights
        dwout_l, rwout = _exchange_wait(arr_a, sems_a, (1, 0), gq, after, f"grad_wait_a_l{l}")
        dpw_l, rpw, gates = _exchange_wait(arr_g, sems_g, (1, 1), gq, rwout, f"grad_wait_g_l{l}")
        dwin_l, rwin = _exchange_wait(arr_b, sems_b, (1, 0), gq, gates, f"grad_wait_b_l{l}")
        (slabs,) = _exchange_wait(arr_c, sems_c, (0, 0), gq, rwin, f"grad_wait_c_l{l}")
        p_win = _sum_owner(dwin_l, rwin, chip_arr, lambda tr: (None, tr, p_in),
                           lambda i, chip: (chip[0], i, 0), f"sum_w_in_l{l}")
        p_wout = _sum_owner(dwout_l, rwout, chip_arr, lambda tr: (None, tr, d),
                            lambda i, chip: (chip[0], i, 0), f"sum_w_out_l{l}")
        p_pw = _sum_owner(dpw_l, rpw, chip_arr, lambda tr: (ng, tr, gd),
                          lambda i, chip: (0, chip[0], 0), f"sum_pool_w_l{l}")
        p_gates = _sum_slots(gates.reshape(4, 2 * nh * hd, hd), f"sum_gates_l{l}", BF16)
        q_win, q_wout, q_pw, q_gates = _sibling_swap([p_win, p_wout, p_pw, p_gates], l)
        prev = prev or {}
        big = {
            "w_in": _adamw_layer(w_in, m_w_in, v_w_in, [p_win, q_win], l, prev.get("w_in"), f"adamw_w_in_l{l}"),
            "w_out": _adamw_layer(w_out, m_w_out, v_w_out, [p_wout, q_wout], l, prev.get("w_out"), f"adamw_w_out_l{l}"),
            "pool_w": _adamw_layer(pool_w, m_pool_w, v_pool_w, [p_pw, q_pw], l, prev.get("pool_w"), f"adamw_pool_w_l{l}"),
            "gate_a_w": _adamw_layer(gate_a_w, m_gate_a_w, v_gate_a_w, [p_gates, q_gates], l, prev.get("gate_a_w"),
                                     f"adamw_gate_a_w_l{l}"),
            "gate_x_w": _adamw_layer(gate_x_w, m_gate_x_w, v_gate_x_w, [p_gates, q_gates], l, prev.get("gate_x_w"),
                                     f"adamw_gate_x_w_l{l}", grad_row_offset=nh * hd),
            "ada_w": _adamw_ada_w_layer(c_all_t, slabs, chip_arr, ada_w, m_ada_w, v_ada_w, l, prev.get("ada_w"),
                                        f"adamw_ada_w_l{l}"),
        }
        return big, _sum_slots(slabs, f"sum_slab_l{l}")

    dx = xs[nl]
    flights = token = big = None
    totals = [None] * nl
    for l in reversed(range(nl)):
        vec_l = vecs[l]
        dycat, dwout_l, dvec_o = _outproj_bwd(dx, ys[l], ycats[l], wout[l], vec_l, l, vec_l if token is None else token)
        sems_a, arr_a, tok_a = _exchange_start([dwout_l, lax.empty(dwout_l.shape, BF16)], (1, 0), gq, f"grad_start_a_l{l}")
        dproj, dgates, dvec_r = _rnn_bwd(projs[l], hss[l], dycat, convw_full[l], rvecs[l], wa_b[l], wx_b[l], l, tok_a)
        dproj, dpw_l, dvec_p = _pool_bwd(projs[l], dycat, dproj, poolw_full[l], pvecs[l], l)
        gates4 = lax.dynamic_update_slice(lax.empty((4, *dgates.shape), BF16), dgates[None], (chip, 0, 0, 0, 0))
        sems_g, arr_g, tok_g = _exchange_start([dpw_l, lax.empty((4, ng, gq, gd), BF16), gates4], (1, 1), gq,
                                               f"grad_start_g_l{l}")
        dwin_l = _inproj_bwd_w(dproj, xs[l], vec_l, l, tok_g)
        sems_b, arr_b, tok_b = _exchange_start([dwin_l, lax.empty(dwin_l.shape, BF16)], (1, 0), gq, f"grad_start_b_l{l}")
        dx, dvec_i = _inproj_bwd_x(dproj, win[l], xs[l], dx, vec_l, l, tok_b)
        parts = [dvec_i, dvec_o, dvec_r, dvec_p]
        if l == nl - 1:
            parts.append(jnp.tile(sq, (1, d // sq.shape[1])))
        slab = jnp.concatenate(parts, axis=0)
        slabs = lax.dynamic_update_slice(lax.empty((8, *slab.shape), F32), slab[None], (me, 0, 0))
        sems_c, arr_c, token = _exchange_start([slabs], (0, 0), gq, f"grad_start_c_l{l}")
        if flights is not None:
            big, totals[l + 1] = finish(l + 1, flights, token, big)
        flights = ((sems_a, arr_a), (sems_g, arr_g), (sems_b, arr_b), (sems_c, arr_c))
    big, totals[0] = finish(0, flights, big["w_in"][3] if big else dx, big)
    grad_x = dx.reshape(x.shape)
    loss = totals[nl - 1][ROW_SQ, 0] * (0.5 / d)

    small = _adamw_small(totals, chip_arr, [
        (ada_b, m_ada_b, v_ada_b), (pre_norm_g, m_pre_norm_g, v_pre_norm_g), (post_norm_g, m_post_norm_g, v_post_norm_g),
        (conv_b, m_conv_b, v_conv_b), (lru_lambda, m_lru_lambda, v_lru_lambda), (pool_scale, m_pool_scale, v_pool_scale),
        (gate_a_b, m_gate_a_b, v_gate_a_b), (gate_x_b, m_gate_x_b, v_gate_x_b),
        (conv_w, m_conv_w, v_conv_w), (pool_b, m_pool_b, v_pool_b)])

    results = {
        "ada_w": tuple(o.reshape(ada_w.shape) for o in big["ada_w"]),
        "ada_b": small[0],
        "pre_norm_g": small[1],
        "w_in": tuple(o.reshape(w_in.shape) for o in big["w_in"]),
        "conv_w": small[8],
        "conv_b": small[3],
        "gate_a_w": tuple(o.reshape(gate_a_w.shape) for o in big["gate_a_w"]),
        "gate_a_b": small[6],
        "gate_x_w": tuple(o.reshape(gate_x_w.shape) for o in big["gate_x_w"]),
        "gate_x_b": small[7],
        "lru_lambda": small[4],
        "pool_w": tuple(o.reshape(pool_w.shape) for o in big["pool_w"]),
        "pool_b": small[9],
        "pool_scale": small[5],
        "w_out": tuple(o.reshape(w_out.shape) for o in big["w_out"]),
        "post_norm_g": small[2],
    }
    names = list(results)
    return (loss, grad_x,
            *[results[n][0] for n in names], *[results[n][1] for n in names],
            *[results[n][2] for n in names], *[results[n][3] for n in names])
```

```python
import jax
import jax.numpy as jnp
from jax import lax
from jax.experimental import pallas as pl
from jax.experimental.pallas import tpu as pltpu

F32 = jnp.float32
BF16 = jnp.bfloat16

NORM_EPS = 1e-6
LRU_C = 8.0
CONV_WIDTH = 4
HALO = 16
ADAM_LR = 0.001
ADAM_B1 = 0.9
ADAM_B2 = 0.999
ADAM_EPS = 1e-08
ADAM_WD = 0.01
ADAM_STEP = 10

V7X_VMEM_LIMIT_BYTES = 56 * 1024 * 1024
MATMUL_ROWS = 512
SCAN_ROWS = 512
BWD_SCAN_ROWS = 2048
ELEMENTWISE_ROWS = 512

MESH = pl.DeviceIdType.MESH
ANY = pl.BlockSpec(memory_space=pl.ANY)
VMEM = pl.BlockSpec(memory_space=pltpu.VMEM)
HBM = pl.BlockSpec(memory_space=pltpu.HBM)
SEM = pl.BlockSpec(memory_space=pltpu.SEMAPHORE)
DATAFLOW_EFFECT = pltpu.SideEffectType.DATAFLOW_SIDE_EFFECTING

NT_DIMS = (((1,), (1,)), ((), ()))
TN_DIMS = (((0,), (0,)), ((), ()))


def _params(n_grid_axes):
    return pltpu.CompilerParams(dimension_semantics=("arbitrary",) * n_grid_axes,
                                vmem_limit_bytes=V7X_VMEM_LIMIT_BYTES)


def _tile(total, want):
    t = min(want, max(total // 2, HALO))
    assert total % t == 0 and t % HALO == 0, (total, t)
    return t


def _row_tile(rows):
    for t in range(min(rows, ELEMENTWISE_ROWS) // 8 * 8, 0, -8):
        if rows % t == 0:
            return t
    return rows


def _sigmoid(z):
    return 1.0 / (1.0 + jnp.exp(-z))


def _softplus(z):
    return jnp.maximum(z, 0.0) + jnp.log(1.0 + jnp.exp(-jnp.abs(z)))


def _neg_expm1(z):
    return -jnp.tanh(0.5 * z) * (jnp.exp(z) + 1.0)


def _colsum(v):
    return jnp.sum(v, axis=0, keepdims=True)


def _prenorm(xt, vec_ref):
    rs = lax.rsqrt(jnp.mean(xt * xt, axis=-1, keepdims=True) + NORM_EPS)
    xn = xt * rs
    h = xn * vec_ref[3:4, :] * (1.0 + vec_ref[1:2, :]) + vec_ref[0:1, :]
    return h, xn, rs


def _shift_down(v, d, fill):
    t = v.shape[0]
    if d % 8 == 0:
        return jnp.concatenate([jnp.full((d, v.shape[1]), fill, v.dtype), v[:t - d]], axis=0)
    row = lax.broadcasted_iota(jnp.int32, v.shape, 0)
    return jnp.where(row >= d, pltpu.roll(v, d, 0), fill)


def _shift_up(v, d, fill):
    t = v.shape[0]
    if d % 8 == 0:
        return jnp.concatenate([v[d:], jnp.full((d, v.shape[1]), fill, v.dtype)], axis=0)
    row = lax.broadcasted_iota(jnp.int32, v.shape, 0)
    return jnp.where(row < t - d, pltpu.roll(v, t - d, 0), fill)


def _scan_fwd(a, v, h_before):
    d = 1
    while d < a.shape[0]:
        v = v + a * _shift_down(v, d, 0.0)
        a = a * _shift_down(a, d, 1.0)
        d *= 2
    return a * h_before + v


def _scan_rev(b, v):
    d = 1
    while d < b.shape[0]:
        v = v + b * _shift_up(v, d, 0.0)
        b = b * _shift_up(b, d, 0.0)
        d *= 2
    return v


def _inproj_fwd(x, vec, w_all, layer):
    s, d = x.shape
    p = w_all.shape[2]
    ts = _tile(s, MATMUL_ROWS)

    def body(x_ref, vec_ref, w_ref, proj_ref):
        h, _, _ = _prenorm(x_ref[...], vec_ref)
        hb = h.astype(BF16)
        for k in range(4):
            proj_ref[k] = jnp.dot(hb, w_ref[k], preferred_element_type=F32)

    return pl.pallas_call(
        body, name=f"inproj_fwd_l{layer}", grid=(s // ts,),
        in_specs=[pl.BlockSpec((ts, d), lambda i: (i, 0)),
                  pl.BlockSpec((8, d), lambda i: (0, 0)),
                  pl.BlockSpec((4, d, p), lambda i: (0, 0, 0))],
        out_specs=pl.BlockSpec((4, ts, p), lambda i: (0, i, 0)),
        out_shape=jax.ShapeDtypeStruct((4, s, p), F32),
        compiler_params=_params(1),
    )(x, vec, w_all)


HEADS_PER_STEP = 2
BWD_HEADS_PER_STEP = 1


def _rnn_gates(u, wa, wx, vec_ref, lanes):
    ub = u.astype(BF16)
    r = _sigmoid(jnp.dot(ub, wa, preferred_element_type=F32) + vec_ref[1:2, lanes])
    ig = _sigmoid(jnp.dot(ub, wx, preferred_element_type=F32) + vec_ref[2:3, lanes])
    sp = _softplus(-vec_ref[3:4, lanes])
    log_a = (-LRU_C) * r * sp
    return ub, r, ig, sp, log_a


def _conv(xbuf, cw_ref, vec_ref, lanes, ts):
    u = vec_ref[0:1, lanes] + cw_ref[CONV_WIDTH - 1:CONV_WIDTH, lanes] * xbuf[pl.ds(HALO, ts), lanes]
    for k in range(CONV_WIDTH - 1):
        u = u + cw_ref[k:k + 1, lanes] * xbuf[pl.ds(HALO - (CONV_WIDTH - 1) + k, ts), lanes]
    return u


def _rnn_fwd(proj, cw, vec, wa, wx, layer):
    _, s, d = proj.shape
    nh, hd, _ = wa.shape
    ts = _tile(s, SCAN_ROWS)
    hps = HEADS_PER_STEP
    wl = hps * hd

    def body(proj_ref, cw_ref, vec_ref, wa_ref, wx_ref, ycat_ref, hs_ref, xbuf, hlast):
        i = pl.program_id(1)

        @pl.when(i == 0)
        def _():
            xbuf[0:HALO, :] = jnp.zeros((HALO, wl), F32)
            hlast[...] = jnp.zeros_like(hlast)

        xbuf[pl.ds(HALO, ts), :] = proj_ref[0]
        for hh in range(hps):
            lanes = slice(hh * hd, (hh + 1) * hd)
            u = _conv(xbuf, cw_ref, vec_ref, lanes, ts)
            _, _, ig, _, log_a = _rnn_gates(u, wa_ref[hh], wx_ref[hh], vec_ref, lanes)
            a = jnp.exp(log_a)
            mult = jnp.sqrt(_neg_expm1(2.0 * log_a))
            hs = _scan_fwd(a, mult * (ig * u), hlast[0:1, lanes])
            hs_ref[:, lanes] = hs
            hlast[0:1, lanes] = hs_ref[ts - 1:ts, lanes]
            g = proj_ref[1, :, lanes]
            ycat_ref[:, lanes] = (hs * (g * _sigmoid(g))).astype(BF16)
        xbuf[0:HALO, :] = xbuf[pl.ds(ts, HALO), :]

    return pl.pallas_call(
        body, name=f"rnn_fwd_l{layer}", grid=(nh // hps, s // ts),
        in_specs=[pl.BlockSpec((2, ts, wl), lambda h, i: (0, i, h)),
                  pl.BlockSpec((CONV_WIDTH, wl), lambda h, i: (0, h)),
                  pl.BlockSpec((8, wl), lambda h, i: (0, h)),
                  pl.BlockSpec((hps, hd, hd), lambda h, i: (h, 0, 0)),
                  pl.BlockSpec((hps, hd, hd), lambda h, i: (h, 0, 0))],
        out_specs=[pl.BlockSpec((ts, wl), lambda h, i: (i, h)),
                   pl.BlockSpec((ts, wl), lambda h, i: (i, h))],
        out_shape=[jax.ShapeDtypeStruct((s, 2 * d), BF16), jax.ShapeDtypeStruct((s, d), F32)],
        scratch_shapes=[pltpu.VMEM((ts + HALO, wl), F32), pltpu.VMEM((8, wl), F32)],
        compiler_params=_params(2),
    )(proj, cw, vec, wa, wx)


def _inv_count(i, ts, lanes, win):
    t = i * ts + lax.broadcasted_iota(jnp.int32, (ts, lanes), 0)
    return 1.0 / jnp.minimum(t + 1, win).astype(F32)


def _window_sum(ext, win, forward):
    rows = ext.shape[0]
    s, d = ext, 1
    while d < win:
        s = s + pltpu.roll(s, d if forward else rows - d, 0)
        d *= 2
    return s


def _pooled(xbuf, xt, lanes, win, inv_cnt, ts):
    acc = _window_sum(xbuf[:, lanes], win, True)[HALO:, :]
    return acc * inv_cnt - xt


def _pool_fwd(proj, ycat, pw, vec, layer):
    _, s, d = proj.shape
    ng, gd, _ = pw.shape
    ts = _tile(s, MATMUL_ROWS)

    def body(proj_ref, ycat_in, pw_ref, vec_ref, ycat_ref, xbuf):
        del ycat_in
        i = pl.program_id(0)

        @pl.when(i == 0)
        def _():
            xbuf[0:HALO, :] = jnp.zeros((HALO, d), F32)

        xbuf[pl.ds(HALO, ts), :] = proj_ref[0]
        for g in range(ng):
            lanes = slice(g * gd, (g + 1) * gd)
            win = 2 << g
            xt = proj_ref[0, :, lanes]
            pooled = _pooled(xbuf, xt, lanes, win, _inv_count(i, ts, gd, win), ts).astype(BF16)
            z = jnp.dot(pooled, pw_ref[g], preferred_element_type=F32) + vec_ref[0:1, lanes]
            gg = proj_ref[1, :, lanes]
            ycat_ref[:, lanes] = (z * vec_ref[1:2, lanes] * (gg * _sigmoid(gg))).astype(BF16)
        xbuf[0:HALO, :] = xbuf[pl.ds(ts, HALO), :]

    return pl.pallas_call(
        body, name=f"pool_fwd_l{layer}", grid=(s // ts,),
        in_specs=[pl.BlockSpec((2, ts, d), lambda i: (1, i, 0)),
                  ANY,
                  pl.BlockSpec((ng, gd, gd), lambda i: (0, 0, 0)),
                  pl.BlockSpec((8, d), lambda i: (0, 0))],
        out_specs=pl.BlockSpec((ts, d), lambda i: (i, 1)),
        out_shape=jax.ShapeDtypeStruct((s, 2 * d), BF16),
        input_output_aliases={1: 0},
        scratch_shapes=[pltpu.VMEM((ts + HALO, d), F32)],
        compiler_params=_params(1),
    )(proj, ycat, pw, vec)


def _outproj_fwd(ycat, w_all, x, vec, target, layer):
    s, d = x.shape
    nk, kd = w_all.shape[0], w_all.shape[1]
    ts = _tile(s, MATMUL_ROWS)
    last = target is not None

    def body(*refs):
        if last:
            ycat_ref, w_ref, x_ref, vec_ref, tgt_ref, y_ref, xo_ref, sq_ref = refs
        else:
            ycat_ref, w_ref, x_ref, vec_ref, y_ref, xo_ref = refs
        y = jnp.dot(ycat_ref[:, 0:kd], w_ref[0], preferred_element_type=F32)
        for k in range(1, nk):
            y = y + jnp.dot(ycat_ref[:, k * kd:(k + 1) * kd], w_ref[k], preferred_element_type=F32)
        y_ref[...] = y
        rs = lax.rsqrt(jnp.mean(y * y, axis=-1, keepdims=True) + NORM_EPS)
        xo = x_ref[...] + vec_ref[2:3, :] * (y * rs * vec_ref[4:5, :])
        if last:
            err = xo - tgt_ref[...]
            xo_ref[...] = err * (1.0 / d)

            @pl.when(pl.program_id(0) == 0)
            def _():
                sq_ref[...] = jnp.zeros_like(sq_ref)

            sq_ref[...] += jnp.sum(err * err)
        else:
            xo_ref[...] = xo

    row = pl.BlockSpec((ts, d), lambda i: (i, 0))
    in_specs = [pl.BlockSpec((ts, nk * kd), lambda i: (i, 0)),
                pl.BlockSpec((nk, kd, d), lambda i: (0, 0, 0)),
                row, pl.BlockSpec((8, d), lambda i: (0, 0))]
    out_specs = [row, row]
    out_shape = [jax.ShapeDtypeStruct((s, d), F32), jax.ShapeDtypeStruct((s, d), F32)]
    args = [ycat, w_all, x, vec]
    if last:
        in_specs.append(row)
        args.append(target)
        out_specs.append(pl.BlockSpec((8, 128), lambda i: (0, 0)))
        out_shape.append(jax.ShapeDtypeStruct((8, 128), F32))
    out = pl.pallas_call(
        body, name=f"outproj_fwd_l{layer}", grid=(s // ts,),
        in_specs=in_specs, out_specs=out_specs, out_shape=out_shape,
        compiler_params=_params(1),
    )(*args)
    return (out[0], out[1], out[2]) if last else (out[0], out[1], None)


def _outproj_bwd(dxo, y, ycat, w_all, vec, layer, after):
    s, d = dxo.shape
    nk, kd = w_all.shape[0], w_all.shape[1]
    ts = _tile(s, MATMUL_ROWS)
    nt = s // ts

    def body(dxo_ref, y_ref, ycat_ref, w_ref, vec_ref, after_ref, dycat_ref, dw_ref, dvec_ref, acc):
        del after_ref
        i = pl.program_id(0)

        @pl.when(i == 0)
        def _():
            acc[...] = jnp.zeros_like(acc)
            dvec_ref[...] = jnp.zeros_like(dvec_ref)

        yt = y_ref[...]
        rs = lax.rsqrt(jnp.mean(yt * yt, axis=-1, keepdims=True) + NORM_EPS)
        yhat = yt * rs
        gate, gpost = vec_ref[2:3, :], vec_ref[4:5, :]
        dxo_t = dxo_ref[...]
        dyn = dxo_t * gate
        dvec_ref[0:1, :] += _colsum(dxo_t * (yhat * gpost))
        dvec_ref[1:2, :] += _colsum(dyn * yhat)
        t = dyn * gpost
        dy = (rs * (t - yhat * jnp.mean(t * yhat, axis=-1, keepdims=True))).astype(BF16)
        for k in range(nk):
            cols = slice(k * kd, (k + 1) * kd)
            dycat_ref[:, cols] = lax.dot_general(dy, w_ref[k], NT_DIMS, preferred_element_type=F32)
            acc[k] += lax.dot_general(ycat_ref[:, cols], dy, TN_DIMS, preferred_element_type=F32)

        @pl.when(i == nt - 1)
        def _():
            dw_ref[...] = acc[...].astype(BF16)

    row = pl.BlockSpec((ts, d), lambda i: (i, 0))
    wide = pl.BlockSpec((ts, nk * kd), lambda i: (i, 0))
    return pl.pallas_call(
        body, name=f"outproj_bwd_l{layer}", grid=(nt,),
        in_specs=[row, row, wide,
                  pl.BlockSpec((nk, kd, d), lambda i: (0, 0, 0)),
                  pl.BlockSpec((8, d), lambda i: (0, 0)), ANY],
        out_specs=[wide,
                   pl.BlockSpec((nk, kd, d), lambda i: (0, 0, 0)),
                   pl.BlockSpec((8, d), lambda i: (0, 0))],
        out_shape=[jax.ShapeDtypeStruct((s, nk * kd), F32),
                   jax.ShapeDtypeStruct((nk, kd, d), BF16),
                   jax.ShapeDtypeStruct((8, d), F32)],
        scratch_shapes=[pltpu.VMEM((nk, kd, d), F32)],
        compiler_params=_params(1),
    )(dxo, y, ycat, w_all, vec, after)


def _halo_index(ts, nt):
    return lambda j: jnp.maximum((nt - 1 - j) * (ts // HALO) - 1, 0)


def _rnn_bwd(proj, hs, dycat, cw, vec, wa, wx, layer, after):
    _, s, d = proj.shape
    nh, hd, _ = wa.shape
    ts = _tile(s, BWD_SCAN_ROWS)
    nt = s // ts
    halo = _halo_index(ts, nt)
    hps = BWD_HEADS_PER_STEP
    wl = hps * hd

    def body(proj_ref, xh_ref, hs_ref, hsh_ref, dy_ref, cw_ref, vec_ref, wa_ref, wx_ref, after_ref,
             dproj_ref, dgates_ref, dvec_ref, xbuf, hbuf, dubuf, carry, dw_acc):
        del after_ref
        j = pl.program_id(1)
        first_tile = j == nt - 1

        @pl.when(j == 0)
        def _():
            dubuf[pl.ds(ts, HALO), :] = jnp.zeros((HALO, wl), F32)
            carry[...] = jnp.zeros_like(carry)
            dw_acc[...] = jnp.zeros_like(dw_acc)
            dvec_ref[...] = jnp.zeros_like(dvec_ref)

        xbuf[0:HALO, :] = jnp.where(first_tile, 0.0, xh_ref[0])
        xbuf[pl.ds(HALO, ts), :] = proj_ref[0]
        hbuf[0:HALO, :] = jnp.where(first_tile, 0.0, hsh_ref[...])
        hbuf[pl.ds(HALO, ts), :] = hs_ref[...]

        for hh in range(hps):
            lanes = slice(hh * hd, (hh + 1) * hd)
            wa, wx = wa_ref[hh], wx_ref[hh]
            hs = hs_ref[:, lanes]
            u = _conv(xbuf, cw_ref, vec_ref, lanes, ts)
            ub, r, ig, sp, log_a = _rnn_gates(u, wa, wx, vec_ref, lanes)
            a = jnp.exp(log_a)
            e2 = jnp.exp(2.0 * log_a)
            one_minus_a2 = _neg_expm1(2.0 * log_a)
            inv_mult = lax.rsqrt(one_minus_a2)
            mult = one_minus_a2 * inv_mult

            g = proj_ref[1, :, lanes]
            sg = _sigmoid(g)
            dyc = dy_ref[:, lanes]
            dproj_ref[1, :, lanes] = (dyc * hs * (sg * (1.0 + g * (1.0 - sg)))).astype(BF16)

            row = lax.broadcasted_iota(jnp.int32, (ts, hd), 0)
            dhs = dyc * (g * sg) + jnp.where(row == ts - 1, carry[0:1, lanes], 0.0)
            dh = _scan_rev(_shift_up(a, 1, 0.0), dhs)
            carry[:, lanes] = (a * dh)[0:8, :]

            h_prev = hbuf[pl.ds(HALO - 1, ts), lanes]
            dlog_a = dh * h_prev * a - dh * (ig * u) * (e2 * inv_mult)
            di = dh * mult * u
            dzr = dlog_a * ((-LRU_C) * sp) * (r * (1.0 - r))
            dzi = di * (ig * (1.0 - ig))
            dvec_ref[3:4, lanes] += _colsum(dlog_a * r) * (LRU_C * _sigmoid(-vec_ref[3:4, lanes]))
            dvec_ref[1:2, lanes] += _colsum(dzr)
            dvec_ref[2:3, lanes] += _colsum(dzi)
            dzr_b, dzi_b = dzr.astype(BF16), dzi.astype(BF16)
            dw_acc[0, hh] += lax.dot_general(ub, dzr_b, TN_DIMS, preferred_element_type=F32)
            dw_acc[1, hh] += lax.dot_general(ub, dzi_b, TN_DIMS, preferred_element_type=F32)
            du = (dh * mult * ig
                  + lax.dot_general(dzr_b, wa, NT_DIMS, preferred_element_type=F32)
                  + lax.dot_general(dzi_b, wx, NT_DIMS, preferred_element_type=F32))
            dvec_ref[0:1, lanes] += _colsum(du)
            for k in range(CONV_WIDTH):
                dvec_ref[4 + k:5 + k, lanes] += _colsum(du * xbuf[pl.ds(HALO - (CONV_WIDTH - 1) + k, ts), lanes])

            dubuf[0:ts, lanes] = du
            dx = cw_ref[CONV_WIDTH - 1:CONV_WIDTH, lanes] * du
            for k in range(CONV_WIDTH - 1):
                dx = dx + cw_ref[k:k + 1, lanes] * dubuf[pl.ds(CONV_WIDTH - 1 - k, ts), lanes]
            dproj_ref[0, :, lanes] = dx.astype(BF16)
        dubuf[pl.ds(ts, HALO), :] = dubuf[0:HALO, :]

        @pl.when(first_tile)
        def _():
            dgates_ref[...] = dw_acc[...].astype(BF16)

    rev = lambda h, j: (nt - 1 - j, h)
    return pl.pallas_call(
        body, name=f"rnn_bwd_l{layer}", grid=(nh // hps, nt),
        in_specs=[pl.BlockSpec((2, ts, wl), lambda h, j: (0, nt - 1 - j, h)),
                  pl.BlockSpec((1, HALO, wl), lambda h, j: (0, halo(j), h)),
                  pl.BlockSpec((ts, wl), rev),
                  pl.BlockSpec((HALO, wl), lambda h, j: (halo(j), h)),
                  pl.BlockSpec((ts, wl), rev),
                  pl.BlockSpec((CONV_WIDTH, wl), lambda h, j: (0, h)),
                  pl.BlockSpec((8, wl), lambda h, j: (0, h)),
                  pl.BlockSpec((hps, hd, hd), lambda h, j: (h, 0, 0)),
                  pl.BlockSpec((hps, hd, hd), lambda h, j: (h, 0, 0)), ANY],
        out_specs=[pl.BlockSpec((2, ts, wl), lambda h, j: (0, nt - 1 - j, h)),
                   pl.BlockSpec((2, hps, hd, hd), lambda h, j: (0, h, 0, 0)),
                   pl.BlockSpec((16, wl), lambda h, j: (0, h))],
        out_shape=[jax.ShapeDtypeStruct((4, s, d), BF16),
                   jax.ShapeDtypeStruct((2, nh, hd, hd), BF16),
                   jax.ShapeDtypeStruct((16, d), F32)],
        scratch_shapes=[pltpu.VMEM((ts + HALO, wl), F32), pltpu.VMEM((ts + HALO, wl), F32),
                        pltpu.VMEM((ts + HALO, wl), F32), pltpu.VMEM((8, wl), F32),
                        pltpu.VMEM((2, hps, hd, hd), F32)],
        compiler_params=_params(2),
    )(proj, proj, hs, hs, dycat, cw, vec, wa, wx, after)


def _pool_bwd(proj, dycat, dproj, pw, vec, layer):
    _, s, d = proj.shape
    ng, gd, _ = pw.shape
    ts = _tile(s, MATMUL_ROWS)
    nt = s // ts
    halo = _halo_index(ts, nt)

    def body(proj_ref, xh_ref, dy_ref, dproj_in, pw_ref, vec_ref, dproj_ref, dpw_ref, dvec_ref, xbuf, qbuf, acc):
        del dproj_in
        j = pl.program_id(0)
        i = nt - 1 - j

        @pl.when(j == 0)
        def _():
            qbuf[pl.ds(ts, HALO), :] = jnp.zeros((HALO, d), F32)
            acc[...] = jnp.zeros_like(acc)
            dvec_ref[...] = jnp.zeros_like(dvec_ref)

        xbuf[0:HALO, :] = jnp.where(i == 0, 0.0, xh_ref[0])
        xbuf[pl.ds(HALO, ts), :] = proj_ref[0]
        for g in range(ng):
            lanes = slice(g * gd, (g + 1) * gd)
            win = 2 << g
            xt = proj_ref[0, :, lanes]
            inv_cnt = _inv_count(i, ts, gd, win)
            pooled = _pooled(xbuf, xt, lanes, win, inv_cnt, ts).astype(BF16)
            z = jnp.dot(pooled, pw_ref[g], preferred_element_type=F32) + vec_ref[0:1, lanes]
            scale = vec_ref[1:2, lanes]
            gg = proj_ref[1, :, lanes]
            sg = _sigmoid(gg)
            dyc = dy_ref[:, lanes]
            dyp = dyc * (gg * sg)
            dproj_ref[1, :, lanes] = (dyc * (z * scale) * (sg * (1.0 + gg * (1.0 - sg)))).astype(BF16)
            dvec_ref[1:2, lanes] += _colsum(dyp * z)
            dz = dyp * scale
            dvec_ref[0:1, lanes] += _colsum(dz)
            dz_b = dz.astype(BF16)
            acc[g] += lax.dot_general(pooled, dz_b, TN_DIMS, preferred_element_type=F32)
            dpooled = lax.dot_general(dz_b, pw_ref[g], NT_DIMS, preferred_element_type=F32)

            qbuf[0:ts, lanes] = dpooled * inv_cnt
            dx = _window_sum(qbuf[:, lanes], win, False)[0:ts, :] - dpooled
            dproj_ref[0, :, lanes] = dx.astype(BF16)
        qbuf[pl.ds(ts, HALO), :] = qbuf[0:HALO, :]

        @pl.when(j == nt - 1)
        def _():
            dpw_ref[...] = acc[...].astype(BF16)

    return pl.pallas_call(
        body, name=f"pool_bwd_l{layer}", grid=(nt,),
        in_specs=[pl.BlockSpec((2, ts, d), lambda j: (1, nt - 1 - j, 0)),
                  pl.BlockSpec((1, HALO, d), lambda j: (2, halo(j), 0)),
                  pl.BlockSpec((ts, d), lambda j: (nt - 1 - j, 1)),
                  ANY,
                  pl.BlockSpec((ng, gd, gd), lambda j: (0, 0, 0)),
                  pl.BlockSpec((8, d), lambda j: (0, 0))],
        out_specs=[pl.BlockSpec((2, ts, d), lambda j: (1, nt - 1 - j, 0)),
                   pl.BlockSpec((ng, gd, gd), lambda j: (0, 0, 0)),
                   pl.BlockSpec((8, d), lambda j: (0, 0))],
        out_shape=[jax.ShapeDtypeStruct((4, s, d), BF16),
                   jax.ShapeDtypeStruct((ng, gd, gd), BF16),
                   jax.ShapeDtypeStruct((8, d), F32)],
        input_output_aliases={3: 0},
        scratch_shapes=[pltpu.VMEM((ts + HALO, d), F32), pltpu.VMEM((ts + HALO, d), F32),
                        pltpu.VMEM((ng, gd, gd), F32)],
        compiler_params=_params(1),
    )(proj, proj, dycat, dproj, pw, vec)


def _inproj_bwd_x(dproj, w_all, x, dxo, vec, layer, after):
    s, d = x.shape
    p = w_all.shape[2]
    ts = _tile(s, MATMUL_ROWS)

    def body(dp_ref, w_ref, x_ref, dxo_ref, vec_ref, after_ref, dx_ref, dvec_ref):
        del after_ref

        @pl.when(pl.program_id(0) == 0)
        def _():
            dvec_ref[...] = jnp.zeros_like(dvec_ref)

        dh = lax.dot_general(dp_ref[0], w_ref[0], NT_DIMS, preferred_element_type=F32)
        for k in range(1, 4):
            dh = dh + lax.dot_general(dp_ref[k], w_ref[k], NT_DIMS, preferred_element_type=F32)
        _, xn, rs = _prenorm(x_ref[...], vec_ref)
        gpre, scale1 = vec_ref[3:4, :], 1.0 + vec_ref[1:2, :]
        dvec_ref[0:1, :] += _colsum(dh)
        dvec_ref[1:2, :] += _colsum(dh * (xn * gpre))
        dvec_ref[2:3, :] += _colsum(dh * (xn * scale1))
        t = dh * (gpre * scale1)
        dx_ref[...] = dxo_ref[...] + rs * (t - xn * jnp.mean(t * xn, axis=-1, keepdims=True))

    row = pl.BlockSpec((ts, d), lambda i: (i, 0))
    return pl.pallas_call(
        body, name=f"inproj_bwd_x_l{layer}", grid=(s // ts,),
        in_specs=[pl.BlockSpec((4, ts, p), lambda i: (0, i, 0)),
                  pl.BlockSpec((4, d, p), lambda i: (0, 0, 0)),
                  row, row, pl.BlockSpec((8, d), lambda i: (0, 0)), ANY],
        out_specs=[row, pl.BlockSpec((8, d), lambda i: (0, 0))],
        out_shape=[jax.ShapeDtypeStruct((s, d), F32), jax.ShapeDtypeStruct((8, d), F32)],
        compiler_params=_params(1),
    )(dproj, w_all, x, dxo, vec, after)


def _inproj_bwd_w(dproj, x, vec, layer, after):
    s, d = x.shape
    p = dproj.shape[2]
    ts = _tile(s, MATMUL_ROWS)
    nt = s // ts

    def body(dp_ref, x_ref, vec_ref, after_ref, dw_ref, acc):
        del after_ref
        i = pl.program_id(0)

        @pl.when(i == 0)
        def _():
            acc[...] = jnp.zeros_like(acc)

        h, _, _ = _prenorm(x_ref[...], vec_ref)
        hb = h.astype(BF16)
        for k in range(4):
            acc[k] += lax.dot_general(hb, dp_ref[k], TN_DIMS, preferred_element_type=F32)

        @pl.when(i == nt - 1)
        def _():
            dw_ref[...] = acc[...].astype(BF16)

    return pl.pallas_call(
        body, name=f"inproj_bwd_w_l{layer}", grid=(nt,),
        in_specs=[pl.BlockSpec((4, ts, p), lambda i: (0, i, 0)),
                  pl.BlockSpec((ts, d), lambda i: (i, 0)),
                  pl.BlockSpec((8, d), lambda i: (0, 0)), ANY],
        out_specs=pl.BlockSpec((4, d, p), lambda i: (0, 0, 0)),
        out_shape=jax.ShapeDtypeStruct((4, d, p), BF16),
        scratch_shapes=[pltpu.VMEM((4, d, p), F32)],
        compiler_params=_params(1),
    )(dproj, x, vec, after)


def _sum_slots(stacked, name, out_dtype=F32):
    n, rows, cols = stacked.shape
    tr = _row_tile(rows)

    def body(in_ref, out_ref):
        total = in_ref[0].astype(F32)
        for b in range(1, n):
            total = total + in_ref[b].astype(F32)
        out_ref[...] = total.astype(out_dtype)

    return pl.pallas_call(
        body, name=name, grid=(rows // tr,),
        in_specs=[pl.BlockSpec((n, tr, cols), lambda i: (0, i, 0))],
        out_specs=pl.BlockSpec((tr, cols), lambda i: (i, 0)),
        out_shape=jax.ShapeDtypeStruct((rows, cols), out_dtype),
        compiler_params=_params(1),
    )(stacked)


def _adam_update(w, m, v, g):
    m_new = ADAM_B1 * m + (1.0 - ADAM_B1) * g
    v_new = ADAM_B2 * v + (1.0 - ADAM_B2) * (g * g)
    m_hat = m_new / (1.0 - ADAM_B1 ** ADAM_STEP)
    v_hat = v_new / (1.0 - ADAM_B2 ** ADAM_STEP)
    return (-ADAM_LR) * (m_hat / (jnp.sqrt(v_hat) + ADAM_EPS) + ADAM_WD * w), m_new, v_new


def _adamw_layer(w, m, v, grads, layer, prev, name, grad_row_offset=0):
    nl = w.shape[0]
    cols = w.shape[-1]
    rows = w.size // (nl * cols)
    tr = _row_tile(rows)
    off = layer * (rows // tr)
    g_off = grad_row_offset // tr
    n = len(grads)
    n_prev = 0 if prev is None else 4

    def body(*refs):
        w_ref, m_ref, v_ref = refs[:3]
        g_refs = refs[3:3 + n]
        g_out, d_out, m_out, v_out = refs[3 + n + n_prev:]
        g = g_refs[0][...].astype(F32)
        for r in g_refs[1:]:
            g = g + r[...].astype(F32)
        g_out[...] = g
        d_out[...], m_out[...], v_out[...] = _adam_update(w_ref[...], m_ref[...], v_ref[...], g)

    mine = pl.BlockSpec((tr, cols), lambda i: (off + i, 0))
    args = [a.reshape(nl * rows, cols) for a in (w, m, v)] + [g.reshape(-1, cols) for g in grads]
    outs = pl.pallas_call(
        body, name=name, grid=(rows // tr,),
        in_specs=[mine] * 3 + [pl.BlockSpec((tr, cols), lambda i: (g_off + i, 0))] * n + [ANY] * n_prev,
        out_specs=[mine] * 4,
        out_shape=[jax.ShapeDtypeStruct((nl * rows, cols), F32)] * 4,
        input_output_aliases={3 + n + k: k for k in range(n_prev)},
        compiler_params=_params(1),
    )(*args, *(prev or ()))
    return tuple(outs)


def _into_slot(a, dtype, chip_arr, name, layer=None, after=None):
    rows, cols = a.shape[-2:]
    tr = _row_tile(rows)

    def body(chip_ref, a_ref, *rest):
        del chip_ref
        rest[-1][...] = a_ref[...].astype(dtype)

    if layer is None:
        in_spec = pl.BlockSpec((tr, cols), lambda i, chip: (i, 0))
    else:
        in_spec = pl.BlockSpec((None, tr, cols), lambda i, chip: (layer, i, 0))
    extra = [] if after is None else [after]
    return pl.pallas_call(
        body, name=name,
        grid_spec=pltpu.PrefetchScalarGridSpec(
            num_scalar_prefetch=1, grid=(rows // tr,),
            in_specs=[in_spec] + [ANY] * len(extra),
            out_specs=pl.BlockSpec((None, tr, cols), lambda i, chip: (chip[0], i, 0))),
        out_shape=jax.ShapeDtypeStruct((4, rows, cols), dtype),
        compiler_params=_params(1),
    )(chip_arr, a, *extra)


def _sum_owner(own, land, chip_arr, own_block, own_index, name):
    blk = land.shape[1:]
    tr = _row_tile(blk[-2])
    steps = blk[-2] // tr
    tile = (*blk[:-2], tr, blk[-1])
    lead = (0,) * (len(blk) - 2)

    def body(chip_ref, own_ref, l1, l2, l3, out_ref):
        del chip_ref
        total = (own_ref[...].astype(F32) + l1[...].astype(F32)) + (l2[...].astype(F32) + l3[...].astype(F32))
        out_ref[...] = total.astype(BF16)

    def landed(k):
        return pl.BlockSpec((None, *tile), lambda i, chip: (chip[0] ^ k, *lead, i, 0))

    return pl.pallas_call(
        body, name=name,
        grid_spec=pltpu.PrefetchScalarGridSpec(
            num_scalar_prefetch=1, grid=(steps,),
            in_specs=[pl.BlockSpec(own_block(tr), own_index), landed(1), landed(2), landed(3)],
            out_specs=pl.BlockSpec(tile, lambda i, chip: (*lead, i, 0))),
        out_shape=jax.ShapeDtypeStruct(blk, BF16),
        compiler_params=_params(1),
    )(chip_arr, own, land, land, land)


_WHOLE_VMEM = pltpu.CompilerParams(vmem_limit_bytes=V7X_VMEM_LIMIT_BYTES)


def _pack_vectors(modbuf, ada_b, pre_norm_g, post_norm_g, conv_b, gate_a_b, gate_x_b, lru_lambda):
    nl, d = pre_norm_g.shape
    n = modbuf.shape[2] // nl
    nh, hd = gate_a_b.shape[1], gate_a_b.shape[2]

    def body(mb_ref, ab_ref, pre_ref, post_ref, cb_ref, gab_ref, gxb_ref, lam_ref, *outs):
        for layer in range(nl):
            vec_ref, rvec_ref = outs[layer], outs[nl + layer]
            vec_ref[...] = jnp.zeros_like(vec_ref)
            rvec_ref[...] = jnp.zeros_like(rvec_ref)
            for k in range(4):
                piece = mb_ref[k, 0:1, layer * n:(layer + 1) * n] + ab_ref[layer:layer + 1, k * n:(k + 1) * n]
                lo = k * n
                while lo < (k + 1) * n:
                    row = lo // d
                    hi = min((row + 1) * d, (k + 1) * n)
                    vec_ref[row:row + 1, lo - row * d:hi - row * d] = piece[:, lo - k * n:hi - k * n]
                    lo = hi
            vec_ref[3:4, :] = pre_ref[layer:layer + 1, :]
            vec_ref[4:5, :] = post_ref[layer:layer + 1, :]
            rvec_ref[0:1, :] = cb_ref[layer:layer + 1, :]
            for h in range(nh):
                rvec_ref[1:2, h * hd:(h + 1) * hd] = gab_ref[layer, h:h + 1, :]
                rvec_ref[2:3, h * hd:(h + 1) * hd] = gxb_ref[layer, h:h + 1, :]
            rvec_ref[3:4, :] = lam_ref[layer:layer + 1, :]

    out = pl.pallas_call(
        body, name="pack_vectors", in_specs=[VMEM] * 8, out_specs=[VMEM] * (2 * nl),
        out_shape=[jax.ShapeDtypeStruct((8, d), F32)] * (2 * nl), compiler_params=_WHOLE_VMEM,
    )(modbuf, ada_b, pre_norm_g, post_norm_g, conv_b, gate_a_b, gate_x_b, lru_lambda)
    return list(out[:nl]), list(out[nl:])


def _pack_gathered(convw_g, poolb_g, pws, pool_scale, ng):
    nl, d = pool_scale.shape
    taps = convw_g.shape[1] // nl
    dq = convw_g.shape[2]
    gq, gd = poolb_g.shape[2], pws[0].shape[2]

    def body(cg_ref, pb_ref, *rest):
        pw_refs, ps_ref = rest[:nl], rest[nl]
        outs = rest[nl + 1:]
        for layer in range(nl):
            cw_ref, pvec_ref, pwf_ref = outs[layer], outs[nl + layer], outs[2 * nl + layer]
            pvec_ref[...] = jnp.zeros_like(pvec_ref)
            pvec_ref[1:2, :] = ps_ref[layer:layer + 1, :]
            for k in range(4):
                cw_ref[:, k * dq:(k + 1) * dq] = cg_ref[k, layer * taps:(layer + 1) * taps, :]
                for g in range(ng):
                    lo = g * gd + k * gq
                    pvec_ref[0:1, lo:lo + gq] = pb_ref[k, layer * ng + g:layer * ng + g + 1, :]
                    pwf_ref[g, k * gq:(k + 1) * gq, :] = pw_refs[layer][k, g * gq:(g + 1) * gq, :]

    out = pl.pallas_call(
        body, name="pack_gathered", in_specs=[VMEM] * (3 + nl), out_specs=[VMEM] * (3 * nl),
        out_shape=[jax.ShapeDtypeStruct((taps, d), F32)] * nl + [jax.ShapeDtypeStruct((8, d), F32)] * nl
        + [jax.ShapeDtypeStruct((ng, gd, gd), BF16)] * nl,
        compiler_params=_WHOLE_VMEM,
    )(convw_g, poolb_g, *pws, pool_scale)
    return list(out[:nl]), list(out[nl:2 * nl]), list(out[2 * nl:])


ROW_SHIFT, ROW_SCALE, ROW_PRE, ROW_GATE, ROW_POST = 0, 1, 2, 8, 9
ROW_CONV_B, ROW_GATE_A_B, ROW_GATE_X_B, ROW_LAMBDA, ROW_CONV_W = 16, 17, 18, 19, 20
ROW_POOL_B, ROW_POOL_SCALE, ROW_SQ = 32, 33, 40


def _adamw_small(totals, chip_arr, params):
    nl = len(totals)
    d = totals[0].shape[1]
    n_par = len(params)
    flat = [a for p in params for a in p]
    nh, hd = params[6][0].shape[1], params[6][0].shape[2]
    taps, dq = params[8][0].shape[1], params[8][0].shape[2]
    ng, gq = params[9][0].shape[1], params[9][0].shape[2]
    gd = d // ng

    def body(chip_ref, *refs):
        tot = refs[:nl]
        ins = refs[nl:nl + 3 * n_par]
        outs = refs[nl + 3 * n_par:]
        chip = chip_ref[0]

        def update(p, idx, g):
            delta, m_new, v_new = _adam_update(ins[3 * p][idx], ins[3 * p + 1][idx], ins[3 * p + 2][idx], g)
            outs[4 * p][idx] = g
            outs[4 * p + 1][idx] = delta
            outs[4 * p + 2][idx] = m_new
            outs[4 * p + 3][idx] = v_new

        def mine(candidates):
            g = candidates[0]
            for k in range(1, 4):
                g = jnp.where(chip == k, candidates[k], g)
            return g

        for layer in range(nl):
            t = tot[layer]
            row = (slice(layer, layer + 1), slice(None))
            for j, r in enumerate((ROW_SHIFT, ROW_SCALE, ROW_GATE)):
                update(0, (slice(layer, layer + 1), slice(j * d, (j + 1) * d)), t[r:r + 1, :])
            for p, r in ((1, ROW_PRE), (2, ROW_POST), (3, ROW_CONV_B), (4, ROW_LAMBDA), (5, ROW_POOL_SCALE)):
                update(p, row, t[r:r + 1, :])
            for h in range(nh):
                idx = (layer, slice(h, h + 1), slice(None))
                update(6, idx, t[ROW_GATE_A_B:ROW_GATE_A_B + 1, h * hd:(h + 1) * hd])
                update(7, idx, t[ROW_GATE_X_B:ROW_GATE_X_B + 1, h * hd:(h + 1) * hd])
            for k in range(taps):
                r = ROW_CONV_W + k
                update(8, (layer, slice(k, k + 1), slice(None)), mine([t[r:r + 1, c * dq:(c + 1) * dq] for c in range(4)]))
            for g in range(ng):
                cands = [t[ROW_POOL_B:ROW_POOL_B + 1, g * gd + c * gq:g * gd + (c + 1) * gq] for c in range(4)]
                update(9, (layer, slice(g, g + 1), slice(None)), mine(cands))

    out = pl.pallas_call(
        body, name="adamw_small",
        in_specs=[pl.BlockSpec(memory_space=pltpu.SMEM)] + [VMEM] * (nl + 3 * n_par),
        out_specs=[VMEM] * (4 * n_par),
        out_shape=[jax.ShapeDtypeStruct(p[0].shape, F32) for p in params for _ in range(4)],
        compiler_params=_WHOLE_VMEM,
    )(chip_arr, *totals, *flat)
    return [tuple(out[4 * p:4 * p + 4]) for p in range(n_par)]


def _adamw_ada_w_layer(c_t, slabs, chip_arr, w, m, v, layer, prev, name):
    nl, d, n = w.shape
    nb = c_t.shape[1]
    tr = _row_tile(d)
    off = layer * (d // tr)
    n_prev = 0 if prev is None else 4
    mod_rows = (ROW_SHIFT, ROW_SCALE, ROW_GATE)

    def body(chip_ref, c_ref, slab_ref, w_ref, m_ref, v_ref, *rest):
        g_out, d_out, m_out, v_out = rest[n_prev:n_prev + 4]
        dm = rest[-1]

        @pl.when(pl.program_id(0) == 0)
        def _():
            for k in range(4):
                @pl.when(chip_ref[0] == k)
                def _():
                    lo = k * n
                    while lo < (k + 1) * n:
                        hi = min((lo // d + 1) * d, (k + 1) * n)
                        row = mod_rows[lo // d]
                        for b in range(nb):
                            dm[b:b + 1, lo - k * n:hi - k * n] = slab_ref[b, row:row + 1, lo % d:lo % d + hi - lo]
                        lo = hi

        g = c_ref[:, 0:1] * dm[0:1, :]
        for b in range(1, nb):
            g = g + c_ref[:, b:b + 1] * dm[b:b + 1, :]
        g_out[...] = g
        d_out[...], m_out[...], v_out[...] = _adam_update(w_ref[...], m_ref[...], v_ref[...], g)

    mine = pl.BlockSpec((tr, n), lambda i, chip: (off + i, 0))
    outs = pl.pallas_call(
        body, name=name,
        grid_spec=pltpu.PrefetchScalarGridSpec(
            num_scalar_prefetch=1, grid=(d // tr,),
            in_specs=[pl.BlockSpec((tr, nb), lambda i, chip: (i, 0)),
                      pl.BlockSpec(slabs.shape, lambda i, chip: (0, 0, 0))] + [mine] * 3 + [ANY] * n_prev,
            out_specs=[mine] * 4,
            scratch_shapes=[pltpu.VMEM((nb, n), F32)]),
        out_shape=[jax.ShapeDtypeStruct((nl * d, n), F32)] * 4,
        input_output_aliases={6 + k: k for k in range(n_prev)},
        compiler_params=_params(1),
    )(chip_arr, c_t, slabs, *[a.reshape(nl * d, n) for a in (w, m, v)], *(prev or ()))
    return tuple(outs)


def _place():
    x, y, c = lax.axis_index("x"), lax.axis_index("y"), lax.axis_index("c")
    return x, y, c


OTHER_CHIPS = ((1, 0), (0, 1), (1, 1))
OTHER_DEVICES = tuple((fx, fy, fc) for fx in (0, 1) for fy in (0, 1) for fc in (0, 1))[1:]


def _mod_exchange(c_row, ada_w, after):
    nl, d, n = ada_w.shape

    def body(c_ref, w_ref, after_ref, cbuf, modbuf, token, cblk, mres, send_a, recv_a, send_c, recv_c):
        del after_ref
        token[...] = jnp.zeros_like(token)
        x, y, c = _place()
        me = 4 * x + 2 * y + c
        chip = 2 * x + y
        cv = c_ref[...]
        cblk[...] = jnp.zeros_like(cblk)
        cblk[0:1, :] = cv * _sigmoid(cv)

        def rows_of(dev):
            return cbuf.at[pl.ds(pl.multiple_of(8 * dev, 8), 8), :]

        cbuf[pl.ds(pl.multiple_of(8 * me, 8), 8), :] = cblk[...]
        sends = []
        for j, (fx, fy, fc) in enumerate(OTHER_DEVICES):
            cp = pltpu.make_async_remote_copy(
                src_ref=cblk, dst_ref=rows_of(me), send_sem=send_a.at[j], recv_sem=recv_a.at[j],
                device_id=(x ^ fx, y ^ fy, c ^ fc), device_id_type=MESH)
            cp.start()
            sends.append(cp)
        for j, (fx, fy, fc) in enumerate(OTHER_DEVICES):
            peer = 4 * (x ^ fx) + 2 * (y ^ fy) + (c ^ fc)
            pltpu.make_async_remote_copy(
                src_ref=cblk, dst_ref=rows_of(peer), send_sem=send_a.at[j], recv_sem=recv_a.at[j],
                device_id=(x ^ fx, y ^ fy, c ^ fc), device_id_type=MESH).wait_recv()
        for cp in sends:
            cp.wait_send()

        call = cbuf[...]
        for layer in range(nl):
            mres[:, layer * n:(layer + 1) * n] = jnp.dot(
                call, w_ref[layer], preferred_element_type=F32, precision=lax.Precision.HIGHEST)

        def block_of(dev):
            return mres.at[pl.ds(pl.multiple_of(8 * dev, 8), 8), :]

        modbuf[chip] = mres[pl.ds(pl.multiple_of(8 * me, 8), 8), :]
        sends = []
        for j, (fx, fy) in enumerate(OTHER_CHIPS):
            peer = 4 * (x ^ fx) + 2 * (y ^ fy) + c
            cp = pltpu.make_async_remote_copy(
                src_ref=block_of(peer), dst_ref=modbuf.at[chip], send_sem=send_c.at[j], recv_sem=recv_c.at[j],
                device_id=(x ^ fx, y ^ fy, c), device_id_type=MESH)
            cp.start()
            sends.append(cp)
        for j, (fx, fy) in enumerate(OTHER_CHIPS):
            pltpu.make_async_remote_copy(
                src_ref=block_of(me), dst_ref=modbuf.at[2 * (x ^ fx) + (y ^ fy)],
                send_sem=send_c.at[j], recv_sem=recv_c.at[j],
                device_id=(x ^ fx, y ^ fy, c), device_id_type=MESH).wait_recv()
        for cp in sends:
            cp.wait_send()

    return pl.pallas_call(
        body, name="mod_exchange", in_specs=[VMEM, VMEM, ANY], out_specs=[VMEM, VMEM, VMEM],
        out_shape=[jax.ShapeDtypeStruct((64, d), F32), jax.ShapeDtypeStruct((4, 8, nl * n), F32),
                   jax.ShapeDtypeStruct((8, 128), F32)],
        scratch_shapes=[pltpu.VMEM((8, d), F32), pltpu.VMEM((64, nl * n), F32),
                        pltpu.SemaphoreType.DMA((7,)), pltpu.SemaphoreType.DMA((7,)),
                        pltpu.SemaphoreType.DMA((3,)), pltpu.SemaphoreType.DMA((3,))],
        compiler_params=pltpu.CompilerParams(vmem_limit_bytes=V7X_VMEM_LIMIT_BYTES, has_side_effects=True),
    )(c_row, ada_w, after)


def _in_hbm(a):
    return pltpu.with_memory_space_constraint(a, pltpu.HBM)


def _gather_copies(lands, split, over_ici):
    x, y, c = _place()
    chip = 2 * x + y
    out = []
    for t, land in enumerate(lands):
        half = land.shape[1] // 2
        mine = pl.ds(pl.multiple_of(c * half, half), half)
        theirs = pl.ds(pl.multiple_of((1 - c) * half, half), half)
        for j, (fx, fy) in enumerate(OTHER_CHIPS):
            them = 2 * (x ^ fx) + (y ^ fy)
            if over_ici and split[t]:
                out.append((land.at[chip, mine], land.at[chip, mine], land.at[them, mine], (x ^ fx, y ^ fy, c), 3 * t + j))
            elif over_ici:
                out.append((land.at[chip], land.at[chip], land.at[them], (x ^ fx, y ^ fy, c), 3 * t + j))
            elif split[t]:
                out.append((land.at[them, mine], land.at[them, mine], land.at[them, theirs], (x, y, 1 - c), 3 * t + j))
    return out


def _gather_start(lands, groups, split, name):
    n, ngr = len(lands), len(groups)

    def body(*refs):
        sems = refs[n:n + 2 * ngr]
        for gi, idxs in enumerate(groups):
            for src, dst, _, peer, k in _gather_copies([refs[i] for i in idxs], [split[i] for i in idxs], True):
                pltpu.make_async_remote_copy(src_ref=src, dst_ref=dst, send_sem=sems[2 * gi].at[k],
                                             recv_sem=sems[2 * gi + 1].at[k], device_id=peer, device_id_type=MESH).start()
        refs[-1][...] = jnp.zeros_like(refs[-1])

    sem_shapes = []
    for idxs in groups:
        sem_shapes += [pltpu.SemaphoreType.DMA((3 * len(idxs),))] * 2
    out = pl.pallas_call(
        body, name=name,
        in_specs=[HBM] * n, out_specs=[SEM] * (2 * ngr) + [HBM] * n + [VMEM],
        out_shape=sem_shapes + [pltpu.HBM(a.shape, a.dtype) for a in lands] + [jax.ShapeDtypeStruct((8, 128), F32)],
        input_output_aliases={i: 2 * ngr + i for i in range(n)},
        compiler_params=pltpu.CompilerParams(has_side_effects=DATAFLOW_EFFECT),
    )(*[_in_hbm(a) for a in lands])
    sems = [(out[2 * gi], out[2 * gi + 1]) for gi in range(ngr)]
    return sems, list(out[2 * ngr:2 * ngr + n]), out[-1]


def _gather_forward(lands, split, sems, after, name):
    n = len(lands)

    def body(*refs):
        ici_send, ici_recv = refs[n], refs[n + 1]
        fwd_send, fwd_recv = refs[n + 3], refs[n + 4]
        forwards = {k: (src, dst, peer) for src, dst, _, peer, k in _gather_copies(refs[:n], split, False)}
        for src, _, landed, peer, k in _gather_copies(refs[:n], split, True):
            cp = pltpu.make_async_remote_copy(src_ref=src, dst_ref=landed, send_sem=ici_send.at[k], recv_sem=ici_recv.at[k],
                                              device_id=peer, device_id_type=MESH)
            cp.wait_recv()
            if k in forwards:
                fsrc, fdst, fpeer = forwards[k]
                pltpu.make_async_remote_copy(src_ref=fsrc, dst_ref=fdst, send_sem=fwd_send.at[k], recv_sem=fwd_recv.at[k],
                                             device_id=fpeer, device_id_type=MESH).start()
            cp.wait_send()

    out = pl.pallas_call(
        body, name=name,
        in_specs=[HBM] * n + [SEM, SEM, ANY], out_specs=[SEM, SEM] + [HBM] * n,
        out_shape=[pltpu.SemaphoreType.DMA((3 * n,))] * 2 + [pltpu.HBM(a.shape, a.dtype) for a in lands],
        input_output_aliases={i: 2 + i for i in range(n)},
        compiler_params=pltpu.CompilerParams(has_side_effects=DATAFLOW_EFFECT),
    )(*lands, sems[0], sems[1], after)
    return (out[0], out[1]), list(out[2:])


def _gather_wait(lands, split, sems, name):
    n = len(lands)

    def body(*refs):
        send_sems, recv_sems = refs[n], refs[n + 1]
        for src, _, landed, peer, k in _gather_copies(refs[:n], split, False):
            cp = pltpu.make_async_remote_copy(src_ref=src, dst_ref=landed, send_sem=send_sems.at[k], recv_sem=recv_sems.at[k],
                                              device_id=peer, device_id_type=MESH)
            cp.wait_send()
            cp.wait_recv()

    out = pl.pallas_call(
        body, name=name,
        in_specs=[HBM] * n + [SEM, SEM], out_specs=[HBM] * n,
        out_shape=[pltpu.HBM(a.shape, a.dtype) for a in lands],
        input_output_aliases={i: i for i in range(n)},
        compiler_params=pltpu.CompilerParams(has_side_effects=DATAFLOW_EFFECT),
    )(*lands, sems[0], sems[1])
    return list(out)


def _to_owner_copies(pairs, q):
    x, y, c = _place()
    chip = 2 * x + y
    out = []
    for t, (part, land) in enumerate(pairs):
        for j, (fx, fy) in enumerate(OTHER_CHIPS):
            owner = 2 * (x ^ fx) + (y ^ fy)
            if part.shape[0] == 4 and part.shape[1:] == land.shape[1:]:
                src = part.at[owner]
            else:
                src = part.at[:, pl.ds(pl.multiple_of(owner * q, q), q), :]
            out.append((src, land.at[chip], land.at[owner], (x ^ fx, y ^ fy, c), 3 * t + j))
    return out


def _to_all_copies(bufs, first_sem):
    x, y, c = _place()
    me = 4 * x + 2 * y + c
    out = []
    for t, buf in enumerate(bufs):
        for j, (fx, fy, fc) in enumerate(OTHER_DEVICES):
            them = 4 * (x ^ fx) + 2 * (y ^ fy) + (c ^ fc)
            out.append((buf.at[me], buf.at[me], buf.at[them], (x ^ fx, y ^ fy, c ^ fc), first_sem + 7 * t + j))
    return out


def _to_chips_copies(bufs, first_sem):
    x, y, c = _place()
    chip = 2 * x + y
    out = []
    for t, buf in enumerate(bufs):
        for j, (fx, fy) in enumerate(OTHER_CHIPS):
            them = 2 * (x ^ fx) + (y ^ fy)
            out.append((buf.at[chip], buf.at[chip], buf.at[them], (x ^ fx, y ^ fy, c), first_sem + 3 * t + j))
    return out


def _exchange_copies(refs, kinds, q):
    n_owner, n_chips = kinds
    pairs = list(zip(refs[:n_owner], refs[n_owner:2 * n_owner]))
    first_all = 3 * (n_owner + n_chips)
    return (_to_owner_copies(pairs, q) + _to_chips_copies(refs[2 * n_owner:2 * n_owner + n_chips], 3 * n_owner)
            + _to_all_copies(refs[2 * n_owner + n_chips:], first_all))


def _exchange_start(arrays, kinds, q, name):
    n = len(arrays)
    n_sems = 3 * (kinds[0] + kinds[1]) + 7 * (n - 2 * kinds[0] - kinds[1])

    def body(*refs):
        send_sems, recv_sems = refs[n], refs[n + 1]
        for src, dst, _, peer, k in _exchange_copies(refs[:n], kinds, q):
            pltpu.make_async_remote_copy(src_ref=src, dst_ref=dst, send_sem=send_sems.at[k], recv_sem=recv_sems.at[k],
                                         device_id=peer, device_id_type=MESH).start()
        refs[-1][...] = jnp.zeros_like(refs[-1])

    out = pl.pallas_call(
        body, name=name,
        in_specs=[HBM] * n, out_specs=[SEM, SEM] + [HBM] * n + [VMEM],
        out_shape=[pltpu.SemaphoreType.DMA((n_sems,))] * 2 + [pltpu.HBM(a.shape, a.dtype) for a in arrays]
        + [jax.ShapeDtypeStruct((8, 128), F32)],
        input_output_aliases={i: 2 + i for i in range(n)},
        compiler_params=pltpu.CompilerParams(has_side_effects=DATAFLOW_EFFECT),
    )(*[_in_hbm(a) for a in arrays])
    return (out[0], out[1]), list(out[2:2 + n]), out[-1]


def _exchange_wait(arrays, sems, kinds, q, after, name):
    n = len(arrays)

    def body(*refs):
        send_sems, recv_sems = refs[n], refs[n + 1]
        for src, _, landed, peer, k in _exchange_copies(refs[:n], kinds, q):
            cp = pltpu.make_async_remote_copy(src_ref=src, dst_ref=landed, send_sem=send_sems.at[k], recv_sem=recv_sems.at[k],
                                              device_id=peer, device_id_type=MESH)
            cp.wait_send()
            cp.wait_recv()

    out = pl.pallas_call(
        body, name=name,
        in_specs=[HBM] * n + [SEM, SEM, ANY], out_specs=[HBM] * n,
        out_shape=[pltpu.HBM(a.shape, a.dtype) for a in arrays],
        input_output_aliases={i: i for i in range(n)},
        compiler_params=pltpu.CompilerParams(has_side_effects=DATAFLOW_EFFECT),
    )(*arrays, sems[0], sems[1], after)
    return list(out)


def _sibling_swap(parts, layer):
    n = len(parts)

    def body(*refs):
        srcs, outs = refs[:n], refs[n:2 * n]
        send_sems, recv_sems = refs[2 * n:]
        x, y, c = _place()
        cps = [pltpu.make_async_remote_copy(
            src_ref=srcs[i], dst_ref=outs[i], send_sem=send_sems.at[i], recv_sem=recv_sems.at[i],
            device_id=(x, y, 1 - c), device_id_type=MESH) for i in range(n)]
        for cp in cps:
            cp.start()
        for cp in cps:
            cp.wait()

    return pl.pallas_call(
        body, name=f"sibling_swap_l{layer}", in_specs=[ANY] * n, out_specs=[ANY] * n,
        out_shape=[jax.ShapeDtypeStruct(a.shape, a.dtype) for a in parts],
        scratch_shapes=[pltpu.SemaphoreType.DMA((n,)), pltpu.SemaphoreType.DMA((n,))],
        compiler_params=pltpu.CompilerParams(has_side_effects=True),
    )(*parts)


def kernel(x, c, ada_w, ada_b, pre_norm_g, w_in, conv_w, conv_b, gate_a_w, gate_a_b, gate_x_w, gate_x_b, lru_lambda, pool_w, pool_b, pool_scale, w_out, post_norm_g, loss_target, m_ada_w, m_ada_b, m_pre_norm_g, m_w_in, m_conv_w, m_conv_b, m_gate_a_w, m_gate_a_b, m_gate_x_w, m_gate_x_b, m_lru_lambda, m_pool_w, m_pool_b, m_pool_scale, m_w_out, m_post_norm_g, v_ada_w, v_ada_b, v_pre_norm_g, v_w_in, v_conv_w, v_conv_b, v_gate_a_w, v_gate_a_b, v_gate_x_w, v_gate_x_b, v_lru_lambda, v_pool_w, v_pool_b, v_pool_scale, v_w_out, v_post_norm_g):
    nl, d, _ = ada_w.shape
    s = x.shape[1]
    nh, hd = gate_a_w.shape[1], gate_a_w.shape[2]
    ng, gq, gd = pool_w.shape[1], pool_w.shape[2], pool_w.shape[3]
    me = 4 * lax.axis_index("x") + 2 * lax.axis_index("y") + lax.axis_index("c")
    chip = 2 * lax.axis_index("x") + lax.axis_index("y")
    chip_arr = jnp.reshape(chip, (1,)).astype(jnp.int32)
    x0 = x.reshape(s, d)
    target = loss_target.reshape(s, d)
    p_in = w_in.shape[2]

    c_row = c.reshape(1, d)
    cbuf, modbuf, mod_token = _mod_exchange(c_row, ada_w, c_row)
    vecs, rvecs = _pack_vectors(modbuf, ada_b, pre_norm_g, post_norm_g, conv_b, gate_a_b, gate_x_b, lru_lambda)

    win = [_into_slot(w_in, BF16, chip_arr, f"slot_w_in_l{l}", l) for l in range(nl)]
    wout = [_into_slot(w_out, BF16, chip_arr, f"slot_w_out_l{l}", l) for l in range(nl)]
    pw = [_into_slot(pool_w.reshape(nl, ng * gq, gd), BF16, chip_arr, f"slot_pool_w_l{l}", l) for l in range(nl)]
    convw = _into_slot(conv_w.reshape(nl * CONV_WIDTH, d // 4), F32, chip_arr, "slot_conv_w", after=mod_token)
    poolb = _into_slot(pool_b.reshape(nl * ng, gq), F32, chip_arr, "slot_pool_b")
    lands = [win[0], convw, poolb, *pw, wout[0]]
    split = [True, False, False] + [True] * (nl + 1)
    groups = [[0], list(range(1, len(lands)))]
    for l in range(1, nl):
        groups.append([len(lands), len(lands) + 1])
        lands += [win[l], wout[l]]
        split += [True, True]
    sems, lands, _ = _gather_start(lands, groups, split, "weight_gather_start")
    wa_b, wx_b = gate_a_w.astype(BF16), gate_x_w.astype(BF16)

    def gathered(gi, after, tag):
        idxs = groups[gi]
        arrays, halves = [lands[i] for i in idxs], [split[i] for i in idxs]
        between, arrays = _gather_forward(arrays, halves, sems[gi], after, f"weight_gather_forward_{tag}")
        return _gather_wait(arrays, halves, between, f"weight_gather_wait_{tag}")

    xs, projs, hss, ycats, ys = [x0], [], [], [], []
    sq = None
    convw_full = poolw_full = pvecs = None
    for l in range(nl):
        if l == 0:
            (win[0],) = gathered(0, modbuf, "a")
        proj = _inproj_fwd(xs[l], vecs[l], win[l], l)
        if l == 0:
            got = gathered(1, proj, "b")
            wout[0] = got[2 + nl]
            convw_full, pvecs, poolw_full = _pack_gathered(got[0], got[1], got[2:2 + nl], pool_scale, ng)
        ycat, hs = _rnn_fwd(proj, convw_full[l], rvecs[l], wa_b[l], wx_b[l], l)
        if l + 1 < nl:
            win[l + 1], wout[l + 1] = gathered(2 + l, hs, f"c{l + 1}")
        ycat = _pool_fwd(proj, ycat, poolw_full[l], pvecs[l], l)
        y, xo, sq = _outproj_fwd(ycat, wout[l], xs[l], vecs[l], target if l == nl - 1 else None, l)
        projs.append(proj), hss.append(hs), ycats.append(ycat), ys.append(y), xs.append(xo)

    c_all_t = cbuf.reshape(8, 8, d)[:, 0, :].T

    def finish(l, flights, after, prev):
        (sems_a, arr_a), (sems_g, arr_g), (sems_b, arr_b), (sems_c, arr_c) = flights
        dwout_l, rwout = _exchange_wait(arr_a, sems_a, (1, 0), gq, after, f"grad_wait_a_l{l}")
        dpw_l, rpw, gates = _exchange_wait(arr_g, sems_g, (1, 1), gq, rwout, f"grad_wait_g_l{l}")
        dwin_l, rwin = _exchange_wait(arr_b, sems_b, (1, 0), gq, gates, f"grad_wait_b_l{l}")
        (slabs,) = _exchange_wait(arr_c, sems_c, (0, 0), gq, rwin, f"grad_wait_c_l{l}")
        p_win = _sum_owner(dwin_l, rwin, chip_arr, lambda tr: (None, tr, p_in),
                           lambda i, chip: (chip[0], i, 0), f"sum_w_in_l{l}")
        p_wout = _sum_owner(dwout_l, rwout, chip_arr, lambda tr: (None, tr, d),
                            lambda i, chip: (chip[0], i, 0), f"sum_w_out_l{l}")
        p_pw = _sum_owner(dpw_l, rpw, chip_arr, lambda tr: (ng, tr, gd),
                          lambda i, chip: (0, chip[0], 0), f"sum_pool_w_l{l}")
        p_gates = _sum_slots(gates.reshape(4, 2 * nh * hd, hd), f"sum_gates_l{l}", BF16)
        q_win, q_wout, q_pw, q_gates = _sibling_swap([p_win, p_wout, p_pw, p_gates], l)
        prev = prev or {}
        big = {
            "w_in": _adamw_layer(w_in, m_w_in, v_w_in, [p_win, q_win], l, prev.get("w_in"), f"adamw_w_in_l{l}"),
            "w_out": _adamw_layer(w_out, m_w_out, v_w_out, [p_wout, q_wout], l, prev.get("w_out"), f"adamw_w_out_l{l}"),
            "pool_w": _adamw_layer(pool_w, m_pool_w, v_pool_w, [p_pw, q_pw], l, prev.get("pool_w"), f"adamw_pool_w_l{l}"),
            "gate_a_w": _adamw_layer(gate_a_w, m_gate_a_w, v_gate_a_w, [p_gates, q_gates], l, prev.get("gate_a_w"),
                                     f"adamw_gate_a_w_l{l}"),
            "gate_x_w": _adamw_layer(gate_x_w, m_gate_x_w, v_gate_x_w, [p_gates, q_gates], l, prev.get("gate_x_w"),
                                     f"adamw_gate_x_w_l{l}", grad_row_offset=nh * hd),
            "ada_w": _adamw_ada_w_layer(c_all_t, slabs, chip_arr, ada_w, m_ada_w, v_ada_w, l, prev.get("ada_w"),
                                        f"adamw_ada_w_l{l}"),
        }
        return big, _sum_slots(slabs, f"sum_slab_l{l}")

    dx = xs[nl]
    flights = token = big = None
    totals = [None] * nl
    for l in reversed(range(nl)):
        vec_l = vecs[l]
        dycat, dwout_l, dvec_o = _outproj_bwd(dx, ys[l], ycats[l], wout[l], vec_l, l, vec_l if token is None else token)
        sems_a, arr_a, tok_a = _exchange_start([dwout_l, lax.empty(dwout_l.shape, BF16)], (1, 0), gq, f"grad_start_a_l{l}")
        dproj, dgates, dvec_r = _rnn_bwd(projs[l], hss[l], dycat, convw_full[l], rvecs[l], wa_b[l], wx_b[l], l, tok_a)
        dproj, dpw_l, dvec_p = _pool_bwd(projs[l], dycat, dproj, poolw_full[l], pvecs[l], l)
        gates4 = lax.dynamic_update_slice(lax.empty((4, *dgates.shape), BF16), dgates[None], (chip, 0, 0, 0, 0))
        sems_g, arr_g, tok_g = _exchange_start([dpw_l, lax.empty((4, ng, gq, gd), BF16), gates4], (1, 1), gq,
                                               f"grad_start_g_l{l}")
        dwin_l = _inproj_bwd_w(dproj, xs[l], vec_l, l, tok_g)
        sems_b, arr_b, tok_b = _exchange_start([dwin_l, lax.empty(dwin_l.shape, BF16)], (1, 0), gq, f"grad_start_b_l{l}")
        dx, dvec_i = _inproj_bwd_x(dproj, win[l], xs[l], dx, vec_l, l, tok_b)
        parts = [dvec_i, dvec_o, dvec_r, dvec_p]
        if l == nl - 1:
            parts.append(jnp.tile(sq, (1, d // sq.shape[1])))
        slab = jnp.concatenate(parts, axis=0)
        slabs = lax.dynamic_update_slice(lax.empty((8, *slab.shape), F32), slab[None], (me, 0, 0))
        sems_c, arr_c, token = _exchange_start([slabs], (0, 0), gq, f"grad_start_c_l{l}")
        if flights is not None:
            big, totals[l + 1] = finish(l + 1, flights, token, big)
        flights = ((sems_a, arr_a), (sems_g, arr_g), (sems_b, arr_b), (sems_c, arr_c))
    big, totals[0] = finish(0, flights, big["w_in"][3] if big else dx, big)
    grad_x = dx.reshape(x.shape)
    loss = totals[nl - 1][ROW_SQ, 0] * (0.5 / d)

    small = _adamw_small(totals, chip_arr, [
        (ada_b, m_ada_b, v_ada_b), (pre_norm_g, m_pre_norm_g, v_pre_norm_g), (post_norm_g, m_post_norm_g, v_post_norm_g),
        (conv_b, m_conv_b, v_conv_b), (lru_lambda, m_lru_lambda, v_lru_lambda), (pool_scale, m_pool_scale, v_pool_scale),
        (gate_a_b, m_gate_a_b, v_gate_a_b), (gate_x_b, m_gate_x_b, v_gate_x_b),
        (conv_w, m_conv_w, v_conv_w), (pool_b, m_pool_b, v_pool_b)])

    results = {
        "ada_w": tuple(o.reshape(ada_w.shape) for o in big["ada_w"]),
        "ada_b": small[0],
        "pre_norm_g": small[1],
        "w_in": tuple(o.reshape(w_in.shape) for o in big["w_in"]),
        "conv_w": small[8],
        "conv_b": small[3],
        "gate_a_w": tuple(o.reshape(gate_a_w.shape) for o in big["gate_a_w"]),
        "gate_a_b": small[6],
        "gate_x_w": tuple(o.reshape(gate_x_w.shape) for o in big["gate_x_w"]),
        "gate_x_b": small[7],
        "lru_lambda": small[4],
        "pool_w": tuple(o.reshape(pool_w.shape) for o in big["pool_w"]),
        "pool_b": small[9],
        "pool_scale": small[5],
        "w_out": tuple(o.reshape(w_out.shape) for o in big["w_out"]),
        "post_norm_g": small[2],
    }
    names = list(results)
    return (loss, grad_x,
            *[results[n][0] for n in names], *[results[n][1] for n in names],
            *[results[n][2] for n in names], *[results[n][3] for n in names])
```

```python
import jax
import jax.numpy as jnp
from jax import lax
from jax.experimental import pallas as pl
from jax.experimental.pallas import tpu as pltpu

F32 = jnp.float32
BF16 = jnp.bfloat16

NORM_EPS = 1e-6
LRU_C = 8.0
CONV_WIDTH = 4
HALO = 16
ADAM_LR = 0.001
ADAM_B1 = 0.9
ADAM_B2 = 0.999
ADAM_EPS = 1e-08
ADAM_WD = 0.01
ADAM_STEP = 10

V7X_VMEM_LIMIT_BYTES = 56 * 1024 * 1024
MATMUL_ROWS = 512
SCAN_ROWS = 512
BWD_SCAN_ROWS = 1024
ELEMENTWISE_ROWS = 512

MESH = pl.DeviceIdType.MESH
ANY = pl.BlockSpec(memory_space=pl.ANY)
VMEM = pl.BlockSpec(memory_space=pltpu.VMEM)
HBM = pl.BlockSpec(memory_space=pltpu.HBM)
SEM = pl.BlockSpec(memory_space=pltpu.SEMAPHORE)
DATAFLOW_EFFECT = pltpu.SideEffectType.DATAFLOW_SIDE_EFFECTING

NT_DIMS = (((1,), (1,)), ((), ()))
TN_DIMS = (((0,), (0,)), ((), ()))


def _params(n_grid_axes):
    return pltpu.CompilerParams(dimension_semantics=("arbitrary",) * n_grid_axes,
                                vmem_limit_bytes=V7X_VMEM_LIMIT_BYTES)


def _tile(total, want):
    t = min(want, max(total // 2, HALO))
    assert total % t == 0 and t % HALO == 0, (total, t)
    return t


def _row_tile(rows):
    for t in range(min(rows, ELEMENTWISE_ROWS) // 8 * 8, 0, -8):
        if rows % t == 0:
            return t
    return rows


def _sigmoid(z):
    return 1.0 / (1.0 + jnp.exp(-z))


def _softplus(z):
    return jnp.maximum(z, 0.0) + jnp.log(1.0 + jnp.exp(-jnp.abs(z)))


def _neg_expm1(z):
    return -jnp.tanh(0.5 * z) * (jnp.exp(z) + 1.0)


def _colsum(v):
    return jnp.sum(v, axis=0, keepdims=True)


def _prenorm(xt, vec_ref):
    rs = lax.rsqrt(jnp.mean(xt * xt, axis=-1, keepdims=True) + NORM_EPS)
    xn = xt * rs
    h = xn * vec_ref[3:4, :] * (1.0 + vec_ref[1:2, :]) + vec_ref[0:1, :]
    return h, xn, rs


def _shift_down(v, d, fill):
    t = v.shape[0]
    if d % 8 == 0:
        return jnp.concatenate([jnp.full((d, v.shape[1]), fill, v.dtype), v[:t - d]], axis=0)
    row = lax.broadcasted_iota(jnp.int32, v.shape, 0)
    return jnp.where(row >= d, pltpu.roll(v, d, 0), fill)


def _shift_up(v, d, fill):
    t = v.shape[0]
    if d % 8 == 0:
        return jnp.concatenate([v[d:], jnp.full((d, v.shape[1]), fill, v.dtype)], axis=0)
    row = lax.broadcasted_iota(jnp.int32, v.shape, 0)
    return jnp.where(row < t - d, pltpu.roll(v, t - d, 0), fill)


def _scan_fwd(a, v, h_before):
    d = 1
    while d < a.shape[0]:
        v = v + a * _shift_down(v, d, 0.0)
        a = a * _shift_down(a, d, 1.0)
        d *= 2
    return a * h_before + v


def _scan_rev(b, v):
    d = 1
    while d < b.shape[0]:
        v = v + b * _shift_up(v, d, 0.0)
        b = b * _shift_up(b, d, 0.0)
        d *= 2
    return v


def _inproj_fwd(x, vec, w_all, layer):
    s, d = x.shape
    p = w_all.shape[2]
    ts = _tile(s, MATMUL_ROWS)

    def body(x_ref, vec_ref, w_ref, proj_ref):
        h, _, _ = _prenorm(x_ref[...], vec_ref)
        hb = h.astype(BF16)
        for k in range(4):
            proj_ref[k] = jnp.dot(hb, w_ref[k], preferred_element_type=F32)

    return pl.pallas_call(
        body, name=f"inproj_fwd_l{layer}", grid=(s // ts,),
        in_specs=[pl.BlockSpec((ts, d), lambda i: (i, 0)),
                  pl.BlockSpec((8, d), lambda i: (0, 0)),
                  pl.BlockSpec((4, d, p), lambda i: (0, 0, 0))],
        out_specs=pl.BlockSpec((4, ts, p), lambda i: (0, i, 0)),
        out_shape=jax.ShapeDtypeStruct((4, s, p), F32),
        compiler_params=_params(1),
    )(x, vec, w_all)


HEADS_PER_STEP = 2
BWD_HEADS_PER_STEP = 1


def _rnn_gates(u, wa, wx, vec_ref, lanes):
    ub = u.astype(BF16)
    r = _sigmoid(jnp.dot(ub, wa, preferred_element_type=F32) + vec_ref[1:2, lanes])
    ig = _sigmoid(jnp.dot(ub, wx, preferred_element_type=F32) + vec_ref[2:3, lanes])
    sp = _softplus(-vec_ref[3:4, lanes])
    log_a = (-LRU_C) * r * sp
    return ub, r, ig, sp, log_a


def _conv(xbuf, cw_ref, vec_ref, lanes, ts):
    u = vec_ref[0:1, lanes] + cw_ref[CONV_WIDTH - 1:CONV_WIDTH, lanes] * xbuf[pl.ds(HALO, ts), lanes]
    for k in range(CONV_WIDTH - 1):
        u = u + cw_ref[k:k + 1, lanes] * xbuf[pl.ds(HALO - (CONV_WIDTH - 1) + k, ts), lanes]
    return u


def _rnn_fwd(proj, cw, vec, wa, wx, layer):
    _, s, d = proj.shape
    nh, hd, _ = wa.shape
    ts = _tile(s, SCAN_ROWS)
    hps = HEADS_PER_STEP
    wl = hps * hd

    def body(proj_ref, cw_ref, vec_ref, wa_ref, wx_ref, ycat_ref, hs_ref, xbuf, hlast):
        i = pl.program_id(1)

        @pl.when(i == 0)
        def _():
            xbuf[0:HALO, :] = jnp.zeros((HALO, wl), F32)
            hlast[...] = jnp.zeros_like(hlast)

        xbuf[pl.ds(HALO, ts), :] = proj_ref[0]
        for hh in range(hps):
            lanes = slice(hh * hd, (hh + 1) * hd)
            u = _conv(xbuf, cw_ref, vec_ref, lanes, ts)
            _, _, ig, _, log_a = _rnn_gates(u, wa_ref[hh], wx_ref[hh], vec_ref, lanes)
            a = jnp.exp(log_a)
            mult = jnp.sqrt(_neg_expm1(2.0 * log_a))
            hs = _scan_fwd(a, mult * (ig * u), hlast[0:1, lanes])
            hs_ref[:, lanes] = hs
            hlast[0:1, lanes] = hs_ref[ts - 1:ts, lanes]
            g = proj_ref[1, :, lanes]
            ycat_ref[:, lanes] = (hs * (g * _sigmoid(g))).astype(BF16)
        xbuf[0:HALO, :] = xbuf[pl.ds(ts, HALO), :]

    return pl.pallas_call(
        body, name=f"rnn_fwd_l{layer}", grid=(nh // hps, s // ts),
        in_specs=[pl.BlockSpec((2, ts, wl), lambda h, i: (0, i, h)),
                  pl.BlockSpec((CONV_WIDTH, wl), lambda h, i: (0, h)),
                  pl.BlockSpec((8, wl), lambda h, i: (0, h)),
                  pl.BlockSpec((hps, hd, hd), lambda h, i: (h, 0, 0)),
                  pl.BlockSpec((hps, hd, hd), lambda h, i: (h, 0, 0))],
        out_specs=[pl.BlockSpec((ts, wl), lambda h, i: (i, h)),
                   pl.BlockSpec((ts, wl), lambda h, i: (i, h))],
        out_shape=[jax.ShapeDtypeStruct((s, 2 * d), BF16), jax.ShapeDtypeStruct((s, d), F32)],
        scratch_shapes=[pltpu.VMEM((ts + HALO, wl), F32), pltpu.VMEM((8, wl), F32)],
        compiler_params=_params(2),
    )(proj, cw, vec, wa, wx)


def _inv_count(i, ts, lanes, win):
    t = i * ts + lax.broadcasted_iota(jnp.int32, (ts, lanes), 0)
    return 1.0 / jnp.minimum(t + 1, win).astype(F32)


def _window_sum(ext, win, forward):
    rows = ext.shape[0]
    s, d = ext, 1
    while d < win:
        s = s + pltpu.roll(s, d if forward else rows - d, 0)
        d *= 2
    return s


def _pooled(xbuf, xt, lanes, win, inv_cnt, ts):
    acc = _window_sum(xbuf[:, lanes], win, True)[HALO:, :]
    return acc * inv_cnt - xt


def _pool_fwd(proj, ycat, pw, vec, layer):
    _, s, d = proj.shape
    ng, gd, _ = pw.shape
    ts = _tile(s, MATMUL_ROWS)

    def body(proj_ref, ycat_in, pw_ref, vec_ref, ycat_ref, xbuf):
        del ycat_in
        i = pl.program_id(0)

        @pl.when(i == 0)
        def _():
            xbuf[0:HALO, :] = jnp.zeros((HALO, d), F32)

        xbuf[pl.ds(HALO, ts), :] = proj_ref[0]
        for g in range(ng):
            lanes = slice(g * gd, (g + 1) * gd)
            win = 2 << g
            xt = proj_ref[0, :, lanes]
            pooled = _pooled(xbuf, xt, lanes, win, _inv_count(i, ts, gd, win), ts).astype(BF16)
            z = jnp.dot(pooled, pw_ref[g], preferred_element_type=F32) + vec_ref[0:1, lanes]
            gg = proj_ref[1, :, lanes]
            ycat_ref[:, lanes] = (z * vec_ref[1:2, lanes] * (gg * _sigmoid(gg))).astype(BF16)
        xbuf[0:HALO, :] = xbuf[pl.ds(ts, HALO), :]

    return pl.pallas_call(
        body, name=f"pool_fwd_l{layer}", grid=(s // ts,),
        in_specs=[pl.BlockSpec((2, ts, d), lambda i: (1, i, 0)),
                  ANY,
                  pl.BlockSpec((ng, gd, gd), lambda i: (0, 0, 0)),
                  pl.BlockSpec((8, d), lambda i: (0, 0))],
        out_specs=pl.BlockSpec((ts, d), lambda i: (i, 1)),
        out_shape=jax.ShapeDtypeStruct((s, 2 * d), BF16),
        input_output_aliases={1: 0},
        scratch_shapes=[pltpu.VMEM((ts + HALO, d), F32)],
        compiler_params=_params(1),
    )(proj, ycat, pw, vec)


def _outproj_fwd(ycat, w_all, x, vec, target, layer):
    s, d = x.shape
    nk, kd = w_all.shape[0], w_all.shape[1]
    ts = _tile(s, MATMUL_ROWS)
    last = target is not None

    def body(*refs):
        if last:
            ycat_ref, w_ref, x_ref, vec_ref, tgt_ref, y_ref, xo_ref, sq_ref = refs
        else:
            ycat_ref, w_ref, x_ref, vec_ref, y_ref, xo_ref = refs
        y = jnp.dot(ycat_ref[:, 0:kd], w_ref[0], preferred_element_type=F32)
        for k in range(1, nk):
            y = y + jnp.dot(ycat_ref[:, k * kd:(k + 1) * kd], w_ref[k], preferred_element_type=F32)
        y_ref[...] = y
        rs = lax.rsqrt(jnp.mean(y * y, axis=-1, keepdims=True) + NORM_EPS)
        xo = x_ref[...] + vec_ref[2:3, :] * (y * rs * vec_ref[4:5, :])
        if last:
            err = xo - tgt_ref[...]
            xo_ref[...] = err * (1.0 / d)

            @pl.when(pl.program_id(0) == 0)
            def _():
                sq_ref[...] = jnp.zeros_like(sq_ref)

            sq_ref[...] += jnp.sum(err * err)
        else:
            xo_ref[...] = xo

    row = pl.BlockSpec((ts, d), lambda i: (i, 0))
    in_specs = [pl.BlockSpec((ts, nk * kd), lambda i: (i, 0)),
                pl.BlockSpec((nk, kd, d), lambda i: (0, 0, 0)),
                row, pl.BlockSpec((8, d), lambda i: (0, 0))]
    out_specs = [row, row]
    out_shape = [jax.ShapeDtypeStruct((s, d), F32), jax.ShapeDtypeStruct((s, d), F32)]
    args = [ycat, w_all, x, vec]
    if last:
        in_specs.append(row)
        args.append(target)
        out_specs.append(pl.BlockSpec((8, 128), lambda i: (0, 0)))
        out_shape.append(jax.ShapeDtypeStruct((8, 128), F32))
    out = pl.pallas_call(
        body, name=f"outproj_fwd_l{layer}", grid=(s // ts,),
        in_specs=in_specs, out_specs=out_specs, out_shape=out_shape,
        compiler_params=_params(1),
    )(*args)
    return (out[0], out[1], out[2]) if last else (out[0], out[1], None)


def _outproj_bwd(dxo, y, ycat, w_all, vec, layer, after):
    s, d = dxo.shape
    nk, kd = w_all.shape[0], w_all.shape[1]
    ts = _tile(s, MATMUL_ROWS)
    nt = s // ts

    def body(dxo_ref, y_ref, ycat_ref, w_ref, vec_ref, after_ref, dycat_ref, dw_ref, dvec_ref, acc):
        del after_ref
        i = pl.program_id(0)

        @pl.when(i == 0)
        def _():
            acc[...] = jnp.zeros_like(acc)
            dvec_ref[...] = jnp.zeros_like(dvec_ref)

        yt = y_ref[...]
        rs = lax.rsqrt(jnp.mean(yt * yt, axis=-1, keepdims=True) + NORM_EPS)
        yhat = yt * rs
        gate, gpost = vec_ref[2:3, :], vec_ref[4:5, :]
        dxo_t = dxo_ref[...]
        dyn = dxo_t * gate
        dvec_ref[0:1, :] += _colsum(dxo_t * (yhat * gpost))
        dvec_ref[1:2, :] += _colsum(dyn * yhat)
        t = dyn * gpost
        dy = (rs * (t - yhat * jnp.mean(t * yhat, axis=-1, keepdims=True))).astype(BF16)
        for k in range(nk):
            cols = slice(k * kd, (k + 1) * kd)
            dycat_ref[:, cols] = lax.dot_general(dy, w_ref[k], NT_DIMS, preferred_element_type=F32)
            acc[k] += lax.dot_general(ycat_ref[:, cols], dy, TN_DIMS, preferred_element_type=F32)

        @pl.when(i == nt - 1)
        def _():
            dw_ref[...] = acc[...].astype(BF16)

    row = pl.BlockSpec((ts, d), lambda i: (i, 0))
    wide = pl.BlockSpec((ts, nk * kd), lambda i: (i, 0))
    return pl.pallas_call(
        body, name=f"outproj_bwd_l{layer}", grid=(nt,),
        in_specs=[row, row, wide,
                  pl.BlockSpec((nk, kd, d), lambda i: (0, 0, 0)),
                  pl.BlockSpec((8, d), lambda i: (0, 0)), ANY],
        out_specs=[wide,
                   pl.BlockSpec((nk, kd, d), lambda i: (0, 0, 0)),
                   pl.BlockSpec((8, d), lambda i: (0, 0))],
        out_shape=[jax.ShapeDtypeStruct((s, nk * kd), F32),
                   jax.ShapeDtypeStruct((nk, kd, d), BF16),
                   jax.ShapeDtypeStruct((8, d), F32)],
        scratch_shapes=[pltpu.VMEM((nk, kd, d), F32)],
        compiler_params=_params(1),
    )(dxo, y, ycat, w_all, vec, after)


def _halo_index(ts, nt):
    return lambda j: jnp.maximum((nt - 1 - j) * (ts // HALO) - 1, 0)


def _rnn_bwd(proj, hs, dycat, cw, vec, wa, wx, layer, after):
    _, s, d = proj.shape
    nh, hd, _ = wa.shape
    ts = _tile(s, BWD_SCAN_ROWS)
    nt = s // ts
    halo = _halo_index(ts, nt)
    hps = BWD_HEADS_PER_STEP
    wl = hps * hd

    def body(proj_ref, xh_ref, hs_ref, hsh_ref, dy_ref, cw_ref, vec_ref, wa_ref, wx_ref, after_ref,
             dproj_ref, dgates_ref, dvec_ref, xbuf, hbuf, dubuf, carry, dw_acc):
        del after_ref
        j = pl.program_id(1)
        first_tile = j == nt - 1

        @pl.when(j == 0)
        def _():
            dubuf[pl.ds(ts, HALO), :] = jnp.zeros((HALO, wl), F32)
            carry[...] = jnp.zeros_like(carry)
            dw_acc[...] = jnp.zeros_like(dw_acc)
            dvec_ref[...] = jnp.zeros_like(dvec_ref)

        xbuf[0:HALO, :] = jnp.where(first_tile, 0.0, xh_ref[0])
        xbuf[pl.ds(HALO, ts), :] = proj_ref[0]
        hbuf[0:HALO, :] = jnp.where(first_tile, 0.0, hsh_ref[...])
        hbuf[pl.ds(HALO, ts), :] = hs_ref[...]

        for hh in range(hps):
            lanes = slice(hh * hd, (hh + 1) * hd)
            wa, wx = wa_ref[hh], wx_ref[hh]
            hs = hs_ref[:, lanes]
            u = _conv(xbuf, cw_ref, vec_ref, lanes, ts)
            ub, r, ig, sp, log_a = _rnn_gates(u, wa, wx, vec_ref, lanes)
            a = jnp.exp(log_a)
            e2 = jnp.exp(2.0 * log_a)
            one_minus_a2 = _neg_expm1(2.0 * log_a)
            inv_mult = lax.rsqrt(one_minus_a2)
            mult = one_minus_a2 * inv_mult

            g = proj_ref[1, :, lanes]
            sg = _sigmoid(g)
            dyc = dy_ref[:, lanes]
            dproj_ref[1, :, lanes] = (dyc * hs * (sg * (1.0 + g * (1.0 - sg)))).astype(BF16)

            row = lax.broadcasted_iota(jnp.int32, (ts, hd), 0)
            dhs = dyc * (g * sg) + jnp.where(row == ts - 1, carry[0:1, lanes], 0.0)
            dh = _scan_rev(_shift_up(a, 1, 0.0), dhs)
            carry[:, lanes] = (a * dh)[0:8, :]

            h_prev = hbuf[pl.ds(HALO - 1, ts), lanes]
            dlog_a = dh * h_prev * a - dh * (ig * u) * (e2 * inv_mult)
            di = dh * mult * u
            dzr = dlog_a * ((-LRU_C) * sp) * (r * (1.0 - r))
            dzi = di * (ig * (1.0 - ig))
            dvec_ref[3:4, lanes] += _colsum(dlog_a * r) * (LRU_C * _sigmoid(-vec_ref[3:4, lanes]))
            dvec_ref[1:2, lanes] += _colsum(dzr)
            dvec_ref[2:3, lanes] += _colsum(dzi)
            dzr_b, dzi_b = dzr.astype(BF16), dzi.astype(BF16)
            dw_acc[0, hh] += lax.dot_general(ub, dzr_b, TN_DIMS, preferred_element_type=F32)
            dw_acc[1, hh] += lax.dot_general(ub, dzi_b, TN_DIMS, preferred_element_type=F32)
            du = (dh * mult * ig
                  + lax.dot_general(dzr_b, wa, NT_DIMS, preferred_element_type=F32)
                  + lax.dot_general(dzi_b, wx, NT_DIMS, preferred_element_type=F32))
            dvec_ref[0:1, lanes] += _colsum(du)
            for k in range(CONV_WIDTH):
                dvec_ref[4 + k:5 + k, lanes] += _colsum(du * xbuf[pl.ds(HALO - (CONV_WIDTH - 1) + k, ts), lanes])

            dubuf[0:ts, lanes] = du
            dx = cw_ref[CONV_WIDTH - 1:CONV_WIDTH, lanes] * du
            for k in range(CONV_WIDTH - 1):
                dx = dx + cw_ref[k:k + 1, lanes] * dubuf[pl.ds(CONV_WIDTH - 1 - k, ts), lanes]
            dproj_ref[0, :, lanes] = dx.astype(BF16)
        dubuf[pl.ds(ts, HALO), :] = dubuf[0:HALO, :]

        @pl.when(first_tile)
        def _():
            dgates_ref[...] = dw_acc[...].astype(BF16)

    rev = lambda h, j: (nt - 1 - j, h)
    return pl.pallas_call(
        body, name=f"rnn_bwd_l{layer}", grid=(nh // hps, nt),
        in_specs=[pl.BlockSpec((2, ts, wl), lambda h, j: (0, nt - 1 - j, h)),
                  pl.BlockSpec((1, HALO, wl), lambda h, j: (0, halo(j), h)),
                  pl.BlockSpec((ts, wl), rev),
                  pl.BlockSpec((HALO, wl), lambda h, j: (halo(j), h)),
                  pl.BlockSpec((ts, wl), rev),
                  pl.BlockSpec((CONV_WIDTH, wl), lambda h, j: (0, h)),
                  pl.BlockSpec((8, wl), lambda h, j: (0, h)),
                  pl.BlockSpec((hps, hd, hd), lambda h, j: (h, 0, 0)),
                  pl.BlockSpec((hps, hd, hd), lambda h, j: (h, 0, 0)), ANY],
        out_specs=[pl.BlockSpec((2, ts, wl), lambda h, j: (0, nt - 1 - j, h)),
                   pl.BlockSpec((2, hps, hd, hd), lambda h, j: (0, h, 0, 0)),
                   pl.BlockSpec((16, wl), lambda h, j: (0, h))],
        out_shape=[jax.ShapeDtypeStruct((4, s, d), BF16),
                   jax.ShapeDtypeStruct((2, nh, hd, hd), BF16),
                   jax.ShapeDtypeStruct((16, d), F32)],
        scratch_shapes=[pltpu.VMEM((ts + HALO, wl), F32), pltpu.VMEM((ts + HALO, wl), F32),
                        pltpu.VMEM((ts + HALO, wl), F32), pltpu.VMEM((8, wl), F32),
                        pltpu.VMEM((2, hps, hd, hd), F32)],
        compiler_params=_params(2),
    )(proj, proj, hs, hs, dycat, cw, vec, wa, wx, after)


def _pool_bwd(proj, dycat, dproj, pw, vec, layer):
    _, s, d = proj.shape
    ng, gd, _ = pw.shape
    ts = _tile(s, MATMUL_ROWS)
    nt = s // ts
    halo = _halo_index(ts, nt)

    def body(proj_ref, xh_ref, dy_ref, dproj_in, pw_ref, vec_ref, dproj_ref, dpw_ref, dvec_ref, xbuf, qbuf, acc):
        del dproj_in
        j = pl.program_id(0)
        i = nt - 1 - j

        @pl.when(j == 0)
        def _():
            qbuf[pl.ds(ts, HALO), :] = jnp.zeros((HALO, d), F32)
            acc[...] = jnp.zeros_like(acc)
            dvec_ref[...] = jnp.zeros_like(dvec_ref)

        xbuf[0:HALO, :] = jnp.where(i == 0, 0.0, xh_ref[0])
        xbuf[pl.ds(HALO, ts), :] = proj_ref[0]
        for g in range(ng):
            lanes = slice(g * gd, (g + 1) * gd)
            win = 2 << g
            xt = proj_ref[0, :, lanes]
            inv_cnt = _inv_count(i, ts, gd, win)
            pooled = _pooled(xbuf, xt, lanes, win, inv_cnt, ts).astype(BF16)
            z = jnp.dot(pooled, pw_ref[g], preferred_element_type=F32) + vec_ref[0:1, lanes]
            scale = vec_ref[1:2, lanes]
            gg = proj_ref[1, :, lanes]
            sg = _sigmoid(gg)
            dyc = dy_ref[:, lanes]
            dyp = dyc * (gg * sg)
            dproj_ref[1, :, lanes] = (dyc * (z * scale) * (sg * (1.0 + gg * (1.0 - sg)))).astype(BF16)
            dvec_ref[1:2, lanes] += _colsum(dyp * z)
            dz = dyp * scale
            dvec_ref[0:1, lanes] += _colsum(dz)
            dz_b = dz.astype(BF16)
            acc[g] += lax.dot_general(pooled, dz_b, TN_DIMS, preferred_element_type=F32)
            dpooled = lax.dot_general(dz_b, pw_ref[g], NT_DIMS, preferred_element_type=F32)

            qbuf[0:ts, lanes] = dpooled * inv_cnt
            dx = _window_sum(qbuf[:, lanes], win, False)[0:ts, :] - dpooled
            dproj_ref[0, :, lanes] = dx.astype(BF16)
        qbuf[pl.ds(ts, HALO), :] = qbuf[0:HALO, :]

        @pl.when(j == nt - 1)
        def _():
            dpw_ref[...] = acc[...].astype(BF16)

    return pl.pallas_call(
        body, name=f"pool_bwd_l{layer}", grid=(nt,),
        in_specs=[pl.BlockSpec((2, ts, d), lambda j: (1, nt - 1 - j, 0)),
                  pl.BlockSpec((1, HALO, d), lambda j: (2, halo(j), 0)),
                  pl.BlockSpec((ts, d), lambda j: (nt - 1 - j, 1)),
                  ANY,
                  pl.BlockSpec((ng, gd, gd), lambda j: (0, 0, 0)),
                  pl.BlockSpec((8, d), lambda j: (0, 0))],
        out_specs=[pl.BlockSpec((2, ts, d), lambda j: (1, nt - 1 - j, 0)),
                   pl.BlockSpec((ng, gd, gd), lambda j: (0, 0, 0)),
                   pl.BlockSpec((8, d), lambda j: (0, 0))],
        out_shape=[jax.ShapeDtypeStruct((4, s, d), BF16),
                   jax.ShapeDtypeStruct((ng, gd, gd), BF16),
                   jax.ShapeDtypeStruct((8, d), F32)],
        input_output_aliases={3: 0},
        scratch_shapes=[pltpu.VMEM((ts + HALO, d), F32), pltpu.VMEM((ts + HALO, d), F32),
                        pltpu.VMEM((ng, gd, gd), F32)],
        compiler_params=_params(1),
    )(proj, proj, dycat, dproj, pw, vec)


def _inproj_bwd_x(dproj, w_all, x, dxo, vec, layer, after):
    s, d = x.shape
    p = w_all.shape[2]
    ts = _tile(s, MATMUL_ROWS)

    def body(dp_ref, w_ref, x_ref, dxo_ref, vec_ref, after_ref, dx_ref, dvec_ref):
        del after_ref

        @pl.when(pl.program_id(0) == 0)
        def _():
            dvec_ref[...] = jnp.zeros_like(dvec_ref)

        dh = lax.dot_general(dp_ref[0], w_ref[0], NT_DIMS, preferred_element_type=F32)
        for k in range(1, 4):
            dh = dh + lax.dot_general(dp_ref[k], w_ref[k], NT_DIMS, preferred_element_type=F32)
        _, xn, rs = _prenorm(x_ref[...], vec_ref)
        gpre, scale1 = vec_ref[3:4, :], 1.0 + vec_ref[1:2, :]
        dvec_ref[0:1, :] += _colsum(dh)
        dvec_ref[1:2, :] += _colsum(dh * (xn * gpre))
        dvec_ref[2:3, :] += _colsum(dh * (xn * scale1))
        t = dh * (gpre * scale1)
        dx_ref[...] = dxo_ref[...] + rs * (t - xn * jnp.mean(t * xn, axis=-1, keepdims=True))

    row = pl.BlockSpec((ts, d), lambda i: (i, 0))
    return pl.pallas_call(
        body, name=f"inproj_bwd_x_l{layer}", grid=(s // ts,),
        in_specs=[pl.BlockSpec((4, ts, p), lambda i: (0, i, 0)),
                  pl.BlockSpec((4, d, p), lambda i: (0, 0, 0)),
                  row, row, pl.BlockSpec((8, d), lambda i: (0, 0)), ANY],
        out_specs=[row, pl.BlockSpec((8, d), lambda i: (0, 0))],
        out_shape=[jax.ShapeDtypeStruct((s, d), F32), jax.ShapeDtypeStruct((8, d), F32)],
        compiler_params=_params(1),
    )(dproj, w_all, x, dxo, vec, after)


def _inproj_bwd_w(dproj, x, vec, layer, after):
    s, d = x.shape
    p = dproj.shape[2]
    ts = _tile(s, MATMUL_ROWS)
    nt = s // ts

    def body(dp_ref, x_ref, vec_ref, after_ref, dw_ref, acc):
        del after_ref
        i = pl.program_id(0)

        @pl.when(i == 0)
        def _():
            acc[...] = jnp.zeros_like(acc)

        h, _, _ = _prenorm(x_ref[...], vec_ref)
        hb = h.astype(BF16)
        for k in range(4):
            acc[k] += lax.dot_general(hb, dp_ref[k], TN_DIMS, preferred_element_type=F32)

        @pl.when(i == nt - 1)
        def _():
            dw_ref[...] = acc[...].astype(BF16)

    return pl.pallas_call(
        body, name=f"inproj_bwd_w_l{layer}", grid=(nt,),
        in_specs=[pl.BlockSpec((4, ts, p), lambda i: (0, i, 0)),
                  pl.BlockSpec((ts, d), lambda i: (i, 0)),
                  pl.BlockSpec((8, d), lambda i: (0, 0)), ANY],
        out_specs=pl.BlockSpec((4, d, p), lambda i: (0, 0, 0)),
        out_shape=jax.ShapeDtypeStruct((4, d, p), BF16),
        scratch_shapes=[pltpu.VMEM((4, d, p), F32)],
        compiler_params=_params(1),
    )(dproj, x, vec, after)


def _sum_slots(stacked, name, out_dtype=F32):
    n, rows, cols = stacked.shape
    tr = _row_tile(rows)

    def body(in_ref, out_ref):
        total = in_ref[0].astype(F32)
        for b in range(1, n):
            total = total + in_ref[b].astype(F32)
        out_ref[...] = total.astype(out_dtype)

    return pl.pallas_call(
        body, name=name, grid=(rows // tr,),
        in_specs=[pl.BlockSpec((n, tr, cols), lambda i: (0, i, 0))],
        out_specs=pl.BlockSpec((tr, cols), lambda i: (i, 0)),
        out_shape=jax.ShapeDtypeStruct((rows, cols), out_dtype),
        compiler_params=_params(1),
    )(stacked)


def _adam_update(w, m, v, g):
    m_new = ADAM_B1 * m + (1.0 - ADAM_B1) * g
    v_new = ADAM_B2 * v + (1.0 - ADAM_B2) * (g * g)
    m_hat = m_new / (1.0 - ADAM_B1 ** ADAM_STEP)
    v_hat = v_new / (1.0 - ADAM_B2 ** ADAM_STEP)
    return (-ADAM_LR) * (m_hat / (jnp.sqrt(v_hat) + ADAM_EPS) + ADAM_WD * w), m_new, v_new


def _adamw_layer(w, m, v, grads, layer, prev, name, grad_row_offset=0):
    nl = w.shape[0]
    cols = w.shape[-1]
    rows = w.size // (nl * cols)
    tr = _row_tile(rows)
    off = layer * (rows // tr)
    g_off = grad_row_offset // tr
    n = len(grads)
    n_prev = 0 if prev is None else 4

    def body(*refs):
        w_ref, m_ref, v_ref = refs[:3]
        g_refs = refs[3:3 + n]
        g_out, d_out, m_out, v_out = refs[3 + n + n_prev:]
        g = g_refs[0][...].astype(F32)
        for r in g_refs[1:]:
            g = g + r[...].astype(F32)
        g_out[...] = g
        d_out[...], m_out[...], v_out[...] = _adam_update(w_ref[...], m_ref[...], v_ref[...], g)

    mine = pl.BlockSpec((tr, cols), lambda i: (off + i, 0))
    args = [a.reshape(nl * rows, cols) for a in (w, m, v)] + [g.reshape(-1, cols) for g in grads]
    outs = pl.pallas_call(
        body, name=name, grid=(rows // tr,),
        in_specs=[mine] * 3 + [pl.BlockSpec((tr, cols), lambda i: (g_off + i, 0))] * n + [ANY] * n_prev,
        out_specs=[mine] * 4,
        out_shape=[jax.ShapeDtypeStruct((nl * rows, cols), F32)] * 4,
        input_output_aliases={3 + n + k: k for k in range(n_prev)},
        compiler_params=_params(1),
    )(*args, *(prev or ()))
    return tuple(outs)


def _into_slot(a, dtype, chip_arr, name, layer=None, after=None):
    rows, cols = a.shape[-2:]
    tr = _row_tile(rows)

    def body(chip_ref, a_ref, *rest):
        del chip_ref
        rest[-1][...] = a_ref[...].astype(dtype)

    if layer is None:
        in_spec = pl.BlockSpec((tr, cols), lambda i, chip: (i, 0))
    else:
        in_spec = pl.BlockSpec((None, tr, cols), lambda i, chip: (layer, i, 0))
    extra = [] if after is None else [after]
    return pl.pallas_call(
        body, name=name,
        grid_spec=pltpu.PrefetchScalarGridSpec(
            num_scalar_prefetch=1, grid=(rows // tr,),
            in_specs=[in_spec] + [ANY] * len(extra),
            out_specs=pl.BlockSpec((None, tr, cols), lambda i, chip: (chip[0], i, 0))),
        out_shape=jax.ShapeDtypeStruct((4, rows, cols), dtype),
        compiler_params=_params(1),
    )(chip_arr, a, *extra)


def _sum_owner(own, land, chip_arr, own_block, own_index, name):
    blk = land.shape[1:]
    tr = _row_tile(blk[-2])
    steps = blk[-2] // tr
    tile = (*blk[:-2], tr, blk[-1])
    lead = (0,) * (len(blk) - 2)

    def body(chip_ref, own_ref, l1, l2, l3, out_ref):
        del chip_ref
        total = (own_ref[...].astype(F32) + l1[...].astype(F32)) + (l2[...].astype(F32) + l3[...].astype(F32))
        out_ref[...] = total.astype(BF16)

    def landed(k):
        return pl.BlockSpec((None, *tile), lambda i, chip: (chip[0] ^ k, *lead, i, 0))

    return pl.pallas_call(
        body, name=name,
        grid_spec=pltpu.PrefetchScalarGridSpec(
            num_scalar_prefetch=1, grid=(steps,),
            in_specs=[pl.BlockSpec(own_block(tr), own_index), landed(1), landed(2), landed(3)],
            out_specs=pl.BlockSpec(tile, lambda i, chip: (*lead, i, 0))),
        out_shape=jax.ShapeDtypeStruct(blk, BF16),
        compiler_params=_params(1),
    )(chip_arr, own, land, land, land)


_WHOLE_VMEM = pltpu.CompilerParams(vmem_limit_bytes=V7X_VMEM_LIMIT_BYTES)


def _pack_vectors(modbuf, ada_b, pre_norm_g, post_norm_g, conv_b, gate_a_b, gate_x_b, lru_lambda):
    nl, d = pre_norm_g.shape
    n = modbuf.shape[2] // nl
    nh, hd = gate_a_b.shape[1], gate_a_b.shape[2]

    def body(mb_ref, ab_ref, pre_ref, post_ref, cb_ref, gab_ref, gxb_ref, lam_ref, *outs):
        for layer in range(nl):
            vec_ref, rvec_ref = outs[layer], outs[nl + layer]
            vec_ref[...] = jnp.zeros_like(vec_ref)
            rvec_ref[...] = jnp.zeros_like(rvec_ref)
            for k in range(4):
                piece = mb_ref[k, 0:1, layer * n:(layer + 1) * n] + ab_ref[layer:layer + 1, k * n:(k + 1) * n]
                lo = k * n
                while lo < (k + 1) * n:
                    row = lo // d
                    hi = min((row + 1) * d, (k + 1) * n)
                    vec_ref[row:row + 1, lo - row * d:hi - row * d] = piece[:, lo - k * n:hi - k * n]
                    lo = hi
            vec_ref[3:4, :] = pre_ref[layer:layer + 1, :]
            vec_ref[4:5, :] = post_ref[layer:layer + 1, :]
            rvec_ref[0:1, :] = cb_ref[layer:layer + 1, :]
            for h in range(nh):
                rvec_ref[1:2, h * hd:(h + 1) * hd] = gab_ref[layer, h:h + 1, :]
                rvec_ref[2:3, h * hd:(h + 1) * hd] = gxb_ref[layer, h:h + 1, :]
            rvec_ref[3:4, :] = lam_ref[layer:layer + 1, :]

    out = pl.pallas_call(
        body, name="pack_vectors", in_specs=[VMEM] * 8, out_specs=[VMEM] * (2 * nl),
        out_shape=[jax.ShapeDtypeStruct((8, d), F32)] * (2 * nl), compiler_params=_WHOLE_VMEM,
    )(modbuf, ada_b, pre_norm_g, post_norm_g, conv_b, gate_a_b, gate_x_b, lru_lambda)
    return list(out[:nl]), list(out[nl:])


def _pack_gathered(convw_g, poolb_g, pws, pool_scale, ng):
    nl, d = pool_scale.shape
    taps = convw_g.shape[1] // nl
    dq = convw_g.shape[2]
    gq, gd = poolb_g.shape[2], pws[0].shape[2]

    def body(cg_ref, pb_ref, *rest):
        pw_refs, ps_ref = rest[:nl], rest[nl]
        outs = rest[nl + 1:]
        for layer in range(nl):
            cw_ref, pvec_ref, pwf_ref = outs[layer], outs[nl + layer], outs[2 * nl + layer]
            pvec_ref[...] = jnp.zeros_like(pvec_ref)
            pvec_ref[1:2, :] = ps_ref[layer:layer + 1, :]
            for k in range(4):
                cw_ref[:, k * dq:(k + 1) * dq] = cg_ref[k, layer * taps:(layer + 1) * taps, :]
                for g in range(ng):
                    lo = g * gd + k * gq
                    pvec_ref[0:1, lo:lo + gq] = pb_ref[k, layer * ng + g:layer * ng + g + 1, :]
                    pwf_ref[g, k * gq:(k + 1) * gq, :] = pw_refs[layer][k, g * gq:(g + 1) * gq, :]

    out = pl.pallas_call(
        body, name="pack_gathered", in_specs=[VMEM] * (3 + nl), out_specs=[VMEM] * (3 * nl),
        out_shape=[jax.ShapeDtypeStruct((taps, d), F32)] * nl + [jax.ShapeDtypeStruct((8, d), F32)] * nl
        + [jax.ShapeDtypeStruct((ng, gd, gd), BF16)] * nl,
        compiler_params=_WHOLE_VMEM,
    )(convw_g, poolb_g, *pws, pool_scale)
    return list(out[:nl]), list(out[nl:2 * nl]), list(out[2 * nl:])


ROW_SHIFT, ROW_SCALE, ROW_PRE, ROW_GATE, ROW_POST = 0, 1, 2, 8, 9
ROW_CONV_B, ROW_GATE_A_B, ROW_GATE_X_B, ROW_LAMBDA, ROW_CONV_W = 16, 17, 18, 19, 20
ROW_POOL_B, ROW_POOL_SCALE, ROW_SQ = 32, 33, 40


def _adamw_small(totals, chip_arr, params):
    nl = len(totals)
    d = totals[0].shape[1]
    n_par = len(params)
    flat = [a for p in params for a in p]
    nh, hd = params[6][0].shape[1], params[6][0].shape[2]
    taps, dq = params[8][0].shape[1], params[8][0].shape[2]
    ng, gq = params[9][0].shape[1], params[9][0].shape[2]
    gd = d // ng

    def body(chip_ref, *refs):
        tot = refs[:nl]
        ins = refs[nl:nl + 3 * n_par]
        outs = refs[nl + 3 * n_par:]
        chip = chip_ref[0]

        def update(p, idx, g):
            delta, m_new, v_new = _adam_update(ins[3 * p][idx], ins[3 * p + 1][idx], ins[3 * p + 2][idx], g)
            outs[4 * p][idx] = g
            outs[4 * p + 1][idx] = delta
            outs[4 * p + 2][idx] = m_new
            outs[4 * p + 3][idx] = v_new

        def mine(candidates):
            g = candidates[0]
            for k in range(1, 4):
                g = jnp.where(chip == k, candidates[k], g)
            return g

        for layer in range(nl):
            t = tot[layer]
            row = (slice(layer, layer + 1), slice(None))
            for j, r in enumerate((ROW_SHIFT, ROW_SCALE, ROW_GATE)):
                update(0, (slice(layer, layer + 1), slice(j * d, (j + 1) * d)), t[r:r + 1, :])
            for p, r in ((1, ROW_PRE), (2, ROW_POST), (3, ROW_CONV_B), (4, ROW_LAMBDA), (5, ROW_POOL_SCALE)):
                update(p, row, t[r:r + 1, :])
            for h in range(nh):
                idx = (layer, slice(h, h + 1), slice(None))
                update(6, idx, t[ROW_GATE_A_B:ROW_GATE_A_B + 1, h * hd:(h + 1) * hd])
                update(7, idx, t[ROW_GATE_X_B:ROW_GATE_X_B + 1, h * hd:(h + 1) * hd])
            for k in range(taps):
                r = ROW_CONV_W + k
                update(8, (layer, slice(k, k + 1), slice(None)), mine([t[r:r + 1, c * dq:(c + 1) * dq] for c in range(4)]))
            for g in range(ng):
                cands = [t[ROW_POOL_B:ROW_POOL_B + 1, g * gd + c * gq:g * gd + (c + 1) * gq] for c in range(4)]
                update(9, (layer, slice(g, g + 1), slice(None)), mine(cands))

    out = pl.pallas_call(
        body, name="adamw_small",
        in_specs=[pl.BlockSpec(memory_space=pltpu.SMEM)] + [VMEM] * (nl + 3 * n_par),
        out_specs=[VMEM] * (4 * n_par),
        out_shape=[jax.ShapeDtypeStruct(p[0].shape, F32) for p in params for _ in range(4)],
        compiler_params=_WHOLE_VMEM,
    )(chip_arr, *totals, *flat)
    return [tuple(out[4 * p:4 * p + 4]) for p in range(n_par)]


def _adamw_ada_w_layer(c_t, slabs, chip_arr, w, m, v, layer, prev, name):
    nl, d, n = w.shape
    nb = c_t.shape[1]
    tr = _row_tile(d)
    off = layer * (d // tr)
    n_prev = 0 if prev is None else 4
    mod_rows = (ROW_SHIFT, ROW_SCALE, ROW_GATE)

    def body(chip_ref, c_ref, slab_ref, w_ref, m_ref, v_ref, *rest):
        g_out, d_out, m_out, v_out = rest[n_prev:n_prev + 4]
        dm = rest[-1]

        @pl.when(pl.program_id(0) == 0)
        def _():
            for k in range(4):
                @pl.when(chip_ref[0] == k)
                def _():
                    lo = k * n
                    while lo < (k + 1) * n:
                        hi = min((lo // d + 1) * d, (k + 1) * n)
                        row = mod_rows[lo // d]
                        for b in range(nb):
                            dm[b:b + 1, lo - k * n:hi - k * n] = slab_ref[b, row:row + 1, lo % d:lo % d + hi - lo]
                        lo = hi

        g = c_ref[:, 0:1] * dm[0:1, :]
        for b in range(1, nb):
            g = g + c_ref[:, b:b + 1] * dm[b:b + 1, :]
        g_out[...] = g
        d_out[...], m_out[...], v_out[...] = _adam_update(w_ref[...], m_ref[...], v_ref[...], g)

    mine = pl.BlockSpec((tr, n), lambda i, chip: (off + i, 0))
    outs = pl.pallas_call(
        body, name=name,
        grid_spec=pltpu.PrefetchScalarGridSpec(
            num_scalar_prefetch=1, grid=(d // tr,),
            in_specs=[pl.BlockSpec((tr, nb), lambda i, chip: (i, 0)),
                      pl.BlockSpec(slabs.shape, lambda i, chip: (0, 0, 0))] + [mine] * 3 + [ANY] * n_prev,
            out_specs=[mine] * 4,
            scratch_shapes=[pltpu.VMEM((nb, n), F32)]),
        out_shape=[jax.ShapeDtypeStruct((nl * d, n), F32)] * 4,
        input_output_aliases={6 + k: k for k in range(n_prev)},
        compiler_params=_params(1),
    )(chip_arr, c_t, slabs, *[a.reshape(nl * d, n) for a in (w, m, v)], *(prev or ()))
    return tuple(outs)


def _place():
    x, y, c = lax.axis_index("x"), lax.axis_index("y"), lax.axis_index("c")
    return x, y, c


OTHER_CHIPS = ((1, 0), (0, 1), (1, 1))
OTHER_DEVICES = tuple((fx, fy, fc) for fx in (0, 1) for fy in (0, 1) for fc in (0, 1))[1:]


def _mod_exchange(c_row, ada_w, after):
    nl, d, n = ada_w.shape

    def body(c_ref, w_ref, after_ref, cbuf, modbuf, token, cblk, mres, send_a, recv_a, send_c, recv_c):
        del after_ref
        token[...] = jnp.zeros_like(token)
        x, y, c = _place()
        me = 4 * x + 2 * y + c
        chip = 2 * x + y
        cv = c_ref[...]
        cblk[...] = jnp.zeros_like(cblk)
        cblk[0:1, :] = cv * _sigmoid(cv)

        def rows_of(dev):
            return cbuf.at[pl.ds(pl.multiple_of(8 * dev, 8), 8), :]

        cbuf[pl.ds(pl.multiple_of(8 * me, 8), 8), :] = cblk[...]
        sends = []
        for j, (fx, fy, fc) in enumerate(OTHER_DEVICES):
            cp = pltpu.make_async_remote_copy(
                src_ref=cblk, dst_ref=rows_of(me), send_sem=send_a.at[j], recv_sem=recv_a.at[j],
                device_id=(x ^ fx, y ^ fy, c ^ fc), device_id_type=MESH)
            cp.start()
            sends.append(cp)
        for j, (fx, fy, fc) in enumerate(OTHER_DEVICES):
            peer = 4 * (x ^ fx) + 2 * (y ^ fy) + (c ^ fc)
            pltpu.make_async_remote_copy(
                src_ref=cblk, dst_ref=rows_of(peer), send_sem=send_a.at[j], recv_sem=recv_a.at[j],
                device_id=(x ^ fx, y ^ fy, c ^ fc), device_id_type=MESH).wait_recv()
        for cp in sends:
            cp.wait_send()

        call = cbuf[...]
        for layer in range(nl):
            mres[:, layer * n:(layer + 1) * n] = jnp.dot(
                call, w_ref[layer], preferred_element_type=F32, precision=lax.Precision.HIGHEST)

        def block_of(dev):
            return mres.at[pl.ds(pl.multiple_of(8 * dev, 8), 8), :]

        modbuf[chip] = mres[pl.ds(pl.multiple_of(8 * me, 8), 8), :]
        sends = []
        for j, (fx, fy) in enumerate(OTHER_CHIPS):
            peer = 4 * (x ^ fx) + 2 * (y ^ fy) + c
            cp = pltpu.make_async_remote_copy(
                src_ref=block_of(peer), dst_ref=modbuf.at[chip], send_sem=send_c.at[j], recv_sem=recv_c.at[j],
                device_id=(x ^ fx, y ^ fy, c), device_id_type=MESH)
            cp.start()
            sends.append(cp)
        for j, (fx, fy) in enumerate(OTHER_CHIPS):
            pltpu.make_async_remote_copy(
                src_ref=block_of(me), dst_ref=modbuf.at[2 * (x ^ fx) + (y ^ fy)],
                send_sem=send_c.at[j], recv_sem=recv_c.at[j],
                device_id=(x ^ fx, y ^ fy, c), device_id_type=MESH).wait_recv()
        for cp in sends:
            cp.wait_send()

    return pl.pallas_call(
        body, name="mod_exchange", in_specs=[VMEM, VMEM, ANY], out_specs=[VMEM, VMEM, VMEM],
        out_shape=[jax.ShapeDtypeStruct((64, d), F32), jax.ShapeDtypeStruct((4, 8, nl * n), F32),
                   jax.ShapeDtypeStruct((8, 128), F32)],
        scratch_shapes=[pltpu.VMEM((8, d), F32), pltpu.VMEM((64, nl * n), F32),
                        pltpu.SemaphoreType.DMA((7,)), pltpu.SemaphoreType.DMA((7,)),
                        pltpu.SemaphoreType.DMA((3,)), pltpu.SemaphoreType.DMA((3,))],
        compiler_params=pltpu.CompilerParams(vmem_limit_bytes=V7X_VMEM_LIMIT_BYTES, has_side_effects=True),
    )(c_row, ada_w, after)


def _in_hbm(a):
    return pltpu.with_memory_space_constraint(a, pltpu.HBM)


def _gather_copies(lands, split, over_ici):
    x, y, c = _place()
    chip = 2 * x + y
    out = []
    for t, land in enumerate(lands):
        half = land.shape[1] // 2
        mine = pl.ds(pl.multiple_of(c * half, half), half)
        theirs = pl.ds(pl.multiple_of((1 - c) * half, half), half)
        for j, (fx, fy) in enumerate(OTHER_CHIPS):
            them = 2 * (x ^ fx) + (y ^ fy)
            if over_ici and split[t]:
                out.append((land.at[chip, mine], land.at[chip, mine], land.at[them, mine], (x ^ fx, y ^ fy, c), 3 * t + j))
            elif over_ici:
                out.append((land.at[chip], land.at[chip], land.at[them], (x ^ fx, y ^ fy, c), 3 * t + j))
            elif split[t]:
                out.append((land.at[them, mine], land.at[them, mine], land.at[them, theirs], (x, y, 1 - c), 3 * t + j))
    return out


def _gather_start(lands, groups, split, name):
    n, ngr = len(lands), len(groups)

    def body(*refs):
        sems = refs[n:n + 2 * ngr]
        for gi, idxs in enumerate(groups):
            for src, dst, _, peer, k in _gather_copies([refs[i] for i in idxs], [split[i] for i in idxs], True):
                pltpu.make_async_remote_copy(src_ref=src, dst_ref=dst, send_sem=sems[2 * gi].at[k],
                                             recv_sem=sems[2 * gi + 1].at[k], device_id=peer, device_id_type=MESH).start()
        refs[-1][...] = jnp.zeros_like(refs[-1])

    sem_shapes = []
    for idxs in groups:
        sem_shapes += [pltpu.SemaphoreType.DMA((3 * len(idxs),))] * 2
    out = pl.pallas_call(
        body, name=name,
        in_specs=[HBM] * n, out_specs=[SEM] * (2 * ngr) + [HBM] * n + [VMEM],
        out_shape=sem_shapes + [pltpu.HBM(a.shape, a.dtype) for a in lands] + [jax.ShapeDtypeStruct((8, 128), F32)],
        input_output_aliases={i: 2 * ngr + i for i in range(n)},
        compiler_params=pltpu.CompilerParams(has_side_effects=DATAFLOW_EFFECT),
    )(*[_in_hbm(a) for a in lands])
    sems = [(out[2 * gi], out[2 * gi + 1]) for gi in range(ngr)]
    return sems, list(out[2 * ngr:2 * ngr + n]), out[-1]


def _gather_forward(lands, split, sems, after, name):
    n = len(lands)

    def body(*refs):
        ici_send, ici_recv = refs[n], refs[n + 1]
        fwd_send, fwd_recv = refs[n + 3], refs[n + 4]
        forwards = {k: (src, dst, peer) for src, dst, _, peer, k in _gather_copies(refs[:n], split, False)}
        for src, _, landed, peer, k in _gather_copies(refs[:n], split, True):
            cp = pltpu.make_async_remote_copy(src_ref=src, dst_ref=landed, send_sem=ici_send.at[k], recv_sem=ici_recv.at[k],
                                              device_id=peer, device_id_type=MESH)
            cp.wait_recv()
            if k in forwards:
                fsrc, fdst, fpeer = forwards[k]
                pltpu.make_async_remote_copy(src_ref=fsrc, dst_ref=fdst, send_sem=fwd_send.at[k], recv_sem=fwd_recv.at[k],
                                             device_id=fpeer, device_id_type=MESH).start()
            cp.wait_send()

    out = pl.pallas_call(
        body, name=name,
        in_specs=[HBM] * n + [SEM, SEM, ANY], out_specs=[SEM, SEM] + [HBM] * n,
        out_shape=[pltpu.SemaphoreType.DMA((3 * n,))] * 2 + [pltpu.HBM(a.shape, a.dtype) for a in lands],
        input_output_aliases={i: 2 + i for i in range(n)},
        compiler_params=pltpu.CompilerParams(has_side_effects=DATAFLOW_EFFECT),
    )(*lands, sems[0], sems[1], after)
    return (out[0], out[1]), list(out[2:])


def _gather_wait(lands, split, sems, name):
    n = len(lands)

    def body(*refs):
        send_sems, recv_sems = refs[n], refs[n + 1]
        for src, _, landed, peer, k in _gather_copies(refs[:n], split, False):
            cp = pltpu.make_async_remote_copy(src_ref=src, dst_ref=landed, send_sem=send_sems.at[k], recv_sem=recv_sems.at[k],
                                              device_id=peer, device_id_type=MESH)
            cp.wait_send()
            cp.wait_recv()

    out = pl.pallas_call(
        body, name=name,
        in_specs=[HBM] * n + [SEM, SEM], out_specs=[HBM] * n,
        out_shape=[pltpu.HBM(a.shape, a.dtype) for a in lands],
        input_output_aliases={i: i for i in range(n)},
        compiler_params=pltpu.CompilerParams(has_side_effects=DATAFLOW_EFFECT),
    )(*lands, sems[0], sems[1])
    return list(out)


def _to_owner_copies(pairs, q):
    x, y, c = _place()
    chip = 2 * x + y
    out = []
    for t, (part, land) in enumerate(pairs):
        for j, (fx, fy) in enumerate(OTHER_CHIPS):
            owner = 2 * (x ^ fx) + (y ^ fy)
            if part.shape[0] == 4 and part.shape[1:] == land.shape[1:]:
                src = part.at[owner]
            else:
                src = part.at[:, pl.ds(pl.multiple_of(owner * q, q), q), :]
            out.append((src, land.at[chip], land.at[owner], (x ^ fx, y ^ fy, c), 3 * t + j))
    return out


def _to_all_copies(bufs, first_sem):
    x, y, c = _place()
    me = 4 * x + 2 * y + c
    out = []
    for t, buf in enumerate(bufs):
        for j, (fx, fy, fc) in enumerate(OTHER_DEVICES):
            them = 4 * (x ^ fx) + 2 * (y ^ fy) + (c ^ fc)
            out.append((buf.at[me], buf.at[me], buf.at[them], (x ^ fx, y ^ fy, c ^ fc), first_sem + 7 * t + j))
    return out


def _to_chips_copies(bufs, first_sem):
    x, y, c = _place()
    chip = 2 * x + y
    out = []
    for t, buf in enumerate(bufs):
        for j, (fx, fy) in enumerate(OTHER_CHIPS):
            them = 2 * (x ^ fx) + (y ^ fy)
            out.append((buf.at[chip], buf.at[chip], buf.at[them], (x ^ fx, y ^ fy, c), first_sem + 3 * t + j))
    return out


def _exchange_copies(refs, kinds, q):
    n_owner, n_chips = kinds
    pairs = list(zip(refs[:n_owner], refs[n_owner:2 * n_owner]))
    first_all = 3 * (n_owner + n_chips)
    return (_to_owner_copies(pairs, q) + _to_chips_copies(refs[2 * n_owner:2 * n_owner + n_chips], 3 * n_owner)
            + _to_all_copies(refs[2 * n_owner + n_chips:], first_all))


def _exchange_start(arrays, kinds, q, name):
    n = len(arrays)
    n_sems = 3 * (kinds[0] + kinds[1]) + 7 * (n - 2 * kinds[0] - kinds[1])

    def body(*refs):
        send_sems, recv_sems = refs[n], refs[n + 1]
        for src, dst, _, peer, k in _exchange_copies(refs[:n], kinds, q):
            pltpu.make_async_remote_copy(src_ref=src, dst_ref=dst, send_sem=send_sems.at[k], recv_sem=recv_sems.at[k],
                                         device_id=peer, device_id_type=MESH).start()
        refs[-1][...] = jnp.zeros_like(refs[-1])

    out = pl.pallas_call(
        body, name=name,
        in_specs=[HBM] * n, out_specs=[SEM, SEM] + [HBM] * n + [VMEM],
        out_shape=[pltpu.SemaphoreType.DMA((n_sems,))] * 2 + [pltpu.HBM(a.shape, a.dtype) for a in arrays]
        + [jax.ShapeDtypeStruct((8, 128), F32)],
        input_output_aliases={i: 2 + i for i in range(n)},
        compiler_params=pltpu.CompilerParams(has_side_effects=DATAFLOW_EFFECT),
    )(*[_in_hbm(a) for a in arrays])
    return (out[0], out[1]), list(out[2:2 + n]), out[-1]


def _exchange_wait(arrays, sems, kinds, q, after, name):
    n = len(arrays)

    def body(*refs):
        send_sems, recv_sems = refs[n], refs[n + 1]
        for src, _, landed, peer, k in _exchange_copies(refs[:n], kinds, q):
            cp = pltpu.make_async_remote_copy(src_ref=src, dst_ref=landed, send_sem=send_sems.at[k], recv_sem=recv_sems.at[k],
                                              device_id=peer, device_id_type=MESH)
            cp.wait_send()
            cp.wait_recv()

    out = pl.pallas_call(
        body, name=name,
        in_specs=[HBM] * n + [SEM, SEM, ANY], out_specs=[HBM] * n,
        out_shape=[pltpu.HBM(a.shape, a.dtype) for a in arrays],
        input_output_aliases={i: i for i in range(n)},
        compiler_params=pltpu.CompilerParams(has_side_effects=DATAFLOW_EFFECT),
    )(*arrays, sems[0], sems[1], after)
    return list(out)


def _sibling_swap(parts, layer):
    n = len(parts)

    def body(*refs):
        srcs, outs = refs[:n], refs[n:2 * n]
        send_sems, recv_sems = refs[2 * n:]
        x, y, c = _place()
        cps = [pltpu.make_async_remote_copy(
            src_ref=srcs[i], dst_ref=outs[i], send_sem=send_sems.at[i], recv_sem=recv_sems.at[i],
            device_id=(x, y, 1 - c), device_id_type=MESH) for i in range(n)]
        for cp in cps:
            cp.start()
        for cp in cps:
            cp.wait()

    return pl.pallas_call(
        body, name=f"sibling_swap_l{layer}", in_specs=[ANY] * n, out_specs=[ANY] * n,
        out_shape=[jax.ShapeDtypeStruct(a.shape, a.dtype) for a in parts],
        scratch_shapes=[pltpu.SemaphoreType.DMA((n,)), pltpu.SemaphoreType.DMA((n,))],
        compiler_params=pltpu.CompilerParams(has_side_effects=True),
    )(*parts)


def kernel(x, c, ada_w, ada_b, pre_norm_g, w_in, conv_w, conv_b, gate_a_w, gate_a_b, gate_x_w, gate_x_b, lru_lambda, pool_w, pool_b, pool_scale, w_out, post_norm_g, loss_target, m_ada_w, m_ada_b, m_pre_norm_g, m_w_in, m_conv_w, m_conv_b, m_gate_a_w, m_gate_a_b, m_gate_x_w, m_gate_x_b, m_lru_lambda, m_pool_w, m_pool_b, m_pool_scale, m_w_out, m_post_norm_g, v_ada_w, v_ada_b, v_pre_norm_g, v_w_in, v_conv_w, v_conv_b, v_gate_a_w, v_gate_a_b, v_gate_x_w, v_gate_x_b, v_lru_lambda, v_pool_w, v_pool_b, v_pool_scale, v_w_out, v_post_norm_g):
    nl, d, _ = ada_w.shape
    s = x.shape[1]
    nh, hd = gate_a_w.shape[1], gate_a_w.shape[2]
    ng, gq, gd = pool_w.shape[1], pool_w.shape[2], pool_w.shape[3]
    me = 4 * lax.axis_index("x") + 2 * lax.axis_index("y") + lax.axis_index("c")
    chip = 2 * lax.axis_index("x") + lax.axis_index("y")
    chip_arr = jnp.reshape(chip, (1,)).astype(jnp.int32)
    x0 = x.reshape(s, d)
    target = loss_target.reshape(s, d)
    p_in = w_in.shape[2]

    c_row = c.reshape(1, d)
    cbuf, modbuf, mod_token = _mod_exchange(c_row, ada_w, c_row)
    vecs, rvecs = _pack_vectors(modbuf, ada_b, pre_norm_g, post_norm_g, conv_b, gate_a_b, gate_x_b, lru_lambda)

    win = [_into_slot(w_in, BF16, chip_arr, f"slot_w_in_l{l}", l) for l in range(nl)]
    wout = [_into_slot(w_out, BF16, chip_arr, f"slot_w_out_l{l}", l) for l in range(nl)]
    pw = [_into_slot(pool_w.reshape(nl, ng * gq, gd), BF16, chip_arr, f"slot_pool_w_l{l}", l) for l in range(nl)]
    convw = _into_slot(conv_w.reshape(nl * CONV_WIDTH, d // 4), F32, chip_arr, "slot_conv_w", after=mod_token)
    poolb = _into_slot(pool_b.reshape(nl * ng, gq), F32, chip_arr, "slot_pool_b")
    lands = [win[0], convw, poolb, *pw, wout[0]]
    split = [True, False, False] + [True] * (nl + 1)
    groups = [[0], list(range(1, len(lands)))]
    for l in range(1, nl):
        groups.append([len(lands), len(lands) + 1])
        lands += [win[l], wout[l]]
        split += [True, True]
    sems, lands, _ = _gather_start(lands, groups, split, "weight_gather_start")
    wa_b, wx_b = gate_a_w.astype(BF16), gate_x_w.astype(BF16)

    def gathered(gi, after, tag):
        idxs = groups[gi]
        arrays, halves = [lands[i] for i in idxs], [split[i] for i in idxs]
        between, arrays = _gather_forward(arrays, halves, sems[gi], after, f"weight_gather_forward_{tag}")
        return _gather_wait(arrays, halves, between, f"weight_gather_wait_{tag}")

    xs, projs, hss, ycats, ys = [x0], [], [], [], []
    sq = None
    convw_full = poolw_full = pvecs = None
    for l in range(nl):
        if l == 0:
            (win[0],) = gathered(0, modbuf, "a")
        proj = _inproj_fwd(xs[l], vecs[l], win[l], l)
        if l == 0:
            got = gathered(1, proj, "b")
            wout[0] = got[2 + nl]
            convw_full, pvecs, poolw_full = _pack_gathered(got[0], got[1], got[2:2 + nl], pool_scale, ng)
        ycat, hs = _rnn_fwd(proj, convw_full[l], rvecs[l], wa_b[l], wx_b[l], l)
        if l + 1 < nl:
            win[l + 1], wout[l + 1] = gathered(2 + l, hs, f"c{l + 1}")
        ycat = _pool_fwd(proj, ycat, poolw_full[l], pvecs[l], l)
        y, xo, sq = _outproj_fwd(ycat, wout[l], xs[l], vecs[l], target if l == nl - 1 else None, l)
        projs.append(proj), hss.append(hs), ycats.append(ycat), ys.append(y), xs.append(xo)

    c_all_t = cbuf.reshape(8, 8, d)[:, 0, :].T

    def finish(l, flights, after, prev):
        (sems_a, arr_a), (sems_g, arr_g), (sems_b, arr_b), (sems_c, arr_c) = flights
        dwout_l, rwout = _exchange_wait(arr_a, sems_a, (1, 0), gq, after, f"grad_wait_a_l{l}")
        dpw_l, rpw, gates = _exchange_wait(arr_g, sems_g, (1, 1), gq, rwout, f"grad_wait_g_l{l}")
        dwin_l, rwin = _exchange_wait(arr_b, sems_b, (1, 0), gq, gates, f"grad_wait_b_l{l}")
        (slabs,) = _exchange_wait(arr_c, sems_c, (0, 0), gq, rwin, f"grad_wait_c_l{l}")
        p_win = _sum_owner(dwin_l, rwin, chip_arr, lambda tr: (None, tr, p_in),
                           lambda i, chip: (chip[0], i, 0), f"sum_w_in_l{l}")
        p_wout = _sum_owner(dwout_l, rwout, chip_arr, lambda tr: (None, tr, d),
                            lambda i, chip: (chip[0], i, 0), f"sum_w_out_l{l}")
        p_pw = _sum_owner(dpw_l, rpw, chip_arr, lambda tr: (ng, tr, gd),
                          lambda i, chip: (0, chip[0], 0), f"sum_pool_w_l{l}")
        p_gates = _sum_slots(gates.reshape(4, 2 * nh * hd, hd), f"sum_gates_l{l}", BF16)
        q_win, q_wout, q_pw, q_gates = _sibling_swap([p_win, p_wout, p_pw, p_gates], l)
        prev = prev or {}
        big = {
            "w_in": _adamw_layer(w_in, m_w_in, v_w_in, [p_win, q_win], l, prev.get("w_in"), f"adamw_w_in_l{l}"),
            "w_out": _adamw_layer(w_out, m_w_out, v_w_out, [p_wout, q_wout], l, prev.get("w_out"), f"adamw_w_out_l{l}"),
            "pool_w": _adamw_layer(pool_w, m_pool_w, v_pool_w, [p_pw, q_pw], l, prev.get("pool_w"), f"adamw_pool_w_l{l}"),
            "gate_a_w": _adamw_layer(gate_a_w, m_gate_a_w, v_gate_a_w, [p_gates, q_gates], l, prev.get("gate_a_w"),
                                     f"adamw_gate_a_w_l{l}"),
            "gate_x_w": _adamw_layer(gate_x_w, m_gate_x_w, v_gate_x_w, [p_gates, q_gates], l, prev.get("gate_x_w"),
                                     f"adamw_gate_x_w_l{l}", grad_row_offset=nh * hd),
            "ada_w": _adamw_ada_w_layer(c_all_t, slabs, chip_arr, ada_w, m_ada_w, v_ada_w, l, prev.get("ada_w"),
                                        f"adamw_ada_w_l{l}"),
        }
        return big, _sum_slots(slabs, f"sum_slab_l{l}")

    dx = xs[nl]
    flights = token = big = None
    totals = [None] * nl
    for l in reversed(range(nl)):
        vec_l = vecs[l]
        dycat, dwout_l, dvec_o = _outproj_bwd(dx, ys[l], ycats[l], wout[l], vec_l, l, vec_l if token is None else token)
        sems_a, arr_a, tok_a = _exchange_start([dwout_l, lax.empty(dwout_l.shape, BF16)], (1, 0), gq, f"grad_start_a_l{l}")
        dproj, dgates, dvec_r = _rnn_bwd(projs[l], hss[l], dycat, convw_full[l], rvecs[l], wa_b[l], wx_b[l], l, tok_a)
        dproj, dpw_l, dvec_p = _pool_bwd(projs[l], dycat, dproj, poolw_full[l], pvecs[l], l)
        gates4 = lax.dynamic_update_slice(lax.empty((4, *dgates.shape), BF16), dgates[None], (chip, 0, 0, 0, 0))
        sems_g, arr_g, tok_g = _exchange_start([dpw_l, lax.empty((4, ng, gq, gd), BF16), gates4], (1, 1), gq,
                                               f"grad_start_g_l{l}")
        dwin_l = _inproj_bwd_w(dproj, xs[l], vec_l, l, tok_g)
        sems_b, arr_b, tok_b = _exchange_start([dwin_l, lax.empty(dwin_l.shape, BF16)], (1, 0), gq, f"grad_start_b_l{l}")
        dx, dvec_i = _inproj_bwd_x(dproj, win[l], xs[l], dx, vec_l, l, tok_b)
        parts = [dvec_i, dvec_o, dvec_r, dvec_p]
        if l == nl - 1:
            parts.append(jnp.tile(sq, (1, d // sq.shape[1])))
        slab = jnp.concatenate(parts, axis=0)
        slabs = lax.dynamic_update_slice(lax.empty((8, *slab.shape), F32), slab[None], (me, 0, 0))
        sems_c, arr_c, token = _exchange_start([slabs], (0, 0), gq, f"grad_start_c_l{l}")
        if flights is not None:
            big, totals[l + 1] = finish(l + 1, flights, token, big)
        flights = ((sems_a, arr_a), (sems_g, arr_g), (sems_b, arr_b), (sems_c, arr_c))
    big, totals[0] = finish(0, flights, big["w_in"][3] if big else dx, big)
    grad_x = dx.reshape(x.shape)
    loss = totals[nl - 1][ROW_SQ, 0] * (0.5 / d)

    small = _adamw_small(totals, chip_arr, [
        (ada_b, m_ada_b, v_ada_b), (pre_norm_g, m_pre_norm_g, v_pre_norm_g), (post_norm_g, m_post_norm_g, v_post_norm_g),
        (conv_b, m_conv_b, v_conv_b), (lru_lambda, m_lru_lambda, v_lru_lambda), (pool_scale, m_pool_scale, v_pool_scale),
        (gate_a_b, m_gate_a_b, v_gate_a_b), (gate_x_b, m_gate_x_b, v_gate_x_b),
        (conv_w, m_conv_w, v_conv_w), (pool_b, m_pool_b, v_pool_b)])

    results = {
        "ada_w": tuple(o.reshape(ada_w.shape) for o in big["ada_w"]),
        "ada_b": small[0],
        "pre_norm_g": small[1],
        "w_in": tuple(o.reshape(w_in.shape) for o in big["w_in"]),
        "conv_w": small[8],
        "conv_b": small[3],
        "gate_a_w": tuple(o.reshape(gate_a_w.shape) for o in big["gate_a_w"]),
        "gate_a_b": small[6],
        "gate_x_w": tuple(o.reshape(gate_x_w.shape) for o in big["gate_x_w"]),
        "gate_x_b": small[7],
        "lru_lambda": small[4],
        "pool_w": tuple(o.reshape(pool_w.shape) for o in big["pool_w"]),
        "pool_b": small[9],
        "pool_scale": small[5],
        "w_out": tuple(o.reshape(w_out.shape) for o in big["w_out"]),
        "post_norm_g": small[2],
    }
    names = list(results)
    return (loss, grad_x,
            *[results[n][0] for n in names], *[results[n][1] for n in names],
            *[results[n][2] for n in names], *[results[n][3] for n in names])
```

```python
import jax
import jax.numpy as jnp
from jax import lax
from jax.experimental import pallas as pl
from jax.experimental.pallas import tpu as pltpu

F32 = jnp.float32
BF16 = jnp.bfloat16

NORM_EPS = 1e-6
LRU_C = 8.0
CONV_WIDTH = 4
HALO = 16
ADAM_LR = 0.001
ADAM_B1 = 0.9
ADAM_B2 = 0.999
ADAM_EPS = 1e-08
ADAM_WD = 0.01
ADAM_STEP = 10

V7X_VMEM_LIMIT_BYTES = 56 * 1024 * 1024
MATMUL_ROWS = 512
SCAN_ROWS = 512
BWD_SCAN_ROWS = 1024
ELEMENTWISE_ROWS = 512

MESH = pl.DeviceIdType.MESH
ANY = pl.BlockSpec(memory_space=pl.ANY)
VMEM = pl.BlockSpec(memory_space=pltpu.VMEM)
HBM = pl.BlockSpec(memory_space=pltpu.HBM)
SEM = pl.BlockSpec(memory_space=pltpu.SEMAPHORE)
DATAFLOW_EFFECT = pltpu.SideEffectType.DATAFLOW_SIDE_EFFECTING

NT_DIMS = (((1,), (1,)), ((), ()))
TN_DIMS = (((0,), (0,)), ((), ()))


def _params(n_grid_axes):
    return pltpu.CompilerParams(dimension_semantics=("arbitrary",) * n_grid_axes,
                                vmem_limit_bytes=V7X_VMEM_LIMIT_BYTES)


def _tile(total, want):
    t = min(want, max(total // 2, HALO))
    assert total % t == 0 and t % HALO == 0, (total, t)
    return t


def _row_tile(rows):
    for t in range(min(rows, ELEMENTWISE_ROWS) // 8 * 8, 0, -8):
        if rows % t == 0:
            return t
    return rows


def _sigmoid(z):
    return 1.0 / (1.0 + jnp.exp(-z))


def _softplus(z):
    return jnp.maximum(z, 0.0) + jnp.log(1.0 + jnp.exp(-jnp.abs(z)))


def _neg_expm1(z):
    return -jnp.tanh(0.5 * z) * (jnp.exp(z) + 1.0)


def _colsum(v):
    return jnp.sum(v, axis=0, keepdims=True)


def _prenorm(xt, vec_ref):
    rs = lax.rsqrt(jnp.mean(xt * xt, axis=-1, keepdims=True) + NORM_EPS)
    xn = xt * rs
    h = xn * vec_ref[3:4, :] * (1.0 + vec_ref[1:2, :]) + vec_ref[0:1, :]
    return h, xn, rs


def _shift_down(v, d, fill):
    t = v.shape[0]
    if d % 8 == 0:
        return jnp.concatenate([jnp.full((d, v.shape[1]), fill, v.dtype), v[:t - d]], axis=0)
    row = lax.broadcasted_iota(jnp.int32, v.shape, 0)
    return jnp.where(row >= d, pltpu.roll(v, d, 0), fill)


def _shift_up(v, d, fill):
    t = v.shape[0]
    if d % 8 == 0:
        return jnp.concatenate([v[d:], jnp.full((d, v.shape[1]), fill, v.dtype)], axis=0)
    row = lax.broadcasted_iota(jnp.int32, v.shape, 0)
    return jnp.where(row < t - d, pltpu.roll(v, t - d, 0), fill)


def _scan_fwd(a, v, h_before):
    d = 1
    while d < a.shape[0]:
        v = v + a * _shift_down(v, d, 0.0)
        a = a * _shift_down(a, d, 1.0)
        d *= 2
    return a * h_before + v


def _scan_rev(b, v):
    d = 1
    while d < b.shape[0]:
        v = v + b * _shift_up(v, d, 0.0)
        b = b * _shift_up(b, d, 0.0)
        d *= 2
    return v


def _inproj_fwd(x, vec, w_all, layer):
    s, d = x.shape
    p = w_all.shape[2]
    ts = _tile(s, MATMUL_ROWS)

    def body(x_ref, vec_ref, w_ref, proj_ref):
        h, _, _ = _prenorm(x_ref[...], vec_ref)
        hb = h.astype(BF16)
        for k in range(4):
            proj_ref[k] = jnp.dot(hb, w_ref[k], preferred_element_type=F32)

    return pl.pallas_call(
        body, name=f"inproj_fwd_l{layer}", grid=(s // ts,),
        in_specs=[pl.BlockSpec((ts, d), lambda i: (i, 0)),
                  pl.BlockSpec((8, d), lambda i: (0, 0)),
                  pl.BlockSpec((4, d, p), lambda i: (0, 0, 0))],
        out_specs=pl.BlockSpec((4, ts, p), lambda i: (0, i, 0)),
        out_shape=jax.ShapeDtypeStruct((4, s, p), F32),
        compiler_params=_params(1),
    )(x, vec, w_all)


HEADS_PER_STEP = 2
BWD_HEADS_PER_STEP = 1


def _rnn_gates(u, wa, wx, vec_ref, lanes):
    ub = u.astype(BF16)
    r = _sigmoid(jnp.dot(ub, wa, preferred_element_type=F32) + vec_ref[1:2, lanes])
    ig = _sigmoid(jnp.dot(ub, wx, preferred_element_type=F32) + vec_ref[2:3, lanes])
    sp = _softplus(-vec_ref[3:4, lanes])
    log_a = (-LRU_C) * r * sp
    return ub, r, ig, sp, log_a


def _conv(xbuf, cw_ref, vec_ref, lanes, ts):
    u = vec_ref[0:1, lanes] + cw_ref[CONV_WIDTH - 1:CONV_WIDTH, lanes] * xbuf[pl.ds(HALO, ts), lanes]
    for k in range(CONV_WIDTH - 1):
        u = u + cw_ref[k:k + 1, lanes] * xbuf[pl.ds(HALO - (CONV_WIDTH - 1) + k, ts), lanes]
    return u


def _rnn_fwd(proj, cw, vec, wa, wx, layer):
    _, s, d = proj.shape
    nh, hd, _ = wa.shape
    ts = _tile(s, SCAN_ROWS)
    hps = HEADS_PER_STEP
    wl = hps * hd

    def body(proj_ref, cw_ref, vec_ref, wa_ref, wx_ref, ycat_ref, hs_ref, xbuf, hlast):
        i = pl.program_id(1)

        @pl.when(i == 0)
        def _():
            xbuf[0:HALO, :] = jnp.zeros((HALO, wl), F32)
            hlast[...] = jnp.zeros_like(hlast)

        xbuf[pl.ds(HALO, ts), :] = proj_ref[0]
        for hh in range(hps):
            lanes = slice(hh * hd, (hh + 1) * hd)
            u = _conv(xbuf, cw_ref, vec_ref, lanes, ts)
            _, _, ig, _, log_a = _rnn_gates(u, wa_ref[hh], wx_ref[hh], vec_ref, lanes)
            a = jnp.exp(log_a)
            mult = jnp.sqrt(_neg_expm1(2.0 * log_a))
            hs = _scan_fwd(a, mult * (ig * u), hlast[0:1, lanes])
            hs_ref[:, lanes] = hs
            hlast[0:1, lanes] = hs_ref[ts - 1:ts, lanes]
            g = proj_ref[1, :, lanes]
            ycat_ref[:, lanes] = (hs * (g * _sigmoid(g))).astype(BF16)
        xbuf[0:HALO, :] = xbuf[pl.ds(ts, HALO), :]

    return pl.pallas_call(
        body, name=f"rnn_fwd_l{layer}", grid=(nh // hps, s // ts),
        in_specs=[pl.BlockSpec((2, ts, wl), lambda h, i: (0, i, h)),
                  pl.BlockSpec((CONV_WIDTH, wl), lambda h, i: (0, h)),
                  pl.BlockSpec((8, wl), lambda h, i: (0, h)),
                  pl.BlockSpec((hps, hd, hd), lambda h, i: (h, 0, 0)),
                  pl.BlockSpec((hps, hd, hd), lambda h, i: (h, 0, 0))],
        out_specs=[pl.BlockSpec((ts, wl), lambda h, i: (i, h)),
                   pl.BlockSpec((ts, wl), lambda h, i: (i, h))],
        out_shape=[jax.ShapeDtypeStruct((s, 2 * d), BF16), jax.ShapeDtypeStruct((s, d), F32)],
        scratch_shapes=[pltpu.VMEM((ts + HALO, wl), F32), pltpu.VMEM((8, wl), F32)],
        compiler_params=_params(2),
    )(proj, cw, vec, wa, wx)


def _inv_count(i, ts, lanes, win):
    t = i * ts + lax.broadcasted_iota(jnp.int32, (ts, lanes), 0)
    return 1.0 / jnp.minimum(t + 1, win).astype(F32)


def _window_sum(ext, win, forward):
    rows = ext.shape[0]
    s, d = ext, 1
    while d < win:
        s = s + pltpu.roll(s, d if forward else rows - d, 0)
        d *= 2
    return s


def _pooled(xbuf, xt, lanes, win, inv_cnt, ts):
    acc = _window_sum(xbuf[:, lanes], win, True)[HALO:, :]
    return acc * inv_cnt - xt


def _pool_fwd(proj, ycat, pw, vec, layer):
    _, s, d = proj.shape
    ng, gd, _ = pw.shape
    ts = _tile(s, MATMUL_ROWS)

    def body(proj_ref, ycat_in, pw_ref, vec_ref, ycat_ref, xbuf):
        del ycat_in
        i = pl.program_id(0)

        @pl.when(i == 0)
        def _():
            xbuf[0:HALO, :] = jnp.zeros((HALO, d), F32)

        xbuf[pl.ds(HALO, ts), :] = proj_ref[0]
        for g in range(ng):
            lanes = slice(g * gd, (g + 1) * gd)
            win = 2 << g
            xt = proj_ref[0, :, lanes]
            pooled = _pooled(xbuf, xt, lanes, win, _inv_count(i, ts, gd, win), ts).astype(BF16)
            z = jnp.dot(pooled, pw_ref[g], preferred_element_type=F32) + vec_ref[0:1, lanes]
            gg = proj_ref[1, :, lanes]
            ycat_ref[:, lanes] = (z * vec_ref[1:2, lanes] * (gg * _sigmoid(gg))).astype(BF16)
        xbuf[0:HALO, :] = xbuf[pl.ds(ts, HALO), :]

    return pl.pallas_call(
        body, name=f"pool_fwd_l{layer}", grid=(s // ts,),
        in_specs=[pl.BlockSpec((2, ts, d), lambda i: (1, i, 0)),
                  ANY,
                  pl.BlockSpec((ng, gd, gd), lambda i: (0, 0, 0)),
                  pl.BlockSpec((8, d), lambda i: (0, 0))],
        out_specs=pl.BlockSpec((ts, d), lambda i: (i, 1)),
        out_shape=jax.ShapeDtypeStruct((s, 2 * d), BF16),
        input_output_aliases={1: 0},
        scratch_shapes=[pltpu.VMEM((ts + HALO, d), F32)],
        compiler_params=_params(1),
    )(proj, ycat, pw, vec)


def _outproj_fwd(ycat, w_all, x, vec, target, layer):
    s, d = x.shape
    nk, kd = w_all.shape[0], w_all.shape[1]
    ts = _tile(s, MATMUL_ROWS)
    last = target is not None

    def body(*refs):
        if last:
            ycat_ref, w_ref, x_ref, vec_ref, tgt_ref, y_ref, xo_ref, sq_ref = refs
        else:
            ycat_ref, w_ref, x_ref, vec_ref, y_ref, xo_ref = refs
        y = jnp.dot(ycat_ref[:, 0:kd], w_ref[0], preferred_element_type=F32)
        for k in range(1, nk):
            y = y + jnp.dot(ycat_ref[:, k * kd:(k + 1) * kd], w_ref[k], preferred_element_type=F32)
        y_ref[...] = y
        rs = lax.rsqrt(jnp.mean(y * y, axis=-1, keepdims=True) + NORM_EPS)
        xo = x_ref[...] + vec_ref[2:3, :] * (y * rs * vec_ref[4:5, :])
        if last:
            err = xo - tgt_ref[...]
            xo_ref[...] = err * (1.0 / d)

            @pl.when(pl.program_id(0) == 0)
            def _():
                sq_ref[...] = jnp.zeros_like(sq_ref)

            sq_ref[...] += jnp.sum(err * err)
        else:
            xo_ref[...] = xo

    row = pl.BlockSpec((ts, d), lambda i: (i, 0))
    in_specs = [pl.BlockSpec((ts, nk * kd), lambda i: (i, 0)),
                pl.BlockSpec((nk, kd, d), lambda i: (0, 0, 0)),
                row, pl.BlockSpec((8, d), lambda i: (0, 0))]
    out_specs = [row, row]
    out_shape = [jax.ShapeDtypeStruct((s, d), F32), jax.ShapeDtypeStruct((s, d), F32)]
    args = [ycat, w_all, x, vec]
    if last:
        in_specs.append(row)
        args.append(target)
        out_specs.append(pl.BlockSpec((8, 128), lambda i: (0, 0)))
        out_shape.append(jax.ShapeDtypeStruct((8, 128), F32))
    out = pl.pallas_call(
        body, name=f"outproj_fwd_l{layer}", grid=(s // ts,),
        in_specs=in_specs, out_specs=out_specs, out_shape=out_shape,
        compiler_params=_params(1),
    )(*args)
    return (out[0], out[1], out[2]) if last else (out[0], out[1], None)


def _outproj_bwd(dxo, y, ycat, w_all, vec, layer, after):
    s, d = dxo.shape
    nk, kd = w_all.shape[0], w_all.shape[1]
    ts = _tile(s, MATMUL_ROWS)
    nt = s // ts

    def body(dxo_ref, y_ref, ycat_ref, w_ref, vec_ref, after_ref, dycat_ref, dw_ref, dvec_ref, acc):
        del after_ref
        i = pl.program_id(0)

        @pl.when(i == 0)
        def _():
            acc[...] = jnp.zeros_like(acc)
            dvec_ref[...] = jnp.zeros_like(dvec_ref)

        yt = y_ref[...]
        rs = lax.rsqrt(jnp.mean(yt * yt, axis=-1, keepdims=True) + NORM_EPS)
        yhat = yt * rs
        gate, gpost = vec_ref[2:3, :], vec_ref[4:5, :]
        dxo_t = dxo_ref[...]
        dyn = dxo_t * gate
        dvec_ref[0:1, :] += _colsum(dxo_t * (yhat * gpost))
        dvec_ref[1:2, :] += _colsum(dyn * yhat)
        t = dyn * gpost
        dy = (rs * (t - yhat * jnp.mean(t * yhat, axis=-1, keepdims=True))).astype(BF16)
        for k in range(nk):
            cols = slice(k * kd, (k + 1) * kd)
            dycat_ref[:, cols] = lax.dot_general(dy, w_ref[k], NT_DIMS, preferred_element_type=F32)
            acc[k] += lax.dot_general(ycat_ref[:, cols], dy, TN_DIMS, preferred_element_type=F32)

        @pl.when(i == nt - 1)
        def _():
            dw_ref[...] = acc[...].astype(BF16)

    row = pl.BlockSpec((ts, d), lambda i: (i, 0))
    wide = pl.BlockSpec((ts, nk * kd), lambda i: (i, 0))
    return pl.pallas_call(
        body, name=f"outproj_bwd_l{layer}", grid=(nt,),
        in_specs=[row, row, wide,
                  pl.BlockSpec((nk, kd, d), lambda i: (0, 0, 0)),
                  pl.BlockSpec((8, d), lambda i: (0, 0)), ANY],
        out_specs=[wide,
                   pl.BlockSpec((nk, kd, d), lambda i: (0, 0, 0)),
                   pl.BlockSpec((8, d), lambda i: (0, 0))],
        out_shape=[jax.ShapeDtypeStruct((s, nk * kd), F32),
                   jax.ShapeDtypeStruct((nk, kd, d), BF16),
                   jax.ShapeDtypeStruct((8, d), F32)],
        scratch_shapes=[pltpu.VMEM((nk, kd, d), F32)],
        compiler_params=_params(1),
    )(dxo, y, ycat, w_all, vec, after)


def _halo_index(ts, nt):
    return lambda j: jnp.maximum((nt - 1 - j) * (ts // HALO) - 1, 0)


def _rnn_bwd(proj, hs, dycat, cw, vec, wa, wx, layer, after):
    _, s, d = proj.shape
    nh, hd, _ = wa.shape
    ts = _tile(s, BWD_SCAN_ROWS)
    nt = s // ts
    halo = _halo_index(ts, nt)
    hps = BWD_HEADS_PER_STEP
    wl = hps * hd

    def body(proj_ref, xh_ref, hs_ref, hsh_ref, dy_ref, cw_ref, vec_ref, wa_ref, wx_ref, after_ref,
             dproj_ref, dgates_ref, dvec_ref, xbuf, hbuf, dubuf, carry, dw_acc):
        del after_ref
        j = pl.program_id(1)
        first_tile = j == nt - 1

        @pl.when(j == 0)
        def _():
            dubuf[pl.ds(ts, HALO), :] = jnp.zeros((HALO, wl), F32)
            carry[...] = jnp.zeros_like(carry)
            dw_acc[...] = jnp.zeros_like(dw_acc)
            dvec_ref[...] = jnp.zeros_like(dvec_ref)

        xbuf[0:HALO, :] = jnp.where(first_tile, 0.0, xh_ref[0])
        xbuf[pl.ds(HALO, ts), :] = proj_ref[0]
        hbuf[0:HALO, :] = jnp.where(first_tile, 0.0, hsh_ref[...])
        hbuf[pl.ds(HALO, ts), :] = hs_ref[...]

        for hh in range(hps):
            lanes = slice(hh * hd, (hh + 1) * hd)
            wa, wx = wa_ref[hh], wx_ref[hh]
            hs = hs_ref[:, lanes]
            u = _conv(xbuf, cw_ref, vec_ref, lanes, ts)
            ub, r, ig, sp, log_a = _rnn_gates(u, wa, wx, vec_ref, lanes)
            a = jnp.exp(log_a)
            e2 = jnp.exp(2.0 * log_a)
            one_minus_a2 = _neg_expm1(2.0 * log_a)
            inv_mult = lax.rsqrt(one_minus_a2)
            mult = one_minus_a2 * inv_mult

            g = proj_ref[1, :, lanes]
            sg = _sigmoid(g)
            dyc = dy_ref[:, lanes]
            dproj_ref[1, :, lanes] = (dyc * hs * (sg * (1.0 + g * (1.0 - sg)))).astype(BF16)

            row = lax.broadcasted_iota(jnp.int32, (ts, hd), 0)
            dhs = dyc * (g * sg) + jnp.where(row == ts - 1, carry[0:1, lanes], 0.0)
            dh = _scan_rev(_shift_up(a, 1, 0.0), dhs)
            carry[:, lanes] = (a * dh)[0:8, :]

            h_prev = hbuf[pl.ds(HALO - 1, ts), lanes]
            dlog_a = dh * h_prev * a - dh * (ig * u) * (e2 * inv_mult)
            di = dh * mult * u
            dzr = dlog_a * ((-LRU_C) * sp) * (r * (1.0 - r))
            dzi = di * (ig * (1.0 - ig))
            dvec_ref[3:4, lanes] += _colsum(dlog_a * r) * (LRU_C * _sigmoid(-vec_ref[3:4, lanes]))
            dvec_ref[1:2, lanes] += _colsum(dzr)
            dvec_ref[2:3, lanes] += _colsum(dzi)
            dzr_b, dzi_b = dzr.astype(BF16), dzi.astype(BF16)
            dw_acc[0, hh] += lax.dot_general(ub, dzr_b, TN_DIMS, preferred_element_type=F32)
            dw_acc[1, hh] += lax.dot_general(ub, dzi_b, TN_DIMS, preferred_element_type=F32)
            du = (dh * mult * ig
                  + lax.dot_general(dzr_b, wa, NT_DIMS, preferred_element_type=F32)
                  + lax.dot_general(dzi_b, wx, NT_DIMS, preferred_element_type=F32))
            dvec_ref[0:1, lanes] += _colsum(du)
            for k in range(CONV_WIDTH):
                dvec_ref[4 + k:5 + k, lanes] += _colsum(du * xbuf[pl.ds(HALO - (CONV_WIDTH - 1) + k, ts), lanes])

            dubuf[0:ts, lanes] = du
            dx = cw_ref[CONV_WIDTH - 1:CONV_WIDTH, lanes] * du
            for k in range(CONV_WIDTH - 1):
                dx = dx + cw_ref[k:k + 1, lanes] * dubuf[pl.ds(CONV_WIDTH - 1 - k, ts), lanes]
            dproj_ref[0, :, lanes] = dx.astype(BF16)
        dubuf[pl.ds(ts, HALO), :] = dubuf[0:HALO, :]

        @pl.when(first_tile)
        def _():
            dgates_ref[...] = dw_acc[...].astype(BF16)

    rev = lambda h, j: (nt - 1 - j, h)
    return pl.pallas_call(
        body, name=f"rnn_bwd_l{layer}", grid=(nh // hps, nt),
        in_specs=[pl.BlockSpec((2, ts, wl), lambda h, j: (0, nt - 1 - j, h)),
                  pl.BlockSpec((1, HALO, wl), lambda h, j: (0, halo(j), h)),
                  pl.BlockSpec((ts, wl), rev),
                  pl.BlockSpec((HALO, wl), lambda h, j: (halo(j), h)),
                  pl.BlockSpec((ts, wl), rev),
                  pl.BlockSpec((CONV_WIDTH, wl), lambda h, j: (0, h)),
                  pl.BlockSpec((8, wl), lambda h, j: (0, h)),
                  pl.BlockSpec((hps, hd, hd), lambda h, j: (h, 0, 0)),
                  pl.BlockSpec((hps, hd, hd), lambda h, j: (h, 0, 0)), ANY],
        out_specs=[pl.BlockSpec((2, ts, wl), lambda h, j: (0, nt - 1 - j, h)),
                   pl.BlockSpec((2, hps, hd, hd), lambda h, j: (0, h, 0, 0)),
                   pl.BlockSpec((16, wl), lambda h, j: (0, h))],
        out_shape=[jax.ShapeDtypeStruct((4, s, d), BF16),
                   jax.ShapeDtypeStruct((2, nh, hd, hd), BF16),
                   jax.ShapeDtypeStruct((16, d), F32)],
        scratch_shapes=[pltpu.VMEM((ts + HALO, wl), F32), pltpu.VMEM((ts + HALO, wl), F32),
                        pltpu.VMEM((ts + HALO, wl), F32), pltpu.VMEM((8, wl), F32),
                        pltpu.VMEM((2, hps, hd, hd), F32)],
        compiler_params=_params(2),
    )(proj, proj, hs, hs, dycat, cw, vec, wa, wx, after)


def _pool_bwd(proj, dycat, dproj, pw, vec, layer):
    _, s, d = proj.shape
    ng, gd, _ = pw.shape
    ts = _tile(s, MATMUL_ROWS)
    nt = s // ts
    halo = _halo_index(ts, nt)

    def body(proj_ref, xh_ref, dy_ref, dproj_in, pw_ref, vec_ref, dproj_ref, dpw_ref, dvec_ref, xbuf, qbuf, acc):
        del dproj_in
        j = pl.program_id(0)
        i = nt - 1 - j

        @pl.when(j == 0)
        def _():
            qbuf[pl.ds(ts, HALO), :] = jnp.zeros((HALO, d), F32)
            acc[...] = jnp.zeros_like(acc)
            dvec_ref[...] = jnp.zeros_like(dvec_ref)

        xbuf[0:HALO, :] = jnp.where(i == 0, 0.0, xh_ref[0])
        xbuf[pl.ds(HALO, ts), :] = proj_ref[0]
        for g in range(ng):
            lanes = slice(g * gd, (g + 1) * gd)
            win = 2 << g
            xt = proj_ref[0, :, lanes]
            inv_cnt = _inv_count(i, ts, gd, win)
            pooled = _pooled(xbuf, xt, lanes, win, inv_cnt, ts).astype(BF16)
            z = jnp.dot(pooled, pw_ref[g], preferred_element_type=F32) + vec_ref[0:1, lanes]
            scale = vec_ref[1:2, lanes]
            gg = proj_ref[1, :, lanes]
            sg = _sigmoid(gg)
            dyc = dy_ref[:, lanes]
            dyp = dyc * (gg * sg)
            dproj_ref[1, :, lanes] = (dyc * (z * scale) * (sg * (1.0 + gg * (1.0 - sg)))).astype(BF16)
            dvec_ref[1:2, lanes] += _colsum(dyp * z)
            dz = dyp * scale
            dvec_ref[0:1, lanes] += _colsum(dz)
            dz_b = dz.astype(BF16)
            acc[g] += lax.dot_general(pooled, dz_b, TN_DIMS, preferred_element_type=F32)
            dpooled = lax.dot_general(dz_b, pw_ref[g], NT_DIMS, preferred_element_type=F32)

            qbuf[0:ts, lanes] = dpooled * inv_cnt
            dx = _window_sum(qbuf[:, lanes], win, False)[0:ts, :] - dpooled
            dproj_ref[0, :, lanes] = dx.astype(BF16)
        qbuf[pl.ds(ts, HALO), :] = qbuf[0:HALO, :]

        @pl.when(j == nt - 1)
        def _():
            dpw_ref[...] = acc[...].astype(BF16)

    return pl.pallas_call(
        body, name=f"pool_bwd_l{layer}", grid=(nt,),
        in_specs=[pl.BlockSpec((2, ts, d), lambda j: (1, nt - 1 - j, 0)),
                  pl.BlockSpec((1, HALO, d), lambda j: (2, halo(j), 0)),
                  pl.BlockSpec((ts, d), lambda j: (nt - 1 - j, 1)),
                  ANY,
                  pl.BlockSpec((ng, gd, gd), lambda j: (0, 0, 0)),
                  pl.BlockSpec((8, d), lambda j: (0, 0))],
        out_specs=[pl.BlockSpec((2, ts, d), lambda j: (1, nt - 1 - j, 0)),
                   pl.BlockSpec((ng, gd, gd), lambda j: (0, 0, 0)),
                   pl.BlockSpec((8, d), lambda j: (0, 0))],
        out_shape=[jax.ShapeDtypeStruct((4, s, d), BF16),
                   jax.ShapeDtypeStruct((ng, gd, gd), BF16),
                   jax.ShapeDtypeStruct((8, d), F32)],
        input_output_aliases={3: 0},
        scratch_shapes=[pltpu.VMEM((ts + HALO, d), F32), pltpu.VMEM((ts + HALO, d), F32),
                        pltpu.VMEM((ng, gd, gd), F32)],
        compiler_params=_params(1),
    )(proj, proj, dycat, dproj, pw, vec)


def _inproj_bwd_x(dproj, w_all, x, dxo, vec, layer, after):
    s, d = x.shape
    p = w_all.shape[2]
    ts = _tile(s, MATMUL_ROWS)

    def body(dp_ref, w_ref, x_ref, dxo_ref, vec_ref, after_ref, dx_ref, dvec_ref):
        del after_ref

        @pl.when(pl.program_id(0) == 0)
        def _():
            dvec_ref[...] = jnp.zeros_like(dvec_ref)

        dh = lax.dot_general(dp_ref[0], w_ref[0], NT_DIMS, preferred_element_type=F32)
        for k in range(1, 4):
            dh = dh + lax.dot_general(dp_ref[k], w_ref[k], NT_DIMS, preferred_element_type=F32)
        _, xn, rs = _prenorm(x_ref[...], vec_ref)
        gpre, scale1 = vec_ref[3:4, :], 1.0 + vec_ref[1:2, :]
        dvec_ref[0:1, :] += _colsum(dh)
        dvec_ref[1:2, :] += _colsum(dh * (xn * gpre))
        dvec_ref[2:3, :] += _colsum(dh * (xn * scale1))
        t = dh * (gpre * scale1)
        dx_ref[...] = dxo_ref[...] + rs * (t - xn * jnp.mean(t * xn, axis=-1, keepdims=True))

    row = pl.BlockSpec((ts, d), lambda i: (i, 0))
    return pl.pallas_call(
        body, name=f"inproj_bwd_x_l{layer}", grid=(s // ts,),
        in_specs=[pl.BlockSpec((4, ts, p), lambda i: (0, i, 0)),
                  pl.BlockSpec((4, d, p), lambda i: (0, 0, 0)),
                  row, row, pl.BlockSpec((8, d), lambda i: (0, 0)), ANY],
        out_specs=[row, pl.BlockSpec((8, d), lambda i: (0, 0))],
        out_shape=[jax.ShapeDtypeStruct((s, d), F32), jax.ShapeDtypeStruct((8, d), F32)],
        compiler_params=_params(1),
    )(dproj, w_all, x, dxo, vec, after)


def _inproj_bwd_w(dproj, x, vec, layer, after):
    s, d = x.shape
    p = dproj.shape[2]
    ts = _tile(s, MATMUL_ROWS)
    nt = s // ts

    def body(dp_ref, x_ref, vec_ref, after_ref, dw_ref, acc):
        del after_ref
        i = pl.program_id(0)

        @pl.when(i == 0)
        def _():
            acc[...] = jnp.zeros_like(acc)

        h, _, _ = _prenorm(x_ref[...], vec_ref)
        hb = h.astype(BF16)
        for k in range(4):
            acc[k] += lax.dot_general(hb, dp_ref[k], TN_DIMS, preferred_element_type=F32)

        @pl.when(i == nt - 1)
        def _():
            dw_ref[...] = acc[...].astype(BF16)

    return pl.pallas_call(
        body, name=f"inproj_bwd_w_l{layer}", grid=(nt,),
        in_specs=[pl.BlockSpec((4, ts, p), lambda i: (0, i, 0)),
                  pl.BlockSpec((ts, d), lambda i: (i, 0)),
                  pl.BlockSpec((8, d), lambda i: (0, 0)), ANY],
        out_specs=pl.BlockSpec((4, d, p), lambda i: (0, 0, 0)),
        out_shape=jax.ShapeDtypeStruct((4, d, p), BF16),
        scratch_shapes=[pltpu.VMEM((4, d, p), F32)],
        compiler_params=_params(1),
    )(dproj, x, vec, after)


def _sum_slots(stacked, name, out_dtype=F32):
    n, rows, cols = stacked.shape
    tr = _row_tile(rows)

    def body(in_ref, out_ref):
        total = in_ref[0].astype(F32)
        for b in range(1, n):
            total = total + in_ref[b].astype(F32)
        out_ref[...] = total.astype(out_dtype)

    return pl.pallas_call(
        body, name=name, grid=(rows // tr,),
        in_specs=[pl.BlockSpec((n, tr, cols), lambda i: (0, i, 0))],
        out_specs=pl.BlockSpec((tr, cols), lambda i: (i, 0)),
        out_shape=jax.ShapeDtypeStruct((rows, cols), out_dtype),
        compiler_params=_params(1),
    )(stacked)


def _adam_update(w, m, v, g):
    m_new = ADAM_B1 * m + (1.0 - ADAM_B1) * g
    v_new = ADAM_B2 * v + (1.0 - ADAM_B2) * (g * g)
    m_hat = m_new / (1.0 - ADAM_B1 ** ADAM_STEP)
    v_hat = v_new / (1.0 - ADAM_B2 ** ADAM_STEP)
    return (-ADAM_LR) * (m_hat / (jnp.sqrt(v_hat) + ADAM_EPS) + ADAM_WD * w), m_new, v_new


def _adamw_layer(w, m, v, grads, layer, prev, name, grad_row_offset=0):
    nl = w.shape[0]
    cols = w.shape[-1]
    rows = w.size // (nl * cols)
    tr = _row_tile(rows)
    off = layer * (rows // tr)
    g_off = grad_row_offset // tr
    n = len(grads)
    n_prev = 0 if prev is None else 4

    def body(*refs):
        w_ref, m_ref, v_ref = refs[:3]
        g_refs = refs[3:3 + n]
        g_out, d_out, m_out, v_out = refs[3 + n + n_prev:]
        g = g_refs[0][...].astype(F32)
        for r in g_refs[1:]:
            g = g + r[...].astype(F32)
        g_out[...] = g
        d_out[...], m_out[...], v_out[...] = _adam_update(w_ref[...], m_ref[...], v_ref[...], g)

    mine = pl.BlockSpec((tr, cols), lambda i: (off + i, 0))
    args = [a.reshape(nl * rows, cols) for a in (w, m, v)] + [g.reshape(-1, cols) for g in grads]
    outs = pl.pallas_call(
        body, name=name, grid=(rows // tr,),
        in_specs=[mine] * 3 + [pl.BlockSpec((tr, cols), lambda i: (g_off + i, 0))] * n + [ANY] * n_prev,
        out_specs=[mine] * 4,
        out_shape=[jax.ShapeDtypeStruct((nl * rows, cols), F32)] * 4,
        input_output_aliases={3 + n + k: k for k in range(n_prev)},
        compiler_params=_params(1),
    )(*args, *(prev or ()))
    return tuple(outs)


def _into_slot(a, dtype, chip_arr, name, layer=None, after=None):
    rows, cols = a.shape[-2:]
    tr = _row_tile(rows)

    def body(chip_ref, a_ref, *rest):
        del chip_ref
        rest[-1][...] = a_ref[...].astype(dtype)

    if layer is None:
        in_spec = pl.BlockSpec((tr, cols), lambda i, chip: (i, 0))
    else:
        in_spec = pl.BlockSpec((None, tr, cols), lambda i, chip: (layer, i, 0))
    extra = [] if after is None else [after]
    return pl.pallas_call(
        body, name=name,
        grid_spec=pltpu.PrefetchScalarGridSpec(
            num_scalar_prefetch=1, grid=(rows // tr,),
            in_specs=[in_spec] + [ANY] * len(extra),
            out_specs=pl.BlockSpec((None, tr, cols), lambda i, chip: (chip[0], i, 0))),
        out_shape=jax.ShapeDtypeStruct((4, rows, cols), dtype),
        compiler_params=_params(1),
    )(chip_arr, a, *extra)


def _sum_owner(own, land, chip_arr, own_block, own_index, name):
    blk = land.shape[1:]
    tr = _row_tile(blk[-2])
    steps = blk[-2] // tr
    tile = (*blk[:-2], tr, blk[-1])
    lead = (0,) * (len(blk) - 2)

    def body(chip_ref, own_ref, l1, l2, l3, out_ref):
        del chip_ref
        total = (own_ref[...].astype(F32) + l1[...].astype(F32)) + (l2[...].astype(F32) + l3[...].astype(F32))
        out_ref[...] = total.astype(BF16)

    def landed(k):
        return pl.BlockSpec((None, *tile), lambda i, chip: (chip[0] ^ k, *lead, i, 0))

    return pl.pallas_call(
        body, name=name,
        grid_spec=pltpu.PrefetchScalarGridSpec(
            num_scalar_prefetch=1, grid=(steps,),
            in_specs=[pl.BlockSpec(own_block(tr), own_index), landed(1), landed(2), landed(3)],
            out_specs=pl.BlockSpec(tile, lambda i, chip: (*lead, i, 0))),
        out_shape=jax.ShapeDtypeStruct(blk, BF16),
        compiler_params=_params(1),
    )(chip_arr, own, land, land, land)


_WHOLE_VMEM = pltpu.CompilerParams(vmem_limit_bytes=V7X_VMEM_LIMIT_BYTES)


def _pack_vectors(modbuf, ada_b, pre_norm_g, post_norm_g, conv_b, gate_a_b, gate_x_b, lru_lambda):
    nl, d = pre_norm_g.shape
    n = modbuf.shape[2] // nl
    nh, hd = gate_a_b.shape[1], gate_a_b.shape[2]

    def body(mb_ref, ab_ref, pre_ref, post_ref, cb_ref, gab_ref, gxb_ref, lam_ref, *outs):
        for layer in range(nl):
            vec_ref, rvec_ref = outs[layer], outs[nl + layer]
            vec_ref[...] = jnp.zeros_like(vec_ref)
            rvec_ref[...] = jnp.zeros_like(rvec_ref)
            for k in range(4):
                piece = mb_ref[k, 0:1, layer * n:(layer + 1) * n] + ab_ref[layer:layer + 1, k * n:(k + 1) * n]
                lo = k * n
                while lo < (k + 1) * n:
                    row = lo // d
                    hi = min((row + 1) * d, (k + 1) * n)
                    vec_ref[row:row + 1, lo - row * d:hi - row * d] = piece[:, lo - k * n:hi - k * n]
                    lo = hi
            vec_ref[3:4, :] = pre_ref[layer:layer + 1, :]
            vec_ref[4:5, :] = post_ref[layer:layer + 1, :]
            rvec_ref[0:1, :] = cb_ref[layer:layer + 1, :]
            for h in range(nh):
                rvec_ref[1:2, h * hd:(h + 1) * hd] = gab_ref[layer, h:h + 1, :]
                rvec_ref[2:3, h * hd:(h + 1) * hd] = gxb_ref[layer, h:h + 1, :]
            rvec_ref[3:4, :] = lam_ref[layer:layer + 1, :]

    out = pl.pallas_call(
        body, name="pack_vectors", in_specs=[VMEM] * 8, out_specs=[VMEM] * (2 * nl),
        out_shape=[jax.ShapeDtypeStruct((8, d), F32)] * (2 * nl), compiler_params=_WHOLE_VMEM,
    )(modbuf, ada_b, pre_norm_g, post_norm_g, conv_b, gate_a_b, gate_x_b, lru_lambda)
    return list(out[:nl]), list(out[nl:])


def _pack_gathered(convw_g, poolb_g, pws, pool_scale, ng):
    nl, d = pool_scale.shape
    taps = convw_g.shape[1] // nl
    dq = convw_g.shape[2]
    gq, gd = poolb_g.shape[2], pws[0].shape[2]

    def body(cg_ref, pb_ref, *rest):
        pw_refs, ps_ref = rest[:nl], rest[nl]
        outs = rest[nl + 1:]
        for layer in range(nl):
            cw_ref, pvec_ref, pwf_ref = outs[layer], outs[nl + layer], outs[2 * nl + layer]
            pvec_ref[...] = jnp.zeros_like(pvec_ref)
            pvec_ref[1:2, :] = ps_ref[layer:layer + 1, :]
            for k in range(4):
                cw_ref[:, k * dq:(k + 1) * dq] = cg_ref[k, layer * taps:(layer + 1) * taps, :]
                for g in range(ng):
                    lo = g * gd + k * gq
                    pvec_ref[0:1, lo:lo + gq] = pb_ref[k, layer * ng + g:layer * ng + g + 1, :]
                    pwf_ref[g, k * gq:(k + 1) * gq, :] = pw_refs[layer][k, g * gq:(g + 1) * gq, :]

    out = pl.pallas_call(
        body, name="pack_gathered", in_specs=[VMEM] * (3 + nl), out_specs=[VMEM] * (3 * nl),
        out_shape=[jax.ShapeDtypeStruct((taps, d), F32)] * nl + [jax.ShapeDtypeStruct((8, d), F32)] * nl
        + [jax.ShapeDtypeStruct((ng, gd, gd), BF16)] * nl,
        compiler_params=_WHOLE_VMEM,
    )(convw_g, poolb_g, *pws, pool_scale)
    return list(out[:nl]), list(out[nl:2 * nl]), list(out[2 * nl:])


ROW_SHIFT, ROW_SCALE, ROW_PRE, ROW_GATE, ROW_POST = 0, 1, 2, 8, 9
ROW_CONV_B, ROW_GATE_A_B, ROW_GATE_X_B, ROW_LAMBDA, ROW_CONV_W = 16, 17, 18, 19, 20
ROW_POOL_B, ROW_POOL_SCALE, ROW_SQ = 32, 33, 40


def _adamw_small(totals, chip_arr, params):
    nl = len(totals)
    d = totals[0].shape[1]
    n_par = len(params)
    flat = [a for p in params for a in p]
    nh, hd = params[6][0].shape[1], params[6][0].shape[2]
    taps, dq = params[8][0].shape[1], params[8][0].shape[2]
    ng, gq = params[9][0].shape[1], params[9][0].shape[2]
    gd = d // ng

    def body(chip_ref, *refs):
        tot = refs[:nl]
        ins = refs[nl:nl + 3 * n_par]
        outs = refs[nl + 3 * n_par:]
        chip = chip_ref[0]

        def update(p, idx, g):
            delta, m_new, v_new = _adam_update(ins[3 * p][idx], ins[3 * p + 1][idx], ins[3 * p + 2][idx], g)
            outs[4 * p][idx] = g
            outs[4 * p + 1][idx] = delta
            outs[4 * p + 2][idx] = m_new
            outs[4 * p + 3][idx] = v_new

        def mine(candidates):
            g = candidates[0]
            for k in range(1, 4):
                g = jnp.where(chip == k, candidates[k], g)
            return g

        for layer in range(nl):
            t = tot[layer]
            row = (slice(layer, layer + 1), slice(None))
            for j, r in enumerate((ROW_SHIFT, ROW_SCALE, ROW_GATE)):
                update(0, (slice(layer, layer + 1), slice(j * d, (j + 1) * d)), t[r:r + 1, :])
            for p, r in ((1, ROW_PRE), (2, ROW_POST), (3, ROW_CONV_B), (4, ROW_LAMBDA), (5, ROW_POOL_SCALE)):
                update(p, row, t[r:r + 1, :])
            for h in range(nh):
                idx = (layer, slice(h, h + 1), slice(None))
                update(6, idx, t[ROW_GATE_A_B:ROW_GATE_A_B + 1, h * hd:(h + 1) * hd])
                update(7, idx, t[ROW_GATE_X_B:ROW_GATE_X_B + 1, h * hd:(h + 1) * hd])
            for k in range(taps):
                r = ROW_CONV_W + k
                update(8, (layer, slice(k, k + 1), slice(None)), mine([t[r:r + 1, c * dq:(c + 1) * dq] for c in range(4)]))
            for g in range(ng):
                cands = [t[ROW_POOL_B:ROW_POOL_B + 1, g * gd + c * gq:g * gd + (c + 1) * gq] for c in range(4)]
                update(9, (layer, slice(g, g + 1), slice(None)), mine(cands))

    out = pl.pallas_call(
        body, name="adamw_small",
        in_specs=[pl.BlockSpec(memory_space=pltpu.SMEM)] + [VMEM] * (nl + 3 * n_par),
        out_specs=[VMEM] * (4 * n_par),
        out_shape=[jax.ShapeDtypeStruct(p[0].shape, F32) for p in params for _ in range(4)],
        compiler_params=_WHOLE_VMEM,
    )(chip_arr, *totals, *flat)
    return [tuple(out[4 * p:4 * p + 4]) for p in range(n_par)]


def _adamw_ada_w_layer(c_t, slabs, chip_arr, w, m, v, layer, prev, name, after):
    nl, d, n = w.shape
    nb = c_t.shape[1]
    tr = _row_tile(d)
    off = layer * (d // tr)
    n_prev = 0 if prev is None else 4
    mod_rows = (ROW_SHIFT, ROW_SCALE, ROW_GATE)

    def body(chip_ref, c_ref, slab_ref, w_ref, m_ref, v_ref, *rest):
        g_out, d_out, m_out, v_out = rest[n_prev + 1:n_prev + 5]
        dm = rest[-1]

        @pl.when(pl.program_id(0) == 0)
        def _():
            for k in range(4):
                @pl.when(chip_ref[0] == k)
                def _():
                    lo = k * n
                    while lo < (k + 1) * n:
                        hi = min((lo // d + 1) * d, (k + 1) * n)
                        row = mod_rows[lo // d]
                        for b in range(nb):
                            dm[b:b + 1, lo - k * n:hi - k * n] = slab_ref[b, row:row + 1, lo % d:lo % d + hi - lo]
                        lo = hi

        g = c_ref[:, 0:1] * dm[0:1, :]
        for b in range(1, nb):
            g = g + c_ref[:, b:b + 1] * dm[b:b + 1, :]
        g_out[...] = g
        d_out[...], m_out[...], v_out[...] = _adam_update(w_ref[...], m_ref[...], v_ref[...], g)

    mine = pl.BlockSpec((tr, n), lambda i, chip: (off + i, 0))
    outs = pl.pallas_call(
        body, name=name,
        grid_spec=pltpu.PrefetchScalarGridSpec(
            num_scalar_prefetch=1, grid=(d // tr,),
            in_specs=[pl.BlockSpec((tr, nb), lambda i, chip: (i, 0)),
                      pl.BlockSpec(slabs.shape, lambda i, chip: (0, 0, 0))] + [mine] * 3 + [ANY] * (n_prev + 1),
            out_specs=[mine] * 4,
            scratch_shapes=[pltpu.VMEM((nb, n), F32)]),
        out_shape=[jax.ShapeDtypeStruct((nl * d, n), F32)] * 4,
        input_output_aliases={6 + k: k for k in range(n_prev)},
        compiler_params=_params(1),
    )(chip_arr, c_t, slabs, *[a.reshape(nl * d, n) for a in (w, m, v)], *(prev or ()), after)
    return tuple(outs)


def _place():
    x, y, c = lax.axis_index("x"), lax.axis_index("y"), lax.axis_index("c")
    return x, y, c


OTHER_CHIPS = ((1, 0), (0, 1), (1, 1))
OTHER_DEVICES = tuple((fx, fy, fc) for fx in (0, 1) for fy in (0, 1) for fc in (0, 1))[1:]


def _mod_exchange(c_row, ada_w, after):
    nl, d, n = ada_w.shape

    def body(c_ref, w_ref, after_ref, cbuf, modbuf, token, cblk, mres, send_a, recv_a, send_c, recv_c):
        del after_ref
        token[...] = jnp.zeros_like(token)
        x, y, c = _place()
        me = 4 * x + 2 * y + c
        chip = 2 * x + y
        cv = c_ref[...]
        cblk[...] = jnp.zeros_like(cblk)
        cblk[0:1, :] = cv * _sigmoid(cv)

        def rows_of(dev):
            return cbuf.at[pl.ds(pl.multiple_of(8 * dev, 8), 8), :]

        cbuf[pl.ds(pl.multiple_of(8 * me, 8), 8), :] = cblk[...]
        sends = []
        for j, (fx, fy, fc) in enumerate(OTHER_DEVICES):
            cp = pltpu.make_async_remote_copy(
                src_ref=cblk, dst_ref=rows_of(me), send_sem=send_a.at[j], recv_sem=recv_a.at[j],
                device_id=(x ^ fx, y ^ fy, c ^ fc), device_id_type=MESH)
            cp.start()
            sends.append(cp)
        for j, (fx, fy, fc) in enumerate(OTHER_DEVICES):
            peer = 4 * (x ^ fx) + 2 * (y ^ fy) + (c ^ fc)
            pltpu.make_async_remote_copy(
                src_ref=cblk, dst_ref=rows_of(peer), send_sem=send_a.at[j], recv_sem=recv_a.at[j],
                device_id=(x ^ fx, y ^ fy, c ^ fc), device_id_type=MESH).wait_recv()
        for cp in sends:
            cp.wait_send()

        call = cbuf[...]
        for layer in range(nl):
            mres[:, layer * n:(layer + 1) * n] = jnp.dot(
                call, w_ref[layer], preferred_element_type=F32, precision=lax.Precision.HIGHEST)

        def block_of(dev):
            return mres.at[pl.ds(pl.multiple_of(8 * dev, 8), 8), :]

        modbuf[chip] = mres[pl.ds(pl.multiple_of(8 * me, 8), 8), :]
        sends = []
        for j, (fx, fy) in enumerate(OTHER_CHIPS):
            peer = 4 * (x ^ fx) + 2 * (y ^ fy) + c
            cp = pltpu.make_async_remote_copy(
                src_ref=block_of(peer), dst_ref=modbuf.at[chip], send_sem=send_c.at[j], recv_sem=recv_c.at[j],
                device_id=(x ^ fx, y ^ fy, c), device_id_type=MESH)
            cp.start()
            sends.append(cp)
        for j, (fx, fy) in enumerate(OTHER_CHIPS):
            pltpu.make_async_remote_copy(
                src_ref=block_of(me), dst_ref=modbuf.at[2 * (x ^ fx) + (y ^ fy)],
                send_sem=send_c.at[j], recv_sem=recv_c.at[j],
                device_id=(x ^ fx, y ^ fy, c), device_id_type=MESH).wait_recv()
        for cp in sends:
            cp.wait_send()

    return pl.pallas_call(
        body, name="mod_exchange", in_specs=[VMEM, VMEM, ANY], out_specs=[VMEM, VMEM, VMEM],
        out_shape=[jax.ShapeDtypeStruct((64, d), F32), jax.ShapeDtypeStruct((4, 8, nl * n), F32),
                   jax.ShapeDtypeStruct((8, 128), F32)],
        scratch_shapes=[pltpu.VMEM((8, d), F32), pltpu.VMEM((64, nl * n), F32),
                        pltpu.SemaphoreType.DMA((7,)), pltpu.SemaphoreType.DMA((7,)),
                        pltpu.SemaphoreType.DMA((3,)), pltpu.SemaphoreType.DMA((3,))],
        compiler_params=pltpu.CompilerParams(vmem_limit_bytes=V7X_VMEM_LIMIT_BYTES, has_side_effects=True),
    )(c_row, ada_w, after)


def _in_hbm(a):
    return pltpu.with_memory_space_constraint(a, pltpu.HBM)


def _gather_copies(lands, split, over_ici):
    x, y, c = _place()
    chip = 2 * x + y
    out = []
    for t, land in enumerate(lands):
        half = land.shape[1] // 2
        mine = pl.ds(pl.multiple_of(c * half, half), half)
        theirs = pl.ds(pl.multiple_of((1 - c) * half, half), half)
        for j, (fx, fy) in enumerate(OTHER_CHIPS):
            them = 2 * (x ^ fx) + (y ^ fy)
            if over_ici and split[t]:
                out.append((land.at[chip, mine], land.at[chip, mine], land.at[them, mine], (x ^ fx, y ^ fy, c), 3 * t + j))
            elif over_ici:
                out.append((land.at[chip], land.at[chip], land.at[them], (x ^ fx, y ^ fy, c), 3 * t + j))
            elif split[t]:
                out.append((land.at[them, mine], land.at[them, mine], land.at[them, theirs], (x, y, 1 - c), 3 * t + j))
    return out


def _gather_start(lands, groups, split, name):
    n, ngr = len(lands), len(groups)

    def body(*refs):
        sems = refs[n:n + 2 * ngr]
        for gi, idxs in enumerate(groups):
            for src, dst, _, peer, k in _gather_copies([refs[i] for i in idxs], [split[i] for i in idxs], True):
                pltpu.make_async_remote_copy(src_ref=src, dst_ref=dst, send_sem=sems[2 * gi].at[k],
                                             recv_sem=sems[2 * gi + 1].at[k], device_id=peer, device_id_type=MESH).start()
        refs[-1][...] = jnp.zeros_like(refs[-1])

    sem_shapes = []
    for idxs in groups:
        sem_shapes += [pltpu.SemaphoreType.DMA((3 * len(idxs),))] * 2
    out = pl.pallas_call(
        body, name=name,
        in_specs=[HBM] * n, out_specs=[SEM] * (2 * ngr) + [HBM] * n + [VMEM],
        out_shape=sem_shapes + [pltpu.HBM(a.shape, a.dtype) for a in lands] + [jax.ShapeDtypeStruct((8, 128), F32)],
        input_output_aliases={i: 2 * ngr + i for i in range(n)},
        compiler_params=pltpu.CompilerParams(has_side_effects=DATAFLOW_EFFECT),
    )(*[_in_hbm(a) for a in lands])
    sems = [(out[2 * gi], out[2 * gi + 1]) for gi in range(ngr)]
    return sems, list(out[2 * ngr:2 * ngr + n]), out[-1]


def _gather_forward(lands, split, sems, after, name):
    n = len(lands)

    def body(*refs):
        ici_send, ici_recv = refs[n], refs[n + 1]
        fwd_send, fwd_recv = refs[n + 3], refs[n + 4]
        forwards = {k: (src, dst, peer) for src, dst, _, peer, k in _gather_copies(refs[:n], split, False)}
        for src, _, landed, peer, k in _gather_copies(refs[:n], split, True):
            cp = pltpu.make_async_remote_copy(src_ref=src, dst_ref=landed, send_sem=ici_send.at[k], recv_sem=ici_recv.at[k],
                                              device_id=peer, device_id_type=MESH)
            cp.wait_recv()
            if k in forwards:
                fsrc, fdst, fpeer = forwards[k]
                pltpu.make_async_remote_copy(src_ref=fsrc, dst_ref=fdst, send_sem=fwd_send.at[k], recv_sem=fwd_recv.at[k],
                                             device_id=fpeer, device_id_type=MESH).start()
            cp.wait_send()

    out = pl.pallas_call(
        body, name=name,
        in_specs=[HBM] * n + [SEM, SEM, ANY], out_specs=[SEM, SEM] + [HBM] * n,
        out_shape=[pltpu.SemaphoreType.DMA((3 * n,))] * 2 + [pltpu.HBM(a.shape, a.dtype) for a in lands],
        input_output_aliases={i: 2 + i for i in range(n)},
        compiler_params=pltpu.CompilerParams(has_side_effects=DATAFLOW_EFFECT),
    )(*lands, sems[0], sems[1], after)
    return (out[0], out[1]), list(out[2:])


def _gather_wait(lands, split, sems, name):
    n = len(lands)

    def body(*refs):
        send_sems, recv_sems = refs[n], refs[n + 1]
        for src, _, landed, peer, k in _gather_copies(refs[:n], split, False):
            cp = pltpu.make_async_remote_copy(src_ref=src, dst_ref=landed, send_sem=send_sems.at[k], recv_sem=recv_sems.at[k],
                                              device_id=peer, device_id_type=MESH)
            cp.wait_send()
            cp.wait_recv()

    out = pl.pallas_call(
        body, name=name,
        in_specs=[HBM] * n + [SEM, SEM], out_specs=[HBM] * n,
        out_shape=[pltpu.HBM(a.shape, a.dtype) for a in lands],
        input_output_aliases={i: i for i in range(n)},
        compiler_params=pltpu.CompilerParams(has_side_effects=DATAFLOW_EFFECT),
    )(*lands, sems[0], sems[1])
    return list(out)


def _to_owner_copies(pairs, q):
    x, y, c = _place()
    chip = 2 * x + y
    out = []
    for t, (part, land) in enumerate(pairs):
        for j, (fx, fy) in enumerate(OTHER_CHIPS):
            owner = 2 * (x ^ fx) + (y ^ fy)
            if part.shape[0] == 4 and part.shape[1:] == land.shape[1:]:
                src = part.at[owner]
            else:
                src = part.at[:, pl.ds(pl.multiple_of(owner * q, q), q), :]
            out.append((src, land.at[chip], land.at[owner], (x ^ fx, y ^ fy, c), 3 * t + j))
    return out


def _to_all_copies(bufs, first_sem):
    x, y, c = _place()
    me = 4 * x + 2 * y + c
    out = []
    for t, buf in enumerate(bufs):
        for j, (fx, fy, fc) in enumerate(OTHER_DEVICES):
            them = 4 * (x ^ fx) + 2 * (y ^ fy) + (c ^ fc)
            out.append((buf.at[me], buf.at[me], buf.at[them], (x ^ fx, y ^ fy, c ^ fc), first_sem + 7 * t + j))
    return out


def _to_chips_copies(bufs, first_sem):
    x, y, c = _place()
    chip = 2 * x + y
    out = []
    for t, buf in enumerate(bufs):
        for j, (fx, fy) in enumerate(OTHER_CHIPS):
            them = 2 * (x ^ fx) + (y ^ fy)
            out.append((buf.at[chip], buf.at[chip], buf.at[them], (x ^ fx, y ^ fy, c), first_sem + 3 * t + j))
    return out


def _exchange_copies(refs, kinds, q):
    n_owner, n_chips = kinds
    pairs = list(zip(refs[:n_owner], refs[n_owner:2 * n_owner]))
    first_all = 3 * (n_owner + n_chips)
    return (_to_owner_copies(pairs, q) + _to_chips_copies(refs[2 * n_owner:2 * n_owner + n_chips], 3 * n_owner)
            + _to_all_copies(refs[2 * n_owner + n_chips:], first_all))


def _exchange_start(arrays, kinds, q, name):
    n = len(arrays)
    n_sems = 3 * (kinds[0] + kinds[1]) + 7 * (n - 2 * kinds[0] - kinds[1])

    def body(*refs):
        send_sems, recv_sems = refs[n], refs[n + 1]
        for src, dst, _, peer, k in _exchange_copies(refs[:n], kinds, q):
            pltpu.make_async_remote_copy(src_ref=src, dst_ref=dst, send_sem=send_sems.at[k], recv_sem=recv_sems.at[k],
                                         device_id=peer, device_id_type=MESH).start()
        refs[-1][...] = jnp.zeros_like(refs[-1])

    out = pl.pallas_call(
        body, name=name,
        in_specs=[HBM] * n, out_specs=[SEM, SEM] + [HBM] * n + [VMEM],
        out_shape=[pltpu.SemaphoreType.DMA((n_sems,))] * 2 + [pltpu.HBM(a.shape, a.dtype) for a in arrays]
        + [jax.ShapeDtypeStruct((8, 128), F32)],
        input_output_aliases={i: 2 + i for i in range(n)},
        compiler_params=pltpu.CompilerParams(has_side_effects=DATAFLOW_EFFECT),
    )(*[_in_hbm(a) for a in arrays])
    return (out[0], out[1]), list(out[2:2 + n]), out[-1]


def _exchange_wait(arrays, sems, kinds, q, after, name):
    n = len(arrays)

    def body(*refs):
        send_sems, recv_sems = refs[n], refs[n + 1]
        for src, _, landed, peer, k in _exchange_copies(refs[:n], kinds, q):
            cp = pltpu.make_async_remote_copy(src_ref=src, dst_ref=landed, send_sem=send_sems.at[k], recv_sem=recv_sems.at[k],
                                              device_id=peer, device_id_type=MESH)
            cp.wait_send()
            cp.wait_recv()

    out = pl.pallas_call(
        body, name=name,
        in_specs=[HBM] * n + [SEM, SEM, ANY], out_specs=[HBM] * n,
        out_shape=[pltpu.HBM(a.shape, a.dtype) for a in arrays],
        input_output_aliases={i: i for i in range(n)},
        compiler_params=pltpu.CompilerParams(has_side_effects=DATAFLOW_EFFECT),
    )(*arrays, sems[0], sems[1], after)
    return list(out)


def _sibling_copies(refs):
    n = len(refs) // 2
    x, y, c = _place()
    return [(refs[i], refs[n + i], (x, y, 1 - c), i) for i in range(n)]


def _sibling_start(parts, layer):
    arrays = list(parts) + [lax.empty(a.shape, a.dtype) for a in parts]
    n = len(arrays)

    def body(*refs):
        send_sems, recv_sems = refs[n], refs[n + 1]
        for src, dst, peer, k in _sibling_copies(refs[:n]):
            pltpu.make_async_remote_copy(src_ref=src, dst_ref=dst, send_sem=send_sems.at[k], recv_sem=recv_sems.at[k],
                                         device_id=peer, device_id_type=MESH).start()
        refs[-1][...] = jnp.zeros_like(refs[-1])

    out = pl.pallas_call(
        body, name=f"sibling_swap_start_l{layer}",
        in_specs=[HBM] * n, out_specs=[SEM, SEM] + [HBM] * n + [VMEM],
        out_shape=[pltpu.SemaphoreType.DMA((n // 2,))] * 2 + [pltpu.HBM(a.shape, a.dtype) for a in arrays]
        + [jax.ShapeDtypeStruct((8, 128), F32)],
        input_output_aliases={i: 2 + i for i in range(n)},
        compiler_params=pltpu.CompilerParams(has_side_effects=DATAFLOW_EFFECT),
    )(*[_in_hbm(a) for a in arrays])
    return (out[0], out[1]), list(out[2:2 + n]), out[-1]


def _sibling_wait(arrays, sems, after, layer):
    n = len(arrays)

    def body(*refs):
        send_sems, recv_sems = refs[n], refs[n + 1]
        for src, dst, peer, k in _sibling_copies(refs[:n]):
            cp = pltpu.make_async_remote_copy(src_ref=src, dst_ref=dst, send_sem=send_sems.at[k], recv_sem=recv_sems.at[k],
                                              device_id=peer, device_id_type=MESH)
            cp.wait_send()
            cp.wait_recv()

    out = pl.pallas_call(
        body, name=f"sibling_swap_wait_l{layer}",
        in_specs=[HBM] * n + [SEM, SEM, ANY], out_specs=[HBM] * n,
        out_shape=[pltpu.HBM(a.shape, a.dtype) for a in arrays],
        input_output_aliases={i: i for i in range(n)},
        compiler_params=pltpu.CompilerParams(has_side_effects=DATAFLOW_EFFECT),
    )(*arrays, sems[0], sems[1], after)
    return list(out)


def kernel(x, c, ada_w, ada_b, pre_norm_g, w_in, conv_w, conv_b, gate_a_w, gate_a_b, gate_x_w, gate_x_b, lru_lambda, pool_w, pool_b, pool_scale, w_out, post_norm_g, loss_target, m_ada_w, m_ada_b, m_pre_norm_g, m_w_in, m_conv_w, m_conv_b, m_gate_a_w, m_gate_a_b, m_gate_x_w, m_gate_x_b, m_lru_lambda, m_pool_w, m_pool_b, m_pool_scale, m_w_out, m_post_norm_g, v_ada_w, v_ada_b, v_pre_norm_g, v_w_in, v_conv_w, v_conv_b, v_gate_a_w, v_gate_a_b, v_gate_x_w, v_gate_x_b, v_lru_lambda, v_pool_w, v_pool_b, v_pool_scale, v_w_out, v_post_norm_g):
    nl, d, _ = ada_w.shape
    s = x.shape[1]
    nh, hd = gate_a_w.shape[1], gate_a_w.shape[2]
    ng, gq, gd = pool_w.shape[1], pool_w.shape[2], pool_w.shape[3]
    me = 4 * lax.axis_index("x") + 2 * lax.axis_index("y") + lax.axis_index("c")
    chip = 2 * lax.axis_index("x") + lax.axis_index("y")
    chip_arr = jnp.reshape(chip, (1,)).astype(jnp.int32)
    x0 = x.reshape(s, d)
    target = loss_target.reshape(s, d)
    p_in = w_in.shape[2]

    c_row = c.reshape(1, d)
    cbuf, modbuf, mod_token = _mod_exchange(c_row, ada_w, c_row)
    vecs, rvecs = _pack_vectors(modbuf, ada_b, pre_norm_g, post_norm_g, conv_b, gate_a_b, gate_x_b, lru_lambda)

    win = [_into_slot(w_in, BF16, chip_arr, f"slot_w_in_l{l}", l) for l in range(nl)]
    wout = [_into_slot(w_out, BF16, chip_arr, f"slot_w_out_l{l}", l) for l in range(nl)]
    pw = [_into_slot(pool_w.reshape(nl, ng * gq, gd), BF16, chip_arr, f"slot_pool_w_l{l}", l) for l in range(nl)]
    convw = _into_slot(conv_w.reshape(nl * CONV_WIDTH, d // 4), F32, chip_arr, "slot_conv_w", after=mod_token)
    poolb = _into_slot(pool_b.reshape(nl * ng, gq), F32, chip_arr, "slot_pool_b")
    lands = [win[0], convw, poolb, *pw, wout[0]]
    split = [True, False, False] + [True] * (nl + 1)
    groups = [[0], list(range(1, len(lands)))]
    for l in range(1, nl):
        groups.append([len(lands), len(lands) + 1])
        lands += [win[l], wout[l]]
        split += [True, True]
    sems, lands, _ = _gather_start(lands, groups, split, "weight_gather_start")
    wa_b, wx_b = gate_a_w.astype(BF16), gate_x_w.astype(BF16)

    def gathered(gi, after, tag):
        idxs = groups[gi]
        arrays, halves = [lands[i] for i in idxs], [split[i] for i in idxs]
        between, arrays = _gather_forward(arrays, halves, sems[gi], after, f"weight_gather_forward_{tag}")
        return _gather_wait(arrays, halves, between, f"weight_gather_wait_{tag}")

    xs, projs, hss, ycats, ys = [x0], [], [], [], []
    sq = None
    convw_full = poolw_full = pvecs = None
    for l in range(nl):
        if l == 0:
            (win[0],) = gathered(0, modbuf, "a")
        proj = _inproj_fwd(xs[l], vecs[l], win[l], l)
        if l == 0:
            got = gathered(1, proj, "b")
            wout[0] = got[2 + nl]
            convw_full, pvecs, poolw_full = _pack_gathered(got[0], got[1], got[2:2 + nl], pool_scale, ng)
        ycat, hs = _rnn_fwd(proj, convw_full[l], rvecs[l], wa_b[l], wx_b[l], l)
        if l + 1 < nl:
            win[l + 1], wout[l + 1] = gathered(2 + l, hs, f"c{l + 1}")
        ycat = _pool_fwd(proj, ycat, poolw_full[l], pvecs[l], l)
        y, xo, sq = _outproj_fwd(ycat, wout[l], xs[l], vecs[l], target if l == nl - 1 else None, l)
        projs.append(proj), hss.append(hs), ycats.append(ycat), ys.append(y), xs.append(xo)

    c_all_t = cbuf.reshape(8, 8, d)[:, 0, :].T

    def finish(l, flights, after, prev):
        (sems_a, arr_a), (sems_g, arr_g), (sems_b, arr_b), (sems_c, arr_c) = flights
        dwout_l, rwout = _exchange_wait(arr_a, sems_a, (1, 0), gq, after, f"grad_wait_a_l{l}")
        dpw_l, rpw, gates = _exchange_wait(arr_g, sems_g, (1, 1), gq, rwout, f"grad_wait_g_l{l}")
        dwin_l, rwin = _exchange_wait(arr_b, sems_b, (1, 0), gq, gates, f"grad_wait_b_l{l}")
        (slabs,) = _exchange_wait(arr_c, sems_c, (0, 0), gq, rwin, f"grad_wait_c_l{l}")
        p_win = _sum_owner(dwin_l, rwin, chip_arr, lambda tr: (None, tr, p_in),
                           lambda i, chip: (chip[0], i, 0), f"sum_w_in_l{l}")
        p_wout = _sum_owner(dwout_l, rwout, chip_arr, lambda tr: (None, tr, d),
                            lambda i, chip: (chip[0], i, 0), f"sum_w_out_l{l}")
        p_pw = _sum_owner(dpw_l, rpw, chip_arr, lambda tr: (ng, tr, gd),
                          lambda i, chip: (0, chip[0], 0), f"sum_pool_w_l{l}")
        p_gates = _sum_slots(gates.reshape(4, 2 * nh * hd, hd), f"sum_gates_l{l}", BF16)
        swap_sems, swapping, swap_token = _sibling_start([p_win, p_wout, p_pw, p_gates], l)
        prev = prev or {}
        ada = _adamw_ada_w_layer(c_all_t, slabs, chip_arr, ada_w, m_ada_w, v_ada_w, l, prev.get("ada_w"),
                                 f"adamw_ada_w_l{l}", swap_token)
        p_win, p_wout, p_pw, p_gates, q_win, q_wout, q_pw, q_gates = _sibling_wait(swapping, swap_sems, ada[3], l)
        big = {
            "w_in": _adamw_layer(w_in, m_w_in, v_w_in, [p_win, q_win], l, prev.get("w_in"), f"adamw_w_in_l{l}"),
            "w_out": _adamw_layer(w_out, m_w_out, v_w_out, [p_wout, q_wout], l, prev.get("w_out"), f"adamw_w_out_l{l}"),
            "pool_w": _adamw_layer(pool_w, m_pool_w, v_pool_w, [p_pw, q_pw], l, prev.get("pool_w"), f"adamw_pool_w_l{l}"),
            "gate_a_w": _adamw_layer(gate_a_w, m_gate_a_w, v_gate_a_w, [p_gates, q_gates], l, prev.get("gate_a_w"),
                                     f"adamw_gate_a_w_l{l}"),
            "gate_x_w": _adamw_layer(gate_x_w, m_gate_x_w, v_gate_x_w, [p_gates, q_gates], l, prev.get("gate_x_w"),
                                     f"adamw_gate_x_w_l{l}", grad_row_offset=nh * hd),
            "ada_w": ada,
        }
        return big, _sum_slots(slabs, f"sum_slab_l{l}")

    dx = xs[nl]
    flights = token = big = None
    totals = [None] * nl
    for l in reversed(range(nl)):
        vec_l = vecs[l]
        dycat, dwout_l, dvec_o = _outproj_bwd(dx, ys[l], ycats[l], wout[l], vec_l, l, vec_l if token is None else token)
        sems_a, arr_a, tok_a = _exchange_start([dwout_l, lax.empty(dwout_l.shape, BF16)], (1, 0), gq, f"grad_start_a_l{l}")
        dproj, dgates, dvec_r = _rnn_bwd(projs[l], hss[l], dycat, convw_full[l], rvecs[l], wa_b[l], wx_b[l], l, tok_a)
        dproj, dpw_l, dvec_p = _pool_bwd(projs[l], dycat, dproj, poolw_full[l], pvecs[l], l)
        gates4 = lax.dynamic_update_slice(lax.empty((4, *dgates.shape), BF16), dgates[None], (chip, 0, 0, 0, 0))
        sems_g, arr_g, tok_g = _exchange_start([dpw_l, lax.empty((4, ng, gq, gd), BF16), gates4], (1, 1), gq,
                                               f"grad_start_g_l{l}")
        dwin_l = _inproj_bwd_w(dproj, xs[l], vec_l, l, tok_g)
        sems_b, arr_b, tok_b = _exchange_start([dwin_l, lax.empty(dwin_l.shape, BF16)], (1, 0), gq, f"grad_start_b_l{l}")
        dx, dvec_i = _inproj_bwd_x(dproj, win[l], xs[l], dx, vec_l, l, tok_b)
        parts = [dvec_i, dvec_o, dvec_r, dvec_p]
        if l == nl - 1:
            parts.append(jnp.tile(sq, (1, d // sq.shape[1])))
        slab = jnp.concatenate(parts, axis=0)
        slabs = lax.dynamic_update_slice(lax.empty((8, *slab.shape), F32), slab[None], (me, 0, 0))
        sems_c, arr_c, token = _exchange_start([slabs], (0, 0), gq, f"grad_start_c_l{l}")
        if flights is not None:
            big, totals[l + 1] = finish(l + 1, flights, token, big)
        flights = ((sems_a, arr_a), (sems_g, arr_g), (sems_b, arr_b), (sems_c, arr_c))
    big, totals[0] = finish(0, flights, big["w_in"][3] if big else dx, big)
    grad_x = dx.reshape(x.shape)
    loss = totals[nl - 1][ROW_SQ, 0] * (0.5 / d)

    small = _adamw_small(totals, chip_arr, [
        (ada_b, m_ada_b, v_ada_b), (pre_norm_g, m_pre_norm_g, v_pre_norm_g), (post_norm_g, m_post_norm_g, v_post_norm_g),
        (conv_b, m_conv_b, v_conv_b), (lru_lambda, m_lru_lambda, v_lru_lambda), (pool_scale, m_pool_scale, v_pool_scale),
        (gate_a_b, m_gate_a_b, v_gate_a_b), (gate_x_b, m_gate_x_b, v_gate_x_b),
        (conv_w, m_conv_w, v_conv_w), (pool_b, m_pool_b, v_pool_b)])

    results = {
        "ada_w": tuple(o.reshape(ada_w.shape) for o in big["ada_w"]),
        "ada_b": small[0],
        "pre_norm_g": small[1],
        "w_in": tuple(o.reshape(w_in.shape) for o in big["w_in"]),
        "conv_w": small[8],
        "conv_b": small[3],
        "gate_a_w": tuple(o.reshape(gate_a_w.shape) for o in big["gate_a_w"]),
        "gate_a_b": small[6],
        "gate_x_w": tuple(o.reshape(gate_x_w.shape) for o in big["gate_x_w"]),
        "gate_x_b": small[7],
        "lru_lambda": small[4],
        "pool_w": tuple(o.reshape(pool_w.shape) for o in big["pool_w"]),
        "pool_b": small[9],
        "pool_scale": small[5],
        "w_out": tuple(o.reshape(w_out.shape) for o in big["w_out"]),
        "post_norm_g": small[2],
    }
    names = list(results)
    return (loss, grad_x,
            *[results[n][0] for n in names], *[results[n][1] for n in names],
            *[results[n][2] for n in names], *[results[n][3] for n in names])
```

```python
import jax
import jax.numpy as jnp
from jax import lax
from jax.experimental import pallas as pl
from jax.experimental.pallas import tpu as pltpu

F32 = jnp.float32
BF16 = jnp.bfloat16

NORM_EPS = 1e-6
LRU_C = 8.0
CONV_WIDTH = 4
HALO = 16
ADAM_LR = 0.001
ADAM_B1 = 0.9
ADAM_B2 = 0.999
ADAM_EPS = 1e-08
ADAM_WD = 0.01
ADAM_STEP = 10

V7X_VMEM_LIMIT_BYTES = 56 * 1024 * 1024
MATMUL_ROWS = 512
SCAN_ROWS = 512
BWD_SCAN_ROWS = 1024
ELEMENTWISE_ROWS = 512

MESH = pl.DeviceIdType.MESH
ANY = pl.BlockSpec(memory_space=pl.ANY)
VMEM = pl.BlockSpec(memory_space=pltpu.VMEM)
HBM = pl.BlockSpec(memory_space=pltpu.HBM)
SEM = pl.BlockSpec(memory_space=pltpu.SEMAPHORE)
DATAFLOW_EFFECT = pltpu.SideEffectType.DATAFLOW_SIDE_EFFECTING

NT_DIMS = (((1,), (1,)), ((), ()))
TN_DIMS = (((0,), (0,)), ((), ()))


def _params(n_grid_axes):
    return pltpu.CompilerParams(dimension_semantics=("arbitrary",) * n_grid_axes,
                                vmem_limit_bytes=V7X_VMEM_LIMIT_BYTES)


def _tile(total, want):
    t = min(want, max(total // 2, HALO))
    assert total % t == 0 and t % HALO == 0, (total, t)
    return t


def _row_tile(rows):
    for t in range(min(rows, ELEMENTWISE_ROWS) // 8 * 8, 0, -8):
        if rows % t == 0:
            return t
    return rows


def _sigmoid(z):
    return 1.0 / (1.0 + jnp.exp(-z))


def _softplus(z):
    return jnp.maximum(z, 0.0) + jnp.log(1.0 + jnp.exp(-jnp.abs(z)))


def _neg_expm1(z):
    return -jnp.tanh(0.5 * z) * (jnp.exp(z) + 1.0)


def _colsum(v):
    return jnp.sum(v, axis=0, keepdims=True)


def _prenorm(xt, vec_ref):
    rs = lax.rsqrt(jnp.mean(xt * xt, axis=-1, keepdims=True) + NORM_EPS)
    xn = xt * rs
    h = xn * vec_ref[3:4, :] * (1.0 + vec_ref[1:2, :]) + vec_ref[0:1, :]
    return h, xn, rs


def _shift_down(v, d, fill):
    t = v.shape[0]
    if d % 8 == 0:
        return jnp.concatenate([jnp.full((d, v.shape[1]), fill, v.dtype), v[:t - d]], axis=0)
    row = lax.broadcasted_iota(jnp.int32, v.shape, 0)
    return jnp.where(row >= d, pltpu.roll(v, d, 0), fill)


def _shift_up(v, d, fill):
    t = v.shape[0]
    if d % 8 == 0:
        return jnp.concatenate([v[d:], jnp.full((d, v.shape[1]), fill, v.dtype)], axis=0)
    row = lax.broadcasted_iota(jnp.int32, v.shape, 0)
    return jnp.where(row < t - d, pltpu.roll(v, t - d, 0), fill)


def _scan_fwd(a, v, h_before):
    d = 1
    while d < a.shape[0]:
        v = v + a * _shift_down(v, d, 0.0)
        a = a * _shift_down(a, d, 1.0)
        d *= 2
    return a * h_before + v


def _scan_rev(b, v):
    d = 1
    while d < b.shape[0]:
        v = v + b * _shift_up(v, d, 0.0)
        b = b * _shift_up(b, d, 0.0)
        d *= 2
    return v


def _inproj_fwd(x, vec, w_all, layer):
    s, d = x.shape
    p = w_all.shape[2]
    ts = _tile(s, MATMUL_ROWS)

    def body(x_ref, vec_ref, w_ref, proj_ref):
        h, _, _ = _prenorm(x_ref[...], vec_ref)
        hb = h.astype(BF16)
        for k in range(4):
            proj_ref[k] = jnp.dot(hb, w_ref[k], preferred_element_type=F32)

    return pl.pallas_call(
        body, name=f"inproj_fwd_l{layer}", grid=(s // ts,),
        in_specs=[pl.BlockSpec((ts, d), lambda i: (i, 0)),
                  pl.BlockSpec((8, d), lambda i: (0, 0)),
                  pl.BlockSpec((4, d, p), lambda i: (0, 0, 0))],
        out_specs=pl.BlockSpec((4, ts, p), lambda i: (0, i, 0)),
        out_shape=jax.ShapeDtypeStruct((4, s, p), F32),
        compiler_params=_params(1),
    )(x, vec, w_all)


HEADS_PER_STEP = 2
BWD_HEADS_PER_STEP = 1


def _rnn_gates(u, wa, wx, vec_ref, lanes):
    ub = u.astype(BF16)
    r = _sigmoid(jnp.dot(ub, wa, preferred_element_type=F32) + vec_ref[1:2, lanes])
    ig = _sigmoid(jnp.dot(ub, wx, preferred_element_type=F32) + vec_ref[2:3, lanes])
    sp = _softplus(-vec_ref[3:4, lanes])
    log_a = (-LRU_C) * r * sp
    return ub, r, ig, sp, log_a


def _conv(xbuf, cw_ref, vec_ref, lanes, ts):
    u = vec_ref[0:1, lanes] + cw_ref[CONV_WIDTH - 1:CONV_WIDTH, lanes] * xbuf[pl.ds(HALO, ts), lanes]
    for k in range(CONV_WIDTH - 1):
        u = u + cw_ref[k:k + 1, lanes] * xbuf[pl.ds(HALO - (CONV_WIDTH - 1) + k, ts), lanes]
    return u


def _rnn_fwd(proj, cw, vec, wa, wx, layer):
    _, s, d = proj.shape
    nh, hd, _ = wa.shape
    ts = _tile(s, SCAN_ROWS)
    hps = HEADS_PER_STEP
    wl = hps * hd

    def body(proj_ref, cw_ref, vec_ref, wa_ref, wx_ref, ycat_ref, hs_ref, xbuf, hlast):
        i = pl.program_id(1)

        @pl.when(i == 0)
        def _():
            xbuf[0:HALO, :] = jnp.zeros((HALO, wl), F32)
            hlast[...] = jnp.zeros_like(hlast)

        xbuf[pl.ds(HALO, ts), :] = proj_ref[0]
        for hh in range(hps):
            lanes = slice(hh * hd, (hh + 1) * hd)
            u = _conv(xbuf, cw_ref, vec_ref, lanes, ts)
            _, _, ig, _, log_a = _rnn_gates(u, wa_ref[hh], wx_ref[hh], vec_ref, lanes)
            a = jnp.exp(log_a)
            mult = jnp.sqrt(_neg_expm1(2.0 * log_a))
            hs = _scan_fwd(a, mult * (ig * u), hlast[0:1, lanes])
            hs_ref[:, lanes] = hs
            hlast[0:1, lanes] = hs_ref[ts - 1:ts, lanes]
            g = proj_ref[1, :, lanes]
            ycat_ref[:, lanes] = (hs * (g * _sigmoid(g))).astype(BF16)
        xbuf[0:HALO, :] = xbuf[pl.ds(ts, HALO), :]

    return pl.pallas_call(
        body, name=f"rnn_fwd_l{layer}", grid=(nh // hps, s // ts),
        in_specs=[pl.BlockSpec((2, ts, wl), lambda h, i: (0, i, h)),
                  pl.BlockSpec((CONV_WIDTH, wl), lambda h, i: (0, h)),
                  pl.BlockSpec((8, wl), lambda h, i: (0, h)),
                  pl.BlockSpec((hps, hd, hd), lambda h, i: (h, 0, 0)),
                  pl.BlockSpec((hps, hd, hd), lambda h, i: (h, 0, 0))],
        out_specs=[pl.BlockSpec((ts, wl), lambda h, i: (i, h)),
                   pl.BlockSpec((ts, wl), lambda h, i: (i, h))],
        out_shape=[jax.ShapeDtypeStruct((s, 2 * d), BF16), jax.ShapeDtypeStruct((s, d), F32)],
        scratch_shapes=[pltpu.VMEM((ts + HALO, wl), F32), pltpu.VMEM((8, wl), F32)],
        compiler_params=_params(2),
    )(proj, cw, vec, wa, wx)


def _inv_count(i, ts, lanes, win):
    t = i * ts + lax.broadcasted_iota(jnp.int32, (ts, lanes), 0)
    return 1.0 / jnp.minimum(t + 1, win).astype(F32)


def _window_sum(ext, win, forward):
    rows = ext.shape[0]
    s, d = ext, 1
    while d < win:
        s = s + pltpu.roll(s, d if forward else rows - d, 0)
        d *= 2
    return s


def _pooled(xbuf, xt, lanes, win, inv_cnt, ts):
    acc = _window_sum(xbuf[:, lanes], win, True)[HALO:, :]
    return acc * inv_cnt - xt


def _pool_fwd(proj, ycat, pw, vec, layer):
    _, s, d = proj.shape
    ng, gd, _ = pw.shape
    ts = _tile(s, MATMUL_ROWS)

    def body(proj_ref, ycat_in, pw_ref, vec_ref, ycat_ref, xbuf):
        del ycat_in
        i = pl.program_id(0)

        @pl.when(i == 0)
        def _():
            xbuf[0:HALO, :] = jnp.zeros((HALO, d), F32)

        xbuf[pl.ds(HALO, ts), :] = proj_ref[0]
        for g in range(ng):
            lanes = slice(g * gd, (g + 1) * gd)
            win = 2 << g
            xt = proj_ref[0, :, lanes]
            pooled = _pooled(xbuf, xt, lanes, win, _inv_count(i, ts, gd, win), ts).astype(BF16)
            z = jnp.dot(pooled, pw_ref[g], preferred_element_type=F32) + vec_ref[0:1, lanes]
            gg = proj_ref[1, :, lanes]
            ycat_ref[:, lanes] = (z * vec_ref[1:2, lanes] * (gg * _sigmoid(gg))).astype(BF16)
        xbuf[0:HALO, :] = xbuf[pl.ds(ts, HALO), :]

    return pl.pallas_call(
        body, name=f"pool_fwd_l{layer}", grid=(s // ts,),
        in_specs=[pl.BlockSpec((2, ts, d), lambda i: (1, i, 0)),
                  ANY,
                  pl.BlockSpec((ng, gd, gd), lambda i: (0, 0, 0)),
                  pl.BlockSpec((8, d), lambda i: (0, 0))],
        out_specs=pl.BlockSpec((ts, d), lambda i: (i, 1)),
        out_shape=jax.ShapeDtypeStruct((s, 2 * d), BF16),
        input_output_aliases={1: 0},
        scratch_shapes=[pltpu.VMEM((ts + HALO, d), F32)],
        compiler_params=_params(1),
    )(proj, ycat, pw, vec)


def _outproj_fwd(ycat, w_all, x, vec, target, layer):
    s, d = x.shape
    nk, kd = w_all.shape[0], w_all.shape[1]
    ts = _tile(s, MATMUL_ROWS)
    last = target is not None

    def body(*refs):
        if last:
            ycat_ref, w_ref, x_ref, vec_ref, tgt_ref, y_ref, xo_ref, sq_ref = refs
        else:
            ycat_ref, w_ref, x_ref, vec_ref, y_ref, xo_ref = refs
        y = jnp.dot(ycat_ref[:, 0:kd], w_ref[0], preferred_element_type=F32)
        for k in range(1, nk):
            y = y + jnp.dot(ycat_ref[:, k * kd:(k + 1) * kd], w_ref[k], preferred_element_type=F32)
        y_ref[...] = y
        rs = lax.rsqrt(jnp.mean(y * y, axis=-1, keepdims=True) + NORM_EPS)
        xo = x_ref[...] + vec_ref[2:3, :] * (y * rs * vec_ref[4:5, :])
        if last:
            err = xo - tgt_ref[...]
            xo_ref[...] = err * (1.0 / d)

            @pl.when(pl.program_id(0) == 0)
            def _():
                sq_ref[...] = jnp.zeros_like(sq_ref)

            sq_ref[...] += jnp.sum(err * err)
        else:
            xo_ref[...] = xo

    row = pl.BlockSpec((ts, d), lambda i: (i, 0))
    in_specs = [pl.BlockSpec((ts, nk * kd), lambda i: (i, 0)),
                pl.BlockSpec((nk, kd, d), lambda i: (0, 0, 0)),
                row, pl.BlockSpec((8, d), lambda i: (0, 0))]
    out_specs = [row, row]
    out_shape = [jax.ShapeDtypeStruct((s, d), F32), jax.ShapeDtypeStruct((s, d), F32)]
    args = [ycat, w_all, x, vec]
    if last:
        in_specs.append(row)
        args.append(target)
        out_specs.append(pl.BlockSpec((8, 128), lambda i: (0, 0)))
        out_shape.append(jax.ShapeDtypeStruct((8, 128), F32))
    out = pl.pallas_call(
        body, name=f"outproj_fwd_l{layer}", grid=(s // ts,),
        in_specs=in_specs, out_specs=out_specs, out_shape=out_shape,
        compiler_params=_params(1),
    )(*args)
    return (out[0], out[1], out[2]) if last else (out[0], out[1], None)


def _outproj_bwd(dxo, y, ycat, w_all, vec, layer, after):
    s, d = dxo.shape
    nk, kd = w_all.shape[0], w_all.shape[1]
    ts = _tile(s, MATMUL_ROWS)
    nt = s // ts

    def body(dxo_ref, y_ref, ycat_ref, w_ref, vec_ref, after_ref, dycat_ref, dw_ref, dvec_ref, acc):
        del after_ref
        i = pl.program_id(0)

        @pl.when(i == 0)
        def _():
            acc[...] = jnp.zeros_like(acc)
            dvec_ref[...] = jnp.zeros_like(dvec_ref)

        yt = y_ref[...]
        rs = lax.rsqrt(jnp.mean(yt * yt, axis=-1, keepdims=True) + NORM_EPS)
        yhat = yt * rs
        gate, gpost = vec_ref[2:3, :], vec_ref[4:5, :]
        dxo_t = dxo_ref[...]
        dyn = dxo_t * gate
        dvec_ref[0:1, :] += _colsum(dxo_t * (yhat * gpost))
        dvec_ref[1:2, :] += _colsum(dyn * yhat)
        t = dyn * gpost
        dy = (rs * (t - yhat * jnp.mean(t * yhat, axis=-1, keepdims=True))).astype(BF16)
        for k in range(nk):
            cols = slice(k * kd, (k + 1) * kd)
            dycat_ref[:, cols] = lax.dot_general(dy, w_ref[k], NT_DIMS, preferred_element_type=F32)
            acc[k] += lax.dot_general(ycat_ref[:, cols], dy, TN_DIMS, preferred_element_type=F32)

        @pl.when(i == nt - 1)
        def _():
            dw_ref[...] = acc[...].astype(BF16)

    row = pl.BlockSpec((ts, d), lambda i: (i, 0))
    wide = pl.BlockSpec((ts, nk * kd), lambda i: (i, 0))
    return pl.pallas_call(
        body, name=f"outproj_bwd_l{layer}", grid=(nt,),
        in_specs=[row, row, wide,
                  pl.BlockSpec((nk, kd, d), lambda i: (0, 0, 0)),
                  pl.BlockSpec((8, d), lambda i: (0, 0)), ANY],
        out_specs=[wide,
                   pl.BlockSpec((nk, kd, d), lambda i: (0, 0, 0)),
                   pl.BlockSpec((8, d), lambda i: (0, 0))],
        out_shape=[jax.ShapeDtypeStruct((s, nk * kd), F32),
                   jax.ShapeDtypeStruct((nk, kd, d), BF16),
                   jax.ShapeDtypeStruct((8, d), F32)],
        scratch_shapes=[pltpu.VMEM((nk, kd, d), F32)],
        compiler_params=_params(1),
    )(dxo, y, ycat, w_all, vec, after)


def _halo_index(ts, nt):
    return lambda j: jnp.maximum((nt - 1 - j) * (ts // HALO) - 1, 0)


def _rnn_bwd(proj, hs, dycat, cw, vec, wa, wx, layer, after):
    _, s, d = proj.shape
    nh, hd, _ = wa.shape
    ts = _tile(s, BWD_SCAN_ROWS)
    nt = s // ts
    halo = _halo_index(ts, nt)
    hps = BWD_HEADS_PER_STEP
    wl = hps * hd

    def body(proj_ref, xh_ref, hs_ref, hsh_ref, dy_ref, cw_ref, vec_ref, wa_ref, wx_ref, after_ref,
             dproj_ref, dgates_ref, dvec_ref, xbuf, hbuf, dubuf, carry, dw_acc):
        del after_ref
        j = pl.program_id(1)
        first_tile = j == nt - 1

        @pl.when(j == 0)
        def _():
            dubuf[pl.ds(ts, HALO), :] = jnp.zeros((HALO, wl), F32)
            carry[...] = jnp.zeros_like(carry)
            dw_acc[...] = jnp.zeros_like(dw_acc)
            dvec_ref[...] = jnp.zeros_like(dvec_ref)

        xbuf[0:HALO, :] = jnp.where(first_tile, 0.0, xh_ref[0])
        xbuf[pl.ds(HALO, ts), :] = proj_ref[0]
        hbuf[0:HALO, :] = jnp.where(first_tile, 0.0, hsh_ref[...])
        hbuf[pl.ds(HALO, ts), :] = hs_ref[...]

        for hh in range(hps):
            lanes = slice(hh * hd, (hh + 1) * hd)
            wa, wx = wa_ref[hh], wx_ref[hh]
            hs = hs_ref[:, lanes]
            u = _conv(xbuf, cw_ref, vec_ref, lanes, ts)
            ub, r, ig, sp, log_a = _rnn_gates(u, wa, wx, vec_ref, lanes)
            a = jnp.exp(log_a)
            e2 = jnp.exp(2.0 * log_a)
            one_minus_a2 = _neg_expm1(2.0 * log_a)
            inv_mult = lax.rsqrt(one_minus_a2)
            mult = one_minus_a2 * inv_mult

            g = proj_ref[1, :, lanes]
            sg = _sigmoid(g)
            dyc = dy_ref[:, lanes]
            dproj_ref[1, :, lanes] = (dyc * hs * (sg * (1.0 + g * (1.0 - sg)))).astype(BF16)

            row = lax.broadcasted_iota(jnp.int32, (ts, hd), 0)
            dhs = dyc * (g * sg) + jnp.where(row == ts - 1, carry[0:1, lanes], 0.0)
            dh = _scan_rev(_shift_up(a, 1, 0.0), dhs)
            carry[:, lanes] = (a * dh)[0:8, :]

            h_prev = hbuf[pl.ds(HALO - 1, ts), lanes]
            dlog_a = dh * h_prev * a - dh * (ig * u) * (e2 * inv_mult)
            di = dh * mult * u
            dzr = dlog_a * ((-LRU_C) * sp) * (r * (1.0 - r))
            dzi = di * (ig * (1.0 - ig))
            dvec_ref[3:4, lanes] += _colsum(dlog_a * r) * (LRU_C * _sigmoid(-vec_ref[3:4, lanes]))
            dvec_ref[1:2, lanes] += _colsum(dzr)
            dvec_ref[2:3, lanes] += _colsum(dzi)
            dzr_b, dzi_b = dzr.astype(BF16), dzi.astype(BF16)
            dw_acc[0, hh] += lax.dot_general(ub, dzr_b, TN_DIMS, preferred_element_type=F32)
            dw_acc[1, hh] += lax.dot_general(ub, dzi_b, TN_DIMS, preferred_element_type=F32)
            du = (dh * mult * ig
                  + lax.dot_general(dzr_b, wa, NT_DIMS, preferred_element_type=F32)
                  + lax.dot_general(dzi_b, wx, NT_DIMS, preferred_element_type=F32))
            dvec_ref[0:1, lanes] += _colsum(du)
            for k in range(CONV_WIDTH):
                dvec_ref[4 + k:5 + k, lanes] += _colsum(du * xbuf[pl.ds(HALO - (CONV_WIDTH - 1) + k, ts), lanes])

            dubuf[0:ts, lanes] = du
            dx = cw_ref[CONV_WIDTH - 1:CONV_WIDTH, lanes] * du
            for k in range(CONV_WIDTH - 1):
                dx = dx + cw_ref[k:k + 1, lanes] * dubuf[pl.ds(CONV_WIDTH - 1 - k, ts), lanes]
            dproj_ref[0, :, lanes] = dx.astype(BF16)
        dubuf[pl.ds(ts, HALO), :] = dubuf[0:HALO, :]

        @pl.when(first_tile)
        def _():
            dgates_ref[...] = dw_acc[...].astype(BF16)

    rev = lambda h, j: (nt - 1 - j, h)
    return pl.pallas_call(
        body, name=f"rnn_bwd_l{layer}", grid=(nh // hps, nt),
        in_specs=[pl.BlockSpec((2, ts, wl), lambda h, j: (0, nt - 1 - j, h)),
                  pl.BlockSpec((1, HALO, wl), lambda h, j: (0, halo(j), h)),
                  pl.BlockSpec((ts, wl), rev),
                  pl.BlockSpec((HALO, wl), lambda h, j: (halo(j), h)),
                  pl.BlockSpec((ts, wl), rev),
                  pl.BlockSpec((CONV_WIDTH, wl), lambda h, j: (0, h)),
                  pl.BlockSpec((8, wl), lambda h, j: (0, h)),
                  pl.BlockSpec((hps, hd, hd), lambda h, j: (h, 0, 0)),
                  pl.BlockSpec((hps, hd, hd), lambda h, j: (h, 0, 0)), ANY],
        out_specs=[pl.BlockSpec((2, ts, wl), lambda h, j: (0, nt - 1 - j, h)),
                   pl.BlockSpec((2, hps, hd, hd), lambda h, j: (0, h, 0, 0)),
                   pl.BlockSpec((16, wl), lambda h, j: (0, h))],
        out_shape=[jax.ShapeDtypeStruct((4, s, d), BF16),
                   jax.ShapeDtypeStruct((2, nh, hd, hd), BF16),
                   jax.ShapeDtypeStruct((16, d), F32)],
        scratch_shapes=[pltpu.VMEM((ts + HALO, wl), F32), pltpu.VMEM((ts + HALO, wl), F32),
                        pltpu.VMEM((ts + HALO, wl), F32), pltpu.VMEM((8, wl), F32),
                        pltpu.VMEM((2, hps, hd, hd), F32)],
        compiler_params=_params(2),
    )(proj, proj, hs, hs, dycat, cw, vec, wa, wx, after)


def _pool_bwd(proj, dycat, dproj, pw, vec, layer):
    _, s, d = proj.shape
    ng, gd, _ = pw.shape
    ts = _tile(s, MATMUL_ROWS)
    nt = s // ts
    halo = _halo_index(ts, nt)

    def body(proj_ref, xh_ref, dy_ref, dproj_in, pw_ref, vec_ref, dproj_ref, dpw_ref, dvec_ref, xbuf, qbuf, acc):
        del dproj_in
        j = pl.program_id(0)
        i = nt - 1 - j

        @pl.when(j == 0)
        def _():
            qbuf[pl.ds(ts, HALO), :] = jnp.zeros((HALO, d), F32)
            acc[...] = jnp.zeros_like(acc)
            dvec_ref[...] = jnp.zeros_like(dvec_ref)

        xbuf[0:HALO, :] = jnp.where(i == 0, 0.0, xh_ref[0])
        xbuf[pl.ds(HALO, ts), :] = proj_ref[0]
        for g in range(ng):
            lanes = slice(g * gd, (g + 1) * gd)
            win = 2 << g
            xt = proj_ref[0, :, lanes]
            inv_cnt = _inv_count(i, ts, gd, win)
            pooled = _pooled(xbuf, xt, lanes, win, inv_cnt, ts).astype(BF16)
            z = jnp.dot(pooled, pw_ref[g], preferred_element_type=F32) + vec_ref[0:1, lanes]
            scale = vec_ref[1:2, lanes]
            gg = proj_ref[1, :, lanes]
            sg = _sigmoid(gg)
            dyc = dy_ref[:, lanes]
            dyp = dyc * (gg * sg)
            dproj_ref[1, :, lanes] = (dyc * (z * scale) * (sg * (1.0 + gg * (1.0 - sg)))).astype(BF16)
            dvec_ref[1:2, lanes] += _colsum(dyp * z)
            dz = dyp * scale
            dvec_ref[0:1, lanes] += _colsum(dz)
            dz_b = dz.astype(BF16)
            acc[g] += lax.dot_general(pooled, dz_b, TN_DIMS, preferred_element_type=F32)
            dpooled = lax.dot_general(dz_b, pw_ref[g], NT_DIMS, preferred_element_type=F32)

            qbuf[0:ts, lanes] = dpooled * inv_cnt
            dx = _window_sum(qbuf[:, lanes], win, False)[0:ts, :] - dpooled
            dproj_ref[0, :, lanes] = dx.astype(BF16)
        qbuf[pl.ds(ts, HALO), :] = qbuf[0:HALO, :]

        @pl.when(j == nt - 1)
        def _():
            dpw_ref[...] = acc[...].astype(BF16)

    return pl.pallas_call(
        body, name=f"pool_bwd_l{layer}", grid=(nt,),
        in_specs=[pl.BlockSpec((2, ts, d), lambda j: (1, nt - 1 - j, 0)),
                  pl.BlockSpec((1, HALO, d), lambda j: (2, halo(j), 0)),
                  pl.BlockSpec((ts, d), lambda j: (nt - 1 - j, 1)),
                  ANY,
                  pl.BlockSpec((ng, gd, gd), lambda j: (0, 0, 0)),
                  pl.BlockSpec((8, d), lambda j: (0, 0))],
        out_specs=[pl.BlockSpec((2, ts, d), lambda j: (1, nt - 1 - j, 0)),
                   pl.BlockSpec((ng, gd, gd), lambda j: (0, 0, 0)),
                   pl.BlockSpec((8, d), lambda j: (0, 0))],
        out_shape=[jax.ShapeDtypeStruct((4, s, d), BF16),
                   jax.ShapeDtypeStruct((ng, gd, gd), BF16),
                   jax.ShapeDtypeStruct((8, d), F32)],
        input_output_aliases={3: 0},
        scratch_shapes=[pltpu.VMEM((ts + HALO, d), F32), pltpu.VMEM((ts + HALO, d), F32),
                        pltpu.VMEM((ng, gd, gd), F32)],
        compiler_params=_params(1),
    )(proj, proj, dycat, dproj, pw, vec)


def _inproj_bwd_x(dproj, w_all, x, dxo, vec, layer, after):
    s, d = x.shape
    p = w_all.shape[2]
    ts = _tile(s, MATMUL_ROWS)

    def body(dp_ref, w_ref, x_ref, dxo_ref, vec_ref, after_ref, dx_ref, dvec_ref):
        del after_ref

        @pl.when(pl.program_id(0) == 0)
        def _():
            dvec_ref[...] = jnp.zeros_like(dvec_ref)

        dh = lax.dot_general(dp_ref[0], w_ref[0], NT_DIMS, preferred_element_type=F32)
        for k in range(1, 4):
            dh = dh + lax.dot_general(dp_ref[k], w_ref[k], NT_DIMS, preferred_element_type=F32)
        _, xn, rs = _prenorm(x_ref[...], vec_ref)
        gpre, scale1 = vec_ref[3:4, :], 1.0 + vec_ref[1:2, :]
        dvec_ref[0:1, :] += _colsum(dh)
        dvec_ref[1:2, :] += _colsum(dh * (xn * gpre))
        dvec_ref[2:3, :] += _colsum(dh * (xn * scale1))
        t = dh * (gpre * scale1)
        dx_ref[...] = dxo_ref[...] + rs * (t - xn * jnp.mean(t * xn, axis=-1, keepdims=True))

    row = pl.BlockSpec((ts, d), lambda i: (i, 0))
    return pl.pallas_call(
        body, name=f"inproj_bwd_x_l{layer}", grid=(s // ts,),
        in_specs=[pl.BlockSpec((4, ts, p), lambda i: (0, i, 0)),
                  pl.BlockSpec((4, d, p), lambda i: (0, 0, 0)),
                  row, row, pl.BlockSpec((8, d), lambda i: (0, 0)), ANY],
        out_specs=[row, pl.BlockSpec((8, d), lambda i: (0, 0))],
        out_shape=[jax.ShapeDtypeStruct((s, d), F32), jax.ShapeDtypeStruct((8, d), F32)],
        compiler_params=_params(1),
    )(dproj, w_all, x, dxo, vec, after)


def _inproj_bwd_w(dproj, x, vec, layer, after):
    s, d = x.shape
    p = dproj.shape[2]
    ts = _tile(s, MATMUL_ROWS)
    nt = s // ts

    def body(dp_ref, x_ref, vec_ref, after_ref, dw_ref, acc):
        del after_ref
        i = pl.program_id(0)

        @pl.when(i == 0)
        def _():
            acc[...] = jnp.zeros_like(acc)

        h, _, _ = _prenorm(x_ref[...], vec_ref)
        hb = h.astype(BF16)
        for k in range(4):
            acc[k] += lax.dot_general(hb, dp_ref[k], TN_DIMS, preferred_element_type=F32)

        @pl.when(i == nt - 1)
        def _():
            dw_ref[...] = acc[...].astype(BF16)

    return pl.pallas_call(
        body, name=f"inproj_bwd_w_l{layer}", grid=(nt,),
        in_specs=[pl.BlockSpec((4, ts, p), lambda i: (0, i, 0)),
                  pl.BlockSpec((ts, d), lambda i: (i, 0)),
                  pl.BlockSpec((8, d), lambda i: (0, 0)), ANY],
        out_specs=pl.BlockSpec((4, d, p), lambda i: (0, 0, 0)),
        out_shape=jax.ShapeDtypeStruct((4, d, p), BF16),
        scratch_shapes=[pltpu.VMEM((4, d, p), F32)],
        compiler_params=_params(1),
    )(dproj, x, vec, after)


def _sum_slots(stacked, name, out_dtype=F32):
    n, rows, cols = stacked.shape
    tr = _row_tile(rows)

    def body(in_ref, out_ref):
        total = in_ref[0].astype(F32)
        for b in range(1, n):
            total = total + in_ref[b].astype(F32)
        out_ref[...] = total.astype(out_dtype)

    return pl.pallas_call(
        body, name=name, grid=(rows // tr,),
        in_specs=[pl.BlockSpec((n, tr, cols), lambda i: (0, i, 0))],
        out_specs=pl.BlockSpec((tr, cols), lambda i: (i, 0)),
        out_shape=jax.ShapeDtypeStruct((rows, cols), out_dtype),
        compiler_params=_params(1),
    )(stacked)


def _adam_update(w, m, v, g):
    m_new = ADAM_B1 * m + (1.0 - ADAM_B1) * g
    v_new = ADAM_B2 * v + (1.0 - ADAM_B2) * (g * g)
    m_hat = m_new / (1.0 - ADAM_B1 ** ADAM_STEP)
    v_hat = v_new / (1.0 - ADAM_B2 ** ADAM_STEP)
    return (-ADAM_LR) * (m_hat / (jnp.sqrt(v_hat) + ADAM_EPS) + ADAM_WD * w), m_new, v_new


def _adamw_layer(w, m, v, grads, layer, prev, name, grad_row_offset=0):
    nl = w.shape[0]
    cols = w.shape[-1]
    rows = w.size // (nl * cols)
    tr = _row_tile(rows)
    off = layer * (rows // tr)
    g_off = grad_row_offset // tr
    n = len(grads)
    n_prev = 0 if prev is None else 4

    def body(*refs):
        w_ref, m_ref, v_ref = refs[:3]
        g_refs = refs[3:3 + n]
        g_out, d_out, m_out, v_out = refs[3 + n + n_prev:]
        g = g_refs[0][...].astype(F32)
        for r in g_refs[1:]:
            g = g + r[...].astype(F32)
        g_out[...] = g
        d_out[...], m_out[...], v_out[...] = _adam_update(w_ref[...], m_ref[...], v_ref[...], g)

    mine = pl.BlockSpec((tr, cols), lambda i: (off + i, 0))
    args = [a.reshape(nl * rows, cols) for a in (w, m, v)] + [g.reshape(-1, cols) for g in grads]
    outs = pl.pallas_call(
        body, name=name, grid=(rows // tr,),
        in_specs=[mine] * 3 + [pl.BlockSpec((tr, cols), lambda i: (g_off + i, 0))] * n + [ANY] * n_prev,
        out_specs=[mine] * 4,
        out_shape=[jax.ShapeDtypeStruct((nl * rows, cols), F32)] * 4,
        input_output_aliases={3 + n + k: k for k in range(n_prev)},
        compiler_params=_params(1),
    )(*args, *(prev or ()))
    return tuple(outs)


def _into_slot(a, dtype, chip_arr, name, layer=None, after=None):
    rows, cols = a.shape[-2:]
    tr = _row_tile(rows)

    def body(chip_ref, a_ref, *rest):
        del chip_ref
        rest[-1][...] = a_ref[...].astype(dtype)

    if layer is None:
        in_spec = pl.BlockSpec((tr, cols), lambda i, chip: (i, 0))
    else:
        in_spec = pl.BlockSpec((None, tr, cols), lambda i, chip: (layer, i, 0))
    extra = [] if after is None else [after]
    return pl.pallas_call(
        body, name=name,
        grid_spec=pltpu.PrefetchScalarGridSpec(
            num_scalar_prefetch=1, grid=(rows // tr,),
            in_specs=[in_spec] + [ANY] * len(extra),
            out_specs=pl.BlockSpec((None, tr, cols), lambda i, chip: (chip[0], i, 0))),
        out_shape=jax.ShapeDtypeStruct((4, rows, cols), dtype),
        compiler_params=_params(1),
    )(chip_arr, a, *extra)


def _sum_owner(own, land, chip_arr, own_block, own_index, name):
    blk = land.shape[1:]
    tr = _row_tile(blk[-2])
    steps = blk[-2] // tr
    tile = (*blk[:-2], tr, blk[-1])
    lead = (0,) * (len(blk) - 2)

    def body(chip_ref, own_ref, l1, l2, l3, out_ref):
        del chip_ref
        total = (own_ref[...].astype(F32) + l1[...].astype(F32)) + (l2[...].astype(F32) + l3[...].astype(F32))
        out_ref[...] = total.astype(BF16)

    def landed(k):
        return pl.BlockSpec((None, *tile), lambda i, chip: (chip[0] ^ k, *lead, i, 0))

    return pl.pallas_call(
        body, name=name,
        grid_spec=pltpu.PrefetchScalarGridSpec(
            num_scalar_prefetch=1, grid=(steps,),
            in_specs=[pl.BlockSpec(own_block(tr), own_index), landed(1), landed(2), landed(3)],
            out_specs=pl.BlockSpec(tile, lambda i, chip: (*lead, i, 0))),
        out_shape=jax.ShapeDtypeStruct(blk, BF16),
        compiler_params=_params(1),
    )(chip_arr, own, land, land, land)


_WHOLE_VMEM = pltpu.CompilerParams(vmem_limit_bytes=V7X_VMEM_LIMIT_BYTES)


def _pack_vectors(modbuf, ada_b, pre_norm_g, post_norm_g, conv_b, gate_a_b, gate_x_b, lru_lambda):
    nl, d = pre_norm_g.shape
    n = modbuf.shape[2] // nl
    nh, hd = gate_a_b.shape[1], gate_a_b.shape[2]

    def body(mb_ref, ab_ref, pre_ref, post_ref, cb_ref, gab_ref, gxb_ref, lam_ref, *outs):
        for layer in range(nl):
            vec_ref, rvec_ref = outs[layer], outs[nl + layer]
            vec_ref[...] = jnp.zeros_like(vec_ref)
            rvec_ref[...] = jnp.zeros_like(rvec_ref)
            for k in range(4):
                piece = mb_ref[k, 0:1, layer * n:(layer + 1) * n] + ab_ref[layer:layer + 1, k * n:(k + 1) * n]
                lo = k * n
                while lo < (k + 1) * n:
                    row = lo // d
                    hi = min((row + 1) * d, (k + 1) * n)
                    vec_ref[row:row + 1, lo - row * d:hi - row * d] = piece[:, lo - k * n:hi - k * n]
                    lo = hi
            vec_ref[3:4, :] = pre_ref[layer:layer + 1, :]
            vec_ref[4:5, :] = post_ref[layer:layer + 1, :]
            rvec_ref[0:1, :] = cb_ref[layer:layer + 1, :]
            for h in range(nh):
                rvec_ref[1:2, h * hd:(h + 1) * hd] = gab_ref[layer, h:h + 1, :]
                rvec_ref[2:3, h * hd:(h + 1) * hd] = gxb_ref[layer, h:h + 1, :]
            rvec_ref[3:4, :] = lam_ref[layer:layer + 1, :]

    out = pl.pallas_call(
        body, name="pack_vectors", in_specs=[VMEM] * 8, out_specs=[VMEM] * (2 * nl),
        out_shape=[jax.ShapeDtypeStruct((8, d), F32)] * (2 * nl), compiler_params=_WHOLE_VMEM,
    )(modbuf, ada_b, pre_norm_g, post_norm_g, conv_b, gate_a_b, gate_x_b, lru_lambda)
    return list(out[:nl]), list(out[nl:])


def _pack_gathered(convw_g, poolb_g, pws, pool_scale, ng):
    nl, d = pool_scale.shape
    taps = convw_g.shape[1] // nl
    dq = convw_g.shape[2]
    gq, gd = poolb_g.shape[2], pws[0].shape[2]

    def body(cg_ref, pb_ref, *rest):
        pw_refs, ps_ref = rest[:nl], rest[nl]
        outs = rest[nl + 1:]
        for layer in range(nl):
            cw_ref, pvec_ref, pwf_ref = outs[layer], outs[nl + layer], outs[2 * nl + layer]
            pvec_ref[...] = jnp.zeros_like(pvec_ref)
            pvec_ref[1:2, :] = ps_ref[layer:layer + 1, :]
            for k in range(4):
                cw_ref[:, k * dq:(k + 1) * dq] = cg_ref[k, layer * taps:(layer + 1) * taps, :]
                for g in range(ng):
                    lo = g * gd + k * gq
                    pvec_ref[0:1, lo:lo + gq] = pb_ref[k, layer * ng + g:layer * ng + g + 1, :]
                    pwf_ref[g, k * gq:(k + 1) * gq, :] = pw_refs[layer][k, g * gq:(g + 1) * gq, :]

    out = pl.pallas_call(
        body, name="pack_gathered", in_specs=[VMEM] * (3 + nl), out_specs=[VMEM] * (3 * nl),
        out_shape=[jax.ShapeDtypeStruct((taps, d), F32)] * nl + [jax.ShapeDtypeStruct((8, d), F32)] * nl
        + [jax.ShapeDtypeStruct((ng, gd, gd), BF16)] * nl,
        compiler_params=_WHOLE_VMEM,
    )(convw_g, poolb_g, *pws, pool_scale)
    return list(out[:nl]), list(out[nl:2 * nl]), list(out[2 * nl:])


ROW_SHIFT, ROW_SCALE, ROW_PRE, ROW_GATE, ROW_POST = 0, 1, 2, 8, 9
ROW_CONV_B, ROW_GATE_A_B, ROW_GATE_X_B, ROW_LAMBDA, ROW_CONV_W = 16, 17, 18, 19, 20
ROW_POOL_B, ROW_POOL_SCALE, ROW_SQ = 32, 33, 40


def _adamw_small(totals, chip_arr, params):
    nl = len(totals)
    d = totals[0].shape[1]
    n_par = len(params)
    flat = [a for p in params for a in p]
    nh, hd = params[6][0].shape[1], params[6][0].shape[2]
    taps, dq = params[8][0].shape[1], params[8][0].shape[2]
    ng, gq = params[9][0].shape[1], params[9][0].shape[2]
    gd = d // ng

    def body(chip_ref, *refs):
        tot = refs[:nl]
        ins = refs[nl:nl + 3 * n_par]
        outs = refs[nl + 3 * n_par:]
        chip = chip_ref[0]

        def update(p, idx, g):
            delta, m_new, v_new = _adam_update(ins[3 * p][idx], ins[3 * p + 1][idx], ins[3 * p + 2][idx], g)
            outs[4 * p][idx] = g
            outs[4 * p + 1][idx] = delta
            outs[4 * p + 2][idx] = m_new
            outs[4 * p + 3][idx] = v_new

        def mine(candidates):
            g = candidates[0]
            for k in range(1, 4):
                g = jnp.where(chip == k, candidates[k], g)
            return g

        for layer in range(nl):
            t = tot[layer]
            row = (slice(layer, layer + 1), slice(None))
            for j, r in enumerate((ROW_SHIFT, ROW_SCALE, ROW_GATE)):
                update(0, (slice(layer, layer + 1), slice(j * d, (j + 1) * d)), t[r:r + 1, :])
            for p, r in ((1, ROW_PRE), (2, ROW_POST), (3, ROW_CONV_B), (4, ROW_LAMBDA), (5, ROW_POOL_SCALE)):
                update(p, row, t[r:r + 1, :])
            for h in range(nh):
                idx = (layer, slice(h, h + 1), slice(None))
                update(6, idx, t[ROW_GATE_A_B:ROW_GATE_A_B + 1, h * hd:(h + 1) * hd])
                update(7, idx, t[ROW_GATE_X_B:ROW_GATE_X_B + 1, h * hd:(h + 1) * hd])
            for k in range(taps):
                r = ROW_CONV_W + k
                update(8, (layer, slice(k, k + 1), slice(None)), mine([t[r:r + 1, c * dq:(c + 1) * dq] for c in range(4)]))
            for g in range(ng):
                cands = [t[ROW_POOL_B:ROW_POOL_B + 1, g * gd + c * gq:g * gd + (c + 1) * gq] for c in range(4)]
                update(9, (layer, slice(g, g + 1), slice(None)), mine(cands))

    out = pl.pallas_call(
        body, name="adamw_small",
        in_specs=[pl.BlockSpec(memory_space=pltpu.SMEM)] + [VMEM] * (nl + 3 * n_par),
        out_specs=[VMEM] * (4 * n_par),
        out_shape=[jax.ShapeDtypeStruct(p[0].shape, F32) for p in params for _ in range(4)],
        compiler_params=_WHOLE_VMEM,
    )(chip_arr, *totals, *flat)
    return [tuple(out[4 * p:4 * p + 4]) for p in range(n_par)]


def _adamw_ada_w_layer(c_t, slabs, chip_arr, w, m, v, layer, prev, name, after):
    nl, d, n = w.shape
    nb = c_t.shape[1]
    tr = _row_tile(d)
    off = layer * (d // tr)
    n_prev = 0 if prev is None else 4
    mod_rows = (ROW_SHIFT, ROW_SCALE, ROW_GATE)

    def body(chip_ref, c_ref, slab_ref, w_ref, m_ref, v_ref, *rest):
        g_out, d_out, m_out, v_out = rest[n_prev + 1:n_prev + 5]
        dm = rest[-1]

        @pl.when(pl.program_id(0) == 0)
        def _():
            for k in range(4):
                @pl.when(chip_ref[0] == k)
                def _():
                    lo = k * n
                    while lo < (k + 1) * n:
                        hi = min((lo // d + 1) * d, (k + 1) * n)
                        row = mod_rows[lo // d]
                        for b in range(nb):
                            dm[b:b + 1, lo - k * n:hi - k * n] = slab_ref[b, row:row + 1, lo % d:lo % d + hi - lo]
                        lo = hi

        g = c_ref[:, 0:1] * dm[0:1, :]
        for b in range(1, nb):
            g = g + c_ref[:, b:b + 1] * dm[b:b + 1, :]
        g_out[...] = g
        d_out[...], m_out[...], v_out[...] = _adam_update(w_ref[...], m_ref[...], v_ref[...], g)

    mine = pl.BlockSpec((tr, n), lambda i, chip: (off + i, 0))
    outs = pl.pallas_call(
        body, name=name,
        grid_spec=pltpu.PrefetchScalarGridSpec(
            num_scalar_prefetch=1, grid=(d // tr,),
            in_specs=[pl.BlockSpec((tr, nb), lambda i, chip: (i, 0)),
                      pl.BlockSpec(slabs.shape, lambda i, chip: (0, 0, 0))] + [mine] * 3 + [ANY] * (n_prev + 1),
            out_specs=[mine] * 4,
            scratch_shapes=[pltpu.VMEM((nb, n), F32)]),
        out_shape=[jax.ShapeDtypeStruct((nl * d, n), F32)] * 4,
        input_output_aliases={6 + k: k for k in range(n_prev)},
        compiler_params=_params(1),
    )(chip_arr, c_t, slabs, *[a.reshape(nl * d, n) for a in (w, m, v)], *(prev or ()), after)
    return tuple(outs)


def _place():
    x, y, c = lax.axis_index("x"), lax.axis_index("y"), lax.axis_index("c")
    return x, y, c


OTHER_CHIPS = ((1, 0), (0, 1), (1, 1))
OTHER_DEVICES = tuple((fx, fy, fc) for fx in (0, 1) for fy in (0, 1) for fc in (0, 1))[1:]


def _mod_exchange(c_row, ada_w, after):
    nl, d, n = ada_w.shape

    def body(c_ref, w_ref, after_ref, cbuf, modbuf, token, cblk, mres, send_a, recv_a, send_c, recv_c):
        del after_ref
        token[...] = jnp.zeros_like(token)
        x, y, c = _place()
        me = 4 * x + 2 * y + c
        chip = 2 * x + y
        cv = c_ref[...]
        cblk[...] = jnp.zeros_like(cblk)
        cblk[0:1, :] = cv * _sigmoid(cv)

        def rows_of(dev):
            return cbuf.at[pl.ds(pl.multiple_of(8 * dev, 8), 8), :]

        cbuf[pl.ds(pl.multiple_of(8 * me, 8), 8), :] = cblk[...]
        sends = []
        for j, (fx, fy, fc) in enumerate(OTHER_DEVICES):
            cp = pltpu.make_async_remote_copy(
                src_ref=cblk, dst_ref=rows_of(me), send_sem=send_a.at[j], recv_sem=recv_a.at[j],
                device_id=(x ^ fx, y ^ fy, c ^ fc), device_id_type=MESH)
            cp.start()
            sends.append(cp)
        for j, (fx, fy, fc) in enumerate(OTHER_DEVICES):
            peer = 4 * (x ^ fx) + 2 * (y ^ fy) + (c ^ fc)
            pltpu.make_async_remote_copy(
                src_ref=cblk, dst_ref=rows_of(peer), send_sem=send_a.at[j], recv_sem=recv_a.at[j],
                device_id=(x ^ fx, y ^ fy, c ^ fc), device_id_type=MESH).wait_recv()
        for cp in sends:
            cp.wait_send()

        call = cbuf[...]
        for layer in range(nl):
            mres[:, layer * n:(layer + 1) * n] = jnp.dot(
                call, w_ref[layer], preferred_element_type=F32, precision=lax.Precision.HIGHEST)

        def block_of(dev):
            return mres.at[pl.ds(pl.multiple_of(8 * dev, 8), 8), :]

        modbuf[chip] = mres[pl.ds(pl.multiple_of(8 * me, 8), 8), :]
        sends = []
        for j, (fx, fy) in enumerate(OTHER_CHIPS):
            peer = 4 * (x ^ fx) + 2 * (y ^ fy) + c
            cp = pltpu.make_async_remote_copy(
                src_ref=block_of(peer), dst_ref=modbuf.at[chip], send_sem=send_c.at[j], recv_sem=recv_c.at[j],
                device_id=(x ^ fx, y ^ fy, c), device_id_type=MESH)
            cp.start()
            sends.append(cp)
        for j, (fx, fy) in enumerate(OTHER_CHIPS):
            pltpu.make_async_remote_copy(
                src_ref=block_of(me), dst_ref=modbuf.at[2 * (x ^ fx) + (y ^ fy)],
                send_sem=send_c.at[j], recv_sem=recv_c.at[j],
                device_id=(x ^ fx, y ^ fy, c), device_id_type=MESH).wait_recv()
        for cp in sends:
            cp.wait_send()

    return pl.pallas_call(
        body, name="mod_exchange", in_specs=[VMEM, VMEM, ANY], out_specs=[VMEM, VMEM, VMEM],
        out_shape=[jax.ShapeDtypeStruct((64, d), F32), jax.ShapeDtypeStruct((4, 8, nl * n), F32),
                   jax.ShapeDtypeStruct((8, 128), F32)],
        scratch_shapes=[pltpu.VMEM((8, d), F32), pltpu.VMEM((64, nl * n), F32),
                        pltpu.SemaphoreType.DMA((7,)), pltpu.SemaphoreType.DMA((7,)),
                        pltpu.SemaphoreType.DMA((3,)), pltpu.SemaphoreType.DMA((3,))],
        compiler_params=pltpu.CompilerParams(vmem_limit_bytes=V7X_VMEM_LIMIT_BYTES, has_side_effects=True),
    )(c_row, ada_w, after)


def _in_hbm(a):
    return pltpu.with_memory_space_constraint(a, pltpu.HBM)


def _gather_copies(lands, split, over_ici):
    x, y, c = _place()
    chip = 2 * x + y
    out = []
    for t, land in enumerate(lands):
        half = land.shape[1] // 2
        mine = pl.ds(pl.multiple_of(c * half, half), half)
        theirs = pl.ds(pl.multiple_of((1 - c) * half, half), half)
        for j, (fx, fy) in enumerate(OTHER_CHIPS):
            them = 2 * (x ^ fx) + (y ^ fy)
            if over_ici and split[t]:
                out.append((land.at[chip, mine], land.at[chip, mine], land.at[them, mine], (x ^ fx, y ^ fy, c), 3 * t + j))
            elif over_ici:
                out.append((land.at[chip], land.at[chip], land.at[them], (x ^ fx, y ^ fy, c), 3 * t + j))
            elif split[t]:
                out.append((land.at[them, mine], land.at[them, mine], land.at[them, theirs], (x, y, 1 - c), 3 * t + j))
    return out


def _gather_start(lands, groups, split, name):
    n, ngr = len(lands), len(groups)

    def body(*refs):
        sems = refs[n:n + 2 * ngr]
        for gi, idxs in enumerate(groups):
            for src, dst, _, peer, k in _gather_copies([refs[i] for i in idxs], [split[i] for i in idxs], True):
                pltpu.make_async_remote_copy(src_ref=src, dst_ref=dst, send_sem=sems[2 * gi].at[k],
                                             recv_sem=sems[2 * gi + 1].at[k], device_id=peer, device_id_type=MESH).start()
        refs[-1][...] = jnp.zeros_like(refs[-1])

    sem_shapes = []
    for idxs in groups:
        sem_shapes += [pltpu.SemaphoreType.DMA((3 * len(idxs),))] * 2
    out = pl.pallas_call(
        body, name=name,
        in_specs=[HBM] * n, out_specs=[SEM] * (2 * ngr) + [HBM] * n + [VMEM],
        out_shape=sem_shapes + [pltpu.HBM(a.shape, a.dtype) for a in lands] + [jax.ShapeDtypeStruct((8, 128), F32)],
        input_output_aliases={i: 2 * ngr + i for i in range(n)},
        compiler_params=pltpu.CompilerParams(has_side_effects=DATAFLOW_EFFECT),
    )(*[_in_hbm(a) for a in lands])
    sems = [(out[2 * gi], out[2 * gi + 1]) for gi in range(ngr)]
    return sems, list(out[2 * ngr:2 * ngr + n]), out[-1]


def _gather_forward(lands, split, sems, after, name):
    n = len(lands)

    def body(*refs):
        ici_send, ici_recv = refs[n], refs[n + 1]
        fwd_send, fwd_recv = refs[n + 3], refs[n + 4]
        forwards = {k: (src, dst, peer) for src, dst, _, peer, k in _gather_copies(refs[:n], split, False)}
        for src, _, landed, peer, k in _gather_copies(refs[:n], split, True):
            cp = pltpu.make_async_remote_copy(src_ref=src, dst_ref=landed, send_sem=ici_send.at[k], recv_sem=ici_recv.at[k],
                                              device_id=peer, device_id_type=MESH)
            cp.wait_recv()
            if k in forwards:
                fsrc, fdst, fpeer = forwards[k]
                pltpu.make_async_remote_copy(src_ref=fsrc, dst_ref=fdst, send_sem=fwd_send.at[k], recv_sem=fwd_recv.at[k],
                                             device_id=fpeer, device_id_type=MESH).start()
            cp.wait_send()

    out = pl.pallas_call(
        body, name=name,
        in_specs=[HBM] * n + [SEM, SEM, ANY], out_specs=[SEM, SEM] + [HBM] * n,
        out_shape=[pltpu.SemaphoreType.DMA((3 * n,))] * 2 + [pltpu.HBM(a.shape, a.dtype) for a in lands],
        input_output_aliases={i: 2 + i for i in range(n)},
        compiler_params=pltpu.CompilerParams(has_side_effects=DATAFLOW_EFFECT),
    )(*lands, sems[0], sems[1], after)
    return (out[0], out[1]), list(out[2:])


def _gather_wait(lands, split, sems, after, name):
    n = len(lands)

    def body(*refs):
        send_sems, recv_sems = refs[n], refs[n + 1]
        for src, _, landed, peer, k in _gather_copies(refs[:n], split, False):
            cp = pltpu.make_async_remote_copy(src_ref=src, dst_ref=landed, send_sem=send_sems.at[k], recv_sem=recv_sems.at[k],
                                              device_id=peer, device_id_type=MESH)
            cp.wait_send()
            cp.wait_recv()

    out = pl.pallas_call(
        body, name=name,
        in_specs=[HBM] * n + [SEM, SEM, ANY], out_specs=[HBM] * n,
        out_shape=[pltpu.HBM(a.shape, a.dtype) for a in lands],
        input_output_aliases={i: i for i in range(n)},
        compiler_params=pltpu.CompilerParams(has_side_effects=DATAFLOW_EFFECT),
    )(*lands, sems[0], sems[1], after)
    return list(out)


def _to_owner_copies(pairs, q):
    x, y, c = _place()
    chip = 2 * x + y
    out = []
    for t, (part, land) in enumerate(pairs):
        for j, (fx, fy) in enumerate(OTHER_CHIPS):
            owner = 2 * (x ^ fx) + (y ^ fy)
            if part.shape[0] == 4 and part.shape[1:] == land.shape[1:]:
                src = part.at[owner]
            else:
                src = part.at[:, pl.ds(pl.multiple_of(owner * q, q), q), :]
            out.append((src, land.at[chip], land.at[owner], (x ^ fx, y ^ fy, c), 3 * t + j))
    return out


def _to_all_copies(bufs, first_sem):
    x, y, c = _place()
    me = 4 * x + 2 * y + c
    out = []
    for t, buf in enumerate(bufs):
        for j, (fx, fy, fc) in enumerate(OTHER_DEVICES):
            them = 4 * (x ^ fx) + 2 * (y ^ fy) + (c ^ fc)
            out.append((buf.at[me], buf.at[me], buf.at[them], (x ^ fx, y ^ fy, c ^ fc), first_sem + 7 * t + j))
    return out


def _to_chips_copies(bufs, first_sem):
    x, y, c = _place()
    chip = 2 * x + y
    out = []
    for t, buf in enumerate(bufs):
        for j, (fx, fy) in enumerate(OTHER_CHIPS):
            them = 2 * (x ^ fx) + (y ^ fy)
            out.append((buf.at[chip], buf.at[chip], buf.at[them], (x ^ fx, y ^ fy, c), first_sem + 3 * t + j))
    return out


def _exchange_copies(refs, kinds, q):
    n_owner, n_chips = kinds
    pairs = list(zip(refs[:n_owner], refs[n_owner:2 * n_owner]))
    first_all = 3 * (n_owner + n_chips)
    return (_to_owner_copies(pairs, q) + _to_chips_copies(refs[2 * n_owner:2 * n_owner + n_chips], 3 * n_owner)
            + _to_all_copies(refs[2 * n_owner + n_chips:], first_all))


def _exchange_start(arrays, kinds, q, name):
    n = len(arrays)
    n_sems = 3 * (kinds[0] + kinds[1]) + 7 * (n - 2 * kinds[0] - kinds[1])

    def body(*refs):
        send_sems, recv_sems = refs[n], refs[n + 1]
        for src, dst, _, peer, k in _exchange_copies(refs[:n], kinds, q):
            pltpu.make_async_remote_copy(src_ref=src, dst_ref=dst, send_sem=send_sems.at[k], recv_sem=recv_sems.at[k],
                                         device_id=peer, device_id_type=MESH).start()
        refs[-1][...] = jnp.zeros_like(refs[-1])

    out = pl.pallas_call(
        body, name=name,
        in_specs=[HBM] * n, out_specs=[SEM, SEM] + [HBM] * n + [VMEM],
        out_shape=[pltpu.SemaphoreType.DMA((n_sems,))] * 2 + [pltpu.HBM(a.shape, a.dtype) for a in arrays]
        + [jax.ShapeDtypeStruct((8, 128), F32)],
        input_output_aliases={i: 2 + i for i in range(n)},
        compiler_params=pltpu.CompilerParams(has_side_effects=DATAFLOW_EFFECT),
    )(*[_in_hbm(a) for a in arrays])
    return (out[0], out[1]), list(out[2:2 + n]), out[-1]


def _exchange_wait(arrays, sems, kinds, q, after, name):
    n = len(arrays)

    def body(*refs):
        send_sems, recv_sems = refs[n], refs[n + 1]
        for src, _, landed, peer, k in _exchange_copies(refs[:n], kinds, q):
            cp = pltpu.make_async_remote_copy(src_ref=src, dst_ref=landed, send_sem=send_sems.at[k], recv_sem=recv_sems.at[k],
                                              device_id=peer, device_id_type=MESH)
            cp.wait_send()
            cp.wait_recv()

    out = pl.pallas_call(
        body, name=name,
        in_specs=[HBM] * n + [SEM, SEM, ANY], out_specs=[HBM] * n,
        out_shape=[pltpu.HBM(a.shape, a.dtype) for a in arrays],
        input_output_aliases={i: i for i in range(n)},
        compiler_params=pltpu.CompilerParams(has_side_effects=DATAFLOW_EFFECT),
    )(*arrays, sems[0], sems[1], after)
    return list(out)


def _sibling_copies(refs):
    n = len(refs) // 2
    x, y, c = _place()
    return [(refs[i], refs[n + i], (x, y, 1 - c), i) for i in range(n)]


def _sibling_start(parts, layer):
    arrays = list(parts) + [lax.empty(a.shape, a.dtype) for a in parts]
    n = len(arrays)

    def body(*refs):
        send_sems, recv_sems = refs[n], refs[n + 1]
        for src, dst, peer, k in _sibling_copies(refs[:n]):
            pltpu.make_async_remote_copy(src_ref=src, dst_ref=dst, send_sem=send_sems.at[k], recv_sem=recv_sems.at[k],
                                         device_id=peer, device_id_type=MESH).start()
        refs[-1][...] = jnp.zeros_like(refs[-1])

    out = pl.pallas_call(
        body, name=f"sibling_swap_start_l{layer}",
        in_specs=[HBM] * n, out_specs=[SEM, SEM] + [HBM] * n + [VMEM],
        out_shape=[pltpu.SemaphoreType.DMA((n // 2,))] * 2 + [pltpu.HBM(a.shape, a.dtype) for a in arrays]
        + [jax.ShapeDtypeStruct((8, 128), F32)],
        input_output_aliases={i: 2 + i for i in range(n)},
        compiler_params=pltpu.CompilerParams(has_side_effects=DATAFLOW_EFFECT),
    )(*[_in_hbm(a) for a in arrays])
    return (out[0], out[1]), list(out[2:2 + n]), out[-1]


def _sibling_wait(arrays, sems, after, layer):
    n = len(arrays)

    def body(*refs):
        send_sems, recv_sems = refs[n], refs[n + 1]
        for src, dst, peer, k in _sibling_copies(refs[:n]):
            cp = pltpu.make_async_remote_copy(src_ref=src, dst_ref=dst, send_sem=send_sems.at[k], recv_sem=recv_sems.at[k],
                                              device_id=peer, device_id_type=MESH)
            cp.wait_send()
            cp.wait_recv()

    out = pl.pallas_call(
        body, name=f"sibling_swap_wait_l{layer}",
        in_specs=[HBM] * n + [SEM, SEM, ANY], out_specs=[HBM] * n,
        out_shape=[pltpu.HBM(a.shape, a.dtype) for a in arrays],
        input_output_aliases={i: i for i in range(n)},
        compiler_params=pltpu.CompilerParams(has_side_effects=DATAFLOW_EFFECT),
    )(*arrays, sems[0], sems[1], after)
    return list(out)


def kernel(x, c, ada_w, ada_b, pre_norm_g, w_in, conv_w, conv_b, gate_a_w, gate_a_b, gate_x_w, gate_x_b, lru_lambda, pool_w, pool_b, pool_scale, w_out, post_norm_g, loss_target, m_ada_w, m_ada_b, m_pre_norm_g, m_w_in, m_conv_w, m_conv_b, m_gate_a_w, m_gate_a_b, m_gate_x_w, m_gate_x_b, m_lru_lambda, m_pool_w, m_pool_b, m_pool_scale, m_w_out, m_post_norm_g, v_ada_w, v_ada_b, v_pre_norm_g, v_w_in, v_conv_w, v_conv_b, v_gate_a_w, v_gate_a_b, v_gate_x_w, v_gate_x_b, v_lru_lambda, v_pool_w, v_pool_b, v_pool_scale, v_w_out, v_post_norm_g):
    nl, d, _ = ada_w.shape
    s = x.shape[1]
    nh, hd = gate_a_w.shape[1], gate_a_w.shape[2]
    ng, gq, gd = pool_w.shape[1], pool_w.shape[2], pool_w.shape[3]
    me = 4 * lax.axis_index("x") + 2 * lax.axis_index("y") + lax.axis_index("c")
    chip = 2 * lax.axis_index("x") + lax.axis_index("y")
    chip_arr = jnp.reshape(chip, (1,)).astype(jnp.int32)
    x0 = x.reshape(s, d)
    target = loss_target.reshape(s, d)
    p_in = w_in.shape[2]

    c_row = c.reshape(1, d)
    cbuf, modbuf, mod_token = _mod_exchange(c_row, ada_w, c_row)
    vecs, rvecs = _pack_vectors(modbuf, ada_b, pre_norm_g, post_norm_g, conv_b, gate_a_b, gate_x_b, lru_lambda)

    win = [_into_slot(w_in, BF16, chip_arr, f"slot_w_in_l{l}", l) for l in range(nl)]
    wout = [_into_slot(w_out, BF16, chip_arr, f"slot_w_out_l{l}", l) for l in range(nl)]
    pw = [_into_slot(pool_w.reshape(nl, ng * gq, gd), BF16, chip_arr, f"slot_pool_w_l{l}", l) for l in range(nl)]
    convw = _into_slot(conv_w.reshape(nl * CONV_WIDTH, d // 4), F32, chip_arr, "slot_conv_w", after=mod_token)
    poolb = _into_slot(pool_b.reshape(nl * ng, gq), F32, chip_arr, "slot_pool_b")
    lands = [win[0], convw, poolb, *pw, wout[0]]
    split = [True, False, False] + [True] * (nl + 1)
    groups = [[0], list(range(1, len(lands)))]
    for l in range(1, nl):
        groups.append([len(lands), len(lands) + 1])
        lands += [win[l], wout[l]]
        split += [True, True]
    sems, lands, _ = _gather_start(lands, groups, split, "weight_gather_start")
    wa_b, wx_b = gate_a_w.astype(BF16), gate_x_w.astype(BF16)

    def forwarded(gi, after, tag):
        idxs = groups[gi]
        arrays, halves = [lands[i] for i in idxs], [split[i] for i in idxs]
        between, arrays = _gather_forward(arrays, halves, sems[gi], after, f"weight_gather_forward_{tag}")
        return arrays, halves, between, tag

    def gathered(flight, after):
        arrays, halves, between, tag = flight
        return _gather_wait(arrays, halves, between, after, f"weight_gather_wait_{tag}")

    xs, projs, hss, ycats, ys = [x0], [], [], [], []
    sq = None
    convw_full = poolw_full = pvecs = None
    for l in range(nl):
        if l == 0:
            (win[0],) = gathered(forwarded(0, modbuf, "a"), modbuf)
        proj = _inproj_fwd(xs[l], vecs[l], win[l], l)
        if l == 0:
            got = gathered(forwarded(1, proj, "b"), proj)
            wout[0] = got[2 + nl]
            convw_full, pvecs, poolw_full = _pack_gathered(got[0], got[1], got[2:2 + nl], pool_scale, ng)
        ycat, hs = _rnn_fwd(proj, convw_full[l], rvecs[l], wa_b[l], wx_b[l], l)
        next_weights = forwarded(2 + l, hs, f"c{l + 1}") if l + 1 < nl else None
        ycat = _pool_fwd(proj, ycat, poolw_full[l], pvecs[l], l)
        y, xo, sq = _outproj_fwd(ycat, wout[l], xs[l], vecs[l], target if l == nl - 1 else None, l)
        if next_weights is not None:
            win[l + 1], wout[l + 1] = gathered(next_weights, xo)
        projs.append(proj), hss.append(hs), ycats.append(ycat), ys.append(y), xs.append(xo)

    c_all_t = cbuf.reshape(8, 8, d)[:, 0, :].T

    def finish(l, flights, after, prev):
        (sems_a, arr_a), (sems_g, arr_g), (sems_b, arr_b), (sems_c, arr_c) = flights
        dwout_l, rwout = _exchange_wait(arr_a, sems_a, (1, 0), gq, after, f"grad_wait_a_l{l}")
        dpw_l, rpw, gates = _exchange_wait(arr_g, sems_g, (1, 1), gq, rwout, f"grad_wait_g_l{l}")
        dwin_l, rwin = _exchange_wait(arr_b, sems_b, (1, 0), gq, gates, f"grad_wait_b_l{l}")
        (slabs,) = _exchange_wait(arr_c, sems_c, (0, 0), gq, rwin, f"grad_wait_c_l{l}")
        p_win = _sum_owner(dwin_l, rwin, chip_arr, lambda tr: (None, tr, p_in),
                           lambda i, chip: (chip[0], i, 0), f"sum_w_in_l{l}")
        p_wout = _sum_owner(dwout_l, rwout, chip_arr, lambda tr: (None, tr, d),
                            lambda i, chip: (chip[0], i, 0), f"sum_w_out_l{l}")
        p_pw = _sum_owner(dpw_l, rpw, chip_arr, lambda tr: (ng, tr, gd),
                          lambda i, chip: (0, chip[0], 0), f"sum_pool_w_l{l}")
        p_gates = _sum_slots(gates.reshape(4, 2 * nh * hd, hd), f"sum_gates_l{l}", BF16)
        swap_sems, swapping, swap_token = _sibling_start([p_win, p_wout, p_pw, p_gates], l)
        prev = prev or {}
        ada = _adamw_ada_w_layer(c_all_t, slabs, chip_arr, ada_w, m_ada_w, v_ada_w, l, prev.get("ada_w"),
                                 f"adamw_ada_w_l{l}", swap_token)
        p_win, p_wout, p_pw, p_gates, q_win, q_wout, q_pw, q_gates = _sibling_wait(swapping, swap_sems, ada[3], l)
        big = {
            "w_in": _adamw_layer(w_in, m_w_in, v_w_in, [p_win, q_win], l, prev.get("w_in"), f"adamw_w_in_l{l}"),
            "w_out": _adamw_layer(w_out, m_w_out, v_w_out, [p_wout, q_wout], l, prev.get("w_out"), f"adamw_w_out_l{l}"),
            "pool_w": _adamw_layer(pool_w, m_pool_w, v_pool_w, [p_pw, q_pw], l, prev.get("pool_w"), f"adamw_pool_w_l{l}"),
            "gate_a_w": _adamw_layer(gate_a_w, m_gate_a_w, v_gate_a_w, [p_gates, q_gates], l, prev.get("gate_a_w"),
                                     f"adamw_gate_a_w_l{l}"),
            "gate_x_w": _adamw_layer(gate_x_w, m_gate_x_w, v_gate_x_w, [p_gates, q_gates], l, prev.get("gate_x_w"),
                                     f"adamw_gate_x_w_l{l}", grad_row_offset=nh * hd),
            "ada_w": ada,
        }
        return big, _sum_slots(slabs, f"sum_slab_l{l}")

    dx = xs[nl]
    flights = token = big = None
    totals = [None] * nl
    for l in reversed(range(nl)):
        vec_l = vecs[l]
        dycat, dwout_l, dvec_o = _outproj_bwd(dx, ys[l], ycats[l], wout[l], vec_l, l, vec_l if token is None else token)
        sems_a, arr_a, tok_a = _exchange_start([dwout_l, lax.empty(dwout_l.shape, BF16)], (1, 0), gq, f"grad_start_a_l{l}")
        dproj, dgates, dvec_r = _rnn_bwd(projs[l], hss[l], dycat, convw_full[l], rvecs[l], wa_b[l], wx_b[l], l, tok_a)
        dproj, dpw_l, dvec_p = _pool_bwd(projs[l], dycat, dproj, poolw_full[l], pvecs[l], l)
        gates4 = lax.dynamic_update_slice(lax.empty((4, *dgates.shape), BF16), dgates[None], (chip, 0, 0, 0, 0))
        sems_g, arr_g, tok_g = _exchange_start([dpw_l, lax.empty((4, ng, gq, gd), BF16), gates4], (1, 1), gq,
                                               f"grad_start_g_l{l}")
        dwin_l = _inproj_bwd_w(dproj, xs[l], vec_l, l, tok_g)
        sems_b, arr_b, tok_b = _exchange_start([dwin_l, lax.empty(dwin_l.shape, BF16)], (1, 0), gq, f"grad_start_b_l{l}")
        dx, dvec_i = _inproj_bwd_x(dproj, win[l], xs[l], dx, vec_l, l, tok_b)
        parts = [dvec_i, dvec_o, dvec_r, dvec_p]
        if l == nl - 1:
            parts.append(jnp.tile(sq, (1, d // sq.shape[1])))
        slab = jnp.concatenate(parts, axis=0)
        slabs = lax.dynamic_update_slice(lax.empty((8, *slab.shape), F32), slab[None], (me, 0, 0))
        sems_c, arr_c, token = _exchange_start([slabs], (0, 0), gq, f"grad_start_c_l{l}")
        if flights is not None:
            big, totals[l + 1] = finish(l + 1, flights, token, big)
        flights = ((sems_a, arr_a), (sems_g, arr_g), (sems_b, arr_b), (sems_c, arr_c))
    big, totals[0] = finish(0, flights, big["w_in"][3] if big else dx, big)
    grad_x = dx.reshape(x.shape)
    loss = totals[nl - 1][ROW_SQ, 0] * (0.5 / d)

    small = _adamw_small(totals, chip_arr, [
        (ada_b, m_ada_b, v_ada_b), (pre_norm_g, m_pre_norm_g, v_pre_norm_g), (post_norm_g, m_post_norm_g, v_post_norm_g),
        (conv_b, m_conv_b, v_conv_b), (lru_lambda, m_lru_lambda, v_lru_lambda), (pool_scale, m_pool_scale, v_pool_scale),
        (gate_a_b, m_gate_a_b, v_gate_a_b), (gate_x_b, m_gate_x_b, v_gate_x_b),
        (conv_w, m_conv_w, v_conv_w), (pool_b, m_pool_b, v_pool_b)])

    results = {
        "ada_w": tuple(o.reshape(ada_w.shape) for o in big["ada_w"]),
        "ada_b": small[0],
        "pre_norm_g": small[1],
        "w_in": tuple(o.reshape(w_in.shape) for o in big["w_in"]),
        "conv_w": small[8],
        "conv_b": small[3],
        "gate_a_w": tuple(o.reshape(gate_a_w.shape) for o in big["gate_a_w"]),
        "gate_a_b": small[6],
        "gate_x_w": tuple(o.reshape(gate_x_w.shape) for o in big["gate_x_w"]),
        "gate_x_b": small[7],
        "lru_lambda": small[4],
        "pool_w": tuple(o.reshape(pool_w.shape) for o in big["pool_w"]),
        "pool_b": small[9],
        "pool_scale": small[5],
        "w_out": tuple(o.reshape(w_out.shape) for o in big["w_out"]),
        "post_norm_g": small[2],
    }
    names = list(results)
    return (loss, grad_x,
            *[results[n][0] for n in names], *[results[n][1] for n in names],
            *[results[n][2] for n in names], *[results[n][3] for n in names])
```

```python
import jax
import jax.numpy as jnp
from jax import lax
from jax.experimental import pallas as pl
from jax.experimental.pallas import tpu as pltpu

F32 = jnp.float32
BF16 = jnp.bfloat16

NORM_EPS = 1e-6
LRU_C = 8.0
CONV_WIDTH = 4
HALO = 16
ADAM_LR = 0.001
ADAM_B1 = 0.9
ADAM_B2 = 0.999
ADAM_EPS = 1e-08
ADAM_WD = 0.01
ADAM_STEP = 10

V7X_VMEM_LIMIT_BYTES = 56 * 1024 * 1024
MATMUL_ROWS = 512
SCAN_ROWS = 512
BWD_SCAN_ROWS = 1024
ELEMENTWISE_ROWS = 512

MESH = pl.DeviceIdType.MESH
ANY = pl.BlockSpec(memory_space=pl.ANY)
VMEM = pl.BlockSpec(memory_space=pltpu.VMEM)
HBM = pl.BlockSpec(memory_space=pltpu.HBM)
SEM = pl.BlockSpec(memory_space=pltpu.SEMAPHORE)
DATAFLOW_EFFECT = pltpu.SideEffectType.DATAFLOW_SIDE_EFFECTING

NT_DIMS = (((1,), (1,)), ((), ()))
TN_DIMS = (((0,), (0,)), ((), ()))


def _params(n_grid_axes):
    return pltpu.CompilerParams(dimension_semantics=("arbitrary",) * n_grid_axes,
                                vmem_limit_bytes=V7X_VMEM_LIMIT_BYTES)


def _tile(total, want):
    t = min(want, max(total // 2, HALO))
    assert total % t == 0 and t % HALO == 0, (total, t)
    return t


def _row_tile(rows):
    for t in range(min(rows, ELEMENTWISE_ROWS) // 8 * 8, 0, -8):
        if rows % t == 0:
            return t
    return rows


def _sigmoid(z):
    return 1.0 / (1.0 + jnp.exp(-z))


def _softplus(z):
    return jnp.maximum(z, 0.0) + jnp.log(1.0 + jnp.exp(-jnp.abs(z)))


def _neg_expm1(z):
    return -jnp.tanh(0.5 * z) * (jnp.exp(z) + 1.0)


def _colsum(v):
    return jnp.sum(v, axis=0, keepdims=True)


def _prenorm(xt, vec_ref):
    rs = lax.rsqrt(jnp.mean(xt * xt, axis=-1, keepdims=True) + NORM_EPS)
    xn = xt * rs
    h = xn * vec_ref[3:4, :] * (1.0 + vec_ref[1:2, :]) + vec_ref[0:1, :]
    return h, xn, rs


def _shift_down(v, d, fill):
    t = v.shape[0]
    if d % 8 == 0:
        return jnp.concatenate([jnp.full((d, v.shape[1]), fill, v.dtype), v[:t - d]], axis=0)
    row = lax.broadcasted_iota(jnp.int32, v.shape, 0)
    return jnp.where(row >= d, pltpu.roll(v, d, 0), fill)


def _shift_up(v, d, fill):
    t = v.shape[0]
    if d % 8 == 0:
        return jnp.concatenate([v[d:], jnp.full((d, v.shape[1]), fill, v.dtype)], axis=0)
    row = lax.broadcasted_iota(jnp.int32, v.shape, 0)
    return jnp.where(row < t - d, pltpu.roll(v, t - d, 0), fill)


def _scan_fwd(a, v, h_before):
    d = 1
    while d < a.shape[0]:
        v = v + a * _shift_down(v, d, 0.0)
        a = a * _shift_down(a, d, 1.0)
        d *= 2
    return a * h_before + v


def _scan_rev(b, v):
    d = 1
    while d < b.shape[0]:
        v = v + b * _shift_up(v, d, 0.0)
        b = b * _shift_up(b, d, 0.0)
        d *= 2
    return v


def _inproj_fwd(x, vec, w_all, layer):
    s, d = x.shape
    p = w_all.shape[2]
    ts = _tile(s, MATMUL_ROWS)

    def body(x_ref, vec_ref, w_ref, proj_ref):
        h, _, _ = _prenorm(x_ref[...], vec_ref)
        hb = h.astype(BF16)
        for k in range(4):
            proj_ref[k] = jnp.dot(hb, w_ref[k], preferred_element_type=F32)

    return pl.pallas_call(
        body, name=f"inproj_fwd_l{layer}", grid=(s // ts,),
        in_specs=[pl.BlockSpec((ts, d), lambda i: (i, 0)),
                  pl.BlockSpec((8, d), lambda i: (0, 0)),
                  pl.BlockSpec((4, d, p), lambda i: (0, 0, 0))],
        out_specs=pl.BlockSpec((4, ts, p), lambda i: (0, i, 0)),
        out_shape=jax.ShapeDtypeStruct((4, s, p), F32),
        compiler_params=_params(1),
    )(x, vec, w_all)


HEADS_PER_STEP = 2
BWD_HEADS_PER_STEP = 1


def _rnn_gates(u, wa, wx, vec_ref, lanes):
    ub = u.astype(BF16)
    r = _sigmoid(jnp.dot(ub, wa, preferred_element_type=F32) + vec_ref[1:2, lanes])
    ig = _sigmoid(jnp.dot(ub, wx, preferred_element_type=F32) + vec_ref[2:3, lanes])
    sp = _softplus(-vec_ref[3:4, lanes])
    log_a = (-LRU_C) * r * sp
    return ub, r, ig, sp, log_a


def _conv(xbuf, cw_ref, vec_ref, lanes, ts):
    u = vec_ref[0:1, lanes] + cw_ref[CONV_WIDTH - 1:CONV_WIDTH, lanes] * xbuf[pl.ds(HALO, ts), lanes]
    for k in range(CONV_WIDTH - 1):
        u = u + cw_ref[k:k + 1, lanes] * xbuf[pl.ds(HALO - (CONV_WIDTH - 1) + k, ts), lanes]
    return u


def _rnn_fwd(proj, cw, vec, wa, wx, layer):
    _, s, d = proj.shape
    nh, hd, _ = wa.shape
    ts = _tile(s, SCAN_ROWS)
    hps = HEADS_PER_STEP
    wl = hps * hd

    def body(proj_ref, cw_ref, vec_ref, wa_ref, wx_ref, ycat_ref, hs_ref, xbuf, hlast):
        i = pl.program_id(1)

        @pl.when(i == 0)
        def _():
            xbuf[0:HALO, :] = jnp.zeros((HALO, wl), F32)
            hlast[...] = jnp.zeros_like(hlast)

        xbuf[pl.ds(HALO, ts), :] = proj_ref[0]
        for hh in range(hps):
            lanes = slice(hh * hd, (hh + 1) * hd)
            u = _conv(xbuf, cw_ref, vec_ref, lanes, ts)
            _, _, ig, _, log_a = _rnn_gates(u, wa_ref[hh], wx_ref[hh], vec_ref, lanes)
            a = jnp.exp(log_a)
            mult = jnp.sqrt(_neg_expm1(2.0 * log_a))
            hs = _scan_fwd(a, mult * (ig * u), hlast[0:1, lanes])
            hs_ref[:, lanes] = hs
            hlast[0:1, lanes] = hs_ref[ts - 1:ts, lanes]
            g = proj_ref[1, :, lanes]
            ycat_ref[:, lanes] = (hs * (g * _sigmoid(g))).astype(BF16)
        xbuf[0:HALO, :] = xbuf[pl.ds(ts, HALO), :]

    return pl.pallas_call(
        body, name=f"rnn_fwd_l{layer}", grid=(nh // hps, s // ts),
        in_specs=[pl.BlockSpec((2, ts, wl), lambda h, i: (0, i, h)),
                  pl.BlockSpec((CONV_WIDTH, wl), lambda h, i: (0, h)),
                  pl.BlockSpec((8, wl), lambda h, i: (0, h)),
                  pl.BlockSpec((hps, hd, hd), lambda h, i: (h, 0, 0)),
                  pl.BlockSpec((hps, hd, hd), lambda h, i: (h, 0, 0))],
        out_specs=[pl.BlockSpec((ts, wl), lambda h, i: (i, h)),
                   pl.BlockSpec((ts, wl), lambda h, i: (i, h))],
        out_shape=[jax.ShapeDtypeStruct((s, 2 * d), BF16), jax.ShapeDtypeStruct((s, d), F32)],
        scratch_shapes=[pltpu.VMEM((ts + HALO, wl), F32), pltpu.VMEM((8, wl), F32)],
        compiler_params=_params(2),
    )(proj, cw, vec, wa, wx)


def _inv_count(i, ts, lanes, win):
    t = i * ts + lax.broadcasted_iota(jnp.int32, (ts, lanes), 0)
    return 1.0 / jnp.minimum(t + 1, win).astype(F32)


def _window_sum(ext, win, forward):
    rows = ext.shape[0]
    s, d = ext, 1
    while d < win:
        s = s + pltpu.roll(s, d if forward else rows - d, 0)
        d *= 2
    return s


def _pooled(xbuf, xt, lanes, win, inv_cnt, ts):
    acc = _window_sum(xbuf[:, lanes], win, True)[HALO:, :]
    return acc * inv_cnt - xt


def _pool_fwd(proj, ycat, pw, vec, layer):
    _, s, d = proj.shape
    ng, gd, _ = pw.shape
    ts = _tile(s, MATMUL_ROWS)

    def body(proj_ref, ycat_in, pw_ref, vec_ref, ycat_ref, xbuf):
        del ycat_in
        i = pl.program_id(0)

        @pl.when(i == 0)
        def _():
            xbuf[0:HALO, :] = jnp.zeros((HALO, d), F32)

        xbuf[pl.ds(HALO, ts), :] = proj_ref[0]
        for g in range(ng):
            lanes = slice(g * gd, (g + 1) * gd)
            win = 2 << g
            xt = proj_ref[0, :, lanes]
            pooled = _pooled(xbuf, xt, lanes, win, _inv_count(i, ts, gd, win), ts).astype(BF16)
            z = jnp.dot(pooled, pw_ref[g], preferred_element_type=F32) + vec_ref[0:1, lanes]
            gg = proj_ref[1, :, lanes]
            ycat_ref[:, lanes] = (z * vec_ref[1:2, lanes] * (gg * _sigmoid(gg))).astype(BF16)
        xbuf[0:HALO, :] = xbuf[pl.ds(ts, HALO), :]

    return pl.pallas_call(
        body, name=f"pool_fwd_l{layer}", grid=(s // ts,),
        in_specs=[pl.BlockSpec((2, ts, d), lambda i: (1, i, 0)),
                  ANY,
                  pl.BlockSpec((ng, gd, gd), lambda i: (0, 0, 0)),
                  pl.BlockSpec((8, d), lambda i: (0, 0))],
        out_specs=pl.BlockSpec((ts, d), lambda i: (i, 1)),
        out_shape=jax.ShapeDtypeStruct((s, 2 * d), BF16),
        input_output_aliases={1: 0},
        scratch_shapes=[pltpu.VMEM((ts + HALO, d), F32)],
        compiler_params=_params(1),
    )(proj, ycat, pw, vec)


def _outproj_fwd(ycat, w_all, x, vec, target, layer):
    s, d = x.shape
    nk, kd = w_all.shape[0], w_all.shape[1]
    ts = _tile(s, MATMUL_ROWS)
    last = target is not None

    def body(*refs):
        if last:
            ycat_ref, w_ref, x_ref, vec_ref, tgt_ref, y_ref, xo_ref, sq_ref = refs
        else:
            ycat_ref, w_ref, x_ref, vec_ref, y_ref, xo_ref = refs
        y = jnp.dot(ycat_ref[:, 0:kd], w_ref[0], preferred_element_type=F32)
        for k in range(1, nk):
            y = y + jnp.dot(ycat_ref[:, k * kd:(k + 1) * kd], w_ref[k], preferred_element_type=F32)
        y_ref[...] = y
        rs = lax.rsqrt(jnp.mean(y * y, axis=-1, keepdims=True) + NORM_EPS)
        xo = x_ref[...] + vec_ref[2:3, :] * (y * rs * vec_ref[4:5, :])
        if last:
            err = xo - tgt_ref[...]
            xo_ref[...] = err * (1.0 / d)

            @pl.when(pl.program_id(0) == 0)
            def _():
                sq_ref[...] = jnp.zeros_like(sq_ref)

            sq_ref[...] += jnp.sum(err * err)
        else:
            xo_ref[...] = xo

    row = pl.BlockSpec((ts, d), lambda i: (i, 0))
    in_specs = [pl.BlockSpec((ts, nk * kd), lambda i: (i, 0)),
                pl.BlockSpec((nk, kd, d), lambda i: (0, 0, 0)),
                row, pl.BlockSpec((8, d), lambda i: (0, 0))]
    out_specs = [row, row]
    out_shape = [jax.ShapeDtypeStruct((s, d), F32), jax.ShapeDtypeStruct((s, d), F32)]
    args = [ycat, w_all, x, vec]
    if last:
        in_specs.append(row)
        args.append(target)
        out_specs.append(pl.BlockSpec((8, 128), lambda i: (0, 0)))
        out_shape.append(jax.ShapeDtypeStruct((8, 128), F32))
    out = pl.pallas_call(
        body, name=f"outproj_fwd_l{layer}", grid=(s // ts,),
        in_specs=in_specs, out_specs=out_specs, out_shape=out_shape,
        compiler_params=_params(1),
    )(*args)
    return (out[0], out[1], out[2]) if last else (out[0], out[1], None)


def _outproj_bwd(dxo, y, ycat, w_all, vec, layer, after):
    s, d = dxo.shape
    nk, kd = w_all.shape[0], w_all.shape[1]
    ts = _tile(s, MATMUL_ROWS)
    nt = s // ts

    def body(dxo_ref, y_ref, ycat_ref, w_ref, vec_ref, after_ref, dycat_ref, dw_ref, dvec_ref, acc):
        del after_ref
        i = pl.program_id(0)

        @pl.when(i == 0)
        def _():
            acc[...] = jnp.zeros_like(acc)
            dvec_ref[...] = jnp.zeros_like(dvec_ref)

        yt = y_ref[...]
        rs = lax.rsqrt(jnp.mean(yt * yt, axis=-1, keepdims=True) + NORM_EPS)
        yhat = yt * rs
        gate, gpost = vec_ref[2:3, :], vec_ref[4:5, :]
        dxo_t = dxo_ref[...]
        dyn = dxo_t * gate
        dvec_ref[0:1, :] += _colsum(dxo_t * (yhat * gpost))
        dvec_ref[1:2, :] += _colsum(dyn * yhat)
        t = dyn * gpost
        dy = (rs * (t - yhat * jnp.mean(t * yhat, axis=-1, keepdims=True))).astype(BF16)
        for k in range(nk):
            cols = slice(k * kd, (k + 1) * kd)
            dycat_ref[:, cols] = lax.dot_general(dy, w_ref[k], NT_DIMS, preferred_element_type=F32)
            acc[k] += lax.dot_general(ycat_ref[:, cols], dy, TN_DIMS, preferred_element_type=F32)

        @pl.when(i == nt - 1)
        def _():
            dw_ref[...] = acc[...].astype(BF16)

    row = pl.BlockSpec((ts, d), lambda i: (i, 0))
    wide = pl.BlockSpec((ts, nk * kd), lambda i: (i, 0))
    return pl.pallas_call(
        body, name=f"outproj_bwd_l{layer}", grid=(nt,),
        in_specs=[row, row, wide,
                  pl.BlockSpec((nk, kd, d), lambda i: (0, 0, 0)),
                  pl.BlockSpec((8, d), lambda i: (0, 0)), ANY],
        out_specs=[wide,
                   pl.BlockSpec((nk, kd, d), lambda i: (0, 0, 0)),
                   pl.BlockSpec((8, d), lambda i: (0, 0))],
        out_shape=[jax.ShapeDtypeStruct((s, nk * kd), F32),
                   jax.ShapeDtypeStruct((nk, kd, d), BF16),
                   jax.ShapeDtypeStruct((8, d), F32)],
        scratch_shapes=[pltpu.VMEM((nk, kd, d), F32)],
        compiler_params=_params(1),
    )(dxo, y, ycat, w_all, vec, after)


def _halo_index(ts, nt):
    return lambda j: jnp.maximum((nt - 1 - j) * (ts // HALO) - 1, 0)


def _rnn_bwd(proj, hs, dycat, cw, vec, wa, wx, layer, after):
    _, s, d = proj.shape
    nh, hd, _ = wa.shape
    ts = _tile(s, BWD_SCAN_ROWS)
    nt = s // ts
    halo = _halo_index(ts, nt)
    hps = BWD_HEADS_PER_STEP
    wl = hps * hd

    def body(proj_ref, xh_ref, hs_ref, hsh_ref, dy_ref, cw_ref, vec_ref, wa_ref, wx_ref, after_ref,
             dproj_ref, dgates_ref, dvec_ref, xbuf, hbuf, dubuf, carry, dw_acc):
        del after_ref
        j = pl.program_id(1)
        first_tile = j == nt - 1

        @pl.when(j == 0)
        def _():
            dubuf[pl.ds(ts, HALO), :] = jnp.zeros((HALO, wl), F32)
            carry[...] = jnp.zeros_like(carry)
            dw_acc[...] = jnp.zeros_like(dw_acc)
            dvec_ref[...] = jnp.zeros_like(dvec_ref)

        xbuf[0:HALO, :] = jnp.where(first_tile, 0.0, xh_ref[0])
        xbuf[pl.ds(HALO, ts), :] = proj_ref[0]
        hbuf[0:HALO, :] = jnp.where(first_tile, 0.0, hsh_ref[...])
        hbuf[pl.ds(HALO, ts), :] = hs_ref[...]

        for hh in range(hps):
            lanes = slice(hh * hd, (hh + 1) * hd)
            wa, wx = wa_ref[hh], wx_ref[hh]
            hs = hs_ref[:, lanes]
            u = _conv(xbuf, cw_ref, vec_ref, lanes, ts)
            ub, r, ig, sp, log_a = _rnn_gates(u, wa, wx, vec_ref, lanes)
            a = jnp.exp(log_a)
            e2 = jnp.exp(2.0 * log_a)
            one_minus_a2 = _neg_expm1(2.0 * log_a)
            inv_mult = lax.rsqrt(one_minus_a2)
            mult = one_minus_a2 * inv_mult

            g = proj_ref[1, :, lanes]
            sg = _sigmoid(g)
            dyc = dy_ref[:, lanes]
            dproj_ref[1, :, lanes] = (dyc * hs * (sg * (1.0 + g * (1.0 - sg)))).astype(BF16)

            row = lax.broadcasted_iota(jnp.int32, (ts, hd), 0)
            dhs = dyc * (g * sg) + jnp.where(row == ts - 1, carry[0:1, lanes], 0.0)
            dh = _scan_rev(_shift_up(a, 1, 0.0), dhs)
            carry[:, lanes] = (a * dh)[0:8, :]

            h_prev = hbuf[pl.ds(HALO - 1, ts), lanes]
            dlog_a = dh * h_prev * a - dh * (ig * u) * (e2 * inv_mult)
            di = dh * mult * u
            dzr = dlog_a * ((-LRU_C) * sp) * (r * (1.0 - r))
            dzi = di * (ig * (1.0 - ig))
            dvec_ref[3:4, lanes] += _colsum(dlog_a * r) * (LRU_C * _sigmoid(-vec_ref[3:4, lanes]))
            dvec_ref[1:2, lanes] += _colsum(dzr)
            dvec_ref[2:3, lanes] += _colsum(dzi)
            dzr_b, dzi_b = dzr.astype(BF16), dzi.astype(BF16)
            dw_acc[0, hh] += lax.dot_general(ub, dzr_b, TN_DIMS, preferred_element_type=F32)
            dw_acc[1, hh] += lax.dot_general(ub, dzi_b, TN_DIMS, preferred_element_type=F32)
            du = (dh * mult * ig
                  + lax.dot_general(dzr_b, wa, NT_DIMS, preferred_element_type=F32)
                  + lax.dot_general(dzi_b, wx, NT_DIMS, preferred_element_type=F32))
            dvec_ref[0:1, lanes] += _colsum(du)
            for k in range(CONV_WIDTH):
                dvec_ref[4 + k:5 + k, lanes] += _colsum(du * xbuf[pl.ds(HALO - (CONV_WIDTH - 1) + k, ts), lanes])

            dubuf[0:ts, lanes] = du
            dx = cw_ref[CONV_WIDTH - 1:CONV_WIDTH, lanes] * du
            for k in range(CONV_WIDTH - 1):
                dx = dx + cw_ref[k:k + 1, lanes] * dubuf[pl.ds(CONV_WIDTH - 1 - k, ts), lanes]
            dproj_ref[0, :, lanes] = dx.astype(BF16)
        dubuf[pl.ds(ts, HALO), :] = dubuf[0:HALO, :]

        @pl.when(first_tile)
        def _():
            dgates_ref[...] = dw_acc[...].astype(BF16)

    rev = lambda h, j: (nt - 1 - j, h)
    return pl.pallas_call(
        body, name=f"rnn_bwd_l{layer}", grid=(nh // hps, nt),
        in_specs=[pl.BlockSpec((2, ts, wl), lambda h, j: (0, nt - 1 - j, h)),
                  pl.BlockSpec((1, HALO, wl), lambda h, j: (0, halo(j), h)),
                  pl.BlockSpec((ts, wl), rev),
                  pl.BlockSpec((HALO, wl), lambda h, j: (halo(j), h)),
                  pl.BlockSpec((ts, wl), rev),
                  pl.BlockSpec((CONV_WIDTH, wl), lambda h, j: (0, h)),
                  pl.BlockSpec((8, wl), lambda h, j: (0, h)),
                  pl.BlockSpec((hps, hd, hd), lambda h, j: (h, 0, 0)),
                  pl.BlockSpec((hps, hd, hd), lambda h, j: (h, 0, 0)), ANY],
        out_specs=[pl.BlockSpec((2, ts, wl), lambda h, j: (0, nt - 1 - j, h)),
                   pl.BlockSpec((2, hps, hd, hd), lambda h, j: (0, h, 0, 0)),
                   pl.BlockSpec((16, wl), lambda h, j: (0, h))],
        out_shape=[jax.ShapeDtypeStruct((4, s, d), BF16),
                   jax.ShapeDtypeStruct((2, nh, hd, hd), BF16),
                   jax.ShapeDtypeStruct((16, d), F32)],
        scratch_shapes=[pltpu.VMEM((ts + HALO, wl), F32), pltpu.VMEM((ts + HALO, wl), F32),
                        pltpu.VMEM((ts + HALO, wl), F32), pltpu.VMEM((8, wl), F32),
                        pltpu.VMEM((2, hps, hd, hd), F32)],
        compiler_params=_params(2),
    )(proj, proj, hs, hs, dycat, cw, vec, wa, wx, after)


def _pool_bwd(proj, dycat, dproj, pw, vec, layer):
    _, s, d = proj.shape
    ng, gd, _ = pw.shape
    ts = _tile(s, MATMUL_ROWS)
    nt = s // ts
    halo = _halo_index(ts, nt)

    def body(proj_ref, xh_ref, dy_ref, dproj_in, pw_ref, vec_ref, dproj_ref, dpw_ref, dvec_ref, xbuf, qbuf, acc):
        del dproj_in
        j = pl.program_id(0)
        i = nt - 1 - j

        @pl.when(j == 0)
        def _():
            qbuf[pl.ds(ts, HALO), :] = jnp.zeros((HALO, d), F32)
            acc[...] = jnp.zeros_like(acc)
            dvec_ref[...] = jnp.zeros_like(dvec_ref)

        xbuf[0:HALO, :] = jnp.where(i == 0, 0.0, xh_ref[0])
        xbuf[pl.ds(HALO, ts), :] = proj_ref[0]
        for g in range(ng):
            lanes = slice(g * gd, (g + 1) * gd)
            win = 2 << g
            xt = proj_ref[0, :, lanes]
            inv_cnt = _inv_count(i, ts, gd, win)
            pooled = _pooled(xbuf, xt, lanes, win, inv_cnt, ts).astype(BF16)
            z = jnp.dot(pooled, pw_ref[g], preferred_element_type=F32) + vec_ref[0:1, lanes]
            scale = vec_ref[1:2, lanes]
            gg = proj_ref[1, :, lanes]
            sg = _sigmoid(gg)
            dyc = dy_ref[:, lanes]
            dyp = dyc * (gg * sg)
            dproj_ref[1, :, lanes] = (dyc * (z * scale) * (sg * (1.0 + gg * (1.0 - sg)))).astype(BF16)
            dvec_ref[1:2, lanes] += _colsum(dyp * z)
            dz = dyp * scale
            dvec_ref[0:1, lanes] += _colsum(dz)
            dz_b = dz.astype(BF16)
            acc[g] += lax.dot_general(pooled, dz_b, TN_DIMS, preferred_element_type=F32)
            dpooled = lax.dot_general(dz_b, pw_ref[g], NT_DIMS, preferred_element_type=F32)

            qbuf[0:ts, lanes] = dpooled * inv_cnt
            dx = _window_sum(qbuf[:, lanes], win, False)[0:ts, :] - dpooled
            dproj_ref[0, :, lanes] = dx.astype(BF16)
        qbuf[pl.ds(ts, HALO), :] = qbuf[0:HALO, :]

        @pl.when(j == nt - 1)
        def _():
            dpw_ref[...] = acc[...].astype(BF16)

    return pl.pallas_call(
        body, name=f"pool_bwd_l{layer}", grid=(nt,),
        in_specs=[pl.BlockSpec((2, ts, d), lambda j: (1, nt - 1 - j, 0)),
                  pl.BlockSpec((1, HALO, d), lambda j: (2, halo(j), 0)),
                  pl.BlockSpec((ts, d), lambda j: (nt - 1 - j, 1)),
                  ANY,
                  pl.BlockSpec((ng, gd, gd), lambda j: (0, 0, 0)),
                  pl.BlockSpec((8, d), lambda j: (0, 0))],
        out_specs=[pl.BlockSpec((2, ts, d), lambda j: (1, nt - 1 - j, 0)),
                   pl.BlockSpec((ng, gd, gd), lambda j: (0, 0, 0)),
                   pl.BlockSpec((8, d), lambda j: (0, 0))],
        out_shape=[jax.ShapeDtypeStruct((4, s, d), BF16),
                   jax.ShapeDtypeStruct((ng, gd, gd), BF16),
                   jax.ShapeDtypeStruct((8, d), F32)],
        input_output_aliases={3: 0},
        scratch_shapes=[pltpu.VMEM((ts + HALO, d), F32), pltpu.VMEM((ts + HALO, d), F32),
                        pltpu.VMEM((ng, gd, gd), F32)],
        compiler_params=_params(1),
    )(proj, proj, dycat, dproj, pw, vec)


def _inproj_bwd_x(dproj, w_all, x, dxo, vec, layer, after):
    s, d = x.shape
    p = w_all.shape[2]
    ts = _tile(s, MATMUL_ROWS)

    def body(dp_ref, w_ref, x_ref, dxo_ref, vec_ref, after_ref, dx_ref, dvec_ref):
        del after_ref

        @pl.when(pl.program_id(0) == 0)
        def _():
            dvec_ref[...] = jnp.zeros_like(dvec_ref)

        dh = lax.dot_general(dp_ref[0], w_ref[0], NT_DIMS, preferred_element_type=F32)
        for k in range(1, 4):
            dh = dh + lax.dot_general(dp_ref[k], w_ref[k], NT_DIMS, preferred_element_type=F32)
        _, xn, rs = _prenorm(x_ref[...], vec_ref)
        gpre, scale1 = vec_ref[3:4, :], 1.0 + vec_ref[1:2, :]
        dvec_ref[0:1, :] += _colsum(dh)
        dvec_ref[1:2, :] += _colsum(dh * (xn * gpre))
        dvec_ref[2:3, :] += _colsum(dh * (xn * scale1))
        t = dh * (gpre * scale1)
        dx_ref[...] = dxo_ref[...] + rs * (t - xn * jnp.mean(t * xn, axis=-1, keepdims=True))

    row = pl.BlockSpec((ts, d), lambda i: (i, 0))
    return pl.pallas_call(
        body, name=f"inproj_bwd_x_l{layer}", grid=(s // ts,),
        in_specs=[pl.BlockSpec((4, ts, p), lambda i: (0, i, 0)),
                  pl.BlockSpec((4, d, p), lambda i: (0, 0, 0)),
                  row, row, pl.BlockSpec((8, d), lambda i: (0, 0)), ANY],
        out_specs=[row, pl.BlockSpec((8, d), lambda i: (0, 0))],
        out_shape=[jax.ShapeDtypeStruct((s, d), F32), jax.ShapeDtypeStruct((8, d), F32)],
        compiler_params=_params(1),
    )(dproj, w_all, x, dxo, vec, after)


def _inproj_bwd_w(dproj, x, vec, layer, after):
    s, d = x.shape
    p = dproj.shape[2]
    ts = _tile(s, MATMUL_ROWS)
    nt = s // ts

    def body(dp_ref, x_ref, vec_ref, after_ref, dw_ref, acc):
        del after_ref
        i = pl.program_id(0)

        @pl.when(i == 0)
        def _():
            acc[...] = jnp.zeros_like(acc)

        h, _, _ = _prenorm(x_ref[...], vec_ref)
        hb = h.astype(BF16)
        for k in range(4):
            acc[k] += lax.dot_general(hb, dp_ref[k], TN_DIMS, preferred_element_type=F32)

        @pl.when(i == nt - 1)
        def _():
            dw_ref[...] = acc[...].astype(BF16)

    return pl.pallas_call(
        body, name=f"inproj_bwd_w_l{layer}", grid=(nt,),
        in_specs=[pl.BlockSpec((4, ts, p), lambda i: (0, i, 0)),
                  pl.BlockSpec((ts, d), lambda i: (i, 0)),
                  pl.BlockSpec((8, d), lambda i: (0, 0)), ANY],
        out_specs=pl.BlockSpec((4, d, p), lambda i: (0, 0, 0)),
        out_shape=jax.ShapeDtypeStruct((4, d, p), BF16),
        scratch_shapes=[pltpu.VMEM((4, d, p), F32)],
        compiler_params=_params(1),
    )(dproj, x, vec, after)


def _sum_slots(stacked, name, out_dtype=F32):
    n, rows, cols = stacked.shape
    tr = _row_tile(rows)

    def body(in_ref, out_ref):
        total = in_ref[0].astype(F32)
        for b in range(1, n):
            total = total + in_ref[b].astype(F32)
        out_ref[...] = total.astype(out_dtype)

    return pl.pallas_call(
        body, name=name, grid=(rows // tr,),
        in_specs=[pl.BlockSpec((n, tr, cols), lambda i: (0, i, 0))],
        out_specs=pl.BlockSpec((tr, cols), lambda i: (i, 0)),
        out_shape=jax.ShapeDtypeStruct((rows, cols), out_dtype),
        compiler_params=_params(1),
    )(stacked)


def _adam_update(w, m, v, g):
    m_new = ADAM_B1 * m + (1.0 - ADAM_B1) * g
    v_new = ADAM_B2 * v + (1.0 - ADAM_B2) * (g * g)
    m_hat = m_new / (1.0 - ADAM_B1 ** ADAM_STEP)
    v_hat = v_new / (1.0 - ADAM_B2 ** ADAM_STEP)
    return (-ADAM_LR) * (m_hat / (jnp.sqrt(v_hat) + ADAM_EPS) + ADAM_WD * w), m_new, v_new


def _adamw_layer(w, m, v, grads, layer, prev, name, grad_row_offset=0):
    nl = w.shape[0]
    cols = w.shape[-1]
    rows = w.size // (nl * cols)
    tr = _row_tile(rows)
    off = layer * (rows // tr)
    g_off = grad_row_offset // tr
    n = len(grads)
    n_prev = 0 if prev is None else 4

    def body(*refs):
        w_ref, m_ref, v_ref = refs[:3]
        g_refs = refs[3:3 + n]
        g_out, d_out, m_out, v_out = refs[3 + n + n_prev:]
        g = g_refs[0][...].astype(F32)
        for r in g_refs[1:]:
            g = g + r[...].astype(F32)
        g_out[...] = g
        d_out[...], m_out[...], v_out[...] = _adam_update(w_ref[...], m_ref[...], v_ref[...], g)

    mine = pl.BlockSpec((tr, cols), lambda i: (off + i, 0))
    args = [a.reshape(nl * rows, cols) for a in (w, m, v)] + [g.reshape(-1, cols) for g in grads]
    outs = pl.pallas_call(
        body, name=name, grid=(rows // tr,),
        in_specs=[mine] * 3 + [pl.BlockSpec((tr, cols), lambda i: (g_off + i, 0))] * n + [ANY] * n_prev,
        out_specs=[mine] * 4,
        out_shape=[jax.ShapeDtypeStruct((nl * rows, cols), F32)] * 4,
        input_output_aliases={3 + n + k: k for k in range(n_prev)},
        compiler_params=_params(1),
    )(*args, *(prev or ()))
    return tuple(outs)


def _into_slot(a, dtype, chip_arr, name, layer=None, after=None):
    rows, cols = a.shape[-2:]
    tr = _row_tile(rows)

    def body(chip_ref, a_ref, *rest):
        del chip_ref
        rest[-1][...] = a_ref[...].astype(dtype)

    if layer is None:
        in_spec = pl.BlockSpec((tr, cols), lambda i, chip: (i, 0))
    else:
        in_spec = pl.BlockSpec((None, tr, cols), lambda i, chip: (layer, i, 0))
    extra = [] if after is None else [after]
    return pl.pallas_call(
        body, name=name,
        grid_spec=pltpu.PrefetchScalarGridSpec(
            num_scalar_prefetch=1, grid=(rows // tr,),
            in_specs=[in_spec] + [ANY] * len(extra),
            out_specs=pl.BlockSpec((None, tr, cols), lambda i, chip: (chip[0], i, 0))),
        out_shape=jax.ShapeDtypeStruct((4, rows, cols), dtype),
        compiler_params=_params(1),
    )(chip_arr, a, *extra)


def _sum_owner(own, land, chip_arr, own_block, own_index, name):
    blk = land.shape[1:]
    tr = _row_tile(blk[-2])
    steps = blk[-2] // tr
    tile = (*blk[:-2], tr, blk[-1])
    lead = (0,) * (len(blk) - 2)

    def body(chip_ref, own_ref, l1, l2, l3, out_ref):
        del chip_ref
        total = (own_ref[...].astype(F32) + l1[...].astype(F32)) + (l2[...].astype(F32) + l3[...].astype(F32))
        out_ref[...] = total.astype(BF16)

    def landed(k):
        return pl.BlockSpec((None, *tile), lambda i, chip: (chip[0] ^ k, *lead, i, 0))

    return pl.pallas_call(
        body, name=name,
        grid_spec=pltpu.PrefetchScalarGridSpec(
            num_scalar_prefetch=1, grid=(steps,),
            in_specs=[pl.BlockSpec(own_block(tr), own_index), landed(1), landed(2), landed(3)],
            out_specs=pl.BlockSpec(tile, lambda i, chip: (*lead, i, 0))),
        out_shape=jax.ShapeDtypeStruct(blk, BF16),
        compiler_params=_params(1),
    )(chip_arr, own, land, land, land)


_WHOLE_VMEM = pltpu.CompilerParams(vmem_limit_bytes=V7X_VMEM_LIMIT_BYTES)


def _pack_vectors(modbuf, ada_b, pre_norm_g, post_norm_g, conv_b, gate_a_b, gate_x_b, lru_lambda):
    nl, d = pre_norm_g.shape
    n = modbuf.shape[2] // nl
    nh, hd = gate_a_b.shape[1], gate_a_b.shape[2]

    def body(mb_ref, ab_ref, pre_ref, post_ref, cb_ref, gab_ref, gxb_ref, lam_ref, *outs):
        for layer in range(nl):
            vec_ref, rvec_ref = outs[layer], outs[nl + layer]
            vec_ref[...] = jnp.zeros_like(vec_ref)
            rvec_ref[...] = jnp.zeros_like(rvec_ref)
            for k in range(4):
                piece = mb_ref[k, 0:1, layer * n:(layer + 1) * n] + ab_ref[layer:layer + 1, k * n:(k + 1) * n]
                lo = k * n
                while lo < (k + 1) * n:
                    row = lo // d
                    hi = min((row + 1) * d, (k + 1) * n)
                    vec_ref[row:row + 1, lo - row * d:hi - row * d] = piece[:, lo - k * n:hi - k * n]
                    lo = hi
            vec_ref[3:4, :] = pre_ref[layer:layer + 1, :]
            vec_ref[4:5, :] = post_ref[layer:layer + 1, :]
            rvec_ref[0:1, :] = cb_ref[layer:layer + 1, :]
            for h in range(nh):
                rvec_ref[1:2, h * hd:(h + 1) * hd] = gab_ref[layer, h:h + 1, :]
                rvec_ref[2:3, h * hd:(h + 1) * hd] = gxb_ref[layer, h:h + 1, :]
            rvec_ref[3:4, :] = lam_ref[layer:layer + 1, :]

    out = pl.pallas_call(
        body, name="pack_vectors", in_specs=[VMEM] * 8, out_specs=[VMEM] * (2 * nl),
        out_shape=[jax.ShapeDtypeStruct((8, d), F32)] * (2 * nl), compiler_params=_WHOLE_VMEM,
    )(modbuf, ada_b, pre_norm_g, post_norm_g, conv_b, gate_a_b, gate_x_b, lru_lambda)
    return list(out[:nl]), list(out[nl:])


def _pack_gathered(convw_g, poolb_g, pws, pool_scale, ng):
    nl, d = pool_scale.shape
    taps = convw_g.shape[1] // nl
    dq = convw_g.shape[2]
    gq, gd = poolb_g.shape[2], pws[0].shape[2]

    def body(cg_ref, pb_ref, *rest):
        pw_refs, ps_ref = rest[:nl], rest[nl]
        outs = rest[nl + 1:]
        for layer in range(nl):
            cw_ref, pvec_ref, pwf_ref = outs[layer], outs[nl + layer], outs[2 * nl + layer]
            pvec_ref[...] = jnp.zeros_like(pvec_ref)
            pvec_ref[1:2, :] = ps_ref[layer:layer + 1, :]
            for k in range(4):
                cw_ref[:, k * dq:(k + 1) * dq] = cg_ref[k, layer * taps:(layer + 1) * taps, :]
                for g in range(ng):
                    lo = g * gd + k * gq
                    pvec_ref[0:1, lo:lo + gq] = pb_ref[k, layer * ng + g:layer * ng + g + 1, :]
                    pwf_ref[g, k * gq:(k + 1) * gq, :] = pw_refs[layer][k, g * gq:(g + 1) * gq, :]

    out = pl.pallas_call(
        body, name="pack_gathered", in_specs=[VMEM] * (3 + nl), out_specs=[VMEM] * (3 * nl),
        out_shape=[jax.ShapeDtypeStruct((taps, d), F32)] * nl + [jax.ShapeDtypeStruct((8, d), F32)] * nl
        + [jax.ShapeDtypeStruct((ng, gd, gd), BF16)] * nl,
        compiler_params=_WHOLE_VMEM,
    )(convw_g, poolb_g, *pws, pool_scale)
    return list(out[:nl]), list(out[nl:2 * nl]), list(out[2 * nl:])


ROW_SHIFT, ROW_SCALE, ROW_PRE, ROW_GATE, ROW_POST = 0, 1, 2, 8, 9
ROW_CONV_B, ROW_GATE_A_B, ROW_GATE_X_B, ROW_LAMBDA, ROW_CONV_W = 16, 17, 18, 19, 20
ROW_POOL_B, ROW_POOL_SCALE, ROW_SQ = 32, 33, 40


def _adamw_small(totals, chip_arr, params):
    nl = len(totals)
    d = totals[0].shape[1]
    n_par = len(params)
    flat = [a for p in params for a in p]
    nh, hd = params[6][0].shape[1], params[6][0].shape[2]
    taps, dq = params[8][0].shape[1], params[8][0].shape[2]
    ng, gq = params[9][0].shape[1], params[9][0].shape[2]
    gd = d // ng

    def body(chip_ref, *refs):
        tot = refs[:nl]
        ins = refs[nl:nl + 3 * n_par]
        outs = refs[nl + 3 * n_par:]
        chip = chip_ref[0]

        def update(p, idx, g):
            delta, m_new, v_new = _adam_update(ins[3 * p][idx], ins[3 * p + 1][idx], ins[3 * p + 2][idx], g)
            outs[4 * p][idx] = g
            outs[4 * p + 1][idx] = delta
            outs[4 * p + 2][idx] = m_new
            outs[4 * p + 3][idx] = v_new

        def mine(candidates):
            g = candidates[0]
            for k in range(1, 4):
                g = jnp.where(chip == k, candidates[k], g)
            return g

        for layer in range(nl):
            t = tot[layer]
            row = (slice(layer, layer + 1), slice(None))
            for j, r in enumerate((ROW_SHIFT, ROW_SCALE, ROW_GATE)):
                update(0, (slice(layer, layer + 1), slice(j * d, (j + 1) * d)), t[r:r + 1, :])
            for p, r in ((1, ROW_PRE), (2, ROW_POST), (3, ROW_CONV_B), (4, ROW_LAMBDA), (5, ROW_POOL_SCALE)):
                update(p, row, t[r:r + 1, :])
            for h in range(nh):
                idx = (layer, slice(h, h + 1), slice(None))
                update(6, idx, t[ROW_GATE_A_B:ROW_GATE_A_B + 1, h * hd:(h + 1) * hd])
                update(7, idx, t[ROW_GATE_X_B:ROW_GATE_X_B + 1, h * hd:(h + 1) * hd])
            for k in range(taps):
                r = ROW_CONV_W + k
                update(8, (layer, slice(k, k + 1), slice(None)), mine([t[r:r + 1, c * dq:(c + 1) * dq] for c in range(4)]))
            for g in range(ng):
                cands = [t[ROW_POOL_B:ROW_POOL_B + 1, g * gd + c * gq:g * gd + (c + 1) * gq] for c in range(4)]
                update(9, (layer, slice(g, g + 1), slice(None)), mine(cands))

    out = pl.pallas_call(
        body, name="adamw_small",
        in_specs=[pl.BlockSpec(memory_space=pltpu.SMEM)] + [VMEM] * (nl + 3 * n_par),
        out_specs=[VMEM] * (4 * n_par),
        out_shape=[jax.ShapeDtypeStruct(p[0].shape, F32) for p in params for _ in range(4)],
        compiler_params=_WHOLE_VMEM,
    )(chip_arr, *totals, *flat)
    return [tuple(out[4 * p:4 * p + 4]) for p in range(n_par)]


def _adamw_ada_w_layer(c_t, slabs, chip_arr, w, m, v, layer, prev, name, after):
    nl, d, n = w.shape
    nb = c_t.shape[1]
    tr = _row_tile(d)
    off = layer * (d // tr)
    n_prev = 0 if prev is None else 4
    mod_rows = (ROW_SHIFT, ROW_SCALE, ROW_GATE)

    def body(chip_ref, c_ref, slab_ref, w_ref, m_ref, v_ref, *rest):
        g_out, d_out, m_out, v_out = rest[n_prev + 1:n_prev + 5]
        dm = rest[-1]

        @pl.when(pl.program_id(0) == 0)
        def _():
            for k in range(4):
                @pl.when(chip_ref[0] == k)
                def _():
                    lo = k * n
                    while lo < (k + 1) * n:
                        hi = min((lo // d + 1) * d, (k + 1) * n)
                        row = mod_rows[lo // d]
                        for b in range(nb):
                            dm[b:b + 1, lo - k * n:hi - k * n] = slab_ref[b, row:row + 1, lo % d:lo % d + hi - lo]
                        lo = hi

        g = c_ref[:, 0:1] * dm[0:1, :]
        for b in range(1, nb):
            g = g + c_ref[:, b:b + 1] * dm[b:b + 1, :]
        g_out[...] = g
        d_out[...], m_out[...], v_out[...] = _adam_update(w_ref[...], m_ref[...], v_ref[...], g)

    mine = pl.BlockSpec((tr, n), lambda i, chip: (off + i, 0))
    outs = pl.pallas_call(
        body, name=name,
        grid_spec=pltpu.PrefetchScalarGridSpec(
            num_scalar_prefetch=1, grid=(d // tr,),
            in_specs=[pl.BlockSpec((tr, nb), lambda i, chip: (i, 0)),
                      pl.BlockSpec(slabs.shape, lambda i, chip: (0, 0, 0))] + [mine] * 3 + [ANY] * (n_prev + 1),
            out_specs=[mine] * 4,
            scratch_shapes=[pltpu.VMEM((nb, n), F32)]),
        out_shape=[jax.ShapeDtypeStruct((nl * d, n), F32)] * 4,
        input_output_aliases={6 + k: k for k in range(n_prev)},
        compiler_params=_params(1),
    )(chip_arr, c_t, slabs, *[a.reshape(nl * d, n) for a in (w, m, v)], *(prev or ()), after)
    return tuple(outs)


def _place():
    x, y, c = lax.axis_index("x"), lax.axis_index("y"), lax.axis_index("c")
    return x, y, c


OTHER_CHIPS = ((1, 0), (0, 1), (1, 1))
OTHER_DEVICES = tuple((fx, fy, fc) for fx in (0, 1) for fy in (0, 1) for fc in (0, 1))[1:]


def _mod_exchange(c_row, ada_w, after):
    nl, d, n = ada_w.shape

    def body(c_ref, w_ref, after_ref, cbuf, modbuf, token, cblk, mres, send_a, recv_a, send_c, recv_c):
        del after_ref
        token[...] = jnp.zeros_like(token)
        x, y, c = _place()
        me = 4 * x + 2 * y + c
        chip = 2 * x + y
        cv = c_ref[...]
        cblk[...] = jnp.zeros_like(cblk)
        cblk[0:1, :] = cv * _sigmoid(cv)

        def rows_of(dev):
            return cbuf.at[pl.ds(pl.multiple_of(8 * dev, 8), 8), :]

        cbuf[pl.ds(pl.multiple_of(8 * me, 8), 8), :] = cblk[...]
        sends = []
        for j, (fx, fy, fc) in enumerate(OTHER_DEVICES):
            cp = pltpu.make_async_remote_copy(
                src_ref=cblk, dst_ref=rows_of(me), send_sem=send_a.at[j], recv_sem=recv_a.at[j],
                device_id=(x ^ fx, y ^ fy, c ^ fc), device_id_type=MESH)
            cp.start()
            sends.append(cp)
        for j, (fx, fy, fc) in enumerate(OTHER_DEVICES):
            peer = 4 * (x ^ fx) + 2 * (y ^ fy) + (c ^ fc)
            pltpu.make_async_remote_copy(
                src_ref=cblk, dst_ref=rows_of(peer), send_sem=send_a.at[j], recv_sem=recv_a.at[j],
                device_id=(x ^ fx, y ^ fy, c ^ fc), device_id_type=MESH).wait_recv()
        for cp in sends:
            cp.wait_send()

        call = cbuf[...]
        for layer in range(nl):
            mres[:, layer * n:(layer + 1) * n] = jnp.dot(
                call, w_ref[layer], preferred_element_type=F32, precision=lax.Precision.HIGHEST)

        def block_of(dev):
            return mres.at[pl.ds(pl.multiple_of(8 * dev, 8), 8), :]

        modbuf[chip] = mres[pl.ds(pl.multiple_of(8 * me, 8), 8), :]
        sends = []
        for j, (fx, fy) in enumerate(OTHER_CHIPS):
            peer = 4 * (x ^ fx) + 2 * (y ^ fy) + c
            cp = pltpu.make_async_remote_copy(
                src_ref=block_of(peer), dst_ref=modbuf.at[chip], send_sem=send_c.at[j], recv_sem=recv_c.at[j],
                device_id=(x ^ fx, y ^ fy, c), device_id_type=MESH)
            cp.start()
            sends.append(cp)
        for j, (fx, fy) in enumerate(OTHER_CHIPS):
            pltpu.make_async_remote_copy(
                src_ref=block_of(me), dst_ref=modbuf.at[2 * (x ^ fx) + (y ^ fy)],
                send_sem=send_c.at[j], recv_sem=recv_c.at[j],
                device_id=(x ^ fx, y ^ fy, c), device_id_type=MESH).wait_recv()
        for cp in sends:
            cp.wait_send()

    return pl.pallas_call(
        body, name="mod_exchange", in_specs=[VMEM, VMEM, ANY], out_specs=[VMEM, VMEM, VMEM],
        out_shape=[jax.ShapeDtypeStruct((64, d), F32), jax.ShapeDtypeStruct((4, 8, nl * n), F32),
                   jax.ShapeDtypeStruct((8, 128), F32)],
        scratch_shapes=[pltpu.VMEM((8, d), F32), pltpu.VMEM((64, nl * n), F32),
                        pltpu.SemaphoreType.DMA((7,)), pltpu.SemaphoreType.DMA((7,)),
                        pltpu.SemaphoreType.DMA((3,)), pltpu.SemaphoreType.DMA((3,))],
        compiler_params=pltpu.CompilerParams(vmem_limit_bytes=V7X_VMEM_LIMIT_BYTES, has_side_effects=True),
    )(c_row, ada_w, after)


def _in_hbm(a):
    return pltpu.with_memory_space_constraint(a, pltpu.HBM)


def _gather_copies(lands, split, over_ici):
    x, y, c = _place()
    chip = 2 * x + y
    out = []
    for t, land in enumerate(lands):
        half = land.shape[1] // 2
        mine = pl.ds(pl.multiple_of(c * half, half), half)
        theirs = pl.ds(pl.multiple_of((1 - c) * half, half), half)
        for j, (fx, fy) in enumerate(OTHER_CHIPS):
            them = 2 * (x ^ fx) + (y ^ fy)
            if over_ici and split[t]:
                out.append((land.at[chip, mine], land.at[chip, mine], land.at[them, mine], (x ^ fx, y ^ fy, c), 3 * t + j))
            elif over_ici:
                out.append((land.at[chip], land.at[chip], land.at[them], (x ^ fx, y ^ fy, c), 3 * t + j))
            elif split[t]:
                out.append((land.at[them, mine], land.at[them, mine], land.at[them, theirs], (x, y, 1 - c), 3 * t + j))
    return out


def _gather_start(lands, groups, split, name):
    n, ngr = len(lands), len(groups)

    def body(*refs):
        sems = refs[n:n + 2 * ngr]
        for gi, idxs in enumerate(groups):
            for src, dst, _, peer, k in _gather_copies([refs[i] for i in idxs], [split[i] for i in idxs], True):
                pltpu.make_async_remote_copy(src_ref=src, dst_ref=dst, send_sem=sems[2 * gi].at[k],
                                             recv_sem=sems[2 * gi + 1].at[k], device_id=peer, device_id_type=MESH).start()
        refs[-1][...] = jnp.zeros_like(refs[-1])

    sem_shapes = []
    for idxs in groups:
        sem_shapes += [pltpu.SemaphoreType.DMA((3 * len(idxs),))] * 2
    out = pl.pallas_call(
        body, name=name,
        in_specs=[HBM] * n, out_specs=[SEM] * (2 * ngr) + [HBM] * n + [VMEM],
        out_shape=sem_shapes + [pltpu.HBM(a.shape, a.dtype) for a in lands] + [jax.ShapeDtypeStruct((8, 128), F32)],
        input_output_aliases={i: 2 * ngr + i for i in range(n)},
        compiler_params=pltpu.CompilerParams(has_side_effects=DATAFLOW_EFFECT),
    )(*[_in_hbm(a) for a in lands])
    sems = [(out[2 * gi], out[2 * gi + 1]) for gi in range(ngr)]
    return sems, list(out[2 * ngr:2 * ngr + n]), out[-1]


def _gather_forward(lands, split, sems, after, name):
    n = len(lands)

    def body(*refs):
        ici_send, ici_recv = refs[n], refs[n + 1]
        fwd_send, fwd_recv = refs[n + 3], refs[n + 4]
        forwards = {k: (src, dst, peer) for src, dst, _, peer, k in _gather_copies(refs[:n], split, False)}
        for src, _, landed, peer, k in _gather_copies(refs[:n], split, True):
            cp = pltpu.make_async_remote_copy(src_ref=src, dst_ref=landed, send_sem=ici_send.at[k], recv_sem=ici_recv.at[k],
                                              device_id=peer, device_id_type=MESH)
            cp.wait_recv()
            if k in forwards:
                fsrc, fdst, fpeer = forwards[k]
                pltpu.make_async_remote_copy(src_ref=fsrc, dst_ref=fdst, send_sem=fwd_send.at[k], recv_sem=fwd_recv.at[k],
                                             device_id=fpeer, device_id_type=MESH).start()
            cp.wait_send()

    out = pl.pallas_call(
        body, name=name,
        in_specs=[HBM] * n + [SEM, SEM, ANY], out_specs=[SEM, SEM] + [HBM] * n,
        out_shape=[pltpu.SemaphoreType.DMA((3 * n,))] * 2 + [pltpu.HBM(a.shape, a.dtype) for a in lands],
        input_output_aliases={i: 2 + i for i in range(n)},
        compiler_params=pltpu.CompilerParams(has_side_effects=DATAFLOW_EFFECT),
    )(*lands, sems[0], sems[1], after)
    return (out[0], out[1]), list(out[2:])


def _gather_wait(lands, split, sems, name):
    n = len(lands)

    def body(*refs):
        send_sems, recv_sems = refs[n], refs[n + 1]
        for src, _, landed, peer, k in _gather_copies(refs[:n], split, False):
            cp = pltpu.make_async_remote_copy(src_ref=src, dst_ref=landed, send_sem=send_sems.at[k], recv_sem=recv_sems.at[k],
                                              device_id=peer, device_id_type=MESH)
            cp.wait_send()
            cp.wait_recv()

    out = pl.pallas_call(
        body, name=name,
        in_specs=[HBM] * n + [SEM, SEM], out_specs=[HBM] * n,
        out_shape=[pltpu.HBM(a.shape, a.dtype) for a in lands],
        input_output_aliases={i: i for i in range(n)},
        compiler_params=pltpu.CompilerParams(has_side_effects=DATAFLOW_EFFECT),
    )(*lands, sems[0], sems[1])
    return list(out)


def _to_owner_copies(pairs, q):
    x, y, c = _place()
    chip = 2 * x + y
    out = []
    for t, (part, land) in enumerate(pairs):
        for j, (fx, fy) in enumerate(OTHER_CHIPS):
            owner = 2 * (x ^ fx) + (y ^ fy)
            if part.shape[0] == 4 and part.shape[1:] == land.shape[1:]:
                src = part.at[owner]
            else:
                src = part.at[:, pl.ds(pl.multiple_of(owner * q, q), q), :]
            out.append((src, land.at[chip], land.at[owner], (x ^ fx, y ^ fy, c), 3 * t + j))
    return out


def _to_all_copies(bufs, first_sem):
    x, y, c = _place()
    me = 4 * x + 2 * y + c
    out = []
    for t, buf in enumerate(bufs):
        for j, (fx, fy, fc) in enumerate(OTHER_DEVICES):
            them = 4 * (x ^ fx) + 2 * (y ^ fy) + (c ^ fc)
            out.append((buf.at[me], buf.at[me], buf.at[them], (x ^ fx, y ^ fy, c ^ fc), first_sem + 7 * t + j))
    return out


def _to_chips_copies(bufs, first_sem):
    x, y, c = _place()
    chip = 2 * x + y
    out = []
    for t, buf in enumerate(bufs):
        for j, (fx, fy) in enumerate(OTHER_CHIPS):
            them = 2 * (x ^ fx) + (y ^ fy)
            out.append((buf.at[chip], buf.at[chip], buf.at[them], (x ^ fx, y ^ fy, c), first_sem + 3 * t + j))
    return out


def _exchange_copies(refs, kinds, q):
    n_owner, n_chips = kinds
    pairs = list(zip(refs[:n_owner], refs[n_owner:2 * n_owner]))
    first_all = 3 * (n_owner + n_chips)
    return (_to_owner_copies(pairs, q) + _to_chips_copies(refs[2 * n_owner:2 * n_owner + n_chips], 3 * n_owner)
            + _to_all_copies(refs[2 * n_owner + n_chips:], first_all))


def _exchange_start(arrays, kinds, q, name):
    n = len(arrays)
    n_sems = 3 * (kinds[0] + kinds[1]) + 7 * (n - 2 * kinds[0] - kinds[1])

    def body(*refs):
        send_sems, recv_sems = refs[n], refs[n + 1]
        for src, dst, _, peer, k in _exchange_copies(refs[:n], kinds, q):
            pltpu.make_async_remote_copy(src_ref=src, dst_ref=dst, send_sem=send_sems.at[k], recv_sem=recv_sems.at[k],
                                         device_id=peer, device_id_type=MESH).start()
        refs[-1][...] = jnp.zeros_like(refs[-1])

    out = pl.pallas_call(
        body, name=name,
        in_specs=[HBM] * n, out_specs=[SEM, SEM] + [HBM] * n + [VMEM],
        out_shape=[pltpu.SemaphoreType.DMA((n_sems,))] * 2 + [pltpu.HBM(a.shape, a.dtype) for a in arrays]
        + [jax.ShapeDtypeStruct((8, 128), F32)],
        input_output_aliases={i: 2 + i for i in range(n)},
        compiler_params=pltpu.CompilerParams(has_side_effects=DATAFLOW_EFFECT),
    )(*[_in_hbm(a) for a in arrays])
    return (out[0], out[1]), list(out[2:2 + n]), out[-1]


def _exchange_wait(arrays, sems, kinds, q, after, name):
    n = len(arrays)

    def body(*refs):
        send_sems, recv_sems = refs[n], refs[n + 1]
        for src, _, landed, peer, k in _exchange_copies(refs[:n], kinds, q):
            cp = pltpu.make_async_remote_copy(src_ref=src, dst_ref=landed, send_sem=send_sems.at[k], recv_sem=recv_sems.at[k],
                                              device_id=peer, device_id_type=MESH)
            cp.wait_send()
            cp.wait_recv()

    out = pl.pallas_call(
        body, name=name,
        in_specs=[HBM] * n + [SEM, SEM, ANY], out_specs=[HBM] * n,
        out_shape=[pltpu.HBM(a.shape, a.dtype) for a in arrays],
        input_output_aliases={i: i for i in range(n)},
        compiler_params=pltpu.CompilerParams(has_side_effects=DATAFLOW_EFFECT),
    )(*arrays, sems[0], sems[1], after)
    return list(out)


def _sibling_copies(refs):
    n = len(refs) // 2
    x, y, c = _place()
    return [(refs[i], refs[n + i], (x, y, 1 - c), i) for i in range(n)]


def _sibling_start(parts, layer):
    arrays = list(parts) + [lax.empty(a.shape, a.dtype) for a in parts]
    n = len(arrays)

    def body(*refs):
        send_sems, recv_sems = refs[n], refs[n + 1]
        for src, dst, peer, k in _sibling_copies(refs[:n]):
            pltpu.make_async_remote_copy(src_ref=src, dst_ref=dst, send_sem=send_sems.at[k], recv_sem=recv_sems.at[k],
                                         device_id=peer, device_id_type=MESH).start()
        refs[-1][...] = jnp.zeros_like(refs[-1])

    out = pl.pallas_call(
        body, name=f"sibling_swap_start_l{layer}",
        in_specs=[HBM] * n, out_specs=[SEM, SEM] + [HBM] * n + [VMEM],
        out_shape=[pltpu.SemaphoreType.DMA((n // 2,))] * 2 + [pltpu.HBM(a.shape, a.dtype) for a in arrays]
        + [jax.ShapeDtypeStruct((8, 128), F32)],
        input_output_aliases={i: 2 + i for i in range(n)},
        compiler_params=pltpu.CompilerParams(has_side_effects=DATAFLOW_EFFECT),
    )(*[_in_hbm(a) for a in arrays])
    return (out[0], out[1]), list(out[2:2 + n]), out[-1]


def _sibling_wait(arrays, sems, after, layer):
    n = len(arrays)

    def body(*refs):
        send_sems, recv_sems = refs[n], refs[n + 1]
        for src, dst, peer, k in _sibling_copies(refs[:n]):
            cp = pltpu.make_async_remote_copy(src_ref=src, dst_ref=dst, send_sem=send_sems.at[k], recv_sem=recv_sems.at[k],
                                              device_id=peer, device_id_type=MESH)
            cp.wait_send()
            cp.wait_recv()

    out = pl.pallas_call(
        body, name=f"sibling_swap_wait_l{layer}",
        in_specs=[HBM] * n + [SEM, SEM, ANY], out_specs=[HBM] * n,
        out_shape=[pltpu.HBM(a.shape, a.dtype) for a in arrays],
        input_output_aliases={i: i for i in range(n)},
        compiler_params=pltpu.CompilerParams(has_side_effects=DATAFLOW_EFFECT),
    )(*arrays, sems[0], sems[1], after)
    return list(out)


def kernel(x, c, ada_w, ada_b, pre_norm_g, w_in, conv_w, conv_b, gate_a_w, gate_a_b, gate_x_w, gate_x_b, lru_lambda, pool_w, pool_b, pool_scale, w_out, post_norm_g, loss_target, m_ada_w, m_ada_b, m_pre_norm_g, m_w_in, m_conv_w, m_conv_b, m_gate_a_w, m_gate_a_b, m_gate_x_w, m_gate_x_b, m_lru_lambda, m_pool_w, m_pool_b, m_pool_scale, m_w_out, m_post_norm_g, v_ada_w, v_ada_b, v_pre_norm_g, v_w_in, v_conv_w, v_conv_b, v_gate_a_w, v_gate_a_b, v_gate_x_w, v_gate_x_b, v_lru_lambda, v_pool_w, v_pool_b, v_pool_scale, v_w_out, v_post_norm_g):
    nl, d, _ = ada_w.shape
    s = x.shape[1]
    nh, hd = gate_a_w.shape[1], gate_a_w.shape[2]
    ng, gq, gd = pool_w.shape[1], pool_w.shape[2], pool_w.shape[3]
    me = 4 * lax.axis_index("x") + 2 * lax.axis_index("y") + lax.axis_index("c")
    chip = 2 * lax.axis_index("x") + lax.axis_index("y")
    chip_arr = jnp.reshape(chip, (1,)).astype(jnp.int32)
    x0 = x.reshape(s, d)
    target = loss_target.reshape(s, d)
    p_in = w_in.shape[2]

    c_row = c.reshape(1, d)
    cbuf, modbuf, mod_token = _mod_exchange(c_row, ada_w, c_row)
    vecs, rvecs = _pack_vectors(modbuf, ada_b, pre_norm_g, post_norm_g, conv_b, gate_a_b, gate_x_b, lru_lambda)

    win = [_into_slot(w_in, BF16, chip_arr, f"slot_w_in_l{l}", l) for l in range(nl)]
    wout = [_into_slot(w_out, BF16, chip_arr, f"slot_w_out_l{l}", l) for l in range(nl)]
    pw = [_into_slot(pool_w.reshape(nl, ng * gq, gd), BF16, chip_arr, f"slot_pool_w_l{l}", l) for l in range(nl)]
    convw = _into_slot(conv_w.reshape(nl * CONV_WIDTH, d // 4), F32, chip_arr, "slot_conv_w", after=mod_token)
    poolb = _into_slot(pool_b.reshape(nl * ng, gq), F32, chip_arr, "slot_pool_b")
    lands = [win[0], convw, poolb, *pw, wout[0]]
    split = [True, False, False] + [True] * (nl + 1)
    groups = [[0], list(range(1, len(lands)))]
    for l in range(1, nl):
        groups.append([len(lands), len(lands) + 1])
        lands += [win[l], wout[l]]
        split += [True, True]
    sems, lands, _ = _gather_start(lands, groups, split, "weight_gather_start")
    wa_b, wx_b = gate_a_w.astype(BF16), gate_x_w.astype(BF16)

    def gathered(gi, after, tag):
        idxs = groups[gi]
        arrays, halves = [lands[i] for i in idxs], [split[i] for i in idxs]
        between, arrays = _gather_forward(arrays, halves, sems[gi], after, f"weight_gather_forward_{tag}")
        return _gather_wait(arrays, halves, between, f"weight_gather_wait_{tag}")

    xs, projs, hss, ycats, ys = [x0], [], [], [], []
    sq = None
    convw_full = poolw_full = pvecs = None
    for l in range(nl):
        if l == 0:
            (win[0],) = gathered(0, modbuf, "a")
        proj = _inproj_fwd(xs[l], vecs[l], win[l], l)
        if l == 0:
            got = gathered(1, proj, "b")
            wout[0] = got[2 + nl]
            convw_full, pvecs, poolw_full = _pack_gathered(got[0], got[1], got[2:2 + nl], pool_scale, ng)
        ycat, hs = _rnn_fwd(proj, convw_full[l], rvecs[l], wa_b[l], wx_b[l], l)
        if l + 1 < nl:
            win[l + 1], wout[l + 1] = gathered(2 + l, hs, f"c{l + 1}")
        ycat = _pool_fwd(proj, ycat, poolw_full[l], pvecs[l], l)
        y, xo, sq = _outproj_fwd(ycat, wout[l], xs[l], vecs[l], target if l == nl - 1 else None, l)
        projs.append(proj), hss.append(hs), ycats.append(ycat), ys.append(y), xs.append(xo)

    c_all_t = cbuf.reshape(8, 8, d)[:, 0, :].T

    def finish(l, flights, after, prev):
        (sems_a, arr_a), (sems_b, arr_b), (sems_c, arr_c) = flights
        dwout_l, rwout = _exchange_wait(arr_a, sems_a, (1, 0), gq, after, f"grad_wait_a_l{l}")
        dwin_l, dpw_l, rwin, rpw, gates = _exchange_wait(arr_b, sems_b, (2, 1), gq, rwout, f"grad_wait_b_l{l}")
        (slabs,) = _exchange_wait(arr_c, sems_c, (0, 0), gq, rwin, f"grad_wait_c_l{l}")
        p_win = _sum_owner(dwin_l, rwin, chip_arr, lambda tr: (None, tr, p_in),
                           lambda i, chip: (chip[0], i, 0), f"sum_w_in_l{l}")
        p_wout = _sum_owner(dwout_l, rwout, chip_arr, lambda tr: (None, tr, d),
                            lambda i, chip: (chip[0], i, 0), f"sum_w_out_l{l}")
        p_pw = _sum_owner(dpw_l, rpw, chip_arr, lambda tr: (ng, tr, gd),
                          lambda i, chip: (0, chip[0], 0), f"sum_pool_w_l{l}")
        p_gates = _sum_slots(gates.reshape(4, 2 * nh * hd, hd), f"sum_gates_l{l}", BF16)
        swap_sems, swapping, swap_token = _sibling_start([p_win, p_wout, p_pw, p_gates], l)
        prev = prev or {}
        ada = _adamw_ada_w_layer(c_all_t, slabs, chip_arr, ada_w, m_ada_w, v_ada_w, l, prev.get("ada_w"),
                                 f"adamw_ada_w_l{l}", swap_token)
        p_win, p_wout, p_pw, p_gates, q_win, q_wout, q_pw, q_gates = _sibling_wait(swapping, swap_sems, ada[3], l)
        big = {
            "w_in": _adamw_layer(w_in, m_w_in, v_w_in, [p_win, q_win], l, prev.get("w_in"), f"adamw_w_in_l{l}"),
            "w_out": _adamw_layer(w_out, m_w_out, v_w_out, [p_wout, q_wout], l, prev.get("w_out"), f"adamw_w_out_l{l}"),
            "pool_w": _adamw_layer(pool_w, m_pool_w, v_pool_w, [p_pw, q_pw], l, prev.get("pool_w"), f"adamw_pool_w_l{l}"),
            "gate_a_w": _adamw_layer(gate_a_w, m_gate_a_w, v_gate_a_w, [p_gates, q_gates], l, prev.get("gate_a_w"),
                                     f"adamw_gate_a_w_l{l}"),
            "gate_x_w": _adamw_layer(gate_x_w, m_gate_x_w, v_gate_x_w, [p_gates, q_gates], l, prev.get("gate_x_w"),
                                     f"adamw_gate_x_w_l{l}", grad_row_offset=nh * hd),
            "ada_w": ada,
        }
        return big, _sum_slots(slabs, f"sum_slab_l{l}")

    dx = xs[nl]
    flights = token = big = None
    totals = [None] * nl
    for l in reversed(range(nl)):
        vec_l = vecs[l]
        dycat, dwout_l, dvec_o = _outproj_bwd(dx, ys[l], ycats[l], wout[l], vec_l, l, vec_l if token is None else token)
        sems_a, arr_a, tok_a = _exchange_start([dwout_l, lax.empty(dwout_l.shape, BF16)], (1, 0), gq, f"grad_start_a_l{l}")
        dproj, dgates, dvec_r = _rnn_bwd(projs[l], hss[l], dycat, convw_full[l], rvecs[l], wa_b[l], wx_b[l], l, tok_a)
        dproj, dpw_l, dvec_p = _pool_bwd(projs[l], dycat, dproj, poolw_full[l], pvecs[l], l)
        gates4 = lax.dynamic_update_slice(lax.empty((4, *dgates.shape), BF16), dgates[None], (chip, 0, 0, 0, 0))
        dwin_l = _inproj_bwd_w(dproj, xs[l], vec_l, l, dpw_l)
        sems_b, arr_b, tok_b = _exchange_start(
            [dwin_l, dpw_l, lax.empty(dwin_l.shape, BF16), lax.empty((4, ng, gq, gd), BF16), gates4],
            (2, 1), gq, f"grad_start_b_l{l}")
        dx, dvec_i = _inproj_bwd_x(dproj, win[l], xs[l], dx, vec_l, l, tok_b)
        parts = [dvec_i, dvec_o, dvec_r, dvec_p]
        if l == nl - 1:
            parts.append(jnp.tile(sq, (1, d // sq.shape[1])))
        slab = jnp.concatenate(parts, axis=0)
        slabs = lax.dynamic_update_slice(lax.empty((8, *slab.shape), F32), slab[None], (me, 0, 0))
        sems_c, arr_c, token = _exchange_start([slabs], (0, 0), gq, f"grad_start_c_l{l}")
        if flights is not None:
            big, totals[l + 1] = finish(l + 1, flights, token, big)
        flights = ((sems_a, arr_a), (sems_b, arr_b), (sems_c, arr_c))
    big, totals[0] = finish(0, flights, big["w_in"][3] if big else dx, big)
    grad_x = dx.reshape(x.shape)
    loss = totals[nl - 1][ROW_SQ, 0] * (0.5 / d)

    small = _adamw_small(totals, chip_arr, [
        (ada_b, m_ada_b, v_ada_b), (pre_norm_g, m_pre_norm_g, v_pre_norm_g), (post_norm_g, m_post_norm_g, v_post_norm_g),
        (conv_b, m_conv_b, v_conv_b), (lru_lambda, m_lru_lambda, v_lru_lambda), (pool_scale, m_pool_scale, v_pool_scale),
        (gate_a_b, m_gate_a_b, v_gate_a_b), (gate_x_b, m_gate_x_b, v_gate_x_b),
        (conv_w, m_conv_w, v_conv_w), (pool_b, m_pool_b, v_pool_b)])

    results = {
        "ada_w": tuple(o.reshape(ada_w.shape) for o in big["ada_w"]),
        "ada_b": small[0],
        "pre_norm_g": small[1],
        "w_in": tuple(o.reshape(w_in.shape) for o in big["w_in"]),
        "conv_w": small[8],
        "conv_b": small[3],
        "gate_a_w": tuple(o.reshape(gate_a_w.shape) for o in big["gate_a_w"]),
        "gate_a_b": small[6],
        "gate_x_w": tuple(o.reshape(gate_x_w.shape) for o in big["gate_x_w"]),
        "gate_x_b": small[7],
        "lru_lambda": small[4],
        "pool_w": tuple(o.reshape(pool_w.shape) for o in big["pool_w"]),
        "pool_b": small[9],
        "pool_scale": small[5],
        "w_out": tuple(o.reshape(w_out.shape) for o in big["w_out"]),
        "post_norm_g": small[2],
    }
    names = list(results)
    return (loss, grad_x,
            *[results[n][0] for n in names], *[results[n][1] for n in names],
            *[results[n][2] for n in names], *[results[n][3] for n in names])
```

```python
import jax
import jax.numpy as jnp
from jax import lax
from jax.experimental import pallas as pl
from jax.experimental.pallas import tpu as pltpu

F32 = jnp.float32
BF16 = jnp.bfloat16

NORM_EPS = 1e-6
LRU_C = 8.0
CONV_WIDTH = 4
HALO = 16
ADAM_LR = 0.001
ADAM_B1 = 0.9
ADAM_B2 = 0.999
ADAM_EPS = 1e-08
ADAM_WD = 0.01
ADAM_STEP = 10

V7X_VMEM_LIMIT_BYTES = 56 * 1024 * 1024
MATMUL_ROWS = 512
SCAN_ROWS = 512
BWD_SCAN_ROWS = 1024
ELEMENTWISE_ROWS = 512

MESH = pl.DeviceIdType.MESH
ANY = pl.BlockSpec(memory_space=pl.ANY)
VMEM = pl.BlockSpec(memory_space=pltpu.VMEM)
HBM = pl.BlockSpec(memory_space=pltpu.HBM)
SEM = pl.BlockSpec(memory_space=pltpu.SEMAPHORE)
DATAFLOW_EFFECT = pltpu.SideEffectType.DATAFLOW_SIDE_EFFECTING

NT_DIMS = (((1,), (1,)), ((), ()))
TN_DIMS = (((0,), (0,)), ((), ()))


def _params(n_grid_axes):
    return pltpu.CompilerParams(dimension_semantics=("arbitrary",) * n_grid_axes,
                                vmem_limit_bytes=V7X_VMEM_LIMIT_BYTES)


def _tile(total, want):
    t = min(want, max(total // 2, HALO))
    assert total % t == 0 and t % HALO == 0, (total, t)
    return t


def _row_tile(rows):
    for t in range(min(rows, ELEMENTWISE_ROWS) // 8 * 8, 0, -8):
        if rows % t == 0:
            return t
    return rows


def _sigmoid(z):
    return 1.0 / (1.0 + jnp.exp(-z))


def _softplus(z):
    return jnp.maximum(z, 0.0) + jnp.log(1.0 + jnp.exp(-jnp.abs(z)))


def _neg_expm1(z):
    return -jnp.tanh(0.5 * z) * (jnp.exp(z) + 1.0)


def _colsum(v):
    return jnp.sum(v, axis=0, keepdims=True)


def _prenorm(xt, vec_ref):
    rs = lax.rsqrt(jnp.mean(xt * xt, axis=-1, keepdims=True) + NORM_EPS)
    xn = xt * rs
    h = xn * vec_ref[3:4, :] * (1.0 + vec_ref[1:2, :]) + vec_ref[0:1, :]
    return h, xn, rs


def _shift_down(v, d, fill):
    t = v.shape[0]
    if d % 8 == 0:
        return jnp.concatenate([jnp.full((d, v.shape[1]), fill, v.dtype), v[:t - d]], axis=0)
    row = lax.broadcasted_iota(jnp.int32, v.shape, 0)
    return jnp.where(row >= d, pltpu.roll(v, d, 0), fill)


def _shift_up(v, d, fill):
    t = v.shape[0]
    if d % 8 == 0:
        return jnp.concatenate([v[d:], jnp.full((d, v.shape[1]), fill, v.dtype)], axis=0)
    row = lax.broadcasted_iota(jnp.int32, v.shape, 0)
    return jnp.where(row < t - d, pltpu.roll(v, t - d, 0), fill)


def _scan_fwd(a, v, h_before):
    d = 1
    while d < a.shape[0]:
        v = v + a * _shift_down(v, d, 0.0)
        a = a * _shift_down(a, d, 1.0)
        d *= 2
    return a * h_before + v


def _scan_rev(b, v):
    d = 1
    while d < b.shape[0]:
        v = v + b * _shift_up(v, d, 0.0)
        b = b * _shift_up(b, d, 0.0)
        d *= 2
    return v


def _inproj_fwd(x, vec, w_all, layer):
    s, d = x.shape
    p = w_all.shape[2]
    ts = _tile(s, MATMUL_ROWS)

    def body(x_ref, vec_ref, w_ref, proj_ref):
        h, _, _ = _prenorm(x_ref[...], vec_ref)
        hb = h.astype(BF16)
        for k in range(4):
            proj_ref[k] = jnp.dot(hb, w_ref[k], preferred_element_type=F32)

    return pl.pallas_call(
        body, name=f"inproj_fwd_l{layer}", grid=(s // ts,),
        in_specs=[pl.BlockSpec((ts, d), lambda i: (i, 0)),
                  pl.BlockSpec((8, d), lambda i: (0, 0)),
                  pl.BlockSpec((4, d, p), lambda i: (0, 0, 0))],
        out_specs=pl.BlockSpec((4, ts, p), lambda i: (0, i, 0)),
        out_shape=jax.ShapeDtypeStruct((4, s, p), F32),
        compiler_params=_params(1),
    )(x, vec, w_all)


HEADS_PER_STEP = 2
BWD_HEADS_PER_STEP = 1


def _rnn_gates(u, wa, wx, vec_ref, lanes):
    ub = u.astype(BF16)
    r = _sigmoid(jnp.dot(ub, wa, preferred_element_type=F32) + vec_ref[1:2, lanes])
    ig = _sigmoid(jnp.dot(ub, wx, preferred_element_type=F32) + vec_ref[2:3, lanes])
    sp = _softplus(-vec_ref[3:4, lanes])
    log_a = (-LRU_C) * r * sp
    return ub, r, ig, sp, log_a


def _conv(xbuf, cw_ref, vec_ref, lanes, ts):
    u = vec_ref[0:1, lanes] + cw_ref[CONV_WIDTH - 1:CONV_WIDTH, lanes] * xbuf[pl.ds(HALO, ts), lanes]
    for k in range(CONV_WIDTH - 1):
        u = u + cw_ref[k:k + 1, lanes] * xbuf[pl.ds(HALO - (CONV_WIDTH - 1) + k, ts), lanes]
    return u


def _rnn_fwd(proj, cw, vec, wa, wx, layer):
    _, s, d = proj.shape
    nh, hd, _ = wa.shape
    ts = _tile(s, SCAN_ROWS)
    hps = HEADS_PER_STEP
    wl = hps * hd

    def body(proj_ref, cw_ref, vec_ref, wa_ref, wx_ref, ycat_ref, hs_ref, xbuf, hlast):
        i = pl.program_id(1)

        @pl.when(i == 0)
        def _():
            xbuf[0:HALO, :] = jnp.zeros((HALO, wl), F32)
            hlast[...] = jnp.zeros_like(hlast)

        xbuf[pl.ds(HALO, ts), :] = proj_ref[0]
        for hh in range(hps):
            lanes = slice(hh * hd, (hh + 1) * hd)
            u = _conv(xbuf, cw_ref, vec_ref, lanes, ts)
            _, _, ig, _, log_a = _rnn_gates(u, wa_ref[hh], wx_ref[hh], vec_ref, lanes)
            a = jnp.exp(log_a)
            mult = jnp.sqrt(_neg_expm1(2.0 * log_a))
            hs = _scan_fwd(a, mult * (ig * u), hlast[0:1, lanes])
            hs_ref[:, lanes] = hs
            hlast[0:1, lanes] = hs_ref[ts - 1:ts, lanes]
            g = proj_ref[1, :, lanes]
            ycat_ref[:, lanes] = (hs * (g * _sigmoid(g))).astype(BF16)
        xbuf[0:HALO, :] = xbuf[pl.ds(ts, HALO), :]

    return pl.pallas_call(
        body, name=f"rnn_fwd_l{layer}", grid=(nh // hps, s // ts),
        in_specs=[pl.BlockSpec((2, ts, wl), lambda h, i: (0, i, h)),
                  pl.BlockSpec((CONV_WIDTH, wl), lambda h, i: (0, h)),
                  pl.BlockSpec((8, wl), lambda h, i: (0, h)),
                  pl.BlockSpec((hps, hd, hd), lambda h, i: (h, 0, 0)),
                  pl.BlockSpec((hps, hd, hd), lambda h, i: (h, 0, 0))],
        out_specs=[pl.BlockSpec((ts, wl), lambda h, i: (i, h)),
                   pl.BlockSpec((ts, wl), lambda h, i: (i, h))],
        out_shape=[jax.ShapeDtypeStruct((s, 2 * d), BF16), jax.ShapeDtypeStruct((s, d), F32)],
        scratch_shapes=[pltpu.VMEM((ts + HALO, wl), F32), pltpu.VMEM((8, wl), F32)],
        compiler_params=_params(2),
    )(proj, cw, vec, wa, wx)


def _inv_count(i, ts, lanes, win):
    t = i * ts + lax.broadcasted_iota(jnp.int32, (ts, lanes), 0)
    return 1.0 / jnp.minimum(t + 1, win).astype(F32)


def _window_sum(ext, win, forward):
    rows = ext.shape[0]
    s, d = ext, 1
    while d < win:
        s = s + pltpu.roll(s, d if forward else rows - d, 0)
        d *= 2
    return s


def _pooled(xbuf, xt, lanes, win, inv_cnt, ts):
    acc = _window_sum(xbuf[:, lanes], win, True)[HALO:, :]
    return acc * inv_cnt - xt


def _pool_fwd(proj, ycat, pw, vec, layer):
    _, s, d = proj.shape
    ng, gd, _ = pw.shape
    ts = _tile(s, MATMUL_ROWS)

    def body(proj_ref, ycat_in, pw_ref, vec_ref, ycat_ref, xbuf):
        del ycat_in
        i = pl.program_id(0)

        @pl.when(i == 0)
        def _():
            xbuf[0:HALO, :] = jnp.zeros((HALO, d), F32)

        xbuf[pl.ds(HALO, ts), :] = proj_ref[0]
        for g in range(ng):
            lanes = slice(g * gd, (g + 1) * gd)
            win = 2 << g
            xt = proj_ref[0, :, lanes]
            pooled = _pooled(xbuf, xt, lanes, win, _inv_count(i, ts, gd, win), ts).astype(BF16)
            z = jnp.dot(pooled, pw_ref[g], preferred_element_type=F32) + vec_ref[0:1, lanes]
            gg = proj_ref[1, :, lanes]
            ycat_ref[:, lanes] = (z * vec_ref[1:2, lanes] * (gg * _sigmoid(gg))).astype(BF16)
        xbuf[0:HALO, :] = xbuf[pl.ds(ts, HALO), :]

    return pl.pallas_call(
        body, name=f"pool_fwd_l{layer}", grid=(s // ts,),
        in_specs=[pl.BlockSpec((2, ts, d), lambda i: (1, i, 0)),
                  ANY,
                  pl.BlockSpec((ng, gd, gd), lambda i: (0, 0, 0)),
                  pl.BlockSpec((8, d), lambda i: (0, 0))],
        out_specs=pl.BlockSpec((ts, d), lambda i: (i, 1)),
        out_shape=jax.ShapeDtypeStruct((s, 2 * d), BF16),
        input_output_aliases={1: 0},
        scratch_shapes=[pltpu.VMEM((ts + HALO, d), F32)],
        compiler_params=_params(1),
    )(proj, ycat, pw, vec)


def _outproj_fwd(ycat, w_all, x, vec, target, layer):
    s, d = x.shape
    nk, kd = w_all.shape[0], w_all.shape[1]
    ts = _tile(s, MATMUL_ROWS)
    last = target is not None

    def body(*refs):
        if last:
            ycat_ref, w_ref, x_ref, vec_ref, tgt_ref, y_ref, xo_ref, sq_ref = refs
        else:
            ycat_ref, w_ref, x_ref, vec_ref, y_ref, xo_ref = refs
        y = jnp.dot(ycat_ref[:, 0:kd], w_ref[0], preferred_element_type=F32)
        for k in range(1, nk):
            y = y + jnp.dot(ycat_ref[:, k * kd:(k + 1) * kd], w_ref[k], preferred_element_type=F32)
        y_ref[...] = y.astype(BF16)
        rs = lax.rsqrt(jnp.mean(y * y, axis=-1, keepdims=True) + NORM_EPS)
        xo = x_ref[...] + vec_ref[2:3, :] * (y * rs * vec_ref[4:5, :])
        if last:
            err = xo - tgt_ref[...]
            xo_ref[...] = err * (1.0 / d)

            @pl.when(pl.program_id(0) == 0)
            def _():
                sq_ref[...] = jnp.zeros_like(sq_ref)

            sq_ref[...] += jnp.sum(err * err)
        else:
            xo_ref[...] = xo

    row = pl.BlockSpec((ts, d), lambda i: (i, 0))
    in_specs = [pl.BlockSpec((ts, nk * kd), lambda i: (i, 0)),
                pl.BlockSpec((nk, kd, d), lambda i: (0, 0, 0)),
                row, pl.BlockSpec((8, d), lambda i: (0, 0))]
    out_specs = [row, row]
    out_shape = [jax.ShapeDtypeStruct((s, d), BF16), jax.ShapeDtypeStruct((s, d), F32)]
    args = [ycat, w_all, x, vec]
    if last:
        in_specs.append(row)
        args.append(target)
        out_specs.append(pl.BlockSpec((8, 128), lambda i: (0, 0)))
        out_shape.append(jax.ShapeDtypeStruct((8, 128), F32))
    out = pl.pallas_call(
        body, name=f"outproj_fwd_l{layer}", grid=(s // ts,),
        in_specs=in_specs, out_specs=out_specs, out_shape=out_shape,
        compiler_params=_params(1),
    )(*args)
    return (out[0], out[1], out[2]) if last else (out[0], out[1], None)


def _outproj_bwd(dxo, y, ycat, w_all, vec, layer, after):
    s, d = dxo.shape
    nk, kd = w_all.shape[0], w_all.shape[1]
    ts = _tile(s, MATMUL_ROWS)
    nt = s // ts

    def body(dxo_ref, y_ref, ycat_ref, w_ref, vec_ref, after_ref, dycat_ref, dw_ref, dvec_ref, acc):
        del after_ref
        i = pl.program_id(0)

        @pl.when(i == 0)
        def _():
            acc[...] = jnp.zeros_like(acc)
            dvec_ref[...] = jnp.zeros_like(dvec_ref)

        yt = y_ref[...].astype(F32)
        rs = lax.rsqrt(jnp.mean(yt * yt, axis=-1, keepdims=True) + NORM_EPS)
        yhat = yt * rs
        gate, gpost = vec_ref[2:3, :], vec_ref[4:5, :]
        dxo_t = dxo_ref[...]
        dyn = dxo_t * gate
        dvec_ref[0:1, :] += _colsum(dxo_t * (yhat * gpost))
        dvec_ref[1:2, :] += _colsum(dyn * yhat)
        t = dyn * gpost
        dy = (rs * (t - yhat * jnp.mean(t * yhat, axis=-1, keepdims=True))).astype(BF16)
        for k in range(nk):
            cols = slice(k * kd, (k + 1) * kd)
            dycat_ref[:, cols] = lax.dot_general(dy, w_ref[k], NT_DIMS, preferred_element_type=F32)
            acc[k] += lax.dot_general(ycat_ref[:, cols], dy, TN_DIMS, preferred_element_type=F32)

        @pl.when(i == nt - 1)
        def _():
            dw_ref[...] = acc[...].astype(BF16)

    row = pl.BlockSpec((ts, d), lambda i: (i, 0))
    wide = pl.BlockSpec((ts, nk * kd), lambda i: (i, 0))
    return pl.pallas_call(
        body, name=f"outproj_bwd_l{layer}", grid=(nt,),
        in_specs=[row, row, wide,
                  pl.BlockSpec((nk, kd, d), lambda i: (0, 0, 0)),
                  pl.BlockSpec((8, d), lambda i: (0, 0)), ANY],
        out_specs=[wide,
                   pl.BlockSpec((nk, kd, d), lambda i: (0, 0, 0)),
                   pl.BlockSpec((8, d), lambda i: (0, 0))],
        out_shape=[jax.ShapeDtypeStruct((s, nk * kd), F32),
                   jax.ShapeDtypeStruct((nk, kd, d), BF16),
                   jax.ShapeDtypeStruct((8, d), F32)],
        scratch_shapes=[pltpu.VMEM((nk, kd, d), F32)],
        compiler_params=_params(1),
    )(dxo, y, ycat, w_all, vec, after)


def _halo_index(ts, nt):
    return lambda j: jnp.maximum((nt - 1 - j) * (ts // HALO) - 1, 0)


def _rnn_bwd(proj, hs, dycat, cw, vec, wa, wx, layer, after):
    _, s, d = proj.shape
    nh, hd, _ = wa.shape
    ts = _tile(s, BWD_SCAN_ROWS)
    nt = s // ts
    halo = _halo_index(ts, nt)
    hps = BWD_HEADS_PER_STEP
    wl = hps * hd

    def body(proj_ref, xh_ref, hs_ref, hsh_ref, dy_ref, cw_ref, vec_ref, wa_ref, wx_ref, after_ref,
             dproj_ref, dgates_ref, dvec_ref, xbuf, hbuf, dubuf, carry, dw_acc):
        del after_ref
        j = pl.program_id(1)
        first_tile = j == nt - 1

        @pl.when(j == 0)
        def _():
            dubuf[pl.ds(ts, HALO), :] = jnp.zeros((HALO, wl), F32)
            carry[...] = jnp.zeros_like(carry)
            dw_acc[...] = jnp.zeros_like(dw_acc)
            dvec_ref[...] = jnp.zeros_like(dvec_ref)

        xbuf[0:HALO, :] = jnp.where(first_tile, 0.0, xh_ref[0])
        xbuf[pl.ds(HALO, ts), :] = proj_ref[0]
        hbuf[0:HALO, :] = jnp.where(first_tile, 0.0, hsh_ref[...])
        hbuf[pl.ds(HALO, ts), :] = hs_ref[...]

        for hh in range(hps):
            lanes = slice(hh * hd, (hh + 1) * hd)
            wa, wx = wa_ref[hh], wx_ref[hh]
            hs = hs_ref[:, lanes]
            u = _conv(xbuf, cw_ref, vec_ref, lanes, ts)
            ub, r, ig, sp, log_a = _rnn_gates(u, wa, wx, vec_ref, lanes)
            a = jnp.exp(log_a)
            e2 = jnp.exp(2.0 * log_a)
            one_minus_a2 = _neg_expm1(2.0 * log_a)
            inv_mult = lax.rsqrt(one_minus_a2)
            mult = one_minus_a2 * inv_mult

            g = proj_ref[1, :, lanes]
            sg = _sigmoid(g)
            dyc = dy_ref[:, lanes]
            dproj_ref[1, :, lanes] = (dyc * hs * (sg * (1.0 + g * (1.0 - sg)))).astype(BF16)

            row = lax.broadcasted_iota(jnp.int32, (ts, hd), 0)
            dhs = dyc * (g * sg) + jnp.where(row == ts - 1, carry[0:1, lanes], 0.0)
            dh = _scan_rev(_shift_up(a, 1, 0.0), dhs)
            carry[:, lanes] = (a * dh)[0:8, :]

            h_prev = hbuf[pl.ds(HALO - 1, ts), lanes]
            dlog_a = dh * h_prev * a - dh * (ig * u) * (e2 * inv_mult)
            di = dh * mult * u
            dzr = dlog_a * ((-LRU_C) * sp) * (r * (1.0 - r))
            dzi = di * (ig * (1.0 - ig))
            dvec_ref[3:4, lanes] += _colsum(dlog_a * r) * (LRU_C * _sigmoid(-vec_ref[3:4, lanes]))
            dvec_ref[1:2, lanes] += _colsum(dzr)
            dvec_ref[2:3, lanes] += _colsum(dzi)
            dzr_b, dzi_b = dzr.astype(BF16), dzi.astype(BF16)
            dw_acc[0, hh] += lax.dot_general(ub, dzr_b, TN_DIMS, preferred_element_type=F32)
            dw_acc[1, hh] += lax.dot_general(ub, dzi_b, TN_DIMS, preferred_element_type=F32)
            du = (dh * mult * ig
                  + lax.dot_general(dzr_b, wa, NT_DIMS, preferred_element_type=F32)
                  + lax.dot_general(dzi_b, wx, NT_DIMS, preferred_element_type=F32))
            dvec_ref[0:1, lanes] += _colsum(du)
            for k in range(CONV_WIDTH):
                dvec_ref[4 + k:5 + k, lanes] += _colsum(du * xbuf[pl.ds(HALO - (CONV_WIDTH - 1) + k, ts), lanes])

            dubuf[0:ts, lanes] = du
            dx = cw_ref[CONV_WIDTH - 1:CONV_WIDTH, lanes] * du
            for k in range(CONV_WIDTH - 1):
                dx = dx + cw_ref[k:k + 1, lanes] * dubuf[pl.ds(CONV_WIDTH - 1 - k, ts), lanes]
            dproj_ref[0, :, lanes] = dx.astype(BF16)
        dubuf[pl.ds(ts, HALO), :] = dubuf[0:HALO, :]

        @pl.when(first_tile)
        def _():
            dgates_ref[...] = dw_acc[...].astype(BF16)

    rev = lambda h, j: (nt - 1 - j, h)
    return pl.pallas_call(
        body, name=f"rnn_bwd_l{layer}", grid=(nh // hps, nt),
        in_specs=[pl.BlockSpec((2, ts, wl), lambda h, j: (0, nt - 1 - j, h)),
                  pl.BlockSpec((1, HALO, wl), lambda h, j: (0, halo(j), h)),
                  pl.BlockSpec((ts, wl), rev),
                  pl.BlockSpec((HALO, wl), lambda h, j: (halo(j), h)),
                  pl.BlockSpec((ts, wl), rev),
                  pl.BlockSpec((CONV_WIDTH, wl), lambda h, j: (0, h)),
                  pl.BlockSpec((8, wl), lambda h, j: (0, h)),
                  pl.BlockSpec((hps, hd, hd), lambda h, j: (h, 0, 0)),
                  pl.BlockSpec((hps, hd, hd), lambda h, j: (h, 0, 0)), ANY],
        out_specs=[pl.BlockSpec((2, ts, wl), lambda h, j: (0, nt - 1 - j, h)),
                   pl.BlockSpec((2, hps, hd, hd), lambda h, j: (0, h, 0, 0)),
                   pl.BlockSpec((16, wl), lambda h, j: (0, h))],
        out_shape=[jax.ShapeDtypeStruct((4, s, d), BF16),
                   jax.ShapeDtypeStruct((2, nh, hd, hd), BF16),
                   jax.ShapeDtypeStruct((16, d), F32)],
        scratch_shapes=[pltpu.VMEM((ts + HALO, wl), F32), pltpu.VMEM((ts + HALO, wl), F32),
                        pltpu.VMEM((ts + HALO, wl), F32), pltpu.VMEM((8, wl), F32),
                        pltpu.VMEM((2, hps, hd, hd), F32)],
        compiler_params=_params(2),
    )(proj, proj, hs, hs, dycat, cw, vec, wa, wx, after)


def _pool_bwd(proj, dycat, dproj, pw, vec, layer):
    _, s, d = proj.shape
    ng, gd, _ = pw.shape
    ts = _tile(s, MATMUL_ROWS)
    nt = s // ts
    halo = _halo_index(ts, nt)

    def body(proj_ref, xh_ref, dy_ref, dproj_in, pw_ref, vec_ref, dproj_ref, dpw_ref, dvec_ref, xbuf, qbuf, acc):
        del dproj_in
        j = pl.program_id(0)
        i = nt - 1 - j

        @pl.when(j == 0)
        def _():
            qbuf[pl.ds(ts, HALO), :] = jnp.zeros((HALO, d), F32)
            acc[...] = jnp.zeros_like(acc)
            dvec_ref[...] = jnp.zeros_like(dvec_ref)

        xbuf[0:HALO, :] = jnp.where(i == 0, 0.0, xh_ref[0])
        xbuf[pl.ds(HALO, ts), :] = proj_ref[0]
        for g in range(ng):
            lanes = slice(g * gd, (g + 1) * gd)
            win = 2 << g
            xt = proj_ref[0, :, lanes]
            inv_cnt = _inv_count(i, ts, gd, win)
            pooled = _pooled(xbuf, xt, lanes, win, inv_cnt, ts).astype(BF16)
            z = jnp.dot(pooled, pw_ref[g], preferred_element_type=F32) + vec_ref[0:1, lanes]
            scale = vec_ref[1:2, lanes]
            gg = proj_ref[1, :, lanes]
            sg = _sigmoid(gg)
            dyc = dy_ref[:, lanes]
            dyp = dyc * (gg * sg)
            dproj_ref[1, :, lanes] = (dyc * (z * scale) * (sg * (1.0 + gg * (1.0 - sg)))).astype(BF16)
            dvec_ref[1:2, lanes] += _colsum(dyp * z)
            dz = dyp * scale
            dvec_ref[0:1, lanes] += _colsum(dz)
            dz_b = dz.astype(BF16)
            acc[g] += lax.dot_general(pooled, dz_b, TN_DIMS, preferred_element_type=F32)
            dpooled = lax.dot_general(dz_b, pw_ref[g], NT_DIMS, preferred_element_type=F32)

            qbuf[0:ts, lanes] = dpooled * inv_cnt
            dx = _window_sum(qbuf[:, lanes], win, False)[0:ts, :] - dpooled
            dproj_ref[0, :, lanes] = dx.astype(BF16)
        qbuf[pl.ds(ts, HALO), :] = qbuf[0:HALO, :]

        @pl.when(j == nt - 1)
        def _():
            dpw_ref[...] = acc[...].astype(BF16)

    return pl.pallas_call(
        body, name=f"pool_bwd_l{layer}", grid=(nt,),
        in_specs=[pl.BlockSpec((2, ts, d), lambda j: (1, nt - 1 - j, 0)),
                  pl.BlockSpec((1, HALO, d), lambda j: (2, halo(j), 0)),
                  pl.BlockSpec((ts, d), lambda j: (nt - 1 - j, 1)),
                  ANY,
                  pl.BlockSpec((ng, gd, gd), lambda j: (0, 0, 0)),
                  pl.BlockSpec((8, d), lambda j: (0, 0))],
        out_specs=[pl.BlockSpec((2, ts, d), lambda j: (1, nt - 1 - j, 0)),
                   pl.BlockSpec((ng, gd, gd), lambda j: (0, 0, 0)),
                   pl.BlockSpec((8, d), lambda j: (0, 0))],
        out_shape=[jax.ShapeDtypeStruct((4, s, d), BF16),
                   jax.ShapeDtypeStruct((ng, gd, gd), BF16),
                   jax.ShapeDtypeStruct((8, d), F32)],
        input_output_aliases={3: 0},
        scratch_shapes=[pltpu.VMEM((ts + HALO, d), F32), pltpu.VMEM((ts + HALO, d), F32),
                        pltpu.VMEM((ng, gd, gd), F32)],
        compiler_params=_params(1),
    )(proj, proj, dycat, dproj, pw, vec)


def _inproj_bwd_x(dproj, w_all, x, dxo, vec, layer, after):
    s, d = x.shape
    p = w_all.shape[2]
    ts = _tile(s, MATMUL_ROWS)

    def body(dp_ref, w_ref, x_ref, dxo_ref, vec_ref, after_ref, dx_ref, dvec_ref):
        del after_ref

        @pl.when(pl.program_id(0) == 0)
        def _():
            dvec_ref[...] = jnp.zeros_like(dvec_ref)

        dh = lax.dot_general(dp_ref[0], w_ref[0], NT_DIMS, preferred_element_type=F32)
        for k in range(1, 4):
            dh = dh + lax.dot_general(dp_ref[k], w_ref[k], NT_DIMS, preferred_element_type=F32)
        _, xn, rs = _prenorm(x_ref[...], vec_ref)
        gpre, scale1 = vec_ref[3:4, :], 1.0 + vec_ref[1:2, :]
        dvec_ref[0:1, :] += _colsum(dh)
        dvec_ref[1:2, :] += _colsum(dh * (xn * gpre))
        dvec_ref[2:3, :] += _colsum(dh * (xn * scale1))
        t = dh * (gpre * scale1)
        dx_ref[...] = dxo_ref[...] + rs * (t - xn * jnp.mean(t * xn, axis=-1, keepdims=True))

    row = pl.BlockSpec((ts, d), lambda i: (i, 0))
    return pl.pallas_call(
        body, name=f"inproj_bwd_x_l{layer}", grid=(s // ts,),
        in_specs=[pl.BlockSpec((4, ts, p), lambda i: (0, i, 0)),
                  pl.BlockSpec((4, d, p), lambda i: (0, 0, 0)),
                  row, row, pl.BlockSpec((8, d), lambda i: (0, 0)), ANY],
        out_specs=[row, pl.BlockSpec((8, d), lambda i: (0, 0))],
        out_shape=[jax.ShapeDtypeStruct((s, d), F32), jax.ShapeDtypeStruct((8, d), F32)],
        compiler_params=_params(1),
    )(dproj, w_all, x, dxo, vec, after)


def _inproj_bwd_w(dproj, x, vec, layer, after):
    s, d = x.shape
    p = dproj.shape[2]
    ts = _tile(s, MATMUL_ROWS)
    nt = s // ts

    def body(dp_ref, x_ref, vec_ref, after_ref, dw_ref, acc):
        del after_ref
        i = pl.program_id(0)

        @pl.when(i == 0)
        def _():
            acc[...] = jnp.zeros_like(acc)

        h, _, _ = _prenorm(x_ref[...], vec_ref)
        hb = h.astype(BF16)
        for k in range(4):
            acc[k] += lax.dot_general(hb, dp_ref[k], TN_DIMS, preferred_element_type=F32)

        @pl.when(i == nt - 1)
        def _():
            dw_ref[...] = acc[...].astype(BF16)

    return pl.pallas_call(
        body, name=f"inproj_bwd_w_l{layer}", grid=(nt,),
        in_specs=[pl.BlockSpec((4, ts, p), lambda i: (0, i, 0)),
                  pl.BlockSpec((ts, d), lambda i: (i, 0)),
                  pl.BlockSpec((8, d), lambda i: (0, 0)), ANY],
        out_specs=pl.BlockSpec((4, d, p), lambda i: (0, 0, 0)),
        out_shape=jax.ShapeDtypeStruct((4, d, p), BF16),
        scratch_shapes=[pltpu.VMEM((4, d, p), F32)],
        compiler_params=_params(1),
    )(dproj, x, vec, after)


def _sum_slots(stacked, name, out_dtype=F32):
    n, rows, cols = stacked.shape
    tr = _row_tile(rows)

    def body(in_ref, out_ref):
        total = in_ref[0].astype(F32)
        for b in range(1, n):
            total = total + in_ref[b].astype(F32)
        out_ref[...] = total.astype(out_dtype)

    return pl.pallas_call(
        body, name=name, grid=(rows // tr,),
        in_specs=[pl.BlockSpec((n, tr, cols), lambda i: (0, i, 0))],
        out_specs=pl.BlockSpec((tr, cols), lambda i: (i, 0)),
        out_shape=jax.ShapeDtypeStruct((rows, cols), out_dtype),
        compiler_params=_params(1),
    )(stacked)


def _adam_update(w, m, v, g):
    m_new = ADAM_B1 * m + (1.0 - ADAM_B1) * g
    v_new = ADAM_B2 * v + (1.0 - ADAM_B2) * (g * g)
    m_hat = m_new / (1.0 - ADAM_B1 ** ADAM_STEP)
    v_hat = v_new / (1.0 - ADAM_B2 ** ADAM_STEP)
    return (-ADAM_LR) * (m_hat / (jnp.sqrt(v_hat) + ADAM_EPS) + ADAM_WD * w), m_new, v_new


def _adamw_layer(w, m, v, grads, layer, prev, name, grad_row_offset=0):
    nl = w.shape[0]
    cols = w.shape[-1]
    rows = w.size // (nl * cols)
    tr = _row_tile(rows)
    off = layer * (rows // tr)
    g_off = grad_row_offset // tr
    n = len(grads)
    n_prev = 0 if prev is None else 4

    def body(*refs):
        w_ref, m_ref, v_ref = refs[:3]
        g_refs = refs[3:3 + n]
        g_out, d_out, m_out, v_out = refs[3 + n + n_prev:]
        g = g_refs[0][...].astype(F32)
        for r in g_refs[1:]:
            g = g + r[...].astype(F32)
        g_out[...] = g
        d_out[...], m_out[...], v_out[...] = _adam_update(w_ref[...], m_ref[...], v_ref[...], g)

    mine = pl.BlockSpec((tr, cols), lambda i: (off + i, 0))
    args = [a.reshape(nl * rows, cols) for a in (w, m, v)] + [g.reshape(-1, cols) for g in grads]
    outs = pl.pallas_call(
        body, name=name, grid=(rows // tr,),
        in_specs=[mine] * 3 + [pl.BlockSpec((tr, cols), lambda i: (g_off + i, 0))] * n + [ANY] * n_prev,
        out_specs=[mine] * 4,
        out_shape=[jax.ShapeDtypeStruct((nl * rows, cols), F32)] * 4,
        input_output_aliases={3 + n + k: k for k in range(n_prev)},
        compiler_params=_params(1),
    )(*args, *(prev or ()))
    return tuple(outs)


def _into_slot(a, dtype, chip_arr, name, layer=None, after=None):
    rows, cols = a.shape[-2:]
    tr = _row_tile(rows)

    def body(chip_ref, a_ref, *rest):
        del chip_ref
        rest[-1][...] = a_ref[...].astype(dtype)

    if layer is None:
        in_spec = pl.BlockSpec((tr, cols), lambda i, chip: (i, 0))
    else:
        in_spec = pl.BlockSpec((None, tr, cols), lambda i, chip: (layer, i, 0))
    extra = [] if after is None else [after]
    return pl.pallas_call(
        body, name=name,
        grid_spec=pltpu.PrefetchScalarGridSpec(
            num_scalar_prefetch=1, grid=(rows // tr,),
            in_specs=[in_spec] + [ANY] * len(extra),
            out_specs=pl.BlockSpec((None, tr, cols), lambda i, chip: (chip[0], i, 0))),
        out_shape=jax.ShapeDtypeStruct((4, rows, cols), dtype),
        compiler_params=_params(1),
    )(chip_arr, a, *extra)


def _sum_owner(own, land, chip_arr, own_block, own_index, name):
    blk = land.shape[1:]
    tr = _row_tile(blk[-2])
    steps = blk[-2] // tr
    tile = (*blk[:-2], tr, blk[-1])
    lead = (0,) * (len(blk) - 2)

    def body(chip_ref, own_ref, l1, l2, l3, out_ref):
        del chip_ref
        total = (own_ref[...].astype(F32) + l1[...].astype(F32)) + (l2[...].astype(F32) + l3[...].astype(F32))
        out_ref[...] = total.astype(BF16)

    def landed(k):
        return pl.BlockSpec((None, *tile), lambda i, chip: (chip[0] ^ k, *lead, i, 0))

    return pl.pallas_call(
        body, name=name,
        grid_spec=pltpu.PrefetchScalarGridSpec(
            num_scalar_prefetch=1, grid=(steps,),
            in_specs=[pl.BlockSpec(own_block(tr), own_index), landed(1), landed(2), landed(3)],
            out_specs=pl.BlockSpec(tile, lambda i, chip: (*lead, i, 0))),
        out_shape=jax.ShapeDtypeStruct(blk, BF16),
        compiler_params=_params(1),
    )(chip_arr, own, land, land, land)


_WHOLE_VMEM = pltpu.CompilerParams(vmem_limit_bytes=V7X_VMEM_LIMIT_BYTES)


def _pack_vectors(modbuf, ada_b, pre_norm_g, post_norm_g, conv_b, gate_a_b, gate_x_b, lru_lambda):
    nl, d = pre_norm_g.shape
    n = modbuf.shape[2] // nl
    nh, hd = gate_a_b.shape[1], gate_a_b.shape[2]

    def body(mb_ref, ab_ref, pre_ref, post_ref, cb_ref, gab_ref, gxb_ref, lam_ref, *outs):
        for layer in range(nl):
            vec_ref, rvec_ref = outs[layer], outs[nl + layer]
            vec_ref[...] = jnp.zeros_like(vec_ref)
            rvec_ref[...] = jnp.zeros_like(rvec_ref)
            for k in range(4):
                piece = mb_ref[k, 0:1, layer * n:(layer + 1) * n] + ab_ref[layer:layer + 1, k * n:(k + 1) * n]
                lo = k * n
                while lo < (k + 1) * n:
                    row = lo // d
                    hi = min((row + 1) * d, (k + 1) * n)
                    vec_ref[row:row + 1, lo - row * d:hi - row * d] = piece[:, lo - k * n:hi - k * n]
                    lo = hi
            vec_ref[3:4, :] = pre_ref[layer:layer + 1, :]
            vec_ref[4:5, :] = post_ref[layer:layer + 1, :]
            rvec_ref[0:1, :] = cb_ref[layer:layer + 1, :]
            for h in range(nh):
                rvec_ref[1:2, h * hd:(h + 1) * hd] = gab_ref[layer, h:h + 1, :]
                rvec_ref[2:3, h * hd:(h + 1) * hd] = gxb_ref[layer, h:h + 1, :]
            rvec_ref[3:4, :] = lam_ref[layer:layer + 1, :]

    out = pl.pallas_call(
        body, name="pack_vectors", in_specs=[VMEM] * 8, out_specs=[VMEM] * (2 * nl),
        out_shape=[jax.ShapeDtypeStruct((8, d), F32)] * (2 * nl), compiler_params=_WHOLE_VMEM,
    )(modbuf, ada_b, pre_norm_g, post_norm_g, conv_b, gate_a_b, gate_x_b, lru_lambda)
    return list(out[:nl]), list(out[nl:])


def _pack_gathered(convw_g, poolb_g, pws, pool_scale, ng):
    nl, d = pool_scale.shape
    taps = convw_g.shape[1] // nl
    dq = convw_g.shape[2]
    gq, gd = poolb_g.shape[2], pws[0].shape[2]

    def body(cg_ref, pb_ref, *rest):
        pw_refs, ps_ref = rest[:nl], rest[nl]
        outs = rest[nl + 1:]
        for layer in range(nl):
            cw_ref, pvec_ref, pwf_ref = outs[layer], outs[nl + layer], outs[2 * nl + layer]
            pvec_ref[...] = jnp.zeros_like(pvec_ref)
            pvec_ref[1:2, :] = ps_ref[layer:layer + 1, :]
            for k in range(4):
                cw_ref[:, k * dq:(k + 1) * dq] = cg_ref[k, layer * taps:(layer + 1) * taps, :]
                for g in range(ng):
                    lo = g * gd + k * gq
                    pvec_ref[0:1, lo:lo + gq] = pb_ref[k, layer * ng + g:layer * ng + g + 1, :]
                    pwf_ref[g, k * gq:(k + 1) * gq, :] = pw_refs[layer][k, g * gq:(g + 1) * gq, :]

    out = pl.pallas_call(
        body, name="pack_gathered", in_specs=[VMEM] * (3 + nl), out_specs=[VMEM] * (3 * nl),
        out_shape=[jax.ShapeDtypeStruct((taps, d), F32)] * nl + [jax.ShapeDtypeStruct((8, d), F32)] * nl
        + [jax.ShapeDtypeStruct((ng, gd, gd), BF16)] * nl,
        compiler_params=_WHOLE_VMEM,
    )(convw_g, poolb_g, *pws, pool_scale)
    return list(out[:nl]), list(out[nl:2 * nl]), list(out[2 * nl:])


ROW_SHIFT, ROW_SCALE, ROW_PRE, ROW_GATE, ROW_POST = 0, 1, 2, 8, 9
ROW_CONV_B, ROW_GATE_A_B, ROW_GATE_X_B, ROW_LAMBDA, ROW_CONV_W = 16, 17, 18, 19, 20
ROW_POOL_B, ROW_POOL_SCALE, ROW_SQ = 32, 33, 40


def _adamw_small(totals, chip_arr, params):
    nl = len(totals)
    d = totals[0].shape[1]
    n_par = len(params)
    flat = [a for p in params for a in p]
    nh, hd = params[6][0].shape[1], params[6][0].shape[2]
    taps, dq = params[8][0].shape[1], params[8][0].shape[2]
    ng, gq = params[9][0].shape[1], params[9][0].shape[2]
    gd = d // ng

    def body(chip_ref, *refs):
        tot = refs[:nl]
        ins = refs[nl:nl + 3 * n_par]
        outs = refs[nl + 3 * n_par:]
        chip = chip_ref[0]

        def update(p, idx, g):
            delta, m_new, v_new = _adam_update(ins[3 * p][idx], ins[3 * p + 1][idx], ins[3 * p + 2][idx], g)
            outs[4 * p][idx] = g
            outs[4 * p + 1][idx] = delta
            outs[4 * p + 2][idx] = m_new
            outs[4 * p + 3][idx] = v_new

        def mine(candidates):
            g = candidates[0]
            for k in range(1, 4):
                g = jnp.where(chip == k, candidates[k], g)
            return g

        for layer in range(nl):
            t = tot[layer]
            row = (slice(layer, layer + 1), slice(None))
            for j, r in enumerate((ROW_SHIFT, ROW_SCALE, ROW_GATE)):
                update(0, (slice(layer, layer + 1), slice(j * d, (j + 1) * d)), t[r:r + 1, :])
            for p, r in ((1, ROW_PRE), (2, ROW_POST), (3, ROW_CONV_B), (4, ROW_LAMBDA), (5, ROW_POOL_SCALE)):
                update(p, row, t[r:r + 1, :])
            for h in range(nh):
                idx = (layer, slice(h, h + 1), slice(None))
                update(6, idx, t[ROW_GATE_A_B:ROW_GATE_A_B + 1, h * hd:(h + 1) * hd])
                update(7, idx, t[ROW_GATE_X_B:ROW_GATE_X_B + 1, h * hd:(h + 1) * hd])
            for k in range(taps):
                r = ROW_CONV_W + k
                update(8, (layer, slice(k, k + 1), slice(None)), mine([t[r:r + 1, c * dq:(c + 1) * dq] for c in range(4)]))
            for g in range(ng):
                cands = [t[ROW_POOL_B:ROW_POOL_B + 1, g * gd + c * gq:g * gd + (c + 1) * gq] for c in range(4)]
                update(9, (layer, slice(g, g + 1), slice(None)), mine(cands))

    out = pl.pallas_call(
        body, name="adamw_small",
        in_specs=[pl.BlockSpec(memory_space=pltpu.SMEM)] + [VMEM] * (nl + 3 * n_par),
        out_specs=[VMEM] * (4 * n_par),
        out_shape=[jax.ShapeDtypeStruct(p[0].shape, F32) for p in params for _ in range(4)],
        compiler_params=_WHOLE_VMEM,
    )(chip_arr, *totals, *flat)
    return [tuple(out[4 * p:4 * p + 4]) for p in range(n_par)]


def _adamw_ada_w_layer(c_t, slabs, chip_arr, w, m, v, layer, prev, name, after):
    nl, d, n = w.shape
    nb = c_t.shape[1]
    tr = _row_tile(d)
    off = layer * (d // tr)
    n_prev = 0 if prev is None else 4
    mod_rows = (ROW_SHIFT, ROW_SCALE, ROW_GATE)

    def body(chip_ref, c_ref, slab_ref, w_ref, m_ref, v_ref, *rest):
        g_out, d_out, m_out, v_out = rest[n_prev + 1:n_prev + 5]
        dm = rest[-1]

        @pl.when(pl.program_id(0) == 0)
        def _():
            for k in range(4):
                @pl.when(chip_ref[0] == k)
                def _():
                    lo = k * n
                    while lo < (k + 1) * n:
                        hi = min((lo // d + 1) * d, (k + 1) * n)
                        row = mod_rows[lo // d]
                        for b in range(nb):
                            dm[b:b + 1, lo - k * n:hi - k * n] = slab_ref[b, row:row + 1, lo % d:lo % d + hi - lo]
                        lo = hi

        g = c_ref[:, 0:1] * dm[0:1, :]
        for b in range(1, nb):
            g = g + c_ref[:, b:b + 1] * dm[b:b + 1, :]
        g_out[...] = g
        d_out[...], m_out[...], v_out[...] = _adam_update(w_ref[...], m_ref[...], v_ref[...], g)

    mine = pl.BlockSpec((tr, n), lambda i, chip: (off + i, 0))
    outs = pl.pallas_call(
        body, name=name,
        grid_spec=pltpu.PrefetchScalarGridSpec(
            num_scalar_prefetch=1, grid=(d // tr,),
            in_specs=[pl.BlockSpec((tr, nb), lambda i, chip: (i, 0)),
                      pl.BlockSpec(slabs.shape, lambda i, chip: (0, 0, 0))] + [mine] * 3 + [ANY] * (n_prev + 1),
            out_specs=[mine] * 4,
            scratch_shapes=[pltpu.VMEM((nb, n), F32)]),
        out_shape=[jax.ShapeDtypeStruct((nl * d, n), F32)] * 4,
        input_output_aliases={6 + k: k for k in range(n_prev)},
        compiler_params=_params(1),
    )(chip_arr, c_t, slabs, *[a.reshape(nl * d, n) for a in (w, m, v)], *(prev or ()), after)
    return tuple(outs)


def _place():
    x, y, c = lax.axis_index("x"), lax.axis_index("y"), lax.axis_index("c")
    return x, y, c


OTHER_CHIPS = ((1, 0), (0, 1), (1, 1))
OTHER_DEVICES = tuple((fx, fy, fc) for fx in (0, 1) for fy in (0, 1) for fc in (0, 1))[1:]


def _mod_exchange(c_row, ada_w, after):
    nl, d, n = ada_w.shape

    def body(c_ref, w_ref, after_ref, cbuf, modbuf, token, cblk, mres, send_a, recv_a, send_c, recv_c):
        del after_ref
        token[...] = jnp.zeros_like(token)
        x, y, c = _place()
        me = 4 * x + 2 * y + c
        chip = 2 * x + y
        cv = c_ref[...]
        cblk[...] = jnp.zeros_like(cblk)
        cblk[0:1, :] = cv * _sigmoid(cv)

        def rows_of(dev):
            return cbuf.at[pl.ds(pl.multiple_of(8 * dev, 8), 8), :]

        cbuf[pl.ds(pl.multiple_of(8 * me, 8), 8), :] = cblk[...]
        sends = []
        for j, (fx, fy, fc) in enumerate(OTHER_DEVICES):
            cp = pltpu.make_async_remote_copy(
                src_ref=cblk, dst_ref=rows_of(me), send_sem=send_a.at[j], recv_sem=recv_a.at[j],
                device_id=(x ^ fx, y ^ fy, c ^ fc), device_id_type=MESH)
            cp.start()
            sends.append(cp)
        for j, (fx, fy, fc) in enumerate(OTHER_DEVICES):
            peer = 4 * (x ^ fx) + 2 * (y ^ fy) + (c ^ fc)
            pltpu.make_async_remote_copy(
                src_ref=cblk, dst_ref=rows_of(peer), send_sem=send_a.at[j], recv_sem=recv_a.at[j],
                device_id=(x ^ fx, y ^ fy, c ^ fc), device_id_type=MESH).wait_recv()
        for cp in sends:
            cp.wait_send()

        call = cbuf[...]
        for layer in range(nl):
            mres[:, layer * n:(layer + 1) * n] = jnp.dot(
                call, w_ref[layer], preferred_element_type=F32, precision=lax.Precision.HIGHEST)

        def block_of(dev):
            return mres.at[pl.ds(pl.multiple_of(8 * dev, 8), 8), :]

        modbuf[chip] = mres[pl.ds(pl.multiple_of(8 * me, 8), 8), :]
        sends = []
        for j, (fx, fy) in enumerate(OTHER_CHIPS):
            peer = 4 * (x ^ fx) + 2 * (y ^ fy) + c
            cp = pltpu.make_async_remote_copy(
                src_ref=block_of(peer), dst_ref=modbuf.at[chip], send_sem=send_c.at[j], recv_sem=recv_c.at[j],
                device_id=(x ^ fx, y ^ fy, c), device_id_type=MESH)
            cp.start()
            sends.append(cp)
        for j, (fx, fy) in enumerate(OTHER_CHIPS):
            pltpu.make_async_remote_copy(
                src_ref=block_of(me), dst_ref=modbuf.at[2 * (x ^ fx) + (y ^ fy)],
                send_sem=send_c.at[j], recv_sem=recv_c.at[j],
                device_id=(x ^ fx, y ^ fy, c), device_id_type=MESH).wait_recv()
        for cp in sends:
            cp.wait_send()

    return pl.pallas_call(
        body, name="mod_exchange", in_specs=[VMEM, VMEM, ANY], out_specs=[VMEM, VMEM, VMEM],
        out_shape=[jax.ShapeDtypeStruct((64, d), F32), jax.ShapeDtypeStruct((4, 8, nl * n), F32),
                   jax.ShapeDtypeStruct((8, 128), F32)],
        scratch_shapes=[pltpu.VMEM((8, d), F32), pltpu.VMEM((64, nl * n), F32),
                        pltpu.SemaphoreType.DMA((7,)), pltpu.SemaphoreType.DMA((7,)),
                        pltpu.SemaphoreType.DMA((3,)), pltpu.SemaphoreType.DMA((3,))],
        compiler_params=pltpu.CompilerParams(vmem_limit_bytes=V7X_VMEM_LIMIT_BYTES, has_side_effects=True),
    )(c_row, ada_w, after)


def _in_hbm(a):
    return pltpu.with_memory_space_constraint(a, pltpu.HBM)


def _gather_copies(lands, split, over_ici):
    x, y, c = _place()
    chip = 2 * x + y
    out = []
    for t, land in enumerate(lands):
        half = land.shape[1] // 2
        mine = pl.ds(pl.multiple_of(c * half, half), half)
        theirs = pl.ds(pl.multiple_of((1 - c) * half, half), half)
        for j, (fx, fy) in enumerate(OTHER_CHIPS):
            them = 2 * (x ^ fx) + (y ^ fy)
            if over_ici and split[t]:
                out.append((land.at[chip, mine], land.at[chip, mine], land.at[them, mine], (x ^ fx, y ^ fy, c), 3 * t + j))
            elif over_ici:
                out.append((land.at[chip], land.at[chip], land.at[them], (x ^ fx, y ^ fy, c), 3 * t + j))
            elif split[t]:
                out.append((land.at[them, mine], land.at[them, mine], land.at[them, theirs], (x, y, 1 - c), 3 * t + j))
    return out


def _gather_start(lands, groups, split, name):
    n, ngr = len(lands), len(groups)

    def body(*refs):
        sems = refs[n:n + 2 * ngr]
        for gi, idxs in enumerate(groups):
            for src, dst, _, peer, k in _gather_copies([refs[i] for i in idxs], [split[i] for i in idxs], True):
                pltpu.make_async_remote_copy(src_ref=src, dst_ref=dst, send_sem=sems[2 * gi].at[k],
                                             recv_sem=sems[2 * gi + 1].at[k], device_id=peer, device_id_type=MESH).start()
        refs[-1][...] = jnp.zeros_like(refs[-1])

    sem_shapes = []
    for idxs in groups:
        sem_shapes += [pltpu.SemaphoreType.DMA((3 * len(idxs),))] * 2
    out = pl.pallas_call(
        body, name=name,
        in_specs=[HBM] * n, out_specs=[SEM] * (2 * ngr) + [HBM] * n + [VMEM],
        out_shape=sem_shapes + [pltpu.HBM(a.shape, a.dtype) for a in lands] + [jax.ShapeDtypeStruct((8, 128), F32)],
        input_output_aliases={i: 2 * ngr + i for i in range(n)},
        compiler_params=pltpu.CompilerParams(has_side_effects=DATAFLOW_EFFECT),
    )(*[_in_hbm(a) for a in lands])
    sems = [(out[2 * gi], out[2 * gi + 1]) for gi in range(ngr)]
    return sems, list(out[2 * ngr:2 * ngr + n]), out[-1]


def _gather_forward(lands, split, sems, after, name):
    n = len(lands)

    def body(*refs):
        ici_send, ici_recv = refs[n], refs[n + 1]
        fwd_send, fwd_recv = refs[n + 3], refs[n + 4]
        forwards = {k: (src, dst, peer) for src, dst, _, peer, k in _gather_copies(refs[:n], split, False)}
        for src, _, landed, peer, k in _gather_copies(refs[:n], split, True):
            cp = pltpu.make_async_remote_copy(src_ref=src, dst_ref=landed, send_sem=ici_send.at[k], recv_sem=ici_recv.at[k],
                                              device_id=peer, device_id_type=MESH)
            cp.wait_recv()
            if k in forwards:
                fsrc, fdst, fpeer = forwards[k]
                pltpu.make_async_remote_copy(src_ref=fsrc, dst_ref=fdst, send_sem=fwd_send.at[k], recv_sem=fwd_recv.at[k],
                                             device_id=fpeer, device_id_type=MESH).start()
            cp.wait_send()

    out = pl.pallas_call(
        body, name=name,
        in_specs=[HBM] * n + [SEM, SEM, ANY], out_specs=[SEM, SEM] + [HBM] * n,
        out_shape=[pltpu.SemaphoreType.DMA((3 * n,))] * 2 + [pltpu.HBM(a.shape, a.dtype) for a in lands],
        input_output_aliases={i: 2 + i for i in range(n)},
        compiler_params=pltpu.CompilerParams(has_side_effects=DATAFLOW_EFFECT),
    )(*lands, sems[0], sems[1], after)
    return (out[0], out[1]), list(out[2:])


def _gather_wait(lands, split, sems, name):
    n = len(lands)

    def body(*refs):
        send_sems, recv_sems = refs[n], refs[n + 1]
        for src, _, landed, peer, k in _gather_copies(refs[:n], split, False):
            cp = pltpu.make_async_remote_copy(src_ref=src, dst_ref=landed, send_sem=send_sems.at[k], recv_sem=recv_sems.at[k],
                                              device_id=peer, device_id_type=MESH)
            cp.wait_send()
            cp.wait_recv()

    out = pl.pallas_call(
        body, name=name,
        in_specs=[HBM] * n + [SEM, SEM], out_specs=[HBM] * n,
        out_shape=[pltpu.HBM(a.shape, a.dtype) for a in lands],
        input_output_aliases={i: i for i in range(n)},
        compiler_params=pltpu.CompilerParams(has_side_effects=DATAFLOW_EFFECT),
    )(*lands, sems[0], sems[1])
    return list(out)


def _to_owner_copies(pairs, q):
    x, y, c = _place()
    chip = 2 * x + y
    out = []
    for t, (part, land) in enumerate(pairs):
        for j, (fx, fy) in enumerate(OTHER_CHIPS):
            owner = 2 * (x ^ fx) + (y ^ fy)
            if part.shape[0] == 4 and part.shape[1:] == land.shape[1:]:
                src = part.at[owner]
            else:
                src = part.at[:, pl.ds(pl.multiple_of(owner * q, q), q), :]
            out.append((src, land.at[chip], land.at[owner], (x ^ fx, y ^ fy, c), 3 * t + j))
    return out


def _to_all_copies(bufs, first_sem):
    x, y, c = _place()
    me = 4 * x + 2 * y + c
    out = []
    for t, buf in enumerate(bufs):
        for j, (fx, fy, fc) in enumerate(OTHER_DEVICES):
            them = 4 * (x ^ fx) + 2 * (y ^ fy) + (c ^ fc)
            out.append((buf.at[me], buf.at[me], buf.at[them], (x ^ fx, y ^ fy, c ^ fc), first_sem + 7 * t + j))
    return out


def _to_chips_copies(bufs, first_sem):
    x, y, c = _place()
    chip = 2 * x + y
    out = []
    for t, buf in enumerate(bufs):
        for j, (fx, fy) in enumerate(OTHER_CHIPS):
            them = 2 * (x ^ fx) + (y ^ fy)
            out.append((buf.at[chip], buf.at[chip], buf.at[them], (x ^ fx, y ^ fy, c), first_sem + 3 * t + j))
    return out


def _exchange_copies(refs, kinds, q):
    n_owner, n_chips = kinds
    pairs = list(zip(refs[:n_owner], refs[n_owner:2 * n_owner]))
    first_all = 3 * (n_owner + n_chips)
    return (_to_owner_copies(pairs, q) + _to_chips_copies(refs[2 * n_owner:2 * n_owner + n_chips], 3 * n_owner)
            + _to_all_copies(refs[2 * n_owner + n_chips:], first_all))


def _exchange_start(arrays, kinds, q, name):
    n = len(arrays)
    n_sems = 3 * (kinds[0] + kinds[1]) + 7 * (n - 2 * kinds[0] - kinds[1])

    def body(*refs):
        send_sems, recv_sems = refs[n], refs[n + 1]
        for src, dst, _, peer, k in _exchange_copies(refs[:n], kinds, q):
            pltpu.make_async_remote_copy(src_ref=src, dst_ref=dst, send_sem=send_sems.at[k], recv_sem=recv_sems.at[k],
                                         device_id=peer, device_id_type=MESH).start()
        refs[-1][...] = jnp.zeros_like(refs[-1])

    out = pl.pallas_call(
        body, name=name,
        in_specs=[HBM] * n, out_specs=[SEM, SEM] + [HBM] * n + [VMEM],
        out_shape=[pltpu.SemaphoreType.DMA((n_sems,))] * 2 + [pltpu.HBM(a.shape, a.dtype) for a in arrays]
        + [jax.ShapeDtypeStruct((8, 128), F32)],
        input_output_aliases={i: 2 + i for i in range(n)},
        compiler_params=pltpu.CompilerParams(has_side_effects=DATAFLOW_EFFECT),
    )(*[_in_hbm(a) for a in arrays])
    return (out[0], out[1]), list(out[2:2 + n]), out[-1]


def _exchange_wait(arrays, sems, kinds, q, after, name):
    n = len(arrays)

    def body(*refs):
        send_sems, recv_sems = refs[n], refs[n + 1]
        for src, _, landed, peer, k in _exchange_copies(refs[:n], kinds, q):
            cp = pltpu.make_async_remote_copy(src_ref=src, dst_ref=landed, send_sem=send_sems.at[k], recv_sem=recv_sems.at[k],
                                              device_id=peer, device_id_type=MESH)
            cp.wait_send()
            cp.wait_recv()

    out = pl.pallas_call(
        body, name=name,
        in_specs=[HBM] * n + [SEM, SEM, ANY], out_specs=[HBM] * n,
        out_shape=[pltpu.HBM(a.shape, a.dtype) for a in arrays],
        input_output_aliases={i: i for i in range(n)},
        compiler_params=pltpu.CompilerParams(has_side_effects=DATAFLOW_EFFECT),
    )(*arrays, sems[0], sems[1], after)
    return list(out)


def _sibling_copies(refs):
    n = len(refs) // 2
    x, y, c = _place()
    return [(refs[i], refs[n + i], (x, y, 1 - c), i) for i in range(n)]


def _sibling_start(parts, layer):
    arrays = list(parts) + [lax.empty(a.shape, a.dtype) for a in parts]
    n = len(arrays)

    def body(*refs):
        send_sems, recv_sems = refs[n], refs[n + 1]
        for src, dst, peer, k in _sibling_copies(refs[:n]):
            pltpu.make_async_remote_copy(src_ref=src, dst_ref=dst, send_sem=send_sems.at[k], recv_sem=recv_sems.at[k],
                                         device_id=peer, device_id_type=MESH).start()
        refs[-1][...] = jnp.zeros_like(refs[-1])

    out = pl.pallas_call(
        body, name=f"sibling_swap_start_l{layer}",
        in_specs=[HBM] * n, out_specs=[SEM, SEM] + [HBM] * n + [VMEM],
        out_shape=[pltpu.SemaphoreType.DMA((n // 2,))] * 2 + [pltpu.HBM(a.shape, a.dtype) for a in arrays]
        + [jax.ShapeDtypeStruct((8, 128), F32)],
        input_output_aliases={i: 2 + i for i in range(n)},
        compiler_params=pltpu.CompilerParams(has_side_effects=DATAFLOW_EFFECT),
    )(*[_in_hbm(a) for a in arrays])
    return (out[0], out[1]), list(out[2:2 + n]), out[-1]


def _sibling_wait(arrays, sems, after, layer):
    n = len(arrays)

    def body(*refs):
        send_sems, recv_sems = refs[n], refs[n + 1]
        for src, dst, peer, k in _sibling_copies(refs[:n]):
            cp = pltpu.make_async_remote_copy(src_ref=src, dst_ref=dst, send_sem=send_sems.at[k], recv_sem=recv_sems.at[k],
                                              device_id=peer, device_id_type=MESH)
            cp.wait_send()
            cp.wait_recv()

    out = pl.pallas_call(
        body, name=f"sibling_swap_wait_l{layer}",
        in_specs=[HBM] * n + [SEM, SEM, ANY], out_specs=[HBM] * n,
        out_shape=[pltpu.HBM(a.shape, a.dtype) for a in arrays],
        input_output_aliases={i: i for i in range(n)},
        compiler_params=pltpu.CompilerParams(has_side_effects=DATAFLOW_EFFECT),
    )(*arrays, sems[0], sems[1], after)
    return list(out)


def kernel(x, c, ada_w, ada_b, pre_norm_g, w_in, conv_w, conv_b, gate_a_w, gate_a_b, gate_x_w, gate_x_b, lru_lambda, pool_w, pool_b, pool_scale, w_out, post_norm_g, loss_target, m_ada_w, m_ada_b, m_pre_norm_g, m_w_in, m_conv_w, m_conv_b, m_gate_a_w, m_gate_a_b, m_gate_x_w, m_gate_x_b, m_lru_lambda, m_pool_w, m_pool_b, m_pool_scale, m_w_out, m_post_norm_g, v_ada_w, v_ada_b, v_pre_norm_g, v_w_in, v_conv_w, v_conv_b, v_gate_a_w, v_gate_a_b, v_gate_x_w, v_gate_x_b, v_lru_lambda, v_pool_w, v_pool_b, v_pool_scale, v_w_out, v_post_norm_g):
    nl, d, _ = ada_w.shape
    s = x.shape[1]
    nh, hd = gate_a_w.shape[1], gate_a_w.shape[2]
    ng, gq, gd = pool_w.shape[1], pool_w.shape[2], pool_w.shape[3]
    me = 4 * lax.axis_index("x") + 2 * lax.axis_index("y") + lax.axis_index("c")
    chip = 2 * lax.axis_index("x") + lax.axis_index("y")
    chip_arr = jnp.reshape(chip, (1,)).astype(jnp.int32)
    x0 = x.reshape(s, d)
    target = loss_target.reshape(s, d)
    p_in = w_in.shape[2]

    c_row = c.reshape(1, d)
    cbuf, modbuf, mod_token = _mod_exchange(c_row, ada_w, c_row)
    vecs, rvecs = _pack_vectors(modbuf, ada_b, pre_norm_g, post_norm_g, conv_b, gate_a_b, gate_x_b, lru_lambda)

    win = [_into_slot(w_in, BF16, chip_arr, f"slot_w_in_l{l}", l) for l in range(nl)]
    wout = [_into_slot(w_out, BF16, chip_arr, f"slot_w_out_l{l}", l) for l in range(nl)]
    pw = [_into_slot(pool_w.reshape(nl, ng * gq, gd), BF16, chip_arr, f"slot_pool_w_l{l}", l) for l in range(nl)]
    convw = _into_slot(conv_w.reshape(nl * CONV_WIDTH, d // 4), F32, chip_arr, "slot_conv_w", after=mod_token)
    poolb = _into_slot(pool_b.reshape(nl * ng, gq), F32, chip_arr, "slot_pool_b")
    lands = [win[0], convw, poolb, *pw, wout[0]]
    split = [True, False, False] + [True] * (nl + 1)
    groups = [[0], list(range(1, len(lands)))]
    for l in range(1, nl):
        groups.append([len(lands), len(lands) + 1])
        lands += [win[l], wout[l]]
        split += [True, True]
    sems, lands, _ = _gather_start(lands, groups, split, "weight_gather_start")
    wa_b, wx_b = gate_a_w.astype(BF16), gate_x_w.astype(BF16)

    def gathered(gi, after, tag):
        idxs = groups[gi]
        arrays, halves = [lands[i] for i in idxs], [split[i] for i in idxs]
        between, arrays = _gather_forward(arrays, halves, sems[gi], after, f"weight_gather_forward_{tag}")
        return _gather_wait(arrays, halves, between, f"weight_gather_wait_{tag}")

    xs, projs, hss, ycats, ys = [x0], [], [], [], []
    sq = None
    convw_full = poolw_full = pvecs = None
    for l in range(nl):
        if l == 0:
            (win[0],) = gathered(0, modbuf, "a")
        proj = _inproj_fwd(xs[l], vecs[l], win[l], l)
        if l == 0:
            got = gathered(1, proj, "b")
            wout[0] = got[2 + nl]
            convw_full, pvecs, poolw_full = _pack_gathered(got[0], got[1], got[2:2 + nl], pool_scale, ng)
        ycat, hs = _rnn_fwd(proj, convw_full[l], rvecs[l], wa_b[l], wx_b[l], l)
        if l + 1 < nl:
            win[l + 1], wout[l + 1] = gathered(2 + l, hs, f"c{l + 1}")
        ycat = _pool_fwd(proj, ycat, poolw_full[l], pvecs[l], l)
        y, xo, sq = _outproj_fwd(ycat, wout[l], xs[l], vecs[l], target if l == nl - 1 else None, l)
        projs.append(proj), hss.append(hs), ycats.append(ycat), ys.append(y), xs.append(xo)

    c_all_t = cbuf.reshape(8, 8, d)[:, 0, :].T

    def finish(l, flights, after, prev):
        (sems_a, arr_a), (sems_g, arr_g), (sems_b, arr_b), (sems_c, arr_c) = flights
        dwout_l, rwout = _exchange_wait(arr_a, sems_a, (1, 0), gq, after, f"grad_wait_a_l{l}")
        dpw_l, rpw, gates = _exchange_wait(arr_g, sems_g, (1, 1), gq, rwout, f"grad_wait_g_l{l}")
        dwin_l, rwin = _exchange_wait(arr_b, sems_b, (1, 0), gq, gates, f"grad_wait_b_l{l}")
        (slabs,) = _exchange_wait(arr_c, sems_c, (0, 0), gq, rwin, f"grad_wait_c_l{l}")
        p_win = _sum_owner(dwin_l, rwin, chip_arr, lambda tr: (None, tr, p_in),
                           lambda i, chip: (chip[0], i, 0), f"sum_w_in_l{l}")
        p_wout = _sum_owner(dwout_l, rwout, chip_arr, lambda tr: (None, tr, d),
                            lambda i, chip: (chip[0], i, 0), f"sum_w_out_l{l}")
        p_pw = _sum_owner(dpw_l, rpw, chip_arr, lambda tr: (ng, tr, gd),
                          lambda i, chip: (0, chip[0], 0), f"sum_pool_w_l{l}")
        p_gates = _sum_slots(gates.reshape(4, 2 * nh * hd, hd), f"sum_gates_l{l}", BF16)
        swap_sems, swapping, swap_token = _sibling_start([p_win, p_wout, p_pw, p_gates], l)
        prev = prev or {}
        ada = _adamw_ada_w_layer(c_all_t, slabs, chip_arr, ada_w, m_ada_w, v_ada_w, l, prev.get("ada_w"),
                                 f"adamw_ada_w_l{l}", swap_token)
        p_win, p_wout, p_pw, p_gates, q_win, q_wout, q_pw, q_gates = _sibling_wait(swapping, swap_sems, ada[3], l)
        big = {
            "w_in": _adamw_layer(w_in, m_w_in, v_w_in, [p_win, q_win], l, prev.get("w_in"), f"adamw_w_in_l{l}"),
            "w_out": _adamw_layer(w_out, m_w_out, v_w_out, [p_wout, q_wout], l, prev.get("w_out"), f"adamw_w_out_l{l}"),
            "pool_w": _adamw_layer(pool_w, m_pool_w, v_pool_w, [p_pw, q_pw], l, prev.get("pool_w"), f"adamw_pool_w_l{l}"),
            "gate_a_w": _adamw_layer(gate_a_w, m_gate_a_w, v_gate_a_w, [p_gates, q_gates], l, prev.get("gate_a_w"),
                                     f"adamw_gate_a_w_l{l}"),
            "gate_x_w": _adamw_layer(gate_x_w, m_gate_x_w, v_gate_x_w, [p_gates, q_gates], l, prev.get("gate_x_w"),
                                     f"adamw_gate_x_w_l{l}", grad_row_offset=nh * hd),
            "ada_w": ada,
        }
        return big, _sum_slots(slabs, f"sum_slab_l{l}")

    dx = xs[nl]
    flights = token = big = None
    totals = [None] * nl
    for l in reversed(range(nl)):
        vec_l = vecs[l]
        dycat, dwout_l, dvec_o = _outproj_bwd(dx, ys[l], ycats[l], wout[l], vec_l, l, vec_l if token is None else token)
        sems_a, arr_a, tok_a = _exchange_start([dwout_l, lax.empty(dwout_l.shape, BF16)], (1, 0), gq, f"grad_start_a_l{l}")
        dproj, dgates, dvec_r = _rnn_bwd(projs[l], hss[l], dycat, convw_full[l], rvecs[l], wa_b[l], wx_b[l], l, tok_a)
        dproj, dpw_l, dvec_p = _pool_bwd(projs[l], dycat, dproj, poolw_full[l], pvecs[l], l)
        gates4 = lax.dynamic_update_slice(lax.empty((4, *dgates.shape), BF16), dgates[None], (chip, 0, 0, 0, 0))
        sems_g, arr_g, tok_g = _exchange_start([dpw_l, lax.empty((4, ng, gq, gd), BF16), gates4], (1, 1), gq,
                                               f"grad_start_g_l{l}")
        dwin_l = _inproj_bwd_w(dproj, xs[l], vec_l, l, tok_g)
        sems_b, arr_b, tok_b = _exchange_start([dwin_l, lax.empty(dwin_l.shape, BF16)], (1, 0), gq, f"grad_start_b_l{l}")
        dx, dvec_i = _inproj_bwd_x(dproj, win[l], xs[l], dx, vec_l, l, tok_b)
        parts = [dvec_i, dvec_o, dvec_r, dvec_p]
        if l == nl - 1:
            parts.append(jnp.tile(sq, (1, d // sq.shape[1])))
        slab = jnp.concatenate(parts, axis=0)
        slabs = lax.dynamic_update_slice(lax.empty((8, *slab.shape), F32), slab[None], (me, 0, 0))
        sems_c, arr_c, token = _exchange_start([slabs], (0, 0), gq, f"grad_start_c_l{l}")
        if flights is not None:
            big, totals[l + 1] = finish(l + 1, flights, token, big)
        flights = ((sems_a, arr_a), (sems_g, arr_g), (sems_b, arr_b), (sems_c, arr_c))
    big, totals[0] = finish(0, flights, big["w_in"][3] if big else dx, big)
    grad_x = dx.reshape(x.shape)
    loss = totals[nl - 1][ROW_SQ, 0] * (0.5 / d)

    small = _adamw_small(totals, chip_arr, [
        (ada_b, m_ada_b, v_ada_b), (pre_norm_g, m_pre_norm_g, v_pre_norm_g), (post_norm_g, m_post_norm_g, v_post_norm_g),
        (conv_b, m_conv_b, v_conv_b), (lru_lambda, m_lru_lambda, v_lru_lambda), (pool_scale, m_pool_scale, v_pool_scale),
        (gate_a_b, m_gate_a_b, v_gate_a_b), (gate_x_b, m_gate_x_b, v_gate_x_b),
        (conv_w, m_conv_w, v_conv_w), (pool_b, m_pool_b, v_pool_b)])

    results = {
        "ada_w": tuple(o.reshape(ada_w.shape) for o in big["ada_w"]),
        "ada_b": small[0],
        "pre_norm_g": small[1],
        "w_in": tuple(o.reshape(w_in.shape) for o in big["w_in"]),
        "conv_w": small[8],
        "conv_b": small[3],
        "gate_a_w": tuple(o.reshape(gate_a_w.shape) for o in big["gate_a_w"]),
        "gate_a_b": small[6],
        "gate_x_w": tuple(o.reshape(gate_x_w.shape) for o in big["gate_x_w"]),
        "gate_x_b": small[7],
        "lru_lambda": small[4],
        "pool_w": tuple(o.reshape(pool_w.shape) for o in big["pool_w"]),
        "pool_b": small[9],
        "pool_scale": small[5],
        "w_out": tuple(o.reshape(w_out.shape) for o in big["w_out"]),
        "post_norm_g": small[2],
    }
    names = list(results)
    return (loss, grad_x,
            *[results[n][0] for n in names], *[results[n][1] for n in names],
            *[results[n][2] for n in names], *[results[n][3] for n in names])
```
